```python
import math
import jax, jax.numpy as jnp
from jax import lax
import numpy as np

D_MODEL = 1024
BATCH = 8
SEQ = 16384
DEPTH = 2

N_EVEN = (DEPTH + 1) // 2
N_ODD = DEPTH // 2
EPS = 1e-6

GM_HEADS = 4
GM_WIDTH = D_MODEL
GM_HEAD_DIM = GM_WIDTH // GM_HEADS
GM_CHUNK = 128

SSM_WIDTH = D_MODEL
SSM_HEAD_DIM = 64
SSM_HEADS = SSM_WIDTH // SSM_HEAD_DIM
SSM_GROUPS = 4
SSM_HEADS_PER_GROUP = SSM_HEADS // SSM_GROUPS
SSM_STATE = 128
SSM_CONV = 4
SSM_CHUNK = 128
SSM_CONV_DIM = SSM_WIDTH + 2 * SSM_GROUPS * SSM_STATE
DT_MIN = 0.001
DT_MAX = 0.1
DT_FLOOR = 1e-4

IN_PROJ_DIM = 2 * GM_WIDTH + SSM_WIDTH + SSM_CONV_DIM + SSM_HEADS
SPLIT_POINTS = (GM_WIDTH, 2 * GM_WIDTH, 2 * GM_WIDTH + SSM_WIDTH, 2 * GM_WIDTH + SSM_WIDTH + SSM_CONV_DIM)
MIX_WIDTH = GM_WIDTH + SSM_WIDTH

POOL_WINDOWS = (2, 4, 8, 16)
POOL_GROUPS = len(POOL_WINDOWS)
POOL_GROUP_DIM = D_MODEL // POOL_GROUPS

D_FF = ((8 * D_MODEL // 3 + 255) // 256) * 256

kernel_name = "hybrid_gmlp_ssd_pool_decoder"


def rms_norm(x, g):
    xf = x.astype(jnp.float32)
    y = xf * lax.rsqrt(jnp.mean(xf * xf, axis=-1, keepdims=True) + EPS)
    return (y * g.astype(jnp.float32)).astype(x.dtype)


def layer_norm(x, g, b):
    xf = x.astype(jnp.float32)
    mu = jnp.mean(xf, axis=-1, keepdims=True)
    xc = xf - mu
    y = xc * lax.rsqrt(jnp.mean(xc * xc, axis=-1, keepdims=True) + EPS)
    return (y * g.astype(jnp.float32) + b.astype(jnp.float32)).astype(x.dtype)


def gmlp_spatial_gating(u, v, ln_g, ln_b, w_s, b_s):
    bsz, seqlen, _ = v.shape
    n_chunks = seqlen // GM_CHUNK
    v = layer_norm(v, ln_g, ln_b).reshape(bsz, n_chunks, GM_CHUNK, GM_HEADS, GM_HEAD_DIM)
    causal = jnp.tril(jnp.ones((GM_CHUNK, GM_CHUNK), dtype=bool))
    w = jnp.where(causal[None], w_s, 0).astype(v.dtype)
    mixed = jnp.einsum("hts,bcshd->bcthd", w, v) + b_s.T.astype(v.dtype)[None, None, :, :, None]
    return u * mixed.reshape(bsz, seqlen, GM_WIDTH)


def causal_depthwise_conv(x, w, b):
    channels = x.shape[-1]
    y = lax.conv_general_dilated(
        x, w[:, None, :].astype(x.dtype), window_strides=(1,), padding=[(SSM_CONV - 1, 0)],
        dimension_numbers=("NWC", "WIO", "NWC"), feature_group_count=channels)
    return y + b.astype(x.dtype)


def segsum_exp(a_cum):
    n = a_cum.shape[-1]
    diff = a_cum[..., :, None] - a_cum[..., None, :]
    mask = jnp.tril(jnp.ones((n, n), dtype=bool))
    return jnp.exp(jnp.where(mask, diff, -jnp.inf))


def ssd_chunked(x, dt, a, b_mat, c_mat):
    bsz, seqlen = x.shape[:2]
    nc = seqlen // SSM_CHUNK

    def chunk(t):
        return t.reshape((bsz, nc, SSM_CHUNK) + t.shape[2:])

    xdt = chunk(x * dt[..., None])
    a_cum = jnp.cumsum(jnp.moveaxis(chunk(dt * a), 2, -1), axis=-1)
    b_c, c_c = chunk(b_mat), chunk(c_mat)
    decay = segsum_exp(a_cum)
    cb = jnp.einsum("bclgn,bcsgn->bcgls", c_c, b_c)
    y_diag = jnp.einsum("bcgls,bcgrls,bcsgrp->bclgrp", cb, decay, xdt)
    decay_to_end = jnp.exp(a_cum[..., -1:] - a_cum)
    chunk_states = jnp.einsum("bclgn,bcgrl,bclgrp->bcgrpn", b_c, decay_to_end, xdt)
    chunk_decay = jnp.exp(a_cum[..., -1])

    def step(state, inp):
        dec, new = inp
        return state * dec[..., None, None] + new, state

    init = jnp.zeros_like(chunk_states[:, 0])
    _, prev_states = lax.scan(step, init, (jnp.moveaxis(chunk_decay, 1, 0), jnp.moveaxis(chunk_states, 1, 0)))
    prev_states = jnp.moveaxis(prev_states, 0, 1)
    y_off = jnp.einsum("bclgn,bcgrpn,bcgrl->bclgrp", c_c, prev_states, jnp.exp(a_cum))
    return (y_diag + y_off).reshape((bsz, seqlen) + x.shape[2:])


def hybrid_gmlp_ssd_mixer(h, w_in, gm_ln_g, gm_ln_b, gm_ws, gm_bs, conv_w, conv_b,
                          dt_bias, a_log, d_skip, ssm_norm_g, w_out):
    f32 = jnp.float32
    bsz, seqlen, _ = h.shape
    proj = h @ w_in
    u, v, z, xbc, dt_raw = jnp.split(proj, SPLIT_POINTS, axis=-1)
    y_a = gmlp_spatial_gating(jax.nn.gelu(u), jax.nn.gelu(v), gm_ln_g, gm_ln_b, gm_ws, gm_bs)
    xbc = jax.nn.silu(causal_depthwise_conv(xbc, conv_w, conv_b))
    xs, b_mat, c_mat = jnp.split(xbc, (SSM_WIDTH, SSM_WIDTH + SSM_GROUPS * SSM_STATE), axis=-1)
    dt = jax.nn.softplus(dt_raw.astype(f32) + dt_bias.astype(f32))
    a = -jnp.exp(a_log.astype(f32))
    xs_h = xs.astype(f32).reshape(bsz, seqlen, SSM_GROUPS, SSM_HEADS_PER_GROUP, SSM_HEAD_DIM)
    y = ssd_chunked(
        xs_h,
        dt.reshape(bsz, seqlen, SSM_GROUPS, SSM_HEADS_PER_GROUP),
        a.reshape(SSM_GROUPS, SSM_HEADS_PER_GROUP),
        b_mat.astype(f32).reshape(bsz, seqlen, SSM_GROUPS, SSM_STATE),
        c_mat.astype(f32).reshape(bsz, seqlen, SSM_GROUPS, SSM_STATE))
    y = y + d_skip.astype(f32).reshape(SSM_GROUPS, SSM_HEADS_PER_GROUP)[:, :, None] * xs_h
    gated = (y.reshape(bsz, seqlen, SSM_WIDTH) * jax.nn.silu(z.astype(f32)))
    gated = gated.reshape(bsz, seqlen, SSM_GROUPS, SSM_WIDTH // SSM_GROUPS)
    gated = gated * lax.rsqrt(jnp.mean(gated * gated, axis=-1, keepdims=True) + EPS)
    y_b = (gated.reshape(bsz, seqlen, SSM_WIDTH) * ssm_norm_g.astype(f32)).astype(h.dtype)
    return jnp.concatenate([y_a, y_b], axis=-1) @ w_out


def multiscale_pool_mixer(h, pool_w, pool_b, pool_scale):
    f32 = jnp.float32
    bsz, seqlen, _ = h.shape
    hf = h.astype(f32).reshape(bsz, seqlen, POOL_GROUPS, POOL_GROUP_DIM)
    cs = jnp.cumsum(hf, axis=1)
    cs = jnp.concatenate([jnp.zeros_like(cs[:, :1]), cs], axis=1)
    pos = jnp.arange(1, seqlen + 1, dtype=f32)
    pooled = []
    for g, win in enumerate(POOL_WINDOWS):
        cs_g = cs[:, :, g]
        upper = cs_g[:, 1:]
        lower = jnp.pad(cs_g, ((0, 0), (win - 1, 0), (0, 0)))[:, :seqlen]
        count = jnp.minimum(pos, float(win))[None, :, None]
        pooled.append((upper - lower) / count)
    pooled = jnp.stack(pooled, axis=2)
    out = jnp.einsum("blgc,gcd->blgd", pooled - hf, pool_w.astype(f32)) + pool_b.astype(f32)
    return (out.reshape(bsz, seqlen, D_MODEL) * pool_scale.astype(f32)).astype(h.dtype)


def swiglu(h, w_gate, w_up, w_down):
    return (jax.nn.silu(h @ w_gate) * (h @ w_up)) @ w_down


def _fwd_setup_inputs(seed: int = 0) -> dict:
    key = jax.random.key(seed)
    ks = jax.random.split(key, 20)
    f32 = jnp.float32

    def nrm(k, shape, scale):
        return jax.random.normal(k, shape, f32) * scale

    x = nrm(ks[0], (BATCH, SEQ, D_MODEL), 1.0)
    norm_g = 1.0 + nrm(ks[1], (DEPTH, 4, D_MODEL), 0.02)
    w_in = nrm(ks[2], (N_EVEN, D_MODEL, IN_PROJ_DIM), D_MODEL ** -0.5)
    gm_ln_g = 1.0 + nrm(ks[3], (N_EVEN, GM_WIDTH), 0.02)
    gm_ln_b = nrm(ks[4], (N_EVEN, GM_WIDTH), 0.02)
    gm_ws = nrm(ks[5], (N_EVEN, GM_HEADS, GM_CHUNK, GM_CHUNK), GM_CHUNK ** -0.5)
    gm_bs = 1.0 + nrm(ks[6], (N_EVEN, GM_HEADS, GM_CHUNK), 0.02)
    conv_w = nrm(ks[7], (N_EVEN, SSM_CONV, SSM_CONV_DIM), SSM_CONV ** -0.5)
    conv_b = nrm(ks[8], (N_EVEN, SSM_CONV_DIM), 0.02)
    dt0 = jnp.exp(jax.random.uniform(ks[9], (N_EVEN, SSM_HEADS), f32, math.log(DT_MIN), math.log(DT_MAX)))
    dt0 = jnp.maximum(dt0, DT_FLOOR)
    dt_bias = dt0 + jnp.log(-jnp.expm1(-dt0))
    a_log = jnp.log(jax.random.uniform(ks[10], (N_EVEN, SSM_HEADS), f32, 1.0, 16.0))
    d_skip = 1.0 + nrm(ks[11], (N_EVEN, SSM_HEADS), 0.02)
    ssm_norm_g = 1.0 + nrm(ks[12], (N_EVEN, SSM_WIDTH), 0.02)
    w_out = nrm(ks[13], (N_EVEN, MIX_WIDTH, D_MODEL), MIX_WIDTH ** -0.5)
    pool_w = nrm(ks[14], (N_ODD, POOL_GROUPS, POOL_GROUP_DIM, POOL_GROUP_DIM), POOL_GROUP_DIM ** -0.5)
    pool_b = nrm(ks[15], (N_ODD, POOL_GROUPS, POOL_GROUP_DIM), 0.02)
    pool_scale = 1.0 + nrm(ks[16], (N_ODD, D_MODEL), 0.1)
    ffn_w_gate = nrm(ks[17], (DEPTH, D_MODEL, D_FF), D_MODEL ** -0.5)
    ffn_w_up = nrm(ks[18], (DEPTH, D_MODEL, D_FF), D_MODEL ** -0.5)
    ffn_w_down = nrm(ks[19], (DEPTH, D_FF, D_MODEL), D_FF ** -0.5)
    return {"x": x, "norm_g": norm_g, "w_in": w_in, "gm_ln_g": gm_ln_g, "gm_ln_b": gm_ln_b,
            "gm_ws": gm_ws, "gm_bs": gm_bs, "conv_w": conv_w, "conv_b": conv_b,
            "dt_bias": dt_bias, "a_log": a_log, "d_skip": d_skip, "ssm_norm_g": ssm_norm_g,
            "w_out": w_out, "pool_w": pool_w, "pool_b": pool_b, "pool_scale": pool_scale,
            "ffn_w_gate": ffn_w_gate, "ffn_w_up": ffn_w_up, "ffn_w_down": ffn_w_down}


def _fwd_reference(x, norm_g, w_in, gm_ln_g, gm_ln_b, gm_ws, gm_bs, conv_w, conv_b, dt_bias, a_log,
              d_skip, ssm_norm_g, w_out, pool_w, pool_b, pool_scale, ffn_w_gate, ffn_w_up, ffn_w_down):
    h = x
    for layer in range(DEPTH):
        i = layer // 2
        y = rms_norm(h, norm_g[layer, 0])
        if layer % 2 == 0:
            y = hybrid_gmlp_ssd_mixer(y, w_in[i], gm_ln_g[i], gm_ln_b[i], gm_ws[i], gm_bs[i],
                                      conv_w[i], conv_b[i], dt_bias[i], a_log[i], d_skip[i],
                                      ssm_norm_g[i], w_out[i])
        else:
            y = multiscale_pool_mixer(y, pool_w[i], pool_b[i], pool_scale[i])
        h = h + rms_norm(y, norm_g[layer, 1])
        y = swiglu(rms_norm(h, norm_g[layer, 2]), ffn_w_gate[layer], ffn_w_up[layer], ffn_w_down[layer])
        h = h + rms_norm(y, norm_g[layer, 3])
    return h


import jax as _jax
import jax.numpy as _jnp

TWIN_FORMAT = 'train_step'
FWD_PARAMS = ['x', 'norm_g', 'w_in', 'gm_ln_g', 'gm_ln_b', 'gm_ws', 'gm_bs', 'conv_w', 'conv_b', 'dt_bias', 'a_log', 'd_skip', 'ssm_norm_g', 'w_out', 'pool_w', 'pool_b', 'pool_scale', 'ffn_w_gate', 'ffn_w_up', 'ffn_w_down']
TWIN_WEIGHTS = ['norm_g', 'w_in', 'gm_ln_g', 'gm_ln_b', 'gm_ws', 'gm_bs', 'conv_w', 'conv_b', 'dt_bias', 'a_log', 'd_skip', 'ssm_norm_g', 'w_out', 'pool_w', 'pool_b', 'pool_scale', 'ffn_w_gate', 'ffn_w_up', 'ffn_w_down']
TWIN_DIFF_INPUT = 'x'
TWIN_INPUTS = ['x', 'norm_g', 'w_in', 'gm_ln_g', 'gm_ln_b', 'gm_ws', 'gm_bs', 'conv_w', 'conv_b', 'dt_bias', 'a_log', 'd_skip', 'ssm_norm_g', 'w_out', 'pool_w', 'pool_b', 'pool_scale', 'ffn_w_gate', 'ffn_w_up', 'ffn_w_down', 'loss_target', 'm_norm_g', 'm_w_in', 'm_gm_ln_g', 'm_gm_ln_b', 'm_gm_ws', 'm_gm_bs', 'm_conv_w', 'm_conv_b', 'm_dt_bias', 'm_a_log', 'm_d_skip', 'm_ssm_norm_g', 'm_w_out', 'm_pool_w', 'm_pool_b', 'm_pool_scale', 'm_ffn_w_gate', 'm_ffn_w_up', 'm_ffn_w_down', 'v_norm_g', 'v_w_in', 'v_gm_ln_g', 'v_gm_ln_b', 'v_gm_ws', 'v_gm_bs', 'v_conv_w', 'v_conv_b', 'v_dt_bias', 'v_a_log', 'v_d_skip', 'v_ssm_norm_g', 'v_w_out', 'v_pool_w', 'v_pool_b', 'v_pool_scale', 'v_ffn_w_gate', 'v_ffn_w_up', 'v_ffn_w_down']
TWIN_OUTPUTS = ['loss', 'grad_x', 'grad_norm_g', 'grad_w_in', 'grad_gm_ln_g', 'grad_gm_ln_b', 'grad_gm_ws', 'grad_gm_bs', 'grad_conv_w', 'grad_conv_b', 'grad_dt_bias', 'grad_a_log', 'grad_d_skip', 'grad_ssm_norm_g', 'grad_w_out', 'grad_pool_w', 'grad_pool_b', 'grad_pool_scale', 'grad_ffn_w_gate', 'grad_ffn_w_up', 'grad_ffn_w_down', 'delta_norm_g', 'delta_w_in', 'delta_gm_ln_g', 'delta_gm_ln_b', 'delta_gm_ws', 'delta_gm_bs', 'delta_conv_w', 'delta_conv_b', 'delta_dt_bias', 'delta_a_log', 'delta_d_skip', 'delta_ssm_norm_g', 'delta_w_out', 'delta_pool_w', 'delta_pool_b', 'delta_pool_scale', 'delta_ffn_w_gate', 'delta_ffn_w_up', 'delta_ffn_w_down', 'new_m_norm_g', 'new_m_w_in', 'new_m_gm_ln_g', 'new_m_gm_ln_b', 'new_m_gm_ws', 'new_m_gm_bs', 'new_m_conv_w', 'new_m_conv_b', 'new_m_dt_bias', 'new_m_a_log', 'new_m_d_skip', 'new_m_ssm_norm_g', 'new_m_w_out', 'new_m_pool_w', 'new_m_pool_b', 'new_m_pool_scale', 'new_m_ffn_w_gate', 'new_m_ffn_w_up', 'new_m_ffn_w_down', 'new_v_norm_g', 'new_v_w_in', 'new_v_gm_ln_g', 'new_v_gm_ln_b', 'new_v_gm_ws', 'new_v_gm_bs', 'new_v_conv_w', 'new_v_conv_b', 'new_v_dt_bias', 'new_v_a_log', 'new_v_d_skip', 'new_v_ssm_norm_g', 'new_v_w_out', 'new_v_pool_w', 'new_v_pool_b', 'new_v_pool_scale', 'new_v_ffn_w_gate', 'new_v_ffn_w_up', 'new_v_ffn_w_down']
TWIN_LEAF_KINDS = {'loss': 'loss', 'grad_x': 'grad_x', 'grad_norm_g': 'grad_w', 'grad_w_in': 'grad_w', 'grad_gm_ln_g': 'grad_w', 'grad_gm_ln_b': 'grad_w', 'grad_gm_ws': 'grad_w', 'grad_gm_bs': 'grad_w', 'grad_conv_w': 'grad_w', 'grad_conv_b': 'grad_w', 'grad_dt_bias': 'grad_w', 'grad_a_log': 'grad_w', 'grad_d_skip': 'grad_w', 'grad_ssm_norm_g': 'grad_w', 'grad_w_out': 'grad_w', 'grad_pool_w': 'grad_w', 'grad_pool_b': 'grad_w', 'grad_pool_scale': 'grad_w', 'grad_ffn_w_gate': 'grad_w', 'grad_ffn_w_up': 'grad_w', 'grad_ffn_w_down': 'grad_w', 'delta_norm_g': 'delta_w', 'delta_w_in': 'delta_w', 'delta_gm_ln_g': 'delta_w', 'delta_gm_ln_b': 'delta_w', 'delta_gm_ws': 'delta_w', 'delta_gm_bs': 'delta_w', 'delta_conv_w': 'delta_w', 'delta_conv_b': 'delta_w', 'delta_dt_bias': 'delta_w', 'delta_a_log': 'delta_w', 'delta_d_skip': 'delta_w', 'delta_ssm_norm_g': 'delta_w', 'delta_w_out': 'delta_w', 'delta_pool_w': 'delta_w', 'delta_pool_b': 'delta_w', 'delta_pool_scale': 'delta_w', 'delta_ffn_w_gate': 'delta_w', 'delta_ffn_w_up': 'delta_w', 'delta_ffn_w_down': 'delta_w', 'new_m_norm_g': 'new_m', 'new_m_w_in': 'new_m', 'new_m_gm_ln_g': 'new_m', 'new_m_gm_ln_b': 'new_m', 'new_m_gm_ws': 'new_m', 'new_m_gm_bs': 'new_m', 'new_m_conv_w': 'new_m', 'new_m_conv_b': 'new_m', 'new_m_dt_bias': 'new_m', 'new_m_a_log': 'new_m', 'new_m_d_skip': 'new_m', 'new_m_ssm_norm_g': 'new_m', 'new_m_w_out': 'new_m', 'new_m_pool_w': 'new_m', 'new_m_pool_b': 'new_m', 'new_m_pool_scale': 'new_m', 'new_m_ffn_w_gate': 'new_m', 'new_m_ffn_w_up': 'new_m', 'new_m_ffn_w_down': 'new_m', 'new_v_norm_g': 'new_v', 'new_v_w_in': 'new_v', 'new_v_gm_ln_g': 'new_v', 'new_v_gm_ln_b': 'new_v', 'new_v_gm_ws': 'new_v', 'new_v_gm_bs': 'new_v', 'new_v_conv_w': 'new_v', 'new_v_conv_b': 'new_v', 'new_v_dt_bias': 'new_v', 'new_v_a_log': 'new_v', 'new_v_d_skip': 'new_v', 'new_v_ssm_norm_g': 'new_v', 'new_v_w_out': 'new_v', 'new_v_pool_w': 'new_v', 'new_v_pool_b': 'new_v', 'new_v_pool_scale': 'new_v', 'new_v_ffn_w_gate': 'new_v', 'new_v_ffn_w_up': 'new_v', 'new_v_ffn_w_down': 'new_v'}


def _forward(args):
    return _fwd_reference(*[args[k] for k in FWD_PARAMS])


def _output_shape():
    def fwd():
        inp = _fwd_setup_inputs(0)
        return _fwd_reference(*[inp[k] for k in FWD_PARAMS])
    out = _jax.eval_shape(fwd)
    return out.shape, out.dtype

N_MICROBATCH = 1
ADAM_LR = 0.001
ADAM_B1 = 0.9
ADAM_B2 = 0.999
ADAM_EPS = 1e-08
ADAM_WD = 0.01
ADAM_STEP = 10
PER_EXAMPLE_BATCH_AXIS = {'x': 0, 'loss_target': 0}
SHARED_INPUTS = []
_WEIGHT_DTYPES = {'norm_g': _jnp.float32, 'w_in': _jnp.float32, 'gm_ln_g': _jnp.float32, 'gm_ln_b': _jnp.float32, 'gm_ws': _jnp.float32, 'gm_bs': _jnp.float32, 'conv_w': _jnp.float32, 'conv_b': _jnp.float32, 'dt_bias': _jnp.float32, 'a_log': _jnp.float32, 'd_skip': _jnp.float32, 'ssm_norm_g': _jnp.float32, 'w_out': _jnp.float32, 'pool_w': _jnp.float32, 'pool_b': _jnp.float32, 'pool_scale': _jnp.float32, 'ffn_w_gate': _jnp.float32, 'ffn_w_up': _jnp.float32, 'ffn_w_down': _jnp.float32}
MOMENT_SCALE = {'norm_g': 9.135687e+01, 'w_in': 1.181347e+00, 'gm_ln_g': 6.000897e-01, 'gm_ln_b': 5.411874e-01, 'gm_ws': 8.070004e-01, 'gm_bs': 1.170857e+00, 'conv_w': 1.433917e+00, 'conv_b': 3.916405e+00, 'dt_bias': 7.751494e+00, 'a_log': 1.154649e+01, 'd_skip': 8.002512e+00, 'ssm_norm_g': 3.878472e+00, 'w_out': 4.225607e+00, 'pool_w': 1.091375e+01, 'pool_b': 2.882290e+01, 'pool_scale': 1.195689e+01, 'ffn_w_gate': 8.815779e-01, 'ffn_w_up': 1.566153e+00, 'ffn_w_down': 2.766469e+00}


def _to_microbatches(a, axis):
    t = _jnp.moveaxis(a, axis, 0)
    t = t.reshape((N_MICROBATCH, t.shape[0] // N_MICROBATCH) + t.shape[1:])
    return _jnp.moveaxis(t, 1, axis + 1)


def setup_inputs(seed: int = 0) -> dict:
    inp = _fwd_setup_inputs(seed)
    key = _jax.random.fold_in(_jax.random.key(seed), 7919)
    shape, _ = _output_shape()
    out = dict(inp)
    out["loss_target"] = _jax.random.normal(_jax.random.fold_in(key, 0), shape, _jnp.float32)
    for i, name in enumerate(TWIN_WEIGHTS):
        w = inp[name].astype(_jnp.float32)
        if MOMENT_SCALE is None:
            s = _jnp.sqrt(_jnp.mean(_jnp.square(w)) + 1e-30)
        else:
            s = MOMENT_SCALE[name]
        km, kv = _jax.random.split(_jax.random.fold_in(key, i + 1))
        out[name] = w
        out["m_" + name] = s * _jax.random.normal(km, w.shape, _jnp.float32)
        out["v_" + name] = (s * s) * _jax.random.uniform(kv, w.shape, _jnp.float32, 0.5, 1.5)
    if N_MICROBATCH > 1:
        for name, axis in PER_EXAMPLE_BATCH_AXIS.items():
            out[name] = _to_microbatches(out[name], axis)
    return {'x': out['x'], 'norm_g': out['norm_g'], 'w_in': out['w_in'], 'gm_ln_g': out['gm_ln_g'], 'gm_ln_b': out['gm_ln_b'], 'gm_ws': out['gm_ws'], 'gm_bs': out['gm_bs'], 'conv_w': out['conv_w'], 'conv_b': out['conv_b'], 'dt_bias': out['dt_bias'], 'a_log': out['a_log'], 'd_skip': out['d_skip'], 'ssm_norm_g': out['ssm_norm_g'], 'w_out': out['w_out'], 'pool_w': out['pool_w'], 'pool_b': out['pool_b'], 'pool_scale': out['pool_scale'], 'ffn_w_gate': out['ffn_w_gate'], 'ffn_w_up': out['ffn_w_up'], 'ffn_w_down': out['ffn_w_down'], 'loss_target': out['loss_target'], 'm_norm_g': out['m_norm_g'], 'm_w_in': out['m_w_in'], 'm_gm_ln_g': out['m_gm_ln_g'], 'm_gm_ln_b': out['m_gm_ln_b'], 'm_gm_ws': out['m_gm_ws'], 'm_gm_bs': out['m_gm_bs'], 'm_conv_w': out['m_conv_w'], 'm_conv_b': out['m_conv_b'], 'm_dt_bias': out['m_dt_bias'], 'm_a_log': out['m_a_log'], 'm_d_skip': out['m_d_skip'], 'm_ssm_norm_g': out['m_ssm_norm_g'], 'm_w_out': out['m_w_out'], 'm_pool_w': out['m_pool_w'], 'm_pool_b': out['m_pool_b'], 'm_pool_scale': out['m_pool_scale'], 'm_ffn_w_gate': out['m_ffn_w_gate'], 'm_ffn_w_up': out['m_ffn_w_up'], 'm_ffn_w_down': out['m_ffn_w_down'], 'v_norm_g': out['v_norm_g'], 'v_w_in': out['v_w_in'], 'v_gm_ln_g': out['v_gm_ln_g'], 'v_gm_ln_b': out['v_gm_ln_b'], 'v_gm_ws': out['v_gm_ws'], 'v_gm_bs': out['v_gm_bs'], 'v_conv_w': out['v_conv_w'], 'v_conv_b': out['v_conv_b'], 'v_dt_bias': out['v_dt_bias'], 'v_a_log': out['v_a_log'], 'v_d_skip': out['v_d_skip'], 'v_ssm_norm_g': out['v_ssm_norm_g'], 'v_w_out': out['v_w_out'], 'v_pool_w': out['v_pool_w'], 'v_pool_b': out['v_pool_b'], 'v_pool_scale': out['v_pool_scale'], 'v_ffn_w_gate': out['v_ffn_w_gate'], 'v_ffn_w_up': out['v_ffn_w_up'], 'v_ffn_w_down': out['v_ffn_w_down']}


def _loss(weights, diff, rest, loss_target):
    with _jax.named_scope("forward"):
        args = {**rest, TWIN_DIFF_INPUT: diff, **{k: w.astype(_WEIGHT_DTYPES[k]) for k, w in weights.items()}}
        y = _forward(args)
    with _jax.named_scope("loss_head"):
        err = _jnp.square(y.astype(_jnp.float32) - loss_target)
        return 0.5 * _jnp.sum(_jnp.mean(err, axis=-1)) if err.ndim else 0.5 * err


def _adamw(w, g, m, v):
    m = ADAM_B1 * m + (1.0 - ADAM_B1) * g
    v = ADAM_B2 * v + (1.0 - ADAM_B2) * _jnp.square(g)
    m_hat = m / (1.0 - ADAM_B1 ** ADAM_STEP)
    v_hat = v / (1.0 - ADAM_B2 ** ADAM_STEP)
    delta = -ADAM_LR * (m_hat / (_jnp.sqrt(v_hat) + ADAM_EPS) + ADAM_WD * w)
    return delta, m, v


def reference(x, norm_g, w_in, gm_ln_g, gm_ln_b, gm_ws, gm_bs, conv_w, conv_b, dt_bias, a_log, d_skip, ssm_norm_g, w_out, pool_w, pool_b, pool_scale, ffn_w_gate, ffn_w_up, ffn_w_down, loss_target, m_norm_g, m_w_in, m_gm_ln_g, m_gm_ln_b, m_gm_ws, m_gm_bs, m_conv_w, m_conv_b, m_dt_bias, m_a_log, m_d_skip, m_ssm_norm_g, m_w_out, m_pool_w, m_pool_b, m_pool_scale, m_ffn_w_gate, m_ffn_w_up, m_ffn_w_down, v_norm_g, v_w_in, v_gm_ln_g, v_gm_ln_b, v_gm_ws, v_gm_bs, v_conv_w, v_conv_b, v_dt_bias, v_a_log, v_d_skip, v_ssm_norm_g, v_w_out, v_pool_w, v_pool_b, v_pool_scale, v_ffn_w_gate, v_ffn_w_up, v_ffn_w_down):
    given = dict(x=x, norm_g=norm_g, w_in=w_in, gm_ln_g=gm_ln_g, gm_ln_b=gm_ln_b, gm_ws=gm_ws, gm_bs=gm_bs, conv_w=conv_w, conv_b=conv_b, dt_bias=dt_bias, a_log=a_log, d_skip=d_skip, ssm_norm_g=ssm_norm_g, w_out=w_out, pool_w=pool_w, pool_b=pool_b, pool_scale=pool_scale, ffn_w_gate=ffn_w_gate, ffn_w_up=ffn_w_up, ffn_w_down=ffn_w_down, loss_target=loss_target, m_norm_g=m_norm_g, m_w_in=m_w_in, m_gm_ln_g=m_gm_ln_g, m_gm_ln_b=m_gm_ln_b, m_gm_ws=m_gm_ws, m_gm_bs=m_gm_bs, m_conv_w=m_conv_w, m_conv_b=m_conv_b, m_dt_bias=m_dt_bias, m_a_log=m_a_log, m_d_skip=m_d_skip, m_ssm_norm_g=m_ssm_norm_g, m_w_out=m_w_out, m_pool_w=m_pool_w, m_pool_b=m_pool_b, m_pool_scale=m_pool_scale, m_ffn_w_gate=m_ffn_w_gate, m_ffn_w_up=m_ffn_w_up, m_ffn_w_down=m_ffn_w_down, v_norm_g=v_norm_g, v_w_in=v_w_in, v_gm_ln_g=v_gm_ln_g, v_gm_ln_b=v_gm_ln_b, v_gm_ws=v_gm_ws, v_gm_bs=v_gm_bs, v_conv_w=v_conv_w, v_conv_b=v_conv_b, v_dt_bias=v_dt_bias, v_a_log=v_a_log, v_d_skip=v_d_skip, v_ssm_norm_g=v_ssm_norm_g, v_w_out=v_w_out, v_pool_w=v_pool_w, v_pool_b=v_pool_b, v_pool_scale=v_pool_scale, v_ffn_w_gate=v_ffn_w_gate, v_ffn_w_up=v_ffn_w_up, v_ffn_w_down=v_ffn_w_down)
    weights = {n: given[n] for n in TWIN_WEIGHTS}
    shared = {n: given[n] for n in SHARED_INPUTS}
    per_example = {n: given[n] for n in ['x']}
    grad_fn = _jax.value_and_grad(_loss, argnums=(0, 1))

    def one_microbatch(ex, loss_target):
        ex = dict(ex)
        diff = ex.pop(TWIN_DIFF_INPUT)
        return grad_fn(weights, diff, {**shared, **ex}, loss_target)

    if N_MICROBATCH == 1:
        loss, (grad_w, grad_x) = one_microbatch(per_example, given["loss_target"])
    else:
        def body(carry, xs):
            loss_sum, grad_sum = carry
            l_k, (gw_k, gx_k) = one_microbatch(xs[0], xs[1])
            with _jax.named_scope("update"):
                return (loss_sum + l_k, _jax.tree.map(_jnp.add, grad_sum, gw_k)), gx_k

        init = (_jnp.zeros((), _jnp.float32), _jax.tree.map(_jnp.zeros_like, weights))
        (loss, grad_w), grad_x = _jax.lax.scan(body, init, (per_example, given["loss_target"]))
    with _jax.named_scope("update"):
        delta_w, new_m, new_v = {}, {}, {}
        for n in TWIN_WEIGHTS:
            delta_w[n], new_m[n], new_v[n] = _adamw(weights[n], grad_w[n], given["m_" + n], given["v_" + n])
    return (loss, grad_x, *[grad_w[n] for n in TWIN_WEIGHTS], *[delta_w[n] for n in TWIN_WEIGHTS],
            *[new_m[n] for n in TWIN_WEIGHTS], *[new_v[n] for n in TWIN_WEIGHTS])
```

```python
import functools
import math

import jax
import jax.numpy as jnp
from jax import lax
from jax.experimental import pallas as pl
from jax.experimental.pallas import tpu as pltpu

F32 = jnp.float32
MXU_DTYPE = jnp.bfloat16

N_DEV = 8
D_MODEL = 1024
EPS = 1e-6
GM_HEADS = 4
GM_HEAD_DIM = 256
CHUNK = 128
SSM_HEADS = 16
SSM_GROUPS = 4
SSM_STATE = 128
SSM_CONV = 4
CONV_DIM = 2048
POOL_WINDOWS = (2, 4, 8, 16)
POOL_DIM = 256
D_FF = 2816
FF_TILE = 256
IN_MAIN = 5120
LANES = 128
CONV_HALO = 8
POOL_HALO = 16
ADAM_LR, ADAM_B1, ADAM_B2, ADAM_EPS, ADAM_WD, ADAM_STEP = 0.001, 0.9, 0.999, 1e-08, 0.01, 10

VMEM_LIMIT = 56 * 1024 * 1024
ROW_TILE = 512
MM_TM = 1024
PACK_TILE = 1024

WEIGHTS = ['norm_g', 'w_in', 'gm_ln_g', 'gm_ln_b', 'gm_ws', 'gm_bs', 'conv_w', 'conv_b', 'dt_bias', 'a_log',
           'd_skip', 'ssm_norm_g', 'w_out', 'pool_w', 'pool_b', 'pool_scale', 'ffn_w_gate', 'ffn_w_up', 'ffn_w_down']
SHARD_AXIS = {'norm_g': 2, 'w_in': 2, 'gm_ln_g': None, 'gm_ln_b': None, 'gm_ws': None, 'gm_bs': None, 'conv_w': 2,
              'conv_b': None, 'dt_bias': None, 'a_log': None, 'd_skip': None, 'ssm_norm_g': None, 'w_out': 1,
              'pool_w': 2, 'pool_b': 2, 'pool_scale': 1, 'ffn_w_gate': 2, 'ffn_w_up': 2, 'ffn_w_down': 1}
GATHER_BF16 = ['w_in', 'w_out', 'pool_w', 'ffn_w_gate', 'ffn_w_up', 'ffn_w_down']
GATHER_F32 = ['norm_g', 'conv_w', 'pool_b', 'pool_scale']


def _cparams(sem=None):
    return pltpu.CompilerParams(dimension_semantics=sem, vmem_limit_bytes=VMEM_LIMIT)


def _dot(a, b):
    return jnp.dot(a.astype(MXU_DTYPE), b.astype(MXU_DTYPE), preferred_element_type=F32)


def _dot_nt(a, b):
    return lax.dot_general(a.astype(MXU_DTYPE), b.astype(MXU_DTYPE), (((1,), (1,)), ((), ())),
                           preferred_element_type=F32)


def _dot_tn(a, b):
    return lax.dot_general(a.astype(MXU_DTYPE), b.astype(MXU_DTYPE), (((0,), (0,)), ((), ())),
                           preferred_element_type=F32)


def _dot_exact(a, b):
    return jnp.dot(a, b, precision=lax.Precision.HIGHEST, preferred_element_type=F32)


def _sigmoid(x):
    return 1.0 / (1.0 + jnp.exp(-x))


def _silu(x):
    return x * _sigmoid(x)


def _silu_grad(x):
    s = _sigmoid(x)
    return s * (1.0 + x * (1.0 - s))


_GELU_C = math.sqrt(2.0 / math.pi)


def _gelu(x):
    return 0.5 * x * (1.0 + jnp.tanh(_GELU_C * (x + 0.044715 * x * x * x)))


def _gelu_grad(x):
    t = jnp.tanh(_GELU_C * (x + 0.044715 * x * x * x))
    return 0.5 * (1.0 + t) + 0.5 * x * (1.0 - t * t) * _GELU_C * (1.0 + 3.0 * 0.044715 * x * x)


def _softplus(x):
    return jnp.maximum(x, 0.0) + jnp.log1p(jnp.exp(-jnp.abs(x)))


def _rms_scale(x):
    return lax.rsqrt(jnp.mean(x * x, axis=-1, keepdims=True) + EPS)


def _rms_bwd(dy, x, g):
    r = _rms_scale(x)
    xn = x * r
    dxn = dy * g
    dx = r * (dxn - xn * jnp.mean(dxn * xn, axis=-1, keepdims=True))
    return dx, dy * xn


def _colsum(x):
    return jnp.sum(x, axis=0, keepdims=True)


def _exchange(x, *, name, gather):
    blk = x.shape if gather else x.shape[1:]

    def body(x_ref, out_ref, send_sems, recv_sems, local_sem):
        mx, my, mc = lax.axis_index("x"), lax.axis_index("y"), lax.axis_index("c")
        me = 4 * mx + 2 * my + mc

        def flip(v, bit):
            return 1 - v if bit else v

        sends, recvs = [], []
        for k in (1, 2, 4, 6, 3, 5, 7):
            px, py, pc = flip(mx, (k >> 2) & 1), flip(my, (k >> 1) & 1), flip(mc, k & 1)
            peer = 4 * px + 2 * py + pc
            src = x_ref if gather else x_ref.at[peer]
            sends.append(pltpu.make_async_remote_copy(
                src_ref=src, dst_ref=out_ref.at[me], send_sem=send_sems.at[k - 1], recv_sem=recv_sems.at[k - 1],
                device_id=(px, py, pc), device_id_type=pl.DeviceIdType.MESH))
            recvs.append(pltpu.make_async_remote_copy(
                src_ref=src, dst_ref=out_ref.at[peer], send_sem=send_sems.at[k - 1], recv_sem=recv_sems.at[k - 1],
                device_id=(px, py, pc), device_id_type=pl.DeviceIdType.MESH))
        own = pltpu.make_async_copy(x_ref if gather else x_ref.at[me], out_ref.at[me], local_sem)
        for cp in sends:
            cp.start()
        own.start()
        for cp in recvs:
            cp.wait_recv()
        for cp in sends:
            cp.wait_send()
        own.wait()

    return pl.pallas_call(
        body, name=name,
        out_shape=jax.ShapeDtypeStruct((N_DEV,) + tuple(blk), x.dtype),
        in_specs=[pl.BlockSpec(memory_space=pl.ANY)],
        out_specs=pl.BlockSpec(memory_space=pl.ANY),
        scratch_shapes=[pltpu.SemaphoreType.DMA((N_DEV - 1,)), pltpu.SemaphoreType.DMA((N_DEV - 1,)),
                        pltpu.SemaphoreType.DMA],
    )(x)


def _mm(a, b, *, name, out_dtype=F32, tm=MM_TM, tn=512, tk=None):
    m, k = a.shape
    n = b.shape[1]
    tm, tn = min(tm, m), min(tn, n)
    tk = k if tk is None else tk
    nk = k // tk
    assert m % tm == 0 and n % tn == 0 and k % tk == 0

    def body(a_ref, b_ref, o_ref, acc_ref):
        kk = pl.program_id(2)
        part = _dot(a_ref[...], b_ref[...])
        if nk == 1:
            o_ref[...] = part.astype(out_dtype)
        else:
            @pl.when(kk == 0)
            def _():
                acc_ref[...] = part

            @pl.when(kk > 0)
            def _():
                acc_ref[...] += part

            @pl.when(kk == nk - 1)
            def _():
                o_ref[...] = acc_ref[...].astype(out_dtype)

    return pl.pallas_call(
        body, name=name, grid=(m // tm, n // tn, nk),
        out_shape=jax.ShapeDtypeStruct((m, n), out_dtype),
        in_specs=[pl.BlockSpec((tm, tk), lambda i, j, kk: (i, kk)), pl.BlockSpec((tk, tn), lambda i, j, kk: (kk, j))],
        out_specs=pl.BlockSpec((tm, tn), lambda i, j, kk: (i, j)),
        scratch_shapes=[pltpu.VMEM((tm, tn) if nk > 1 else (8, LANES), F32)],
        compiler_params=_cparams(("parallel", "parallel", "arbitrary")),
    )(a, b)


def _mm_tn(a, b, *, name, tm=1024, tn=512, tk=1024):
    t, m = a.shape
    n = b.shape[1]
    tm, tn, tk = min(tm, m), min(tn, n), min(tk, t)
    nk = t // tk
    assert m % tm == 0 and n % tn == 0 and t % tk == 0

    def body(a_ref, b_ref, o_ref):
        kk = pl.program_id(2)
        part = _dot_tn(a_ref[...], b_ref[...])

        @pl.when(kk == 0)
        def _():
            o_ref[...] = part

        @pl.when(kk > 0)
        def _():
            o_ref[...] += part

    return pl.pallas_call(
        body, name=name, grid=(m // tm, n // tn, nk),
        out_shape=jax.ShapeDtypeStruct((m, n), F32),
        in_specs=[pl.BlockSpec((tk, tm), lambda i, j, kk: (kk, i)), pl.BlockSpec((tk, tn), lambda i, j, kk: (kk, j))],
        out_specs=pl.BlockSpec((tm, tn), lambda i, j, kk: (i, j)),
        compiler_params=_cparams(("parallel", "parallel", "arbitrary")),
    )(a, b)


def _mm_swiglu(a, w_gu, *, name, tm=MM_TM):
    m, k = a.shape
    n = w_gu.shape[1]
    nt = n // (2 * FF_TILE)

    def body(a_ref, b_ref, gu_ref, act_ref):
        gu = _dot(a_ref[...], b_ref[...])
        gu_ref[...] = gu
        act_ref[...] = (_silu(gu[:, :FF_TILE]) * gu[:, FF_TILE:]).astype(MXU_DTYPE)

    return pl.pallas_call(
        body, name=name, grid=(m // tm, nt),
        out_shape=(jax.ShapeDtypeStruct((m, n), F32), jax.ShapeDtypeStruct((m, n // 2), MXU_DTYPE)),
        in_specs=[pl.BlockSpec((tm, k), lambda i, j: (i, 0)), pl.BlockSpec((k, 2 * FF_TILE), lambda i, j: (0, j))],
        out_specs=(pl.BlockSpec((tm, 2 * FF_TILE), lambda i, j: (i, j)), pl.BlockSpec((tm, FF_TILE), lambda i, j: (i, j))),
        compiler_params=_cparams(("parallel", "parallel")),
    )(a, w_gu)


def _mm_dswiglu(dd, w_down_t, gu, *, name, tm=MM_TM):
    m, k = dd.shape
    n = gu.shape[1]
    nt = n // (2 * FF_TILE)

    def body(d_ref, w_ref, gu_ref, o_ref):
        dact = _dot(d_ref[...], w_ref[...])
        gate, up = gu_ref[:, :FF_TILE], gu_ref[:, FF_TILE:]
        o_ref[:, :FF_TILE] = (dact * up * _silu_grad(gate)).astype(MXU_DTYPE)
        o_ref[:, FF_TILE:] = (dact * _silu(gate)).astype(MXU_DTYPE)

    return pl.pallas_call(
        body, name=name, grid=(m // tm, nt),
        out_shape=jax.ShapeDtypeStruct((m, n), MXU_DTYPE),
        in_specs=[pl.BlockSpec((tm, k), lambda i, j: (i, 0)), pl.BlockSpec((k, FF_TILE), lambda i, j: (0, j)),
                  pl.BlockSpec((tm, 2 * FF_TILE), lambda i, j: (i, j))],
        out_specs=pl.BlockSpec((tm, 2 * FF_TILE), lambda i, j: (i, j)),
        compiler_params=_cparams(("parallel", "parallel")),
    )(dd, w_down_t, gu)


def _row_spec(width, tr=ROW_TILE):
    return pl.BlockSpec((tr, width), lambda i: (i, 0))


def _vec_spec(width, rows=1):
    return pl.BlockSpec((rows, width), lambda i: (0, 0))


def _rn_fwd(h, g, *, name, out_dtype):
    rows, d = h.shape

    def body(h_ref, g_ref, o_ref):
        x = h_ref[...]
        o_ref[...] = (x * _rms_scale(x) * g_ref[...]).astype(out_dtype)

    return pl.pallas_call(
        body, name=name, grid=(rows // ROW_TILE,),
        out_shape=jax.ShapeDtypeStruct((rows, d), out_dtype),
        in_specs=[_row_spec(d), _vec_spec(d)], out_specs=_row_spec(d),
        compiler_params=_cparams(("parallel",)),
    )(h, g)


def _resid_rn_fwd(h_in, o, g_post, g_next, *, name, next_dtype):
    rows, d = h_in.shape

    def body(h_ref, o_ref, gp_ref, gn_ref, ho_ref, yn_ref):
        ov = o_ref[...]
        h = h_ref[...] + ov * _rms_scale(ov) * gp_ref[...]
        ho_ref[...] = h
        yn_ref[...] = (h * _rms_scale(h) * gn_ref[...]).astype(next_dtype)

    return pl.pallas_call(
        body, name=name, grid=(rows // ROW_TILE,),
        out_shape=(jax.ShapeDtypeStruct((rows, d), F32), jax.ShapeDtypeStruct((rows, d), next_dtype)),
        in_specs=[_row_spec(d), _row_spec(d), _vec_spec(d), _vec_spec(d)],
        out_specs=(_row_spec(d), _row_spec(d)),
        compiler_params=_cparams(("parallel",)),
    )(h_in, o, g_post, g_next)


def _resid_loss(h_in, o, g_post, target, *, name):
    rows, d = h_in.shape

    def body(h_ref, o_ref, gp_ref, t_ref, dh_ref, loss_ref):
        ov = o_ref[...]
        err = h_ref[...] + ov * _rms_scale(ov) * gp_ref[...] - t_ref[...]
        dh_ref[...] = err * (1.0 / d)

        @pl.when(pl.program_id(0) == 0)
        def _():
            loss_ref[...] = jnp.zeros_like(loss_ref)

        loss_ref[...] += 0.5 * jnp.sum(jnp.mean(err * err, axis=-1, keepdims=True), axis=0, keepdims=True)

    return pl.pallas_call(
        body, name=name, grid=(rows // ROW_TILE,),
        out_shape=(jax.ShapeDtypeStruct((rows, d), F32), jax.ShapeDtypeStruct((1, 1), F32)),
        in_specs=[_row_spec(d), _row_spec(d), _vec_spec(d), _row_spec(d)],
        out_specs=(_row_spec(d), pl.BlockSpec((1, 1), lambda i: (0, 0))),
        compiler_params=_cparams(("arbitrary",)),
    )(h_in, o, g_post, target)


def _resid_bwd_post(dh, o, g_post, *, name, out_dtype):
    rows, d = dh.shape

    def body(dh_ref, o_ref, g_ref, do_ref, dg_ref):
        do, dg = _rms_bwd(dh_ref[...], o_ref[...], g_ref[...])
        do_ref[...] = do.astype(out_dtype)

        @pl.when(pl.program_id(0) == 0)
        def _():
            dg_ref[...] = jnp.zeros_like(dg_ref)

        dg_ref[...] += _colsum(dg)

    return pl.pallas_call(
        body, name=name, grid=(rows // ROW_TILE,),
        out_shape=(jax.ShapeDtypeStruct((rows, d), out_dtype), jax.ShapeDtypeStruct((1, d), F32)),
        in_specs=[_row_spec(d), _row_spec(d), _vec_spec(d)],
        out_specs=(_row_spec(d), _vec_spec(d)),
        compiler_params=_cparams(("arbitrary",)),
    )(dh, o, g_post)


def _resid_bwd_pre(dh, dyn_list, h_in, g_pre, *, name):
    rows, d = dh.shape
    n_dyn = len(dyn_list)

    def body(*refs):
        dh_ref, dyn_refs, h_ref, g_ref, out_ref, dg_ref = refs[0], refs[1:1 + n_dyn], *refs[1 + n_dyn:]
        dyn = dyn_refs[0][...]
        for r in dyn_refs[1:]:
            dyn = dyn + r[...]
        dx, dg = _rms_bwd(dyn, h_ref[...], g_ref[...])
        out_ref[...] = dh_ref[...] + dx

        @pl.when(pl.program_id(0) == 0)
        def _():
            dg_ref[...] = jnp.zeros_like(dg_ref)

        dg_ref[...] += _colsum(dg)

    return pl.pallas_call(
        body, name=name, grid=(rows // ROW_TILE,),
        out_shape=(jax.ShapeDtypeStruct((rows, d), F32), jax.ShapeDtypeStruct((1, d), F32)),
        in_specs=[_row_spec(d)] + [_row_spec(d)] * n_dyn + [_row_spec(d), _vec_spec(d)],
        out_specs=(_row_spec(d), _vec_spec(d)),
        compiler_params=_cparams(("arbitrary",)),
    )(dh, *dyn_list, h_in, g_pre)


def _layer_norm_stats(x):
    mu = jnp.mean(x, axis=-1, keepdims=True)
    xc = x - mu
    rstd = lax.rsqrt(jnp.mean(xc * xc, axis=-1, keepdims=True) + EPS)
    return xc * rstd, rstd


def _gmlp_fwd(proj, ln_g, ln_b, wm, bcol, *, name):
    rows = proj.shape[0]
    tr = ROW_TILE

    def body(u_ref, v_ref, lg_ref, lb_ref, wm_ref, bc_ref, ya_ref):
        vhat, _ = _layer_norm_stats(_gelu(v_ref[...]))
        vl = (vhat * lg_ref[...] + lb_ref[...]).astype(MXU_DTYPE)
        gu = _gelu(u_ref[...])
        bc = bc_ref[...]
        for c in range(tr // CHUNK):
            rs = slice(c * CHUNK, (c + 1) * CHUNK)
            for h in range(GM_HEADS):
                cs = slice(h * GM_HEAD_DIM, (h + 1) * GM_HEAD_DIM)
                mixed = _dot(wm_ref[h], vl[rs, cs]) + bc[:, h:h + 1]
                ya_ref[rs, cs] = (gu[rs, cs] * mixed).astype(MXU_DTYPE)

    return pl.pallas_call(
        body, name=name, grid=(rows // tr,),
        out_shape=jax.ShapeDtypeStruct((rows, D_MODEL), MXU_DTYPE),
        in_specs=[pl.BlockSpec((tr, D_MODEL), lambda i: (i, 2)), pl.BlockSpec((tr, D_MODEL), lambda i: (i, 3)),
                  _vec_spec(D_MODEL), _vec_spec(D_MODEL),
                  pl.BlockSpec((GM_HEADS, CHUNK, CHUNK), lambda i: (0, 0, 0)), _vec_spec(LANES, CHUNK)],
        out_specs=_row_spec(D_MODEL, tr),
        compiler_params=_cparams(("parallel",)),
    )(proj, proj, ln_g, ln_b, wm, bcol)


def _gmlp_bwd(proj, dcat, ln_g, ln_b, wm, wm_t, bcol, *, name):
    rows = proj.shape[0]
    tr = ROW_TILE

    def body(u_ref, v_ref, dy_ref, lg_ref, lb_ref, wm_ref, wmt_ref, bc_ref,
             du_ref, dv_ref, dwm_ref, dbc_ref, dlg_ref, dlb_ref, dvl_scr):
        @pl.when(pl.program_id(0) == 0)
        def _():
            dwm_ref[...] = jnp.zeros_like(dwm_ref)
            dbc_ref[...] = jnp.zeros_like(dbc_ref)
            dlg_ref[...] = jnp.zeros_like(dlg_ref)
            dlb_ref[...] = jnp.zeros_like(dlb_ref)

        u, v = u_ref[...], v_ref[...]
        gv = _gelu(v)
        vhat, rstd = _layer_norm_stats(gv)
        lg = lg_ref[...]
        vl = (vhat * lg + lb_ref[...]).astype(MXU_DTYPE)
        gu = _gelu(u)
        dy = dy_ref[...]
        bc = bc_ref[...]
        row = lax.broadcasted_iota(jnp.int32, (CHUNK, CHUNK), 0)
        lane = lax.broadcasted_iota(jnp.int32, (CHUNK, CHUNK), 1)
        causal = lane <= row
        dbc = jnp.zeros((CHUNK, LANES), F32)
        for c in range(tr // CHUNK):
            rs = slice(c * CHUNK, (c + 1) * CHUNK)
            for h in range(GM_HEADS):
                cs = slice(h * GM_HEAD_DIM, (h + 1) * GM_HEAD_DIM)
                vl_h = vl[rs, cs]
                mixed = _dot(wm_ref[h], vl_h) + bc[:, h:h + 1]
                dy_h = dy[rs, cs]
                du_ref[rs, cs] = (dy_h * mixed * _gelu_grad(u[rs, cs])).astype(MXU_DTYPE)
                dmixed = dy_h * gu[rs, cs]
                dwm_ref[h] += jnp.where(causal, _dot_nt(dmixed, vl_h), 0.0)
                dbc = dbc + jnp.where(lane == h, jnp.sum(dmixed, axis=1, keepdims=True), 0.0)
                dvl_scr[rs, cs] = _dot(wmt_ref[h], dmixed)
        dbc_ref[...] += dbc
        dvl = dvl_scr[...]
        dlg_ref[...] += _colsum(dvl * vhat)
        dlb_ref[...] += _colsum(dvl)
        dvh = dvl * lg
        dgv = rstd * (dvh - jnp.mean(dvh, axis=-1, keepdims=True) - vhat * jnp.mean(dvh * vhat, axis=-1, keepdims=True))
        dv_ref[...] = (dgv * _gelu_grad(v)).astype(MXU_DTYPE)

    return pl.pallas_call(
        body, name=name, grid=(rows // tr,),
        out_shape=(jax.ShapeDtypeStruct((rows, D_MODEL), MXU_DTYPE), jax.ShapeDtypeStruct((rows, D_MODEL), MXU_DTYPE),
                   jax.ShapeDtypeStruct((GM_HEADS, CHUNK, CHUNK), F32), jax.ShapeDtypeStruct((CHUNK, LANES), F32),
                   jax.ShapeDtypeStruct((1, D_MODEL), F32), jax.ShapeDtypeStruct((1, D_MODEL), F32)),
        in_specs=[pl.BlockSpec((tr, D_MODEL), lambda i: (i, 2)), pl.BlockSpec((tr, D_MODEL), lambda i: (i, 3)),
                  pl.BlockSpec((tr, D_MODEL), lambda i: (i, 0)), _vec_spec(D_MODEL), _vec_spec(D_MODEL),
                  pl.BlockSpec((GM_HEADS, CHUNK, CHUNK), lambda i: (0, 0, 0)),
                  pl.BlockSpec((GM_HEADS, CHUNK, CHUNK), lambda i: (0, 0, 0)), _vec_spec(LANES, CHUNK)],
        out_specs=(_row_spec(D_MODEL, tr), _row_spec(D_MODEL, tr),
                   pl.BlockSpec((GM_HEADS, CHUNK, CHUNK), lambda i: (0, 0, 0)), _vec_spec(LANES, CHUNK),
                   _vec_spec(D_MODEL), _vec_spec(D_MODEL)),
        scratch_shapes=[pltpu.VMEM((tr, D_MODEL), F32)],
        compiler_params=_cparams(("arbitrary",)),
    )(proj, proj, dcat, ln_g, ln_b, wm, wm_t, bcol)


def _conv_fwd(proj, conv_w8, conv_b, *, name):
    rows = proj.shape[0]
    tr = ROW_TILE
    hb = tr // CONV_HALO

    def body(x_ref, prev_ref, w_ref, b_ref, pre_ref, buf):
        first = pl.program_id(0) == 0
        buf[pl.ds(0, CONV_HALO), :] = jnp.where(first, 0.0, prev_ref[...])
        buf[pl.ds(CONV_HALO, tr), :] = x_ref[...]
        acc = jnp.broadcast_to(b_ref[...], (tr, CONV_DIM))
        for k in range(SSM_CONV):
            acc = acc + w_ref[k:k + 1, :] * buf[pl.ds(CONV_HALO - (SSM_CONV - 1) + k, tr), :]
        pre_ref[...] = acc

    return pl.pallas_call(
        body, name=name, grid=(rows // tr,),
        out_shape=jax.ShapeDtypeStruct((rows, CONV_DIM), F32),
        in_specs=[pl.BlockSpec((tr, CONV_DIM), lambda i: (i, 0)),
                  pl.BlockSpec((CONV_HALO, CONV_DIM), lambda i: (jnp.maximum(i * hb - 1, 0), 0)),
                  _vec_spec(CONV_DIM, 8), _vec_spec(CONV_DIM)],
        out_specs=_row_spec(CONV_DIM, tr),
        scratch_shapes=[pltpu.VMEM((tr + CONV_HALO, CONV_DIM), F32)],
        compiler_params=_cparams(("parallel",)),
    )(proj, proj, conv_w8, conv_b)


def _conv_bwd(dpre, proj, conv_w8, *, name):
    rows = proj.shape[0]
    tr = ROW_TILE
    hb = tr // CONV_HALO
    nblk = rows // tr

    def body(d_ref, dnext_ref, x_ref, prev_ref, w_ref, dx_ref, dw_ref, db_ref, dbuf, xbuf):
        i = pl.program_id(0)

        @pl.when(i == 0)
        def _():
            dw_ref[...] = jnp.zeros_like(dw_ref)
            db_ref[...] = jnp.zeros_like(db_ref)

        d = d_ref[...]
        dbuf[pl.ds(0, tr), :] = d
        dbuf[pl.ds(tr, CONV_HALO), :] = jnp.where(i == nblk - 1, 0.0, dnext_ref[...])
        xbuf[pl.ds(0, CONV_HALO), :] = jnp.where(i == 0, 0.0, prev_ref[...])
        xbuf[pl.ds(CONV_HALO, tr), :] = x_ref[...]
        acc = jnp.zeros((tr, CONV_DIM), F32)
        for k in range(SSM_CONV):
            acc = acc + w_ref[k:k + 1, :] * dbuf[pl.ds(SSM_CONV - 1 - k, tr), :]
            dw_ref[k:k + 1, :] += _colsum(d * xbuf[pl.ds(CONV_HALO - (SSM_CONV - 1) + k, tr), :])
        dx_ref[...] = acc.astype(MXU_DTYPE)
        db_ref[...] += _colsum(d)

    return pl.pallas_call(
        body, name=name, grid=(nblk,),
        out_shape=(jax.ShapeDtypeStruct((rows, CONV_DIM), MXU_DTYPE), jax.ShapeDtypeStruct((8, CONV_DIM), F32),
                   jax.ShapeDtypeStruct((1, CONV_DIM), F32)),
        in_specs=[_row_spec(CONV_DIM, tr),
                  pl.BlockSpec((CONV_HALO, CONV_DIM), lambda i: (jnp.minimum((i + 1) * hb, rows // CONV_HALO - 1), 0)),
                  pl.BlockSpec((tr, CONV_DIM), lambda i: (i, 0)),
                  pl.BlockSpec((CONV_HALO, CONV_DIM), lambda i: (jnp.maximum(i * hb - 1, 0), 0)),
                  _vec_spec(CONV_DIM, 8)],
        out_specs=(_row_spec(CONV_DIM, tr), _vec_spec(CONV_DIM, 8), _vec_spec(CONV_DIM)),
        scratch_shapes=[pltpu.VMEM((tr + CONV_HALO, CONV_DIM), F32), pltpu.VMEM((tr + CONV_HALO, CONV_DIM), F32)],
        compiler_params=_cparams(("arbitrary",)),
    )(dpre, dpre, proj, proj, conv_w8)


N_PAIRS = SSM_HEADS // 2


def _ssd_common(dtr, dtb, alog):
    row = lax.broadcasted_iota(jnp.int32, (CHUNK, CHUNK), 0)
    lane = lax.broadcasted_iota(jnp.int32, (CHUNK, CHUNK), 1)
    tril = lane <= row
    dt = _softplus(dtr + dtb)
    a = -jnp.exp(alog)
    acum = _dot_exact(tril.astype(F32), dt * a)
    return row, lane, tril, dt, a, acum, acum.T


def _pair_select(lo, mat, ha):
    return jnp.where(lo, mat[:, ha:ha + 1], mat[:, ha + 1:ha + 2])


def _ssd_fwd(pre, dtr, proj, dtb, alog, dsk, gn, *, name):
    rows = pre.shape[0]
    nc = rows // CHUNK

    def body(pre_ref, dtr_ref, z_ref, dtb_ref, alog_ref, dsk_ref, gn_ref, yb_ref, y_ref, st_ref, s_scr):
        @pl.when(pl.program_id(0) == 0)
        def _():
            s_scr[...] = jnp.zeros_like(s_scr)

        row, lane, tril, dt, a, acum, acum_t = _ssd_common(dtr_ref[...], dtb_ref[...], alog_ref[...])
        lo = lane < 64
        eacum = jnp.exp(acum)
        a_end = acum[CHUNK - 1:CHUNK, :]
        e_end = jnp.exp(a_end)
        dte_all = jnp.exp(a_end - acum)
        dsk_v = dsk_ref[...]
        for g in range(SSM_GROUPS):
            b_g = _silu(pre_ref[:, 1024 + SSM_STATE * g:1024 + SSM_STATE * (g + 1)]).astype(MXU_DTYPE)
            c_g = _silu(pre_ref[:, 1536 + SSM_STATE * g:1536 + SSM_STATE * (g + 1)]).astype(MXU_DTYPE)
            cb = _dot_nt(c_g, b_g)
            gated = []
            for jj in range(2):
                j = 2 * g + jj
                ha = 2 * j
                cs = slice(LANES * j, LANES * (j + 1))
                xs = _silu(pre_ref[:, cs])
                xdt = xs * _pair_select(lo, dt, ha)
                xdt_m = xdt.astype(MXU_DTYPE)
                y_heads = []
                for h in (ha, ha + 1):
                    dec = jnp.exp(jnp.where(tril, acum[:, h:h + 1] - acum_t[h:h + 1, :], -jnp.inf))
                    y_heads.append(_dot(cb * dec, xdt_m))
                s_prev = s_scr[j]
                st_ref[0, j] = s_prev
                y = jnp.where(lo, y_heads[0], y_heads[1])
                y = y + _dot_nt(c_g, s_prev) * _pair_select(lo, eacum, ha)
                y = y + _pair_select(lo, dsk_v, ha) * xs
                xw = xdt * _pair_select(lo, dte_all, ha)
                e_rows = jnp.where(row < 64, e_end[:, ha:ha + 1], e_end[:, ha + 1:ha + 2])
                s_scr[j] = e_rows * s_prev + _dot(xw.T, b_g)
                y_ref[:, cs] = y
                gated.append(y * _silu(z_ref[:, cs]))
            ms = (jnp.sum(gated[0] * gated[0], axis=1, keepdims=True)
                  + jnp.sum(gated[1] * gated[1], axis=1, keepdims=True)) * (1.0 / 256.0)
            r = lax.rsqrt(ms + EPS)
            for jj in range(2):
                cs = slice(LANES * (2 * g + jj), LANES * (2 * g + jj + 1))
                yb_ref[:, cs] = (gated[jj] * r * gn_ref[:, cs]).astype(MXU_DTYPE)

    return pl.pallas_call(
        body, name=name, grid=(nc,),
        out_shape=(jax.ShapeDtypeStruct((rows, D_MODEL), MXU_DTYPE), jax.ShapeDtypeStruct((rows, D_MODEL), F32),
                   jax.ShapeDtypeStruct((nc, N_PAIRS, LANES, SSM_STATE), F32)),
        in_specs=[_row_spec(CONV_DIM, CHUNK), _row_spec(LANES, CHUNK), pl.BlockSpec((CHUNK, D_MODEL), lambda i: (i, 4)),
                  _vec_spec(LANES), _vec_spec(LANES), _vec_spec(LANES), _vec_spec(D_MODEL)],
        out_specs=(_row_spec(D_MODEL, CHUNK), _row_spec(D_MODEL, CHUNK),
                   pl.BlockSpec((1, N_PAIRS, LANES, SSM_STATE), lambda i: (i, 0, 0, 0))),
        scratch_shapes=[pltpu.VMEM((N_PAIRS, LANES, SSM_STATE), F32)],
        compiler_params=_cparams(("arbitrary",)),
    )(pre, dtr, proj, dtb, alog, dsk, gn)


def _ssd_bwd(pre, dtr, proj, y_saved, states, dcat, dtb, alog, dsk, gn, *, name):
    rows = pre.shape[0]
    nc = rows // CHUNK

    def rev(i):
        return nc - 1 - i

    def body(pre_ref, dtr_ref, z_ref, y_ref, st_ref, dyb_ref, dtb_ref, alog_ref, dsk_ref, gn_ref,
             dpre_ref, dz_ref, ddtr_ref, dgn_ref, dvec_ref, g_scr):
        @pl.when(pl.program_id(0) == 0)
        def _():
            g_scr[...] = jnp.zeros_like(g_scr)
            dgn_ref[...] = jnp.zeros_like(dgn_ref)
            dvec_ref[...] = jnp.zeros_like(dvec_ref)

        dtb = dtb_ref[...]
        dtr = dtr_ref[...]
        row, lane, tril, dt, a, acum, acum_t = _ssd_common(dtr, dtb, alog_ref[...])
        lo = lane < 64
        eacum = jnp.exp(acum)
        a_end = acum[CHUNK - 1:CHUNK, :]
        e_end = jnp.exp(a_end)
        dte_all = jnp.exp(a_end - acum)
        dsk_v = dsk_ref[...]
        zero = jnp.zeros((CHUNK, LANES), F32)
        dacum_c, dacum_r, ddt_c = zero, zero, zero
        d_aend = jnp.zeros((1, LANES), F32)
        d_dsk = jnp.zeros((1, LANES), F32)
        lane1 = lane[0:1, :]

        def put_col(acc, h, colvec):
            return acc + jnp.where(lane == h, colvec, 0.0)

        for g in range(SSM_GROUPS):
            gated, sz, dgh = [], [], []
            for jj in range(2):
                cs = slice(LANES * (2 * g + jj), LANES * (2 * g + jj + 1))
                gated.append(y_ref[:, cs] * _silu(z_ref[:, cs]))
                dgh.append(dyb_ref[:, cs] * gn_ref[:, cs])
            ms = (jnp.sum(gated[0] * gated[0], axis=1, keepdims=True)
                  + jnp.sum(gated[1] * gated[1], axis=1, keepdims=True)) * (1.0 / 256.0)
            r = lax.rsqrt(ms + EPS)
            proj_g = (jnp.sum(dgh[0] * gated[0], axis=1, keepdims=True)
                      + jnp.sum(dgh[1] * gated[1], axis=1, keepdims=True)) * (1.0 / 256.0)
            dys = []
            for jj in range(2):
                cs = slice(LANES * (2 * g + jj), LANES * (2 * g + jj + 1))
                dgn_ref[:, cs] += _colsum(dyb_ref[:, cs] * gated[jj] * r)
                dgated = r * dgh[jj] - gated[jj] * (r * r * r * proj_g)
                zz = z_ref[:, cs]
                dys.append(dgated * _silu(zz))
                dz_ref[:, cs] = (dgated * y_ref[:, cs] * _silu_grad(zz)).astype(MXU_DTYPE)

            pre_b = pre_ref[:, 1024 + SSM_STATE * g:1024 + SSM_STATE * (g + 1)]
            pre_c = pre_ref[:, 1536 + SSM_STATE * g:1536 + SSM_STATE * (g + 1)]
            b_g = _silu(pre_b).astype(MXU_DTYPE)
            c_g = _silu(pre_c).astype(MXU_DTYPE)
            cb = _dot_nt(c_g, b_g)
            dcb = zero
            db_g, dc_g = zero, zero
            for jj in range(2):
                j = 2 * g + jj
                ha = 2 * j
                cs = slice(LANES * j, LANES * (j + 1))
                pre_x = pre_ref[:, cs]
                xs = _silu(pre_x)
                dtsel = _pair_select(lo, dt, ha)
                xdt = xs * dtsel
                xdt_m = xdt.astype(MXU_DTYPE)
                dyp = dys[jj]
                dyp_m = dyp.astype(MXU_DTYPE)
                s_prev = st_ref[0, j]
                g_next = g_scr[j]
                eac = _pair_select(lo, eacum, ha)
                dte = _pair_select(lo, dte_all, ha)
                yoff = _dot_nt(c_g, s_prev) * eac
                t_off = dyp * yoff
                dye = dyp * eac
                dc_g = dc_g + _dot(dye, s_prev)
                bg = _dot_nt(b_g, g_next)
                dxdt = bg * dte
                xw = xdt * dte
                db_g = db_g + _dot(xw, g_next)
                t_w = xw * bg
                gs = g_next * s_prev
                e_rows = jnp.where(row < 64, e_end[:, ha:ha + 1], e_end[:, ha + 1:ha + 2])
                g_scr[j] = e_rows * g_next + _dot(dye.T, c_g)
                dxdt_heads = []
                for hh, h in enumerate((ha, ha + 1)):
                    half = slice(64 * hh, 64 * (hh + 1))
                    dec = jnp.exp(jnp.where(tril, acum[:, h:h + 1] - acum_t[h:h + 1, :], -jnp.inf))
                    m_h = cb * dec
                    dy_h = jnp.where(lo if hh == 0 else jnp.logical_not(lo), dyp, 0.0)
                    dm = _dot_nt(dy_h, xdt_m)
                    dxdt_heads.append(_dot(m_h.T, dyp_m))
                    e_h = dm * m_h
                    dcb = dcb + dm * dec
                    w_col = jnp.sum(t_w[:, half], axis=1, keepdims=True)
                    col = (jnp.sum(e_h, axis=1, keepdims=True) + jnp.sum(t_off[:, half], axis=1, keepdims=True) - w_col)
                    dacum_c = put_col(dacum_c, h, col)
                    dacum_r = dacum_r + jnp.where(row == h, _colsum(e_h), 0.0)
                    d_end_h = jnp.sum(w_col, keepdims=True) + e_end[:, h:h + 1] * jnp.sum(gs[half, :], keepdims=True)
                    d_aend = d_aend + jnp.where(lane1 == h, d_end_h, 0.0)
                dxdt = dxdt + jnp.where(lo, dxdt_heads[0], dxdt_heads[1])
                dsel = _pair_select(lo, dsk_v, ha)
                dxs = dxdt * dtsel + dsel * dyp
                dpre_ref[:, cs] = dxs * _silu_grad(pre_x)
                t_dt = dxdt * xs
                t_dk = dyp * xs
                for hh, h in enumerate((ha, ha + 1)):
                    half = slice(64 * hh, 64 * (hh + 1))
                    ddt_c = put_col(ddt_c, h, jnp.sum(t_dt[:, half], axis=1, keepdims=True))
                    d_dsk = d_dsk + jnp.where(lane1 == h, jnp.sum(t_dk[:, half], keepdims=True), 0.0)
            dc_g = dc_g + _dot(dcb, b_g)
            db_g = db_g + _dot(dcb.T, c_g)
            dpre_ref[:, 1024 + SSM_STATE * g:1024 + SSM_STATE * (g + 1)] = db_g * _silu_grad(pre_b)
            dpre_ref[:, 1536 + SSM_STATE * g:1536 + SSM_STATE * (g + 1)] = dc_g * _silu_grad(pre_c)

        dacum = dacum_c - dacum_r.T + jnp.where(row == CHUNK - 1, d_aend, 0.0)
        dda = _dot_exact((lane >= row).astype(F32), dacum)
        ddt = dda * a + ddt_c
        ddtr = ddt * _sigmoid(dtr + dtb)
        ddtr_ref[...] = ddtr.astype(MXU_DTYPE)
        dvec_ref[0:1, :] += _colsum(ddtr)
        dvec_ref[1:2, :] += _colsum(dda * dt)
        dvec_ref[2:3, :] += d_dsk

    return pl.pallas_call(
        body, name=name, grid=(nc,),
        out_shape=(jax.ShapeDtypeStruct((rows, CONV_DIM), F32), jax.ShapeDtypeStruct((rows, D_MODEL), MXU_DTYPE),
                   jax.ShapeDtypeStruct((rows, LANES), MXU_DTYPE), jax.ShapeDtypeStruct((1, D_MODEL), F32),
                   jax.ShapeDtypeStruct((8, LANES), F32)),
        in_specs=[pl.BlockSpec((CHUNK, CONV_DIM), lambda i: (rev(i), 0)), pl.BlockSpec((CHUNK, LANES), lambda i: (rev(i), 0)),
                  pl.BlockSpec((CHUNK, D_MODEL), lambda i: (rev(i), 4)), pl.BlockSpec((CHUNK, D_MODEL), lambda i: (rev(i), 0)),
                  pl.BlockSpec((1, N_PAIRS, LANES, SSM_STATE), lambda i: (rev(i), 0, 0, 0)),
                  pl.BlockSpec((CHUNK, D_MODEL), lambda i: (rev(i), 1)),
                  _vec_spec(LANES), _vec_spec(LANES), _vec_spec(LANES), _vec_spec(D_MODEL)],
        out_specs=(pl.BlockSpec((CHUNK, CONV_DIM), lambda i: (rev(i), 0)), pl.BlockSpec((CHUNK, D_MODEL), lambda i: (rev(i), 0)),
                   pl.BlockSpec((CHUNK, LANES), lambda i: (rev(i), 0)), _vec_spec(D_MODEL), _vec_spec(LANES, 8)),
        scratch_shapes=[pltpu.VMEM((N_PAIRS, LANES, SSM_STATE), F32)],
        compiler_params=_cparams(("arbitrary",)),
    )(pre, dtr, proj, y_saved, states, dcat, dtb, alog, dsk, gn)


def _pool_counts(first_row, n_rows, win):
    t = first_row + lax.broadcasted_iota(jnp.int32, (n_rows, POOL_DIM), 0)
    return jnp.minimum(t + 1, win).astype(F32)


def _pool_fwd(yn, pool_w, pool_b, pool_scale, *, name):
    rows = yn.shape[0]
    tr = ROW_TILE
    hb = tr // POOL_HALO

    def body(y_ref, prev_ref, w_ref, b_ref, s_ref, pm_ref, diff_ref, buf):
        i = pl.program_id(0)
        buf[pl.ds(0, POOL_HALO), :] = jnp.where(i == 0, 0.0, prev_ref[...])
        buf[pl.ds(POOL_HALO, tr), :] = y_ref[...]
        for g, win in enumerate(POOL_WINDOWS):
            cs = slice(POOL_DIM * g, POOL_DIM * (g + 1))
            acc = buf[pl.ds(POOL_HALO, tr), cs]
            for s in range(1, win):
                acc = acc + buf[pl.ds(POOL_HALO - s, tr), cs]
            diff = (acc / _pool_counts(i * tr, tr, win) - y_ref[:, cs]).astype(MXU_DTYPE)
            diff_ref[:, cs] = diff
            pm_ref[:, cs] = (_dot(diff, w_ref[g]) + b_ref[:, cs]) * s_ref[:, cs]

    return pl.pallas_call(
        body, name=name, grid=(rows // tr,),
        out_shape=(jax.ShapeDtypeStruct((rows, D_MODEL), F32), jax.ShapeDtypeStruct((rows, D_MODEL), MXU_DTYPE)),
        in_specs=[_row_spec(D_MODEL, tr),
                  pl.BlockSpec((POOL_HALO, D_MODEL), lambda i: (jnp.maximum(i * hb - 1, 0), 0)),
                  pl.BlockSpec((4, POOL_DIM, POOL_DIM), lambda i: (0, 0, 0)), _vec_spec(D_MODEL), _vec_spec(D_MODEL)],
        out_specs=(_row_spec(D_MODEL, tr), _row_spec(D_MODEL, tr)),
        scratch_shapes=[pltpu.VMEM((tr + POOL_HALO, D_MODEL), F32)],
        compiler_params=_cparams(("parallel",)),
    )(yn, yn, pool_w, pool_b, pool_scale)


def _pool_bwd(dpm, diff, pool_w, pool_w_t, pool_b, pool_scale, *, name):
    rows = dpm.shape[0]
    tr = ROW_TILE
    hb = tr // POOL_HALO
    nblk = rows // tr

    def body(d_ref, dnext_ref, diff_ref, w_ref, wt_ref, b_ref, s_ref, dy_ref, dw_ref, db_ref, ds_ref, ebuf):
        i = pl.program_id(0)

        @pl.when(i == 0)
        def _():
            dw_ref[...] = jnp.zeros_like(dw_ref)
            db_ref[...] = jnp.zeros_like(db_ref)
            ds_ref[...] = jnp.zeros_like(ds_ref)

        last = i == nblk - 1
        for g, win in enumerate(POOL_WINDOWS):
            cs = slice(POOL_DIM * g, POOL_DIM * (g + 1))
            d = d_ref[:, cs]
            diff = diff_ref[:, cs]
            out_pre = _dot(diff, w_ref[g]) + b_ref[:, cs]
            ds_ref[:, cs] += _colsum(d * out_pre)
            dout = d * s_ref[:, cs]
            db_ref[:, cs] += _colsum(dout)
            dw_ref[g] += _dot_tn(diff, dout)
            ddiff = _dot(dout, wt_ref[g])
            ddiff_next = _dot(jnp.where(last, 0.0, dnext_ref[:, cs]) * s_ref[:, cs], wt_ref[g])
            ebuf[pl.ds(0, tr), cs] = ddiff / _pool_counts(i * tr, tr, win)
            ebuf[pl.ds(tr, POOL_HALO), cs] = ddiff_next / _pool_counts((i + 1) * tr, POOL_HALO, win)
            acc = -ddiff
            for s in range(win):
                acc = acc + ebuf[pl.ds(s, tr), cs]
            dy_ref[:, cs] = acc

    return pl.pallas_call(
        body, name=name, grid=(nblk,),
        out_shape=(jax.ShapeDtypeStruct((rows, D_MODEL), F32), jax.ShapeDtypeStruct((4, POOL_DIM, POOL_DIM), F32),
                   jax.ShapeDtypeStruct((1, D_MODEL), F32), jax.ShapeDtypeStruct((1, D_MODEL), F32)),
        in_specs=[_row_spec(D_MODEL, tr),
                  pl.BlockSpec((POOL_HALO, D_MODEL), lambda i: (jnp.minimum((i + 1) * hb, rows // POOL_HALO - 1), 0)),
                  _row_spec(D_MODEL, tr),
                  pl.BlockSpec((4, POOL_DIM, POOL_DIM), lambda i: (0, 0, 0)),
                  pl.BlockSpec((4, POOL_DIM, POOL_DIM), lambda i: (0, 0, 0)), _vec_spec(D_MODEL), _vec_spec(D_MODEL)],
        out_specs=(_row_spec(D_MODEL, tr), pl.BlockSpec((4, POOL_DIM, POOL_DIM), lambda i: (0, 0, 0)),
                   _vec_spec(D_MODEL), _vec_spec(D_MODEL)),
        scratch_shapes=[pltpu.VMEM((tr + POOL_HALO, D_MODEL), F32)],
        compiler_params=_cparams(("arbitrary",)),
    )(dpm, dpm, diff, pool_w, pool_w_t, pool_b, pool_scale)


def _adamw(recv, w, m, v, *, name):
    rows = w.shape[0]
    tr = PACK_TILE
    c1 = 1.0 / (1.0 - ADAM_B1 ** ADAM_STEP)
    c2 = 1.0 / (1.0 - ADAM_B2 ** ADAM_STEP)

    def body(r_ref, w_ref, m_ref, v_ref, g_ref, d_ref, mo_ref, vo_ref):
        g = r_ref[0].astype(F32)
        for j in range(1, N_DEV):
            g = g + r_ref[j].astype(F32)
        m_new = ADAM_B1 * m_ref[...] + (1.0 - ADAM_B1) * g
        v_new = ADAM_B2 * v_ref[...] + (1.0 - ADAM_B2) * (g * g)
        g_ref[...] = g
        mo_ref[...] = m_new
        vo_ref[...] = v_new
        d_ref[...] = -ADAM_LR * ((m_new * c1) / (jnp.sqrt(v_new * c2) + ADAM_EPS) + ADAM_WD * w_ref[...])

    spec = pl.BlockSpec((tr, LANES), lambda i: (i, 0))
    return pl.pallas_call(
        body, name=name, grid=(rows // tr,),
        out_shape=tuple(jax.ShapeDtypeStruct((rows, LANES), F32) for _ in range(4)),
        in_specs=[pl.BlockSpec((N_DEV, tr, LANES), lambda i: (0, i, 0)), spec, spec, spec],
        out_specs=(spec, spec, spec, spec),
        compiler_params=_cparams(("parallel",)),
    )(recv, w, m, v)


def _pad_rows(flat, mult):
    n = flat.shape[-1]
    pad = (-n) % mult
    if pad:
        flat = jnp.pad(flat, [(0, 0)] * (flat.ndim - 1) + [(0, pad)])
    return flat


def _pack_blocks(blocks, row_mult):
    flat = jnp.concatenate([_pad_rows(b.reshape(-1), LANES) for b in blocks])
    return _pad_rows(flat, LANES * row_mult).reshape(-1, LANES)


def _block_sizes(blocks):
    return [-(-math.prod(b.shape) // LANES) * LANES for b in blocks]


def _unpack_blocks(slab, like, lead=()):
    flat = slab.reshape(lead + (-1,))
    out, off = [], 0
    for b, size in zip(like, _block_sizes(like)):
        n = math.prod(b.shape)
        out.append(flat[..., off:off + n].reshape(lead + tuple(b.shape)))
        off += size
    return out


def _join_shards(gathered, axis):
    return jnp.concatenate([gathered[j] for j in range(N_DEV)], axis=axis)


def _split_shards(full, axis):
    return jnp.stack(jnp.split(full, N_DEV, axis=axis))


def _interleave_ff(w_gate, w_up):
    k = w_gate.shape[0]
    nt = D_FF // FF_TILE
    return jnp.stack([w_gate.reshape(k, nt, FF_TILE), w_up.reshape(k, nt, FF_TILE)], axis=2).reshape(k, 2 * D_FF)


def _deinterleave_ff(w_gu):
    k = w_gu.shape[0]
    nt = D_FF // FF_TILE
    t = w_gu.reshape(k, nt, 2, FF_TILE)
    return t[:, :, 0].reshape(k, D_FF), t[:, :, 1].reshape(k, D_FF)


def _row128(vec):
    return jnp.pad(vec.reshape(1, -1), ((0, 0), (0, LANES - vec.shape[-1])))


def kernel(x, norm_g, w_in, gm_ln_g, gm_ln_b, gm_ws, gm_bs, conv_w, conv_b, dt_bias, a_log, d_skip, ssm_norm_g, w_out, pool_w, pool_b, pool_scale, ffn_w_gate, ffn_w_up, ffn_w_down, loss_target, m_norm_g, m_w_in, m_gm_ln_g, m_gm_ln_b, m_gm_ws, m_gm_bs, m_conv_w, m_conv_b, m_dt_bias, m_a_log, m_d_skip, m_ssm_norm_g, m_w_out, m_pool_w, m_pool_b, m_pool_scale, m_ffn_w_gate, m_ffn_w_up, m_ffn_w_down, v_norm_g, v_w_in, v_gm_ln_g, v_gm_ln_b, v_gm_ws, v_gm_bs, v_conv_w, v_conv_b, v_dt_bias, v_a_log, v_d_skip, v_ssm_norm_g, v_w_out, v_pool_w, v_pool_b, v_pool_scale, v_ffn_w_gate, v_ffn_w_up, v_ffn_w_down):
    w_loc = dict(norm_g=norm_g, w_in=w_in, gm_ln_g=gm_ln_g, gm_ln_b=gm_ln_b, gm_ws=gm_ws, gm_bs=gm_bs, conv_w=conv_w,
                 conv_b=conv_b, dt_bias=dt_bias, a_log=a_log, d_skip=d_skip, ssm_norm_g=ssm_norm_g, w_out=w_out,
                 pool_w=pool_w, pool_b=pool_b, pool_scale=pool_scale, ffn_w_gate=ffn_w_gate, ffn_w_up=ffn_w_up,
                 ffn_w_down=ffn_w_down)
    m_loc = dict(zip(WEIGHTS, [m_norm_g, m_w_in, m_gm_ln_g, m_gm_ln_b, m_gm_ws, m_gm_bs, m_conv_w, m_conv_b, m_dt_bias,
                               m_a_log, m_d_skip, m_ssm_norm_g, m_w_out, m_pool_w, m_pool_b, m_pool_scale,
                               m_ffn_w_gate, m_ffn_w_up, m_ffn_w_down]))
    v_loc = dict(zip(WEIGHTS, [v_norm_g, v_w_in, v_gm_ln_g, v_gm_ln_b, v_gm_ws, v_gm_bs, v_conv_w, v_conv_b, v_dt_bias,
                               v_a_log, v_d_skip, v_ssm_norm_g, v_w_out, v_pool_w, v_pool_b, v_pool_scale,
                               v_ffn_w_gate, v_ffn_w_up, v_ffn_w_down]))

    big_blocks = [w_loc[n].astype(MXU_DTYPE) for n in GATHER_BF16]
    small_blocks = [w_loc[n] for n in GATHER_F32]
    got_big = _exchange(_pack_blocks(big_blocks, 16), name="gather_weights_bf16", gather=True)
    got_small = _exchange(_pack_blocks(small_blocks, 8), name="gather_weights_f32", gather=True)
    full = {}
    for n, g in zip(GATHER_BF16, _unpack_blocks(got_big, big_blocks, (N_DEV,))):
        full[n] = _join_shards(g, SHARD_AXIS[n])
    for n, g in zip(GATHER_F32, _unpack_blocks(got_small, small_blocks, (N_DEV,))):
        full[n] = _join_shards(g, SHARD_AXIS[n])

    for n in WEIGHTS:
        if SHARD_AXIS[n] is None:
            full[n] = w_loc[n]
    loss_part, grad_x, grads = _local_step(x[0], loss_target[0], full)

    like = [w_loc[n] for n in WEIGHTS]
    slots = []
    for n in WEIGHTS:
        ax = SHARD_AXIS[n]
        g = grads[n].astype(F32)
        sh = _split_shards(g, ax) if ax is not None else jnp.broadcast_to(g[None], (N_DEV,) + g.shape)
        slots.append(_pad_rows(sh.reshape(N_DEV, -1), LANES))
    send = _pad_rows(jnp.concatenate(slots, axis=1), LANES * PACK_TILE).reshape(N_DEV, -1, LANES)
    recv = _exchange(send, name="exchange_grads", gather=False)

    pk = lambda d: _pack_blocks([d[n] for n in WEIGHTS], PACK_TILE)
    g_sum, delta, m_new, v_new = _adamw(recv, pk(w_loc), pk(m_loc), pk(v_loc), name="adamw")
    outs = []
    for slab in (g_sum, delta, m_new, v_new):
        outs += _unpack_blocks(slab, like)

    loss = lax.psum(loss_part[0, 0], ("x", "y", "c"))
    return (loss, grad_x[None], *outs)


def _local_step(h0, tgt, full):
    gm_ln_g, gm_ln_b, gm_ws, gm_bs = full['gm_ln_g'], full['gm_ln_b'], full['gm_ws'], full['gm_bs']
    conv_b, dt_bias, a_log, d_skip, ssm_norm_g = (full['conv_b'], full['dt_bias'], full['a_log'], full['d_skip'],
                                                  full['ssm_norm_g'])
    w_in_f = full['w_in'][0]
    w_main = jnp.concatenate([w_in_f[:, 3072:5120], w_in_f[:, :3072]], axis=1)
    w_dt = jnp.pad(w_in_f[:, 5120:], ((0, 0), (0, LANES - SSM_HEADS)))
    w_out_f = full['w_out'][0]
    w_gu = [_interleave_ff(full['ffn_w_gate'][l], full['ffn_w_up'][l]) for l in range(2)]
    w_dn = [full['ffn_w_down'][l] for l in range(2)]
    pool_w_f = full['pool_w'][0]
    ng = full['norm_g']
    causal = jnp.tril(jnp.ones((CHUNK, CHUNK), bool))
    wm = jnp.where(causal[None], gm_ws[0], 0.0).astype(MXU_DTYPE)
    wm_t = jnp.swapaxes(wm, 1, 2)
    bcol = jnp.pad(gm_bs[0].T, ((0, 0), (0, LANES - GM_HEADS)))
    conv_w8 = jnp.pad(full['conv_w'][0], ((0, 8 - SSM_CONV), (0, 0)))
    dtb, alog, dsk = _row128(dt_bias[0]), _row128(a_log[0]), _row128(d_skip[0])
    pool_b_f = full['pool_b'][0].reshape(1, D_MODEL)
    pool_s_f = full['pool_scale']

    def g_(layer, i):
        return ng[layer, i].reshape(1, D_MODEL)

    yn0 =_rn_fwd(h0, g_(0, 0), name="rn_fwd_0", out_dtype=MXU_DTYPE)
    proj = _mm(yn0, w_main, name="mm_in_proj")
    dtr = _mm(yn0, w_dt, name="mm_in_proj_dt")
    pre = _conv_fwd(proj, conv_w8, conv_b, name="conv_fwd")
    ya = _gmlp_fwd(proj, gm_ln_g, gm_ln_b, wm, bcol, name="gmlp_fwd")
    yb, y_ssd, states = _ssd_fwd(pre, dtr, proj, dtb, alog, dsk, ssm_norm_g, name="ssd_fwd")
    cat = jnp.concatenate([ya, yb], axis=1)
    o0 = _mm(cat, w_out_f, name="mm_out_proj", tk=1024)
    h1, yn1 = _resid_rn_fwd(h0, o0, g_(0, 1), g_(0, 2), name="resid_fwd_0a", next_dtype=MXU_DTYPE)
    gu0, act0 = _mm_swiglu(yn1, w_gu[0], name="mm_ffn0_gate_up")
    d0 = _mm(act0, w_dn[0], name="mm_ffn0_down", tk=1408)
    h2, yn2 = _resid_rn_fwd(h1, d0, g_(0, 3), g_(1, 0), name="resid_fwd_0b", next_dtype=F32)
    pm, pdiff = _pool_fwd(yn2, pool_w_f, pool_b_f, pool_s_f, name="pool_fwd")
    h3, yn3 = _resid_rn_fwd(h2, pm, g_(1, 1), g_(1, 2), name="resid_fwd_1a", next_dtype=MXU_DTYPE)
    gu1, act1 = _mm_swiglu(yn3, w_gu[1], name="mm_ffn1_gate_up")
    d1 = _mm(act1, w_dn[1], name="mm_ffn1_down", tk=1408)
    dh4, loss_part = _resid_loss(h3, d1, g_(1, 3), tgt, name="resid_loss")

    grads = {}
    dng = [[None] * 4 for _ in range(2)]

    def ffn_bwd(layer, dh, d_out, gu, act, yn, h_in):
        dd, dng[layer][3] = _resid_bwd_post(dh, d_out, g_(layer, 3), name=f"resid_bwd_post_{layer}b", out_dtype=MXU_DTYPE)
        dw_dn = _mm_tn(act, dd, name=f"mm_ffn{layer}_dw_down", tm=1408)
        dgu = _mm_dswiglu(dd, w_dn[layer].T, gu, name=f"mm_ffn{layer}_dact")
        dw_gu = _mm_tn(yn, dgu, name=f"mm_ffn{layer}_dw_gate_up")
        dyn = _mm(dgu, w_gu[layer].T, name=f"mm_ffn{layer}_dyn", tk=1408)
        dh_in, dng[layer][2] = _resid_bwd_pre(dh, [dyn], h_in, g_(layer, 2), name=f"resid_bwd_pre_{layer}b")
        return dh_in, dw_gu, dw_dn

    dh3, dw_gu1, dw_dn1 = ffn_bwd(1, dh4, d1, gu1, act1, yn3, h3)
    dpm, dng[1][1] = _resid_bwd_post(dh3, pm, g_(1, 1), name="resid_bwd_post_1a", out_dtype=F32)
    dyn2, d_pool_w, d_pool_b, d_pool_s = _pool_bwd(dpm, pdiff, pool_w_f, jnp.swapaxes(pool_w_f, 1, 2), pool_b_f, pool_s_f,
                                                   name="pool_bwd")
    dh2, dng[1][0] = _resid_bwd_pre(dh3, [dyn2], h2, g_(1, 0), name="resid_bwd_pre_1a")
    dh1, dw_gu0, dw_dn0 = ffn_bwd(0, dh2, d0, gu0, act0, yn1, h1)
    do0, dng[0][1] = _resid_bwd_post(dh1, o0, g_(0, 1), name="resid_bwd_post_0a", out_dtype=MXU_DTYPE)
    d_w_out = _mm_tn(cat, do0, name="mm_out_proj_dw")
    dcat = _mm(do0, w_out_f.T, name="mm_out_proj_dx")
    du, dv, d_wm, d_bcol, d_ln_g, d_ln_b = _gmlp_bwd(proj, dcat, gm_ln_g, gm_ln_b, wm, wm_t, bcol, name="gmlp_bwd")
    dpre, dz, ddtr, d_gn, d_vec = _ssd_bwd(pre, dtr, proj, y_ssd, states, dcat, dtb, alog, dsk, ssm_norm_g, name="ssd_bwd")
    dxbc, d_conv_w8, d_conv_b = _conv_bwd(dpre, proj, conv_w8, name="conv_bwd")
    dproj = jnp.concatenate([dxbc, du, dv, dz], axis=1)
    d_w_main = _mm_tn(yn0, dproj, name="mm_in_proj_dw")
    d_w_dt = _mm_tn(yn0, ddtr, name="mm_in_proj_dt_dw")
    dyn0 = _mm(dproj, w_main.T, name="mm_in_proj_dx", tk=1024)
    dyn0_dt = _mm(ddtr, w_dt.T, name="mm_in_proj_dt_dx")
    grad_x, dng[0][0] = _resid_bwd_pre(dh1, [dyn0, dyn0_dt], h0, g_(0, 0), name="resid_bwd_pre_0a")

    grads['norm_g'] = jnp.stack([jnp.concatenate(dng[l], axis=0) for l in range(2)])
    grads['w_in'] = jnp.concatenate([d_w_main[:, 2048:], d_w_main[:, :2048], d_w_dt[:, :SSM_HEADS]], axis=1)[None]
    grads['gm_ln_g'], grads['gm_ln_b'] = d_ln_g, d_ln_b
    grads['gm_ws'] = d_wm[None]
    grads['gm_bs'] = d_bcol[:, :GM_HEADS].T[None]
    grads['conv_w'] = d_conv_w8[None, :SSM_CONV]
    grads['conv_b'] = d_conv_b
    grads['dt_bias'] = d_vec[0:1, :SSM_HEADS]
    grads['a_log'] = d_vec[1:2, :SSM_HEADS] * (-jnp.exp(a_log))
    grads['d_skip'] = d_vec[2:3, :SSM_HEADS]
    grads['ssm_norm_g'] = d_gn
    grads['w_out'] = d_w_out[None]
    grads['pool_w'] = d_pool_w[None]
    grads['pool_b'] = d_pool_b.reshape(1, 4, POOL_DIM)
    grads['pool_scale'] = d_pool_s
    dg0, du0 = _deinterleave_ff(dw_gu0)
    dg1, du1 = _deinterleave_ff(dw_gu1)
    grads['ffn_w_gate'] = jnp.stack([dg0, dg1])
    grads['ffn_w_up'] = jnp.stack([du0, du1])
    grads['ffn_w_down'] = jnp.stack([dw_dn0, dw_dn1])
    return loss_part, grad_x, grads
```

```python
import functools
import math

import jax
import jax.numpy as jnp
from jax import lax
from jax.experimental import pallas as pl
from jax.experimental.pallas import tpu as pltpu

F32 = jnp.float32
MXU_DTYPE = jnp.bfloat16

N_DEV = 8
D_MODEL = 1024
EPS = 1e-6
GM_HEADS = 4
GM_HEAD_DIM = 256
CHUNK = 128
SSM_HEADS = 16
SSM_GROUPS = 4
SSM_STATE = 128
SSM_CONV = 4
CONV_DIM = 2048
POOL_WINDOWS = (2, 4, 8, 16)
POOL_DIM = 256
D_FF = 2816
FF_TILE = 256
IN_MAIN = 5120
LANES = 128
CONV_HALO = 8
POOL_HALO = 16
ADAM_LR, ADAM_B1, ADAM_B2, ADAM_EPS, ADAM_WD, ADAM_STEP = 0.001, 0.9, 0.999, 1e-08, 0.01, 10

VMEM_LIMIT = 56 * 1024 * 1024
ROW_TILE = 512
MM_TM = 2048

WEIGHTS = ['norm_g', 'w_in', 'gm_ln_g', 'gm_ln_b', 'gm_ws', 'gm_bs', 'conv_w', 'conv_b', 'dt_bias', 'a_log',
           'd_skip', 'ssm_norm_g', 'w_out', 'pool_w', 'pool_b', 'pool_scale', 'ffn_w_gate', 'ffn_w_up', 'ffn_w_down']
SHARD_AXIS = {'norm_g': 2, 'w_in': 2, 'gm_ln_g': None, 'gm_ln_b': None, 'gm_ws': None, 'gm_bs': None, 'conv_w': 2,
              'conv_b': None, 'dt_bias': None, 'a_log': None, 'd_skip': None, 'ssm_norm_g': None, 'w_out': 1,
              'pool_w': 2, 'pool_b': 2, 'pool_scale': 1, 'ffn_w_gate': 2, 'ffn_w_up': 2, 'ffn_w_down': 1}
GATHER_BF16 = ['w_in', 'w_out', 'pool_w', 'ffn_w_gate', 'ffn_w_up', 'ffn_w_down']
GATHER_F32 = ['norm_g', 'conv_w', 'pool_b', 'pool_scale']
BIG_WEIGHTS = ['w_in', 'w_out', 'ffn_w_gate', 'ffn_w_up', 'ffn_w_down']


def _cparams(sem=None):
    return pltpu.CompilerParams(dimension_semantics=sem, vmem_limit_bytes=VMEM_LIMIT)


def _dot(a, b):
    return jnp.dot(a.astype(MXU_DTYPE), b.astype(MXU_DTYPE), preferred_element_type=F32)


def _dot_nt(a, b):
    return lax.dot_general(a.astype(MXU_DTYPE), b.astype(MXU_DTYPE), (((1,), (1,)), ((), ())),
                           preferred_element_type=F32)


def _dot_tn(a, b):
    return lax.dot_general(a.astype(MXU_DTYPE), b.astype(MXU_DTYPE), (((0,), (0,)), ((), ())),
                           preferred_element_type=F32)


def _dot_exact(a, b):
    return jnp.dot(a, b, precision=lax.Precision.HIGHEST, preferred_element_type=F32)


def _sigmoid(x):
    return 1.0 / (1.0 + jnp.exp(-x))


def _silu(x):
    return x * _sigmoid(x)


def _silu_grad(x):
    s = _sigmoid(x)
    return s * (1.0 + x * (1.0 - s))


_GELU_C = math.sqrt(2.0 / math.pi)


def _gelu(x):
    return 0.5 * x * (1.0 + jnp.tanh(_GELU_C * (x + 0.044715 * x * x * x)))


def _gelu_grad(x):
    t = jnp.tanh(_GELU_C * (x + 0.044715 * x * x * x))
    return 0.5 * (1.0 + t) + 0.5 * x * (1.0 - t * t) * _GELU_C * (1.0 + 3.0 * 0.044715 * x * x)


def _softplus(x):
    return jnp.maximum(x, 0.0) + jnp.log1p(jnp.exp(-jnp.abs(x)))


def _rms_scale(x):
    return lax.rsqrt(jnp.mean(x * x, axis=-1, keepdims=True) + EPS)


def _rms_bwd(dy, x, g):
    r = _rms_scale(x)
    xn = x * r
    dxn = dy * g
    dx = r * (dxn - xn * jnp.mean(dxn * xn, axis=-1, keepdims=True))
    return dx, dy * xn


def _colsum(x):
    return jnp.sum(x, axis=0, keepdims=True)


def _exchange(arrays, modes, *, name):
    n_arr = len(arrays)
    blks = []
    for x, mode in zip(arrays, modes):
        if mode == 'gather':
            blks.append(tuple(x.shape))
        elif mode == 'slots':
            blks.append(tuple(x.shape[1:]))
        else:
            blks.append((x.shape[0] // N_DEV,) + tuple(x.shape[1:]))

    def body(*refs):
        x_refs, out_refs = refs[:n_arr], refs[n_arr:2 * n_arr]
        send_sems, recv_sems, local_sems = refs[2 * n_arr:]
        mx, my, mc = lax.axis_index("x"), lax.axis_index("y"), lax.axis_index("c")
        me = 4 * mx + 2 * my + mc

        def flip(v, bit):
            return 1 - v if bit else v

        def part(a, dev):
            if modes[a] == 'gather':
                return x_refs[a]
            if modes[a] == 'slots':
                return x_refs[a].at[dev]
            r = blks[a][0]
            return x_refs[a].at[pl.ds(pl.multiple_of(dev * r, 16), r)]

        sends, recvs, owns = [], [], []
        for k in (1, 2, 4, 6, 3, 5, 7):
            px, py, pc = flip(mx, (k >> 2) & 1), flip(my, (k >> 1) & 1), flip(mc, k & 1)
            peer = 4 * px + 2 * py + pc
            for a in range(n_arr):
                sem = a * (N_DEV - 1) + k - 1
                sends.append(pltpu.make_async_remote_copy(
                    src_ref=part(a, peer), dst_ref=out_refs[a].at[me], send_sem=send_sems.at[sem],
                    recv_sem=recv_sems.at[sem], device_id=(px, py, pc), device_id_type=pl.DeviceIdType.MESH))
                recvs.append(pltpu.make_async_remote_copy(
                    src_ref=part(a, peer), dst_ref=out_refs[a].at[peer], send_sem=send_sems.at[sem],
                    recv_sem=recv_sems.at[sem], device_id=(px, py, pc), device_id_type=pl.DeviceIdType.MESH))
        for a in range(n_arr):
            owns.append(pltpu.make_async_copy(part(a, me), out_refs[a].at[me], local_sems.at[a]))
        for cp in sends + owns:
            cp.start()
        for cp in recvs:
            cp.wait_recv()
        for cp in sends:
            cp.wait_send()
        for cp in owns:
            cp.wait()

    n_sem = n_arr * (N_DEV - 1)
    return pl.pallas_call(
        body, name=name,
        out_shape=tuple(jax.ShapeDtypeStruct((N_DEV,) + blk, x.dtype) for x, blk in zip(arrays, blks)),
        in_specs=[pl.BlockSpec(memory_space=pl.ANY)] * n_arr,
        out_specs=tuple(pl.BlockSpec(memory_space=pl.ANY) for _ in range(n_arr)),
        scratch_shapes=[pltpu.SemaphoreType.DMA((n_sem,)), pltpu.SemaphoreType.DMA((n_sem,)),
                        pltpu.SemaphoreType.DMA((n_arr,))],
    )(*arrays)


def _mm(a, b, *, name, out_dtype=F32, tm=MM_TM, tn=512, tk=None):
    m, k = a.shape
    n = b.shape[1]
    tm, tn = min(tm, m), min(tn, n)
    tk = k if tk is None else tk
    nk = k // tk
    assert m % tm == 0 and n % tn == 0 and k % tk == 0

    def body(a_ref, b_ref, o_ref, acc_ref):
        kk = pl.program_id(2)
        part = _dot(a_ref[...], b_ref[...])
        if nk == 1:
            o_ref[...] = part.astype(out_dtype)
        else:
            @pl.when(kk == 0)
            def _():
                acc_ref[...] = part

            @pl.when(kk > 0)
            def _():
                acc_ref[...] += part

            @pl.when(kk == nk - 1)
            def _():
                o_ref[...] = acc_ref[...].astype(out_dtype)

    return pl.pallas_call(
        body, name=name, grid=(m // tm, n // tn, nk),
        out_shape=jax.ShapeDtypeStruct((m, n), out_dtype),
        in_specs=[pl.BlockSpec((tm, tk), lambda i, j, kk: (i, kk)), pl.BlockSpec((tk, tn), lambda i, j, kk: (kk, j))],
        out_specs=pl.BlockSpec((tm, tn), lambda i, j, kk: (i, j)),
        scratch_shapes=[pltpu.VMEM((tm, tn) if nk > 1 else (8, LANES), F32)],
        compiler_params=_cparams(("parallel", "parallel", "arbitrary")),
    )(a, b)


def _mm_tn(a, b, *, name, out_dtype=F32, tm=1024, tn=512, tk=1024, shift=0):
    t, m = a.shape
    n = b.shape[1]
    tm, tn, tk = min(tm, m), min(tn, n), min(tk, t)
    nk = t // tk
    nb = m // tm
    assert m % tm == 0 and n % tn == 0 and t % tk == 0

    def body(a_ref, b_ref, o_ref, acc_ref):
        kk = pl.program_id(2)
        part = _dot_tn(a_ref[...], b_ref[...])

        @pl.when(kk == 0)
        def _():
            acc_ref[...] = part

        @pl.when(kk > 0)
        def _():
            acc_ref[...] += part

        @pl.when(kk == nk - 1)
        def _():
            o_ref[...] = acc_ref[...].astype(out_dtype)

    return pl.pallas_call(
        body, name=name, grid=(nb, n // tn, nk),
        out_shape=jax.ShapeDtypeStruct((m, n), out_dtype),
        in_specs=[pl.BlockSpec((tk, tm), lambda i, j, kk: (kk, i)), pl.BlockSpec((tk, tn), lambda i, j, kk: (kk, j))],
        out_specs=pl.BlockSpec((tm, tn), lambda i, j, kk: ((i + shift) % nb, j)),
        scratch_shapes=[pltpu.VMEM((tm, tn), F32)],
        compiler_params=_cparams(("parallel", "parallel", "arbitrary")),
    )(a, b)


def _mm_tn_gate_up(dgu, yn, *, name, out_dtype, tk=2048):
    t, m = dgu.shape
    n = yn.shape[1]
    tk = min(tk, t)
    nk = t // tk
    nb = m // (2 * FF_TILE)

    def body(a_ref, b_ref, og_ref, ou_ref, acc_ref):
        kk = pl.program_id(1)
        part = _dot_tn(a_ref[...], b_ref[...])

        @pl.when(kk == 0)
        def _():
            acc_ref[...] = part

        @pl.when(kk > 0)
        def _():
            acc_ref[...] += part

        @pl.when(kk == nk - 1)
        def _():
            og_ref[...] = acc_ref[:FF_TILE, :].astype(out_dtype)
            ou_ref[...] = acc_ref[FF_TILE:, :].astype(out_dtype)

    out = jax.ShapeDtypeStruct((m // 2, n), out_dtype)
    o_spec = pl.BlockSpec((FF_TILE, n), lambda i, kk: (i, 0))
    return pl.pallas_call(
        body, name=name, grid=(nb, nk), out_shape=(out, out),
        in_specs=[pl.BlockSpec((tk, 2 * FF_TILE), lambda i, kk: (kk, i)), pl.BlockSpec((tk, n), lambda i, kk: (kk, 0))],
        out_specs=(o_spec, o_spec),
        scratch_shapes=[pltpu.VMEM((2 * FF_TILE, n), F32)],
        compiler_params=_cparams(("parallel", "arbitrary")),
    )(dgu, yn)


def _mm_swiglu(a, w_gu, *, name, tm=MM_TM):
    m, k = a.shape
    n = w_gu.shape[1]
    nt = n // (2 * FF_TILE)
    tm = min(tm, m)

    def body(a_ref, b_ref, gu_ref, act_ref):
        gu = _dot(a_ref[...], b_ref[...])
        gu_ref[...] = gu
        act_ref[...] = (_silu(gu[:, :FF_TILE]) * gu[:, FF_TILE:]).astype(MXU_DTYPE)

    return pl.pallas_call(
        body, name=name, grid=(m // tm, nt),
        out_shape=(jax.ShapeDtypeStruct((m, n), F32), jax.ShapeDtypeStruct((m, n // 2), MXU_DTYPE)),
        in_specs=[pl.BlockSpec((tm, k), lambda i, j: (i, 0)), pl.BlockSpec((k, 2 * FF_TILE), lambda i, j: (0, j))],
        out_specs=(pl.BlockSpec((tm, 2 * FF_TILE), lambda i, j: (i, j)), pl.BlockSpec((tm, FF_TILE), lambda i, j: (i, j))),
        compiler_params=_cparams(("parallel", "parallel")),
    )(a, w_gu)


def _mm_dswiglu(dd, w_down_t, gu, *, name, tm=MM_TM):
    m, k = dd.shape
    n = gu.shape[1]
    nt = n // (2 * FF_TILE)
    tm = min(tm, m)

    def body(d_ref, w_ref, gu_ref, o_ref):
        dact = _dot(d_ref[...], w_ref[...])
        gate, up = gu_ref[:, :FF_TILE], gu_ref[:, FF_TILE:]
        o_ref[:, :FF_TILE] = (dact * up * _silu_grad(gate)).astype(MXU_DTYPE)
        o_ref[:, FF_TILE:] = (dact * _silu(gate)).astype(MXU_DTYPE)

    return pl.pallas_call(
        body, name=name, grid=(m // tm, nt),
        out_shape=jax.ShapeDtypeStruct((m, n), MXU_DTYPE),
        in_specs=[pl.BlockSpec((tm, k), lambda i, j: (i, 0)), pl.BlockSpec((k, FF_TILE), lambda i, j: (0, j)),
                  pl.BlockSpec((tm, 2 * FF_TILE), lambda i, j: (i, j))],
        out_specs=pl.BlockSpec((tm, 2 * FF_TILE), lambda i, j: (i, j)),
        compiler_params=_cparams(("parallel", "parallel")),
    )(dd, w_down_t, gu)


def _row_spec(width, tr=ROW_TILE):
    return pl.BlockSpec((tr, width), lambda i: (i, 0))


def _vec_spec(width, rows=1):
    return pl.BlockSpec((rows, width), lambda i: (0, 0))


def _rn_fwd(h, g, *, name, out_dtype):
    rows, d = h.shape

    def body(h_ref, g_ref, o_ref):
        x = h_ref[...]
        o_ref[...] = (x * _rms_scale(x) * g_ref[...]).astype(out_dtype)

    return pl.pallas_call(
        body, name=name, grid=(rows // ROW_TILE,),
        out_shape=jax.ShapeDtypeStruct((rows, d), out_dtype),
        in_specs=[_row_spec(d), _vec_spec(d)], out_specs=_row_spec(d),
        compiler_params=_cparams(("parallel",)),
    )(h, g)


def _resid_rn_fwd(h_in, o, g_post, g_next, *, name, next_dtype):
    rows, d = h_in.shape

    def body(h_ref, o_ref, gp_ref, gn_ref, ho_ref, yn_ref):
        ov = o_ref[...]
        h = h_ref[...] + ov * _rms_scale(ov) * gp_ref[...]
        ho_ref[...] = h
        yn_ref[...] = (h * _rms_scale(h) * gn_ref[...]).astype(next_dtype)

    return pl.pallas_call(
        body, name=name, grid=(rows // ROW_TILE,),
        out_shape=(jax.ShapeDtypeStruct((rows, d), F32), jax.ShapeDtypeStruct((rows, d), next_dtype)),
        in_specs=[_row_spec(d), _row_spec(d), _vec_spec(d), _vec_spec(d)],
        out_specs=(_row_spec(d), _row_spec(d)),
        compiler_params=_cparams(("parallel",)),
    )(h_in, o, g_post, g_next)


def _resid_loss(h_in, o, g_post, target, *, name):
    rows, d = h_in.shape

    def body(h_ref, o_ref, gp_ref, t_ref, dh_ref, loss_ref):
        ov = o_ref[...]
        err = h_ref[...] + ov * _rms_scale(ov) * gp_ref[...] - t_ref[...]
        dh_ref[...] = err * (1.0 / d)

        @pl.when(pl.program_id(0) == 0)
        def _():
            loss_ref[...] = jnp.zeros_like(loss_ref)

        loss_ref[...] += 0.5 * jnp.sum(jnp.mean(err * err, axis=-1, keepdims=True), axis=0, keepdims=True)

    return pl.pallas_call(
        body, name=name, grid=(rows // ROW_TILE,),
        out_shape=(jax.ShapeDtypeStruct((rows, d), F32), jax.ShapeDtypeStruct((1, 1), F32)),
        in_specs=[_row_spec(d), _row_spec(d), _vec_spec(d), _row_spec(d)],
        out_specs=(_row_spec(d), pl.BlockSpec((1, 1), lambda i: (0, 0))),
        compiler_params=_cparams(("arbitrary",)),
    )(h_in, o, g_post, target)


def _resid_bwd_post(dh, o, g_post, *, name, out_dtype):
    rows, d = dh.shape

    def body(dh_ref, o_ref, g_ref, do_ref, dg_ref):
        do, dg = _rms_bwd(dh_ref[...], o_ref[...], g_ref[...])
        do_ref[...] = do.astype(out_dtype)

        @pl.when(pl.program_id(0) == 0)
        def _():
            dg_ref[...] = jnp.zeros_like(dg_ref)

        dg_ref[...] += _colsum(dg)

    return pl.pallas_call(
        body, name=name, grid=(rows // ROW_TILE,),
        out_shape=(jax.ShapeDtypeStruct((rows, d), out_dtype), jax.ShapeDtypeStruct((1, d), F32)),
        in_specs=[_row_spec(d), _row_spec(d), _vec_spec(d)],
        out_specs=(_row_spec(d), _vec_spec(d)),
        compiler_params=_cparams(("arbitrary",)),
    )(dh, o, g_post)


def _resid_bwd_pre(dh, dyn_list, h_in, g_pre, *, name):
    rows, d = dh.shape
    n_dyn = len(dyn_list)

    def body(*refs):
        dh_ref, dyn_refs, h_ref, g_ref, out_ref, dg_ref = refs[0], refs[1:1 + n_dyn], *refs[1 + n_dyn:]
        dyn = dyn_refs[0][...]
        for r in dyn_refs[1:]:
            dyn = dyn + r[...]
        dx, dg = _rms_bwd(dyn, h_ref[...], g_ref[...])
        out_ref[...] = dh_ref[...] + dx

        @pl.when(pl.program_id(0) == 0)
        def _():
            dg_ref[...] = jnp.zeros_like(dg_ref)

        dg_ref[...] += _colsum(dg)

    return pl.pallas_call(
        body, name=name, grid=(rows // ROW_TILE,),
        out_shape=(jax.ShapeDtypeStruct((rows, d), F32), jax.ShapeDtypeStruct((1, d), F32)),
        in_specs=[_row_spec(d)] + [_row_spec(d)] * n_dyn + [_row_spec(d), _vec_spec(d)],
        out_specs=(_row_spec(d), _vec_spec(d)),
        compiler_params=_cparams(("arbitrary",)),
    )(dh, *dyn_list, h_in, g_pre)


def _layer_norm_stats(x):
    mu = jnp.mean(x, axis=-1, keepdims=True)
    xc = x - mu
    rstd = lax.rsqrt(jnp.mean(xc * xc, axis=-1, keepdims=True) + EPS)
    return xc * rstd, rstd


def _gmlp_fwd(proj, ln_g, ln_b, wm, bcol, *, name):
    rows = proj.shape[0]
    tr = ROW_TILE

    def body(u_ref, v_ref, lg_ref, lb_ref, wm_ref, bc_ref, ya_ref):
        vhat, _ = _layer_norm_stats(_gelu(v_ref[...]))
        vl = (vhat * lg_ref[...] + lb_ref[...]).astype(MXU_DTYPE)
        gu = _gelu(u_ref[...])
        bc = bc_ref[...]
        for c in range(tr // CHUNK):
            rs = slice(c * CHUNK, (c + 1) * CHUNK)
            for h in range(GM_HEADS):
                cs = slice(h * GM_HEAD_DIM, (h + 1) * GM_HEAD_DIM)
                mixed = _dot(wm_ref[h], vl[rs, cs]) + bc[:, h:h + 1]
                ya_ref[rs, cs] = (gu[rs, cs] * mixed).astype(MXU_DTYPE)

    return pl.pallas_call(
        body, name=name, grid=(rows // tr,),
        out_shape=jax.ShapeDtypeStruct((rows, D_MODEL), MXU_DTYPE),
        in_specs=[pl.BlockSpec((tr, D_MODEL), lambda i: (i, 2)), pl.BlockSpec((tr, D_MODEL), lambda i: (i, 3)),
                  _vec_spec(D_MODEL), _vec_spec(D_MODEL),
                  pl.BlockSpec((GM_HEADS, CHUNK, CHUNK), lambda i: (0, 0, 0)), _vec_spec(LANES, CHUNK)],
        out_specs=_row_spec(D_MODEL, tr),
        compiler_params=_cparams(("parallel",)),
    )(proj, proj, ln_g, ln_b, wm, bcol)


def _gmlp_bwd(proj, dcat, ln_g, ln_b, wm, wm_t, bcol, *, name):
    rows = proj.shape[0]
    tr = ROW_TILE

    def body(u_ref, v_ref, dy_ref, lg_ref, lb_ref, wm_ref, wmt_ref, bc_ref,
             du_ref, dv_ref, dwm_ref, dbc_ref, dlg_ref, dlb_ref, dvl_scr):
        @pl.when(pl.program_id(0) == 0)
        def _():
            dwm_ref[...] = jnp.zeros_like(dwm_ref)
            dbc_ref[...] = jnp.zeros_like(dbc_ref)
            dlg_ref[...] = jnp.zeros_like(dlg_ref)
            dlb_ref[...] = jnp.zeros_like(dlb_ref)

        u, v = u_ref[...], v_ref[...]
        gv = _gelu(v)
        vhat, rstd = _layer_norm_stats(gv)
        lg = lg_ref[...]
        vl = (vhat * lg + lb_ref[...]).astype(MXU_DTYPE)
        gu = _gelu(u)
        dy = dy_ref[...]
        bc = bc_ref[...]
        row = lax.broadcasted_iota(jnp.int32, (CHUNK, CHUNK), 0)
        lane = lax.broadcasted_iota(jnp.int32, (CHUNK, CHUNK), 1)
        causal = lane <= row
        dbc = jnp.zeros((CHUNK, LANES), F32)
        for c in range(tr // CHUNK):
            rs = slice(c * CHUNK, (c + 1) * CHUNK)
            for h in range(GM_HEADS):
                cs = slice(h * GM_HEAD_DIM, (h + 1) * GM_HEAD_DIM)
                vl_h = vl[rs, cs]
                mixed = _dot(wm_ref[h], vl_h) + bc[:, h:h + 1]
                dy_h = dy[rs, cs]
                du_ref[rs, cs] = (dy_h * mixed * _gelu_grad(u[rs, cs])).astype(MXU_DTYPE)
                dmixed = dy_h * gu[rs, cs]
                dwm_ref[h] += jnp.where(causal, _dot_nt(dmixed, vl_h), 0.0)
                dbc = dbc + jnp.where(lane == h, jnp.sum(dmixed, axis=1, keepdims=True), 0.0)
                dvl_scr[rs, cs] = _dot(wmt_ref[h], dmixed)
        dbc_ref[...] += dbc
        dvl = dvl_scr[...]
        dlg_ref[...] += _colsum(dvl * vhat)
        dlb_ref[...] += _colsum(dvl)
        dvh = dvl * lg
        dgv = rstd * (dvh - jnp.mean(dvh, axis=-1, keepdims=True) - vhat * jnp.mean(dvh * vhat, axis=-1, keepdims=True))
        dv_ref[...] = (dgv * _gelu_grad(v)).astype(MXU_DTYPE)

    return pl.pallas_call(
        body, name=name, grid=(rows // tr,),
        out_shape=(jax.ShapeDtypeStruct((rows, D_MODEL), MXU_DTYPE), jax.ShapeDtypeStruct((rows, D_MODEL), MXU_DTYPE),
                   jax.ShapeDtypeStruct((GM_HEADS, CHUNK, CHUNK), F32), jax.ShapeDtypeStruct((CHUNK, LANES), F32),
                   jax.ShapeDtypeStruct((1, D_MODEL), F32), jax.ShapeDtypeStruct((1, D_MODEL), F32)),
        in_specs=[pl.BlockSpec((tr, D_MODEL), lambda i: (i, 2)), pl.BlockSpec((tr, D_MODEL), lambda i: (i, 3)),
                  pl.BlockSpec((tr, D_MODEL), lambda i: (i, 0)), _vec_spec(D_MODEL), _vec_spec(D_MODEL),
                  pl.BlockSpec((GM_HEADS, CHUNK, CHUNK), lambda i: (0, 0, 0)),
                  pl.BlockSpec((GM_HEADS, CHUNK, CHUNK), lambda i: (0, 0, 0)), _vec_spec(LANES, CHUNK)],
        out_specs=(_row_spec(D_MODEL, tr), _row_spec(D_MODEL, tr),
                   pl.BlockSpec((GM_HEADS, CHUNK, CHUNK), lambda i: (0, 0, 0)), _vec_spec(LANES, CHUNK),
                   _vec_spec(D_MODEL), _vec_spec(D_MODEL)),
        scratch_shapes=[pltpu.VMEM((tr, D_MODEL), F32)],
        compiler_params=_cparams(("arbitrary",)),
    )(proj, proj, dcat, ln_g, ln_b, wm, wm_t, bcol)


def _conv_fwd(proj, conv_w8, conv_b, *, name):
    rows = proj.shape[0]
    tr = ROW_TILE
    hb = tr // CONV_HALO

    def body(x_ref, prev_ref, w_ref, b_ref, pre_ref, buf):
        first = pl.program_id(0) == 0
        buf[pl.ds(0, CONV_HALO), :] = jnp.where(first, 0.0, prev_ref[...])
        buf[pl.ds(CONV_HALO, tr), :] = x_ref[...]
        acc = jnp.broadcast_to(b_ref[...], (tr, CONV_DIM))
        for k in range(SSM_CONV):
            acc = acc + w_ref[k:k + 1, :] * buf[pl.ds(CONV_HALO - (SSM_CONV - 1) + k, tr), :]
        pre_ref[...] = acc

    return pl.pallas_call(
        body, name=name, grid=(rows // tr,),
        out_shape=jax.ShapeDtypeStruct((rows, CONV_DIM), F32),
        in_specs=[pl.BlockSpec((tr, CONV_DIM), lambda i: (i, 0)),
                  pl.BlockSpec((CONV_HALO, CONV_DIM), lambda i: (jnp.maximum(i * hb - 1, 0), 0)),
                  _vec_spec(CONV_DIM, 8), _vec_spec(CONV_DIM)],
        out_specs=_row_spec(CONV_DIM, tr),
        scratch_shapes=[pltpu.VMEM((tr + CONV_HALO, CONV_DIM), F32)],
        compiler_params=_cparams(("parallel",)),
    )(proj, proj, conv_w8, conv_b)


def _conv_bwd(dpre, proj, conv_w8, *, name):
    rows = proj.shape[0]
    tr = ROW_TILE
    hb = tr // CONV_HALO
    nblk = rows // tr

    def body(d_ref, dnext_ref, x_ref, prev_ref, w_ref, dx_ref, dw_ref, db_ref, dbuf, xbuf):
        i = pl.program_id(0)

        @pl.when(i == 0)
        def _():
            dw_ref[...] = jnp.zeros_like(dw_ref)
            db_ref[...] = jnp.zeros_like(db_ref)

        d = d_ref[...]
        dbuf[pl.ds(0, tr), :] = d
        dbuf[pl.ds(tr, CONV_HALO), :] = jnp.where(i == nblk - 1, 0.0, dnext_ref[...])
        xbuf[pl.ds(0, CONV_HALO), :] = jnp.where(i == 0, 0.0, prev_ref[...])
        xbuf[pl.ds(CONV_HALO, tr), :] = x_ref[...]
        acc = jnp.zeros((tr, CONV_DIM), F32)
        for k in range(SSM_CONV):
            acc = acc + w_ref[k:k + 1, :] * dbuf[pl.ds(SSM_CONV - 1 - k, tr), :]
            dw_ref[k:k + 1, :] += _colsum(d * xbuf[pl.ds(CONV_HALO - (SSM_CONV - 1) + k, tr), :])
        dx_ref[...] = acc.astype(MXU_DTYPE)
        db_ref[...] += _colsum(d)

    return pl.pallas_call(
        body, name=name, grid=(nblk,),
        out_shape=(jax.ShapeDtypeStruct((rows, CONV_DIM), MXU_DTYPE), jax.ShapeDtypeStruct((8, CONV_DIM), F32),
                   jax.ShapeDtypeStruct((1, CONV_DIM), F32)),
        in_specs=[_row_spec(CONV_DIM, tr),
                  pl.BlockSpec((CONV_HALO, CONV_DIM), lambda i: (jnp.minimum((i + 1) * hb, rows // CONV_HALO - 1), 0)),
                  pl.BlockSpec((tr, CONV_DIM), lambda i: (i, 0)),
                  pl.BlockSpec((CONV_HALO, CONV_DIM), lambda i: (jnp.maximum(i * hb - 1, 0), 0)),
                  _vec_spec(CONV_DIM, 8)],
        out_specs=(_row_spec(CONV_DIM, tr), _vec_spec(CONV_DIM, 8), _vec_spec(CONV_DIM)),
        scratch_shapes=[pltpu.VMEM((tr + CONV_HALO, CONV_DIM), F32), pltpu.VMEM((tr + CONV_HALO, CONV_DIM), F32)],
        compiler_params=_cparams(("arbitrary",)),
    )(dpre, dpre, proj, proj, conv_w8)


N_PAIRS = SSM_HEADS // 2


def _chunk_iotas():
    row = lax.broadcasted_iota(jnp.int32, (CHUNK, CHUNK), 0)
    lane = lax.broadcasted_iota(jnp.int32, (CHUNK, CHUNK), 1)
    return row, lane, lane <= row


def _silu_and_grad(x):
    s = _sigmoid(x)
    return x * s, s * (1.0 + x * (1.0 - s))


def _pair_select(lo, mat, ha):
    return jnp.where(lo, mat[:, ha:ha + 1], mat[:, ha + 1:ha + 2])


def _ssd_fwd(pre, dtr, proj, dtb, alog, dsk, gn, *, name):
    rows = pre.shape[0]
    nc = rows // CHUNK

    def body(pre_ref, dtr_ref, z_ref, dtb_ref, alog_ref, dsk_ref, gn_ref, yb_ref, y_ref, st_ref, dt_ref, acum_ref, s_scr):
        @pl.when(pl.program_id(0) == 0)
        def _():
            s_scr[...] = jnp.zeros_like(s_scr)

        row, lane, tril = _chunk_iotas()
        dt = _softplus(dtr_ref[...] + dtb_ref[...])
        acum = _dot_exact(tril.astype(F32), dt * (-jnp.exp(alog_ref[...])))
        dt_ref[...] = dt
        acum_ref[...] = acum
        acum_t = acum.T
        lo = lane < 64
        eacum = jnp.exp(acum)
        a_end = acum[CHUNK - 1:CHUNK, :]
        e_end = jnp.exp(a_end)
        dte_all = jnp.exp(a_end - acum)
        dsk_v = dsk_ref[...]
        for g in range(SSM_GROUPS):
            b_g = _silu(pre_ref[:, 1024 + SSM_STATE * g:1024 + SSM_STATE * (g + 1)]).astype(MXU_DTYPE)
            c_g = _silu(pre_ref[:, 1536 + SSM_STATE * g:1536 + SSM_STATE * (g + 1)]).astype(MXU_DTYPE)
            cb = _dot_nt(c_g, b_g)
            gated = []
            for jj in range(2):
                j = 2 * g + jj
                ha = 2 * j
                cs = slice(LANES * j, LANES * (j + 1))
                xs = _silu(pre_ref[:, cs])
                xdt = xs * _pair_select(lo, dt, ha)
                xdt_m = xdt.astype(MXU_DTYPE)
                y_heads = []
                for h in (ha, ha + 1):
                    dec = jnp.exp(jnp.where(tril, acum[:, h:h + 1] - acum_t[h:h + 1, :], -jnp.inf))
                    y_heads.append(_dot(cb * dec, xdt_m))
                s_prev = s_scr[j]
                st_ref[0, j] = s_prev
                y = jnp.where(lo, y_heads[0], y_heads[1])
                y = y + _dot_nt(c_g, s_prev) * _pair_select(lo, eacum, ha)
                y = y + _pair_select(lo, dsk_v, ha) * xs
                xw = xdt * _pair_select(lo, dte_all, ha)
                e_rows = jnp.where(row < 64, e_end[:, ha:ha + 1], e_end[:, ha + 1:ha + 2])
                s_scr[j] = e_rows * s_prev + _dot(xw.T, b_g)
                y_ref[:, cs] = y
                gated.append(y * _silu(z_ref[:, cs]))
            ms = (jnp.sum(gated[0] * gated[0], axis=1, keepdims=True)
                  + jnp.sum(gated[1] * gated[1], axis=1, keepdims=True)) * (1.0 / 256.0)
            r = lax.rsqrt(ms + EPS)
            for jj in range(2):
                cs = slice(LANES * (2 * g + jj), LANES * (2 * g + jj + 1))
                yb_ref[:, cs] = (gated[jj] * r * gn_ref[:, cs]).astype(MXU_DTYPE)

    return pl.pallas_call(
        body, name=name, grid=(nc,),
        out_shape=(jax.ShapeDtypeStruct((rows, D_MODEL), MXU_DTYPE), jax.ShapeDtypeStruct((rows, D_MODEL), F32),
                   jax.ShapeDtypeStruct((nc, N_PAIRS, LANES, SSM_STATE), F32),
                   jax.ShapeDtypeStruct((rows, LANES), F32), jax.ShapeDtypeStruct((rows, LANES), F32)),
        in_specs=[_row_spec(CONV_DIM, CHUNK), _row_spec(LANES, CHUNK), pl.BlockSpec((CHUNK, D_MODEL), lambda i: (i, 4)),
                  _vec_spec(LANES), _vec_spec(LANES), _vec_spec(LANES), _vec_spec(D_MODEL)],
        out_specs=(_row_spec(D_MODEL, CHUNK), _row_spec(D_MODEL, CHUNK),
                   pl.BlockSpec((1, N_PAIRS, LANES, SSM_STATE), lambda i: (i, 0, 0, 0)),
                   _row_spec(LANES, CHUNK), _row_spec(LANES, CHUNK)),
        scratch_shapes=[pltpu.VMEM((N_PAIRS, LANES, SSM_STATE), F32)],
        compiler_params=_cparams(("arbitrary",)),
    )(pre, dtr, proj, dtb, alog, dsk, gn)


def _ssd_bwd(pre, dtr, dt_saved, acum_saved, proj, y_saved, states, dcat, dtb, alog, dsk, gn, *, name):
    rows = pre.shape[0]
    nc = rows // CHUNK

    def rev(i):
        return nc - 1 - i

    def body(pre_ref, dtr_ref, dt_ref, acum_ref, z_ref, y_ref, st_ref, dyb_ref, dtb_ref, alog_ref, dsk_ref, gn_ref,
             dpre_ref, dz_ref, ddtr_ref, dgn_ref, dvec_ref, g_scr):
        @pl.when(pl.program_id(0) == 0)
        def _():
            g_scr[...] = jnp.zeros_like(g_scr)
            dgn_ref[...] = jnp.zeros_like(dgn_ref)
            dvec_ref[...] = jnp.zeros_like(dvec_ref)

        dtb = dtb_ref[...]
        dtr = dtr_ref[...]
        row, lane, tril = _chunk_iotas()
        dt, acum = dt_ref[...], acum_ref[...]
        a = -jnp.exp(alog_ref[...])
        acum_t = acum.T
        lo = lane < 64
        eacum = jnp.exp(acum)
        a_end = acum[CHUNK - 1:CHUNK, :]
        e_end = jnp.exp(a_end)
        dte_all = jnp.exp(a_end - acum)
        dsk_v = dsk_ref[...]
        zero = jnp.zeros((CHUNK, LANES), F32)
        dacum_c, dacum_r, ddt_c = zero, zero, zero
        d_aend = jnp.zeros((1, LANES), F32)
        d_dsk = jnp.zeros((1, LANES), F32)
        lane1 = lane[0:1, :]

        def put_col(acc, h, colvec):
            return acc + jnp.where(lane == h, colvec, 0.0)

        for g in range(SSM_GROUPS):
            gated, sz, dgh = [], [], []
            for jj in range(2):
                cs = slice(LANES * (2 * g + jj), LANES * (2 * g + jj + 1))
                sz.append(_silu_and_grad(z_ref[:, cs]))
                gated.append(y_ref[:, cs] * sz[jj][0])
                dgh.append(dyb_ref[:, cs] * gn_ref[:, cs])
            ms = (jnp.sum(gated[0] * gated[0], axis=1, keepdims=True)
                  + jnp.sum(gated[1] * gated[1], axis=1, keepdims=True)) * (1.0 / 256.0)
            r = lax.rsqrt(ms + EPS)
            proj_g = (jnp.sum(dgh[0] * gated[0], axis=1, keepdims=True)
                      + jnp.sum(dgh[1] * gated[1], axis=1, keepdims=True)) * (1.0 / 256.0)
            dys = []
            for jj in range(2):
                cs = slice(LANES * (2 * g + jj), LANES * (2 * g + jj + 1))
                dgn_ref[:, cs] += _colsum(dyb_ref[:, cs] * gated[jj] * r)
                dgated = r * dgh[jj] - gated[jj] * (r * r * r * proj_g)
                dys.append(dgated * sz[jj][0])
                dz_ref[:, cs] = (dgated * y_ref[:, cs] * sz[jj][1]).astype(MXU_DTYPE)

            b_f, b_grad = _silu_and_grad(pre_ref[:, 1024 + SSM_STATE * g:1024 + SSM_STATE * (g + 1)])
            c_f, c_grad = _silu_and_grad(pre_ref[:, 1536 + SSM_STATE * g:1536 + SSM_STATE * (g + 1)])
            b_g = b_f.astype(MXU_DTYPE)
            c_g = c_f.astype(MXU_DTYPE)
            cb = _dot_nt(c_g, b_g)
            dcb = zero
            db_g, dc_g = zero, zero
            for jj in range(2):
                j = 2 * g + jj
                ha = 2 * j
                cs = slice(LANES * j, LANES * (j + 1))
                xs, xs_grad = _silu_and_grad(pre_ref[:, cs])
                dtsel = _pair_select(lo, dt, ha)
                xdt = xs * dtsel
                xdt_m = xdt.astype(MXU_DTYPE)
                dyp = dys[jj]
                dyp_m = dyp.astype(MXU_DTYPE)
                s_prev = st_ref[0, j]
                g_next = g_scr[j]
                eac = _pair_select(lo, eacum, ha)
                dte = _pair_select(lo, dte_all, ha)
                yoff = _dot_nt(c_g, s_prev) * eac
                t_off = dyp * yoff
                dye = dyp * eac
                dc_g = dc_g + _dot(dye, s_prev)
                bg = _dot_nt(b_g, g_next)
                dxdt = bg * dte
                xw = xdt * dte
                db_g = db_g + _dot(xw, g_next)
                t_w = xw * bg
                gs = g_next * s_prev
                e_rows = jnp.where(row < 64, e_end[:, ha:ha + 1], e_end[:, ha + 1:ha + 2])
                g_scr[j] = e_rows * g_next + _dot(dye.T, c_g)
                dxdt_heads = []
                for hh, h in enumerate((ha, ha + 1)):
                    half = slice(64 * hh, 64 * (hh + 1))
                    dec = jnp.exp(jnp.where(tril, acum[:, h:h + 1] - acum_t[h:h + 1, :], -jnp.inf))
                    m_h = cb * dec
                    dy_h = jnp.where(lo if hh == 0 else jnp.logical_not(lo), dyp, 0.0)
                    dm = _dot_nt(dy_h, xdt_m)
                    dxdt_heads.append(_dot(m_h.T, dyp_m))
                    e_h = dm * m_h
                    dcb = dcb + dm * dec
                    w_col = jnp.sum(t_w[:, half], axis=1, keepdims=True)
                    col = (jnp.sum(e_h, axis=1, keepdims=True) + jnp.sum(t_off[:, half], axis=1, keepdims=True) - w_col)
                    dacum_c = put_col(dacum_c, h, col)
                    dacum_r = dacum_r + jnp.where(row == h, _colsum(e_h), 0.0)
                    d_end_h = jnp.sum(w_col, keepdims=True) + e_end[:, h:h + 1] * jnp.sum(gs[half, :], keepdims=True)
                    d_aend = d_aend + jnp.where(lane1 == h, d_end_h, 0.0)
                dxdt = dxdt + jnp.where(lo, dxdt_heads[0], dxdt_heads[1])
                dsel = _pair_select(lo, dsk_v, ha)
                dxs = dxdt * dtsel + dsel * dyp
                dpre_ref[:, cs] = dxs * xs_grad
                t_dt = dxdt * xs
                t_dk = dyp * xs
                for hh, h in enumerate((ha, ha + 1)):
                    half = slice(64 * hh, 64 * (hh + 1))
                    ddt_c = put_col(ddt_c, h, jnp.sum(t_dt[:, half], axis=1, keepdims=True))
                    d_dsk = d_dsk + jnp.where(lane1 == h, jnp.sum(t_dk[:, half], keepdims=True), 0.0)
            dc_g = dc_g + _dot(dcb, b_g)
            db_g = db_g + _dot(dcb.T, c_g)
            dpre_ref[:, 1024 + SSM_STATE * g:1024 + SSM_STATE * (g + 1)] = db_g * b_grad
            dpre_ref[:, 1536 + SSM_STATE * g:1536 + SSM_STATE * (g + 1)] = dc_g * c_grad

        dacum = dacum_c - dacum_r.T + jnp.where(row == CHUNK - 1, d_aend, 0.0)
        dda = _dot_exact((lane >= row).astype(F32), dacum)
        ddt = dda * a + ddt_c
        ddtr = ddt * _sigmoid(dtr + dtb)
        ddtr_ref[...] = ddtr.astype(MXU_DTYPE)
        dvec_ref[0:1, :] += _colsum(ddtr)
        dvec_ref[1:2, :] += _colsum(dda * dt)
        dvec_ref[2:3, :] += d_dsk

    return pl.pallas_call(
        body, name=name, grid=(nc,),
        out_shape=(jax.ShapeDtypeStruct((rows, CONV_DIM), F32), jax.ShapeDtypeStruct((rows, D_MODEL), MXU_DTYPE),
                   jax.ShapeDtypeStruct((rows, LANES), MXU_DTYPE), jax.ShapeDtypeStruct((1, D_MODEL), F32),
                   jax.ShapeDtypeStruct((8, LANES), F32)),
        in_specs=[pl.BlockSpec((CHUNK, CONV_DIM), lambda i: (rev(i), 0)), pl.BlockSpec((CHUNK, LANES), lambda i: (rev(i), 0)),
                  pl.BlockSpec((CHUNK, LANES), lambda i: (rev(i), 0)), pl.BlockSpec((CHUNK, LANES), lambda i: (rev(i), 0)),
                  pl.BlockSpec((CHUNK, D_MODEL), lambda i: (rev(i), 4)), pl.BlockSpec((CHUNK, D_MODEL), lambda i: (rev(i), 0)),
                  pl.BlockSpec((1, N_PAIRS, LANES, SSM_STATE), lambda i: (rev(i), 0, 0, 0)),
                  pl.BlockSpec((CHUNK, D_MODEL), lambda i: (rev(i), 1)),
                  _vec_spec(LANES), _vec_spec(LANES), _vec_spec(LANES), _vec_spec(D_MODEL)],
        out_specs=(pl.BlockSpec((CHUNK, CONV_DIM), lambda i: (rev(i), 0)), pl.BlockSpec((CHUNK, D_MODEL), lambda i: (rev(i), 0)),
                   pl.BlockSpec((CHUNK, LANES), lambda i: (rev(i), 0)), _vec_spec(D_MODEL), _vec_spec(LANES, 8)),
        scratch_shapes=[pltpu.VMEM((N_PAIRS, LANES, SSM_STATE), F32)],
        compiler_params=_cparams(("arbitrary",)),
    )(pre, dtr, dt_saved, acum_saved, proj, y_saved, states, dcat, dtb, alog, dsk, gn)


def _pool_counts(first_row, n_rows, win):
    t = first_row + lax.broadcasted_iota(jnp.int32, (n_rows, POOL_DIM), 0)
    return jnp.minimum(t + 1, win).astype(F32)


def _pool_fwd(yn, pool_w, pool_b, pool_scale, *, name):
    rows = yn.shape[0]
    tr = ROW_TILE
    hb = tr // POOL_HALO

    def body(y_ref, prev_ref, w_ref, b_ref, s_ref, pm_ref, diff_ref, buf):
        i = pl.program_id(0)
        buf[pl.ds(0, POOL_HALO), :] = jnp.where(i == 0, 0.0, prev_ref[...])
        buf[pl.ds(POOL_HALO, tr), :] = y_ref[...]
        for g, win in enumerate(POOL_WINDOWS):
            cs = slice(POOL_DIM * g, POOL_DIM * (g + 1))
            acc = buf[pl.ds(POOL_HALO, tr), cs]
            for s in range(1, win):
                acc = acc + buf[pl.ds(POOL_HALO - s, tr), cs]
            diff = (acc / _pool_counts(i * tr, tr, win) - y_ref[:, cs]).astype(MXU_DTYPE)
            diff_ref[:, cs] = diff
            pm_ref[:, cs] = (_dot(diff, w_ref[g]) + b_ref[:, cs]) * s_ref[:, cs]

    return pl.pallas_call(
        body, name=name, grid=(rows // tr,),
        out_shape=(jax.ShapeDtypeStruct((rows, D_MODEL), F32), jax.ShapeDtypeStruct((rows, D_MODEL), MXU_DTYPE)),
        in_specs=[_row_spec(D_MODEL, tr),
                  pl.BlockSpec((POOL_HALO, D_MODEL), lambda i: (jnp.maximum(i * hb - 1, 0), 0)),
                  pl.BlockSpec((4, POOL_DIM, POOL_DIM), lambda i: (0, 0, 0)), _vec_spec(D_MODEL), _vec_spec(D_MODEL)],
        out_specs=(_row_spec(D_MODEL, tr), _row_spec(D_MODEL, tr)),
        scratch_shapes=[pltpu.VMEM((tr + POOL_HALO, D_MODEL), F32)],
        compiler_params=_cparams(("parallel",)),
    )(yn, yn, pool_w, pool_b, pool_scale)


def _pool_bwd(dpm, diff, pool_w, pool_w_t, pool_b, pool_scale, *, name):
    rows = dpm.shape[0]
    tr = ROW_TILE
    hb = tr // POOL_HALO
    nblk = rows // tr

    def body(d_ref, dnext_ref, diff_ref, w_ref, wt_ref, b_ref, s_ref, dy_ref, dw_ref, db_ref, ds_ref, ebuf):
        i = pl.program_id(0)

        @pl.when(i == 0)
        def _():
            dw_ref[...] = jnp.zeros_like(dw_ref)
            db_ref[...] = jnp.zeros_like(db_ref)
            ds_ref[...] = jnp.zeros_like(ds_ref)

        last = i == nblk - 1
        for g, win in enumerate(POOL_WINDOWS):
            cs = slice(POOL_DIM * g, POOL_DIM * (g + 1))
            d = d_ref[:, cs]
            diff = diff_ref[:, cs]
            out_pre = _dot(diff, w_ref[g]) + b_ref[:, cs]
            ds_ref[:, cs] += _colsum(d * out_pre)
            dout = d * s_ref[:, cs]
            db_ref[:, cs] += _colsum(dout)
            dw_ref[g] += _dot_tn(diff, dout)
            ddiff = _dot(dout, wt_ref[g])
            ddiff_next = _dot(jnp.where(last, 0.0, dnext_ref[:, cs]) * s_ref[:, cs], wt_ref[g])
            ebuf[pl.ds(0, tr), cs] = ddiff / _pool_counts(i * tr, tr, win)
            ebuf[pl.ds(tr, POOL_HALO), cs] = ddiff_next / _pool_counts((i + 1) * tr, POOL_HALO, win)
            acc = -ddiff
            for s in range(win):
                acc = acc + ebuf[pl.ds(s, tr), cs]
            dy_ref[:, cs] = acc

    return pl.pallas_call(
        body, name=name, grid=(nblk,),
        out_shape=(jax.ShapeDtypeStruct((rows, D_MODEL), F32), jax.ShapeDtypeStruct((4, POOL_DIM, POOL_DIM), F32),
                   jax.ShapeDtypeStruct((1, D_MODEL), F32), jax.ShapeDtypeStruct((1, D_MODEL), F32)),
        in_specs=[_row_spec(D_MODEL, tr),
                  pl.BlockSpec((POOL_HALO, D_MODEL), lambda i: (jnp.minimum((i + 1) * hb, rows // POOL_HALO - 1), 0)),
                  _row_spec(D_MODEL, tr),
                  pl.BlockSpec((4, POOL_DIM, POOL_DIM), lambda i: (0, 0, 0)),
                  pl.BlockSpec((4, POOL_DIM, POOL_DIM), lambda i: (0, 0, 0)), _vec_spec(D_MODEL), _vec_spec(D_MODEL)],
        out_specs=(_row_spec(D_MODEL, tr), pl.BlockSpec((4, POOL_DIM, POOL_DIM), lambda i: (0, 0, 0)),
                   _vec_spec(D_MODEL), _vec_spec(D_MODEL)),
        scratch_shapes=[pltpu.VMEM((tr + POOL_HALO, D_MODEL), F32)],
        compiler_params=_cparams(("arbitrary",)),
    )(dpm, dpm, diff, pool_w, pool_w_t, pool_b, pool_scale)


def _row_tile(rows, cap, step):
    best = rows
    for t in range(step, min(rows, cap) + 1, step):
        if rows % t == 0:
            best = t
    return best if best <= cap else rows


def _sum8(recv, *, name):
    _, r, c = recv.shape
    step = 8 if recv.dtype == F32 else 16

    def body(r_ref, g_ref):
        g = r_ref[0].astype(F32)
        for j in range(1, N_DEV):
            g = g + r_ref[j].astype(F32)
        g_ref[...] = g

    if r % step == 0:
        tr = _row_tile(r, 256, step)
        grid, in_spec, out_spec = (r // tr,), pl.BlockSpec((N_DEV, tr, c), lambda i: (0, i, 0)), pl.BlockSpec((tr, c), lambda i: (i, 0))
    else:
        tc = 256
        grid, in_spec, out_spec = (c // tc,), pl.BlockSpec((N_DEV, r, tc), lambda i: (0, 0, i)), pl.BlockSpec((r, tc), lambda i: (0, i))
    return pl.pallas_call(
        body, name=name, grid=grid, out_shape=jax.ShapeDtypeStruct((r, c), F32),
        in_specs=[in_spec], out_specs=out_spec, compiler_params=_cparams(("parallel",)),
    )(recv)


def _adamw(g, w, m, v, *, name):
    rows, cols = w.shape
    tr = _row_tile(rows, max(8, (256 * 1024) // cols // 8 * 8), 8)
    c1 = 1.0 / (1.0 - ADAM_B1 ** ADAM_STEP)
    c2 = 1.0 / (1.0 - ADAM_B2 ** ADAM_STEP)

    def body(g_ref, w_ref, m_ref, v_ref, d_ref, mo_ref, vo_ref):
        g = g_ref[...]
        m_new = ADAM_B1 * m_ref[...] + (1.0 - ADAM_B1) * g
        v_new = ADAM_B2 * v_ref[...] + (1.0 - ADAM_B2) * (g * g)
        mo_ref[...] = m_new
        vo_ref[...] = v_new
        d_ref[...] = -ADAM_LR * ((m_new * c1) / (jnp.sqrt(v_new * c2) + ADAM_EPS) + ADAM_WD * w_ref[...])

    spec = pl.BlockSpec((tr, cols), lambda i: (i, 0))
    return pl.pallas_call(
        body, name=name, grid=(rows // tr,),
        out_shape=tuple(jax.ShapeDtypeStruct((rows, cols), F32) for _ in range(3)),
        in_specs=[spec] * 4, out_specs=(spec, spec, spec),
        compiler_params=_cparams(("parallel",)),
    )(g, w, m, v)


def _pad_rows(flat, mult):
    n = flat.shape[-1]
    pad = (-n) % mult
    if pad:
        flat = jnp.pad(flat, [(0, 0)] * (flat.ndim - 1) + [(0, pad)])
    return flat


def _pack_blocks(blocks, row_mult):
    flat = jnp.concatenate([_pad_rows(b.reshape(-1), LANES) for b in blocks])
    return _pad_rows(flat, LANES * row_mult).reshape(-1, LANES)


def _block_sizes(blocks):
    return [-(-math.prod(b.shape) // LANES) * LANES for b in blocks]


def _unpack_blocks(slab, like, lead=()):
    flat = slab.reshape(lead + (-1,))
    out, off = [], 0
    for b, size in zip(like, _block_sizes(like)):
        n = math.prod(b.shape)
        out.append(flat[..., off:off + n].reshape(lead + tuple(b.shape)))
        off += size
    return out


def _join_shards(gathered, axis):
    return jnp.concatenate([gathered[j] for j in range(N_DEV)], axis=axis)


def _split_shards(full, axis):
    return jnp.stack(jnp.split(full, N_DEV, axis=axis))


def _interleave_ff(w_gate, w_up):
    k = w_gate.shape[0]
    nt = D_FF // FF_TILE
    return jnp.stack([w_gate.reshape(k, nt, FF_TILE), w_up.reshape(k, nt, FF_TILE)], axis=2).reshape(k, 2 * D_FF)


def _row128(vec):
    return jnp.pad(vec.reshape(1, -1), ((0, 0), (0, LANES - vec.shape[-1])))


def kernel(x, norm_g, w_in, gm_ln_g, gm_ln_b, gm_ws, gm_bs, conv_w, conv_b, dt_bias, a_log, d_skip, ssm_norm_g, w_out, pool_w, pool_b, pool_scale, ffn_w_gate, ffn_w_up, ffn_w_down, loss_target, m_norm_g, m_w_in, m_gm_ln_g, m_gm_ln_b, m_gm_ws, m_gm_bs, m_conv_w, m_conv_b, m_dt_bias, m_a_log, m_d_skip, m_ssm_norm_g, m_w_out, m_pool_w, m_pool_b, m_pool_scale, m_ffn_w_gate, m_ffn_w_up, m_ffn_w_down, v_norm_g, v_w_in, v_gm_ln_g, v_gm_ln_b, v_gm_ws, v_gm_bs, v_conv_w, v_conv_b, v_dt_bias, v_a_log, v_d_skip, v_ssm_norm_g, v_w_out, v_pool_w, v_pool_b, v_pool_scale, v_ffn_w_gate, v_ffn_w_up, v_ffn_w_down):
    w_loc = dict(norm_g=norm_g, w_in=w_in, gm_ln_g=gm_ln_g, gm_ln_b=gm_ln_b, gm_ws=gm_ws, gm_bs=gm_bs, conv_w=conv_w,
                 conv_b=conv_b, dt_bias=dt_bias, a_log=a_log, d_skip=d_skip, ssm_norm_g=ssm_norm_g, w_out=w_out,
                 pool_w=pool_w, pool_b=pool_b, pool_scale=pool_scale, ffn_w_gate=ffn_w_gate, ffn_w_up=ffn_w_up,
                 ffn_w_down=ffn_w_down)
    m_loc = dict(zip(WEIGHTS, [m_norm_g, m_w_in, m_gm_ln_g, m_gm_ln_b, m_gm_ws, m_gm_bs, m_conv_w, m_conv_b, m_dt_bias,
                               m_a_log, m_d_skip, m_ssm_norm_g, m_w_out, m_pool_w, m_pool_b, m_pool_scale,
                               m_ffn_w_gate, m_ffn_w_up, m_ffn_w_down]))
    v_loc = dict(zip(WEIGHTS, [v_norm_g, v_w_in, v_gm_ln_g, v_gm_ln_b, v_gm_ws, v_gm_bs, v_conv_w, v_conv_b, v_dt_bias,
                               v_a_log, v_d_skip, v_ssm_norm_g, v_w_out, v_pool_w, v_pool_b, v_pool_scale,
                               v_ffn_w_gate, v_ffn_w_up, v_ffn_w_down]))

    small_blocks = [w_loc[n] for n in GATHER_F32]
    got = _exchange(
        [w_in[0].astype(MXU_DTYPE), w_out[0].astype(MXU_DTYPE), ffn_w_gate.astype(MXU_DTYPE), ffn_w_up.astype(MXU_DTYPE),
         ffn_w_down.astype(MXU_DTYPE), pool_w[0].astype(MXU_DTYPE), _pack_blocks(small_blocks, 8)],
        ['gather'] * 7, name="gather_weights")
    full = {n: w_loc[n] for n in WEIGHTS if SHARD_AXIS[n] is None}
    full['w_in'] = got[0].transpose(1, 0, 2).reshape(1, D_MODEL, -1)
    full['w_out'] = got[1].reshape(1, -1, D_MODEL)
    full['ffn_w_gate'] = got[2].transpose(1, 2, 0, 3).reshape(2, D_MODEL, D_FF)
    full['ffn_w_up'] = got[3].transpose(1, 2, 0, 3).reshape(2, D_MODEL, D_FF)
    full['ffn_w_down'] = got[4].transpose(1, 0, 2, 3).reshape(2, D_FF, D_MODEL)
    full['pool_w'] = got[5].transpose(1, 0, 2, 3).reshape(1, 4, POOL_DIM, POOL_DIM)
    for n, g in zip(GATHER_F32, _unpack_blocks(got[6], small_blocks, (N_DEV,))):
        full[n] = _join_shards(g, SHARD_AXIS[n])

    loss_part, grad_x, grads = _local_step(x[0], loss_target[0], full)

    small = [n for n in WEIGHTS if n not in BIG_WEIGHTS]
    like = [w_loc[n] for n in small]
    slots = []
    for n in small:
        ax = SHARD_AXIS[n]
        g = grads[n].astype(F32)
        sh = _split_shards(g, ax) if ax is not None else jnp.broadcast_to(g[None], (N_DEV,) + g.shape)
        slots.append(_pad_rows(sh.reshape(N_DEV, -1), LANES))
    send_small = _pad_rows(jnp.concatenate(slots, axis=1), LANES * 8).reshape(N_DEV, -1, LANES)
    w_in_t = grads['w_in_t']
    recv = _exchange(
        [send_small, w_in_t.reshape(N_DEV, -1, D_MODEL), grads['w_out'], *grads['ffn_gu_t'][0], *grads['ffn_gu_t'][1],
         grads['ffn_w_down'][0], grads['ffn_w_down'][1]],
        ['slots', 'slots'] + ['rows'] * 7, name="exchange_grads")

    g_small = _sum8(recv[0], name="sum_small")
    g_own = dict(zip(small, _unpack_blocks(g_small, like)))
    g_own['w_in'] = _sum8(recv[1], name="sum_w_in").T[None]
    g_own['w_out'] = _sum8(recv[2], name="sum_w_out")[None]
    g_own['ffn_w_gate'] = jnp.stack([_sum8(recv[3 + 2 * l], name=f"sum_ffn{l}_gate").T for l in range(2)])
    g_own['ffn_w_up'] = jnp.stack([_sum8(recv[4 + 2 * l], name=f"sum_ffn{l}_up").T for l in range(2)])
    g_own['ffn_w_down'] = jnp.stack([_sum8(recv[7 + l], name=f"sum_ffn{l}_down") for l in range(2)])

    delta, m_new, v_new = {}, {}, {}
    pk = lambda d: _pack_blocks([d[n] for n in small], 8)
    d_s, m_s, v_s = _adamw(g_small, pk(w_loc), pk(m_loc), pk(v_loc), name="adamw_small")
    for dst, slab in ((delta, d_s), (m_new, m_s), (v_new, v_s)):
        dst.update(zip(small, _unpack_blocks(slab, like)))
    for n in BIG_WEIGHTS:
        shape = w_loc[n].shape
        two_d = lambda t: t.reshape(-1, shape[-1])
        res = _adamw(two_d(g_own[n]), two_d(w_loc[n]), two_d(m_loc[n]), two_d(v_loc[n]), name=f"adamw_{n}")
        delta[n], m_new[n], v_new[n] = (t.reshape(shape) for t in res)

    loss = lax.psum(loss_part[0, 0], ("x", "y", "c"))
    outs = [d[n] for d in (g_own, delta, m_new, v_new) for n in WEIGHTS]
    return (loss, grad_x[None], *outs)


def _local_step(h0, tgt, full):
    gm_ln_g, gm_ln_b, gm_ws, gm_bs = full['gm_ln_g'], full['gm_ln_b'], full['gm_ws'], full['gm_bs']
    conv_b, dt_bias, a_log, d_skip, ssm_norm_g = (full['conv_b'], full['dt_bias'], full['a_log'], full['d_skip'],
                                                  full['ssm_norm_g'])
    w_in_f = full['w_in'][0]
    w_main = jnp.concatenate([w_in_f[:, 3072:5120], w_in_f[:, :3072]], axis=1)
    w_dt = jnp.pad(w_in_f[:, 5120:], ((0, 0), (0, LANES - SSM_HEADS)))
    w_out_f = full['w_out'][0]
    w_gu = [_interleave_ff(full['ffn_w_gate'][l], full['ffn_w_up'][l]) for l in range(2)]
    w_dn = [full['ffn_w_down'][l] for l in range(2)]
    pool_w_f = full['pool_w'][0]
    ng = full['norm_g']
    causal = jnp.tril(jnp.ones((CHUNK, CHUNK), bool))
    wm = jnp.where(causal[None], gm_ws[0], 0.0).astype(MXU_DTYPE)
    wm_t = jnp.swapaxes(wm, 1, 2)
    bcol = jnp.pad(gm_bs[0].T, ((0, 0), (0, LANES - GM_HEADS)))
    conv_w8 = jnp.pad(full['conv_w'][0], ((0, 8 - SSM_CONV), (0, 0)))
    dtb, alog, dsk = _row128(dt_bias[0]), _row128(a_log[0]), _row128(d_skip[0])
    pool_b_f = full['pool_b'][0].reshape(1, D_MODEL)
    pool_s_f = full['pool_scale']

    def g_(layer, i):
        return ng[layer, i].reshape(1, D_MODEL)

    yn0 =_rn_fwd(h0, g_(0, 0), name="rn_fwd_0", out_dtype=MXU_DTYPE)
    proj = _mm(yn0, w_main, name="mm_in_proj", tm=2048)
    dtr = _mm(yn0, w_dt, name="mm_in_proj_dt")
    pre = _conv_fwd(proj, conv_w8, conv_b, name="conv_fwd")
    ya = _gmlp_fwd(proj, gm_ln_g, gm_ln_b, wm, bcol, name="gmlp_fwd")
    yb, y_ssd, states, dt_ssd, acum_ssd = _ssd_fwd(pre, dtr, proj, dtb, alog, dsk, ssm_norm_g, name="ssd_fwd")
    cat = jnp.concatenate([ya, yb], axis=1)
    o0 = _mm(cat, w_out_f, name="mm_out_proj", tm=2048, tn=1024, tk=1024)
    h1, yn1 = _resid_rn_fwd(h0, o0, g_(0, 1), g_(0, 2), name="resid_fwd_0a", next_dtype=MXU_DTYPE)
    gu0, act0 = _mm_swiglu(yn1, w_gu[0], name="mm_ffn0_gate_up")
    d0 = _mm(act0, w_dn[0], name="mm_ffn0_down", tm=2048, tn=1024, tk=1408)
    h2, yn2 = _resid_rn_fwd(h1, d0, g_(0, 3), g_(1, 0), name="resid_fwd_0b", next_dtype=F32)
    pm, pdiff = _pool_fwd(yn2, pool_w_f, pool_b_f, pool_s_f, name="pool_fwd")
    h3, yn3 = _resid_rn_fwd(h2, pm, g_(1, 1), g_(1, 2), name="resid_fwd_1a", next_dtype=MXU_DTYPE)
    gu1, act1 = _mm_swiglu(yn3, w_gu[1], name="mm_ffn1_gate_up")
    d1 = _mm(act1, w_dn[1], name="mm_ffn1_down", tm=2048, tn=1024, tk=1408)
    dh4, loss_part = _resid_loss(h3, d1, g_(1, 3), tgt, name="resid_loss")

    grads = {}
    dng = [[None] * 4 for _ in range(2)]

    def ffn_bwd(layer, dh, d_out, gu, act, yn, h_in):
        dd, dng[layer][3] = _resid_bwd_post(dh, d_out, g_(layer, 3), name=f"resid_bwd_post_{layer}b", out_dtype=MXU_DTYPE)
        dw_dn = _mm_tn(act, dd, name=f"mm_ffn{layer}_dw_down", out_dtype=MXU_DTYPE, tm=1408, tn=1024)
        dgu = _mm_dswiglu(dd, w_dn[layer].T, gu, name=f"mm_ffn{layer}_dact")
        dw_gu_t = _mm_tn_gate_up(dgu, yn, name=f"mm_ffn{layer}_dw_gate_up", out_dtype=MXU_DTYPE)
        dyn = _mm(dgu, w_gu[layer].T, name=f"mm_ffn{layer}_dyn", tm=2048, tn=1024, tk=1408)
        dh_in, dng[layer][2] = _resid_bwd_pre(dh, [dyn], h_in, g_(layer, 2), name=f"resid_bwd_pre_{layer}b")
        return dh_in, dw_gu_t, dw_dn

    dh3, dw_gu_t1, dw_dn1 = ffn_bwd(1, dh4, d1, gu1, act1, yn3, h3)
    dpm, dng[1][1] = _resid_bwd_post(dh3, pm, g_(1, 1), name="resid_bwd_post_1a", out_dtype=F32)
    dyn2, d_pool_w, d_pool_b, d_pool_s = _pool_bwd(dpm, pdiff, pool_w_f, jnp.swapaxes(pool_w_f, 1, 2), pool_b_f, pool_s_f,
                                                   name="pool_bwd")
    dh2, dng[1][0] = _resid_bwd_pre(dh3, [dyn2], h2, g_(1, 0), name="resid_bwd_pre_1a")
    dh1, dw_gu_t0, dw_dn0 = ffn_bwd(0, dh2, d0, gu0, act0, yn1, h1)
    do0, dng[0][1] = _resid_bwd_post(dh1, o0, g_(0, 1), name="resid_bwd_post_0a", out_dtype=MXU_DTYPE)
    d_w_out = _mm_tn(cat, do0, name="mm_out_proj_dw", out_dtype=MXU_DTYPE, tn=1024)
    dcat = _mm(do0, w_out_f.T, name="mm_out_proj_dx", tm=2048, tn=1024)
    du, dv, d_wm, d_bcol, d_ln_g, d_ln_b = _gmlp_bwd(proj, dcat, gm_ln_g, gm_ln_b, wm, wm_t, bcol, name="gmlp_bwd")
    dpre, dz, ddtr, d_gn, d_vec = _ssd_bwd(pre, dtr, dt_ssd, acum_ssd, proj, y_ssd, states, dcat, dtb, alog, dsk, ssm_norm_g, name="ssd_bwd")
    dxbc, d_conv_w8, d_conv_b = _conv_bwd(dpre, proj, conv_w8, name="conv_bwd")
    dproj = jnp.concatenate([dxbc, du, dv, dz], axis=1)
    d_w_main_t = _mm_tn(dproj, yn0, name="mm_in_proj_dw", out_dtype=MXU_DTYPE, tn=1024, shift=3)
    d_w_dt_t = _mm_tn(ddtr, yn0, name="mm_in_proj_dt_dw", out_dtype=MXU_DTYPE, tn=1024)
    dyn0 = _mm(dproj, w_main.T, name="mm_in_proj_dx", tm=2048, tn=1024, tk=1024)
    dyn0_dt = _mm(ddtr, w_dt.T, name="mm_in_proj_dt_dx")
    grad_x, dng[0][0] = _resid_bwd_pre(dh1, [dyn0, dyn0_dt], h0, g_(0, 0), name="resid_bwd_pre_0a")

    grads['norm_g'] = jnp.stack([jnp.concatenate(dng[l], axis=0) for l in range(2)])
    grads['w_in_t'] = jnp.concatenate([d_w_main_t, d_w_dt_t[:SSM_HEADS]], axis=0)
    grads['gm_ln_g'], grads['gm_ln_b'] = d_ln_g, d_ln_b
    grads['gm_ws'] = d_wm[None]
    grads['gm_bs'] = d_bcol[:, :GM_HEADS].T[None]
    grads['conv_w'] = d_conv_w8[None, :SSM_CONV]
    grads['conv_b'] = d_conv_b
    grads['dt_bias'] = d_vec[0:1, :SSM_HEADS]
    grads['a_log'] = d_vec[1:2, :SSM_HEADS] * (-jnp.exp(a_log))
    grads['d_skip'] = d_vec[2:3, :SSM_HEADS]
    grads['ssm_norm_g'] = d_gn
    grads['w_out'] = d_w_out
    grads['pool_w'] = d_pool_w[None]
    grads['pool_b'] = d_pool_b.reshape(1, 4, POOL_DIM)
    grads['pool_scale'] = d_pool_s
    grads['ffn_gu_t'] = [dw_gu_t0, dw_gu_t1]
    grads['ffn_w_down'] = [dw_dn0, dw_dn1]
    return loss_part, grad_x, grads
```

```python
import functools
import math

import jax
import jax.numpy as jnp
from jax import lax
from jax.experimental import pallas as pl
from jax.experimental.pallas import tpu as pltpu

F32 = jnp.float32
MXU_DTYPE = jnp.bfloat16

N_DEV = 8
D_MODEL = 1024
EPS = 1e-6
GM_HEADS = 4
GM_HEAD_DIM = 256
CHUNK = 128
SSM_HEADS = 16
SSM_GROUPS = 4
SSM_STATE = 128
SSM_CONV = 4
CONV_DIM = 2048
POOL_WINDOWS = (2, 4, 8, 16)
POOL_DIM = 256
D_FF = 2816
FF_TILE = 256
IN_MAIN = 5120
LANES = 128
CONV_HALO = 8
POOL_HALO = 16
ADAM_LR, ADAM_B1, ADAM_B2, ADAM_EPS, ADAM_WD, ADAM_STEP = 0.001, 0.9, 0.999, 1e-08, 0.01, 10

VMEM_LIMIT = 56 * 1024 * 1024
ROW_TILE = 512
MM_TM = 2048

WEIGHTS = ['norm_g', 'w_in', 'gm_ln_g', 'gm_ln_b', 'gm_ws', 'gm_bs', 'conv_w', 'conv_b', 'dt_bias', 'a_log',
           'd_skip', 'ssm_norm_g', 'w_out', 'pool_w', 'pool_b', 'pool_scale', 'ffn_w_gate', 'ffn_w_up', 'ffn_w_down']
SHARD_AXIS = {'norm_g': 2, 'w_in': 2, 'gm_ln_g': None, 'gm_ln_b': None, 'gm_ws': None, 'gm_bs': None, 'conv_w': 2,
              'conv_b': None, 'dt_bias': None, 'a_log': None, 'd_skip': None, 'ssm_norm_g': None, 'w_out': 1,
              'pool_w': 2, 'pool_b': 2, 'pool_scale': 1, 'ffn_w_gate': 2, 'ffn_w_up': 2, 'ffn_w_down': 1}
GATHER_BF16 = ['w_in', 'w_out', 'pool_w', 'ffn_w_gate', 'ffn_w_up', 'ffn_w_down']
GATHER_F32 = ['norm_g', 'conv_w', 'pool_b', 'pool_scale']
BIG_WEIGHTS = ['w_in', 'w_out', 'ffn_w_gate', 'ffn_w_up', 'ffn_w_down']


def _cparams(sem=None):
    return pltpu.CompilerParams(dimension_semantics=sem, vmem_limit_bytes=VMEM_LIMIT)


def _dot(a, b):
    return jnp.dot(a.astype(MXU_DTYPE), b.astype(MXU_DTYPE), preferred_element_type=F32)


def _dot_nt(a, b):
    return lax.dot_general(a.astype(MXU_DTYPE), b.astype(MXU_DTYPE), (((1,), (1,)), ((), ())),
                           preferred_element_type=F32)


def _dot_tn(a, b):
    return lax.dot_general(a.astype(MXU_DTYPE), b.astype(MXU_DTYPE), (((0,), (0,)), ((), ())),
                           preferred_element_type=F32)


def _dot_exact(a, b):
    return jnp.dot(a, b, precision=lax.Precision.HIGHEST, preferred_element_type=F32)


def _sigmoid(x):
    return 1.0 / (1.0 + jnp.exp(-x))


def _silu(x):
    return x * _sigmoid(x)


def _silu_grad(x):
    s = _sigmoid(x)
    return s * (1.0 + x * (1.0 - s))


_GELU_C = math.sqrt(2.0 / math.pi)


def _gelu(x):
    return 0.5 * x * (1.0 + jnp.tanh(_GELU_C * (x + 0.044715 * x * x * x)))


def _gelu_grad(x):
    t = jnp.tanh(_GELU_C * (x + 0.044715 * x * x * x))
    return 0.5 * (1.0 + t) + 0.5 * x * (1.0 - t * t) * _GELU_C * (1.0 + 3.0 * 0.044715 * x * x)


def _softplus(x):
    return jnp.maximum(x, 0.0) + jnp.log1p(jnp.exp(-jnp.abs(x)))


def _rms_scale(x):
    return lax.rsqrt(jnp.mean(x * x, axis=-1, keepdims=True) + EPS)


def _rms_bwd(dy, x, g):
    r = _rms_scale(x)
    xn = x * r
    dxn = dy * g
    dx = r * (dxn - xn * jnp.mean(dxn * xn, axis=-1, keepdims=True))
    return dx, dy * xn


def _colsum(x):
    return jnp.sum(x, axis=0, keepdims=True)


class _Exchange:
    def __init__(self, arrays, modes):
        self.arrays, self.modes, self.n = list(arrays), list(modes), len(arrays)
        self.blks = []
        for x, mode in zip(arrays, modes):
            if mode == 'gather':
                self.blks.append(tuple(x.shape))
            elif mode == 'slots':
                self.blks.append(tuple(x.shape[1:]))
            else:
                self.blks.append((x.shape[0] // N_DEV,) + tuple(x.shape[1:]))
        self.out_shape = [jax.ShapeDtypeStruct((N_DEV,) + blk, x.dtype) for x, blk in zip(arrays, self.blks)]
        self.in_specs = [pl.BlockSpec(memory_space=pl.ANY)] * self.n
        self.out_specs = [pl.BlockSpec(memory_space=pl.ANY) for _ in range(self.n)]
        n_sem = self.n * (N_DEV - 1)
        self.scratch = [pltpu.SemaphoreType.DMA((n_sem,)), pltpu.SemaphoreType.DMA((n_sem,)),
                        pltpu.SemaphoreType.DMA((self.n,))]

    def _copies(self, x_refs, out_refs, send_sems, recv_sems, local_sems, with_recvs):
        mx, my, mc = lax.axis_index("x"), lax.axis_index("y"), lax.axis_index("c")
        me = 4 * mx + 2 * my + mc

        def flip(v, bit):
            return 1 - v if bit else v

        def part(a, dev):
            if self.modes[a] == 'gather':
                return x_refs[a]
            if self.modes[a] == 'slots':
                return x_refs[a].at[dev]
            r = self.blks[a][0]
            return x_refs[a].at[pl.ds(pl.multiple_of(dev * r, 16), r)]

        sends, recvs, owns = [], [], []
        for k in (1, 2, 4, 6, 3, 5, 7):
            px, py, pc = flip(mx, (k >> 2) & 1), flip(my, (k >> 1) & 1), flip(mc, k & 1)
            peer = 4 * px + 2 * py + pc
            for a in range(self.n):
                sem = a * (N_DEV - 1) + k - 1
                sends.append(pltpu.make_async_remote_copy(
                    src_ref=part(a, peer), dst_ref=out_refs[a].at[me], send_sem=send_sems.at[sem],
                    recv_sem=recv_sems.at[sem], device_id=(px, py, pc), device_id_type=pl.DeviceIdType.MESH))
                if with_recvs:
                    recvs.append(pltpu.make_async_remote_copy(
                        src_ref=part(a, peer), dst_ref=out_refs[a].at[peer], send_sem=send_sems.at[sem],
                        recv_sem=recv_sems.at[sem], device_id=(px, py, pc), device_id_type=pl.DeviceIdType.MESH))
        for a in range(self.n):
            owns.append(pltpu.make_async_copy(part(a, me), out_refs[a].at[me], local_sems.at[a]))
        return sends, recvs, owns

    def start(self, *refs):
        sends, _, owns = self._copies(*refs, with_recvs=False)
        for cp in sends + owns:
            cp.start()

    def wait(self, *refs):
        sends, recvs, owns = self._copies(*refs, with_recvs=True)
        for cp in recvs:
            cp.wait_recv()
        for cp in sends:
            cp.wait_send()
        for cp in owns:
            cp.wait()


def _exchange(arrays, modes, *, name):
    ex = _Exchange(arrays, modes)

    def body(*refs):
        x_refs, out_refs, sems = refs[:ex.n], refs[ex.n:2 * ex.n], refs[2 * ex.n:]
        ex.start(x_refs, out_refs, *sems)
        ex.wait(x_refs, out_refs, *sems)

    return pl.pallas_call(
        body, name=name, out_shape=tuple(ex.out_shape), in_specs=ex.in_specs, out_specs=tuple(ex.out_specs),
        scratch_shapes=ex.scratch,
    )(*arrays)


def _hosted(body, n_in, n_out, n_scratch, grid, ex):
    def wrapped(*refs):
        ins, x_refs = refs[:n_in], refs[n_in:n_in + ex.n]
        outs = refs[n_in + ex.n:n_in + ex.n + n_out]
        xo_refs = refs[n_in + ex.n + n_out:n_in + 2 * ex.n + n_out]
        scr = refs[n_in + 2 * ex.n + n_out:n_in + 2 * ex.n + n_out + n_scratch]
        sems = refs[n_in + 2 * ex.n + n_out + n_scratch:]
        ids = [pl.program_id(d) for d in range(len(grid))]
        first = functools.reduce(jnp.logical_and, [i == 0 for i in ids])
        last = functools.reduce(jnp.logical_and, [i == g - 1 for i, g in zip(ids, grid)])

        @pl.when(first)
        def _():
            ex.start(x_refs, xo_refs, *sems)

        body(*ins, *outs, *scr)

        @pl.when(last)
        def _():
            ex.wait(x_refs, xo_refs, *sems)

    return wrapped


def _call(body, *, name, grid, inputs, in_specs, out_shape, out_specs, scratch, semantics, ex=None):
    if ex is None:
        return pl.pallas_call(
            body, name=name, grid=grid, out_shape=tuple(out_shape), in_specs=list(in_specs),
            out_specs=tuple(out_specs), scratch_shapes=list(scratch), compiler_params=_cparams(semantics))(*inputs)
    n_out = len(out_shape)
    res = pl.pallas_call(
        _hosted(body, len(inputs), n_out, len(scratch), grid, ex), name=name, grid=grid,
        out_shape=tuple(out_shape) + tuple(ex.out_shape), in_specs=list(in_specs) + ex.in_specs,
        out_specs=tuple(out_specs) + tuple(ex.out_specs), scratch_shapes=list(scratch) + ex.scratch,
        compiler_params=_cparams(("arbitrary",) * len(grid)))(*inputs, *ex.arrays)
    return res[:n_out], res[n_out:]


def _mm(a, b, *, name, out_dtype=F32, tm=MM_TM, tn=512, tk=None, ex=None):
    m, k = a.shape
    n = b.shape[1]
    tm, tn = min(tm, m), min(tn, n)
    tk = k if tk is None else tk
    nk = k // tk
    assert m % tm == 0 and n % tn == 0 and k % tk == 0

    def body(a_ref, b_ref, o_ref, acc_ref):
        kk = pl.program_id(2)
        part = _dot(a_ref[...], b_ref[...])
        if nk == 1:
            o_ref[...] = part.astype(out_dtype)
        else:
            @pl.when(kk == 0)
            def _():
                acc_ref[...] = part

            @pl.when(kk > 0)
            def _():
                acc_ref[...] += part

            @pl.when(kk == nk - 1)
            def _():
                o_ref[...] = acc_ref[...].astype(out_dtype)

    res = _call(
        body, name=name, grid=(m // tm, n // tn, nk), inputs=(a, b),
        in_specs=[pl.BlockSpec((tm, tk), lambda i, j, kk: (i, kk)), pl.BlockSpec((tk, tn), lambda i, j, kk: (kk, j))],
        out_shape=[jax.ShapeDtypeStruct((m, n), out_dtype)],
        out_specs=[pl.BlockSpec((tm, tn), lambda i, j, kk: (i, j))],
        scratch=[pltpu.VMEM((tm, tn) if nk > 1 else (8, LANES), F32)],
        semantics=("parallel", "parallel", "arbitrary"), ex=ex)
    return res[0] if ex is None else (res[0][0], res[1])


def _mm_tn(a, b, *, name, out_dtype=F32, tm=1024, tn=512, tk=1024, shift=0):
    t, m = a.shape
    n = b.shape[1]
    tm, tn, tk = min(tm, m), min(tn, n), min(tk, t)
    nk = t // tk
    nb = m // tm
    assert m % tm == 0 and n % tn == 0 and t % tk == 0

    def body(a_ref, b_ref, o_ref, acc_ref):
        kk = pl.program_id(2)
        part = _dot_tn(a_ref[...], b_ref[...])

        @pl.when(kk == 0)
        def _():
            acc_ref[...] = part

        @pl.when(kk > 0)
        def _():
            acc_ref[...] += part

        @pl.when(kk == nk - 1)
        def _():
            o_ref[...] = acc_ref[...].astype(out_dtype)

    return pl.pallas_call(
        body, name=name, grid=(nb, n // tn, nk),
        out_shape=jax.ShapeDtypeStruct((m, n), out_dtype),
        in_specs=[pl.BlockSpec((tk, tm), lambda i, j, kk: (kk, i)), pl.BlockSpec((tk, tn), lambda i, j, kk: (kk, j))],
        out_specs=pl.BlockSpec((tm, tn), lambda i, j, kk: ((i + shift) % nb, j)),
        scratch_shapes=[pltpu.VMEM((tm, tn), F32)],
        compiler_params=_cparams(("parallel", "parallel", "arbitrary")),
    )(a, b)


def _mm_tn_gate_up(dgu, yn, *, name, out_dtype, tk=2048):
    t, m = dgu.shape
    n = yn.shape[1]
    tk = min(tk, t)
    nk = t // tk
    nb = m // (2 * FF_TILE)

    def body(a_ref, b_ref, og_ref, ou_ref, acc_ref):
        kk = pl.program_id(1)
        part = _dot_tn(a_ref[...], b_ref[...])

        @pl.when(kk == 0)
        def _():
            acc_ref[...] = part

        @pl.when(kk > 0)
        def _():
            acc_ref[...] += part

        @pl.when(kk == nk - 1)
        def _():
            og_ref[...] = acc_ref[:FF_TILE, :].astype(out_dtype)
            ou_ref[...] = acc_ref[FF_TILE:, :].astype(out_dtype)

    out = jax.ShapeDtypeStruct((m // 2, n), out_dtype)
    o_spec = pl.BlockSpec((FF_TILE, n), lambda i, kk: (i, 0))
    return pl.pallas_call(
        body, name=name, grid=(nb, nk), out_shape=(out, out),
        in_specs=[pl.BlockSpec((tk, 2 * FF_TILE), lambda i, kk: (kk, i)), pl.BlockSpec((tk, n), lambda i, kk: (kk, 0))],
        out_specs=(o_spec, o_spec),
        scratch_shapes=[pltpu.VMEM((2 * FF_TILE, n), F32)],
        compiler_params=_cparams(("parallel", "arbitrary")),
    )(dgu, yn)


def _mm_swiglu(a, w_gu, *, name, tm=MM_TM, ex=None):
    m, k = a.shape
    n = w_gu.shape[1]
    nt = n // (2 * FF_TILE)
    tm = min(tm, m)

    def body(a_ref, b_ref, gu_ref, act_ref):
        gu = _dot(a_ref[...], b_ref[...])
        gu_ref[...] = gu
        act_ref[...] = (_silu(gu[:, :FF_TILE]) * gu[:, FF_TILE:]).astype(MXU_DTYPE)

    return _call(
        body, name=name, grid=(m // tm, nt), inputs=(a, w_gu),
        in_specs=[pl.BlockSpec((tm, k), lambda i, j: (i, 0)), pl.BlockSpec((k, 2 * FF_TILE), lambda i, j: (0, j))],
        out_shape=[jax.ShapeDtypeStruct((m, n), F32), jax.ShapeDtypeStruct((m, n // 2), MXU_DTYPE)],
        out_specs=[pl.BlockSpec((tm, 2 * FF_TILE), lambda i, j: (i, j)), pl.BlockSpec((tm, FF_TILE), lambda i, j: (i, j))],
        scratch=[], semantics=("parallel", "parallel"), ex=ex)


def _mm_dswiglu(dd, w_down_t, gu, *, name, tm=MM_TM):
    m, k = dd.shape
    n = gu.shape[1]
    nt = n // (2 * FF_TILE)
    tm = min(tm, m)

    def body(d_ref, w_ref, gu_ref, o_ref):
        dact = _dot(d_ref[...], w_ref[...])
        gate, up = gu_ref[:, :FF_TILE], gu_ref[:, FF_TILE:]
        o_ref[:, :FF_TILE] = (dact * up * _silu_grad(gate)).astype(MXU_DTYPE)
        o_ref[:, FF_TILE:] = (dact * _silu(gate)).astype(MXU_DTYPE)

    return pl.pallas_call(
        body, name=name, grid=(m // tm, nt),
        out_shape=jax.ShapeDtypeStruct((m, n), MXU_DTYPE),
        in_specs=[pl.BlockSpec((tm, k), lambda i, j: (i, 0)), pl.BlockSpec((k, FF_TILE), lambda i, j: (0, j)),
                  pl.BlockSpec((tm, 2 * FF_TILE), lambda i, j: (i, j))],
        out_specs=pl.BlockSpec((tm, 2 * FF_TILE), lambda i, j: (i, j)),
        compiler_params=_cparams(("parallel", "parallel")),
    )(dd, w_down_t, gu)


def _row_spec(width, tr=ROW_TILE):
    return pl.BlockSpec((tr, width), lambda i: (i, 0))


def _vec_spec(width, rows=1):
    return pl.BlockSpec((rows, width), lambda i: (0, 0))


def _rn_fwd(h, g, *, name, out_dtype):
    rows, d = h.shape

    def body(h_ref, g_ref, o_ref):
        x = h_ref[...]
        o_ref[...] = (x * _rms_scale(x) * g_ref[...]).astype(out_dtype)

    return pl.pallas_call(
        body, name=name, grid=(rows // ROW_TILE,),
        out_shape=jax.ShapeDtypeStruct((rows, d), out_dtype),
        in_specs=[_row_spec(d), _vec_spec(d)], out_specs=_row_spec(d),
        compiler_params=_cparams(("parallel",)),
    )(h, g)


def _resid_rn_fwd(h_in, o, g_post, g_next, *, name, next_dtype):
    rows, d = h_in.shape

    def body(h_ref, o_ref, gp_ref, gn_ref, ho_ref, yn_ref):
        ov = o_ref[...]
        h = h_ref[...] + ov * _rms_scale(ov) * gp_ref[...]
        ho_ref[...] = h
        yn_ref[...] = (h * _rms_scale(h) * gn_ref[...]).astype(next_dtype)

    return pl.pallas_call(
        body, name=name, grid=(rows // ROW_TILE,),
        out_shape=(jax.ShapeDtypeStruct((rows, d), F32), jax.ShapeDtypeStruct((rows, d), next_dtype)),
        in_specs=[_row_spec(d), _row_spec(d), _vec_spec(d), _vec_spec(d)],
        out_specs=(_row_spec(d), _row_spec(d)),
        compiler_params=_cparams(("parallel",)),
    )(h_in, o, g_post, g_next)


def _resid_loss(h_in, o, g_post, target, *, name):
    rows, d = h_in.shape

    def body(h_ref, o_ref, gp_ref, t_ref, dh_ref, loss_ref):
        ov = o_ref[...]
        err = h_ref[...] + ov * _rms_scale(ov) * gp_ref[...] - t_ref[...]
        dh_ref[...] = err * (1.0 / d)

        @pl.when(pl.program_id(0) == 0)
        def _():
            loss_ref[...] = jnp.zeros_like(loss_ref)

        loss_ref[...] += 0.5 * jnp.sum(jnp.mean(err * err, axis=-1, keepdims=True), axis=0, keepdims=True)

    return pl.pallas_call(
        body, name=name, grid=(rows // ROW_TILE,),
        out_shape=(jax.ShapeDtypeStruct((rows, d), F32), jax.ShapeDtypeStruct((1, 1), F32)),
        in_specs=[_row_spec(d), _row_spec(d), _vec_spec(d), _row_spec(d)],
        out_specs=(_row_spec(d), pl.BlockSpec((1, 1), lambda i: (0, 0))),
        compiler_params=_cparams(("arbitrary",)),
    )(h_in, o, g_post, target)


def _resid_bwd_post(dh, o, g_post, *, name, out_dtype):
    rows, d = dh.shape

    def body(dh_ref, o_ref, g_ref, do_ref, dg_ref):
        do, dg = _rms_bwd(dh_ref[...], o_ref[...], g_ref[...])
        do_ref[...] = do.astype(out_dtype)

        @pl.when(pl.program_id(0) == 0)
        def _():
            dg_ref[...] = jnp.zeros_like(dg_ref)

        dg_ref[...] += _colsum(dg)

    return pl.pallas_call(
        body, name=name, grid=(rows // ROW_TILE,),
        out_shape=(jax.ShapeDtypeStruct((rows, d), out_dtype), jax.ShapeDtypeStruct((1, d), F32)),
        in_specs=[_row_spec(d), _row_spec(d), _vec_spec(d)],
        out_specs=(_row_spec(d), _vec_spec(d)),
        compiler_params=_cparams(("arbitrary",)),
    )(dh, o, g_post)


def _resid_bwd_pre(dh, dyn_list, h_in, g_pre, *, name):
    rows, d = dh.shape
    n_dyn = len(dyn_list)

    def body(*refs):
        dh_ref, dyn_refs, h_ref, g_ref, out_ref, dg_ref = refs[0], refs[1:1 + n_dyn], *refs[1 + n_dyn:]
        dyn = dyn_refs[0][...]
        for r in dyn_refs[1:]:
            dyn = dyn + r[...]
        dx, dg = _rms_bwd(dyn, h_ref[...], g_ref[...])
        out_ref[...] = dh_ref[...] + dx

        @pl.when(pl.program_id(0) == 0)
        def _():
            dg_ref[...] = jnp.zeros_like(dg_ref)

        dg_ref[...] += _colsum(dg)

    return pl.pallas_call(
        body, name=name, grid=(rows // ROW_TILE,),
        out_shape=(jax.ShapeDtypeStruct((rows, d), F32), jax.ShapeDtypeStruct((1, d), F32)),
        in_specs=[_row_spec(d)] + [_row_spec(d)] * n_dyn + [_row_spec(d), _vec_spec(d)],
        out_specs=(_row_spec(d), _vec_spec(d)),
        compiler_params=_cparams(("arbitrary",)),
    )(dh, *dyn_list, h_in, g_pre)


def _layer_norm_stats(x):
    mu = jnp.mean(x, axis=-1, keepdims=True)
    xc = x - mu
    rstd = lax.rsqrt(jnp.mean(xc * xc, axis=-1, keepdims=True) + EPS)
    return xc * rstd, rstd


def _gmlp_fwd(proj, ln_g, ln_b, wm, bcol, *, name):
    rows = proj.shape[0]
    tr = ROW_TILE

    def body(u_ref, v_ref, lg_ref, lb_ref, wm_ref, bc_ref, ya_ref):
        vhat, _ = _layer_norm_stats(_gelu(v_ref[...]))
        vl = (vhat * lg_ref[...] + lb_ref[...]).astype(MXU_DTYPE)
        gu = _gelu(u_ref[...])
        bc = bc_ref[...]
        for c in range(tr // CHUNK):
            rs = slice(c * CHUNK, (c + 1) * CHUNK)
            for h in range(GM_HEADS):
                cs = slice(h * GM_HEAD_DIM, (h + 1) * GM_HEAD_DIM)
                mixed = _dot(wm_ref[h], vl[rs, cs]) + bc[:, h:h + 1]
                ya_ref[rs, cs] = (gu[rs, cs] * mixed).astype(MXU_DTYPE)

    return pl.pallas_call(
        body, name=name, grid=(rows // tr,),
        out_shape=jax.ShapeDtypeStruct((rows, D_MODEL), MXU_DTYPE),
        in_specs=[pl.BlockSpec((tr, D_MODEL), lambda i: (i, 2)), pl.BlockSpec((tr, D_MODEL), lambda i: (i, 3)),
                  _vec_spec(D_MODEL), _vec_spec(D_MODEL),
                  pl.BlockSpec((GM_HEADS, CHUNK, CHUNK), lambda i: (0, 0, 0)), _vec_spec(LANES, CHUNK)],
        out_specs=_row_spec(D_MODEL, tr),
        compiler_params=_cparams(("parallel",)),
    )(proj, proj, ln_g, ln_b, wm, bcol)


def _gmlp_bwd(proj, dcat, ln_g, ln_b, wm, wm_t, bcol, *, name):
    rows = proj.shape[0]
    tr = ROW_TILE

    def body(u_ref, v_ref, dy_ref, lg_ref, lb_ref, wm_ref, wmt_ref, bc_ref,
             du_ref, dv_ref, dwm_ref, dbc_ref, dlg_ref, dlb_ref, dvl_scr):
        @pl.when(pl.program_id(0) == 0)
        def _():
            dwm_ref[...] = jnp.zeros_like(dwm_ref)
            dbc_ref[...] = jnp.zeros_like(dbc_ref)
            dlg_ref[...] = jnp.zeros_like(dlg_ref)
            dlb_ref[...] = jnp.zeros_like(dlb_ref)

        u, v = u_ref[...], v_ref[...]
        gv = _gelu(v)
        vhat, rstd = _layer_norm_stats(gv)
        lg = lg_ref[...]
        vl = (vhat * lg + lb_ref[...]).astype(MXU_DTYPE)
        gu = _gelu(u)
        dy = dy_ref[...]
        bc = bc_ref[...]
        row = lax.broadcasted_iota(jnp.int32, (CHUNK, CHUNK), 0)
        lane = lax.broadcasted_iota(jnp.int32, (CHUNK, CHUNK), 1)
        causal = lane <= row
        dbc = jnp.zeros((CHUNK, LANES), F32)
        for c in range(tr // CHUNK):
            rs = slice(c * CHUNK, (c + 1) * CHUNK)
            for h in range(GM_HEADS):
                cs = slice(h * GM_HEAD_DIM, (h + 1) * GM_HEAD_DIM)
                vl_h = vl[rs, cs]
                mixed = _dot(wm_ref[h], vl_h) + bc[:, h:h + 1]
                dy_h = dy[rs, cs]
                du_ref[rs, cs] = (dy_h * mixed * _gelu_grad(u[rs, cs])).astype(MXU_DTYPE)
                dmixed = dy_h * gu[rs, cs]
                dwm_ref[h] += jnp.where(causal, _dot_nt(dmixed, vl_h), 0.0)
                dbc = dbc + jnp.where(lane == h, jnp.sum(dmixed, axis=1, keepdims=True), 0.0)
                dvl_scr[rs, cs] = _dot(wmt_ref[h], dmixed)
        dbc_ref[...] += dbc
        dvl = dvl_scr[...]
        dlg_ref[...] += _colsum(dvl * vhat)
        dlb_ref[...] += _colsum(dvl)
        dvh = dvl * lg
        dgv = rstd * (dvh - jnp.mean(dvh, axis=-1, keepdims=True) - vhat * jnp.mean(dvh * vhat, axis=-1, keepdims=True))
        dv_ref[...] = (dgv * _gelu_grad(v)).astype(MXU_DTYPE)

    return pl.pallas_call(
        body, name=name, grid=(rows // tr,),
        out_shape=(jax.ShapeDtypeStruct((rows, D_MODEL), MXU_DTYPE), jax.ShapeDtypeStruct((rows, D_MODEL), MXU_DTYPE),
                   jax.ShapeDtypeStruct((GM_HEADS, CHUNK, CHUNK), F32), jax.ShapeDtypeStruct((CHUNK, LANES), F32),
                   jax.ShapeDtypeStruct((1, D_MODEL), F32), jax.ShapeDtypeStruct((1, D_MODEL), F32)),
        in_specs=[pl.BlockSpec((tr, D_MODEL), lambda i: (i, 2)), pl.BlockSpec((tr, D_MODEL), lambda i: (i, 3)),
                  pl.BlockSpec((tr, D_MODEL), lambda i: (i, 0)), _vec_spec(D_MODEL), _vec_spec(D_MODEL),
                  pl.BlockSpec((GM_HEADS, CHUNK, CHUNK), lambda i: (0, 0, 0)),
                  pl.BlockSpec((GM_HEADS, CHUNK, CHUNK), lambda i: (0, 0, 0)), _vec_spec(LANES, CHUNK)],
        out_specs=(_row_spec(D_MODEL, tr), _row_spec(D_MODEL, tr),
                   pl.BlockSpec((GM_HEADS, CHUNK, CHUNK), lambda i: (0, 0, 0)), _vec_spec(LANES, CHUNK),
                   _vec_spec(D_MODEL), _vec_spec(D_MODEL)),
        scratch_shapes=[pltpu.VMEM((tr, D_MODEL), F32)],
        compiler_params=_cparams(("arbitrary",)),
    )(proj, proj, dcat, ln_g, ln_b, wm, wm_t, bcol)


def _conv_fwd(proj, conv_w8, conv_b, *, name):
    rows = proj.shape[0]
    tr = ROW_TILE
    hb = tr // CONV_HALO

    def body(x_ref, prev_ref, w_ref, b_ref, pre_ref, buf):
        first = pl.program_id(0) == 0
        buf[pl.ds(0, CONV_HALO), :] = jnp.where(first, 0.0, prev_ref[...])
        buf[pl.ds(CONV_HALO, tr), :] = x_ref[...]
        acc = jnp.broadcast_to(b_ref[...], (tr, CONV_DIM))
        for k in range(SSM_CONV):
            acc = acc + w_ref[k:k + 1, :] * buf[pl.ds(CONV_HALO - (SSM_CONV - 1) + k, tr), :]
        pre_ref[...] = acc

    return pl.pallas_call(
        body, name=name, grid=(rows // tr,),
        out_shape=jax.ShapeDtypeStruct((rows, CONV_DIM), F32),
        in_specs=[pl.BlockSpec((tr, CONV_DIM), lambda i: (i, 0)),
                  pl.BlockSpec((CONV_HALO, CONV_DIM), lambda i: (jnp.maximum(i * hb - 1, 0), 0)),
                  _vec_spec(CONV_DIM, 8), _vec_spec(CONV_DIM)],
        out_specs=_row_spec(CONV_DIM, tr),
        scratch_shapes=[pltpu.VMEM((tr + CONV_HALO, CONV_DIM), F32)],
        compiler_params=_cparams(("parallel",)),
    )(proj, proj, conv_w8, conv_b)


def _conv_bwd(dpre, proj, conv_w8, *, name):
    rows = proj.shape[0]
    tr = ROW_TILE
    hb = tr // CONV_HALO
    nblk = rows // tr

    def body(d_ref, dnext_ref, x_ref, prev_ref, w_ref, dx_ref, dw_ref, db_ref, dbuf, xbuf):
        i = pl.program_id(0)

        @pl.when(i == 0)
        def _():
            dw_ref[...] = jnp.zeros_like(dw_ref)
            db_ref[...] = jnp.zeros_like(db_ref)

        d = d_ref[...]
        dbuf[pl.ds(0, tr), :] = d
        dbuf[pl.ds(tr, CONV_HALO), :] = jnp.where(i == nblk - 1, 0.0, dnext_ref[...])
        xbuf[pl.ds(0, CONV_HALO), :] = jnp.where(i == 0, 0.0, prev_ref[...])
        xbuf[pl.ds(CONV_HALO, tr), :] = x_ref[...]
        acc = jnp.zeros((tr, CONV_DIM), F32)
        for k in range(SSM_CONV):
            acc = acc + w_ref[k:k + 1, :] * dbuf[pl.ds(SSM_CONV - 1 - k, tr), :]
            dw_ref[k:k + 1, :] += _colsum(d * xbuf[pl.ds(CONV_HALO - (SSM_CONV - 1) + k, tr), :])
        dx_ref[...] = acc.astype(MXU_DTYPE)
        db_ref[...] += _colsum(d)

    return pl.pallas_call(
        body, name=name, grid=(nblk,),
        out_shape=(jax.ShapeDtypeStruct((rows, CONV_DIM), MXU_DTYPE), jax.ShapeDtypeStruct((8, CONV_DIM), F32),
                   jax.ShapeDtypeStruct((1, CONV_DIM), F32)),
        in_specs=[_row_spec(CONV_DIM, tr),
                  pl.BlockSpec((CONV_HALO, CONV_DIM), lambda i: (jnp.minimum((i + 1) * hb, rows // CONV_HALO - 1), 0)),
                  pl.BlockSpec((tr, CONV_DIM), lambda i: (i, 0)),
                  pl.BlockSpec((CONV_HALO, CONV_DIM), lambda i: (jnp.maximum(i * hb - 1, 0), 0)),
                  _vec_spec(CONV_DIM, 8)],
        out_specs=(_row_spec(CONV_DIM, tr), _vec_spec(CONV_DIM, 8), _vec_spec(CONV_DIM)),
        scratch_shapes=[pltpu.VMEM((tr + CONV_HALO, CONV_DIM), F32), pltpu.VMEM((tr + CONV_HALO, CONV_DIM), F32)],
        compiler_params=_cparams(("arbitrary",)),
    )(dpre, dpre, proj, proj, conv_w8)


N_PAIRS = SSM_HEADS // 2


def _chunk_iotas():
    row = lax.broadcasted_iota(jnp.int32, (CHUNK, CHUNK), 0)
    lane = lax.broadcasted_iota(jnp.int32, (CHUNK, CHUNK), 1)
    return row, lane, lane <= row


def _silu_and_grad(x):
    s = _sigmoid(x)
    return x * s, s * (1.0 + x * (1.0 - s))


def _pair_select(lo, mat, ha):
    return jnp.where(lo, mat[:, ha:ha + 1], mat[:, ha + 1:ha + 2])


def _ssd_fwd(pre, dtr, proj, dtb, alog, dsk, gn, *, name):
    rows = pre.shape[0]
    nc = rows // CHUNK

    def body(pre_ref, dtr_ref, z_ref, dtb_ref, alog_ref, dsk_ref, gn_ref, yb_ref, y_ref, st_ref, dt_ref, acum_ref, s_scr):
        @pl.when(pl.program_id(0) == 0)
        def _():
            s_scr[...] = jnp.zeros_like(s_scr)

        row, lane, tril = _chunk_iotas()
        dt = _softplus(dtr_ref[...] + dtb_ref[...])
        acum = _dot_exact(tril.astype(F32), dt * (-jnp.exp(alog_ref[...])))
        dt_ref[...] = dt
        acum_ref[...] = acum
        acum_t = acum.T
        lo = lane < 64
        eacum = jnp.exp(acum)
        a_end = acum[CHUNK - 1:CHUNK, :]
        e_end = jnp.exp(a_end)
        dte_all = jnp.exp(a_end - acum)
        dsk_v = dsk_ref[...]
        for g in range(SSM_GROUPS):
            b_g = _silu(pre_ref[:, 1024 + SSM_STATE * g:1024 + SSM_STATE * (g + 1)]).astype(MXU_DTYPE)
            c_g = _silu(pre_ref[:, 1536 + SSM_STATE * g:1536 + SSM_STATE * (g + 1)]).astype(MXU_DTYPE)
            cb = _dot_nt(c_g, b_g)
            gated = []
            for jj in range(2):
                j = 2 * g + jj
                ha = 2 * j
                cs = slice(LANES * j, LANES * (j + 1))
                xs = _silu(pre_ref[:, cs])
                xdt = xs * _pair_select(lo, dt, ha)
                xdt_m = xdt.astype(MXU_DTYPE)
                y_heads = []
                for h in (ha, ha + 1):
                    dec = jnp.exp(jnp.where(tril, acum[:, h:h + 1] - acum_t[h:h + 1, :], -jnp.inf))
                    y_heads.append(_dot(cb * dec, xdt_m))
                s_prev = s_scr[j]
                st_ref[0, j] = s_prev
                y = jnp.where(lo, y_heads[0], y_heads[1])
                y = y + _dot_nt(c_g, s_prev) * _pair_select(lo, eacum, ha)
                y = y + _pair_select(lo, dsk_v, ha) * xs
                xw = xdt * _pair_select(lo, dte_all, ha)
                e_rows = jnp.where(row < 64, e_end[:, ha:ha + 1], e_end[:, ha + 1:ha + 2])
                s_scr[j] = e_rows * s_prev + _dot(xw.T, b_g)
                y_ref[:, cs] = y
                gated.append(y * _silu(z_ref[:, cs]))
            ms = (jnp.sum(gated[0] * gated[0], axis=1, keepdims=True)
                  + jnp.sum(gated[1] * gated[1], axis=1, keepdims=True)) * (1.0 / 256.0)
            r = lax.rsqrt(ms + EPS)
            for jj in range(2):
                cs = slice(LANES * (2 * g + jj), LANES * (2 * g + jj + 1))
                yb_ref[:, cs] = (gated[jj] * r * gn_ref[:, cs]).astype(MXU_DTYPE)

    return pl.pallas_call(
        body, name=name, grid=(nc,),
        out_shape=(jax.ShapeDtypeStruct((rows, D_MODEL), MXU_DTYPE), jax.ShapeDtypeStruct((rows, D_MODEL), F32),
                   jax.ShapeDtypeStruct((nc, N_PAIRS, LANES, SSM_STATE), F32),
                   jax.ShapeDtypeStruct((rows, LANES), F32), jax.ShapeDtypeStruct((rows, LANES), F32)),
        in_specs=[_row_spec(CONV_DIM, CHUNK), _row_spec(LANES, CHUNK), pl.BlockSpec((CHUNK, D_MODEL), lambda i: (i, 4)),
                  _vec_spec(LANES), _vec_spec(LANES), _vec_spec(LANES), _vec_spec(D_MODEL)],
        out_specs=(_row_spec(D_MODEL, CHUNK), _row_spec(D_MODEL, CHUNK),
                   pl.BlockSpec((1, N_PAIRS, LANES, SSM_STATE), lambda i: (i, 0, 0, 0)),
                   _row_spec(LANES, CHUNK), _row_spec(LANES, CHUNK)),
        scratch_shapes=[pltpu.VMEM((N_PAIRS, LANES, SSM_STATE), F32)],
        compiler_params=_cparams(("arbitrary",)),
    )(pre, dtr, proj, dtb, alog, dsk, gn)


def _ssd_bwd(pre, dtr, dt_saved, acum_saved, proj, y_saved, states, dcat, dtb, alog, dsk, gn, *, name):
    rows = pre.shape[0]
    nc = rows // CHUNK

    def rev(i):
        return nc - 1 - i

    def body(pre_ref, dtr_ref, dt_ref, acum_ref, z_ref, y_ref, st_ref, dyb_ref, dtb_ref, alog_ref, dsk_ref, gn_ref,
             dpre_ref, dz_ref, ddtr_ref, dgn_ref, dvec_ref, g_scr):
        @pl.when(pl.program_id(0) == 0)
        def _():
            g_scr[...] = jnp.zeros_like(g_scr)
            dgn_ref[...] = jnp.zeros_like(dgn_ref)
            dvec_ref[...] = jnp.zeros_like(dvec_ref)

        dtb = dtb_ref[...]
        dtr = dtr_ref[...]
        row, lane, tril = _chunk_iotas()
        dt, acum = dt_ref[...], acum_ref[...]
        a = -jnp.exp(alog_ref[...])
        acum_t = acum.T
        lo = lane < 64
        eacum = jnp.exp(acum)
        a_end = acum[CHUNK - 1:CHUNK, :]
        e_end = jnp.exp(a_end)
        dte_all = jnp.exp(a_end - acum)
        dsk_v = dsk_ref[...]
        zero = jnp.zeros((CHUNK, LANES), F32)
        dacum_c, dacum_r, ddt_c = zero, zero, zero
        d_aend = jnp.zeros((1, LANES), F32)
        d_dsk = jnp.zeros((1, LANES), F32)
        lane1 = lane[0:1, :]

        def put_col(acc, h, colvec):
            return acc + jnp.where(lane == h, colvec, 0.0)

        for g in range(SSM_GROUPS):
            gated, sz, dgh = [], [], []
            for jj in range(2):
                cs = slice(LANES * (2 * g + jj), LANES * (2 * g + jj + 1))
                sz.append(_silu_and_grad(z_ref[:, cs]))
                gated.append(y_ref[:, cs] * sz[jj][0])
                dgh.append(dyb_ref[:, cs] * gn_ref[:, cs])
            ms = (jnp.sum(gated[0] * gated[0], axis=1, keepdims=True)
                  + jnp.sum(gated[1] * gated[1], axis=1, keepdims=True)) * (1.0 / 256.0)
            r = lax.rsqrt(ms + EPS)
            proj_g = (jnp.sum(dgh[0] * gated[0], axis=1, keepdims=True)
                      + jnp.sum(dgh[1] * gated[1], axis=1, keepdims=True)) * (1.0 / 256.0)
            dys = []
            for jj in range(2):
                cs = slice(LANES * (2 * g + jj), LANES * (2 * g + jj + 1))
                dgn_ref[:, cs] += _colsum(dyb_ref[:, cs] * gated[jj] * r)
                dgated = r * dgh[jj] - gated[jj] * (r * r * r * proj_g)
                dys.append(dgated * sz[jj][0])
                dz_ref[:, cs] = (dgated * y_ref[:, cs] * sz[jj][1]).astype(MXU_DTYPE)

            b_f, b_grad = _silu_and_grad(pre_ref[:, 1024 + SSM_STATE * g:1024 + SSM_STATE * (g + 1)])
            c_f, c_grad = _silu_and_grad(pre_ref[:, 1536 + SSM_STATE * g:1536 + SSM_STATE * (g + 1)])
            b_g = b_f.astype(MXU_DTYPE)
            c_g = c_f.astype(MXU_DTYPE)
            cb = _dot_nt(c_g, b_g)
            dcb = zero
            db_g, dc_g = zero, zero
            for jj in range(2):
                j = 2 * g + jj
                ha = 2 * j
                cs = slice(LANES * j, LANES * (j + 1))
                xs, xs_grad = _silu_and_grad(pre_ref[:, cs])
                dtsel = _pair_select(lo, dt, ha)
                xdt = xs * dtsel
                xdt_m = xdt.astype(MXU_DTYPE)
                dyp = dys[jj]
                dyp_m = dyp.astype(MXU_DTYPE)
                s_prev = st_ref[0, j]
                g_next = g_scr[j]
                eac = _pair_select(lo, eacum, ha)
                dte = _pair_select(lo, dte_all, ha)
                yoff = _dot_nt(c_g, s_prev) * eac
                t_off = dyp * yoff
                dye = dyp * eac
                dc_g = dc_g + _dot(dye, s_prev)
                bg = _dot_nt(b_g, g_next)
                dxdt = bg * dte
                xw = xdt * dte
                db_g = db_g + _dot(xw, g_next)
                t_w = xw * bg
                gs = g_next * s_prev
                e_rows = jnp.where(row < 64, e_end[:, ha:ha + 1], e_end[:, ha + 1:ha + 2])
                g_scr[j] = e_rows * g_next + _dot(dye.T, c_g)
                dxdt_heads = []
                for hh, h in enumerate((ha, ha + 1)):
                    half = slice(64 * hh, 64 * (hh + 1))
                    dec = jnp.exp(jnp.where(tril, acum[:, h:h + 1] - acum_t[h:h + 1, :], -jnp.inf))
                    m_h = cb * dec
                    dy_h = jnp.where(lo if hh == 0 else jnp.logical_not(lo), dyp, 0.0)
                    dm = _dot_nt(dy_h, xdt_m)
                    dxdt_heads.append(_dot(m_h.T, dyp_m))
                    e_h = dm * m_h
                    dcb = dcb + dm * dec
                    w_col = jnp.sum(t_w[:, half], axis=1, keepdims=True)
                    col = (jnp.sum(e_h, axis=1, keepdims=True) + jnp.sum(t_off[:, half], axis=1, keepdims=True) - w_col)
                    dacum_c = put_col(dacum_c, h, col)
                    dacum_r = dacum_r + jnp.where(row == h, _colsum(e_h), 0.0)
                    d_end_h = jnp.sum(w_col, keepdims=True) + e_end[:, h:h + 1] * jnp.sum(gs[half, :], keepdims=True)
                    d_aend = d_aend + jnp.where(lane1 == h, d_end_h, 0.0)
                dxdt = dxdt + jnp.where(lo, dxdt_heads[0], dxdt_heads[1])
                dsel = _pair_select(lo, dsk_v, ha)
                dxs = dxdt * dtsel + dsel * dyp
                dpre_ref[:, cs] = dxs * xs_grad
                t_dt = dxdt * xs
                t_dk = dyp * xs
                for hh, h in enumerate((ha, ha + 1)):
                    half = slice(64 * hh, 64 * (hh + 1))
                    ddt_c = put_col(ddt_c, h, jnp.sum(t_dt[:, half], axis=1, keepdims=True))
                    d_dsk = d_dsk + jnp.where(lane1 == h, jnp.sum(t_dk[:, half], keepdims=True), 0.0)
            dc_g = dc_g + _dot(dcb, b_g)
            db_g = db_g + _dot(dcb.T, c_g)
            dpre_ref[:, 1024 + SSM_STATE * g:1024 + SSM_STATE * (g + 1)] = db_g * b_grad
            dpre_ref[:, 1536 + SSM_STATE * g:1536 + SSM_STATE * (g + 1)] = dc_g * c_grad

        dacum = dacum_c - dacum_r.T + jnp.where(row == CHUNK - 1, d_aend, 0.0)
        dda = _dot_exact((lane >= row).astype(F32), dacum)
        ddt = dda * a + ddt_c
        ddtr = ddt * _sigmoid(dtr + dtb)
        ddtr_ref[...] = ddtr.astype(MXU_DTYPE)
        dvec_ref[0:1, :] += _colsum(ddtr)
        dvec_ref[1:2, :] += _colsum(dda * dt)
        dvec_ref[2:3, :] += d_dsk

    return pl.pallas_call(
        body, name=name, grid=(nc,),
        out_shape=(jax.ShapeDtypeStruct((rows, CONV_DIM), F32), jax.ShapeDtypeStruct((rows, D_MODEL), MXU_DTYPE),
                   jax.ShapeDtypeStruct((rows, LANES), MXU_DTYPE), jax.ShapeDtypeStruct((1, D_MODEL), F32),
                   jax.ShapeDtypeStruct((8, LANES), F32)),
        in_specs=[pl.BlockSpec((CHUNK, CONV_DIM), lambda i: (rev(i), 0)), pl.BlockSpec((CHUNK, LANES), lambda i: (rev(i), 0)),
                  pl.BlockSpec((CHUNK, LANES), lambda i: (rev(i), 0)), pl.BlockSpec((CHUNK, LANES), lambda i: (rev(i), 0)),
                  pl.BlockSpec((CHUNK, D_MODEL), lambda i: (rev(i), 4)), pl.BlockSpec((CHUNK, D_MODEL), lambda i: (rev(i), 0)),
                  pl.BlockSpec((1, N_PAIRS, LANES, SSM_STATE), lambda i: (rev(i), 0, 0, 0)),
                  pl.BlockSpec((CHUNK, D_MODEL), lambda i: (rev(i), 1)),
                  _vec_spec(LANES), _vec_spec(LANES), _vec_spec(LANES), _vec_spec(D_MODEL)],
        out_specs=(pl.BlockSpec((CHUNK, CONV_DIM), lambda i: (rev(i), 0)), pl.BlockSpec((CHUNK, D_MODEL), lambda i: (rev(i), 0)),
                   pl.BlockSpec((CHUNK, LANES), lambda i: (rev(i), 0)), _vec_spec(D_MODEL), _vec_spec(LANES, 8)),
        scratch_shapes=[pltpu.VMEM((N_PAIRS, LANES, SSM_STATE), F32)],
        compiler_params=_cparams(("arbitrary",)),
    )(pre, dtr, dt_saved, acum_saved, proj, y_saved, states, dcat, dtb, alog, dsk, gn)


def _pool_counts(first_row, n_rows, win):
    t = first_row + lax.broadcasted_iota(jnp.int32, (n_rows, POOL_DIM), 0)
    return jnp.minimum(t + 1, win).astype(F32)


def _pool_fwd(yn, pool_w, pool_b, pool_scale, *, name):
    rows = yn.shape[0]
    tr = ROW_TILE
    hb = tr // POOL_HALO

    def body(y_ref, prev_ref, w_ref, b_ref, s_ref, pm_ref, diff_ref, buf):
        i = pl.program_id(0)
        buf[pl.ds(0, POOL_HALO), :] = jnp.where(i == 0, 0.0, prev_ref[...])
        buf[pl.ds(POOL_HALO, tr), :] = y_ref[...]
        for g, win in enumerate(POOL_WINDOWS):
            cs = slice(POOL_DIM * g, POOL_DIM * (g + 1))
            acc = buf[pl.ds(POOL_HALO, tr), cs]
            for s in range(1, win):
                acc = acc + buf[pl.ds(POOL_HALO - s, tr), cs]
            diff = (acc / _pool_counts(i * tr, tr, win) - y_ref[:, cs]).astype(MXU_DTYPE)
            diff_ref[:, cs] = diff
            pm_ref[:, cs] = (_dot(diff, w_ref[g]) + b_ref[:, cs]) * s_ref[:, cs]

    return pl.pallas_call(
        body, name=name, grid=(rows // tr,),
        out_shape=(jax.ShapeDtypeStruct((rows, D_MODEL), F32), jax.ShapeDtypeStruct((rows, D_MODEL), MXU_DTYPE)),
        in_specs=[_row_spec(D_MODEL, tr),
                  pl.BlockSpec((POOL_HALO, D_MODEL), lambda i: (jnp.maximum(i * hb - 1, 0), 0)),
                  pl.BlockSpec((4, POOL_DIM, POOL_DIM), lambda i: (0, 0, 0)), _vec_spec(D_MODEL), _vec_spec(D_MODEL)],
        out_specs=(_row_spec(D_MODEL, tr), _row_spec(D_MODEL, tr)),
        scratch_shapes=[pltpu.VMEM((tr + POOL_HALO, D_MODEL), F32)],
        compiler_params=_cparams(("parallel",)),
    )(yn, yn, pool_w, pool_b, pool_scale)


def _pool_bwd(dpm, diff, pool_w, pool_w_t, pool_b, pool_scale, *, name):
    rows = dpm.shape[0]
    tr = ROW_TILE
    hb = tr // POOL_HALO
    nblk = rows // tr

    def body(d_ref, dnext_ref, diff_ref, w_ref, wt_ref, b_ref, s_ref, dy_ref, dw_ref, db_ref, ds_ref, ebuf):
        i = pl.program_id(0)

        @pl.when(i == 0)
        def _():
            dw_ref[...] = jnp.zeros_like(dw_ref)
            db_ref[...] = jnp.zeros_like(db_ref)
            ds_ref[...] = jnp.zeros_like(ds_ref)

        last = i == nblk - 1
        for g, win in enumerate(POOL_WINDOWS):
            cs = slice(POOL_DIM * g, POOL_DIM * (g + 1))
            d = d_ref[:, cs]
            diff = diff_ref[:, cs]
            out_pre = _dot(diff, w_ref[g]) + b_ref[:, cs]
            ds_ref[:, cs] += _colsum(d * out_pre)
            dout = d * s_ref[:, cs]
            db_ref[:, cs] += _colsum(dout)
            dw_ref[g] += _dot_tn(diff, dout)
            ddiff = _dot(dout, wt_ref[g])
            ddiff_next = _dot(jnp.where(last, 0.0, dnext_ref[:, cs]) * s_ref[:, cs], wt_ref[g])
            ebuf[pl.ds(0, tr), cs] = ddiff / _pool_counts(i * tr, tr, win)
            ebuf[pl.ds(tr, POOL_HALO), cs] = ddiff_next / _pool_counts((i + 1) * tr, POOL_HALO, win)
            acc = -ddiff
            for s in range(win):
                acc = acc + ebuf[pl.ds(s, tr), cs]
            dy_ref[:, cs] = acc

    return pl.pallas_call(
        body, name=name, grid=(nblk,),
        out_shape=(jax.ShapeDtypeStruct((rows, D_MODEL), F32), jax.ShapeDtypeStruct((4, POOL_DIM, POOL_DIM), F32),
                   jax.ShapeDtypeStruct((1, D_MODEL), F32), jax.ShapeDtypeStruct((1, D_MODEL), F32)),
        in_specs=[_row_spec(D_MODEL, tr),
                  pl.BlockSpec((POOL_HALO, D_MODEL), lambda i: (jnp.minimum((i + 1) * hb, rows // POOL_HALO - 1), 0)),
                  _row_spec(D_MODEL, tr),
                  pl.BlockSpec((4, POOL_DIM, POOL_DIM), lambda i: (0, 0, 0)),
                  pl.BlockSpec((4, POOL_DIM, POOL_DIM), lambda i: (0, 0, 0)), _vec_spec(D_MODEL), _vec_spec(D_MODEL)],
        out_specs=(_row_spec(D_MODEL, tr), pl.BlockSpec((4, POOL_DIM, POOL_DIM), lambda i: (0, 0, 0)),
                   _vec_spec(D_MODEL), _vec_spec(D_MODEL)),
        scratch_shapes=[pltpu.VMEM((tr + POOL_HALO, D_MODEL), F32)],
        compiler_params=_cparams(("arbitrary",)),
    )(dpm, dpm, diff, pool_w, pool_w_t, pool_b, pool_scale)


def _row_tile(rows, cap, step):
    best = rows
    for t in range(step, min(rows, cap) + 1, step):
        if rows % t == 0:
            best = t
    return best if best <= cap else rows


def _sum8(recv, *, name):
    _, r, c = recv.shape
    step = 8 if recv.dtype == F32 else 16

    def body(r_ref, g_ref):
        g = r_ref[0].astype(F32)
        for j in range(1, N_DEV):
            g = g + r_ref[j].astype(F32)
        g_ref[...] = g

    if r % step == 0:
        tr = _row_tile(r, 256, step)
        grid, in_spec, out_spec = (r // tr,), pl.BlockSpec((N_DEV, tr, c), lambda i: (0, i, 0)), pl.BlockSpec((tr, c), lambda i: (i, 0))
    else:
        tc = 256
        grid, in_spec, out_spec = (c // tc,), pl.BlockSpec((N_DEV, r, tc), lambda i: (0, 0, i)), pl.BlockSpec((r, tc), lambda i: (0, i))
    return pl.pallas_call(
        body, name=name, grid=grid, out_shape=jax.ShapeDtypeStruct((r, c), F32),
        in_specs=[in_spec], out_specs=out_spec, compiler_params=_cparams(("parallel",)),
    )(recv)


def _adamw(g, w, m, v, *, name):
    rows, cols = w.shape
    tr = _row_tile(rows, max(8, (256 * 1024) // cols // 8 * 8), 8)
    c1 = 1.0 / (1.0 - ADAM_B1 ** ADAM_STEP)
    c2 = 1.0 / (1.0 - ADAM_B2 ** ADAM_STEP)

    def body(g_ref, w_ref, m_ref, v_ref, d_ref, mo_ref, vo_ref):
        g = g_ref[...]
        m_new = ADAM_B1 * m_ref[...] + (1.0 - ADAM_B1) * g
        v_new = ADAM_B2 * v_ref[...] + (1.0 - ADAM_B2) * (g * g)
        mo_ref[...] = m_new
        vo_ref[...] = v_new
        d_ref[...] = -ADAM_LR * ((m_new * c1) / (jnp.sqrt(v_new * c2) + ADAM_EPS) + ADAM_WD * w_ref[...])

    spec = pl.BlockSpec((tr, cols), lambda i: (i, 0))
    return pl.pallas_call(
        body, name=name, grid=(rows // tr,),
        out_shape=tuple(jax.ShapeDtypeStruct((rows, cols), F32) for _ in range(3)),
        in_specs=[spec] * 4, out_specs=(spec, spec, spec),
        compiler_params=_cparams(("parallel",)),
    )(g, w, m, v)


def _pad_rows(flat, mult):
    n = flat.shape[-1]
    pad = (-n) % mult
    if pad:
        flat = jnp.pad(flat, [(0, 0)] * (flat.ndim - 1) + [(0, pad)])
    return flat


def _pack_blocks(blocks, row_mult):
    flat = jnp.concatenate([_pad_rows(b.reshape(-1), LANES) for b in blocks])
    return _pad_rows(flat, LANES * row_mult).reshape(-1, LANES)


def _block_sizes(blocks):
    return [-(-math.prod(b.shape) // LANES) * LANES for b in blocks]


def _unpack_blocks(slab, like, lead=()):
    flat = slab.reshape(lead + (-1,))
    out, off = [], 0
    for b, size in zip(like, _block_sizes(like)):
        n = math.prod(b.shape)
        out.append(flat[..., off:off + n].reshape(lead + tuple(b.shape)))
        off += size
    return out


def _join_shards(gathered, axis):
    return jnp.concatenate([gathered[j] for j in range(N_DEV)], axis=axis)


def _split_shards(full, axis):
    return jnp.stack(jnp.split(full, N_DEV, axis=axis))


def _interleave_ff(w_gate, w_up):
    k = w_gate.shape[0]
    nt = D_FF // FF_TILE
    return jnp.stack([w_gate.reshape(k, nt, FF_TILE), w_up.reshape(k, nt, FF_TILE)], axis=2).reshape(k, 2 * D_FF)


def _row128(vec):
    return jnp.pad(vec.reshape(1, -1), ((0, 0), (0, LANES - vec.shape[-1])))


def kernel(x, norm_g, w_in, gm_ln_g, gm_ln_b, gm_ws, gm_bs, conv_w, conv_b, dt_bias, a_log, d_skip, ssm_norm_g, w_out, pool_w, pool_b, pool_scale, ffn_w_gate, ffn_w_up, ffn_w_down, loss_target, m_norm_g, m_w_in, m_gm_ln_g, m_gm_ln_b, m_gm_ws, m_gm_bs, m_conv_w, m_conv_b, m_dt_bias, m_a_log, m_d_skip, m_ssm_norm_g, m_w_out, m_pool_w, m_pool_b, m_pool_scale, m_ffn_w_gate, m_ffn_w_up, m_ffn_w_down, v_norm_g, v_w_in, v_gm_ln_g, v_gm_ln_b, v_gm_ws, v_gm_bs, v_conv_w, v_conv_b, v_dt_bias, v_a_log, v_d_skip, v_ssm_norm_g, v_w_out, v_pool_w, v_pool_b, v_pool_scale, v_ffn_w_gate, v_ffn_w_up, v_ffn_w_down):
    w_loc = dict(norm_g=norm_g, w_in=w_in, gm_ln_g=gm_ln_g, gm_ln_b=gm_ln_b, gm_ws=gm_ws, gm_bs=gm_bs, conv_w=conv_w,
                 conv_b=conv_b, dt_bias=dt_bias, a_log=a_log, d_skip=d_skip, ssm_norm_g=ssm_norm_g, w_out=w_out,
                 pool_w=pool_w, pool_b=pool_b, pool_scale=pool_scale, ffn_w_gate=ffn_w_gate, ffn_w_up=ffn_w_up,
                 ffn_w_down=ffn_w_down)
    m_loc = dict(zip(WEIGHTS, [m_norm_g, m_w_in, m_gm_ln_g, m_gm_ln_b, m_gm_ws, m_gm_bs, m_conv_w, m_conv_b, m_dt_bias,
                               m_a_log, m_d_skip, m_ssm_norm_g, m_w_out, m_pool_w, m_pool_b, m_pool_scale,
                               m_ffn_w_gate, m_ffn_w_up, m_ffn_w_down]))
    v_loc = dict(zip(WEIGHTS, [v_norm_g, v_w_in, v_gm_ln_g, v_gm_ln_b, v_gm_ws, v_gm_bs, v_conv_w, v_conv_b, v_dt_bias,
                               v_a_log, v_d_skip, v_ssm_norm_g, v_w_out, v_pool_w, v_pool_b, v_pool_scale,
                               v_ffn_w_gate, v_ffn_w_up, v_ffn_w_down]))

    small_blocks = [w_loc[n] for n in GATHER_F32]
    got = _exchange([w_in[0].astype(MXU_DTYPE), _pack_blocks(small_blocks, 8)], ['gather'] * 2, name="gather_first")
    full = {n: w_loc[n] for n in WEIGHTS if SHARD_AXIS[n] is None}
    full['w_in'] = got[0].transpose(1, 0, 2).reshape(1, D_MODEL, -1)
    for n, g in zip(GATHER_F32, _unpack_blocks(got[1], small_blocks, (N_DEV,))):
        full[n] = _join_shards(g, SHARD_AXIS[n])
    shards = {n: w_loc[n].astype(MXU_DTYPE) for n in ('w_out', 'ffn_w_gate', 'ffn_w_up', 'ffn_w_down', 'pool_w')}

    loss_part, grad_x, grads, recv = _local_step(x[0], loss_target[0], full, shards)

    small = [n for n in WEIGHTS if n not in BIG_WEIGHTS]
    like = [w_loc[n] for n in small]
    slots = []
    for n in small:
        ax = SHARD_AXIS[n]
        g = grads[n].astype(F32)
        sh = _split_shards(g, ax) if ax is not None else jnp.broadcast_to(g[None], (N_DEV,) + g.shape)
        slots.append(_pad_rows(sh.reshape(N_DEV, -1), LANES))
    send_small = _pad_rows(jnp.concatenate(slots, axis=1), LANES * 8).reshape(N_DEV, -1, LANES)
    w_in_t = grads['w_in_t']
    recv_small, recv_w_in = _exchange([send_small, w_in_t.reshape(N_DEV, -1, D_MODEL)], ['slots', 'slots'],
                                      name="exchange_last")

    g_small = _sum8(recv_small, name="sum_small")
    g_own = dict(zip(small, _unpack_blocks(g_small, like)))
    g_own['w_in'] = _sum8(recv_w_in, name="sum_w_in").T[None]
    g_own['w_out'] = _sum8(recv['w_out'], name="sum_w_out")[None]
    g_own['ffn_w_gate'] = jnp.stack([_sum8(recv['ffn_w_gate'][l], name=f"sum_ffn{l}_gate").T for l in range(2)])
    g_own['ffn_w_up'] = jnp.stack([_sum8(recv['ffn_w_up'][l], name=f"sum_ffn{l}_up").T for l in range(2)])
    g_own['ffn_w_down'] = jnp.stack([_sum8(recv['ffn_w_down'][l], name=f"sum_ffn{l}_down") for l in range(2)])

    delta, m_new, v_new = {}, {}, {}
    pk = lambda d: _pack_blocks([d[n] for n in small], 8)
    d_s, m_s, v_s = _adamw(g_small, pk(w_loc), pk(m_loc), pk(v_loc), name="adamw_small")
    for dst, slab in ((delta, d_s), (m_new, m_s), (v_new, v_s)):
        dst.update(zip(small, _unpack_blocks(slab, like)))
    for n in BIG_WEIGHTS:
        shape = w_loc[n].shape
        two_d = lambda t: t.reshape(-1, shape[-1])
        res = _adamw(two_d(g_own[n]), two_d(w_loc[n]), two_d(m_loc[n]), two_d(v_loc[n]), name=f"adamw_{n}")
        delta[n], m_new[n], v_new[n] = (t.reshape(shape) for t in res)

    loss = lax.psum(loss_part[0, 0], ("x", "y", "c"))
    outs = [d[n] for d in (g_own, delta, m_new, v_new) for n in WEIGHTS]
    return (loss, grad_x[None], *outs)


def _local_step(h0, tgt, full, shards):
    gm_ln_g, gm_ln_b, gm_ws, gm_bs = full['gm_ln_g'], full['gm_ln_b'], full['gm_ws'], full['gm_bs']
    conv_b, dt_bias, a_log, d_skip, ssm_norm_g = (full['conv_b'], full['dt_bias'], full['a_log'], full['d_skip'],
                                                  full['ssm_norm_g'])
    w_in_f = full['w_in'][0]
    w_main = jnp.concatenate([w_in_f[:, 3072:5120], w_in_f[:, :3072]], axis=1)
    w_dt = jnp.pad(w_in_f[:, 5120:], ((0, 0), (0, LANES - SSM_HEADS)))
    ng = full['norm_g']

    def ffn_shards(layer):
        return [shards['ffn_w_gate'][layer], shards['ffn_w_up'][layer], shards['ffn_w_down'][layer]]

    def ffn_weights(got_gate, got_up, got_down):
        cols = lambda g: g.transpose(1, 0, 2).reshape(D_MODEL, D_FF)
        return _interleave_ff(cols(got_gate), cols(got_up)), got_down.reshape(D_FF, D_MODEL)

    w_gu, w_dn = [None, None], [None, None]
    causal = jnp.tril(jnp.ones((CHUNK, CHUNK), bool))
    wm = jnp.where(causal[None], gm_ws[0], 0.0).astype(MXU_DTYPE)
    wm_t = jnp.swapaxes(wm, 1, 2)
    bcol = jnp.pad(gm_bs[0].T, ((0, 0), (0, LANES - GM_HEADS)))
    conv_w8 = jnp.pad(full['conv_w'][0], ((0, 8 - SSM_CONV), (0, 0)))
    dtb, alog, dsk = _row128(dt_bias[0]), _row128(a_log[0]), _row128(d_skip[0])
    pool_b_f = full['pool_b'][0].reshape(1, D_MODEL)
    pool_s_f = full['pool_scale']

    def g_(layer, i):
        return ng[layer, i].reshape(1, D_MODEL)

    yn0 = _rn_fwd(h0, g_(0, 0), name="rn_fwd_0", out_dtype=MXU_DTYPE)
    proj, got = _mm(yn0, w_main, name="mm_in_proj", tm=2048,
                    ex=_Exchange([shards['w_out'][0]] + ffn_shards(0), ['gather'] * 4))
    w_out_f = got[0].reshape(-1, D_MODEL)
    w_gu[0], w_dn[0] = ffn_weights(*got[1:])
    dtr = _mm(yn0, w_dt, name="mm_in_proj_dt")
    pre = _conv_fwd(proj, conv_w8, conv_b, name="conv_fwd")
    ya = _gmlp_fwd(proj, gm_ln_g, gm_ln_b, wm, bcol, name="gmlp_fwd")
    yb, y_ssd, states, dt_ssd, acum_ssd = _ssd_fwd(pre, dtr, proj, dtb, alog, dsk, ssm_norm_g, name="ssd_fwd")
    cat = jnp.concatenate([ya, yb], axis=1)
    o0 = _mm(cat, w_out_f, name="mm_out_proj", tm=2048, tn=1024, tk=1024)
    h1, yn1 = _resid_rn_fwd(h0, o0, g_(0, 1), g_(0, 2), name="resid_fwd_0a", next_dtype=MXU_DTYPE)
    (gu0, act0), got = _mm_swiglu(yn1, w_gu[0], name="mm_ffn0_gate_up",
                                  ex=_Exchange(ffn_shards(1) + [shards['pool_w'][0]], ['gather'] * 4))
    w_gu[1], w_dn[1] = ffn_weights(*got[:3])
    pool_w_f = got[3].transpose(1, 0, 2, 3).reshape(4, POOL_DIM, POOL_DIM)
    d0 = _mm(act0, w_dn[0], name="mm_ffn0_down", tm=2048, tn=1024, tk=1408)
    h2, yn2 = _resid_rn_fwd(h1, d0, g_(0, 3), g_(1, 0), name="resid_fwd_0b", next_dtype=F32)
    pm, pdiff = _pool_fwd(yn2, pool_w_f, pool_b_f, pool_s_f, name="pool_fwd")
    h3, yn3 = _resid_rn_fwd(h2, pm, g_(1, 1), g_(1, 2), name="resid_fwd_1a", next_dtype=MXU_DTYPE)
    gu1, act1 = _mm_swiglu(yn3, w_gu[1], name="mm_ffn1_gate_up")
    d1 = _mm(act1, w_dn[1], name="mm_ffn1_down", tm=2048, tn=1024, tk=1408)
    dh4, loss_part = _resid_loss(h3, d1, g_(1, 3), tgt, name="resid_loss")

    grads = {}
    recv = {'ffn_w_gate': [None, None], 'ffn_w_up': [None, None], 'ffn_w_down': [None, None]}
    dng = [[None] * 4 for _ in range(2)]

    def ffn_bwd(layer, dh, d_out, gu, act, yn, h_in):
        dd, dng[layer][3] = _resid_bwd_post(dh, d_out, g_(layer, 3), name=f"resid_bwd_post_{layer}b", out_dtype=MXU_DTYPE)
        dw_dn = _mm_tn(act, dd, name=f"mm_ffn{layer}_dw_down", out_dtype=MXU_DTYPE, tm=1408, tn=1024)
        dgu = _mm_dswiglu(dd, w_dn[layer].T, gu, name=f"mm_ffn{layer}_dact")
        dw_g_t, dw_u_t = _mm_tn_gate_up(dgu, yn, name=f"mm_ffn{layer}_dw_gate_up", out_dtype=MXU_DTYPE)
        dyn, got = _mm(dgu, w_gu[layer].T, name=f"mm_ffn{layer}_dyn", tm=2048, tn=1024, tk=1408,
                       ex=_Exchange([dw_g_t, dw_u_t, dw_dn], ['rows'] * 3))
        recv['ffn_w_gate'][layer], recv['ffn_w_up'][layer], recv['ffn_w_down'][layer] = got
        dh_in, dng[layer][2] = _resid_bwd_pre(dh, [dyn], h_in, g_(layer, 2), name=f"resid_bwd_pre_{layer}b")
        return dh_in

    dh3 = ffn_bwd(1, dh4, d1, gu1, act1, yn3, h3)
    dpm, dng[1][1] = _resid_bwd_post(dh3, pm, g_(1, 1), name="resid_bwd_post_1a", out_dtype=F32)
    dyn2, d_pool_w, d_pool_b, d_pool_s = _pool_bwd(dpm, pdiff, pool_w_f, jnp.swapaxes(pool_w_f, 1, 2), pool_b_f, pool_s_f,
                                                   name="pool_bwd")
    dh2, dng[1][0] = _resid_bwd_pre(dh3, [dyn2], h2, g_(1, 0), name="resid_bwd_pre_1a")
    dh1 = ffn_bwd(0, dh2, d0, gu0, act0, yn1, h1)
    do0, dng[0][1] = _resid_bwd_post(dh1, o0, g_(0, 1), name="resid_bwd_post_0a", out_dtype=MXU_DTYPE)
    d_w_out = _mm_tn(cat, do0, name="mm_out_proj_dw", out_dtype=MXU_DTYPE, tn=1024)
    dcat, got = _mm(do0, w_out_f.T, name="mm_out_proj_dx", tm=2048, tn=1024, ex=_Exchange([d_w_out], ['rows']))
    recv['w_out'] = got[0]
    du, dv, d_wm, d_bcol, d_ln_g, d_ln_b = _gmlp_bwd(proj, dcat, gm_ln_g, gm_ln_b, wm, wm_t, bcol, name="gmlp_bwd")
    dpre, dz, ddtr, d_gn, d_vec = _ssd_bwd(pre, dtr, dt_ssd, acum_ssd, proj, y_ssd, states, dcat, dtb, alog, dsk, ssm_norm_g, name="ssd_bwd")
    dxbc, d_conv_w8, d_conv_b = _conv_bwd(dpre, proj, conv_w8, name="conv_bwd")
    dproj = jnp.concatenate([dxbc, du, dv, dz], axis=1)
    d_w_main_t = _mm_tn(dproj, yn0, name="mm_in_proj_dw", out_dtype=MXU_DTYPE, tn=1024, shift=3)
    d_w_dt_t = _mm_tn(ddtr, yn0, name="mm_in_proj_dt_dw", out_dtype=MXU_DTYPE, tn=1024)
    dyn0 = _mm(dproj, w_main.T, name="mm_in_proj_dx", tm=2048, tn=1024, tk=1024)
    dyn0_dt = _mm(ddtr, w_dt.T, name="mm_in_proj_dt_dx")
    grad_x, dng[0][0] = _resid_bwd_pre(dh1, [dyn0, dyn0_dt], h0, g_(0, 0), name="resid_bwd_pre_0a")

    grads['norm_g'] = jnp.stack([jnp.concatenate(dng[l], axis=0) for l in range(2)])
    grads['w_in_t'] = jnp.concatenate([d_w_main_t, d_w_dt_t[:SSM_HEADS]], axis=0)
    grads['gm_ln_g'], grads['gm_ln_b'] = d_ln_g, d_ln_b
    grads['gm_ws'] = d_wm[None]
    grads['gm_bs'] = d_bcol[:, :GM_HEADS].T[None]
    grads['conv_w'] = d_conv_w8[None, :SSM_CONV]
    grads['conv_b'] = d_conv_b
    grads['dt_bias'] = d_vec[0:1, :SSM_HEADS]
    grads['a_log'] = d_vec[1:2, :SSM_HEADS] * (-jnp.exp(a_log))
    grads['d_skip'] = d_vec[2:3, :SSM_HEADS]
    grads['ssm_norm_g'] = d_gn
    grads['pool_w'] = d_pool_w[None]
    grads['pool_b'] = d_pool_b.reshape(1, 4, POOL_DIM)
    grads['pool_scale'] = d_pool_s
    return loss_part, grad_x, grads, recv
```

```python
import functools
import math

import jax
import jax.numpy as jnp
from jax import lax
from jax.experimental import pallas as pl
from jax.experimental.pallas import tpu as pltpu

F32 = jnp.float32
MXU_DTYPE = jnp.bfloat16

N_DEV = 8
D_MODEL = 1024
EPS = 1e-6
GM_HEADS = 4
GM_HEAD_DIM = 256
CHUNK = 128
SSM_HEADS = 16
SSM_GROUPS = 4
SSM_STATE = 128
SSM_CONV = 4
CONV_DIM = 2048
POOL_WINDOWS = (2, 4, 8, 16)
POOL_DIM = 256
D_FF = 2816
FF_TILE = 256
IN_MAIN = 5120
LANES = 128
CONV_HALO = 8
POOL_HALO = 16
ADAM_LR, ADAM_B1, ADAM_B2, ADAM_EPS, ADAM_WD, ADAM_STEP = 0.001, 0.9, 0.999, 1e-08, 0.01, 10

VMEM_LIMIT = 56 * 1024 * 1024
ROW_TILE = 512
MM_TM = 2048

WEIGHTS = ['norm_g', 'w_in', 'gm_ln_g', 'gm_ln_b', 'gm_ws', 'gm_bs', 'conv_w', 'conv_b', 'dt_bias', 'a_log',
           'd_skip', 'ssm_norm_g', 'w_out', 'pool_w', 'pool_b', 'pool_scale', 'ffn_w_gate', 'ffn_w_up', 'ffn_w_down']
SHARD_AXIS = {'norm_g': 2, 'w_in': 2, 'gm_ln_g': None, 'gm_ln_b': None, 'gm_ws': None, 'gm_bs': None, 'conv_w': 2,
              'conv_b': None, 'dt_bias': None, 'a_log': None, 'd_skip': None, 'ssm_norm_g': None, 'w_out': 1,
              'pool_w': 2, 'pool_b': 2, 'pool_scale': 1, 'ffn_w_gate': 2, 'ffn_w_up': 2, 'ffn_w_down': 1}
GATHER_BF16 = ['w_in', 'w_out', 'pool_w', 'ffn_w_gate', 'ffn_w_up', 'ffn_w_down']
GATHER_F32 = ['norm_g', 'conv_w', 'pool_b', 'pool_scale']
BIG_WEIGHTS = ['w_in', 'w_out', 'ffn_w_gate', 'ffn_w_up', 'ffn_w_down']


def _cparams(sem=None):
    return pltpu.CompilerParams(dimension_semantics=sem, vmem_limit_bytes=VMEM_LIMIT)


def _dot(a, b):
    return jnp.dot(a.astype(MXU_DTYPE), b.astype(MXU_DTYPE), preferred_element_type=F32)


def _dot_nt(a, b):
    return lax.dot_general(a.astype(MXU_DTYPE), b.astype(MXU_DTYPE), (((1,), (1,)), ((), ())),
                           preferred_element_type=F32)


def _dot_tn(a, b):
    return lax.dot_general(a.astype(MXU_DTYPE), b.astype(MXU_DTYPE), (((0,), (0,)), ((), ())),
                           preferred_element_type=F32)


def _dot_exact(a, b):
    return jnp.dot(a, b, precision=lax.Precision.HIGHEST, preferred_element_type=F32)


def _sigmoid(x):
    return 1.0 / (1.0 + jnp.exp(-x))


def _silu(x):
    return x * _sigmoid(x)


def _silu_grad(x):
    s = _sigmoid(x)
    return s * (1.0 + x * (1.0 - s))


_GELU_C = math.sqrt(2.0 / math.pi)


def _gelu(x):
    return 0.5 * x * (1.0 + jnp.tanh(_GELU_C * (x + 0.044715 * x * x * x)))


def _gelu_grad(x):
    t = jnp.tanh(_GELU_C * (x + 0.044715 * x * x * x))
    return 0.5 * (1.0 + t) + 0.5 * x * (1.0 - t * t) * _GELU_C * (1.0 + 3.0 * 0.044715 * x * x)


def _softplus(x):
    return jnp.maximum(x, 0.0) + jnp.log1p(jnp.exp(-jnp.abs(x)))


def _rms_scale(x):
    return lax.rsqrt(jnp.mean(x * x, axis=-1, keepdims=True) + EPS)


def _rms_bwd(dy, x, g):
    r = _rms_scale(x)
    xn = x * r
    dxn = dy * g
    dx = r * (dxn - xn * jnp.mean(dxn * xn, axis=-1, keepdims=True))
    return dx, dy * xn


def _colsum(x):
    return jnp.sum(x, axis=0, keepdims=True)


class _Exchange:
    def __init__(self, arrays, modes):
        self.arrays, self.modes, self.n = list(arrays), list(modes), len(arrays)
        self.blks = []
        for x, mode in zip(arrays, modes):
            if mode == 'gather':
                self.blks.append(tuple(x.shape))
            elif mode == 'slots':
                self.blks.append(tuple(x.shape[1:]))
            else:
                self.blks.append((x.shape[0] // N_DEV,) + tuple(x.shape[1:]))
        self.out_shape = [jax.ShapeDtypeStruct((N_DEV,) + blk, x.dtype) for x, blk in zip(arrays, self.blks)]
        self.in_specs = [pl.BlockSpec(memory_space=pl.ANY)] * self.n
        self.out_specs = [pl.BlockSpec(memory_space=pl.ANY) for _ in range(self.n)]
        n_sem = self.n * (N_DEV - 1)
        self.scratch = [pltpu.SemaphoreType.DMA((n_sem,)), pltpu.SemaphoreType.DMA((n_sem,)),
                        pltpu.SemaphoreType.DMA((self.n,))]

    def _copies(self, x_refs, out_refs, send_sems, recv_sems, local_sems, with_recvs):
        mx, my, mc = lax.axis_index("x"), lax.axis_index("y"), lax.axis_index("c")
        me = 4 * mx + 2 * my + mc

        def flip(v, bit):
            return 1 - v if bit else v

        def part(a, dev):
            if self.modes[a] == 'gather':
                return x_refs[a]
            if self.modes[a] == 'slots':
                return x_refs[a].at[dev]
            r = self.blks[a][0]
            return x_refs[a].at[pl.ds(pl.multiple_of(dev * r, 16), r)]

        sends, recvs, owns = [], [], []
        for k in (1, 2, 4, 6, 3, 5, 7):
            px, py, pc = flip(mx, (k >> 2) & 1), flip(my, (k >> 1) & 1), flip(mc, k & 1)
            peer = 4 * px + 2 * py + pc
            for a in range(self.n):
                sem = a * (N_DEV - 1) + k - 1
                sends.append(pltpu.make_async_remote_copy(
                    src_ref=part(a, peer), dst_ref=out_refs[a].at[me], send_sem=send_sems.at[sem],
                    recv_sem=recv_sems.at[sem], device_id=(px, py, pc), device_id_type=pl.DeviceIdType.MESH))
                if with_recvs:
                    recvs.append(pltpu.make_async_remote_copy(
                        src_ref=part(a, peer), dst_ref=out_refs[a].at[peer], send_sem=send_sems.at[sem],
                        recv_sem=recv_sems.at[sem], device_id=(px, py, pc), device_id_type=pl.DeviceIdType.MESH))
        for a in range(self.n):
            owns.append(pltpu.make_async_copy(part(a, me), out_refs[a].at[me], local_sems.at[a]))
        return sends, recvs, owns

    def start(self, *refs):
        sends, _, owns = self._copies(*refs, with_recvs=False)
        for cp in sends + owns:
            cp.start()

    def wait(self, *refs):
        sends, recvs, owns = self._copies(*refs, with_recvs=True)
        for cp in recvs:
            cp.wait_recv()
        for cp in sends:
            cp.wait_send()
        for cp in owns:
            cp.wait()


def _exchange(arrays, modes, *, name):
    ex = _Exchange(arrays, modes)

    def body(*refs):
        x_refs, out_refs, sems = refs[:ex.n], refs[ex.n:2 * ex.n], refs[2 * ex.n:]
        ex.start(x_refs, out_refs, *sems)
        ex.wait(x_refs, out_refs, *sems)

    return pl.pallas_call(
        body, name=name, out_shape=tuple(ex.out_shape), in_specs=ex.in_specs, out_specs=tuple(ex.out_specs),
        scratch_shapes=ex.scratch,
    )(*arrays)


def _hosted(body, n_in, n_out, n_scratch, grid, ex):
    def wrapped(*refs):
        ins, x_refs = refs[:n_in], refs[n_in:n_in + ex.n]
        outs = refs[n_in + ex.n:n_in + ex.n + n_out]
        xo_refs = refs[n_in + ex.n + n_out:n_in + 2 * ex.n + n_out]
        scr = refs[n_in + 2 * ex.n + n_out:n_in + 2 * ex.n + n_out + n_scratch]
        sems = refs[n_in + 2 * ex.n + n_out + n_scratch:]
        ids = [pl.program_id(d) for d in range(len(grid))]
        first = functools.reduce(jnp.logical_and, [i == 0 for i in ids])
        last = functools.reduce(jnp.logical_and, [i == g - 1 for i, g in zip(ids, grid)])

        @pl.when(first)
        def _():
            ex.start(x_refs, xo_refs, *sems)

        body(*ins, *outs, *scr)

        @pl.when(last)
        def _():
            ex.wait(x_refs, xo_refs, *sems)

    return wrapped


def _call(body, *, name, grid, inputs, in_specs, out_shape, out_specs, scratch, semantics, ex=None):
    if ex is None:
        return pl.pallas_call(
            body, name=name, grid=grid, out_shape=tuple(out_shape), in_specs=list(in_specs),
            out_specs=tuple(out_specs), scratch_shapes=list(scratch), compiler_params=_cparams(semantics))(*inputs)
    n_out = len(out_shape)
    res = pl.pallas_call(
        _hosted(body, len(inputs), n_out, len(scratch), grid, ex), name=name, grid=grid,
        out_shape=tuple(out_shape) + tuple(ex.out_shape), in_specs=list(in_specs) + ex.in_specs,
        out_specs=tuple(out_specs) + tuple(ex.out_specs), scratch_shapes=list(scratch) + ex.scratch,
        compiler_params=_cparams(("arbitrary",) * len(grid)))(*inputs, *ex.arrays)
    return res[:n_out], res[n_out:]


def _mm(a, b, *, name, out_dtype=F32, tm=MM_TM, tn=512, tk=None, ex=None):
    m, k = a.shape
    n = b.shape[1]
    tm, tn = min(tm, m), min(tn, n)
    tk = k if tk is None else tk
    nk = k // tk
    assert m % tm == 0 and n % tn == 0 and k % tk == 0

    def body(a_ref, b_ref, o_ref, acc_ref):
        kk = pl.program_id(2)
        part = _dot(a_ref[...], b_ref[...])
        if nk == 1:
            o_ref[...] = part.astype(out_dtype)
        else:
            @pl.when(kk == 0)
            def _():
                acc_ref[...] = part

            @pl.when(kk > 0)
            def _():
                acc_ref[...] += part

            @pl.when(kk == nk - 1)
            def _():
                o_ref[...] = acc_ref[...].astype(out_dtype)

    res = _call(
        body, name=name, grid=(m // tm, n // tn, nk), inputs=(a, b),
        in_specs=[pl.BlockSpec((tm, tk), lambda i, j, kk: (i, kk)), pl.BlockSpec((tk, tn), lambda i, j, kk: (kk, j))],
        out_shape=[jax.ShapeDtypeStruct((m, n), out_dtype)],
        out_specs=[pl.BlockSpec((tm, tn), lambda i, j, kk: (i, j))],
        scratch=[pltpu.VMEM((tm, tn) if nk > 1 else (8, LANES), F32)],
        semantics=("parallel", "parallel", "arbitrary"), ex=ex)
    return res[0] if ex is None else (res[0][0], res[1])


def _mm_tn(a, b, *, name, out_dtype=F32, tm=1024, tn=512, tk=1024, shift=0):
    t, m = a.shape
    n = b.shape[1]
    tm, tn, tk = min(tm, m), min(tn, n), min(tk, t)
    nk = t // tk
    nb = m // tm
    assert m % tm == 0 and n % tn == 0 and t % tk == 0

    def body(a_ref, b_ref, o_ref, acc_ref):
        kk = pl.program_id(2)
        part = _dot_tn(a_ref[...], b_ref[...])

        @pl.when(kk == 0)
        def _():
            acc_ref[...] = part

        @pl.when(kk > 0)
        def _():
            acc_ref[...] += part

        @pl.when(kk == nk - 1)
        def _():
            o_ref[...] = acc_ref[...].astype(out_dtype)

    return pl.pallas_call(
        body, name=name, grid=(nb, n // tn, nk),
        out_shape=jax.ShapeDtypeStruct((m, n), out_dtype),
        in_specs=[pl.BlockSpec((tk, tm), lambda i, j, kk: (kk, i)), pl.BlockSpec((tk, tn), lambda i, j, kk: (kk, j))],
        out_specs=pl.BlockSpec((tm, tn), lambda i, j, kk: ((i + shift) % nb, j)),
        scratch_shapes=[pltpu.VMEM((tm, tn), F32)],
        compiler_params=_cparams(("parallel", "parallel", "arbitrary")),
    )(a, b)


def _mm_tn_gate_up(dgu, yn, *, name, out_dtype, tk=2048):
    t, m = dgu.shape
    n = yn.shape[1]
    tk = min(tk, t)
    nk = t // tk
    nb = m // (2 * FF_TILE)

    def body(a_ref, b_ref, og_ref, ou_ref, acc_ref):
        kk = pl.program_id(1)
        part = _dot_tn(a_ref[...], b_ref[...])

        @pl.when(kk == 0)
        def _():
            acc_ref[...] = part

        @pl.when(kk > 0)
        def _():
            acc_ref[...] += part

        @pl.when(kk == nk - 1)
        def _():
            og_ref[...] = acc_ref[:FF_TILE, :].astype(out_dtype)
            ou_ref[...] = acc_ref[FF_TILE:, :].astype(out_dtype)

    out = jax.ShapeDtypeStruct((m // 2, n), out_dtype)
    o_spec = pl.BlockSpec((FF_TILE, n), lambda i, kk: (i, 0))
    return pl.pallas_call(
        body, name=name, grid=(nb, nk), out_shape=(out, out),
        in_specs=[pl.BlockSpec((tk, 2 * FF_TILE), lambda i, kk: (kk, i)), pl.BlockSpec((tk, n), lambda i, kk: (kk, 0))],
        out_specs=(o_spec, o_spec),
        scratch_shapes=[pltpu.VMEM((2 * FF_TILE, n), F32)],
        compiler_params=_cparams(("parallel", "arbitrary")),
    )(dgu, yn)


def _mm_swiglu(a, w_gu, *, name, tm=MM_TM, ex=None):
    m, k = a.shape
    n = w_gu.shape[1]
    nt = n // (2 * FF_TILE)
    tm = min(tm, m)

    def body(a_ref, b_ref, gu_ref, act_ref):
        gu = _dot(a_ref[...], b_ref[...])
        gu_ref[...] = gu
        act_ref[...] = (_silu(gu[:, :FF_TILE]) * gu[:, FF_TILE:]).astype(MXU_DTYPE)

    return _call(
        body, name=name, grid=(m // tm, nt), inputs=(a, w_gu),
        in_specs=[pl.BlockSpec((tm, k), lambda i, j: (i, 0)), pl.BlockSpec((k, 2 * FF_TILE), lambda i, j: (0, j))],
        out_shape=[jax.ShapeDtypeStruct((m, n), F32), jax.ShapeDtypeStruct((m, n // 2), MXU_DTYPE)],
        out_specs=[pl.BlockSpec((tm, 2 * FF_TILE), lambda i, j: (i, j)), pl.BlockSpec((tm, FF_TILE), lambda i, j: (i, j))],
        scratch=[], semantics=("parallel", "parallel"), ex=ex)


def _mm_dswiglu(dd, w_down_t, gu, *, name, tm=MM_TM):
    m, k = dd.shape
    n = gu.shape[1]
    nt = n // (2 * FF_TILE)
    tm = min(tm, m)

    def body(d_ref, w_ref, gu_ref, o_ref):
        dact = _dot(d_ref[...], w_ref[...])
        gate, up = gu_ref[:, :FF_TILE], gu_ref[:, FF_TILE:]
        o_ref[:, :FF_TILE] = (dact * up * _silu_grad(gate)).astype(MXU_DTYPE)
        o_ref[:, FF_TILE:] = (dact * _silu(gate)).astype(MXU_DTYPE)

    return pl.pallas_call(
        body, name=name, grid=(m // tm, nt),
        out_shape=jax.ShapeDtypeStruct((m, n), MXU_DTYPE),
        in_specs=[pl.BlockSpec((tm, k), lambda i, j: (i, 0)), pl.BlockSpec((k, FF_TILE), lambda i, j: (0, j)),
                  pl.BlockSpec((tm, 2 * FF_TILE), lambda i, j: (i, j))],
        out_specs=pl.BlockSpec((tm, 2 * FF_TILE), lambda i, j: (i, j)),
        compiler_params=_cparams(("parallel", "parallel")),
    )(dd, w_down_t, gu)


def _row_spec(width, tr=ROW_TILE):
    return pl.BlockSpec((tr, width), lambda i: (i, 0))


def _vec_spec(width, rows=1):
    return pl.BlockSpec((rows, width), lambda i: (0, 0))


def _rn_fwd(h, g, *, name, out_dtype):
    rows, d = h.shape

    def body(h_ref, g_ref, o_ref):
        x = h_ref[...]
        o_ref[...] = (x * _rms_scale(x) * g_ref[...]).astype(out_dtype)

    return pl.pallas_call(
        body, name=name, grid=(rows // ROW_TILE,),
        out_shape=jax.ShapeDtypeStruct((rows, d), out_dtype),
        in_specs=[_row_spec(d), _vec_spec(d)], out_specs=_row_spec(d),
        compiler_params=_cparams(("parallel",)),
    )(h, g)


def _resid_rn_fwd(h_in, o, g_post, g_next, *, name, next_dtype):
    rows, d = h_in.shape

    def body(h_ref, o_ref, gp_ref, gn_ref, ho_ref, yn_ref):
        ov = o_ref[...]
        h = h_ref[...] + ov * _rms_scale(ov) * gp_ref[...]
        ho_ref[...] = h
        yn_ref[...] = (h * _rms_scale(h) * gn_ref[...]).astype(next_dtype)

    return pl.pallas_call(
        body, name=name, grid=(rows // ROW_TILE,),
        out_shape=(jax.ShapeDtypeStruct((rows, d), F32), jax.ShapeDtypeStruct((rows, d), next_dtype)),
        in_specs=[_row_spec(d), _row_spec(d), _vec_spec(d), _vec_spec(d)],
        out_specs=(_row_spec(d), _row_spec(d)),
        compiler_params=_cparams(("parallel",)),
    )(h_in, o, g_post, g_next)


def _resid_loss(h_in, o, g_post, target, *, name):
    rows, d = h_in.shape

    def body(h_ref, o_ref, gp_ref, t_ref, dh_ref, loss_ref):
        ov = o_ref[...]
        err = h_ref[...] + ov * _rms_scale(ov) * gp_ref[...] - t_ref[...]
        dh_ref[...] = err * (1.0 / d)

        @pl.when(pl.program_id(0) == 0)
        def _():
            loss_ref[...] = jnp.zeros_like(loss_ref)

        loss_ref[...] += 0.5 * jnp.sum(jnp.mean(err * err, axis=-1, keepdims=True), axis=0, keepdims=True)

    return pl.pallas_call(
        body, name=name, grid=(rows // ROW_TILE,),
        out_shape=(jax.ShapeDtypeStruct((rows, d), F32), jax.ShapeDtypeStruct((1, 1), F32)),
        in_specs=[_row_spec(d), _row_spec(d), _vec_spec(d), _row_spec(d)],
        out_specs=(_row_spec(d), pl.BlockSpec((1, 1), lambda i: (0, 0))),
        compiler_params=_cparams(("arbitrary",)),
    )(h_in, o, g_post, target)


def _resid_bwd_post(dh, o, g_post, *, name, out_dtype):
    rows, d = dh.shape

    def body(dh_ref, o_ref, g_ref, do_ref, dg_ref):
        do, dg = _rms_bwd(dh_ref[...], o_ref[...], g_ref[...])
        do_ref[...] = do.astype(out_dtype)

        @pl.when(pl.program_id(0) == 0)
        def _():
            dg_ref[...] = jnp.zeros_like(dg_ref)

        dg_ref[...] += _colsum(dg)

    return pl.pallas_call(
        body, name=name, grid=(rows // ROW_TILE,),
        out_shape=(jax.ShapeDtypeStruct((rows, d), out_dtype), jax.ShapeDtypeStruct((1, d), F32)),
        in_specs=[_row_spec(d), _row_spec(d), _vec_spec(d)],
        out_specs=(_row_spec(d), _vec_spec(d)),
        compiler_params=_cparams(("arbitrary",)),
    )(dh, o, g_post)


def _resid_bwd_pre(dh, dyn_list, h_in, g_pre, *, name):
    rows, d = dh.shape
    n_dyn = len(dyn_list)

    def body(*refs):
        dh_ref, dyn_refs, h_ref, g_ref, out_ref, dg_ref = refs[0], refs[1:1 + n_dyn], *refs[1 + n_dyn:]
        dyn = dyn_refs[0][...]
        for r in dyn_refs[1:]:
            dyn = dyn + r[...]
        dx, dg = _rms_bwd(dyn, h_ref[...], g_ref[...])
        out_ref[...] = dh_ref[...] + dx

        @pl.when(pl.program_id(0) == 0)
        def _():
            dg_ref[...] = jnp.zeros_like(dg_ref)

        dg_ref[...] += _colsum(dg)

    return pl.pallas_call(
        body, name=name, grid=(rows // ROW_TILE,),
        out_shape=(jax.ShapeDtypeStruct((rows, d), F32), jax.ShapeDtypeStruct((1, d), F32)),
        in_specs=[_row_spec(d)] + [_row_spec(d)] * n_dyn + [_row_spec(d), _vec_spec(d)],
        out_specs=(_row_spec(d), _vec_spec(d)),
        compiler_params=_cparams(("arbitrary",)),
    )(dh, *dyn_list, h_in, g_pre)


def _layer_norm_stats(x):
    mu = jnp.mean(x, axis=-1, keepdims=True)
    xc = x - mu
    rstd = lax.rsqrt(jnp.mean(xc * xc, axis=-1, keepdims=True) + EPS)
    return xc * rstd, rstd


def _gmlp_fwd(proj, ln_g, ln_b, wm, bcol, *, name):
    rows = proj.shape[0]
    tr = ROW_TILE

    def body(u_ref, v_ref, lg_ref, lb_ref, wm_ref, bc_ref, ya_ref):
        vhat, _ = _layer_norm_stats(_gelu(v_ref[...]))
        vl = (vhat * lg_ref[...] + lb_ref[...]).astype(MXU_DTYPE)
        gu = _gelu(u_ref[...])
        bc = bc_ref[...]
        for c in range(tr // CHUNK):
            rs = slice(c * CHUNK, (c + 1) * CHUNK)
            for h in range(GM_HEADS):
                cs = slice(h * GM_HEAD_DIM, (h + 1) * GM_HEAD_DIM)
                mixed = _dot(wm_ref[h], vl[rs, cs]) + bc[:, h:h + 1]
                ya_ref[rs, cs] = (gu[rs, cs] * mixed).astype(MXU_DTYPE)

    return pl.pallas_call(
        body, name=name, grid=(rows // tr,),
        out_shape=jax.ShapeDtypeStruct((rows, 2 * D_MODEL), MXU_DTYPE),
        in_specs=[pl.BlockSpec((tr, D_MODEL), lambda i: (i, 2)), pl.BlockSpec((tr, D_MODEL), lambda i: (i, 3)),
                  _vec_spec(D_MODEL), _vec_spec(D_MODEL),
                  pl.BlockSpec((GM_HEADS, CHUNK, CHUNK), lambda i: (0, 0, 0)), _vec_spec(LANES, CHUNK)],
        out_specs=_row_spec(D_MODEL, tr),
        compiler_params=_cparams(("parallel",)),
    )(proj, proj, ln_g, ln_b, wm, bcol)


def _gmlp_bwd(proj, dcat, ln_g, ln_b, wm, wm_t, bcol, *, name):
    rows = proj.shape[0]
    tr = ROW_TILE

    def body(u_ref, v_ref, dy_ref, lg_ref, lb_ref, wm_ref, wmt_ref, bc_ref,
             duv_ref, dwm_ref, dbc_ref, dlg_ref, dlb_ref, dvl_scr):
        @pl.when(pl.program_id(0) == 0)
        def _():
            dwm_ref[...] = jnp.zeros_like(dwm_ref)
            dbc_ref[...] = jnp.zeros_like(dbc_ref)
            dlg_ref[...] = jnp.zeros_like(dlg_ref)
            dlb_ref[...] = jnp.zeros_like(dlb_ref)

        u, v = u_ref[...], v_ref[...]
        gv = _gelu(v)
        vhat, rstd = _layer_norm_stats(gv)
        lg = lg_ref[...]
        vl = (vhat * lg + lb_ref[...]).astype(MXU_DTYPE)
        gu = _gelu(u)
        dy = dy_ref[...]
        bc = bc_ref[...]
        row = lax.broadcasted_iota(jnp.int32, (CHUNK, CHUNK), 0)
        lane = lax.broadcasted_iota(jnp.int32, (CHUNK, CHUNK), 1)
        causal = lane <= row
        dbc = jnp.zeros((CHUNK, LANES), F32)
        for c in range(tr // CHUNK):
            rs = slice(c * CHUNK, (c + 1) * CHUNK)
            for h in range(GM_HEADS):
                cs = slice(h * GM_HEAD_DIM, (h + 1) * GM_HEAD_DIM)
                vl_h = vl[rs, cs]
                mixed = _dot(wm_ref[h], vl_h) + bc[:, h:h + 1]
                dy_h = dy[rs, cs]
                duv_ref[rs, cs] = (dy_h * mixed * _gelu_grad(u[rs, cs])).astype(MXU_DTYPE)
                dmixed = dy_h * gu[rs, cs]
                dwm_ref[h] += jnp.where(causal, _dot_nt(dmixed, vl_h), 0.0)
                dbc = dbc + jnp.where(lane == h, jnp.sum(dmixed, axis=1, keepdims=True), 0.0)
                dvl_scr[rs, cs] = _dot(wmt_ref[h], dmixed)
        dbc_ref[...] += dbc
        dvl = dvl_scr[...]
        dlg_ref[...] += _colsum(dvl * vhat)
        dlb_ref[...] += _colsum(dvl)
        dvh = dvl * lg
        dgv = rstd * (dvh - jnp.mean(dvh, axis=-1, keepdims=True) - vhat * jnp.mean(dvh * vhat, axis=-1, keepdims=True))
        duv_ref[:, D_MODEL:] = (dgv * _gelu_grad(v)).astype(MXU_DTYPE)

    return pl.pallas_call(
        body, name=name, grid=(rows // tr,),
        out_shape=(jax.ShapeDtypeStruct((rows, IN_MAIN), MXU_DTYPE),
                   jax.ShapeDtypeStruct((GM_HEADS, CHUNK, CHUNK), F32), jax.ShapeDtypeStruct((CHUNK, LANES), F32),
                   jax.ShapeDtypeStruct((1, D_MODEL), F32), jax.ShapeDtypeStruct((1, D_MODEL), F32)),
        in_specs=[pl.BlockSpec((tr, D_MODEL), lambda i: (i, 2)), pl.BlockSpec((tr, D_MODEL), lambda i: (i, 3)),
                  pl.BlockSpec((tr, D_MODEL), lambda i: (i, 0)), _vec_spec(D_MODEL), _vec_spec(D_MODEL),
                  pl.BlockSpec((GM_HEADS, CHUNK, CHUNK), lambda i: (0, 0, 0)),
                  pl.BlockSpec((GM_HEADS, CHUNK, CHUNK), lambda i: (0, 0, 0)), _vec_spec(LANES, CHUNK)],
        out_specs=(pl.BlockSpec((tr, 2 * D_MODEL), lambda i: (i, 1)),
                   pl.BlockSpec((GM_HEADS, CHUNK, CHUNK), lambda i: (0, 0, 0)), _vec_spec(LANES, CHUNK),
                   _vec_spec(D_MODEL), _vec_spec(D_MODEL)),
        scratch_shapes=[pltpu.VMEM((tr, D_MODEL), F32)],
        compiler_params=_cparams(("arbitrary",)),
    )(proj, proj, dcat, ln_g, ln_b, wm, wm_t, bcol)


def _conv_fwd(proj, conv_w8, conv_b, *, name):
    rows = proj.shape[0]
    tr = ROW_TILE
    hb = tr // CONV_HALO

    def body(x_ref, prev_ref, w_ref, b_ref, pre_ref, buf):
        first = pl.program_id(0) == 0
        buf[pl.ds(0, CONV_HALO), :] = jnp.where(first, 0.0, prev_ref[...])
        buf[pl.ds(CONV_HALO, tr), :] = x_ref[...]
        acc = jnp.broadcast_to(b_ref[...], (tr, CONV_DIM))
        for k in range(SSM_CONV):
            acc = acc + w_ref[k:k + 1, :] * buf[pl.ds(CONV_HALO - (SSM_CONV - 1) + k, tr), :]
        pre_ref[...] = acc

    return pl.pallas_call(
        body, name=name, grid=(rows // tr,),
        out_shape=jax.ShapeDtypeStruct((rows, CONV_DIM), F32),
        in_specs=[pl.BlockSpec((tr, CONV_DIM), lambda i: (i, 0)),
                  pl.BlockSpec((CONV_HALO, CONV_DIM), lambda i: (jnp.maximum(i * hb - 1, 0), 0)),
                  _vec_spec(CONV_DIM, 8), _vec_spec(CONV_DIM)],
        out_specs=_row_spec(CONV_DIM, tr),
        scratch_shapes=[pltpu.VMEM((tr + CONV_HALO, CONV_DIM), F32)],
        compiler_params=_cparams(("parallel",)),
    )(proj, proj, conv_w8, conv_b)


def _conv_bwd(dpre, proj, conv_w8, dproj, *, name):
    rows = proj.shape[0]
    tr = ROW_TILE
    hb = tr // CONV_HALO
    nblk = rows // tr

    def body(d_ref, dnext_ref, x_ref, prev_ref, w_ref, dproj_ref, dx_ref, dw_ref, db_ref, dbuf, xbuf):
        i = pl.program_id(0)

        @pl.when(i == 0)
        def _():
            dw_ref[...] = jnp.zeros_like(dw_ref)
            db_ref[...] = jnp.zeros_like(db_ref)

        d = d_ref[...]
        dbuf[pl.ds(0, tr), :] = d
        dbuf[pl.ds(tr, CONV_HALO), :] = jnp.where(i == nblk - 1, 0.0, dnext_ref[...])
        xbuf[pl.ds(0, CONV_HALO), :] = jnp.where(i == 0, 0.0, prev_ref[...])
        xbuf[pl.ds(CONV_HALO, tr), :] = x_ref[...]
        acc = jnp.zeros((tr, CONV_DIM), F32)
        for k in range(SSM_CONV):
            acc = acc + w_ref[k:k + 1, :] * dbuf[pl.ds(SSM_CONV - 1 - k, tr), :]
            dw_ref[k:k + 1, :] += _colsum(d * xbuf[pl.ds(CONV_HALO - (SSM_CONV - 1) + k, tr), :])
        dx_ref[...] = acc.astype(MXU_DTYPE)
        db_ref[...] += _colsum(d)

    return pl.pallas_call(
        body, name=name, grid=(nblk,),
        out_shape=(jax.ShapeDtypeStruct((rows, IN_MAIN), MXU_DTYPE), jax.ShapeDtypeStruct((8, CONV_DIM), F32),
                   jax.ShapeDtypeStruct((1, CONV_DIM), F32)),
        in_specs=[_row_spec(CONV_DIM, tr),
                  pl.BlockSpec((CONV_HALO, CONV_DIM), lambda i: (jnp.minimum((i + 1) * hb, rows // CONV_HALO - 1), 0)),
                  pl.BlockSpec((tr, CONV_DIM), lambda i: (i, 0)),
                  pl.BlockSpec((CONV_HALO, CONV_DIM), lambda i: (jnp.maximum(i * hb - 1, 0), 0)),
                  _vec_spec(CONV_DIM, 8), pl.BlockSpec(memory_space=pl.ANY)],
        out_specs=(_row_spec(CONV_DIM, tr), _vec_spec(CONV_DIM, 8), _vec_spec(CONV_DIM)),
        scratch_shapes=[pltpu.VMEM((tr + CONV_HALO, CONV_DIM), F32), pltpu.VMEM((tr + CONV_HALO, CONV_DIM), F32)],
        input_output_aliases={5: 0},
        compiler_params=_cparams(("arbitrary",)),
    )(dpre, dpre, proj, proj, conv_w8, dproj)


N_PAIRS = SSM_HEADS // 2


def _chunk_iotas():
    row = lax.broadcasted_iota(jnp.int32, (CHUNK, CHUNK), 0)
    lane = lax.broadcasted_iota(jnp.int32, (CHUNK, CHUNK), 1)
    return row, lane, lane <= row


def _silu_and_grad(x):
    s = _sigmoid(x)
    return x * s, s * (1.0 + x * (1.0 - s))


def _pair_select(lo, mat, ha):
    return jnp.where(lo, mat[:, ha:ha + 1], mat[:, ha + 1:ha + 2])


def _ssd_fwd(pre, dtr, proj, dtb, alog, dsk, gn, *, name):
    rows = pre.shape[0]
    nc = rows // CHUNK

    def body(pre_ref, dtr_ref, z_ref, dtb_ref, alog_ref, dsk_ref, gn_ref, yb_ref, y_ref, st_ref, dt_ref, acum_ref, s_scr):
        @pl.when(pl.program_id(0) == 0)
        def _():
            s_scr[...] = jnp.zeros_like(s_scr)

        row, lane, tril = _chunk_iotas()
        dt = _softplus(dtr_ref[...] + dtb_ref[...])
        acum = _dot_exact(tril.astype(F32), dt * (-jnp.exp(alog_ref[...])))
        dt_ref[...] = dt
        acum_ref[...] = acum
        acum_t = acum.T
        lo = lane < 64
        eacum = jnp.exp(acum)
        a_end = acum[CHUNK - 1:CHUNK, :]
        e_end = jnp.exp(a_end)
        dte_all = jnp.exp(a_end - acum)
        dsk_v = dsk_ref[...]
        for g in range(SSM_GROUPS):
            b_g = _silu(pre_ref[:, 1024 + SSM_STATE * g:1024 + SSM_STATE * (g + 1)]).astype(MXU_DTYPE)
            c_g = _silu(pre_ref[:, 1536 + SSM_STATE * g:1536 + SSM_STATE * (g + 1)]).astype(MXU_DTYPE)
            cb = _dot_nt(c_g, b_g)
            gated = []
            for jj in range(2):
                j = 2 * g + jj
                ha = 2 * j
                cs = slice(LANES * j, LANES * (j + 1))
                xs = _silu(pre_ref[:, cs])
                xdt = xs * _pair_select(lo, dt, ha)
                xdt_m = xdt.astype(MXU_DTYPE)
                y_heads = []
                for h in (ha, ha + 1):
                    dec = jnp.exp(jnp.where(tril, acum[:, h:h + 1] - acum_t[h:h + 1, :], -jnp.inf))
                    y_heads.append(_dot(cb * dec, xdt_m))
                s_prev = s_scr[j]
                st_ref[0, j] = s_prev
                y = jnp.where(lo, y_heads[0], y_heads[1])
                y = y + _dot_nt(c_g, s_prev) * _pair_select(lo, eacum, ha)
                y = y + _pair_select(lo, dsk_v, ha) * xs
                xw = xdt * _pair_select(lo, dte_all, ha)
                e_rows = jnp.where(row < 64, e_end[:, ha:ha + 1], e_end[:, ha + 1:ha + 2])
                s_scr[j] = e_rows * s_prev + _dot(xw.T, b_g)
                y_ref[:, cs] = y
                gated.append(y * _silu(z_ref[:, cs]))
            ms = (jnp.sum(gated[0] * gated[0], axis=1, keepdims=True)
                  + jnp.sum(gated[1] * gated[1], axis=1, keepdims=True)) * (1.0 / 256.0)
            r = lax.rsqrt(ms + EPS)
            for jj in range(2):
                cs = slice(LANES * (2 * g + jj), LANES * (2 * g + jj + 1))
                yb_ref[:, cs] = (gated[jj] * r * gn_ref[:, cs]).astype(MXU_DTYPE)

    return pl.pallas_call(
        body, name=name, grid=(nc,),
        out_shape=(jax.ShapeDtypeStruct((rows, D_MODEL), MXU_DTYPE), jax.ShapeDtypeStruct((rows, D_MODEL), F32),
                   jax.ShapeDtypeStruct((nc, N_PAIRS, LANES, SSM_STATE), F32),
                   jax.ShapeDtypeStruct((rows, LANES), F32), jax.ShapeDtypeStruct((rows, LANES), F32)),
        in_specs=[_row_spec(CONV_DIM, CHUNK), _row_spec(LANES, CHUNK), pl.BlockSpec((CHUNK, D_MODEL), lambda i: (i, 4)),
                  _vec_spec(LANES), _vec_spec(LANES), _vec_spec(LANES), _vec_spec(D_MODEL)],
        out_specs=(_row_spec(D_MODEL, CHUNK), _row_spec(D_MODEL, CHUNK),
                   pl.BlockSpec((1, N_PAIRS, LANES, SSM_STATE), lambda i: (i, 0, 0, 0)),
                   _row_spec(LANES, CHUNK), _row_spec(LANES, CHUNK)),
        scratch_shapes=[pltpu.VMEM((N_PAIRS, LANES, SSM_STATE), F32)],
        compiler_params=_cparams(("arbitrary",)),
    )(pre, dtr, proj, dtb, alog, dsk, gn)


def _ssd_bwd(pre, dtr, dt_saved, acum_saved, proj, y_saved, states, dcat, dtb, alog, dsk, gn, *, name):
    rows = pre.shape[0]
    nc = rows // CHUNK

    def rev(i):
        return nc - 1 - i

    def body(pre_ref, dtr_ref, dt_ref, acum_ref, z_ref, y_ref, st_ref, dyb_ref, dtb_ref, alog_ref, dsk_ref, gn_ref,
             dpre_ref, dz_ref, ddtr_ref, dgn_ref, dvec_ref, g_scr):
        @pl.when(pl.program_id(0) == 0)
        def _():
            g_scr[...] = jnp.zeros_like(g_scr)
            dgn_ref[...] = jnp.zeros_like(dgn_ref)
            dvec_ref[...] = jnp.zeros_like(dvec_ref)

        dtb = dtb_ref[...]
        dtr = dtr_ref[...]
        row, lane, tril = _chunk_iotas()
        dt, acum = dt_ref[...], acum_ref[...]
        a = -jnp.exp(alog_ref[...])
        acum_t = acum.T
        lo = lane < 64
        eacum = jnp.exp(acum)
        a_end = acum[CHUNK - 1:CHUNK, :]
        e_end = jnp.exp(a_end)
        dte_all = jnp.exp(a_end - acum)
        dsk_v = dsk_ref[...]
        zero = jnp.zeros((CHUNK, LANES), F32)
        dacum_c, dacum_r, ddt_c = zero, zero, zero
        d_aend = jnp.zeros((1, LANES), F32)
        d_dsk = jnp.zeros((1, LANES), F32)
        lane1 = lane[0:1, :]

        def put_col(acc, h, colvec):
            return acc + jnp.where(lane == h, colvec, 0.0)

        for g in range(SSM_GROUPS):
            gated, sz, dgh = [], [], []
            for jj in range(2):
                cs = slice(LANES * (2 * g + jj), LANES * (2 * g + jj + 1))
                sz.append(_silu_and_grad(z_ref[:, cs]))
                gated.append(y_ref[:, cs] * sz[jj][0])
                dgh.append(dyb_ref[:, cs] * gn_ref[:, cs])
            ms = (jnp.sum(gated[0] * gated[0], axis=1, keepdims=True)
                  + jnp.sum(gated[1] * gated[1], axis=1, keepdims=True)) * (1.0 / 256.0)
            r = lax.rsqrt(ms + EPS)
            proj_g = (jnp.sum(dgh[0] * gated[0], axis=1, keepdims=True)
                      + jnp.sum(dgh[1] * gated[1], axis=1, keepdims=True)) * (1.0 / 256.0)
            dys = []
            for jj in range(2):
                cs = slice(LANES * (2 * g + jj), LANES * (2 * g + jj + 1))
                dgn_ref[:, cs] += _colsum(dyb_ref[:, cs] * gated[jj] * r)
                dgated = r * dgh[jj] - gated[jj] * (r * r * r * proj_g)
                dys.append(dgated * sz[jj][0])
                dz_ref[:, cs] = (dgated * y_ref[:, cs] * sz[jj][1]).astype(MXU_DTYPE)

            b_f, b_grad = _silu_and_grad(pre_ref[:, 1024 + SSM_STATE * g:1024 + SSM_STATE * (g + 1)])
            c_f, c_grad = _silu_and_grad(pre_ref[:, 1536 + SSM_STATE * g:1536 + SSM_STATE * (g + 1)])
            b_g = b_f.astype(MXU_DTYPE)
            c_g = c_f.astype(MXU_DTYPE)
            cb = _dot_nt(c_g, b_g)
            dcb = zero
            db_g, dc_g = zero, zero
            for jj in range(2):
                j = 2 * g + jj
                ha = 2 * j
                cs = slice(LANES * j, LANES * (j + 1))
                xs, xs_grad = _silu_and_grad(pre_ref[:, cs])
                dtsel = _pair_select(lo, dt, ha)
                xdt = xs * dtsel
                xdt_m = xdt.astype(MXU_DTYPE)
                dyp = dys[jj]
                dyp_m = dyp.astype(MXU_DTYPE)
                s_prev = st_ref[0, j]
                g_next = g_scr[j]
                eac = _pair_select(lo, eacum, ha)
                dte = _pair_select(lo, dte_all, ha)
                yoff = _dot_nt(c_g, s_prev) * eac
                t_off = dyp * yoff
                dye = dyp * eac
                dc_g = dc_g + _dot(dye, s_prev)
                bg = _dot_nt(b_g, g_next)
                dxdt = bg * dte
                xw = xdt * dte
                db_g = db_g + _dot(xw, g_next)
                t_w = xw * bg
                gs = g_next * s_prev
                e_rows = jnp.where(row < 64, e_end[:, ha:ha + 1], e_end[:, ha + 1:ha + 2])
                g_scr[j] = e_rows * g_next + _dot(dye.T, c_g)
                dxdt_heads = []
                for hh, h in enumerate((ha, ha + 1)):
                    half = slice(64 * hh, 64 * (hh + 1))
                    dec = jnp.exp(jnp.where(tril, acum[:, h:h + 1] - acum_t[h:h + 1, :], -jnp.inf))
                    m_h = cb * dec
                    dy_h = jnp.where(lo if hh == 0 else jnp.logical_not(lo), dyp, 0.0)
                    dm = _dot_nt(dy_h, xdt_m)
                    dxdt_heads.append(_dot(m_h.T, dyp_m))
                    e_h = dm * m_h
                    dcb = dcb + dm * dec
                    w_col = jnp.sum(t_w[:, half], axis=1, keepdims=True)
                    col = (jnp.sum(e_h, axis=1, keepdims=True) + jnp.sum(t_off[:, half], axis=1, keepdims=True) - w_col)
                    dacum_c = put_col(dacum_c, h, col)
                    dacum_r = dacum_r + jnp.where(row == h, _colsum(e_h), 0.0)
                    d_end_h = jnp.sum(w_col, keepdims=True) + e_end[:, h:h + 1] * jnp.sum(gs[half, :], keepdims=True)
                    d_aend = d_aend + jnp.where(lane1 == h, d_end_h, 0.0)
                dxdt = dxdt + jnp.where(lo, dxdt_heads[0], dxdt_heads[1])
                dsel = _pair_select(lo, dsk_v, ha)
                dxs = dxdt * dtsel + dsel * dyp
                dpre_ref[:, cs] = dxs * xs_grad
                t_dt = dxdt * xs
                t_dk = dyp * xs
                for hh, h in enumerate((ha, ha + 1)):
                    half = slice(64 * hh, 64 * (hh + 1))
                    ddt_c = put_col(ddt_c, h, jnp.sum(t_dt[:, half], axis=1, keepdims=True))
                    d_dsk = d_dsk + jnp.where(lane1 == h, jnp.sum(t_dk[:, half], keepdims=True), 0.0)
            dc_g = dc_g + _dot(dcb, b_g)
            db_g = db_g + _dot(dcb.T, c_g)
            dpre_ref[:, 1024 + SSM_STATE * g:1024 + SSM_STATE * (g + 1)] = db_g * b_grad
            dpre_ref[:, 1536 + SSM_STATE * g:1536 + SSM_STATE * (g + 1)] = dc_g * c_grad

        dacum = dacum_c - dacum_r.T + jnp.where(row == CHUNK - 1, d_aend, 0.0)
        dda = _dot_exact((lane >= row).astype(F32), dacum)
        ddt = dda * a + ddt_c
        ddtr = ddt * _sigmoid(dtr + dtb)
        ddtr_ref[...] = ddtr.astype(MXU_DTYPE)
        dvec_ref[0:1, :] += _colsum(ddtr)
        dvec_ref[1:2, :] += _colsum(dda * dt)
        dvec_ref[2:3, :] += d_dsk

    return pl.pallas_call(
        body, name=name, grid=(nc,),
        out_shape=(jax.ShapeDtypeStruct((rows, CONV_DIM), F32), jax.ShapeDtypeStruct((rows, D_MODEL), MXU_DTYPE),
                   jax.ShapeDtypeStruct((rows, LANES), MXU_DTYPE), jax.ShapeDtypeStruct((1, D_MODEL), F32),
                   jax.ShapeDtypeStruct((8, LANES), F32)),
        in_specs=[pl.BlockSpec((CHUNK, CONV_DIM), lambda i: (rev(i), 0)), pl.BlockSpec((CHUNK, LANES), lambda i: (rev(i), 0)),
                  pl.BlockSpec((CHUNK, LANES), lambda i: (rev(i), 0)), pl.BlockSpec((CHUNK, LANES), lambda i: (rev(i), 0)),
                  pl.BlockSpec((CHUNK, D_MODEL), lambda i: (rev(i), 4)), pl.BlockSpec((CHUNK, D_MODEL), lambda i: (rev(i), 0)),
                  pl.BlockSpec((1, N_PAIRS, LANES, SSM_STATE), lambda i: (rev(i), 0, 0, 0)),
                  pl.BlockSpec((CHUNK, D_MODEL), lambda i: (rev(i), 1)),
                  _vec_spec(LANES), _vec_spec(LANES), _vec_spec(LANES), _vec_spec(D_MODEL)],
        out_specs=(pl.BlockSpec((CHUNK, CONV_DIM), lambda i: (rev(i), 0)), pl.BlockSpec((CHUNK, D_MODEL), lambda i: (rev(i), 0)),
                   pl.BlockSpec((CHUNK, LANES), lambda i: (rev(i), 0)), _vec_spec(D_MODEL), _vec_spec(LANES, 8)),
        scratch_shapes=[pltpu.VMEM((N_PAIRS, LANES, SSM_STATE), F32)],
        compiler_params=_cparams(("arbitrary",)),
    )(pre, dtr, dt_saved, acum_saved, proj, y_saved, states, dcat, dtb, alog, dsk, gn)


GROUP_DIM = D_MODEL // SSM_GROUPS
HEADS_PER_GROUP = SSM_HEADS // SSM_GROUPS
HEAD_DIM = GROUP_DIM // HEADS_PER_GROUP


def _by_quarter(index, pieces):
    out = pieces[3]
    for q in (2, 1, 0):
        out = jnp.where(index == q, pieces[q], out)
    return out


def _ssd_fwd_grouped(pre, dtr, proj, dtb, alog, dsk, gn, cat, *, name):
    rows = pre.shape[0]
    nc = rows // CHUNK

    def body(pre_ref, dtr_ref, z_ref, dtb_ref, alog_ref, dsk_ref, gn_ref, cat_ref, yb_ref, y_ref, st_ref, dt_ref,
             acum_ref, s_scr):
        @pl.when(pl.program_id(0) == 0)
        def _():
            s_scr[...] = jnp.zeros_like(s_scr)

        row, lane, tril = _chunk_iotas()
        dt = _softplus(dtr_ref[...] + dtb_ref[...])
        acum = _dot_exact(tril.astype(F32), dt * (-jnp.exp(alog_ref[...])))
        dt_ref[...] = dt
        acum_ref[...] = acum
        acum_t = acum.T
        eacum = jnp.exp(acum)
        a_end = acum[CHUNK - 1:CHUNK, :]
        e_end = jnp.exp(a_end)
        dte_all = jnp.exp(a_end - acum)
        dsk_v = dsk_ref[...]
        lane_q = lax.broadcasted_iota(jnp.int32, (CHUNK, GROUP_DIM), 1) // HEAD_DIM
        row_q = lax.broadcasted_iota(jnp.int32, (GROUP_DIM, SSM_STATE), 0) // HEAD_DIM

        def cols(mat, g):
            return _by_quarter(lane_q, [mat[:, 4 * g + q:4 * g + q + 1] for q in range(HEADS_PER_GROUP)])

        for g in range(SSM_GROUPS):
            cs = slice(GROUP_DIM * g, GROUP_DIM * (g + 1))
            b_g = _silu(pre_ref[:, 1024 + SSM_STATE * g:1024 + SSM_STATE * (g + 1)]).astype(MXU_DTYPE)
            c_g = _silu(pre_ref[:, 1536 + SSM_STATE * g:1536 + SSM_STATE * (g + 1)]).astype(MXU_DTYPE)
            cb = _dot_nt(c_g, b_g)
            xs = _silu(pre_ref[:, cs])
            xdt = xs * cols(dt, g)
            m_stack = jnp.concatenate(
                [(cb * jnp.exp(jnp.where(tril, acum[:, h:h + 1] - acum_t[h:h + 1, :], -jnp.inf))).astype(MXU_DTYPE)
                 for h in range(4 * g, 4 * g + 4)], axis=0)
            y_all = _dot(m_stack, xdt)
            y = _by_quarter(lane_q, [y_all[CHUNK * q:CHUNK * (q + 1)] for q in range(HEADS_PER_GROUP)])
            s_prev = s_scr[g]
            st_ref[0, g] = s_prev
            y = y + _dot_nt(c_g, s_prev) * cols(eacum, g) + cols(dsk_v, g) * xs
            xw = xdt * cols(dte_all, g)
            e_rows = _by_quarter(row_q, [e_end[:, 4 * g + q:4 * g + q + 1] for q in range(HEADS_PER_GROUP)])
            s_scr[g] = e_rows * s_prev + _dot(xw.T, b_g)
            y_ref[:, cs] = y
            gated = y * _silu(z_ref[:, cs])
            r = lax.rsqrt(jnp.mean(gated * gated, axis=1, keepdims=True) + EPS)
            yb_ref[:, cs] = (gated * r * gn_ref[:, cs]).astype(MXU_DTYPE)

    return pl.pallas_call(
        body, name=name, grid=(nc,),
        out_shape=(jax.ShapeDtypeStruct((rows, 2 * D_MODEL), MXU_DTYPE), jax.ShapeDtypeStruct((rows, D_MODEL), F32),
                   jax.ShapeDtypeStruct((nc, SSM_GROUPS, GROUP_DIM, SSM_STATE), F32),
                   jax.ShapeDtypeStruct((rows, LANES), F32), jax.ShapeDtypeStruct((rows, LANES), F32)),
        in_specs=[_row_spec(CONV_DIM, CHUNK), _row_spec(LANES, CHUNK), pl.BlockSpec((CHUNK, D_MODEL), lambda i: (i, 4)),
                  _vec_spec(LANES), _vec_spec(LANES), _vec_spec(LANES), _vec_spec(D_MODEL),
                  pl.BlockSpec(memory_space=pl.ANY)],
        out_specs=(pl.BlockSpec((CHUNK, D_MODEL), lambda i: (i, 1)), _row_spec(D_MODEL, CHUNK),
                   pl.BlockSpec((1, SSM_GROUPS, GROUP_DIM, SSM_STATE), lambda i: (i, 0, 0, 0)),
                   _row_spec(LANES, CHUNK), _row_spec(LANES, CHUNK)),
        scratch_shapes=[pltpu.VMEM((SSM_GROUPS, GROUP_DIM, SSM_STATE), F32)],
        input_output_aliases={7: 0},
        compiler_params=_cparams(("arbitrary",)),
    )(pre, dtr, proj, dtb, alog, dsk, gn, cat)


def _ssd_bwd_grouped(pre, dtr, dt_saved, acum_saved, proj, y_saved, states, dcat, dtb, alog, dsk, gn, dproj, *, name):
    rows = pre.shape[0]
    nc = rows // CHUNK

    def rev(i):
        return nc - 1 - i

    def body(pre_ref, dtr_ref, dt_ref, acum_ref, z_ref, y_ref, st_ref, dyb_ref, dtb_ref, alog_ref, dsk_ref, gn_ref,
             dproj_ref, dpre_ref, dz_ref, ddtr_ref, dgn_ref, dvec_ref, g_scr):
        @pl.when(pl.program_id(0) == 0)
        def _():
            g_scr[...] = jnp.zeros_like(g_scr)
            dgn_ref[...] = jnp.zeros_like(dgn_ref)
            dvec_ref[...] = jnp.zeros_like(dvec_ref)

        row, lane, tril = _chunk_iotas()
        triu = lane >= row
        dt, acum = dt_ref[...], acum_ref[...]
        a = -jnp.exp(alog_ref[...])
        acum_t = acum.T
        eacum = jnp.exp(acum)
        a_end = acum[CHUNK - 1:CHUNK, :]
        e_end = jnp.exp(a_end)
        dte_all = jnp.exp(a_end - acum)
        dsk_v = dsk_ref[...]
        lane_q = lax.broadcasted_iota(jnp.int32, (CHUNK, GROUP_DIM), 1) // HEAD_DIM
        row_q = lax.broadcasted_iota(jnp.int32, (GROUP_DIM, SSM_STATE), 0) // HEAD_DIM
        zero = jnp.zeros((CHUNK, LANES), F32)
        dacum_c, dacum_r, ddt_c = zero, zero, zero
        d_aend = jnp.zeros((1, LANES), F32)
        d_dsk = jnp.zeros((1, LANES), F32)
        lane1 = lane[0:1, :]

        def cols(mat, g):
            return _by_quarter(lane_q, [mat[:, 4 * g + q:4 * g + q + 1] for q in range(HEADS_PER_GROUP)])

        for g in range(SSM_GROUPS):
            cs = slice(GROUP_DIM * g, GROUP_DIM * (g + 1))
            yv = y_ref[:, cs]
            sz, sz_grad = _silu_and_grad(z_ref[:, cs])
            gated = yv * sz
            dyb = dyb_ref[:, cs]
            dgh = dyb * gn_ref[:, cs]
            r = lax.rsqrt(jnp.mean(gated * gated, axis=1, keepdims=True) + EPS)
            dgn_ref[:, cs] += _colsum(dyb * gated * r)
            dgated = r * dgh - gated * (r * r * r * jnp.mean(dgh * gated, axis=1, keepdims=True))
            dy = dgated * sz
            dz_ref[:, cs] = (dgated * yv * sz_grad).astype(MXU_DTYPE)

            b_f, b_grad = _silu_and_grad(pre_ref[:, 1024 + SSM_STATE * g:1024 + SSM_STATE * (g + 1)])
            c_f, c_grad = _silu_and_grad(pre_ref[:, 1536 + SSM_STATE * g:1536 + SSM_STATE * (g + 1)])
            b_g, c_g = b_f.astype(MXU_DTYPE), c_f.astype(MXU_DTYPE)
            xs, xs_grad = _silu_and_grad(pre_ref[:, cs])
            dtq = cols(dt, g)
            xdt = xs * dtq
            xdt_m = xdt.astype(MXU_DTYPE)
            dy_m = dy.astype(MXU_DTYPE)
            s_prev = st_ref[0, g]
            g_next = g_scr[g]
            eacq, dteq = cols(eacum, g), cols(dte_all, g)
            t_off = dy * (_dot_nt(c_g, s_prev) * eacq)
            dye = dy * eacq
            dc_g = _dot(dye, s_prev)
            bg = _dot_nt(b_g, g_next)
            xw = xdt * dteq
            db_g = _dot(xw, g_next)
            t_w = xw * bg
            gs = g_next * s_prev
            e_rows = _by_quarter(row_q, [e_end[:, 4 * g + q:4 * g + q + 1] for q in range(HEADS_PER_GROUP)])
            g_scr[g] = e_rows * g_next + _dot(dye.T, c_g)
            cb = _dot_nt(c_g, b_g)
            cb_t = cb.T
            heads = range(4 * g, 4 * g + 4)
            decs = [jnp.exp(jnp.where(tril, acum[:, h:h + 1] - acum_t[h:h + 1, :], -jnp.inf)) for h in heads]
            mt_stack = jnp.concatenate(
                [(cb_t * jnp.exp(jnp.where(triu, acum_t[h:h + 1, :] - acum[:, h:h + 1], -jnp.inf))).astype(MXU_DTYPE)
                 for h in heads], axis=0)
            dy_stack = jnp.concatenate([jnp.where(lane_q == q, dy, 0.0).astype(MXU_DTYPE)
                                        for q in range(HEADS_PER_GROUP)], axis=0)
            dm_all = _dot_nt(dy_stack, xdt_m)
            dx_all = _dot(mt_stack, dy_m)
            dxdt = bg * dteq + _by_quarter(lane_q, [dx_all[CHUNK * q:CHUNK * (q + 1)] for q in range(HEADS_PER_GROUP)])
            dcb = zero
            t_dt = dxdt * xs
            t_dk = dy * xs
            for q, h in enumerate(heads):
                qs = slice(HEAD_DIM * q, HEAD_DIM * (q + 1))
                dm = dm_all[CHUNK * q:CHUNK * (q + 1)]
                e_h = dm * (cb * decs[q])
                dcb = dcb + dm * decs[q]
                w_col = jnp.sum(t_w[:, qs], axis=1, keepdims=True)
                col = jnp.sum(e_h, axis=1, keepdims=True) + jnp.sum(t_off[:, qs], axis=1, keepdims=True) - w_col
                dacum_c = dacum_c + jnp.where(lane == h, col, 0.0)
                dacum_r = dacum_r + jnp.where(row == h, _colsum(e_h), 0.0)
                d_end_h = jnp.sum(w_col, keepdims=True) + e_end[:, h:h + 1] * jnp.sum(gs[qs, :], keepdims=True)
                d_aend = d_aend + jnp.where(lane1 == h, d_end_h, 0.0)
                ddt_c = ddt_c + jnp.where(lane == h, jnp.sum(t_dt[:, qs], axis=1, keepdims=True), 0.0)
                d_dsk = d_dsk + jnp.where(lane1 == h, jnp.sum(t_dk[:, qs], keepdims=True), 0.0)
            dpre_ref[:, cs] = (dxdt * dtq + cols(dsk_v, g) * dy) * xs_grad
            dc_g = dc_g + _dot(dcb, b_g)
            db_g = db_g + _dot(dcb.T, c_g)
            dpre_ref[:, 1024 + SSM_STATE * g:1024 + SSM_STATE * (g + 1)] = db_g * b_grad
            dpre_ref[:, 1536 + SSM_STATE * g:1536 + SSM_STATE * (g + 1)] = dc_g * c_grad

        dacum = dacum_c - dacum_r.T + jnp.where(row == CHUNK - 1, d_aend, 0.0)
        dda = _dot_exact(triu.astype(F32), dacum)
        ddtr = (dda * a + ddt_c) * _sigmoid(dtr_ref[...] + dtb_ref[...])
        ddtr_ref[...] = ddtr.astype(MXU_DTYPE)
        dvec_ref[0:1, :] += _colsum(ddtr)
        dvec_ref[1:2, :] += _colsum(dda * dt)
        dvec_ref[2:3, :] += d_dsk

    return pl.pallas_call(
        body, name=name, grid=(nc,),
        out_shape=(jax.ShapeDtypeStruct((rows, CONV_DIM), F32), jax.ShapeDtypeStruct((rows, IN_MAIN), MXU_DTYPE),
                   jax.ShapeDtypeStruct((rows, LANES), MXU_DTYPE), jax.ShapeDtypeStruct((1, D_MODEL), F32),
                   jax.ShapeDtypeStruct((8, LANES), F32)),
        in_specs=[pl.BlockSpec((CHUNK, CONV_DIM), lambda i: (rev(i), 0)), pl.BlockSpec((CHUNK, LANES), lambda i: (rev(i), 0)),
                  pl.BlockSpec((CHUNK, LANES), lambda i: (rev(i), 0)), pl.BlockSpec((CHUNK, LANES), lambda i: (rev(i), 0)),
                  pl.BlockSpec((CHUNK, D_MODEL), lambda i: (rev(i), 4)), pl.BlockSpec((CHUNK, D_MODEL), lambda i: (rev(i), 0)),
                  pl.BlockSpec((1, SSM_GROUPS, GROUP_DIM, SSM_STATE), lambda i: (rev(i), 0, 0, 0)),
                  pl.BlockSpec((CHUNK, D_MODEL), lambda i: (rev(i), 1)),
                  _vec_spec(LANES), _vec_spec(LANES), _vec_spec(LANES), _vec_spec(D_MODEL),
                  pl.BlockSpec(memory_space=pl.ANY)],
        out_specs=(pl.BlockSpec((CHUNK, CONV_DIM), lambda i: (rev(i), 0)), pl.BlockSpec((CHUNK, D_MODEL), lambda i: (rev(i), 4)),
                   pl.BlockSpec((CHUNK, LANES), lambda i: (rev(i), 0)), _vec_spec(D_MODEL), _vec_spec(LANES, 8)),
        scratch_shapes=[pltpu.VMEM((SSM_GROUPS, GROUP_DIM, SSM_STATE), F32)],
        input_output_aliases={12: 1},
        compiler_params=_cparams(("arbitrary",)),
    )(pre, dtr, dt_saved, acum_saved, proj, y_saved, states, dcat, dtb, alog, dsk, gn, dproj)


def _pool_counts(first_row, n_rows, win):
    t = first_row + lax.broadcasted_iota(jnp.int32, (n_rows, POOL_DIM), 0)
    return jnp.minimum(t + 1, win).astype(F32)


def _pool_fwd(yn, pool_w, pool_b, pool_scale, *, name):
    rows = yn.shape[0]
    tr = ROW_TILE
    hb = tr // POOL_HALO

    def body(y_ref, prev_ref, w_ref, b_ref, s_ref, pm_ref, diff_ref, buf):
        i = pl.program_id(0)
        buf[pl.ds(0, POOL_HALO), :] = jnp.where(i == 0, 0.0, prev_ref[...])
        buf[pl.ds(POOL_HALO, tr), :] = y_ref[...]
        for g, win in enumerate(POOL_WINDOWS):
            cs = slice(POOL_DIM * g, POOL_DIM * (g + 1))
            acc = buf[pl.ds(POOL_HALO, tr), cs]
            for s in range(1, win):
                acc = acc + buf[pl.ds(POOL_HALO - s, tr), cs]
            diff = (acc / _pool_counts(i * tr, tr, win) - y_ref[:, cs]).astype(MXU_DTYPE)
            diff_ref[:, cs] = diff
            pm_ref[:, cs] = (_dot(diff, w_ref[g]) + b_ref[:, cs]) * s_ref[:, cs]

    return pl.pallas_call(
        body, name=name, grid=(rows // tr,),
        out_shape=(jax.ShapeDtypeStruct((rows, D_MODEL), F32), jax.ShapeDtypeStruct((rows, D_MODEL), MXU_DTYPE)),
        in_specs=[_row_spec(D_MODEL, tr),
                  pl.BlockSpec((POOL_HALO, D_MODEL), lambda i: (jnp.maximum(i * hb - 1, 0), 0)),
                  pl.BlockSpec((4, POOL_DIM, POOL_DIM), lambda i: (0, 0, 0)), _vec_spec(D_MODEL), _vec_spec(D_MODEL)],
        out_specs=(_row_spec(D_MODEL, tr), _row_spec(D_MODEL, tr)),
        scratch_shapes=[pltpu.VMEM((tr + POOL_HALO, D_MODEL), F32)],
        compiler_params=_cparams(("parallel",)),
    )(yn, yn, pool_w, pool_b, pool_scale)


def _pool_bwd(dpm, diff, pool_w, pool_w_t, pool_b, pool_scale, *, name):
    rows = dpm.shape[0]
    tr = ROW_TILE
    hb = tr // POOL_HALO
    nblk = rows // tr

    def body(d_ref, dnext_ref, diff_ref, w_ref, wt_ref, b_ref, s_ref, dy_ref, dw_ref, db_ref, ds_ref, ebuf):
        i = pl.program_id(0)

        @pl.when(i == 0)
        def _():
            dw_ref[...] = jnp.zeros_like(dw_ref)
            db_ref[...] = jnp.zeros_like(db_ref)
            ds_ref[...] = jnp.zeros_like(ds_ref)

        last = i == nblk - 1
        for g, win in enumerate(POOL_WINDOWS):
            cs = slice(POOL_DIM * g, POOL_DIM * (g + 1))
            d = d_ref[:, cs]
            diff = diff_ref[:, cs]
            out_pre = _dot(diff, w_ref[g]) + b_ref[:, cs]
            ds_ref[:, cs] += _colsum(d * out_pre)
            dout = d * s_ref[:, cs]
            db_ref[:, cs] += _colsum(dout)
            dw_ref[g] += _dot_tn(diff, dout)
            ddiff = _dot(dout, wt_ref[g])
            ddiff_next = _dot(jnp.where(last, 0.0, dnext_ref[:, cs]) * s_ref[:, cs], wt_ref[g])
            ebuf[pl.ds(0, tr), cs] = ddiff / _pool_counts(i * tr, tr, win)
            ebuf[pl.ds(tr, POOL_HALO), cs] = ddiff_next / _pool_counts((i + 1) * tr, POOL_HALO, win)
            acc = -ddiff
            for s in range(win):
                acc = acc + ebuf[pl.ds(s, tr), cs]
            dy_ref[:, cs] = acc

    return pl.pallas_call(
        body, name=name, grid=(nblk,),
        out_shape=(jax.ShapeDtypeStruct((rows, D_MODEL), F32), jax.ShapeDtypeStruct((4, POOL_DIM, POOL_DIM), F32),
                   jax.ShapeDtypeStruct((1, D_MODEL), F32), jax.ShapeDtypeStruct((1, D_MODEL), F32)),
        in_specs=[_row_spec(D_MODEL, tr),
                  pl.BlockSpec((POOL_HALO, D_MODEL), lambda i: (jnp.minimum((i + 1) * hb, rows // POOL_HALO - 1), 0)),
                  _row_spec(D_MODEL, tr),
                  pl.BlockSpec((4, POOL_DIM, POOL_DIM), lambda i: (0, 0, 0)),
                  pl.BlockSpec((4, POOL_DIM, POOL_DIM), lambda i: (0, 0, 0)), _vec_spec(D_MODEL), _vec_spec(D_MODEL)],
        out_specs=(_row_spec(D_MODEL, tr), pl.BlockSpec((4, POOL_DIM, POOL_DIM), lambda i: (0, 0, 0)),
                   _vec_spec(D_MODEL), _vec_spec(D_MODEL)),
        scratch_shapes=[pltpu.VMEM((tr + POOL_HALO, D_MODEL), F32)],
        compiler_params=_cparams(("arbitrary",)),
    )(dpm, dpm, diff, pool_w, pool_w_t, pool_b, pool_scale)


def _row_tile(rows, cap, step):
    best = rows
    for t in range(step, min(rows, cap) + 1, step):
        if rows % t == 0:
            best = t
    return best if best <= cap else rows


def _sum8(recv, *, name):
    _, r, c = recv.shape
    step = 8 if recv.dtype == F32 else 16

    def body(r_ref, g_ref):
        g = r_ref[0].astype(F32)
        for j in range(1, N_DEV):
            g = g + r_ref[j].astype(F32)
        g_ref[...] = g

    if r % step == 0:
        tr = _row_tile(r, 256, step)
        grid, in_spec, out_spec = (r // tr,), pl.BlockSpec((N_DEV, tr, c), lambda i: (0, i, 0)), pl.BlockSpec((tr, c), lambda i: (i, 0))
    else:
        tc = 256
        grid, in_spec, out_spec = (c // tc,), pl.BlockSpec((N_DEV, r, tc), lambda i: (0, 0, i)), pl.BlockSpec((r, tc), lambda i: (0, i))
    return pl.pallas_call(
        body, name=name, grid=grid, out_shape=jax.ShapeDtypeStruct((r, c), F32),
        in_specs=[in_spec], out_specs=out_spec, compiler_params=_cparams(("parallel",)),
    )(recv)


def _adamw(g, w, m, v, *, name):
    rows, cols = w.shape
    tr = _row_tile(rows, max(8, (256 * 1024) // cols // 8 * 8), 8)
    c1 = 1.0 / (1.0 - ADAM_B1 ** ADAM_STEP)
    c2 = 1.0 / (1.0 - ADAM_B2 ** ADAM_STEP)

    def body(g_ref, w_ref, m_ref, v_ref, d_ref, mo_ref, vo_ref):
        g = g_ref[...]
        m_new = ADAM_B1 * m_ref[...] + (1.0 - ADAM_B1) * g
        v_new = ADAM_B2 * v_ref[...] + (1.0 - ADAM_B2) * (g * g)
        mo_ref[...] = m_new
        vo_ref[...] = v_new
        d_ref[...] = -ADAM_LR * ((m_new * c1) / (jnp.sqrt(v_new * c2) + ADAM_EPS) + ADAM_WD * w_ref[...])

    spec = pl.BlockSpec((tr, cols), lambda i: (i, 0))
    return pl.pallas_call(
        body, name=name, grid=(rows // tr,),
        out_shape=tuple(jax.ShapeDtypeStruct((rows, cols), F32) for _ in range(3)),
        in_specs=[spec] * 4, out_specs=(spec, spec, spec),
        compiler_params=_cparams(("parallel",)),
    )(g, w, m, v)


def _pad_rows(flat, mult):
    n = flat.shape[-1]
    pad = (-n) % mult
    if pad:
        flat = jnp.pad(flat, [(0, 0)] * (flat.ndim - 1) + [(0, pad)])
    return flat


def _pack_blocks(blocks, row_mult):
    flat = jnp.concatenate([_pad_rows(b.reshape(-1), LANES) for b in blocks])
    return _pad_rows(flat, LANES * row_mult).reshape(-1, LANES)


def _block_sizes(blocks):
    return [-(-math.prod(b.shape) // LANES) * LANES for b in blocks]


def _unpack_blocks(slab, like, lead=()):
    flat = slab.reshape(lead + (-1,))
    out, off = [], 0
    for b, size in zip(like, _block_sizes(like)):
        n = math.prod(b.shape)
        out.append(flat[..., off:off + n].reshape(lead + tuple(b.shape)))
        off += size
    return out


def _join_shards(gathered, axis):
    return jnp.concatenate([gathered[j] for j in range(N_DEV)], axis=axis)


def _split_shards(full, axis):
    return jnp.stack(jnp.split(full, N_DEV, axis=axis))


def _interleave_ff(w_gate, w_up):
    k = w_gate.shape[0]
    nt = D_FF // FF_TILE
    return jnp.stack([w_gate.reshape(k, nt, FF_TILE), w_up.reshape(k, nt, FF_TILE)], axis=2).reshape(k, 2 * D_FF)


def _row128(vec):
    return jnp.pad(vec.reshape(1, -1), ((0, 0), (0, LANES - vec.shape[-1])))


def kernel(x, norm_g, w_in, gm_ln_g, gm_ln_b, gm_ws, gm_bs, conv_w, conv_b, dt_bias, a_log, d_skip, ssm_norm_g, w_out, pool_w, pool_b, pool_scale, ffn_w_gate, ffn_w_up, ffn_w_down, loss_target, m_norm_g, m_w_in, m_gm_ln_g, m_gm_ln_b, m_gm_ws, m_gm_bs, m_conv_w, m_conv_b, m_dt_bias, m_a_log, m_d_skip, m_ssm_norm_g, m_w_out, m_pool_w, m_pool_b, m_pool_scale, m_ffn_w_gate, m_ffn_w_up, m_ffn_w_down, v_norm_g, v_w_in, v_gm_ln_g, v_gm_ln_b, v_gm_ws, v_gm_bs, v_conv_w, v_conv_b, v_dt_bias, v_a_log, v_d_skip, v_ssm_norm_g, v_w_out, v_pool_w, v_pool_b, v_pool_scale, v_ffn_w_gate, v_ffn_w_up, v_ffn_w_down):
    w_loc = dict(norm_g=norm_g, w_in=w_in, gm_ln_g=gm_ln_g, gm_ln_b=gm_ln_b, gm_ws=gm_ws, gm_bs=gm_bs, conv_w=conv_w,
                 conv_b=conv_b, dt_bias=dt_bias, a_log=a_log, d_skip=d_skip, ssm_norm_g=ssm_norm_g, w_out=w_out,
                 pool_w=pool_w, pool_b=pool_b, pool_scale=pool_scale, ffn_w_gate=ffn_w_gate, ffn_w_up=ffn_w_up,
                 ffn_w_down=ffn_w_down)
    m_loc = dict(zip(WEIGHTS, [m_norm_g, m_w_in, m_gm_ln_g, m_gm_ln_b, m_gm_ws, m_gm_bs, m_conv_w, m_conv_b, m_dt_bias,
                               m_a_log, m_d_skip, m_ssm_norm_g, m_w_out, m_pool_w, m_pool_b, m_pool_scale,
                               m_ffn_w_gate, m_ffn_w_up, m_ffn_w_down]))
    v_loc = dict(zip(WEIGHTS, [v_norm_g, v_w_in, v_gm_ln_g, v_gm_ln_b, v_gm_ws, v_gm_bs, v_conv_w, v_conv_b, v_dt_bias,
                               v_a_log, v_d_skip, v_ssm_norm_g, v_w_out, v_pool_w, v_pool_b, v_pool_scale,
                               v_ffn_w_gate, v_ffn_w_up, v_ffn_w_down]))

    small_blocks = [w_loc[n] for n in GATHER_F32]
    got = _exchange([w_in[0].astype(MXU_DTYPE), _pack_blocks(small_blocks, 8)], ['gather'] * 2, name="gather_first")
    full = {n: w_loc[n] for n in WEIGHTS if SHARD_AXIS[n] is None}
    full['w_in'] = got[0].transpose(1, 0, 2).reshape(1, D_MODEL, -1)
    for n, g in zip(GATHER_F32, _unpack_blocks(got[1], small_blocks, (N_DEV,))):
        full[n] = _join_shards(g, SHARD_AXIS[n])
    shards = {n: w_loc[n].astype(MXU_DTYPE) for n in ('w_out', 'ffn_w_gate', 'ffn_w_up', 'ffn_w_down', 'pool_w')}

    loss_part, grad_x, grads, recv = _local_step(x[0], loss_target[0], full, shards)

    small = [n for n in WEIGHTS if n not in BIG_WEIGHTS]
    like = [w_loc[n] for n in small]
    slots = []
    for n in small:
        ax = SHARD_AXIS[n]
        g = grads[n].astype(F32)
        sh = _split_shards(g, ax) if ax is not None else jnp.broadcast_to(g[None], (N_DEV,) + g.shape)
        slots.append(_pad_rows(sh.reshape(N_DEV, -1), LANES))
    send_small = _pad_rows(jnp.concatenate(slots, axis=1), LANES * 8).reshape(N_DEV, -1, LANES)
    recv_small, = _exchange([send_small], ['slots'], name="exchange_last")

    g_small = _sum8(recv_small, name="sum_small")
    g_own = dict(zip(small, _unpack_blocks(g_small, like)))
    g_own['w_in'] = _sum8(recv['w_in'], name="sum_w_in").T[None]
    g_own['w_out'] = _sum8(recv['w_out'], name="sum_w_out")[None]
    g_own['ffn_w_gate'] = jnp.stack([_sum8(recv['ffn_w_gate'][l], name=f"sum_ffn{l}_gate").T for l in range(2)])
    g_own['ffn_w_up'] = jnp.stack([_sum8(recv['ffn_w_up'][l], name=f"sum_ffn{l}_up").T for l in range(2)])
    g_own['ffn_w_down'] = jnp.stack([_sum8(recv['ffn_w_down'][l], name=f"sum_ffn{l}_down") for l in range(2)])

    delta, m_new, v_new = {}, {}, {}
    pk = lambda d: _pack_blocks([d[n] for n in small], 8)
    d_s, m_s, v_s = _adamw(g_small, pk(w_loc), pk(m_loc), pk(v_loc), name="adamw_small")
    for dst, slab in ((delta, d_s), (m_new, m_s), (v_new, v_s)):
        dst.update(zip(small, _unpack_blocks(slab, like)))
    for n in BIG_WEIGHTS:
        shape = w_loc[n].shape
        two_d = lambda t: t.reshape(-1, shape[-1])
        res = _adamw(two_d(g_own[n]), two_d(w_loc[n]), two_d(m_loc[n]), two_d(v_loc[n]), name=f"adamw_{n}")
        delta[n], m_new[n], v_new[n] = (t.reshape(shape) for t in res)

    loss = lax.psum(loss_part[0, 0], ("x", "y", "c"))
    outs = [d[n] for d in (g_own, delta, m_new, v_new) for n in WEIGHTS]
    return (loss, grad_x[None], *outs)


def _local_step(h0, tgt, full, shards):
    gm_ln_g, gm_ln_b, gm_ws, gm_bs = full['gm_ln_g'], full['gm_ln_b'], full['gm_ws'], full['gm_bs']
    conv_b, dt_bias, a_log, d_skip, ssm_norm_g = (full['conv_b'], full['dt_bias'], full['a_log'], full['d_skip'],
                                                  full['ssm_norm_g'])
    w_in_f = full['w_in'][0]
    w_main = jnp.concatenate([w_in_f[:, 3072:5120], w_in_f[:, :3072]], axis=1)
    w_dt = jnp.pad(w_in_f[:, 5120:], ((0, 0), (0, LANES - SSM_HEADS)))
    ng = full['norm_g']

    def ffn_shards(layer):
        return [shards['ffn_w_gate'][layer], shards['ffn_w_up'][layer], shards['ffn_w_down'][layer]]

    def ffn_weights(got_gate, got_up, got_down):
        cols = lambda g: g.transpose(1, 0, 2).reshape(D_MODEL, D_FF)
        return _interleave_ff(cols(got_gate), cols(got_up)), got_down.reshape(D_FF, D_MODEL)

    w_gu, w_dn = [None, None], [None, None]
    causal = jnp.tril(jnp.ones((CHUNK, CHUNK), bool))
    wm = jnp.where(causal[None], gm_ws[0], 0.0).astype(MXU_DTYPE)
    wm_t = jnp.swapaxes(wm, 1, 2)
    bcol = jnp.pad(gm_bs[0].T, ((0, 0), (0, LANES - GM_HEADS)))
    conv_w8 = jnp.pad(full['conv_w'][0], ((0, 8 - SSM_CONV), (0, 0)))
    dtb, alog, dsk = _row128(dt_bias[0]), _row128(a_log[0]), _row128(d_skip[0])
    pool_b_f = full['pool_b'][0].reshape(1, D_MODEL)
    pool_s_f = full['pool_scale']

    def g_(layer, i):
        return ng[layer, i].reshape(1, D_MODEL)

    yn0 = _rn_fwd(h0, g_(0, 0), name="rn_fwd_0", out_dtype=MXU_DTYPE)
    proj, got = _mm(yn0, w_main, name="mm_in_proj", tm=2048,
                    ex=_Exchange([shards['w_out'][0]] + ffn_shards(0), ['gather'] * 4))
    w_out_f = got[0].reshape(-1, D_MODEL)
    w_gu[0], w_dn[0] = ffn_weights(*got[1:])
    dtr = _mm(yn0, w_dt, name="mm_in_proj_dt")
    pre = _conv_fwd(proj, conv_w8, conv_b, name="conv_fwd")
    cat = _gmlp_fwd(proj, gm_ln_g, gm_ln_b, wm, bcol, name="gmlp_fwd")
    cat, y_ssd, states, dt_ssd, acum_ssd = _ssd_fwd_grouped(pre, dtr, proj, dtb, alog, dsk, ssm_norm_g, cat,
                                                            name="ssd_fwd")
    o0 = _mm(cat, w_out_f, name="mm_out_proj", tm=2048, tn=1024, tk=1024)
    h1, yn1 = _resid_rn_fwd(h0, o0, g_(0, 1), g_(0, 2), name="resid_fwd_0a", next_dtype=MXU_DTYPE)
    (gu0, act0), got = _mm_swiglu(yn1, w_gu[0], name="mm_ffn0_gate_up",
                                  ex=_Exchange(ffn_shards(1) + [shards['pool_w'][0]], ['gather'] * 4))
    w_gu[1], w_dn[1] = ffn_weights(*got[:3])
    pool_w_f = got[3].transpose(1, 0, 2, 3).reshape(4, POOL_DIM, POOL_DIM)
    d0 = _mm(act0, w_dn[0], name="mm_ffn0_down", tm=2048, tn=1024, tk=1408)
    h2, yn2 = _resid_rn_fwd(h1, d0, g_(0, 3), g_(1, 0), name="resid_fwd_0b", next_dtype=F32)
    pm, pdiff = _pool_fwd(yn2, pool_w_f, pool_b_f, pool_s_f, name="pool_fwd")
    h3, yn3 = _resid_rn_fwd(h2, pm, g_(1, 1), g_(1, 2), name="resid_fwd_1a", next_dtype=MXU_DTYPE)
    gu1, act1 = _mm_swiglu(yn3, w_gu[1], name="mm_ffn1_gate_up")
    d1 = _mm(act1, w_dn[1], name="mm_ffn1_down", tm=2048, tn=1024, tk=1408)
    dh4, loss_part = _resid_loss(h3, d1, g_(1, 3), tgt, name="resid_loss")

    grads = {}
    recv = {'ffn_w_gate': [None, None], 'ffn_w_up': [None, None], 'ffn_w_down': [None, None]}
    dng = [[None] * 4 for _ in range(2)]

    def ffn_bwd(layer, dh, d_out, gu, act, yn, h_in):
        dd, dng[layer][3] = _resid_bwd_post(dh, d_out, g_(layer, 3), name=f"resid_bwd_post_{layer}b", out_dtype=MXU_DTYPE)
        dw_dn = _mm_tn(act, dd, name=f"mm_ffn{layer}_dw_down", out_dtype=MXU_DTYPE, tm=1408, tn=1024)
        dgu = _mm_dswiglu(dd, w_dn[layer].T, gu, name=f"mm_ffn{layer}_dact")
        dw_g_t, dw_u_t = _mm_tn_gate_up(dgu, yn, name=f"mm_ffn{layer}_dw_gate_up", out_dtype=MXU_DTYPE)
        dyn, got = _mm(dgu, w_gu[layer].T, name=f"mm_ffn{layer}_dyn", tm=2048, tn=1024, tk=1408,
                       ex=_Exchange([dw_g_t, dw_u_t, dw_dn], ['rows'] * 3))
        recv['ffn_w_gate'][layer], recv['ffn_w_up'][layer], recv['ffn_w_down'][layer] = got
        dh_in, dng[layer][2] = _resid_bwd_pre(dh, [dyn], h_in, g_(layer, 2), name=f"resid_bwd_pre_{layer}b")
        return dh_in

    dh3 = ffn_bwd(1, dh4, d1, gu1, act1, yn3, h3)
    dpm, dng[1][1] = _resid_bwd_post(dh3, pm, g_(1, 1), name="resid_bwd_post_1a", out_dtype=F32)
    dyn2, d_pool_w, d_pool_b, d_pool_s = _pool_bwd(dpm, pdiff, pool_w_f, jnp.swapaxes(pool_w_f, 1, 2), pool_b_f, pool_s_f,
                                                   name="pool_bwd")
    dh2, dng[1][0] = _resid_bwd_pre(dh3, [dyn2], h2, g_(1, 0), name="resid_bwd_pre_1a")
    dh1 = ffn_bwd(0, dh2, d0, gu0, act0, yn1, h1)
    do0, dng[0][1] = _resid_bwd_post(dh1, o0, g_(0, 1), name="resid_bwd_post_0a", out_dtype=MXU_DTYPE)
    d_w_out = _mm_tn(cat, do0, name="mm_out_proj_dw", out_dtype=MXU_DTYPE, tn=1024)
    dcat, got = _mm(do0, w_out_f.T, name="mm_out_proj_dx", tm=2048, tn=1024, ex=_Exchange([d_w_out], ['rows']))
    recv['w_out'] = got[0]
    dproj, d_wm, d_bcol, d_ln_g, d_ln_b = _gmlp_bwd(proj, dcat, gm_ln_g, gm_ln_b, wm, wm_t, bcol, name="gmlp_bwd")
    dpre, dproj, ddtr, d_gn, d_vec = _ssd_bwd_grouped(pre, dtr, dt_ssd, acum_ssd, proj, y_ssd, states, dcat, dtb, alog,
                                                      dsk, ssm_norm_g, dproj, name="ssd_bwd")
    dproj, d_conv_w8, d_conv_b = _conv_bwd(dpre, proj, conv_w8, dproj, name="conv_bwd")
    d_w_main_t = _mm_tn(dproj, yn0, name="mm_in_proj_dw", out_dtype=MXU_DTYPE, tn=1024, shift=3)
    d_w_dt_t = _mm_tn(ddtr, yn0, name="mm_in_proj_dt_dw", out_dtype=MXU_DTYPE, tn=1024)
    d_w_in_t = jnp.concatenate([d_w_main_t, d_w_dt_t[:SSM_HEADS]], axis=0).reshape(N_DEV, -1, D_MODEL)
    dyn0, got = _mm(dproj, w_main.T, name="mm_in_proj_dx", tm=2048, tn=1024, tk=1024,
                    ex=_Exchange([d_w_in_t], ['slots']))
    recv['w_in'] = got[0]
    dyn0_dt = _mm(ddtr, w_dt.T, name="mm_in_proj_dt_dx")
    grad_x, dng[0][0] = _resid_bwd_pre(dh1, [dyn0, dyn0_dt], h0, g_(0, 0), name="resid_bwd_pre_0a")

    grads['norm_g'] = jnp.stack([jnp.concatenate(dng[l], axis=0) for l in range(2)])
    grads['gm_ln_g'], grads['gm_ln_b'] = d_ln_g, d_ln_b
    grads['gm_ws'] = d_wm[None]
    grads['gm_bs'] = d_bcol[:, :GM_HEADS].T[None]
    grads['conv_w'] = d_conv_w8[None, :SSM_CONV]
    grads['conv_b'] = d_conv_b
    grads['dt_bias'] = d_vec[0:1, :SSM_HEADS]
    grads['a_log'] = d_vec[1:2, :SSM_HEADS] * (-jnp.exp(a_log))
    grads['d_skip'] = d_vec[2:3, :SSM_HEADS]
    grads['ssm_norm_g'] = d_gn
    grads['pool_w'] = d_pool_w[None]
    grads['pool_b'] = d_pool_b.reshape(1, 4, POOL_DIM)
    grads['pool_scale'] = d_pool_s
    return loss_part, grad_x, grads, recv
```

```python
import functools
import math

import jax
import jax.numpy as jnp
from jax import lax
from jax.experimental import pallas as pl
from jax.experimental.pallas import tpu as pltpu

F32 = jnp.float32
MXU_DTYPE = jnp.bfloat16

N_DEV = 8
D_MODEL = 1024
EPS = 1e-6
GM_HEADS = 4
GM_HEAD_DIM = 256
CHUNK = 128
SSM_HEADS = 16
SSM_GROUPS = 4
SSM_STATE = 128
SSM_CONV = 4
CONV_DIM = 2048
POOL_WINDOWS = (2, 4, 8, 16)
POOL_DIM = 256
D_FF = 2816
FF_TILE = 256
IN_MAIN = 5120
LANES = 128
CONV_HALO = 8
POOL_HALO = 16
ADAM_LR, ADAM_B1, ADAM_B2, ADAM_EPS, ADAM_WD, ADAM_STEP = 0.001, 0.9, 0.999, 1e-08, 0.01, 10

VMEM_LIMIT = 56 * 1024 * 1024
ROW_TILE = 512
MM_TM = 2048

WEIGHTS = ['norm_g', 'w_in', 'gm_ln_g', 'gm_ln_b', 'gm_ws', 'gm_bs', 'conv_w', 'conv_b', 'dt_bias', 'a_log',
           'd_skip', 'ssm_norm_g', 'w_out', 'pool_w', 'pool_b', 'pool_scale', 'ffn_w_gate', 'ffn_w_up', 'ffn_w_down']
SHARD_AXIS = {'norm_g': 2, 'w_in': 2, 'gm_ln_g': None, 'gm_ln_b': None, 'gm_ws': None, 'gm_bs': None, 'conv_w': 2,
              'conv_b': None, 'dt_bias': None, 'a_log': None, 'd_skip': None, 'ssm_norm_g': None, 'w_out': 1,
              'pool_w': 2, 'pool_b': 2, 'pool_scale': 1, 'ffn_w_gate': 2, 'ffn_w_up': 2, 'ffn_w_down': 1}
GATHER_BF16 = ['w_in', 'w_out', 'pool_w', 'ffn_w_gate', 'ffn_w_up', 'ffn_w_down']
GATHER_F32 = ['norm_g', 'conv_w', 'pool_b', 'pool_scale']
BIG_WEIGHTS = ['w_in', 'w_out', 'ffn_w_gate', 'ffn_w_up', 'ffn_w_down']


def _cparams(sem=None):
    return pltpu.CompilerParams(dimension_semantics=sem, vmem_limit_bytes=VMEM_LIMIT)


def _dot(a, b):
    return jnp.dot(a.astype(MXU_DTYPE), b.astype(MXU_DTYPE), preferred_element_type=F32)


def _dot_nt(a, b):
    return lax.dot_general(a.astype(MXU_DTYPE), b.astype(MXU_DTYPE), (((1,), (1,)), ((), ())),
                           preferred_element_type=F32)


def _dot_tn(a, b):
    return lax.dot_general(a.astype(MXU_DTYPE), b.astype(MXU_DTYPE), (((0,), (0,)), ((), ())),
                           preferred_element_type=F32)


def _dot_exact(a, b):
    return jnp.dot(a, b, precision=lax.Precision.HIGHEST, preferred_element_type=F32)


def _sigmoid(x):
    return 1.0 / (1.0 + jnp.exp(-x))


def _silu(x):
    return x * _sigmoid(x)


def _silu_grad(x):
    s = _sigmoid(x)
    return s * (1.0 + x * (1.0 - s))


_GELU_C = math.sqrt(2.0 / math.pi)


def _gelu(x):
    return 0.5 * x * (1.0 + jnp.tanh(_GELU_C * (x + 0.044715 * x * x * x)))


def _gelu_grad(x):
    t = jnp.tanh(_GELU_C * (x + 0.044715 * x * x * x))
    return 0.5 * (1.0 + t) + 0.5 * x * (1.0 - t * t) * _GELU_C * (1.0 + 3.0 * 0.044715 * x * x)


def _softplus(x):
    return jnp.maximum(x, 0.0) + jnp.log1p(jnp.exp(-jnp.abs(x)))


def _rms_scale(x):
    return lax.rsqrt(jnp.mean(x * x, axis=-1, keepdims=True) + EPS)


def _rms_bwd(dy, x, g):
    r = _rms_scale(x)
    xn = x * r
    dxn = dy * g
    dx = r * (dxn - xn * jnp.mean(dxn * xn, axis=-1, keepdims=True))
    return dx, dy * xn


def _colsum(x):
    return jnp.sum(x, axis=0, keepdims=True)


class _Exchange:
    def __init__(self, arrays, modes):
        self.arrays, self.modes, self.n = list(arrays), list(modes), len(arrays)
        self.blks = []
        for x, mode in zip(arrays, modes):
            if mode == 'gather':
                self.blks.append(tuple(x.shape))
            elif mode == 'slots':
                self.blks.append(tuple(x.shape[1:]))
            else:
                self.blks.append((x.shape[0] // N_DEV,) + tuple(x.shape[1:]))
        self.out_shape = [jax.ShapeDtypeStruct((N_DEV,) + blk, x.dtype) for x, blk in zip(arrays, self.blks)]
        self.in_specs = [pl.BlockSpec(memory_space=pl.ANY)] * self.n
        self.out_specs = [pl.BlockSpec(memory_space=pl.ANY) for _ in range(self.n)]
        n_sem = self.n * (N_DEV - 1)
        self.scratch = [pltpu.SemaphoreType.DMA((n_sem,)), pltpu.SemaphoreType.DMA((n_sem,)),
                        pltpu.SemaphoreType.DMA((self.n,))]

    def _copies(self, x_refs, out_refs, send_sems, recv_sems, local_sems, with_recvs):
        mx, my, mc = lax.axis_index("x"), lax.axis_index("y"), lax.axis_index("c")
        me = 4 * mx + 2 * my + mc

        def flip(v, bit):
            return 1 - v if bit else v

        def part(a, dev):
            if self.modes[a] == 'gather':
                return x_refs[a]
            if self.modes[a] == 'slots':
                return x_refs[a].at[dev]
            r = self.blks[a][0]
            return x_refs[a].at[pl.ds(pl.multiple_of(dev * r, 16), r)]

        sends, recvs, owns = [], [], []
        for k in (1, 2, 4, 6, 3, 5, 7):
            px, py, pc = flip(mx, (k >> 2) & 1), flip(my, (k >> 1) & 1), flip(mc, k & 1)
            peer = 4 * px + 2 * py + pc
            for a in range(self.n):
                sem = a * (N_DEV - 1) + k - 1
                sends.append(pltpu.make_async_remote_copy(
                    src_ref=part(a, peer), dst_ref=out_refs[a].at[me], send_sem=send_sems.at[sem],
                    recv_sem=recv_sems.at[sem], device_id=(px, py, pc), device_id_type=pl.DeviceIdType.MESH))
                if with_recvs:
                    recvs.append(pltpu.make_async_remote_copy(
                        src_ref=part(a, peer), dst_ref=out_refs[a].at[peer], send_sem=send_sems.at[sem],
                        recv_sem=recv_sems.at[sem], device_id=(px, py, pc), device_id_type=pl.DeviceIdType.MESH))
        for a in range(self.n):
            owns.append(pltpu.make_async_copy(part(a, me), out_refs[a].at[me], local_sems.at[a]))
        return sends, recvs, owns

    def start(self, *refs):
        sends, _, owns = self._copies(*refs, with_recvs=False)
        for cp in sends + owns:
            cp.start()

    def wait(self, *refs):
        sends, recvs, owns = self._copies(*refs, with_recvs=True)
        for cp in recvs:
            cp.wait_recv()
        for cp in sends:
            cp.wait_send()
        for cp in owns:
            cp.wait()


def _exchange(arrays, modes, *, name):
    ex = _Exchange(arrays, modes)

    def body(*refs):
        x_refs, out_refs, sems = refs[:ex.n], refs[ex.n:2 * ex.n], refs[2 * ex.n:]
        ex.start(x_refs, out_refs, *sems)
        ex.wait(x_refs, out_refs, *sems)

    return pl.pallas_call(
        body, name=name, out_shape=tuple(ex.out_shape), in_specs=ex.in_specs, out_specs=tuple(ex.out_specs),
        scratch_shapes=ex.scratch,
    )(*arrays)


def _gather_two_level(arrays, *, name):
    n = len(arrays)
    per = N_DEV - 1

    def body(*refs):
        x_refs, out_refs = refs[:n], refs[n:2 * n]
        send_sems, recv_sems, local_sems = refs[2 * n:]
        x, y, c = lax.axis_index("x"), lax.axis_index("y"), lax.axis_index("c")
        me, sibling = (x, y, c), (x, y, 1 - c)
        chips = [(1 - x, y), (x, 1 - y), (1 - x, 1 - y)]

        def copy(a, k, block, to, src=None):
            slot = out_refs[a].at[4 * block[0] + 2 * block[1] + block[2]]
            return pltpu.make_async_remote_copy(
                src_ref=slot if src is None else src, dst_ref=slot, send_sem=send_sems.at[a * per + k],
                recv_sem=recv_sems.at[a * per + k], device_id=to, device_id_type=pl.DeviceIdType.MESH)

        mines = [pltpu.make_async_copy(x_refs[a], out_refs[a].at[4 * x + 2 * y + c], local_sems.at[a]) for a in range(n)]
        firsts = []
        for a in range(n):
            firsts.append(copy(a, 0, me, sibling, src=x_refs[a]))
            firsts += [copy(a, 1 + j, me, (*chip, c), src=x_refs[a]) for j, chip in enumerate(chips)]
        for cp in mines + firsts:
            cp.start()
        passed = []
        for j, chip in enumerate(chips):
            for a in range(n):
                copy(a, 1 + j, (*chip, c), me).wait_recv()
                passed.append(copy(a, 4 + j, (*chip, c), sibling))
                passed[-1].start()
        for a in range(n):
            copy(a, 0, sibling, me).wait_recv()
            for j, chip in enumerate(chips):
                copy(a, 4 + j, (*chip, 1 - c), me).wait_recv()
        for cp in firsts + passed:
            cp.wait_send()
        for cp in mines:
            cp.wait()

    return pl.pallas_call(
        body, name=name,
        out_shape=tuple(jax.ShapeDtypeStruct((N_DEV,) + tuple(a.shape), a.dtype) for a in arrays),
        in_specs=[pl.BlockSpec(memory_space=pl.ANY)] * n,
        out_specs=tuple(pl.BlockSpec(memory_space=pl.ANY) for _ in range(n)),
        scratch_shapes=[pltpu.SemaphoreType.DMA((n * per,)), pltpu.SemaphoreType.DMA((n * per,)),
                        pltpu.SemaphoreType.DMA((n,))],
    )(*arrays)


def _hosted(body, n_in, n_out, n_scratch, grid, ex):
    def wrapped(*refs):
        ins, x_refs = refs[:n_in], refs[n_in:n_in + ex.n]
        outs = refs[n_in + ex.n:n_in + ex.n + n_out]
        xo_refs = refs[n_in + ex.n + n_out:n_in + 2 * ex.n + n_out]
        scr = refs[n_in + 2 * ex.n + n_out:n_in + 2 * ex.n + n_out + n_scratch]
        sems = refs[n_in + 2 * ex.n + n_out + n_scratch:]
        ids = [pl.program_id(d) for d in range(len(grid))]
        first = functools.reduce(jnp.logical_and, [i == 0 for i in ids])
        last = functools.reduce(jnp.logical_and, [i == g - 1 for i, g in zip(ids, grid)])

        @pl.when(first)
        def _():
            ex.start(x_refs, xo_refs, *sems)

        body(*ins, *outs, *scr)

        @pl.when(last)
        def _():
            ex.wait(x_refs, xo_refs, *sems)

    return wrapped


def _call(body, *, name, grid, inputs, in_specs, out_shape, out_specs, scratch, semantics, ex=None):
    if ex is None:
        return pl.pallas_call(
            body, name=name, grid=grid, out_shape=tuple(out_shape), in_specs=list(in_specs),
            out_specs=tuple(out_specs), scratch_shapes=list(scratch), compiler_params=_cparams(semantics))(*inputs)
    n_out = len(out_shape)
    res = pl.pallas_call(
        _hosted(body, len(inputs), n_out, len(scratch), grid, ex), name=name, grid=grid,
        out_shape=tuple(out_shape) + tuple(ex.out_shape), in_specs=list(in_specs) + ex.in_specs,
        out_specs=tuple(out_specs) + tuple(ex.out_specs), scratch_shapes=list(scratch) + ex.scratch,
        compiler_params=_cparams(("arbitrary",) * len(grid)))(*inputs, *ex.arrays)
    return res[:n_out], res[n_out:]


def _mm(a, b, *, name, out_dtype=F32, tm=MM_TM, tn=512, tk=None, ex=None):
    m, k = a.shape
    n = b.shape[1]
    tm, tn = min(tm, m), min(tn, n)
    tk = k if tk is None else tk
    nk = k // tk
    assert m % tm == 0 and n % tn == 0 and k % tk == 0

    def body(a_ref, b_ref, o_ref, acc_ref):
        kk = pl.program_id(2)
        part = _dot(a_ref[...], b_ref[...])
        if nk == 1:
            o_ref[...] = part.astype(out_dtype)
        else:
            @pl.when(kk == 0)
            def _():
                acc_ref[...] = part

            @pl.when(kk > 0)
            def _():
                acc_ref[...] += part

            @pl.when(kk == nk - 1)
            def _():
                o_ref[...] = acc_ref[...].astype(out_dtype)

    res = _call(
        body, name=name, grid=(m // tm, n // tn, nk), inputs=(a, b),
        in_specs=[pl.BlockSpec((tm, tk), lambda i, j, kk: (i, kk)), pl.BlockSpec((tk, tn), lambda i, j, kk: (kk, j))],
        out_shape=[jax.ShapeDtypeStruct((m, n), out_dtype)],
        out_specs=[pl.BlockSpec((tm, tn), lambda i, j, kk: (i, j))],
        scratch=[pltpu.VMEM((tm, tn) if nk > 1 else (8, LANES), F32)],
        semantics=("parallel", "parallel", "arbitrary"), ex=ex)
    return res[0] if ex is None else (res[0][0], res[1])


def _mm_tn(a, b, *, name, out_dtype=F32, tm=1024, tn=512, tk=1024, shift=0):
    t, m = a.shape
    n = b.shape[1]
    tm, tn, tk = min(tm, m), min(tn, n), min(tk, t)
    nk = t // tk
    nb = m // tm
    assert m % tm == 0 and n % tn == 0 and t % tk == 0

    def body(a_ref, b_ref, o_ref, acc_ref):
        kk = pl.program_id(2)
        part = _dot_tn(a_ref[...], b_ref[...])

        @pl.when(kk == 0)
        def _():
            acc_ref[...] = part

        @pl.when(kk > 0)
        def _():
            acc_ref[...] += part

        @pl.when(kk == nk - 1)
        def _():
            o_ref[...] = acc_ref[...].astype(out_dtype)

    return pl.pallas_call(
        body, name=name, grid=(nb, n // tn, nk),
        out_shape=jax.ShapeDtypeStruct((m, n), out_dtype),
        in_specs=[pl.BlockSpec((tk, tm), lambda i, j, kk: (kk, i)), pl.BlockSpec((tk, tn), lambda i, j, kk: (kk, j))],
        out_specs=pl.BlockSpec((tm, tn), lambda i, j, kk: ((i + shift) % nb, j)),
        scratch_shapes=[pltpu.VMEM((tm, tn), F32)],
        compiler_params=_cparams(("parallel", "parallel", "arbitrary")),
    )(a, b)


def _mm_tn_gate_up(dgu, yn, *, name, out_dtype, tk=2048):
    t, m = dgu.shape
    n = yn.shape[1]
    tk = min(tk, t)
    nk = t // tk
    nb = m // (2 * FF_TILE)

    def body(a_ref, b_ref, og_ref, ou_ref, acc_ref):
        kk = pl.program_id(1)
        part = _dot_tn(a_ref[...], b_ref[...])

        @pl.when(kk == 0)
        def _():
            acc_ref[...] = part

        @pl.when(kk > 0)
        def _():
            acc_ref[...] += part

        @pl.when(kk == nk - 1)
        def _():
            og_ref[...] = acc_ref[:FF_TILE, :].astype(out_dtype)
            ou_ref[...] = acc_ref[FF_TILE:, :].astype(out_dtype)

    out = jax.ShapeDtypeStruct((m // 2, n), out_dtype)
    o_spec = pl.BlockSpec((FF_TILE, n), lambda i, kk: (i, 0))
    return pl.pallas_call(
        body, name=name, grid=(nb, nk), out_shape=(out, out),
        in_specs=[pl.BlockSpec((tk, 2 * FF_TILE), lambda i, kk: (kk, i)), pl.BlockSpec((tk, n), lambda i, kk: (kk, 0))],
        out_specs=(o_spec, o_spec),
        scratch_shapes=[pltpu.VMEM((2 * FF_TILE, n), F32)],
        compiler_params=_cparams(("parallel", "arbitrary")),
    )(dgu, yn)


def _mm_swiglu(a, w_gu, *, name, tm=MM_TM, ex=None):
    m, k = a.shape
    n = w_gu.shape[1]
    nt = n // (2 * FF_TILE)
    tm = min(tm, m)

    def body(a_ref, b_ref, gu_ref, act_ref):
        gu = _dot(a_ref[...], b_ref[...])
        gu_ref[...] = gu.astype(MXU_DTYPE)
        act_ref[...] = (_silu(gu[:, :FF_TILE]) * gu[:, FF_TILE:]).astype(MXU_DTYPE)

    return _call(
        body, name=name, grid=(m // tm, nt), inputs=(a, w_gu),
        in_specs=[pl.BlockSpec((tm, k), lambda i, j: (i, 0)), pl.BlockSpec((k, 2 * FF_TILE), lambda i, j: (0, j))],
        out_shape=[jax.ShapeDtypeStruct((m, n), MXU_DTYPE), jax.ShapeDtypeStruct((m, n // 2), MXU_DTYPE)],
        out_specs=[pl.BlockSpec((tm, 2 * FF_TILE), lambda i, j: (i, j)), pl.BlockSpec((tm, FF_TILE), lambda i, j: (i, j))],
        scratch=[], semantics=("parallel", "parallel"), ex=ex)


def _mm_dswiglu(dd, w_down_t, gu, *, name, tm=MM_TM):
    m, k = dd.shape
    n = gu.shape[1]
    nt = n // (2 * FF_TILE)
    tm = min(tm, m)

    def body(d_ref, w_ref, gu_ref, o_ref):
        dact = _dot(d_ref[...], w_ref[...])
        gate, up = gu_ref[:, :FF_TILE].astype(F32), gu_ref[:, FF_TILE:].astype(F32)
        o_ref[:, :FF_TILE] = (dact * up * _silu_grad(gate)).astype(MXU_DTYPE)
        o_ref[:, FF_TILE:] = (dact * _silu(gate)).astype(MXU_DTYPE)

    return pl.pallas_call(
        body, name=name, grid=(m // tm, nt),
        out_shape=jax.ShapeDtypeStruct((m, n), MXU_DTYPE),
        in_specs=[pl.BlockSpec((tm, k), lambda i, j: (i, 0)), pl.BlockSpec((k, FF_TILE), lambda i, j: (0, j)),
                  pl.BlockSpec((tm, 2 * FF_TILE), lambda i, j: (i, j))],
        out_specs=pl.BlockSpec((tm, 2 * FF_TILE), lambda i, j: (i, j)),
        compiler_params=_cparams(("parallel", "parallel")),
    )(dd, w_down_t, gu)


def _row_spec(width, tr=ROW_TILE):
    return pl.BlockSpec((tr, width), lambda i: (i, 0))


def _vec_spec(width, rows=1):
    return pl.BlockSpec((rows, width), lambda i: (0, 0))


def _rn_fwd(h, g, *, name, out_dtype):
    rows, d = h.shape

    def body(h_ref, g_ref, o_ref):
        x = h_ref[...]
        o_ref[...] = (x * _rms_scale(x) * g_ref[...]).astype(out_dtype)

    return pl.pallas_call(
        body, name=name, grid=(rows // ROW_TILE,),
        out_shape=jax.ShapeDtypeStruct((rows, d), out_dtype),
        in_specs=[_row_spec(d), _vec_spec(d)], out_specs=_row_spec(d),
        compiler_params=_cparams(("parallel",)),
    )(h, g)


def _resid_rn_fwd(h_in, o, g_post, g_next, *, name, next_dtype):
    rows, d = h_in.shape

    def body(h_ref, o_ref, gp_ref, gn_ref, ho_ref, yn_ref):
        ov = o_ref[...]
        h = h_ref[...] + ov * _rms_scale(ov) * gp_ref[...]
        ho_ref[...] = h
        yn_ref[...] = (h * _rms_scale(h) * gn_ref[...]).astype(next_dtype)

    return pl.pallas_call(
        body, name=name, grid=(rows // ROW_TILE,),
        out_shape=(jax.ShapeDtypeStruct((rows, d), F32), jax.ShapeDtypeStruct((rows, d), next_dtype)),
        in_specs=[_row_spec(d), _row_spec(d), _vec_spec(d), _vec_spec(d)],
        out_specs=(_row_spec(d), _row_spec(d)),
        compiler_params=_cparams(("parallel",)),
    )(h_in, o, g_post, g_next)


def _resid_loss(h_in, o, g_post, target, *, name):
    rows, d = h_in.shape

    def body(h_ref, o_ref, gp_ref, t_ref, dh_ref, loss_ref):
        ov = o_ref[...]
        err = h_ref[...] + ov * _rms_scale(ov) * gp_ref[...] - t_ref[...]
        dh_ref[...] = err * (1.0 / d)

        @pl.when(pl.program_id(0) == 0)
        def _():
            loss_ref[...] = jnp.zeros_like(loss_ref)

        loss_ref[...] += 0.5 * jnp.sum(jnp.mean(err * err, axis=-1, keepdims=True), axis=0, keepdims=True)

    return pl.pallas_call(
        body, name=name, grid=(rows // ROW_TILE,),
        out_shape=(jax.ShapeDtypeStruct((rows, d), F32), jax.ShapeDtypeStruct((1, 1), F32)),
        in_specs=[_row_spec(d), _row_spec(d), _vec_spec(d), _row_spec(d)],
        out_specs=(_row_spec(d), pl.BlockSpec((1, 1), lambda i: (0, 0))),
        compiler_params=_cparams(("arbitrary",)),
    )(h_in, o, g_post, target)


def _resid_bwd_post(dh, o, g_post, *, name, out_dtype):
    rows, d = dh.shape

    def body(dh_ref, o_ref, g_ref, do_ref, dg_ref):
        do, dg = _rms_bwd(dh_ref[...], o_ref[...], g_ref[...])
        do_ref[...] = do.astype(out_dtype)

        @pl.when(pl.program_id(0) == 0)
        def _():
            dg_ref[...] = jnp.zeros_like(dg_ref)

        dg_ref[...] += _colsum(dg)

    return pl.pallas_call(
        body, name=name, grid=(rows // ROW_TILE,),
        out_shape=(jax.ShapeDtypeStruct((rows, d), out_dtype), jax.ShapeDtypeStruct((1, d), F32)),
        in_specs=[_row_spec(d), _row_spec(d), _vec_spec(d)],
        out_specs=(_row_spec(d), _vec_spec(d)),
        compiler_params=_cparams(("arbitrary",)),
    )(dh, o, g_post)


def _resid_bwd_pre(dh, dyn_list, h_in, g_pre, *, name):
    rows, d = dh.shape
    n_dyn = len(dyn_list)

    def body(*refs):
        dh_ref, dyn_refs, h_ref, g_ref, out_ref, dg_ref = refs[0], refs[1:1 + n_dyn], *refs[1 + n_dyn:]
        dyn = dyn_refs[0][...]
        for r in dyn_refs[1:]:
            dyn = dyn + r[...]
        dx, dg = _rms_bwd(dyn, h_ref[...], g_ref[...])
        out_ref[...] = dh_ref[...] + dx

        @pl.when(pl.program_id(0) == 0)
        def _():
            dg_ref[...] = jnp.zeros_like(dg_ref)

        dg_ref[...] += _colsum(dg)

    return pl.pallas_call(
        body, name=name, grid=(rows // ROW_TILE,),
        out_shape=(jax.ShapeDtypeStruct((rows, d), F32), jax.ShapeDtypeStruct((1, d), F32)),
        in_specs=[_row_spec(d)] + [_row_spec(d)] * n_dyn + [_row_spec(d), _vec_spec(d)],
        out_specs=(_row_spec(d), _vec_spec(d)),
        compiler_params=_cparams(("arbitrary",)),
    )(dh, *dyn_list, h_in, g_pre)


def _layer_norm_stats(x):
    mu = jnp.mean(x, axis=-1, keepdims=True)
    xc = x - mu
    rstd = lax.rsqrt(jnp.mean(xc * xc, axis=-1, keepdims=True) + EPS)
    return xc * rstd, rstd


def _gmlp_fwd(proj, ln_g, ln_b, wm, bcol, *, name):
    rows = proj.shape[0]
    tr = ROW_TILE

    def body(u_ref, v_ref, lg_ref, lb_ref, wm_ref, bc_ref, ya_ref):
        vhat, _ = _layer_norm_stats(_gelu(v_ref[...]))
        vl = (vhat * lg_ref[...] + lb_ref[...]).astype(MXU_DTYPE)
        gu = _gelu(u_ref[...])
        bc = bc_ref[...]
        for c in range(tr // CHUNK):
            rs = slice(c * CHUNK, (c + 1) * CHUNK)
            for h in range(GM_HEADS):
                cs = slice(h * GM_HEAD_DIM, (h + 1) * GM_HEAD_DIM)
                mixed = _dot(wm_ref[h], vl[rs, cs]) + bc[:, h:h + 1]
                ya_ref[rs, cs] = (gu[rs, cs] * mixed).astype(MXU_DTYPE)

    return pl.pallas_call(
        body, name=name, grid=(rows // tr,),
        out_shape=jax.ShapeDtypeStruct((rows, 2 * D_MODEL), MXU_DTYPE),
        in_specs=[pl.BlockSpec((tr, D_MODEL), lambda i: (i, 2)), pl.BlockSpec((tr, D_MODEL), lambda i: (i, 3)),
                  _vec_spec(D_MODEL), _vec_spec(D_MODEL),
                  pl.BlockSpec((GM_HEADS, CHUNK, CHUNK), lambda i: (0, 0, 0)), _vec_spec(LANES, CHUNK)],
        out_specs=_row_spec(D_MODEL, tr),
        compiler_params=_cparams(("parallel",)),
    )(proj, proj, ln_g, ln_b, wm, bcol)


def _gmlp_bwd(proj, dcat, ln_g, ln_b, wm, wm_t, bcol, *, name):
    rows = proj.shape[0]
    tr = ROW_TILE

    def body(u_ref, v_ref, dy_ref, lg_ref, lb_ref, wm_ref, wmt_ref, bc_ref,
             duv_ref, dwm_ref, dbc_ref, dlg_ref, dlb_ref, dvl_scr):
        @pl.when(pl.program_id(0) == 0)
        def _():
            dwm_ref[...] = jnp.zeros_like(dwm_ref)
            dbc_ref[...] = jnp.zeros_like(dbc_ref)
            dlg_ref[...] = jnp.zeros_like(dlg_ref)
            dlb_ref[...] = jnp.zeros_like(dlb_ref)

        u, v = u_ref[...], v_ref[...]
        gv = _gelu(v)
        vhat, rstd = _layer_norm_stats(gv)
        lg = lg_ref[...]
        vl = (vhat * lg + lb_ref[...]).astype(MXU_DTYPE)
        gu = _gelu(u)
        dy = dy_ref[...]
        bc = bc_ref[...]
        row = lax.broadcasted_iota(jnp.int32, (CHUNK, CHUNK), 0)
        lane = lax.broadcasted_iota(jnp.int32, (CHUNK, CHUNK), 1)
        causal = lane <= row
        dbc = jnp.zeros((CHUNK, LANES), F32)
        for c in range(tr // CHUNK):
            rs = slice(c * CHUNK, (c + 1) * CHUNK)
            for h in range(GM_HEADS):
                cs = slice(h * GM_HEAD_DIM, (h + 1) * GM_HEAD_DIM)
                vl_h = vl[rs, cs]
                mixed = _dot(wm_ref[h], vl_h) + bc[:, h:h + 1]
                dy_h = dy[rs, cs]
                duv_ref[rs, cs] = (dy_h * mixed * _gelu_grad(u[rs, cs])).astype(MXU_DTYPE)
                dmixed = dy_h * gu[rs, cs]
                dwm_ref[h] += jnp.where(causal, _dot_nt(dmixed, vl_h), 0.0)
                dbc = dbc + jnp.where(lane == h, jnp.sum(dmixed, axis=1, keepdims=True), 0.0)
                dvl_scr[rs, cs] = _dot(wmt_ref[h], dmixed)
        dbc_ref[...] += dbc
        dvl = dvl_scr[...]
        dlg_ref[...] += _colsum(dvl * vhat)
        dlb_ref[...] += _colsum(dvl)
        dvh = dvl * lg
        dgv = rstd * (dvh - jnp.mean(dvh, axis=-1, keepdims=True) - vhat * jnp.mean(dvh * vhat, axis=-1, keepdims=True))
        duv_ref[:, D_MODEL:] = (dgv * _gelu_grad(v)).astype(MXU_DTYPE)

    return pl.pallas_call(
        body, name=name, grid=(rows // tr,),
        out_shape=(jax.ShapeDtypeStruct((rows, IN_MAIN), MXU_DTYPE),
                   jax.ShapeDtypeStruct((GM_HEADS, CHUNK, CHUNK), F32), jax.ShapeDtypeStruct((CHUNK, LANES), F32),
                   jax.ShapeDtypeStruct((1, D_MODEL), F32), jax.ShapeDtypeStruct((1, D_MODEL), F32)),
        in_specs=[pl.BlockSpec((tr, D_MODEL), lambda i: (i, 2)), pl.BlockSpec((tr, D_MODEL), lambda i: (i, 3)),
                  pl.BlockSpec((tr, D_MODEL), lambda i: (i, 0)), _vec_spec(D_MODEL), _vec_spec(D_MODEL),
                  pl.BlockSpec((GM_HEADS, CHUNK, CHUNK), lambda i: (0, 0, 0)),
                  pl.BlockSpec((GM_HEADS, CHUNK, CHUNK), lambda i: (0, 0, 0)), _vec_spec(LANES, CHUNK)],
        out_specs=(pl.BlockSpec((tr, 2 * D_MODEL), lambda i: (i, 1)),
                   pl.BlockSpec((GM_HEADS, CHUNK, CHUNK), lambda i: (0, 0, 0)), _vec_spec(LANES, CHUNK),
                   _vec_spec(D_MODEL), _vec_spec(D_MODEL)),
        scratch_shapes=[pltpu.VMEM((tr, D_MODEL), F32)],
        compiler_params=_cparams(("arbitrary",)),
    )(proj, proj, dcat, ln_g, ln_b, wm, wm_t, bcol)


def _conv_fwd(proj, conv_w8, conv_b, *, name):
    rows = proj.shape[0]
    tr = ROW_TILE
    hb = tr // CONV_HALO

    def body(x_ref, prev_ref, w_ref, b_ref, pre_ref, buf):
        first = pl.program_id(0) == 0
        buf[pl.ds(0, CONV_HALO), :] = jnp.where(first, 0.0, prev_ref[...])
        buf[pl.ds(CONV_HALO, tr), :] = x_ref[...]
        acc = jnp.broadcast_to(b_ref[...], (tr, CONV_DIM))
        for k in range(SSM_CONV):
            acc = acc + w_ref[k:k + 1, :] * buf[pl.ds(CONV_HALO - (SSM_CONV - 1) + k, tr), :]
        pre_ref[...] = acc

    return pl.pallas_call(
        body, name=name, grid=(rows // tr,),
        out_shape=jax.ShapeDtypeStruct((rows, CONV_DIM), F32),
        in_specs=[pl.BlockSpec((tr, CONV_DIM), lambda i: (i, 0)),
                  pl.BlockSpec((CONV_HALO, CONV_DIM), lambda i: (jnp.maximum(i * hb - 1, 0), 0)),
                  _vec_spec(CONV_DIM, 8), _vec_spec(CONV_DIM)],
        out_specs=_row_spec(CONV_DIM, tr),
        scratch_shapes=[pltpu.VMEM((tr + CONV_HALO, CONV_DIM), F32)],
        compiler_params=_cparams(("parallel",)),
    )(proj, proj, conv_w8, conv_b)


def _conv_bwd(dpre, proj, conv_w8, dproj, *, name):
    rows = proj.shape[0]
    tr = ROW_TILE
    hb = tr // CONV_HALO
    nblk = rows // tr

    def body(d_ref, dnext_ref, x_ref, w_ref, dproj_ref, dx_ref, dw_ref, db_ref, dbuf):
        i = pl.program_id(0)

        @pl.when(i == 0)
        def _():
            dw_ref[...] = jnp.zeros_like(dw_ref)
            db_ref[...] = jnp.zeros_like(db_ref)

        d = d_ref[...]
        x = x_ref[...]
        dbuf[pl.ds(0, tr), :] = d
        dbuf[pl.ds(tr, CONV_HALO), :] = jnp.where(i == nblk - 1, 0.0, dnext_ref[...])
        acc = jnp.zeros((tr, CONV_DIM), F32)
        for k in range(SSM_CONV):
            shifted = dbuf[pl.ds(SSM_CONV - 1 - k, tr), :]
            acc = acc + w_ref[k:k + 1, :] * shifted
            dw_ref[k:k + 1, :] += _colsum(shifted * x)
        dx_ref[...] = acc.astype(MXU_DTYPE)
        db_ref[...] += _colsum(d)

    return pl.pallas_call(
        body, name=name, grid=(nblk,),
        out_shape=(jax.ShapeDtypeStruct((rows, IN_MAIN), MXU_DTYPE), jax.ShapeDtypeStruct((8, CONV_DIM), F32),
                   jax.ShapeDtypeStruct((1, CONV_DIM), F32)),
        in_specs=[_row_spec(CONV_DIM, tr),
                  pl.BlockSpec((CONV_HALO, CONV_DIM), lambda i: (jnp.minimum((i + 1) * hb, rows // CONV_HALO - 1), 0)),
                  pl.BlockSpec((tr, CONV_DIM), lambda i: (i, 0)),
                  _vec_spec(CONV_DIM, 8), pl.BlockSpec(memory_space=pl.ANY)],
        out_specs=(_row_spec(CONV_DIM, tr), _vec_spec(CONV_DIM, 8), _vec_spec(CONV_DIM)),
        scratch_shapes=[pltpu.VMEM((tr + CONV_HALO, CONV_DIM), F32)],
        input_output_aliases={4: 0},
        compiler_params=_cparams(("arbitrary",)),
    )(dpre, dpre, proj, conv_w8, dproj)


N_PAIRS = SSM_HEADS // 2


def _chunk_iotas():
    row = lax.broadcasted_iota(jnp.int32, (CHUNK, CHUNK), 0)
    lane = lax.broadcasted_iota(jnp.int32, (CHUNK, CHUNK), 1)
    return row, lane, lane <= row


def _silu_and_grad(x):
    s = _sigmoid(x)
    return x * s, s * (1.0 + x * (1.0 - s))


def _pair_select(lo, mat, ha):
    return jnp.where(lo, mat[:, ha:ha + 1], mat[:, ha + 1:ha + 2])


def _ssd_fwd(pre, dtr, proj, dtb, alog, dsk, gn, *, name):
    rows = pre.shape[0]
    nc = rows // CHUNK

    def body(pre_ref, dtr_ref, z_ref, dtb_ref, alog_ref, dsk_ref, gn_ref, yb_ref, y_ref, st_ref, dt_ref, acum_ref, s_scr):
        @pl.when(pl.program_id(0) == 0)
        def _():
            s_scr[...] = jnp.zeros_like(s_scr)

        row, lane, tril = _chunk_iotas()
        dt = _softplus(dtr_ref[...] + dtb_ref[...])
        acum = _dot_exact(tril.astype(F32), dt * (-jnp.exp(alog_ref[...])))
        dt_ref[...] = dt
        acum_ref[...] = acum
        acum_t = acum.T
        lo = lane < 64
        eacum = jnp.exp(acum)
        a_end = acum[CHUNK - 1:CHUNK, :]
        e_end = jnp.exp(a_end)
        dte_all = jnp.exp(a_end - acum)
        dsk_v = dsk_ref[...]
        for g in range(SSM_GROUPS):
            b_g = _silu(pre_ref[:, 1024 + SSM_STATE * g:1024 + SSM_STATE * (g + 1)]).astype(MXU_DTYPE)
            c_g = _silu(pre_ref[:, 1536 + SSM_STATE * g:1536 + SSM_STATE * (g + 1)]).astype(MXU_DTYPE)
            cb = _dot_nt(c_g, b_g)
            gated = []
            for jj in range(2):
                j = 2 * g + jj
                ha = 2 * j
                cs = slice(LANES * j, LANES * (j + 1))
                xs = _silu(pre_ref[:, cs])
                xdt = xs * _pair_select(lo, dt, ha)
                xdt_m = xdt.astype(MXU_DTYPE)
                y_heads = []
                for h in (ha, ha + 1):
                    dec = jnp.exp(jnp.where(tril, acum[:, h:h + 1] - acum_t[h:h + 1, :], -jnp.inf))
                    y_heads.append(_dot(cb * dec, xdt_m))
                s_prev = s_scr[j]
                st_ref[0, j] = s_prev
                y = jnp.where(lo, y_heads[0], y_heads[1])
                y = y + _dot_nt(c_g, s_prev) * _pair_select(lo, eacum, ha)
                y = y + _pair_select(lo, dsk_v, ha) * xs
                xw = xdt * _pair_select(lo, dte_all, ha)
                e_rows = jnp.where(row < 64, e_end[:, ha:ha + 1], e_end[:, ha + 1:ha + 2])
                s_scr[j] = e_rows * s_prev + _dot(xw.T, b_g)
                y_ref[:, cs] = y
                gated.append(y * _silu(z_ref[:, cs]))
            ms = (jnp.sum(gated[0] * gated[0], axis=1, keepdims=True)
                  + jnp.sum(gated[1] * gated[1], axis=1, keepdims=True)) * (1.0 / 256.0)
            r = lax.rsqrt(ms + EPS)
            for jj in range(2):
                cs = slice(LANES * (2 * g + jj), LANES * (2 * g + jj + 1))
                yb_ref[:, cs] = (gated[jj] * r * gn_ref[:, cs]).astype(MXU_DTYPE)

    return pl.pallas_call(
        body, name=name, grid=(nc,),
        out_shape=(jax.ShapeDtypeStruct((rows, D_MODEL), MXU_DTYPE), jax.ShapeDtypeStruct((rows, D_MODEL), F32),
                   jax.ShapeDtypeStruct((nc, N_PAIRS, LANES, SSM_STATE), F32),
                   jax.ShapeDtypeStruct((rows, LANES), F32), jax.ShapeDtypeStruct((rows, LANES), F32)),
        in_specs=[_row_spec(CONV_DIM, CHUNK), _row_spec(LANES, CHUNK), pl.BlockSpec((CHUNK, D_MODEL), lambda i: (i, 4)),
                  _vec_spec(LANES), _vec_spec(LANES), _vec_spec(LANES), _vec_spec(D_MODEL)],
        out_specs=(_row_spec(D_MODEL, CHUNK), _row_spec(D_MODEL, CHUNK),
                   pl.BlockSpec((1, N_PAIRS, LANES, SSM_STATE), lambda i: (i, 0, 0, 0)),
                   _row_spec(LANES, CHUNK), _row_spec(LANES, CHUNK)),
        scratch_shapes=[pltpu.VMEM((N_PAIRS, LANES, SSM_STATE), F32)],
        compiler_params=_cparams(("arbitrary",)),
    )(pre, dtr, proj, dtb, alog, dsk, gn)


def _ssd_bwd(pre, dtr, dt_saved, acum_saved, proj, y_saved, states, dcat, dtb, alog, dsk, gn, *, name):
    rows = pre.shape[0]
    nc = rows // CHUNK

    def rev(i):
        return nc - 1 - i

    def body(pre_ref, dtr_ref, dt_ref, acum_ref, z_ref, y_ref, st_ref, dyb_ref, dtb_ref, alog_ref, dsk_ref, gn_ref,
             dpre_ref, dz_ref, ddtr_ref, dgn_ref, dvec_ref, g_scr):
        @pl.when(pl.program_id(0) == 0)
        def _():
            g_scr[...] = jnp.zeros_like(g_scr)
            dgn_ref[...] = jnp.zeros_like(dgn_ref)
            dvec_ref[...] = jnp.zeros_like(dvec_ref)

        dtb = dtb_ref[...]
        dtr = dtr_ref[...]
        row, lane, tril = _chunk_iotas()
        dt, acum = dt_ref[...], acum_ref[...]
        a = -jnp.exp(alog_ref[...])
        acum_t = acum.T
        lo = lane < 64
        eacum = jnp.exp(acum)
        a_end = acum[CHUNK - 1:CHUNK, :]
        e_end = jnp.exp(a_end)
        dte_all = jnp.exp(a_end - acum)
        dsk_v = dsk_ref[...]
        zero = jnp.zeros((CHUNK, LANES), F32)
        dacum_c, dacum_r, ddt_c = zero, zero, zero
        d_aend = jnp.zeros((1, LANES), F32)
        d_dsk = jnp.zeros((1, LANES), F32)
        lane1 = lane[0:1, :]

        def put_col(acc, h, colvec):
            return acc + jnp.where(lane == h, colvec, 0.0)

        for g in range(SSM_GROUPS):
            gated, sz, dgh = [], [], []
            for jj in range(2):
                cs = slice(LANES * (2 * g + jj), LANES * (2 * g + jj + 1))
                sz.append(_silu_and_grad(z_ref[:, cs]))
                gated.append(y_ref[:, cs] * sz[jj][0])
                dgh.append(dyb_ref[:, cs] * gn_ref[:, cs])
            ms = (jnp.sum(gated[0] * gated[0], axis=1, keepdims=True)
                  + jnp.sum(gated[1] * gated[1], axis=1, keepdims=True)) * (1.0 / 256.0)
            r = lax.rsqrt(ms + EPS)
            proj_g = (jnp.sum(dgh[0] * gated[0], axis=1, keepdims=True)
                      + jnp.sum(dgh[1] * gated[1], axis=1, keepdims=True)) * (1.0 / 256.0)
            dys = []
            for jj in range(2):
                cs = slice(LANES * (2 * g + jj), LANES * (2 * g + jj + 1))
                dgn_ref[:, cs] += _colsum(dyb_ref[:, cs] * gated[jj] * r)
                dgated = r * dgh[jj] - gated[jj] * (r * r * r * proj_g)
                dys.append(dgated * sz[jj][0])
                dz_ref[:, cs] = (dgated * y_ref[:, cs] * sz[jj][1]).astype(MXU_DTYPE)

            b_f, b_grad = _silu_and_grad(pre_ref[:, 1024 + SSM_STATE * g:1024 + SSM_STATE * (g + 1)])
            c_f, c_grad = _silu_and_grad(pre_ref[:, 1536 + SSM_STATE * g:1536 + SSM_STATE * (g + 1)])
            b_g = b_f.astype(MXU_DTYPE)
            c_g = c_f.astype(MXU_DTYPE)
            cb = _dot_nt(c_g, b_g)
            dcb = zero
            db_g, dc_g = zero, zero
            for jj in range(2):
                j = 2 * g + jj
                ha = 2 * j
                cs = slice(LANES * j, LANES * (j + 1))
                xs, xs_grad = _silu_and_grad(pre_ref[:, cs])
                dtsel = _pair_select(lo, dt, ha)
                xdt = xs * dtsel
                xdt_m = xdt.astype(MXU_DTYPE)
                dyp = dys[jj]
                dyp_m = dyp.astype(MXU_DTYPE)
                s_prev = st_ref[0, j]
                g_next = g_scr[j]
                eac = _pair_select(lo, eacum, ha)
                dte = _pair_select(lo, dte_all, ha)
                yoff = _dot_nt(c_g, s_prev) * eac
                t_off = dyp * yoff
                dye = dyp * eac
                dc_g = dc_g + _dot(dye, s_prev)
                bg = _dot_nt(b_g, g_next)
                dxdt = bg * dte
                xw = xdt * dte
                db_g = db_g + _dot(xw, g_next)
                t_w = xw * bg
                gs = g_next * s_prev
                e_rows = jnp.where(row < 64, e_end[:, ha:ha + 1], e_end[:, ha + 1:ha + 2])
                g_scr[j] = e_rows * g_next + _dot(dye.T, c_g)
                dxdt_heads = []
                for hh, h in enumerate((ha, ha + 1)):
                    half = slice(64 * hh, 64 * (hh + 1))
                    dec = jnp.exp(jnp.where(tril, acum[:, h:h + 1] - acum_t[h:h + 1, :], -jnp.inf))
                    m_h = cb * dec
                    dy_h = jnp.where(lo if hh == 0 else jnp.logical_not(lo), dyp, 0.0)
                    dm = _dot_nt(dy_h, xdt_m)
                    dxdt_heads.append(_dot(m_h.T, dyp_m))
                    e_h = dm * m_h
                    dcb = dcb + dm * dec
                    w_col = jnp.sum(t_w[:, half], axis=1, keepdims=True)
                    col = (jnp.sum(e_h, axis=1, keepdims=True) + jnp.sum(t_off[:, half], axis=1, keepdims=True) - w_col)
                    dacum_c = put_col(dacum_c, h, col)
                    dacum_r = dacum_r + jnp.where(row == h, _colsum(e_h), 0.0)
                    d_end_h = jnp.sum(w_col, keepdims=True) + e_end[:, h:h + 1] * jnp.sum(gs[half, :], keepdims=True)
                    d_aend = d_aend + jnp.where(lane1 == h, d_end_h, 0.0)
                dxdt = dxdt + jnp.where(lo, dxdt_heads[0], dxdt_heads[1])
                dsel = _pair_select(lo, dsk_v, ha)
                dxs = dxdt * dtsel + dsel * dyp
                dpre_ref[:, cs] = dxs * xs_grad
                t_dt = dxdt * xs
                t_dk = dyp * xs
                for hh, h in enumerate((ha, ha + 1)):
                    half = slice(64 * hh, 64 * (hh + 1))
                    ddt_c = put_col(ddt_c, h, jnp.sum(t_dt[:, half], axis=1, keepdims=True))
                    d_dsk = d_dsk + jnp.where(lane1 == h, jnp.sum(t_dk[:, half], keepdims=True), 0.0)
            dc_g = dc_g + _dot(dcb, b_g)
            db_g = db_g + _dot(dcb.T, c_g)
            dpre_ref[:, 1024 + SSM_STATE * g:1024 + SSM_STATE * (g + 1)] = db_g * b_grad
            dpre_ref[:, 1536 + SSM_STATE * g:1536 + SSM_STATE * (g + 1)] = dc_g * c_grad

        dacum = dacum_c - dacum_r.T + jnp.where(row == CHUNK - 1, d_aend, 0.0)
        dda = _dot_exact((lane >= row).astype(F32), dacum)
        ddt = dda * a + ddt_c
        ddtr = ddt * _sigmoid(dtr + dtb)
        ddtr_ref[...] = ddtr.astype(MXU_DTYPE)
        dvec_ref[0:1, :] += _colsum(ddtr)
        dvec_ref[1:2, :] += _colsum(dda * dt)
        dvec_ref[2:3, :] += d_dsk

    return pl.pallas_call(
        body, name=name, grid=(nc,),
        out_shape=(jax.ShapeDtypeStruct((rows, CONV_DIM), F32), jax.ShapeDtypeStruct((rows, D_MODEL), MXU_DTYPE),
                   jax.ShapeDtypeStruct((rows, LANES), MXU_DTYPE), jax.ShapeDtypeStruct((1, D_MODEL), F32),
                   jax.ShapeDtypeStruct((8, LANES), F32)),
        in_specs=[pl.BlockSpec((CHUNK, CONV_DIM), lambda i: (rev(i), 0)), pl.BlockSpec((CHUNK, LANES), lambda i: (rev(i), 0)),
                  pl.BlockSpec((CHUNK, LANES), lambda i: (rev(i), 0)), pl.BlockSpec((CHUNK, LANES), lambda i: (rev(i), 0)),
                  pl.BlockSpec((CHUNK, D_MODEL), lambda i: (rev(i), 4)), pl.BlockSpec((CHUNK, D_MODEL), lambda i: (rev(i), 0)),
                  pl.BlockSpec((1, N_PAIRS, LANES, SSM_STATE), lambda i: (rev(i), 0, 0, 0)),
                  pl.BlockSpec((CHUNK, D_MODEL), lambda i: (rev(i), 1)),
                  _vec_spec(LANES), _vec_spec(LANES), _vec_spec(LANES), _vec_spec(D_MODEL)],
        out_specs=(pl.BlockSpec((CHUNK, CONV_DIM), lambda i: (rev(i), 0)), pl.BlockSpec((CHUNK, D_MODEL), lambda i: (rev(i), 0)),
                   pl.BlockSpec((CHUNK, LANES), lambda i: (rev(i), 0)), _vec_spec(D_MODEL), _vec_spec(LANES, 8)),
        scratch_shapes=[pltpu.VMEM((N_PAIRS, LANES, SSM_STATE), F32)],
        compiler_params=_cparams(("arbitrary",)),
    )(pre, dtr, dt_saved, acum_saved, proj, y_saved, states, dcat, dtb, alog, dsk, gn)


GROUP_DIM = D_MODEL // SSM_GROUPS
HEADS_PER_GROUP = SSM_HEADS // SSM_GROUPS
HEAD_DIM = GROUP_DIM // HEADS_PER_GROUP


def _by_quarter(index, pieces):
    out = pieces[3]
    for q in (2, 1, 0):
        out = jnp.where(index == q, pieces[q], out)
    return out


def _ssd_fwd_grouped(pre, dtr, proj, dtb, alog, dsk, gn, cat, *, name):
    rows = pre.shape[0]
    nc = rows // CHUNK

    def body(pre_ref, dtr_ref, z_ref, dtb_ref, alog_ref, dsk_ref, gn_ref, cat_ref, yb_ref, y_ref, st_ref, dt_ref,
             acum_ref, s_scr):
        @pl.when(pl.program_id(0) == 0)
        def _():
            s_scr[...] = jnp.zeros_like(s_scr)

        row, lane, tril = _chunk_iotas()
        dt = _softplus(dtr_ref[...] + dtb_ref[...])
        acum = _dot_exact(tril.astype(F32), dt * (-jnp.exp(alog_ref[...])))
        dt_ref[...] = dt
        acum_ref[...] = acum
        acum_t = acum.T
        eacum = jnp.exp(acum)
        a_end = acum[CHUNK - 1:CHUNK, :]
        e_end = jnp.exp(a_end)
        dte_all = jnp.exp(a_end - acum)
        dsk_v = dsk_ref[...]
        lane_q = lax.broadcasted_iota(jnp.int32, (CHUNK, GROUP_DIM), 1) // HEAD_DIM
        row_q = lax.broadcasted_iota(jnp.int32, (GROUP_DIM, SSM_STATE), 0) // HEAD_DIM

        def cols(mat, g):
            return _by_quarter(lane_q, [mat[:, 4 * g + q:4 * g + q + 1] for q in range(HEADS_PER_GROUP)])

        for g in range(SSM_GROUPS):
            cs = slice(GROUP_DIM * g, GROUP_DIM * (g + 1))
            b_g = _silu(pre_ref[:, 1024 + SSM_STATE * g:1024 + SSM_STATE * (g + 1)]).astype(MXU_DTYPE)
            c_g = _silu(pre_ref[:, 1536 + SSM_STATE * g:1536 + SSM_STATE * (g + 1)]).astype(MXU_DTYPE)
            cb = _dot_nt(c_g, b_g)
            xs = _silu(pre_ref[:, cs])
            xdt = xs * cols(dt, g)
            m_stack = jnp.concatenate(
                [(cb * jnp.exp(jnp.where(tril, acum[:, h:h + 1] - acum_t[h:h + 1, :], -jnp.inf))).astype(MXU_DTYPE)
                 for h in range(4 * g, 4 * g + 4)], axis=0)
            y_all = _dot(m_stack, xdt)
            y = _by_quarter(lane_q, [y_all[CHUNK * q:CHUNK * (q + 1)] for q in range(HEADS_PER_GROUP)])
            s_prev = s_scr[g]
            st_ref[0, g] = s_prev
            y = y + _dot_nt(c_g, s_prev) * cols(eacum, g) + cols(dsk_v, g) * xs
            xw = xdt * cols(dte_all, g)
            e_rows = _by_quarter(row_q, [e_end[:, 4 * g + q:4 * g + q + 1] for q in range(HEADS_PER_GROUP)])
            s_scr[g] = e_rows * s_prev + _dot(xw.T, b_g)
            y_ref[:, cs] = y
            gated = y * _silu(z_ref[:, cs])
            r = lax.rsqrt(jnp.mean(gated * gated, axis=1, keepdims=True) + EPS)
            yb_ref[:, cs] = (gated * r * gn_ref[:, cs]).astype(MXU_DTYPE)

    return pl.pallas_call(
        body, name=name, grid=(nc,),
        out_shape=(jax.ShapeDtypeStruct((rows, 2 * D_MODEL), MXU_DTYPE), jax.ShapeDtypeStruct((rows, D_MODEL), F32),
                   jax.ShapeDtypeStruct((nc, SSM_GROUPS, GROUP_DIM, SSM_STATE), F32),
                   jax.ShapeDtypeStruct((rows, LANES), F32), jax.ShapeDtypeStruct((rows, LANES), F32)),
        in_specs=[_row_spec(CONV_DIM, CHUNK), _row_spec(LANES, CHUNK), pl.BlockSpec((CHUNK, D_MODEL), lambda i: (i, 4)),
                  _vec_spec(LANES), _vec_spec(LANES), _vec_spec(LANES), _vec_spec(D_MODEL),
                  pl.BlockSpec(memory_space=pl.ANY)],
        out_specs=(pl.BlockSpec((CHUNK, D_MODEL), lambda i: (i, 1)), _row_spec(D_MODEL, CHUNK),
                   pl.BlockSpec((1, SSM_GROUPS, GROUP_DIM, SSM_STATE), lambda i: (i, 0, 0, 0)),
                   _row_spec(LANES, CHUNK), _row_spec(LANES, CHUNK)),
        scratch_shapes=[pltpu.VMEM((SSM_GROUPS, GROUP_DIM, SSM_STATE), F32)],
        input_output_aliases={7: 0},
        compiler_params=_cparams(("arbitrary",)),
    )(pre, dtr, proj, dtb, alog, dsk, gn, cat)


def _ssd_bwd_grouped(pre, dtr, dt_saved, acum_saved, proj, y_saved, states, dcat, dtb, alog, dsk, gn, dproj, *, name):
    rows = pre.shape[0]
    nc = rows // CHUNK

    def rev(i):
        return nc - 1 - i

    def body(pre_ref, dtr_ref, dt_ref, acum_ref, z_ref, y_ref, st_ref, dyb_ref, dtb_ref, alog_ref, dsk_ref, gn_ref,
             dproj_ref, dpre_ref, dz_ref, ddtr_ref, dgn_ref, dvec_ref, g_scr):
        @pl.when(pl.program_id(0) == 0)
        def _():
            g_scr[...] = jnp.zeros_like(g_scr)
            dgn_ref[...] = jnp.zeros_like(dgn_ref)
            dvec_ref[...] = jnp.zeros_like(dvec_ref)

        row, lane, tril = _chunk_iotas()
        triu = lane >= row
        dt, acum = dt_ref[...], acum_ref[...]
        a = -jnp.exp(alog_ref[...])
        acum_t = acum.T
        eacum = jnp.exp(acum)
        a_end = acum[CHUNK - 1:CHUNK, :]
        e_end = jnp.exp(a_end)
        dte_all = jnp.exp(a_end - acum)
        dsk_v = dsk_ref[...]
        lane_q = lax.broadcasted_iota(jnp.int32, (CHUNK, GROUP_DIM), 1) // HEAD_DIM
        row_q = lax.broadcasted_iota(jnp.int32, (GROUP_DIM, SSM_STATE), 0) // HEAD_DIM
        zero = jnp.zeros((CHUNK, LANES), F32)
        dacum_c, dacum_r, ddt_c = zero, zero, zero
        d_aend = jnp.zeros((1, LANES), F32)
        d_dsk = jnp.zeros((1, LANES), F32)
        lane1 = lane[0:1, :]

        def cols(mat, g):
            return _by_quarter(lane_q, [mat[:, 4 * g + q:4 * g + q + 1] for q in range(HEADS_PER_GROUP)])

        for g in range(SSM_GROUPS):
            cs = slice(GROUP_DIM * g, GROUP_DIM * (g + 1))
            yv = y_ref[:, cs]
            sz, sz_grad = _silu_and_grad(z_ref[:, cs])
            gated = yv * sz
            dyb = dyb_ref[:, cs]
            dgh = dyb * gn_ref[:, cs]
            r = lax.rsqrt(jnp.mean(gated * gated, axis=1, keepdims=True) + EPS)
            dgn_ref[:, cs] += _colsum(dyb * gated * r)
            dgated = r * dgh - gated * (r * r * r * jnp.mean(dgh * gated, axis=1, keepdims=True))
            dy = dgated * sz
            dz_ref[:, cs] = (dgated * yv * sz_grad).astype(MXU_DTYPE)

            b_f, b_grad = _silu_and_grad(pre_ref[:, 1024 + SSM_STATE * g:1024 + SSM_STATE * (g + 1)])
            c_f, c_grad = _silu_and_grad(pre_ref[:, 1536 + SSM_STATE * g:1536 + SSM_STATE * (g + 1)])
            b_g, c_g = b_f.astype(MXU_DTYPE), c_f.astype(MXU_DTYPE)
            xs, xs_grad = _silu_and_grad(pre_ref[:, cs])
            dtq = cols(dt, g)
            xdt = xs * dtq
            xdt_m = xdt.astype(MXU_DTYPE)
            dy_m = dy.astype(MXU_DTYPE)
            s_prev = st_ref[0, g]
            g_next = g_scr[g]
            eacq, dteq = cols(eacum, g), cols(dte_all, g)
            t_off = dy * (_dot_nt(c_g, s_prev) * eacq)
            dye = dy * eacq
            dc_g = _dot(dye, s_prev)
            bg = _dot_nt(b_g, g_next)
            xw = xdt * dteq
            db_g = _dot(xw, g_next)
            t_w = xw * bg
            gs = g_next * s_prev
            e_rows = _by_quarter(row_q, [e_end[:, 4 * g + q:4 * g + q + 1] for q in range(HEADS_PER_GROUP)])
            g_scr[g] = e_rows * g_next + _dot(dye.T, c_g)
            cb = _dot_nt(c_g, b_g)
            cb_t = cb.T
            heads = range(4 * g, 4 * g + 4)
            decs = [jnp.exp(jnp.where(tril, acum[:, h:h + 1] - acum_t[h:h + 1, :], -jnp.inf)) for h in heads]
            mt_stack = jnp.concatenate(
                [(cb_t * jnp.exp(jnp.where(triu, acum_t[h:h + 1, :] - acum[:, h:h + 1], -jnp.inf))).astype(MXU_DTYPE)
                 for h in heads], axis=0)
            dy_stack = jnp.concatenate([jnp.where(lane_q == q, dy, 0.0).astype(MXU_DTYPE)
                                        for q in range(HEADS_PER_GROUP)], axis=0)
            dm_all = _dot_nt(dy_stack, xdt_m)
            dx_all = _dot(mt_stack, dy_m)
            dxdt = bg * dteq + _by_quarter(lane_q, [dx_all[CHUNK * q:CHUNK * (q + 1)] for q in range(HEADS_PER_GROUP)])
            dcb = zero
            t_dt = dxdt * xs
            t_dk = dy * xs
            for q, h in enumerate(heads):
                qs = slice(HEAD_DIM * q, HEAD_DIM * (q + 1))
                dm = dm_all[CHUNK * q:CHUNK * (q + 1)]
                e_h = dm * (cb * decs[q])
                dcb = dcb + dm * decs[q]
                w_col = jnp.sum(t_w[:, qs], axis=1, keepdims=True)
                col = jnp.sum(e_h, axis=1, keepdims=True) + jnp.sum(t_off[:, qs], axis=1, keepdims=True) - w_col
                dacum_c = dacum_c + jnp.where(lane == h, col, 0.0)
                dacum_r = dacum_r + jnp.where(row == h, _colsum(e_h), 0.0)
                d_end_h = jnp.sum(w_col, keepdims=True) + e_end[:, h:h + 1] * jnp.sum(gs[qs, :], keepdims=True)
                d_aend = d_aend + jnp.where(lane1 == h, d_end_h, 0.0)
                ddt_c = ddt_c + jnp.where(lane == h, jnp.sum(t_dt[:, qs], axis=1, keepdims=True), 0.0)
                d_dsk = d_dsk + jnp.where(lane1 == h, jnp.sum(t_dk[:, qs], keepdims=True), 0.0)
            dpre_ref[:, cs] = (dxdt * dtq + cols(dsk_v, g) * dy) * xs_grad
            dc_g = dc_g + _dot(dcb, b_g)
            db_g = db_g + _dot(dcb.T, c_g)
            dpre_ref[:, 1024 + SSM_STATE * g:1024 + SSM_STATE * (g + 1)] = db_g * b_grad
            dpre_ref[:, 1536 + SSM_STATE * g:1536 + SSM_STATE * (g + 1)] = dc_g * c_grad

        dacum = dacum_c - dacum_r.T + jnp.where(row == CHUNK - 1, d_aend, 0.0)
        dda = _dot_exact(triu.astype(F32), dacum)
        ddtr = (dda * a + ddt_c) * _sigmoid(dtr_ref[...] + dtb_ref[...])
        ddtr_ref[...] = ddtr.astype(MXU_DTYPE)
        dvec_ref[0:1, :] += _colsum(ddtr)
        dvec_ref[1:2, :] += _colsum(dda * dt)
        dvec_ref[2:3, :] += d_dsk

    return pl.pallas_call(
        body, name=name, grid=(nc,),
        out_shape=(jax.ShapeDtypeStruct((rows, CONV_DIM), F32), jax.ShapeDtypeStruct((rows, IN_MAIN), MXU_DTYPE),
                   jax.ShapeDtypeStruct((rows, LANES), MXU_DTYPE), jax.ShapeDtypeStruct((1, D_MODEL), F32),
                   jax.ShapeDtypeStruct((8, LANES), F32)),
        in_specs=[pl.BlockSpec((CHUNK, CONV_DIM), lambda i: (rev(i), 0)), pl.BlockSpec((CHUNK, LANES), lambda i: (rev(i), 0)),
                  pl.BlockSpec((CHUNK, LANES), lambda i: (rev(i), 0)), pl.BlockSpec((CHUNK, LANES), lambda i: (rev(i), 0)),
                  pl.BlockSpec((CHUNK, D_MODEL), lambda i: (rev(i), 4)), pl.BlockSpec((CHUNK, D_MODEL), lambda i: (rev(i), 0)),
                  pl.BlockSpec((1, SSM_GROUPS, GROUP_DIM, SSM_STATE), lambda i: (rev(i), 0, 0, 0)),
                  pl.BlockSpec((CHUNK, D_MODEL), lambda i: (rev(i), 1)),
                  _vec_spec(LANES), _vec_spec(LANES), _vec_spec(LANES), _vec_spec(D_MODEL),
                  pl.BlockSpec(memory_space=pl.ANY)],
        out_specs=(pl.BlockSpec((CHUNK, CONV_DIM), lambda i: (rev(i), 0)), pl.BlockSpec((CHUNK, D_MODEL), lambda i: (rev(i), 4)),
                   pl.BlockSpec((CHUNK, LANES), lambda i: (rev(i), 0)), _vec_spec(D_MODEL), _vec_spec(LANES, 8)),
        scratch_shapes=[pltpu.VMEM((SSM_GROUPS, GROUP_DIM, SSM_STATE), F32)],
        input_output_aliases={12: 1},
        compiler_params=_cparams(("arbitrary",)),
    )(pre, dtr, dt_saved, acum_saved, proj, y_saved, states, dcat, dtb, alog, dsk, gn, dproj)


def _pool_counts(first_row, n_rows, win):
    t = first_row + lax.broadcasted_iota(jnp.int32, (n_rows, POOL_DIM), 0)
    return jnp.minimum(t + 1, win).astype(F32)


def _pool_fwd(yn, pool_w, pool_b, pool_scale, *, name):
    rows = yn.shape[0]
    tr = ROW_TILE
    hb = tr // POOL_HALO

    def body(y_ref, prev_ref, w_ref, b_ref, s_ref, pm_ref, diff_ref, buf):
        i = pl.program_id(0)
        buf[pl.ds(0, POOL_HALO), :] = jnp.where(i == 0, 0.0, prev_ref[...])
        buf[pl.ds(POOL_HALO, tr), :] = y_ref[...]
        for g, win in enumerate(POOL_WINDOWS):
            cs = slice(POOL_DIM * g, POOL_DIM * (g + 1))
            acc = buf[pl.ds(POOL_HALO, tr), cs]
            for s in range(1, win):
                acc = acc + buf[pl.ds(POOL_HALO - s, tr), cs]
            diff = (acc / _pool_counts(i * tr, tr, win) - y_ref[:, cs]).astype(MXU_DTYPE)
            diff_ref[:, cs] = diff
            pm_ref[:, cs] = (_dot(diff, w_ref[g]) + b_ref[:, cs]) * s_ref[:, cs]

    return pl.pallas_call(
        body, name=name, grid=(rows // tr,),
        out_shape=(jax.ShapeDtypeStruct((rows, D_MODEL), F32), jax.ShapeDtypeStruct((rows, D_MODEL), MXU_DTYPE)),
        in_specs=[_row_spec(D_MODEL, tr),
                  pl.BlockSpec((POOL_HALO, D_MODEL), lambda i: (jnp.maximum(i * hb - 1, 0), 0)),
                  pl.BlockSpec((4, POOL_DIM, POOL_DIM), lambda i: (0, 0, 0)), _vec_spec(D_MODEL), _vec_spec(D_MODEL)],
        out_specs=(_row_spec(D_MODEL, tr), _row_spec(D_MODEL, tr)),
        scratch_shapes=[pltpu.VMEM((tr + POOL_HALO, D_MODEL), F32)],
        compiler_params=_cparams(("parallel",)),
    )(yn, yn, pool_w, pool_b, pool_scale)


def _pool_bwd(dpm, diff, pool_w, pool_w_t, pool_b, pool_scale, *, name):
    rows = dpm.shape[0]
    tr = ROW_TILE
    hb = tr // POOL_HALO
    nblk = rows // tr

    def body(d_ref, dnext_ref, diff_ref, w_ref, wt_ref, b_ref, s_ref, dy_ref, dw_ref, db_ref, ds_ref, ebuf):
        i = pl.program_id(0)

        @pl.when(i == 0)
        def _():
            dw_ref[...] = jnp.zeros_like(dw_ref)
            db_ref[...] = jnp.zeros_like(db_ref)
            ds_ref[...] = jnp.zeros_like(ds_ref)

        last = i == nblk - 1
        for g, win in enumerate(POOL_WINDOWS):
            cs = slice(POOL_DIM * g, POOL_DIM * (g + 1))
            d = d_ref[:, cs]
            diff = diff_ref[:, cs]
            out_pre = _dot(diff, w_ref[g]) + b_ref[:, cs]
            ds_ref[:, cs] += _colsum(d * out_pre)
            dout = d * s_ref[:, cs]
            db_ref[:, cs] += _colsum(dout)
            dw_ref[g] += _dot_tn(diff, dout)
            ddiff = _dot(dout, wt_ref[g])
            ddiff_next = _dot(jnp.where(last, 0.0, dnext_ref[:, cs]) * s_ref[:, cs], wt_ref[g])
            ebuf[pl.ds(0, tr), cs] = ddiff / _pool_counts(i * tr, tr, win)
            ebuf[pl.ds(tr, POOL_HALO), cs] = ddiff_next / _pool_counts((i + 1) * tr, POOL_HALO, win)
            acc = -ddiff
            for s in range(win):
                acc = acc + ebuf[pl.ds(s, tr), cs]
            dy_ref[:, cs] = acc

    return pl.pallas_call(
        body, name=name, grid=(nblk,),
        out_shape=(jax.ShapeDtypeStruct((rows, D_MODEL), F32), jax.ShapeDtypeStruct((4, POOL_DIM, POOL_DIM), F32),
                   jax.ShapeDtypeStruct((1, D_MODEL), F32), jax.ShapeDtypeStruct((1, D_MODEL), F32)),
        in_specs=[_row_spec(D_MODEL, tr),
                  pl.BlockSpec((POOL_HALO, D_MODEL), lambda i: (jnp.minimum((i + 1) * hb, rows // POOL_HALO - 1), 0)),
                  _row_spec(D_MODEL, tr),
                  pl.BlockSpec((4, POOL_DIM, POOL_DIM), lambda i: (0, 0, 0)),
                  pl.BlockSpec((4, POOL_DIM, POOL_DIM), lambda i: (0, 0, 0)), _vec_spec(D_MODEL), _vec_spec(D_MODEL)],
        out_specs=(_row_spec(D_MODEL, tr), pl.BlockSpec((4, POOL_DIM, POOL_DIM), lambda i: (0, 0, 0)),
                   _vec_spec(D_MODEL), _vec_spec(D_MODEL)),
        scratch_shapes=[pltpu.VMEM((tr + POOL_HALO, D_MODEL), F32)],
        compiler_params=_cparams(("arbitrary",)),
    )(dpm, dpm, diff, pool_w, pool_w_t, pool_b, pool_scale)


def _row_tile(rows, cap, step):
    best = rows
    for t in range(step, min(rows, cap) + 1, step):
        if rows % t == 0:
            best = t
    return best if best <= cap else rows


def _sum8(recv, *, name):
    _, r, c = recv.shape
    step = 8 if recv.dtype == F32 else 16

    def body(r_ref, g_ref):
        g = r_ref[0].astype(F32)
        for j in range(1, N_DEV):
            g = g + r_ref[j].astype(F32)
        g_ref[...] = g

    if r % step == 0:
        tr = _row_tile(r, 256, step)
        grid, in_spec, out_spec = (r // tr,), pl.BlockSpec((N_DEV, tr, c), lambda i: (0, i, 0)), pl.BlockSpec((tr, c), lambda i: (i, 0))
    else:
        tc = 256
        grid, in_spec, out_spec = (c // tc,), pl.BlockSpec((N_DEV, r, tc), lambda i: (0, 0, i)), pl.BlockSpec((r, tc), lambda i: (0, i))
    return pl.pallas_call(
        body, name=name, grid=grid, out_shape=jax.ShapeDtypeStruct((r, c), F32),
        in_specs=[in_spec], out_specs=out_spec, compiler_params=_cparams(("parallel",)),
    )(recv)


def _adamw(g, w, m, v, *, name):
    rows, cols = w.shape
    tr = _row_tile(rows, max(8, (256 * 1024) // cols // 8 * 8), 8)
    c1 = 1.0 / (1.0 - ADAM_B1 ** ADAM_STEP)
    c2 = 1.0 / (1.0 - ADAM_B2 ** ADAM_STEP)

    def body(g_ref, w_ref, m_ref, v_ref, d_ref, mo_ref, vo_ref):
        g = g_ref[...]
        m_new = ADAM_B1 * m_ref[...] + (1.0 - ADAM_B1) * g
        v_new = ADAM_B2 * v_ref[...] + (1.0 - ADAM_B2) * (g * g)
        mo_ref[...] = m_new
        vo_ref[...] = v_new
        d_ref[...] = -ADAM_LR * ((m_new * c1) / (jnp.sqrt(v_new * c2) + ADAM_EPS) + ADAM_WD * w_ref[...])

    spec = pl.BlockSpec((tr, cols), lambda i: (i, 0))
    return pl.pallas_call(
        body, name=name, grid=(rows // tr,),
        out_shape=tuple(jax.ShapeDtypeStruct((rows, cols), F32) for _ in range(3)),
        in_specs=[spec] * 4, out_specs=(spec, spec, spec),
        compiler_params=_cparams(("parallel",)),
    )(g, w, m, v)


def _pad_rows(flat, mult):
    n = flat.shape[-1]
    pad = (-n) % mult
    if pad:
        flat = jnp.pad(flat, [(0, 0)] * (flat.ndim - 1) + [(0, pad)])
    return flat


def _pack_blocks(blocks, row_mult):
    flat = jnp.concatenate([_pad_rows(b.reshape(-1), LANES) for b in blocks])
    return _pad_rows(flat, LANES * row_mult).reshape(-1, LANES)


def _block_sizes(blocks):
    return [-(-math.prod(b.shape) // LANES) * LANES for b in blocks]


def _unpack_blocks(slab, like, lead=()):
    flat = slab.reshape(lead + (-1,))
    out, off = [], 0
    for b, size in zip(like, _block_sizes(like)):
        n = math.prod(b.shape)
        out.append(flat[..., off:off + n].reshape(lead + tuple(b.shape)))
        off += size
    return out


def _join_shards(gathered, axis):
    return jnp.concatenate([gathered[j] for j in range(N_DEV)], axis=axis)


def _split_shards(full, axis):
    return jnp.stack(jnp.split(full, N_DEV, axis=axis))


def _interleave_ff(w_gate, w_up):
    k = w_gate.shape[0]
    nt = D_FF // FF_TILE
    return jnp.stack([w_gate.reshape(k, nt, FF_TILE), w_up.reshape(k, nt, FF_TILE)], axis=2).reshape(k, 2 * D_FF)


def _row128(vec):
    return jnp.pad(vec.reshape(1, -1), ((0, 0), (0, LANES - vec.shape[-1])))


def kernel(x, norm_g, w_in, gm_ln_g, gm_ln_b, gm_ws, gm_bs, conv_w, conv_b, dt_bias, a_log, d_skip, ssm_norm_g, w_out, pool_w, pool_b, pool_scale, ffn_w_gate, ffn_w_up, ffn_w_down, loss_target, m_norm_g, m_w_in, m_gm_ln_g, m_gm_ln_b, m_gm_ws, m_gm_bs, m_conv_w, m_conv_b, m_dt_bias, m_a_log, m_d_skip, m_ssm_norm_g, m_w_out, m_pool_w, m_pool_b, m_pool_scale, m_ffn_w_gate, m_ffn_w_up, m_ffn_w_down, v_norm_g, v_w_in, v_gm_ln_g, v_gm_ln_b, v_gm_ws, v_gm_bs, v_conv_w, v_conv_b, v_dt_bias, v_a_log, v_d_skip, v_ssm_norm_g, v_w_out, v_pool_w, v_pool_b, v_pool_scale, v_ffn_w_gate, v_ffn_w_up, v_ffn_w_down):
    w_loc = dict(norm_g=norm_g, w_in=w_in, gm_ln_g=gm_ln_g, gm_ln_b=gm_ln_b, gm_ws=gm_ws, gm_bs=gm_bs, conv_w=conv_w,
                 conv_b=conv_b, dt_bias=dt_bias, a_log=a_log, d_skip=d_skip, ssm_norm_g=ssm_norm_g, w_out=w_out,
                 pool_w=pool_w, pool_b=pool_b, pool_scale=pool_scale, ffn_w_gate=ffn_w_gate, ffn_w_up=ffn_w_up,
                 ffn_w_down=ffn_w_down)
    m_loc = dict(zip(WEIGHTS, [m_norm_g, m_w_in, m_gm_ln_g, m_gm_ln_b, m_gm_ws, m_gm_bs, m_conv_w, m_conv_b, m_dt_bias,
                               m_a_log, m_d_skip, m_ssm_norm_g, m_w_out, m_pool_w, m_pool_b, m_pool_scale,
                               m_ffn_w_gate, m_ffn_w_up, m_ffn_w_down]))
    v_loc = dict(zip(WEIGHTS, [v_norm_g, v_w_in, v_gm_ln_g, v_gm_ln_b, v_gm_ws, v_gm_bs, v_conv_w, v_conv_b, v_dt_bias,
                               v_a_log, v_d_skip, v_ssm_norm_g, v_w_out, v_pool_w, v_pool_b, v_pool_scale,
                               v_ffn_w_gate, v_ffn_w_up, v_ffn_w_down]))

    small_blocks = [w_loc[n] for n in GATHER_F32]
    got = _gather_two_level([w_in[0].astype(MXU_DTYPE), _pack_blocks(small_blocks, 8)], name="gather_first")
    full = {n: w_loc[n] for n in WEIGHTS if SHARD_AXIS[n] is None}
    full['w_in'] = got[0].transpose(1, 0, 2).reshape(1, D_MODEL, -1)
    for n, g in zip(GATHER_F32, _unpack_blocks(got[1], small_blocks, (N_DEV,))):
        full[n] = _join_shards(g, SHARD_AXIS[n])
    shards = {n: w_loc[n].astype(MXU_DTYPE) for n in ('w_out', 'ffn_w_gate', 'ffn_w_up', 'ffn_w_down', 'pool_w')}

    loss_part, grad_x, grads, recv = _local_step(x[0], loss_target[0], full, shards)

    small = [n for n in WEIGHTS if n not in BIG_WEIGHTS]
    like = [w_loc[n] for n in small]
    slots = []
    for n in small:
        ax = SHARD_AXIS[n]
        g = grads[n].astype(F32)
        sh = _split_shards(g, ax) if ax is not None else jnp.broadcast_to(g[None], (N_DEV,) + g.shape)
        slots.append(_pad_rows(sh.reshape(N_DEV, -1), LANES))
    send_small = _pad_rows(jnp.concatenate(slots, axis=1), LANES * 8).reshape(N_DEV, -1, LANES)
    recv_small, = _exchange([send_small], ['slots'], name="exchange_last")

    g_small = _sum8(recv_small, name="sum_small")
    g_own = dict(zip(small, _unpack_blocks(g_small, like)))
    g_own['w_in'] = _sum8(recv['w_in'], name="sum_w_in").T[None]
    g_own['w_out'] = _sum8(recv['w_out'], name="sum_w_out")[None]
    g_own['ffn_w_gate'] = jnp.stack([_sum8(recv['ffn_w_gate'][l], name=f"sum_ffn{l}_gate").T for l in range(2)])
    g_own['ffn_w_up'] = jnp.stack([_sum8(recv['ffn_w_up'][l], name=f"sum_ffn{l}_up").T for l in range(2)])
    g_own['ffn_w_down'] = jnp.stack([_sum8(recv['ffn_w_down'][l], name=f"sum_ffn{l}_down") for l in range(2)])

    delta, m_new, v_new = {}, {}, {}
    pk = lambda d: _pack_blocks([d[n] for n in small], 8)
    d_s, m_s, v_s = _adamw(g_small, pk(w_loc), pk(m_loc), pk(v_loc), name="adamw_small")
    for dst, slab in ((delta, d_s), (m_new, m_s), (v_new, v_s)):
        dst.update(zip(small, _unpack_blocks(slab, like)))
    for n in BIG_WEIGHTS:
        shape = w_loc[n].shape
        two_d = lambda t: t.reshape(-1, shape[-1])
        res = _adamw(two_d(g_own[n]), two_d(w_loc[n]), two_d(m_loc[n]), two_d(v_loc[n]), name=f"adamw_{n}")
        delta[n], m_new[n], v_new[n] = (t.reshape(shape) for t in res)

    loss = lax.psum(loss_part[0, 0], ("x", "y", "c"))
    outs = [d[n] for d in (g_own, delta, m_new, v_new) for n in WEIGHTS]
    return (loss, grad_x[None], *outs)


def _local_step(h0, tgt, full, shards):
    gm_ln_g, gm_ln_b, gm_ws, gm_bs = full['gm_ln_g'], full['gm_ln_b'], full['gm_ws'], full['gm_bs']
    conv_b, dt_bias, a_log, d_skip, ssm_norm_g = (full['conv_b'], full['dt_bias'], full['a_log'], full['d_skip'],
                                                  full['ssm_norm_g'])
    w_in_f = full['w_in'][0]
    w_main = jnp.concatenate([w_in_f[:, 3072:5120], w_in_f[:, :3072]], axis=1)
    w_dt = jnp.pad(w_in_f[:, 5120:], ((0, 0), (0, LANES - SSM_HEADS)))
    ng = full['norm_g']

    def ffn_shards(layer):
        return [shards['ffn_w_gate'][layer], shards['ffn_w_up'][layer], shards['ffn_w_down'][layer]]

    def ffn_weights(got_gate, got_up, got_down):
        cols = lambda g: g.transpose(1, 0, 2).reshape(D_MODEL, D_FF)
        return _interleave_ff(cols(got_gate), cols(got_up)), got_down.reshape(D_FF, D_MODEL)

    w_gu, w_dn = [None, None], [None, None]
    causal = jnp.tril(jnp.ones((CHUNK, CHUNK), bool))
    wm = jnp.where(causal[None], gm_ws[0], 0.0).astype(MXU_DTYPE)
    wm_t = jnp.swapaxes(wm, 1, 2)
    bcol = jnp.pad(gm_bs[0].T, ((0, 0), (0, LANES - GM_HEADS)))
    conv_w8 = jnp.pad(full['conv_w'][0], ((0, 8 - SSM_CONV), (0, 0)))
    dtb, alog, dsk = _row128(dt_bias[0]), _row128(a_log[0]), _row128(d_skip[0])
    pool_b_f = full['pool_b'][0].reshape(1, D_MODEL)
    pool_s_f = full['pool_scale']

    def g_(layer, i):
        return ng[layer, i].reshape(1, D_MODEL)

    yn0 = _rn_fwd(h0, g_(0, 0), name="rn_fwd_0", out_dtype=MXU_DTYPE)
    proj, got = _mm(yn0, w_main, name="mm_in_proj", tm=2048,
                    ex=_Exchange([shards['w_out'][0]] + ffn_shards(0), ['gather'] * 4))
    w_out_f = got[0].reshape(-1, D_MODEL)
    w_gu[0], w_dn[0] = ffn_weights(*got[1:])
    dtr = _mm(yn0, w_dt, name="mm_in_proj_dt")
    pre = _conv_fwd(proj, conv_w8, conv_b, name="conv_fwd")
    cat = _gmlp_fwd(proj, gm_ln_g, gm_ln_b, wm, bcol, name="gmlp_fwd")
    cat, y_ssd, states, dt_ssd, acum_ssd = _ssd_fwd_grouped(pre, dtr, proj, dtb, alog, dsk, ssm_norm_g, cat,
                                                            name="ssd_fwd")
    o0 = _mm(cat, w_out_f, name="mm_out_proj", tm=1024, tn=1024)
    h1, yn1 = _resid_rn_fwd(h0, o0, g_(0, 1), g_(0, 2), name="resid_fwd_0a", next_dtype=MXU_DTYPE)
    (gu0, act0), got = _mm_swiglu(yn1, w_gu[0], name="mm_ffn0_gate_up",
                                  ex=_Exchange(ffn_shards(1) + [shards['pool_w'][0]], ['gather'] * 4))
    w_gu[1], w_dn[1] = ffn_weights(*got[:3])
    pool_w_f = got[3].transpose(1, 0, 2, 3).reshape(4, POOL_DIM, POOL_DIM)
    d0 = _mm(act0, w_dn[0], name="mm_ffn0_down", tm=1024, tn=1024)
    h2, yn2 = _resid_rn_fwd(h1, d0, g_(0, 3), g_(1, 0), name="resid_fwd_0b", next_dtype=F32)
    pm, pdiff = _pool_fwd(yn2, pool_w_f, pool_b_f, pool_s_f, name="pool_fwd")
    h3, yn3 = _resid_rn_fwd(h2, pm, g_(1, 1), g_(1, 2), name="resid_fwd_1a", next_dtype=MXU_DTYPE)
    gu1, act1 = _mm_swiglu(yn3, w_gu[1], name="mm_ffn1_gate_up")
    d1 = _mm(act1, w_dn[1], name="mm_ffn1_down", tm=1024, tn=1024)
    dh4, loss_part = _resid_loss(h3, d1, g_(1, 3), tgt, name="resid_loss")

    grads = {}
    recv = {'ffn_w_gate': [None, None], 'ffn_w_up': [None, None], 'ffn_w_down': [None, None]}
    dng = [[None] * 4 for _ in range(2)]

    def ffn_bwd(layer, dh, d_out, gu, act, yn, h_in):
        dd, dng[layer][3] = _resid_bwd_post(dh, d_out, g_(layer, 3), name=f"resid_bwd_post_{layer}b", out_dtype=MXU_DTYPE)
        dw_dn = _mm_tn(act, dd, name=f"mm_ffn{layer}_dw_down", out_dtype=MXU_DTYPE, tm=1408, tn=1024)
        dgu = _mm_dswiglu(dd, w_dn[layer].T, gu, name=f"mm_ffn{layer}_dact")
        dw_g_t, dw_u_t = _mm_tn_gate_up(dgu, yn, name=f"mm_ffn{layer}_dw_gate_up", out_dtype=MXU_DTYPE)
        dyn, got = _mm(dgu, w_gu[layer].T, name=f"mm_ffn{layer}_dyn", tm=512, tn=1024,
                       ex=_Exchange([dw_g_t, dw_u_t, dw_dn], ['rows'] * 3))
        recv['ffn_w_gate'][layer], recv['ffn_w_up'][layer], recv['ffn_w_down'][layer] = got
        dh_in, dng[layer][2] = _resid_bwd_pre(dh, [dyn], h_in, g_(layer, 2), name=f"resid_bwd_pre_{layer}b")
        return dh_in

    dh3 = ffn_bwd(1, dh4, d1, gu1, act1, yn3, h3)
    dpm, dng[1][1] = _resid_bwd_post(dh3, pm, g_(1, 1), name="resid_bwd_post_1a", out_dtype=F32)
    dyn2, d_pool_w, d_pool_b, d_pool_s = _pool_bwd(dpm, pdiff, pool_w_f, jnp.swapaxes(pool_w_f, 1, 2), pool_b_f, pool_s_f,
                                                   name="pool_bwd")
    dh2, dng[1][0] = _resid_bwd_pre(dh3, [dyn2], h2, g_(1, 0), name="resid_bwd_pre_1a")
    dh1 = ffn_bwd(0, dh2, d0, gu0, act0, yn1, h1)
    do0, dng[0][1] = _resid_bwd_post(dh1, o0, g_(0, 1), name="resid_bwd_post_0a", out_dtype=MXU_DTYPE)
    d_w_out = _mm_tn(cat, do0, name="mm_out_proj_dw", out_dtype=MXU_DTYPE, tn=1024)
    dcat, got = _mm(do0, w_out_f.T, name="mm_out_proj_dx", tm=2048, tn=1024, ex=_Exchange([d_w_out], ['rows']))
    recv['w_out'] = got[0]
    dproj, d_wm, d_bcol, d_ln_g, d_ln_b = _gmlp_bwd(proj, dcat, gm_ln_g, gm_ln_b, wm, wm_t, bcol, name="gmlp_bwd")
    dpre, dproj, ddtr, d_gn, d_vec = _ssd_bwd_grouped(pre, dtr, dt_ssd, acum_ssd, proj, y_ssd, states, dcat, dtb, alog,
                                                      dsk, ssm_norm_g, dproj, name="ssd_bwd")
    dproj, d_conv_w8, d_conv_b = _conv_bwd(dpre, proj, conv_w8, dproj, name="conv_bwd")
    d_w_main_t = _mm_tn(dproj, yn0, name="mm_in_proj_dw", out_dtype=MXU_DTYPE, tn=1024, shift=3)
    d_w_dt_t = _mm_tn(ddtr, yn0, name="mm_in_proj_dt_dw", out_dtype=MXU_DTYPE, tn=1024)
    d_w_in_t = jnp.concatenate([d_w_main_t, d_w_dt_t[:SSM_HEADS]], axis=0).reshape(N_DEV, -1, D_MODEL)
    dyn0, got = _mm(dproj, w_main.T, name="mm_in_proj_dx", tm=512, tn=1024,
                    ex=_Exchange([d_w_in_t], ['slots']))
    recv['w_in'] = got[0]
    dyn0_dt = _mm(ddtr, w_dt.T, name="mm_in_proj_dt_dx")
    grad_x, dng[0][0] = _resid_bwd_pre(dh1, [dyn0, dyn0_dt], h0, g_(0, 0), name="resid_bwd_pre_0a")

    grads['norm_g'] = jnp.stack([jnp.concatenate(dng[l], axis=0) for l in range(2)])
    grads['gm_ln_g'], grads['gm_ln_b'] = d_ln_g, d_ln_b
    grads['gm_ws'] = d_wm[None]
    grads['gm_bs'] = d_bcol[:, :GM_HEADS].T[None]
    grads['conv_w'] = d_conv_w8[None, :SSM_CONV]
    grads['conv_b'] = d_conv_b
    grads['dt_bias'] = d_vec[0:1, :SSM_HEADS]
    grads['a_log'] = d_vec[1:2, :SSM_HEADS] * (-jnp.exp(a_log))
    grads['d_skip'] = d_vec[2:3, :SSM_HEADS]
    grads['ssm_norm_g'] = d_gn
    grads['pool_w'] = d_pool_w[None]
    grads['pool_b'] = d_pool_b.reshape(1, 4, POOL_DIM)
    grads['pool_scale'] = d_pool_s
    return loss_part, grad_x, grads, recv
```

```python
import functools
import math

import jax
import jax.numpy as jnp
from jax import lax
from jax.experimental import pallas as pl
from jax.experimental.pallas import tpu as pltpu

F32 = jnp.float32
MXU_DTYPE = jnp.bfloat16

N_DEV = 8
D_MODEL = 1024
EPS = 1e-6
GM_HEADS = 4
GM_HEAD_DIM = 256
CHUNK = 128
SSM_HEADS = 16
SSM_GROUPS = 4
SSM_STATE = 128
SSM_CONV = 4
CONV_DIM = 2048
POOL_WINDOWS = (2, 4, 8, 16)
POOL_DIM = 256
D_FF = 2816
FF_TILE = 256
IN_MAIN = 5120
LANES = 128
CONV_HALO = 8
POOL_HALO = 16
ADAM_LR, ADAM_B1, ADAM_B2, ADAM_EPS, ADAM_WD, ADAM_STEP = 0.001, 0.9, 0.999, 1e-08, 0.01, 10

VMEM_LIMIT = 56 * 1024 * 1024
ROW_TILE = 512
MM_TM = 2048

WEIGHTS = ['norm_g', 'w_in', 'gm_ln_g', 'gm_ln_b', 'gm_ws', 'gm_bs', 'conv_w', 'conv_b', 'dt_bias', 'a_log',
           'd_skip', 'ssm_norm_g', 'w_out', 'pool_w', 'pool_b', 'pool_scale', 'ffn_w_gate', 'ffn_w_up', 'ffn_w_down']
SHARD_AXIS = {'norm_g': 2, 'w_in': 2, 'gm_ln_g': None, 'gm_ln_b': None, 'gm_ws': None, 'gm_bs': None, 'conv_w': 2,
              'conv_b': None, 'dt_bias': None, 'a_log': None, 'd_skip': None, 'ssm_norm_g': None, 'w_out': 1,
              'pool_w': 2, 'pool_b': 2, 'pool_scale': 1, 'ffn_w_gate': 2, 'ffn_w_up': 2, 'ffn_w_down': 1}
GATHER_BF16 = ['w_in', 'w_out', 'pool_w', 'ffn_w_gate', 'ffn_w_up', 'ffn_w_down']
GATHER_F32 = ['norm_g', 'conv_w', 'pool_b', 'pool_scale']
BIG_WEIGHTS = ['w_in', 'w_out', 'ffn_w_gate', 'ffn_w_up', 'ffn_w_down']


def _cparams(sem=None):
    return pltpu.CompilerParams(dimension_semantics=sem, vmem_limit_bytes=VMEM_LIMIT)


def _dot(a, b):
    return jnp.dot(a.astype(MXU_DTYPE), b.astype(MXU_DTYPE), preferred_element_type=F32)


def _dot_nt(a, b):
    return lax.dot_general(a.astype(MXU_DTYPE), b.astype(MXU_DTYPE), (((1,), (1,)), ((), ())),
                           preferred_element_type=F32)


def _dot_tn(a, b):
    return lax.dot_general(a.astype(MXU_DTYPE), b.astype(MXU_DTYPE), (((0,), (0,)), ((), ())),
                           preferred_element_type=F32)


def _dot_exact(a, b):
    return jnp.dot(a, b, precision=lax.Precision.HIGHEST, preferred_element_type=F32)


def _sigmoid(x):
    return 1.0 / (1.0 + jnp.exp(-x))


def _silu(x):
    return x * _sigmoid(x)


def _silu_grad(x):
    s = _sigmoid(x)
    return s * (1.0 + x * (1.0 - s))


_GELU_C = math.sqrt(2.0 / math.pi)


def _gelu(x):
    return 0.5 * x * (1.0 + jnp.tanh(_GELU_C * (x + 0.044715 * x * x * x)))


def _gelu_grad(x):
    t = jnp.tanh(_GELU_C * (x + 0.044715 * x * x * x))
    return 0.5 * (1.0 + t) + 0.5 * x * (1.0 - t * t) * _GELU_C * (1.0 + 3.0 * 0.044715 * x * x)


def _softplus(x):
    return jnp.maximum(x, 0.0) + jnp.log1p(jnp.exp(-jnp.abs(x)))


def _rms_scale(x):
    return lax.rsqrt(jnp.mean(x * x, axis=-1, keepdims=True) + EPS)


def _rms_bwd(dy, x, g):
    r = _rms_scale(x)
    xn = x * r
    dxn = dy * g
    dx = r * (dxn - xn * jnp.mean(dxn * xn, axis=-1, keepdims=True))
    return dx, dy * xn


def _colsum(x):
    return jnp.sum(x, axis=0, keepdims=True)


class _Exchange:
    def __init__(self, arrays, modes):
        self.arrays, self.modes, self.n = list(arrays), list(modes), len(arrays)
        self.blks = []
        for x, mode in zip(arrays, modes):
            if mode == 'gather':
                self.blks.append(tuple(x.shape))
            elif mode == 'slots':
                self.blks.append(tuple(x.shape[1:]))
            else:
                self.blks.append((x.shape[0] // N_DEV,) + tuple(x.shape[1:]))
        self.out_shape = [jax.ShapeDtypeStruct((N_DEV,) + blk, x.dtype) for x, blk in zip(arrays, self.blks)]
        self.in_specs = [pl.BlockSpec(memory_space=pl.ANY)] * self.n
        self.out_specs = [pl.BlockSpec(memory_space=pl.ANY) for _ in range(self.n)]
        n_sem = self.n * (N_DEV - 1)
        self.scratch = [pltpu.SemaphoreType.DMA((n_sem,)), pltpu.SemaphoreType.DMA((n_sem,)),
                        pltpu.SemaphoreType.DMA((self.n,))]

    def _copies(self, x_refs, out_refs, send_sems, recv_sems, local_sems, with_recvs):
        mx, my, mc = lax.axis_index("x"), lax.axis_index("y"), lax.axis_index("c")
        me = 4 * mx + 2 * my + mc

        def flip(v, bit):
            return 1 - v if bit else v

        def part(a, dev):
            if self.modes[a] == 'gather':
                return x_refs[a]
            if self.modes[a] == 'slots':
                return x_refs[a].at[dev]
            r = self.blks[a][0]
            return x_refs[a].at[pl.ds(pl.multiple_of(dev * r, 16), r)]

        sends, recvs, owns = [], [], []
        for k in (1, 2, 4, 6, 3, 5, 7):
            px, py, pc = flip(mx, (k >> 2) & 1), flip(my, (k >> 1) & 1), flip(mc, k & 1)
            peer = 4 * px + 2 * py + pc
            for a in range(self.n):
                sem = a * (N_DEV - 1) + k - 1
                sends.append(pltpu.make_async_remote_copy(
                    src_ref=part(a, peer), dst_ref=out_refs[a].at[me], send_sem=send_sems.at[sem],
                    recv_sem=recv_sems.at[sem], device_id=(px, py, pc), device_id_type=pl.DeviceIdType.MESH))
                if with_recvs:
                    recvs.append(pltpu.make_async_remote_copy(
                        src_ref=part(a, peer), dst_ref=out_refs[a].at[peer], send_sem=send_sems.at[sem],
                        recv_sem=recv_sems.at[sem], device_id=(px, py, pc), device_id_type=pl.DeviceIdType.MESH))
        for a in range(self.n):
            owns.append(pltpu.make_async_copy(part(a, me), out_refs[a].at[me], local_sems.at[a]))
        return sends, recvs, owns

    def start(self, *refs):
        sends, _, owns = self._copies(*refs, with_recvs=False)
        for cp in sends + owns:
            cp.start()

    def wait(self, *refs):
        sends, recvs, owns = self._copies(*refs, with_recvs=True)
        for cp in recvs:
            cp.wait_recv()
        for cp in sends:
            cp.wait_send()
        for cp in owns:
            cp.wait()


def _exchange(arrays, modes, *, name):
    ex = _Exchange(arrays, modes)

    def body(*refs):
        x_refs, out_refs, sems = refs[:ex.n], refs[ex.n:2 * ex.n], refs[2 * ex.n:]
        ex.start(x_refs, out_refs, *sems)
        ex.wait(x_refs, out_refs, *sems)

    return pl.pallas_call(
        body, name=name, out_shape=tuple(ex.out_shape), in_specs=ex.in_specs, out_specs=tuple(ex.out_specs),
        scratch_shapes=ex.scratch,
    )(*arrays)


def _gather_two_level(arrays, *, name):
    n = len(arrays)
    per = N_DEV - 1

    def body(*refs):
        x_refs, out_refs = refs[:n], refs[n:2 * n]
        send_sems, recv_sems, local_sems = refs[2 * n:]
        x, y, c = lax.axis_index("x"), lax.axis_index("y"), lax.axis_index("c")
        me, sibling = (x, y, c), (x, y, 1 - c)
        chips = [(1 - x, y), (x, 1 - y), (1 - x, 1 - y)]

        def copy(a, k, block, to, src=None):
            slot = out_refs[a].at[4 * block[0] + 2 * block[1] + block[2]]
            return pltpu.make_async_remote_copy(
                src_ref=slot if src is None else src, dst_ref=slot, send_sem=send_sems.at[a * per + k],
                recv_sem=recv_sems.at[a * per + k], device_id=to, device_id_type=pl.DeviceIdType.MESH)

        mines = [pltpu.make_async_copy(x_refs[a], out_refs[a].at[4 * x + 2 * y + c], local_sems.at[a]) for a in range(n)]
        firsts = []
        for a in range(n):
            firsts.append(copy(a, 0, me, sibling, src=x_refs[a]))
            firsts += [copy(a, 1 + j, me, (*chip, c), src=x_refs[a]) for j, chip in enumerate(chips)]
        for cp in mines + firsts:
            cp.start()
        passed = []
        for j, chip in enumerate(chips):
            for a in range(n):
                copy(a, 1 + j, (*chip, c), me).wait_recv()
                passed.append(copy(a, 4 + j, (*chip, c), sibling))
                passed[-1].start()
        for a in range(n):
            copy(a, 0, sibling, me).wait_recv()
            for j, chip in enumerate(chips):
                copy(a, 4 + j, (*chip, 1 - c), me).wait_recv()
        for cp in firsts + passed:
            cp.wait_send()
        for cp in mines:
            cp.wait()

    return pl.pallas_call(
        body, name=name,
        out_shape=tuple(jax.ShapeDtypeStruct((N_DEV,) + tuple(a.shape), a.dtype) for a in arrays),
        in_specs=[pl.BlockSpec(memory_space=pl.ANY)] * n,
        out_specs=tuple(pl.BlockSpec(memory_space=pl.ANY) for _ in range(n)),
        scratch_shapes=[pltpu.SemaphoreType.DMA((n * per,)), pltpu.SemaphoreType.DMA((n * per,)),
                        pltpu.SemaphoreType.DMA((n,))],
    )(*arrays)


def _hosted(body, n_in, n_out, n_scratch, grid, ex):
    def wrapped(*refs):
        ins, x_refs = refs[:n_in], refs[n_in:n_in + ex.n]
        outs = refs[n_in + ex.n:n_in + ex.n + n_out]
        xo_refs = refs[n_in + ex.n + n_out:n_in + 2 * ex.n + n_out]
        scr = refs[n_in + 2 * ex.n + n_out:n_in + 2 * ex.n + n_out + n_scratch]
        sems = refs[n_in + 2 * ex.n + n_out + n_scratch:]
        ids = [pl.program_id(d) for d in range(len(grid))]
        first = functools.reduce(jnp.logical_and, [i == 0 for i in ids])
        last = functools.reduce(jnp.logical_and, [i == g - 1 for i, g in zip(ids, grid)])

        @pl.when(first)
        def _():
            ex.start(x_refs, xo_refs, *sems)

        body(*ins, *outs, *scr)

        @pl.when(last)
        def _():
            ex.wait(x_refs, xo_refs, *sems)

    return wrapped


def _call(body, *, name, grid, inputs, in_specs, out_shape, out_specs, scratch, semantics, ex=None):
    if ex is None:
        return pl.pallas_call(
            body, name=name, grid=grid, out_shape=tuple(out_shape), in_specs=list(in_specs),
            out_specs=tuple(out_specs), scratch_shapes=list(scratch), compiler_params=_cparams(semantics))(*inputs)
    n_out = len(out_shape)
    res = pl.pallas_call(
        _hosted(body, len(inputs), n_out, len(scratch), grid, ex), name=name, grid=grid,
        out_shape=tuple(out_shape) + tuple(ex.out_shape), in_specs=list(in_specs) + ex.in_specs,
        out_specs=tuple(out_specs) + tuple(ex.out_specs), scratch_shapes=list(scratch) + ex.scratch,
        compiler_params=_cparams(("arbitrary",) * len(grid)))(*inputs, *ex.arrays)
    return res[:n_out], res[n_out:]


def _mm(a, b, *, name, out_dtype=F32, tm=MM_TM, tn=512, tk=None, ex=None):
    m, k = a.shape
    n = b.shape[1]
    tm, tn = min(tm, m), min(tn, n)
    tk = k if tk is None else tk
    nk = k // tk
    assert m % tm == 0 and n % tn == 0 and k % tk == 0

    def body(a_ref, b_ref, o_ref, acc_ref):
        kk = pl.program_id(2)
        part = _dot(a_ref[...], b_ref[...])
        if nk == 1:
            o_ref[...] = part.astype(out_dtype)
        else:
            @pl.when(kk == 0)
            def _():
                acc_ref[...] = part

            @pl.when(kk > 0)
            def _():
                acc_ref[...] += part

            @pl.when(kk == nk - 1)
            def _():
                o_ref[...] = acc_ref[...].astype(out_dtype)

    res = _call(
        body, name=name, grid=(m // tm, n // tn, nk), inputs=(a, b),
        in_specs=[pl.BlockSpec((tm, tk), lambda i, j, kk: (i, kk)), pl.BlockSpec((tk, tn), lambda i, j, kk: (kk, j))],
        out_shape=[jax.ShapeDtypeStruct((m, n), out_dtype)],
        out_specs=[pl.BlockSpec((tm, tn), lambda i, j, kk: (i, j))],
        scratch=[pltpu.VMEM((tm, tn) if nk > 1 else (8, LANES), F32)],
        semantics=("parallel", "parallel", "arbitrary"), ex=ex)
    return res[0] if ex is None else (res[0][0], res[1])


def _mm_tn(a, b, *, name, out_dtype=F32, tm=1024, tn=512, tk=1024, shift=0):
    t, m = a.shape
    n = b.shape[1]
    tm, tn, tk = min(tm, m), min(tn, n), min(tk, t)
    nk = t // tk
    nb = m // tm
    assert m % tm == 0 and n % tn == 0 and t % tk == 0

    def body(a_ref, b_ref, o_ref, acc_ref):
        kk = pl.program_id(2)
        part = _dot_tn(a_ref[...], b_ref[...])

        @pl.when(kk == 0)
        def _():
            acc_ref[...] = part

        @pl.when(kk > 0)
        def _():
            acc_ref[...] += part

        @pl.when(kk == nk - 1)
        def _():
            o_ref[...] = acc_ref[...].astype(out_dtype)

    return pl.pallas_call(
        body, name=name, grid=(nb, n // tn, nk),
        out_shape=jax.ShapeDtypeStruct((m, n), out_dtype),
        in_specs=[pl.BlockSpec((tk, tm), lambda i, j, kk: (kk, i)), pl.BlockSpec((tk, tn), lambda i, j, kk: (kk, j))],
        out_specs=pl.BlockSpec((tm, tn), lambda i, j, kk: ((i + shift) % nb, j)),
        scratch_shapes=[pltpu.VMEM((tm, tn), F32)],
        compiler_params=_cparams(("parallel", "parallel", "arbitrary")),
    )(a, b)


def _mm_tn_gate_up(dgu, yn, *, name, out_dtype, tk=2048):
    t, m = dgu.shape
    n = yn.shape[1]
    tk = min(tk, t)
    nk = t // tk
    nb = m // (2 * FF_TILE)

    def body(a_ref, b_ref, og_ref, ou_ref, acc_ref):
        kk = pl.program_id(1)
        part = _dot_tn(a_ref[...], b_ref[...])

        @pl.when(kk == 0)
        def _():
            acc_ref[...] = part

        @pl.when(kk > 0)
        def _():
            acc_ref[...] += part

        @pl.when(kk == nk - 1)
        def _():
            og_ref[...] = acc_ref[:FF_TILE, :].astype(out_dtype)
            ou_ref[...] = acc_ref[FF_TILE:, :].astype(out_dtype)

    out = jax.ShapeDtypeStruct((m // 2, n), out_dtype)
    o_spec = pl.BlockSpec((FF_TILE, n), lambda i, kk: (i, 0))
    return pl.pallas_call(
        body, name=name, grid=(nb, nk), out_shape=(out, out),
        in_specs=[pl.BlockSpec((tk, 2 * FF_TILE), lambda i, kk: (kk, i)), pl.BlockSpec((tk, n), lambda i, kk: (kk, 0))],
        out_specs=(o_spec, o_spec),
        scratch_shapes=[pltpu.VMEM((2 * FF_TILE, n), F32)],
        compiler_params=_cparams(("parallel", "arbitrary")),
    )(dgu, yn)


def _mm_swiglu(a, w_gu, *, name, tm=MM_TM, ex=None):
    m, k = a.shape
    n = w_gu.shape[1]
    nt = n // (2 * FF_TILE)
    tm = min(tm, m)

    def body(a_ref, b_ref, gu_ref, act_ref):
        gu = _dot(a_ref[...], b_ref[...])
        gu_ref[...] = gu.astype(MXU_DTYPE)
        act_ref[...] = (_silu(gu[:, :FF_TILE]) * gu[:, FF_TILE:]).astype(MXU_DTYPE)

    return _call(
        body, name=name, grid=(m // tm, nt), inputs=(a, w_gu),
        in_specs=[pl.BlockSpec((tm, k), lambda i, j: (i, 0)), pl.BlockSpec((k, 2 * FF_TILE), lambda i, j: (0, j))],
        out_shape=[jax.ShapeDtypeStruct((m, n), MXU_DTYPE), jax.ShapeDtypeStruct((m, n // 2), MXU_DTYPE)],
        out_specs=[pl.BlockSpec((tm, 2 * FF_TILE), lambda i, j: (i, j)), pl.BlockSpec((tm, FF_TILE), lambda i, j: (i, j))],
        scratch=[], semantics=("parallel", "parallel"), ex=ex)


def _mm_dswiglu(dd, w_down_t, gu, *, name, tm=MM_TM):
    m, k = dd.shape
    n = gu.shape[1]
    nt = n // (2 * FF_TILE)
    tm = min(tm, m)

    def body(d_ref, w_ref, gu_ref, o_ref):
        dact = _dot(d_ref[...], w_ref[...])
        gate, up = gu_ref[:, :FF_TILE].astype(F32), gu_ref[:, FF_TILE:].astype(F32)
        o_ref[:, :FF_TILE] = (dact * up * _silu_grad(gate)).astype(MXU_DTYPE)
        o_ref[:, FF_TILE:] = (dact * _silu(gate)).astype(MXU_DTYPE)

    return pl.pallas_call(
        body, name=name, grid=(m // tm, nt),
        out_shape=jax.ShapeDtypeStruct((m, n), MXU_DTYPE),
        in_specs=[pl.BlockSpec((tm, k), lambda i, j: (i, 0)), pl.BlockSpec((k, FF_TILE), lambda i, j: (0, j)),
                  pl.BlockSpec((tm, 2 * FF_TILE), lambda i, j: (i, j))],
        out_specs=pl.BlockSpec((tm, 2 * FF_TILE), lambda i, j: (i, j)),
        compiler_params=_cparams(("parallel", "parallel")),
    )(dd, w_down_t, gu)


def _row_spec(width, tr=ROW_TILE):
    return pl.BlockSpec((tr, width), lambda i: (i, 0))


def _vec_spec(width, rows=1):
    return pl.BlockSpec((rows, width), lambda i: (0, 0))


def _rn_fwd(h, g, *, name, out_dtype):
    rows, d = h.shape

    def body(h_ref, g_ref, o_ref):
        x = h_ref[...]
        o_ref[...] = (x * _rms_scale(x) * g_ref[...]).astype(out_dtype)

    return pl.pallas_call(
        body, name=name, grid=(rows // ROW_TILE,),
        out_shape=jax.ShapeDtypeStruct((rows, d), out_dtype),
        in_specs=[_row_spec(d), _vec_spec(d)], out_specs=_row_spec(d),
        compiler_params=_cparams(("parallel",)),
    )(h, g)


def _resid_rn_fwd(h_in, o, g_post, g_next, *, name, next_dtype):
    rows, d = h_in.shape

    def body(h_ref, o_ref, gp_ref, gn_ref, ho_ref, yn_ref):
        ov = o_ref[...]
        h = h_ref[...] + ov * _rms_scale(ov) * gp_ref[...]
        ho_ref[...] = h
        yn_ref[...] = (h * _rms_scale(h) * gn_ref[...]).astype(next_dtype)

    return pl.pallas_call(
        body, name=name, grid=(rows // ROW_TILE,),
        out_shape=(jax.ShapeDtypeStruct((rows, d), F32), jax.ShapeDtypeStruct((rows, d), next_dtype)),
        in_specs=[_row_spec(d), _row_spec(d), _vec_spec(d), _vec_spec(d)],
        out_specs=(_row_spec(d), _row_spec(d)),
        compiler_params=_cparams(("parallel",)),
    )(h_in, o, g_post, g_next)


def _resid_loss(h_in, o, g_post, target, *, name):
    rows, d = h_in.shape

    def body(h_ref, o_ref, gp_ref, t_ref, dh_ref, loss_ref):
        ov = o_ref[...]
        err = h_ref[...] + ov * _rms_scale(ov) * gp_ref[...] - t_ref[...]
        dh_ref[...] = err * (1.0 / d)

        @pl.when(pl.program_id(0) == 0)
        def _():
            loss_ref[...] = jnp.zeros_like(loss_ref)

        loss_ref[...] += 0.5 * jnp.sum(jnp.mean(err * err, axis=-1, keepdims=True), axis=0, keepdims=True)

    return pl.pallas_call(
        body, name=name, grid=(rows // ROW_TILE,),
        out_shape=(jax.ShapeDtypeStruct((rows, d), F32), jax.ShapeDtypeStruct((1, 1), F32)),
        in_specs=[_row_spec(d), _row_spec(d), _vec_spec(d), _row_spec(d)],
        out_specs=(_row_spec(d), pl.BlockSpec((1, 1), lambda i: (0, 0))),
        compiler_params=_cparams(("arbitrary",)),
    )(h_in, o, g_post, target)


def _resid_bwd_post(dh, o, g_post, *, name, out_dtype):
    rows, d = dh.shape

    def body(dh_ref, o_ref, g_ref, do_ref, dg_ref):
        do, dg = _rms_bwd(dh_ref[...], o_ref[...], g_ref[...])
        do_ref[...] = do.astype(out_dtype)

        @pl.when(pl.program_id(0) == 0)
        def _():
            dg_ref[...] = jnp.zeros_like(dg_ref)

        dg_ref[...] += _colsum(dg)

    return pl.pallas_call(
        body, name=name, grid=(rows // ROW_TILE,),
        out_shape=(jax.ShapeDtypeStruct((rows, d), out_dtype), jax.ShapeDtypeStruct((1, d), F32)),
        in_specs=[_row_spec(d), _row_spec(d), _vec_spec(d)],
        out_specs=(_row_spec(d), _vec_spec(d)),
        compiler_params=_cparams(("arbitrary",)),
    )(dh, o, g_post)


def _resid_bwd_pre(dh, dyn_list, h_in, g_pre, *, name):
    rows, d = dh.shape
    n_dyn = len(dyn_list)

    def body(*refs):
        dh_ref, dyn_refs, h_ref, g_ref, out_ref, dg_ref = refs[0], refs[1:1 + n_dyn], *refs[1 + n_dyn:]
        dyn = dyn_refs[0][...]
        for r in dyn_refs[1:]:
            dyn = dyn + r[...]
        dx, dg = _rms_bwd(dyn, h_ref[...], g_ref[...])
        out_ref[...] = dh_ref[...] + dx

        @pl.when(pl.program_id(0) == 0)
        def _():
            dg_ref[...] = jnp.zeros_like(dg_ref)

        dg_ref[...] += _colsum(dg)

    return pl.pallas_call(
        body, name=name, grid=(rows // ROW_TILE,),
        out_shape=(jax.ShapeDtypeStruct((rows, d), F32), jax.ShapeDtypeStruct((1, d), F32)),
        in_specs=[_row_spec(d)] + [_row_spec(d)] * n_dyn + [_row_spec(d), _vec_spec(d)],
        out_specs=(_row_spec(d), _vec_spec(d)),
        compiler_params=_cparams(("arbitrary",)),
    )(dh, *dyn_list, h_in, g_pre)


def _layer_norm_stats(x):
    mu = jnp.mean(x, axis=-1, keepdims=True)
    xc = x - mu
    rstd = lax.rsqrt(jnp.mean(xc * xc, axis=-1, keepdims=True) + EPS)
    return xc * rstd, rstd


def _gmlp_fwd(proj, ln_g, ln_b, wm, bcol, *, name):
    rows = proj.shape[0]
    tr = ROW_TILE

    def body(u_ref, v_ref, lg_ref, lb_ref, wm_ref, bc_ref, ya_ref):
        vhat, _ = _layer_norm_stats(_gelu(v_ref[...]))
        vl = (vhat * lg_ref[...] + lb_ref[...]).astype(MXU_DTYPE)
        gu = _gelu(u_ref[...])
        bc = bc_ref[...]
        for c in range(tr // CHUNK):
            rs = slice(c * CHUNK, (c + 1) * CHUNK)
            for h in range(GM_HEADS):
                cs = slice(h * GM_HEAD_DIM, (h + 1) * GM_HEAD_DIM)
                mixed = _dot(wm_ref[h], vl[rs, cs]) + bc[:, h:h + 1]
                ya_ref[rs, cs] = (gu[rs, cs] * mixed).astype(MXU_DTYPE)

    return pl.pallas_call(
        body, name=name, grid=(rows // tr,),
        out_shape=jax.ShapeDtypeStruct((rows, 2 * D_MODEL), MXU_DTYPE),
        in_specs=[pl.BlockSpec((tr, D_MODEL), lambda i: (i, 2)), pl.BlockSpec((tr, D_MODEL), lambda i: (i, 3)),
                  _vec_spec(D_MODEL), _vec_spec(D_MODEL),
                  pl.BlockSpec((GM_HEADS, CHUNK, CHUNK), lambda i: (0, 0, 0)), _vec_spec(LANES, CHUNK)],
        out_specs=_row_spec(D_MODEL, tr),
        compiler_params=_cparams(("parallel",)),
    )(proj, proj, ln_g, ln_b, wm, bcol)


def _gmlp_bwd(proj, dcat, ln_g, ln_b, wm, wm_t, bcol, *, name):
    rows = proj.shape[0]
    tr = ROW_TILE

    def body(u_ref, v_ref, dy_ref, lg_ref, lb_ref, wm_ref, wmt_ref, bc_ref,
             duv_ref, dwm_ref, dbc_ref, dlg_ref, dlb_ref, dvl_scr):
        @pl.when(pl.program_id(0) == 0)
        def _():
            dwm_ref[...] = jnp.zeros_like(dwm_ref)
            dbc_ref[...] = jnp.zeros_like(dbc_ref)
            dlg_ref[...] = jnp.zeros_like(dlg_ref)
            dlb_ref[...] = jnp.zeros_like(dlb_ref)

        u, v = u_ref[...], v_ref[...]
        gv = _gelu(v)
        vhat, rstd = _layer_norm_stats(gv)
        lg = lg_ref[...]
        vl = (vhat * lg + lb_ref[...]).astype(MXU_DTYPE)
        gu = _gelu(u)
        dy = dy_ref[...]
        bc = bc_ref[...]
        row = lax.broadcasted_iota(jnp.int32, (CHUNK, CHUNK), 0)
        lane = lax.broadcasted_iota(jnp.int32, (CHUNK, CHUNK), 1)
        causal = lane <= row
        dbc = jnp.zeros((CHUNK, LANES), F32)
        for c in range(tr // CHUNK):
            rs = slice(c * CHUNK, (c + 1) * CHUNK)
            for h in range(GM_HEADS):
                cs = slice(h * GM_HEAD_DIM, (h + 1) * GM_HEAD_DIM)
                vl_h = vl[rs, cs]
                mixed = _dot(wm_ref[h], vl_h) + bc[:, h:h + 1]
                dy_h = dy[rs, cs]
                duv_ref[rs, cs] = (dy_h * mixed * _gelu_grad(u[rs, cs])).astype(MXU_DTYPE)
                dmixed = dy_h * gu[rs, cs]
                dwm_ref[h] += jnp.where(causal, _dot_nt(dmixed, vl_h), 0.0)
                dbc = dbc + jnp.where(lane == h, jnp.sum(dmixed, axis=1, keepdims=True), 0.0)
                dvl_scr[rs, cs] = _dot(wmt_ref[h], dmixed)
        dbc_ref[...] += dbc
        dvl = dvl_scr[...]
        dlg_ref[...] += _colsum(dvl * vhat)
        dlb_ref[...] += _colsum(dvl)
        dvh = dvl * lg
        dgv = rstd * (dvh - jnp.mean(dvh, axis=-1, keepdims=True) - vhat * jnp.mean(dvh * vhat, axis=-1, keepdims=True))
        duv_ref[:, D_MODEL:] = (dgv * _gelu_grad(v)).astype(MXU_DTYPE)

    return pl.pallas_call(
        body, name=name, grid=(rows // tr,),
        out_shape=(jax.ShapeDtypeStruct((rows, IN_MAIN), MXU_DTYPE),
                   jax.ShapeDtypeStruct((GM_HEADS, CHUNK, CHUNK), F32), jax.ShapeDtypeStruct((CHUNK, LANES), F32),
                   jax.ShapeDtypeStruct((1, D_MODEL), F32), jax.ShapeDtypeStruct((1, D_MODEL), F32)),
        in_specs=[pl.BlockSpec((tr, D_MODEL), lambda i: (i, 2)), pl.BlockSpec((tr, D_MODEL), lambda i: (i, 3)),
                  pl.BlockSpec((tr, D_MODEL), lambda i: (i, 0)), _vec_spec(D_MODEL), _vec_spec(D_MODEL),
                  pl.BlockSpec((GM_HEADS, CHUNK, CHUNK), lambda i: (0, 0, 0)),
                  pl.BlockSpec((GM_HEADS, CHUNK, CHUNK), lambda i: (0, 0, 0)), _vec_spec(LANES, CHUNK)],
        out_specs=(pl.BlockSpec((tr, 2 * D_MODEL), lambda i: (i, 1)),
                   pl.BlockSpec((GM_HEADS, CHUNK, CHUNK), lambda i: (0, 0, 0)), _vec_spec(LANES, CHUNK),
                   _vec_spec(D_MODEL), _vec_spec(D_MODEL)),
        scratch_shapes=[pltpu.VMEM((tr, D_MODEL), F32)],
        compiler_params=_cparams(("arbitrary",)),
    )(proj, proj, dcat, ln_g, ln_b, wm, wm_t, bcol)


def _conv_fwd(proj, conv_w8, conv_b, *, name):
    rows = proj.shape[0]
    tr = ROW_TILE
    hb = tr // CONV_HALO

    def body(x_ref, prev_ref, w_ref, b_ref, pre_ref, buf):
        first = pl.program_id(0) == 0
        buf[pl.ds(0, CONV_HALO), :] = jnp.where(first, 0.0, prev_ref[...])
        buf[pl.ds(CONV_HALO, tr), :] = x_ref[...]
        ext = buf[...]
        acc = jnp.broadcast_to(b_ref[...], (tr, CONV_DIM))
        for k in range(SSM_CONV):
            s = SSM_CONV - 1 - k
            acc = acc + w_ref[k:k + 1, :] * (x_ref[...] if s == 0 else pltpu.roll(ext, s, axis=0)[CONV_HALO:])
        pre_ref[...] = acc

    return pl.pallas_call(
        body, name=name, grid=(rows // tr,),
        out_shape=jax.ShapeDtypeStruct((rows, CONV_DIM), F32),
        in_specs=[pl.BlockSpec((tr, CONV_DIM), lambda i: (i, 0)),
                  pl.BlockSpec((CONV_HALO, CONV_DIM), lambda i: (jnp.maximum(i * hb - 1, 0), 0)),
                  _vec_spec(CONV_DIM, 8), _vec_spec(CONV_DIM)],
        out_specs=_row_spec(CONV_DIM, tr),
        scratch_shapes=[pltpu.VMEM((tr + CONV_HALO, CONV_DIM), F32)],
        compiler_params=_cparams(("parallel",)),
    )(proj, proj, conv_w8, conv_b)


def _conv_bwd(dpre, proj, conv_w8, dproj, *, name):
    rows = proj.shape[0]
    tr = ROW_TILE
    hb = tr // CONV_HALO
    nblk = rows // tr

    def body(d_ref, dnext_ref, x_ref, w_ref, dproj_ref, dx_ref, dw_ref, db_ref, dbuf):
        i = pl.program_id(0)

        @pl.when(i == 0)
        def _():
            dw_ref[...] = jnp.zeros_like(dw_ref)
            db_ref[...] = jnp.zeros_like(db_ref)

        d = d_ref[...]
        x = x_ref[...]
        dbuf[pl.ds(0, tr), :] = d
        dbuf[pl.ds(tr, CONV_HALO), :] = jnp.where(i == nblk - 1, 0.0, dnext_ref[...])
        ext = dbuf[...]
        acc = jnp.zeros((tr, CONV_DIM), F32)
        for k in range(SSM_CONV):
            s = SSM_CONV - 1 - k
            shifted = d if s == 0 else pltpu.roll(ext, tr + CONV_HALO - s, axis=0)[:tr]
            acc = acc + w_ref[k:k + 1, :] * shifted
            dw_ref[k:k + 1, :] += _colsum(shifted * x)
        dx_ref[...] = acc.astype(MXU_DTYPE)
        db_ref[...] += _colsum(d)

    return pl.pallas_call(
        body, name=name, grid=(nblk,),
        out_shape=(jax.ShapeDtypeStruct((rows, IN_MAIN), MXU_DTYPE), jax.ShapeDtypeStruct((8, CONV_DIM), F32),
                   jax.ShapeDtypeStruct((1, CONV_DIM), F32)),
        in_specs=[_row_spec(CONV_DIM, tr),
                  pl.BlockSpec((CONV_HALO, CONV_DIM), lambda i: (jnp.minimum((i + 1) * hb, rows // CONV_HALO - 1), 0)),
                  pl.BlockSpec((tr, CONV_DIM), lambda i: (i, 0)),
                  _vec_spec(CONV_DIM, 8), pl.BlockSpec(memory_space=pl.ANY)],
        out_specs=(_row_spec(CONV_DIM, tr), _vec_spec(CONV_DIM, 8), _vec_spec(CONV_DIM)),
        scratch_shapes=[pltpu.VMEM((tr + CONV_HALO, CONV_DIM), F32)],
        input_output_aliases={4: 0},
        compiler_params=_cparams(("arbitrary",)),
    )(dpre, dpre, proj, conv_w8, dproj)


N_PAIRS = SSM_HEADS // 2


def _chunk_iotas():
    row = lax.broadcasted_iota(jnp.int32, (CHUNK, CHUNK), 0)
    lane = lax.broadcasted_iota(jnp.int32, (CHUNK, CHUNK), 1)
    return row, lane, lane <= row


def _silu_and_grad(x):
    s = _sigmoid(x)
    return x * s, s * (1.0 + x * (1.0 - s))


def _pair_select(lo, mat, ha):
    return jnp.where(lo, mat[:, ha:ha + 1], mat[:, ha + 1:ha + 2])


def _ssd_fwd(pre, dtr, proj, dtb, alog, dsk, gn, *, name):
    rows = pre.shape[0]
    nc = rows // CHUNK

    def body(pre_ref, dtr_ref, z_ref, dtb_ref, alog_ref, dsk_ref, gn_ref, yb_ref, y_ref, st_ref, dt_ref, acum_ref, s_scr):
        @pl.when(pl.program_id(0) == 0)
        def _():
            s_scr[...] = jnp.zeros_like(s_scr)

        row, lane, tril = _chunk_iotas()
        dt = _softplus(dtr_ref[...] + dtb_ref[...])
        acum = _dot_exact(tril.astype(F32), dt * (-jnp.exp(alog_ref[...])))
        dt_ref[...] = dt
        acum_ref[...] = acum
        acum_t = acum.T
        lo = lane < 64
        eacum = jnp.exp(acum)
        a_end = acum[CHUNK - 1:CHUNK, :]
        e_end = jnp.exp(a_end)
        dte_all = jnp.exp(a_end - acum)
        dsk_v = dsk_ref[...]
        for g in range(SSM_GROUPS):
            b_g = _silu(pre_ref[:, 1024 + SSM_STATE * g:1024 + SSM_STATE * (g + 1)]).astype(MXU_DTYPE)
            c_g = _silu(pre_ref[:, 1536 + SSM_STATE * g:1536 + SSM_STATE * (g + 1)]).astype(MXU_DTYPE)
            cb = _dot_nt(c_g, b_g)
            gated = []
            for jj in range(2):
                j = 2 * g + jj
                ha = 2 * j
                cs = slice(LANES * j, LANES * (j + 1))
                xs = _silu(pre_ref[:, cs])
                xdt = xs * _pair_select(lo, dt, ha)
                xdt_m = xdt.astype(MXU_DTYPE)
                y_heads = []
                for h in (ha, ha + 1):
                    dec = jnp.exp(jnp.where(tril, acum[:, h:h + 1] - acum_t[h:h + 1, :], -jnp.inf))
                    y_heads.append(_dot(cb * dec, xdt_m))
                s_prev = s_scr[j]
                st_ref[0, j] = s_prev
                y = jnp.where(lo, y_heads[0], y_heads[1])
                y = y + _dot_nt(c_g, s_prev) * _pair_select(lo, eacum, ha)
                y = y + _pair_select(lo, dsk_v, ha) * xs
                xw = xdt * _pair_select(lo, dte_all, ha)
                e_rows = jnp.where(row < 64, e_end[:, ha:ha + 1], e_end[:, ha + 1:ha + 2])
                s_scr[j] = e_rows * s_prev + _dot(xw.T, b_g)
                y_ref[:, cs] = y
                gated.append(y * _silu(z_ref[:, cs]))
            ms = (jnp.sum(gated[0] * gated[0], axis=1, keepdims=True)
                  + jnp.sum(gated[1] * gated[1], axis=1, keepdims=True)) * (1.0 / 256.0)
            r = lax.rsqrt(ms + EPS)
            for jj in range(2):
                cs = slice(LANES * (2 * g + jj), LANES * (2 * g + jj + 1))
                yb_ref[:, cs] = (gated[jj] * r * gn_ref[:, cs]).astype(MXU_DTYPE)

    return pl.pallas_call(
        body, name=name, grid=(nc,),
        out_shape=(jax.ShapeDtypeStruct((rows, D_MODEL), MXU_DTYPE), jax.ShapeDtypeStruct((rows, D_MODEL), F32),
                   jax.ShapeDtypeStruct((nc, N_PAIRS, LANES, SSM_STATE), F32),
                   jax.ShapeDtypeStruct((rows, LANES), F32), jax.ShapeDtypeStruct((rows, LANES), F32)),
        in_specs=[_row_spec(CONV_DIM, CHUNK), _row_spec(LANES, CHUNK), pl.BlockSpec((CHUNK, D_MODEL), lambda i: (i, 4)),
                  _vec_spec(LANES), _vec_spec(LANES), _vec_spec(LANES), _vec_spec(D_MODEL)],
        out_specs=(_row_spec(D_MODEL, CHUNK), _row_spec(D_MODEL, CHUNK),
                   pl.BlockSpec((1, N_PAIRS, LANES, SSM_STATE), lambda i: (i, 0, 0, 0)),
                   _row_spec(LANES, CHUNK), _row_spec(LANES, CHUNK)),
        scratch_shapes=[pltpu.VMEM((N_PAIRS, LANES, SSM_STATE), F32)],
        compiler_params=_cparams(("arbitrary",)),
    )(pre, dtr, proj, dtb, alog, dsk, gn)


def _ssd_bwd(pre, dtr, dt_saved, acum_saved, proj, y_saved, states, dcat, dtb, alog, dsk, gn, *, name):
    rows = pre.shape[0]
    nc = rows // CHUNK

    def rev(i):
        return nc - 1 - i

    def body(pre_ref, dtr_ref, dt_ref, acum_ref, z_ref, y_ref, st_ref, dyb_ref, dtb_ref, alog_ref, dsk_ref, gn_ref,
             dpre_ref, dz_ref, ddtr_ref, dgn_ref, dvec_ref, g_scr):
        @pl.when(pl.program_id(0) == 0)
        def _():
            g_scr[...] = jnp.zeros_like(g_scr)
            dgn_ref[...] = jnp.zeros_like(dgn_ref)
            dvec_ref[...] = jnp.zeros_like(dvec_ref)

        dtb = dtb_ref[...]
        dtr = dtr_ref[...]
        row, lane, tril = _chunk_iotas()
        dt, acum = dt_ref[...], acum_ref[...]
        a = -jnp.exp(alog_ref[...])
        acum_t = acum.T
        lo = lane < 64
        eacum = jnp.exp(acum)
        a_end = acum[CHUNK - 1:CHUNK, :]
        e_end = jnp.exp(a_end)
        dte_all = jnp.exp(a_end - acum)
        dsk_v = dsk_ref[...]
        zero = jnp.zeros((CHUNK, LANES), F32)
        dacum_c, dacum_r, ddt_c = zero, zero, zero
        d_aend = jnp.zeros((1, LANES), F32)
        d_dsk = jnp.zeros((1, LANES), F32)
        lane1 = lane[0:1, :]

        def put_col(acc, h, colvec):
            return acc + jnp.where(lane == h, colvec, 0.0)

        for g in range(SSM_GROUPS):
            gated, sz, dgh = [], [], []
            for jj in range(2):
                cs = slice(LANES * (2 * g + jj), LANES * (2 * g + jj + 1))
                sz.append(_silu_and_grad(z_ref[:, cs]))
                gated.append(y_ref[:, cs] * sz[jj][0])
                dgh.append(dyb_ref[:, cs] * gn_ref[:, cs])
            ms = (jnp.sum(gated[0] * gated[0], axis=1, keepdims=True)
                  + jnp.sum(gated[1] * gated[1], axis=1, keepdims=True)) * (1.0 / 256.0)
            r = lax.rsqrt(ms + EPS)
            proj_g = (jnp.sum(dgh[0] * gated[0], axis=1, keepdims=True)
                      + jnp.sum(dgh[1] * gated[1], axis=1, keepdims=True)) * (1.0 / 256.0)
            dys = []
            for jj in range(2):
                cs = slice(LANES * (2 * g + jj), LANES * (2 * g + jj + 1))
                dgn_ref[:, cs] += _colsum(dyb_ref[:, cs] * gated[jj] * r)
                dgated = r * dgh[jj] - gated[jj] * (r * r * r * proj_g)
                dys.append(dgated * sz[jj][0])
                dz_ref[:, cs] = (dgated * y_ref[:, cs] * sz[jj][1]).astype(MXU_DTYPE)

            b_f, b_grad = _silu_and_grad(pre_ref[:, 1024 + SSM_STATE * g:1024 + SSM_STATE * (g + 1)])
            c_f, c_grad = _silu_and_grad(pre_ref[:, 1536 + SSM_STATE * g:1536 + SSM_STATE * (g + 1)])
            b_g = b_f.astype(MXU_DTYPE)
            c_g = c_f.astype(MXU_DTYPE)
            cb = _dot_nt(c_g, b_g)
            dcb = zero
            db_g, dc_g = zero, zero
            for jj in range(2):
                j = 2 * g + jj
                ha = 2 * j
                cs = slice(LANES * j, LANES * (j + 1))
                xs, xs_grad = _silu_and_grad(pre_ref[:, cs])
                dtsel = _pair_select(lo, dt, ha)
                xdt = xs * dtsel
                xdt_m = xdt.astype(MXU_DTYPE)
                dyp = dys[jj]
                dyp_m = dyp.astype(MXU_DTYPE)
                s_prev = st_ref[0, j]
                g_next = g_scr[j]
                eac = _pair_select(lo, eacum, ha)
                dte = _pair_select(lo, dte_all, ha)
                yoff = _dot_nt(c_g, s_prev) * eac
                t_off = dyp * yoff
                dye = dyp * eac
                dc_g = dc_g + _dot(dye, s_prev)
                bg = _dot_nt(b_g, g_next)
                dxdt = bg * dte
                xw = xdt * dte
                db_g = db_g + _dot(xw, g_next)
                t_w = xw * bg
                gs = g_next * s_prev
                e_rows = jnp.where(row < 64, e_end[:, ha:ha + 1], e_end[:, ha + 1:ha + 2])
                g_scr[j] = e_rows * g_next + _dot(dye.T, c_g)
                dxdt_heads = []
                for hh, h in enumerate((ha, ha + 1)):
                    half = slice(64 * hh, 64 * (hh + 1))
                    dec = jnp.exp(jnp.where(tril, acum[:, h:h + 1] - acum_t[h:h + 1, :], -jnp.inf))
                    m_h = cb * dec
                    dy_h = jnp.where(lo if hh == 0 else jnp.logical_not(lo), dyp, 0.0)
                    dm = _dot_nt(dy_h, xdt_m)
                    dxdt_heads.append(_dot(m_h.T, dyp_m))
                    e_h = dm * m_h
                    dcb = dcb + dm * dec
                    w_col = jnp.sum(t_w[:, half], axis=1, keepdims=True)
                    col = (jnp.sum(e_h, axis=1, keepdims=True) + jnp.sum(t_off[:, half], axis=1, keepdims=True) - w_col)
                    dacum_c = put_col(dacum_c, h, col)
                    dacum_r = dacum_r + jnp.where(row == h, _colsum(e_h), 0.0)
                    d_end_h = jnp.sum(w_col, keepdims=True) + e_end[:, h:h + 1] * jnp.sum(gs[half, :], keepdims=True)
                    d_aend = d_aend + jnp.where(lane1 == h, d_end_h, 0.0)
                dxdt = dxdt + jnp.where(lo, dxdt_heads[0], dxdt_heads[1])
                dsel = _pair_select(lo, dsk_v, ha)
                dxs = dxdt * dtsel + dsel * dyp
                dpre_ref[:, cs] = dxs * xs_grad
                t_dt = dxdt * xs
                t_dk = dyp * xs
                for hh, h in enumerate((ha, ha + 1)):
                    half = slice(64 * hh, 64 * (hh + 1))
                    ddt_c = put_col(ddt_c, h, jnp.sum(t_dt[:, half], axis=1, keepdims=True))
                    d_dsk = d_dsk + jnp.where(lane1 == h, jnp.sum(t_dk[:, half], keepdims=True), 0.0)
            dc_g = dc_g + _dot(dcb, b_g)
            db_g = db_g + _dot(dcb.T, c_g)
            dpre_ref[:, 1024 + SSM_STATE * g:1024 + SSM_STATE * (g + 1)] = db_g * b_grad
            dpre_ref[:, 1536 + SSM_STATE * g:1536 + SSM_STATE * (g + 1)] = dc_g * c_grad

        dacum = dacum_c - dacum_r.T + jnp.where(row == CHUNK - 1, d_aend, 0.0)
        dda = _dot_exact((lane >= row).astype(F32), dacum)
        ddt = dda * a + ddt_c
        ddtr = ddt * _sigmoid(dtr + dtb)
        ddtr_ref[...] = ddtr.astype(MXU_DTYPE)
        dvec_ref[0:1, :] += _colsum(ddtr)
        dvec_ref[1:2, :] += _colsum(dda * dt)
        dvec_ref[2:3, :] += d_dsk

    return pl.pallas_call(
        body, name=name, grid=(nc,),
        out_shape=(jax.ShapeDtypeStruct((rows, CONV_DIM), F32), jax.ShapeDtypeStruct((rows, D_MODEL), MXU_DTYPE),
                   jax.ShapeDtypeStruct((rows, LANES), MXU_DTYPE), jax.ShapeDtypeStruct((1, D_MODEL), F32),
                   jax.ShapeDtypeStruct((8, LANES), F32)),
        in_specs=[pl.BlockSpec((CHUNK, CONV_DIM), lambda i: (rev(i), 0)), pl.BlockSpec((CHUNK, LANES), lambda i: (rev(i), 0)),
                  pl.BlockSpec((CHUNK, LANES), lambda i: (rev(i), 0)), pl.BlockSpec((CHUNK, LANES), lambda i: (rev(i), 0)),
                  pl.BlockSpec((CHUNK, D_MODEL), lambda i: (rev(i), 4)), pl.BlockSpec((CHUNK, D_MODEL), lambda i: (rev(i), 0)),
                  pl.BlockSpec((1, N_PAIRS, LANES, SSM_STATE), lambda i: (rev(i), 0, 0, 0)),
                  pl.BlockSpec((CHUNK, D_MODEL), lambda i: (rev(i), 1)),
                  _vec_spec(LANES), _vec_spec(LANES), _vec_spec(LANES), _vec_spec(D_MODEL)],
        out_specs=(pl.BlockSpec((CHUNK, CONV_DIM), lambda i: (rev(i), 0)), pl.BlockSpec((CHUNK, D_MODEL), lambda i: (rev(i), 0)),
                   pl.BlockSpec((CHUNK, LANES), lambda i: (rev(i), 0)), _vec_spec(D_MODEL), _vec_spec(LANES, 8)),
        scratch_shapes=[pltpu.VMEM((N_PAIRS, LANES, SSM_STATE), F32)],
        compiler_params=_cparams(("arbitrary",)),
    )(pre, dtr, dt_saved, acum_saved, proj, y_saved, states, dcat, dtb, alog, dsk, gn)


SSD_BWD_CHUNKS = 1
GROUP_DIM = D_MODEL // SSM_GROUPS
HEADS_PER_GROUP = SSM_HEADS // SSM_GROUPS
HEAD_DIM = GROUP_DIM // HEADS_PER_GROUP


def _split(x):
    hi = x.astype(MXU_DTYPE)
    return hi, (x - hi.astype(F32)).astype(MXU_DTYPE)


def _dot_split(x, sel):
    hi, lo = _split(x)
    return jnp.dot(hi, sel, preferred_element_type=F32) + jnp.dot(lo, sel, preferred_element_type=F32)


def _dot_split_rhs(sel, x):
    hi, lo = _split(x)
    return jnp.dot(sel, hi, preferred_element_type=F32) + jnp.dot(sel, lo, preferred_element_type=F32)


def _dot_split_tn(x, sel):
    hi, lo = _split(x)
    dims = (((0,), (0,)), ((), ()))
    return (lax.dot_general(hi, sel, dims, preferred_element_type=F32)
            + lax.dot_general(lo, sel, dims, preferred_element_type=F32))


def _split3(x):
    hi = x.astype(MXU_DTYPE)
    r = x - hi.astype(F32)
    mid = r.astype(MXU_DTYPE)
    return hi, mid, (r - mid.astype(F32)).astype(MXU_DTYPE)


def _head_selectors():
    h = lax.broadcasted_iota(jnp.int32, (LANES, D_MODEL), 0)
    p = lax.broadcasted_iota(jnp.int32, (LANES, D_MODEL), 1)
    sel_t = (h == p // HEAD_DIM).astype(MXU_DTYPE)
    return sel_t, sel_t.T


def _expand_heads(per_head, e_end, selt_ref, sel_ref):
    stacked = jnp.concatenate(per_head, axis=0)
    wide = _dot_split(stacked, selt_ref[...])
    n = per_head[0].shape[0]
    e_cols = jnp.broadcast_to(e_end, (LANES, LANES)).T
    tall = _dot_split_rhs(sel_ref[...], e_cols)
    return [wide[n * i:n * (i + 1)] for i in range(len(per_head))], tall


def _by_quarter(index, pieces):
    out = pieces[3]
    for q in (2, 1, 0):
        out = jnp.where(index == q, pieces[q], out)
    return out


def _ssd_fwd_grouped(pre, dtr, proj, dtb, alog, dsk, gn, cat, *, name):
    rows = pre.shape[0]
    nc = rows // CHUNK

    def body(pre_ref, dtr_ref, z_ref, dtb_ref, alog_ref, dsk_ref, gn_ref, cat_ref, selt_ref, sel_ref, yb_ref, y_ref,
             st_ref, dt_ref, acum_ref, s_scr):
        @pl.when(pl.program_id(0) == 0)
        def _():
            s_scr[...] = jnp.zeros_like(s_scr)

        row, lane, tril = _chunk_iotas()
        dt = _softplus(dtr_ref[...] + dtb_ref[...])
        acum = _dot_exact(tril.astype(F32), dt * (-jnp.exp(alog_ref[...])))
        dt_ref[...] = dt
        acum_ref[...] = acum
        acum_t = acum.T
        a_end = acum[CHUNK - 1:CHUNK, :]
        (dt_x, eacum_x, dte_x, dsk_x), e_rows_all = _expand_heads(
            [dt, jnp.exp(acum), jnp.exp(a_end - acum), jnp.broadcast_to(dsk_ref[...], (CHUNK, LANES))], jnp.exp(a_end),
            selt_ref, sel_ref)
        lane_q = lax.broadcasted_iota(jnp.int32, (CHUNK, GROUP_DIM), 1) // HEAD_DIM

        for g in range(SSM_GROUPS):
            cs = slice(GROUP_DIM * g, GROUP_DIM * (g + 1))
            b_g = _silu(pre_ref[:, 1024 + SSM_STATE * g:1024 + SSM_STATE * (g + 1)]).astype(MXU_DTYPE)
            c_g = _silu(pre_ref[:, 1536 + SSM_STATE * g:1536 + SSM_STATE * (g + 1)]).astype(MXU_DTYPE)
            cb = _dot_nt(c_g, b_g)
            xs = _silu(pre_ref[:, cs])
            xdt = xs * dt_x[:, cs]
            m_stack = jnp.concatenate(
                [(cb * jnp.exp(jnp.where(tril, acum[:, h:h + 1] - acum_t[h:h + 1, :], -jnp.inf))).astype(MXU_DTYPE)
                 for h in range(4 * g, 4 * g + 4)], axis=0)
            y_all = _dot(m_stack, xdt)
            y = _by_quarter(lane_q, [y_all[CHUNK * q:CHUNK * (q + 1)] for q in range(HEADS_PER_GROUP)])
            s_prev = s_scr[g]
            st_ref[0, g] = s_prev
            y = y + _dot_nt(c_g, s_prev) * eacum_x[:, cs] + dsk_x[:, cs] * xs
            xw = xdt * dte_x[:, cs]
            s_scr[g] = e_rows_all[cs, :] * s_prev + _dot(xw.T, b_g)
            y_ref[:, cs] = y
            gated = y * _silu(z_ref[:, cs])
            r = lax.rsqrt(jnp.mean(gated * gated, axis=1, keepdims=True) + EPS)
            yb_ref[:, cs] = (gated * r * gn_ref[:, cs]).astype(MXU_DTYPE)

    return pl.pallas_call(
        body, name=name, grid=(nc,),
        out_shape=(jax.ShapeDtypeStruct((rows, 2 * D_MODEL), MXU_DTYPE), jax.ShapeDtypeStruct((rows, D_MODEL), F32),
                   jax.ShapeDtypeStruct((nc, SSM_GROUPS, GROUP_DIM, SSM_STATE), F32),
                   jax.ShapeDtypeStruct((rows, LANES), F32), jax.ShapeDtypeStruct((rows, LANES), F32)),
        in_specs=[_row_spec(CONV_DIM, CHUNK), _row_spec(LANES, CHUNK), pl.BlockSpec((CHUNK, D_MODEL), lambda i: (i, 4)),
                  _vec_spec(LANES), _vec_spec(LANES), _vec_spec(LANES), _vec_spec(D_MODEL),
                  pl.BlockSpec(memory_space=pl.ANY), _vec_spec(D_MODEL, LANES), _vec_spec(LANES, D_MODEL)],
        out_specs=(pl.BlockSpec((CHUNK, D_MODEL), lambda i: (i, 1)), _row_spec(D_MODEL, CHUNK),
                   pl.BlockSpec((1, SSM_GROUPS, GROUP_DIM, SSM_STATE), lambda i: (i, 0, 0, 0)),
                   _row_spec(LANES, CHUNK), _row_spec(LANES, CHUNK)),
        scratch_shapes=[pltpu.VMEM((SSM_GROUPS, GROUP_DIM, SSM_STATE), F32)],
        input_output_aliases={7: 0},
        compiler_params=_cparams(("arbitrary",)),
    )(pre, dtr, proj, dtb, alog, dsk, gn, cat, *_head_selectors())


def _ssd_bwd_grouped(pre, dtr, dt_saved, acum_saved, proj, y_saved, states, dcat, dtb, alog, dsk, gn, dproj, *, name):
    rows = pre.shape[0]
    cps = SSD_BWD_CHUNKS
    tr = CHUNK * cps
    nsteps = rows // tr

    def rev(i):
        return nsteps - 1 - i

    def body(pre_ref, dtr_ref, dt_ref, acum_ref, z_ref, y_ref, st_ref, dyb_ref, dtb_ref, alog_ref, dsk_ref, gn_ref,
             dproj_ref, selt_ref, sel_ref, dpre_ref, dz_ref, ddtr_ref, dgn_ref, dvec_ref, g_scr):
        @pl.when(pl.program_id(0) == 0)
        def _():
            g_scr[...] = jnp.zeros_like(g_scr)
            dgn_ref[...] = jnp.zeros_like(dgn_ref)
            dvec_ref[...] = jnp.zeros_like(dvec_ref)

        for cc in reversed(range(cps)):
            at = lambda ref: ref.at[pl.ds(cc * CHUNK, CHUNK)]
            chunk(at(pre_ref), at(dtr_ref), at(dt_ref), at(acum_ref), at(z_ref), at(y_ref), st_ref.at[cc], at(dyb_ref),
                  dtb_ref, alog_ref, dsk_ref, gn_ref, selt_ref, sel_ref, at(dpre_ref), at(dz_ref), at(ddtr_ref), dgn_ref,
                  dvec_ref, g_scr)

    def chunk(pre_ref, dtr_ref, dt_ref, acum_ref, z_ref, y_ref, st_ref, dyb_ref, dtb_ref, alog_ref, dsk_ref, gn_ref,
              selt_ref, sel_ref, dpre_ref, dz_ref, ddtr_ref, dgn_ref, dvec_ref, g_scr):
        row, lane, tril = _chunk_iotas()
        triu = lane >= row
        dt, acum = dt_ref[...], acum_ref[...]
        a = -jnp.exp(alog_ref[...])
        acum_t = acum.T
        a_end = acum[CHUNK - 1:CHUNK, :]
        e_end = jnp.exp(a_end)
        (dt_x, eacum_x, dte_x, dsk_x), e_rows_all = _expand_heads(
            [dt, jnp.exp(acum), jnp.exp(a_end - acum), jnp.broadcast_to(dsk_ref[...], (CHUNK, LANES))], e_end,
            selt_ref, sel_ref)
        lane_q = lax.broadcasted_iota(jnp.int32, (CHUNK, GROUP_DIM), 1) // HEAD_DIM
        zero = jnp.zeros((CHUNK, LANES), F32)
        dacum_c, dacum_r, ddt_c = zero, zero, zero
        d_aend = jnp.zeros((1, LANES), F32)
        d_dsk = jnp.zeros((1, LANES), F32)
        iota = lambda shape, dim: lax.broadcasted_iota(jnp.int32, shape, dim)
        q256, lane_256 = iota((GROUP_DIM, LANES), 0) // HEAD_DIM, iota((GROUP_DIM, LANES), 1)
        q512, lane_512 = iota((4 * CHUNK, LANES), 0) // CHUNK, iota((4 * CHUNK, LANES), 1)
        row_512t, q512t = iota((LANES, 4 * CHUNK), 0), iota((LANES, 4 * CHUNK), 1) // CHUNK

        for g in range(SSM_GROUPS):
            cs = slice(GROUP_DIM * g, GROUP_DIM * (g + 1))
            yv = y_ref[:, cs]
            sz, sz_grad = _silu_and_grad(z_ref[:, cs])
            gated = yv * sz
            dyb = dyb_ref[:, cs]
            dgh = dyb * gn_ref[:, cs]
            r = lax.rsqrt(jnp.mean(gated * gated, axis=1, keepdims=True) + EPS)
            dgn_ref[:, cs] += _colsum(dyb * gated * r)
            dgated = r * dgh - gated * (r * r * r * jnp.mean(dgh * gated, axis=1, keepdims=True))
            dy = dgated * sz
            dz_ref[:, cs] = (dgated * yv * sz_grad).astype(MXU_DTYPE)

            b_f, b_grad = _silu_and_grad(pre_ref[:, 1024 + SSM_STATE * g:1024 + SSM_STATE * (g + 1)])
            c_f, c_grad = _silu_and_grad(pre_ref[:, 1536 + SSM_STATE * g:1536 + SSM_STATE * (g + 1)])
            b_g, c_g = b_f.astype(MXU_DTYPE), c_f.astype(MXU_DTYPE)
            xs, xs_grad = _silu_and_grad(pre_ref[:, cs])
            dtq = dt_x[:, cs]
            xdt = xs * dtq
            xdt_m = xdt.astype(MXU_DTYPE)
            dy_m = dy.astype(MXU_DTYPE)
            s_prev = st_ref[g]
            g_next = g_scr[g]
            eacq, dteq = eacum_x[:, cs], dte_x[:, cs]
            t_off = dy * (_dot_nt(c_g, s_prev) * eacq)
            dye = dy * eacq
            dc_g = _dot(dye, s_prev)
            bg = _dot_nt(b_g, g_next)
            xw = xdt * dteq
            db_g = _dot(xw, g_next)
            t_w = xw * bg
            gs = g_next * s_prev
            g_scr[g] = e_rows_all[cs, :] * g_next + _dot(dye.T, c_g)
            cb = _dot_nt(c_g, b_g)
            cb_t = cb.T
            heads = range(4 * g, 4 * g + 4)
            decs = [jnp.exp(jnp.where(tril, acum[:, h:h + 1] - acum_t[h:h + 1, :], -jnp.inf)) for h in heads]
            mt_stack = jnp.concatenate(
                [(cb_t * jnp.exp(jnp.where(triu, acum_t[h:h + 1, :] - acum[:, h:h + 1], -jnp.inf))).astype(MXU_DTYPE)
                 for h in heads], axis=0)
            dy_stack = jnp.concatenate([jnp.where(lane_q == q, dy, 0.0).astype(MXU_DTYPE)
                                        for q in range(HEADS_PER_GROUP)], axis=0)
            dm_all = _dot_nt(dy_stack, xdt_m)
            dx_all = _dot(mt_stack, dy_m)
            dxdt = bg * dteq + _by_quarter(lane_q, [dx_all[CHUNK * q:CHUNK * (q + 1)] for q in range(HEADS_PER_GROUP)])
            t_dt = dxdt * xs
            t_dk = dy * xs
            dec_stack = jnp.concatenate(decs, axis=0)
            dm_dec = dm_all * dec_stack
            e_all = dm_dec * jnp.concatenate([cb] * HEADS_PER_GROUP, axis=0)
            dcb = functools.reduce(jnp.add, [dm_dec[CHUNK * q:CHUNK * (q + 1)] for q in range(HEADS_PER_GROUP)])
            one = jnp.ones((), MXU_DTYPE)
            sel_lanes = jnp.where(q256 + 4 * g == lane_256, one, 0)
            sel_rows = jnp.where(q512 + 4 * g == lane_512, one, 0)
            sel_rows_t = jnp.where(row_512t == q512t + 4 * g, one, 0)
            e_lanes = jnp.concatenate([e_all[CHUNK * q:CHUNK * (q + 1)] for q in range(HEADS_PER_GROUP)], axis=1)
            w_heads = _dot_split(t_w, sel_lanes)
            dacum_c = dacum_c + _dot_split(e_lanes, sel_rows) + _dot_split(t_off, sel_lanes) - w_heads
            dacum_r = dacum_r + _dot_split_rhs(sel_rows_t, e_all)
            ddt_c = ddt_c + _dot_split(t_dt, sel_lanes)
            d_aend = d_aend + _colsum(w_heads) + e_end * _colsum(_dot_split_tn(gs, sel_lanes))
            d_dsk = d_dsk + _colsum(_dot_split(t_dk, sel_lanes))
            dpre_ref[:, cs] = (dxdt * dtq + dsk_x[:, cs] * dy) * xs_grad
            dc_g = dc_g + _dot(dcb, b_g)
            db_g = db_g + _dot(dcb.T, c_g)
            dpre_ref[:, 1024 + SSM_STATE * g:1024 + SSM_STATE * (g + 1)] = db_g * b_grad
            dpre_ref[:, 1536 + SSM_STATE * g:1536 + SSM_STATE * (g + 1)] = dc_g * c_grad

        dacum = dacum_c - dacum_r.T + jnp.where(row == CHUNK - 1, d_aend, 0.0)
        dda = _dot_exact(triu.astype(F32), dacum)
        ddtr = (dda * a + ddt_c) * _sigmoid(dtr_ref[...] + dtb_ref[...])
        ddtr_ref[...] = ddtr.astype(MXU_DTYPE)
        dvec_ref[0:1, :] += _colsum(ddtr)
        dvec_ref[1:2, :] += _colsum(dda * dt)
        dvec_ref[2:3, :] += d_dsk

    return pl.pallas_call(
        body, name=name, grid=(nsteps,),
        out_shape=(jax.ShapeDtypeStruct((rows, CONV_DIM), F32), jax.ShapeDtypeStruct((rows, IN_MAIN), MXU_DTYPE),
                   jax.ShapeDtypeStruct((rows, LANES), MXU_DTYPE), jax.ShapeDtypeStruct((1, D_MODEL), F32),
                   jax.ShapeDtypeStruct((8, LANES), F32)),
        in_specs=[pl.BlockSpec((tr, CONV_DIM), lambda i: (rev(i), 0)), pl.BlockSpec((tr, LANES), lambda i: (rev(i), 0)),
                  pl.BlockSpec((tr, LANES), lambda i: (rev(i), 0)), pl.BlockSpec((tr, LANES), lambda i: (rev(i), 0)),
                  pl.BlockSpec((tr, D_MODEL), lambda i: (rev(i), 4)), pl.BlockSpec((tr, D_MODEL), lambda i: (rev(i), 0)),
                  pl.BlockSpec((cps, SSM_GROUPS, GROUP_DIM, SSM_STATE), lambda i: (rev(i), 0, 0, 0)),
                  pl.BlockSpec((tr, D_MODEL), lambda i: (rev(i), 1)),
                  _vec_spec(LANES), _vec_spec(LANES), _vec_spec(LANES), _vec_spec(D_MODEL),
                  pl.BlockSpec(memory_space=pl.ANY), _vec_spec(D_MODEL, LANES), _vec_spec(LANES, D_MODEL)],
        out_specs=(pl.BlockSpec((tr, CONV_DIM), lambda i: (rev(i), 0)), pl.BlockSpec((tr, D_MODEL), lambda i: (rev(i), 4)),
                   pl.BlockSpec((tr, LANES), lambda i: (rev(i), 0)), _vec_spec(D_MODEL), _vec_spec(LANES, 8)),
        scratch_shapes=[pltpu.VMEM((SSM_GROUPS, GROUP_DIM, SSM_STATE), F32)],
        input_output_aliases={12: 1},
        compiler_params=_cparams(("arbitrary",)),
    )(pre, dtr, dt_saved, acum_saved, proj, y_saved, states, dcat, dtb, alog, dsk, gn, dproj, *_head_selectors())


def _pool_counts(first_row, n_rows, win):
    t = first_row + lax.broadcasted_iota(jnp.int32, (n_rows, POOL_DIM), 0)
    return jnp.minimum(t + 1, win).astype(F32)


def _pool_fwd(yn, pool_w, pool_b, pool_scale, *, name):
    rows = yn.shape[0]
    tr = ROW_TILE
    hb = tr // POOL_HALO

    def body(y_ref, prev_ref, w_ref, b_ref, s_ref, pm_ref, diff_ref, buf):
        i = pl.program_id(0)
        buf[pl.ds(0, POOL_HALO), :] = jnp.where(i == 0, 0.0, prev_ref[...])
        buf[pl.ds(POOL_HALO, tr), :] = y_ref[...]
        for g, win in enumerate(POOL_WINDOWS):
            cs = slice(POOL_DIM * g, POOL_DIM * (g + 1))
            acc = buf[pl.ds(POOL_HALO, tr), cs]
            for s in range(1, win):
                acc = acc + buf[pl.ds(POOL_HALO - s, tr), cs]
            diff = (acc / _pool_counts(i * tr, tr, win) - y_ref[:, cs]).astype(MXU_DTYPE)
            diff_ref[:, cs] = diff
            pm_ref[:, cs] = (_dot(diff, w_ref[g]) + b_ref[:, cs]) * s_ref[:, cs]

    return pl.pallas_call(
        body, name=name, grid=(rows // tr,),
        out_shape=(jax.ShapeDtypeStruct((rows, D_MODEL), F32), jax.ShapeDtypeStruct((rows, D_MODEL), MXU_DTYPE)),
        in_specs=[_row_spec(D_MODEL, tr),
                  pl.BlockSpec((POOL_HALO, D_MODEL), lambda i: (jnp.maximum(i * hb - 1, 0), 0)),
                  pl.BlockSpec((4, POOL_DIM, POOL_DIM), lambda i: (0, 0, 0)), _vec_spec(D_MODEL), _vec_spec(D_MODEL)],
        out_specs=(_row_spec(D_MODEL, tr), _row_spec(D_MODEL, tr)),
        scratch_shapes=[pltpu.VMEM((tr + POOL_HALO, D_MODEL), F32)],
        compiler_params=_cparams(("parallel",)),
    )(yn, yn, pool_w, pool_b, pool_scale)


def _pool_bwd(dpm, diff, pool_w, pool_w_t, pool_b, pool_scale, *, name):
    rows = dpm.shape[0]
    tr = ROW_TILE
    hb = tr // POOL_HALO
    nblk = rows // tr

    def body(d_ref, dnext_ref, diff_ref, w_ref, wt_ref, b_ref, s_ref, dy_ref, dw_ref, db_ref, ds_ref, ebuf):
        i = pl.program_id(0)

        @pl.when(i == 0)
        def _():
            dw_ref[...] = jnp.zeros_like(dw_ref)
            db_ref[...] = jnp.zeros_like(db_ref)
            ds_ref[...] = jnp.zeros_like(ds_ref)

        last = i == nblk - 1
        for g, win in enumerate(POOL_WINDOWS):
            cs = slice(POOL_DIM * g, POOL_DIM * (g + 1))
            d = d_ref[:, cs]
            diff = diff_ref[:, cs]
            out_pre = _dot(diff, w_ref[g]) + b_ref[:, cs]
            ds_ref[:, cs] += _colsum(d * out_pre)
            dout = d * s_ref[:, cs]
            db_ref[:, cs] += _colsum(dout)
            dw_ref[g] += _dot_tn(diff, dout)
            ddiff = _dot(dout, wt_ref[g])
            ddiff_next = _dot(jnp.where(last, 0.0, dnext_ref[:, cs]) * s_ref[:, cs], wt_ref[g])
            ebuf[pl.ds(0, tr), cs] = ddiff / _pool_counts(i * tr, tr, win)
            ebuf[pl.ds(tr, POOL_HALO), cs] = ddiff_next / _pool_counts((i + 1) * tr, POOL_HALO, win)
            acc = -ddiff
            for s in range(win):
                acc = acc + ebuf[pl.ds(s, tr), cs]
            dy_ref[:, cs] = acc

    return pl.pallas_call(
        body, name=name, grid=(nblk,),
        out_shape=(jax.ShapeDtypeStruct((rows, D_MODEL), F32), jax.ShapeDtypeStruct((4, POOL_DIM, POOL_DIM), F32),
                   jax.ShapeDtypeStruct((1, D_MODEL), F32), jax.ShapeDtypeStruct((1, D_MODEL), F32)),
        in_specs=[_row_spec(D_MODEL, tr),
                  pl.BlockSpec((POOL_HALO, D_MODEL), lambda i: (jnp.minimum((i + 1) * hb, rows // POOL_HALO - 1), 0)),
                  _row_spec(D_MODEL, tr),
                  pl.BlockSpec((4, POOL_DIM, POOL_DIM), lambda i: (0, 0, 0)),
                  pl.BlockSpec((4, POOL_DIM, POOL_DIM), lambda i: (0, 0, 0)), _vec_spec(D_MODEL), _vec_spec(D_MODEL)],
        out_specs=(_row_spec(D_MODEL, tr), pl.BlockSpec((4, POOL_DIM, POOL_DIM), lambda i: (0, 0, 0)),
                   _vec_spec(D_MODEL), _vec_spec(D_MODEL)),
        scratch_shapes=[pltpu.VMEM((tr + POOL_HALO, D_MODEL), F32)],
        compiler_params=_cparams(("arbitrary",)),
    )(dpm, dpm, diff, pool_w, pool_w_t, pool_b, pool_scale)


def _row_tile(rows, cap, step):
    best = rows
    for t in range(step, min(rows, cap) + 1, step):
        if rows % t == 0:
            best = t
    return best if best <= cap else rows


def _sum8(recv, *, name):
    _, r, c = recv.shape
    step = 8 if recv.dtype == F32 else 16

    def body(r_ref, g_ref):
        g = r_ref[0].astype(F32)
        for j in range(1, N_DEV):
            g = g + r_ref[j].astype(F32)
        g_ref[...] = g

    if r % step == 0:
        tr = _row_tile(r, 256, step)
        grid, in_spec, out_spec = (r // tr,), pl.BlockSpec((N_DEV, tr, c), lambda i: (0, i, 0)), pl.BlockSpec((tr, c), lambda i: (i, 0))
    else:
        tc = 256
        grid, in_spec, out_spec = (c // tc,), pl.BlockSpec((N_DEV, r, tc), lambda i: (0, 0, i)), pl.BlockSpec((r, tc), lambda i: (0, i))
    return pl.pallas_call(
        body, name=name, grid=grid, out_shape=jax.ShapeDtypeStruct((r, c), F32),
        in_specs=[in_spec], out_specs=out_spec, compiler_params=_cparams(("parallel",)),
    )(recv)


def _adamw(g, w, m, v, *, name):
    rows, cols = w.shape
    tr = _row_tile(rows, max(8, (256 * 1024) // cols // 8 * 8), 8)
    c1 = 1.0 / (1.0 - ADAM_B1 ** ADAM_STEP)
    c2 = 1.0 / (1.0 - ADAM_B2 ** ADAM_STEP)

    def body(g_ref, w_ref, m_ref, v_ref, d_ref, mo_ref, vo_ref):
        g = g_ref[...]
        m_new = ADAM_B1 * m_ref[...] + (1.0 - ADAM_B1) * g
        v_new = ADAM_B2 * v_ref[...] + (1.0 - ADAM_B2) * (g * g)
        mo_ref[...] = m_new
        vo_ref[...] = v_new
        d_ref[...] = -ADAM_LR * ((m_new * c1) / (jnp.sqrt(v_new * c2) + ADAM_EPS) + ADAM_WD * w_ref[...])

    spec = pl.BlockSpec((tr, cols), lambda i: (i, 0))
    return pl.pallas_call(
        body, name=name, grid=(rows // tr,),
        out_shape=tuple(jax.ShapeDtypeStruct((rows, cols), F32) for _ in range(3)),
        in_specs=[spec] * 4, out_specs=(spec, spec, spec),
        compiler_params=_cparams(("parallel",)),
    )(g, w, m, v)


def _pad_rows(flat, mult):
    n = flat.shape[-1]
    pad = (-n) % mult
    if pad:
        flat = jnp.pad(flat, [(0, 0)] * (flat.ndim - 1) + [(0, pad)])
    return flat


def _pack_blocks(blocks, row_mult):
    flat = jnp.concatenate([_pad_rows(b.reshape(-1), LANES) for b in blocks])
    return _pad_rows(flat, LANES * row_mult).reshape(-1, LANES)


def _block_sizes(blocks):
    return [-(-math.prod(b.shape) // LANES) * LANES for b in blocks]


def _unpack_blocks(slab, like, lead=()):
    flat = slab.reshape(lead + (-1,))
    out, off = [], 0
    for b, size in zip(like, _block_sizes(like)):
        n = math.prod(b.shape)
        out.append(flat[..., off:off + n].reshape(lead + tuple(b.shape)))
        off += size
    return out


def _join_shards(gathered, axis):
    return jnp.concatenate([gathered[j] for j in range(N_DEV)], axis=axis)


def _split_shards(full, axis):
    return jnp.stack(jnp.split(full, N_DEV, axis=axis))


def _interleave_ff(w_gate, w_up):
    k = w_gate.shape[0]
    nt = D_FF // FF_TILE
    return jnp.stack([w_gate.reshape(k, nt, FF_TILE), w_up.reshape(k, nt, FF_TILE)], axis=2).reshape(k, 2 * D_FF)


def _row128(vec):
    return jnp.pad(vec.reshape(1, -1), ((0, 0), (0, LANES - vec.shape[-1])))


def kernel(x, norm_g, w_in, gm_ln_g, gm_ln_b, gm_ws, gm_bs, conv_w, conv_b, dt_bias, a_log, d_skip, ssm_norm_g, w_out, pool_w, pool_b, pool_scale, ffn_w_gate, ffn_w_up, ffn_w_down, loss_target, m_norm_g, m_w_in, m_gm_ln_g, m_gm_ln_b, m_gm_ws, m_gm_bs, m_conv_w, m_conv_b, m_dt_bias, m_a_log, m_d_skip, m_ssm_norm_g, m_w_out, m_pool_w, m_pool_b, m_pool_scale, m_ffn_w_gate, m_ffn_w_up, m_ffn_w_down, v_norm_g, v_w_in, v_gm_ln_g, v_gm_ln_b, v_gm_ws, v_gm_bs, v_conv_w, v_conv_b, v_dt_bias, v_a_log, v_d_skip, v_ssm_norm_g, v_w_out, v_pool_w, v_pool_b, v_pool_scale, v_ffn_w_gate, v_ffn_w_up, v_ffn_w_down):
    w_loc = dict(norm_g=norm_g, w_in=w_in, gm_ln_g=gm_ln_g, gm_ln_b=gm_ln_b, gm_ws=gm_ws, gm_bs=gm_bs, conv_w=conv_w,
                 conv_b=conv_b, dt_bias=dt_bias, a_log=a_log, d_skip=d_skip, ssm_norm_g=ssm_norm_g, w_out=w_out,
                 pool_w=pool_w, pool_b=pool_b, pool_scale=pool_scale, ffn_w_gate=ffn_w_gate, ffn_w_up=ffn_w_up,
                 ffn_w_down=ffn_w_down)
    m_loc = dict(zip(WEIGHTS, [m_norm_g, m_w_in, m_gm_ln_g, m_gm_ln_b, m_gm_ws, m_gm_bs, m_conv_w, m_conv_b, m_dt_bias,
                               m_a_log, m_d_skip, m_ssm_norm_g, m_w_out, m_pool_w, m_pool_b, m_pool_scale,
                               m_ffn_w_gate, m_ffn_w_up, m_ffn_w_down]))
    v_loc = dict(zip(WEIGHTS, [v_norm_g, v_w_in, v_gm_ln_g, v_gm_ln_b, v_gm_ws, v_gm_bs, v_conv_w, v_conv_b, v_dt_bias,
                               v_a_log, v_d_skip, v_ssm_norm_g, v_w_out, v_pool_w, v_pool_b, v_pool_scale,
                               v_ffn_w_gate, v_ffn_w_up, v_ffn_w_down]))

    small_blocks = [w_loc[n] for n in GATHER_F32]
    got = _gather_two_level([w_in[0].astype(MXU_DTYPE), _pack_blocks(small_blocks, 8)], name="gather_first")
    full = {n: w_loc[n] for n in WEIGHTS if SHARD_AXIS[n] is None}
    full['w_in'] = got[0].transpose(1, 0, 2).reshape(1, D_MODEL, -1)
    for n, g in zip(GATHER_F32, _unpack_blocks(got[1], small_blocks, (N_DEV,))):
        full[n] = _join_shards(g, SHARD_AXIS[n])
    shards = {n: w_loc[n].astype(MXU_DTYPE) for n in ('w_out', 'ffn_w_gate', 'ffn_w_up', 'ffn_w_down', 'pool_w')}

    loss_part, grad_x, grads, recv = _local_step(x[0], loss_target[0], full, shards)

    small = [n for n in WEIGHTS if n not in BIG_WEIGHTS]
    like = [w_loc[n] for n in small]
    slots = []
    for n in small:
        ax = SHARD_AXIS[n]
        g = grads[n].astype(F32)
        sh = _split_shards(g, ax) if ax is not None else jnp.broadcast_to(g[None], (N_DEV,) + g.shape)
        slots.append(_pad_rows(sh.reshape(N_DEV, -1), LANES))
    send_small = _pad_rows(jnp.concatenate(slots, axis=1), LANES * 8).reshape(N_DEV, -1, LANES)
    recv_small, = _exchange([send_small], ['slots'], name="exchange_last")

    g_small = _sum8(recv_small, name="sum_small")
    g_own = dict(zip(small, _unpack_blocks(g_small, like)))
    g_own['w_in'] = _sum8(recv['w_in'], name="sum_w_in").T[None]
    g_own['w_out'] = _sum8(recv['w_out'], name="sum_w_out")[None]
    g_own['ffn_w_gate'] = jnp.stack([_sum8(recv['ffn_w_gate'][l], name=f"sum_ffn{l}_gate").T for l in range(2)])
    g_own['ffn_w_up'] = jnp.stack([_sum8(recv['ffn_w_up'][l], name=f"sum_ffn{l}_up").T for l in range(2)])
    g_own['ffn_w_down'] = jnp.stack([_sum8(recv['ffn_w_down'][l], name=f"sum_ffn{l}_down") for l in range(2)])

    delta, m_new, v_new = {}, {}, {}
    pk = lambda d: _pack_blocks([d[n] for n in small], 8)
    d_s, m_s, v_s = _adamw(g_small, pk(w_loc), pk(m_loc), pk(v_loc), name="adamw_small")
    for dst, slab in ((delta, d_s), (m_new, m_s), (v_new, v_s)):
        dst.update(zip(small, _unpack_blocks(slab, like)))
    for n in BIG_WEIGHTS:
        shape = w_loc[n].shape
        two_d = lambda t: t.reshape(-1, shape[-1])
        res = _adamw(two_d(g_own[n]), two_d(w_loc[n]), two_d(m_loc[n]), two_d(v_loc[n]), name=f"adamw_{n}")
        delta[n], m_new[n], v_new[n] = (t.reshape(shape) for t in res)

    loss = lax.psum(loss_part[0, 0], ("x", "y", "c"))
    outs = [d[n] for d in (g_own, delta, m_new, v_new) for n in WEIGHTS]
    return (loss, grad_x[None], *outs)


def _local_step(h0, tgt, full, shards):
    gm_ln_g, gm_ln_b, gm_ws, gm_bs = full['gm_ln_g'], full['gm_ln_b'], full['gm_ws'], full['gm_bs']
    conv_b, dt_bias, a_log, d_skip, ssm_norm_g = (full['conv_b'], full['dt_bias'], full['a_log'], full['d_skip'],
                                                  full['ssm_norm_g'])
    w_in_f = full['w_in'][0]
    w_main = jnp.concatenate([w_in_f[:, 3072:5120], w_in_f[:, :3072]], axis=1)
    w_dt = jnp.pad(w_in_f[:, 5120:], ((0, 0), (0, LANES - SSM_HEADS)))
    ng = full['norm_g']

    def ffn_shards(layer):
        return [shards['ffn_w_gate'][layer], shards['ffn_w_up'][layer], shards['ffn_w_down'][layer]]

    def ffn_weights(got_gate, got_up, got_down):
        cols = lambda g: g.transpose(1, 0, 2).reshape(D_MODEL, D_FF)
        return _interleave_ff(cols(got_gate), cols(got_up)), got_down.reshape(D_FF, D_MODEL)

    w_gu, w_dn = [None, None], [None, None]
    causal = jnp.tril(jnp.ones((CHUNK, CHUNK), bool))
    wm = jnp.where(causal[None], gm_ws[0], 0.0).astype(MXU_DTYPE)
    wm_t = jnp.swapaxes(wm, 1, 2)
    bcol = jnp.pad(gm_bs[0].T, ((0, 0), (0, LANES - GM_HEADS)))
    conv_w8 = jnp.pad(full['conv_w'][0], ((0, 8 - SSM_CONV), (0, 0)))
    dtb, alog, dsk = _row128(dt_bias[0]), _row128(a_log[0]), _row128(d_skip[0])
    pool_b_f = full['pool_b'][0].reshape(1, D_MODEL)
    pool_s_f = full['pool_scale']

    def g_(layer, i):
        return ng[layer, i].reshape(1, D_MODEL)

    yn0 = _rn_fwd(h0, g_(0, 0), name="rn_fwd_0", out_dtype=MXU_DTYPE)
    proj, got = _mm(yn0, w_main, name="mm_in_proj", tm=2048,
                    ex=_Exchange([shards['w_out'][0]] + ffn_shards(0), ['gather'] * 4))
    w_out_f = got[0].reshape(-1, D_MODEL)
    w_gu[0], w_dn[0] = ffn_weights(*got[1:])
    dtr = _mm(yn0, w_dt, name="mm_in_proj_dt")
    pre = _conv_fwd(proj, conv_w8, conv_b, name="conv_fwd")
    cat = _gmlp_fwd(proj, gm_ln_g, gm_ln_b, wm, bcol, name="gmlp_fwd")
    cat, y_ssd, states, dt_ssd, acum_ssd = _ssd_fwd_grouped(pre, dtr, proj, dtb, alog, dsk, ssm_norm_g, cat,
                                                            name="ssd_fwd")
    o0 = _mm(cat, w_out_f, name="mm_out_proj", tm=1024, tn=1024)
    h1, yn1 = _resid_rn_fwd(h0, o0, g_(0, 1), g_(0, 2), name="resid_fwd_0a", next_dtype=MXU_DTYPE)
    (gu0, act0), got = _mm_swiglu(yn1, w_gu[0], name="mm_ffn0_gate_up",
                                  ex=_Exchange(ffn_shards(1) + [shards['pool_w'][0]], ['gather'] * 4))
    w_gu[1], w_dn[1] = ffn_weights(*got[:3])
    pool_w_f = got[3].transpose(1, 0, 2, 3).reshape(4, POOL_DIM, POOL_DIM)
    d0 = _mm(act0, w_dn[0], name="mm_ffn0_down", tm=1024, tn=1024)
    h2, yn2 = _resid_rn_fwd(h1, d0, g_(0, 3), g_(1, 0), name="resid_fwd_0b", next_dtype=F32)
    pm, pdiff = _pool_fwd(yn2, pool_w_f, pool_b_f, pool_s_f, name="pool_fwd")
    h3, yn3 = _resid_rn_fwd(h2, pm, g_(1, 1), g_(1, 2), name="resid_fwd_1a", next_dtype=MXU_DTYPE)
    gu1, act1 = _mm_swiglu(yn3, w_gu[1], name="mm_ffn1_gate_up")
    d1 = _mm(act1, w_dn[1], name="mm_ffn1_down", tm=1024, tn=1024)
    dh4, loss_part = _resid_loss(h3, d1, g_(1, 3), tgt, name="resid_loss")

    grads = {}
    recv = {'ffn_w_gate': [None, None], 'ffn_w_up': [None, None], 'ffn_w_down': [None, None]}
    dng = [[None] * 4 for _ in range(2)]

    def ffn_bwd(layer, dh, d_out, gu, act, yn, h_in):
        dd, dng[layer][3] = _resid_bwd_post(dh, d_out, g_(layer, 3), name=f"resid_bwd_post_{layer}b", out_dtype=MXU_DTYPE)
        dw_dn = _mm_tn(act, dd, name=f"mm_ffn{layer}_dw_down", out_dtype=MXU_DTYPE, tm=1408, tn=1024)
        dgu = _mm_dswiglu(dd, w_dn[layer].T, gu, name=f"mm_ffn{layer}_dact")
        dw_g_t, dw_u_t = _mm_tn_gate_up(dgu, yn, name=f"mm_ffn{layer}_dw_gate_up", out_dtype=MXU_DTYPE)
        dyn, got = _mm(dgu, w_gu[layer].T, name=f"mm_ffn{layer}_dyn", tm=512, tn=1024,
                       ex=_Exchange([dw_g_t, dw_u_t, dw_dn], ['rows'] * 3))
        recv['ffn_w_gate'][layer], recv['ffn_w_up'][layer], recv['ffn_w_down'][layer] = got
        dh_in, dng[layer][2] = _resid_bwd_pre(dh, [dyn], h_in, g_(layer, 2), name=f"resid_bwd_pre_{layer}b")
        return dh_in

    dh3 = ffn_bwd(1, dh4, d1, gu1, act1, yn3, h3)
    dpm, dng[1][1] = _resid_bwd_post(dh3, pm, g_(1, 1), name="resid_bwd_post_1a", out_dtype=F32)
    dyn2, d_pool_w, d_pool_b, d_pool_s = _pool_bwd(dpm, pdiff, pool_w_f, jnp.swapaxes(pool_w_f, 1, 2), pool_b_f, pool_s_f,
                                                   name="pool_bwd")
    dh2, dng[1][0] = _resid_bwd_pre(dh3, [dyn2], h2, g_(1, 0), name="resid_bwd_pre_1a")
    dh1 = ffn_bwd(0, dh2, d0, gu0, act0, yn1, h1)
    do0, dng[0][1] = _resid_bwd_post(dh1, o0, g_(0, 1), name="resid_bwd_post_0a", out_dtype=MXU_DTYPE)
    d_w_out = _mm_tn(cat, do0, name="mm_out_proj_dw", out_dtype=MXU_DTYPE, tn=1024)
    dcat, got = _mm(do0, w_out_f.T, name="mm_out_proj_dx", tm=2048, tn=1024, ex=_Exchange([d_w_out], ['rows']))
    recv['w_out'] = got[0]
    dproj, d_wm, d_bcol, d_ln_g, d_ln_b = _gmlp_bwd(proj, dcat, gm_ln_g, gm_ln_b, wm, wm_t, bcol, name="gmlp_bwd")
    dpre, dproj, ddtr, d_gn, d_vec = _ssd_bwd_grouped(pre, dtr, dt_ssd, acum_ssd, proj, y_ssd, states, dcat, dtb, alog,
                                                      dsk, ssm_norm_g, dproj, name="ssd_bwd")
    dproj, d_conv_w8, d_conv_b = _conv_bwd(dpre, proj, conv_w8, dproj, name="conv_bwd")
    d_w_main_t = _mm_tn(dproj, yn0, name="mm_in_proj_dw", out_dtype=MXU_DTYPE, tn=1024, shift=3)
    d_w_dt_t = _mm_tn(ddtr, yn0, name="mm_in_proj_dt_dw", out_dtype=MXU_DTYPE, tn=1024)
    d_w_in_t = jnp.concatenate([d_w_main_t, d_w_dt_t[:SSM_HEADS]], axis=0).reshape(N_DEV, -1, D_MODEL)
    dyn0, got = _mm(dproj, w_main.T, name="mm_in_proj_dx", tm=512, tn=1024,
                    ex=_Exchange([d_w_in_t], ['slots']))
    recv['w_in'] = got[0]
    dyn0_dt = _mm(ddtr, w_dt.T, name="mm_in_proj_dt_dx")
    grad_x, dng[0][0] = _resid_bwd_pre(dh1, [dyn0, dyn0_dt], h0, g_(0, 0), name="resid_bwd_pre_0a")

    grads['norm_g'] = jnp.stack([jnp.concatenate(dng[l], axis=0) for l in range(2)])
    grads['gm_ln_g'], grads['gm_ln_b'] = d_ln_g, d_ln_b
    grads['gm_ws'] = d_wm[None]
    grads['gm_bs'] = d_bcol[:, :GM_HEADS].T[None]
    grads['conv_w'] = d_conv_w8[None, :SSM_CONV]
    grads['conv_b'] = d_conv_b
    grads['dt_bias'] = d_vec[0:1, :SSM_HEADS]
    grads['a_log'] = d_vec[1:2, :SSM_HEADS] * (-jnp.exp(a_log))
    grads['d_skip'] = d_vec[2:3, :SSM_HEADS]
    grads['ssm_norm_g'] = d_gn
    grads['pool_w'] = d_pool_w[None]
    grads['pool_b'] = d_pool_b.reshape(1, 4, POOL_DIM)
    grads['pool_scale'] = d_pool_s
    return loss_part, grad_x, grads, recv
```

```python
import functools
import math

import jax
import jax.numpy as jnp
from jax import lax
from jax.experimental import pallas as pl
from jax.experimental.pallas import tpu as pltpu

F32 = jnp.float32
MXU_DTYPE = jnp.bfloat16

N_DEV = 8
D_MODEL = 1024
EPS = 1e-6
GM_HEADS = 4
GM_HEAD_DIM = 256
CHUNK = 128
SSM_HEADS = 16
SSM_GROUPS = 4
SSM_STATE = 128
SSM_CONV = 4
CONV_DIM = 2048
POOL_WINDOWS = (2, 4, 8, 16)
POOL_DIM = 256
D_FF = 2816
FF_TILE = 256
IN_MAIN = 5120
LANES = 128
CONV_HALO = 8
POOL_HALO = 16
ADAM_LR, ADAM_B1, ADAM_B2, ADAM_EPS, ADAM_WD, ADAM_STEP = 0.001, 0.9, 0.999, 1e-08, 0.01, 10

VMEM_LIMIT = 56 * 1024 * 1024
ROW_TILE = 512
MM_TM = 2048

WEIGHTS = ['norm_g', 'w_in', 'gm_ln_g', 'gm_ln_b', 'gm_ws', 'gm_bs', 'conv_w', 'conv_b', 'dt_bias', 'a_log',
           'd_skip', 'ssm_norm_g', 'w_out', 'pool_w', 'pool_b', 'pool_scale', 'ffn_w_gate', 'ffn_w_up', 'ffn_w_down']
SHARD_AXIS = {'norm_g': 2, 'w_in': 2, 'gm_ln_g': None, 'gm_ln_b': None, 'gm_ws': None, 'gm_bs': None, 'conv_w': 2,
              'conv_b': None, 'dt_bias': None, 'a_log': None, 'd_skip': None, 'ssm_norm_g': None, 'w_out': 1,
              'pool_w': 2, 'pool_b': 2, 'pool_scale': 1, 'ffn_w_gate': 2, 'ffn_w_up': 2, 'ffn_w_down': 1}
GATHER_BF16 = ['w_in', 'w_out', 'pool_w', 'ffn_w_gate', 'ffn_w_up', 'ffn_w_down']
GATHER_F32 = ['norm_g', 'conv_w', 'pool_b', 'pool_scale']
BIG_WEIGHTS = ['w_in', 'w_out', 'ffn_w_gate', 'ffn_w_up', 'ffn_w_down']


def _cparams(sem=None):
    return pltpu.CompilerParams(dimension_semantics=sem, vmem_limit_bytes=VMEM_LIMIT)


def _dot(a, b):
    return jnp.dot(a.astype(MXU_DTYPE), b.astype(MXU_DTYPE), preferred_element_type=F32)


def _dot_nt(a, b):
    return lax.dot_general(a.astype(MXU_DTYPE), b.astype(MXU_DTYPE), (((1,), (1,)), ((), ())),
                           preferred_element_type=F32)


def _dot_tn(a, b):
    return lax.dot_general(a.astype(MXU_DTYPE), b.astype(MXU_DTYPE), (((0,), (0,)), ((), ())),
                           preferred_element_type=F32)


def _dot_exact(a, b):
    return jnp.dot(a, b, precision=lax.Precision.HIGHEST, preferred_element_type=F32)


def _sigmoid(x):
    return 1.0 / (1.0 + jnp.exp(-x))


def _silu(x):
    return x * _sigmoid(x)


def _silu_grad(x):
    s = _sigmoid(x)
    return s * (1.0 + x * (1.0 - s))


_GELU_C = math.sqrt(2.0 / math.pi)


def _gelu(x):
    return 0.5 * x * (1.0 + jnp.tanh(_GELU_C * (x + 0.044715 * x * x * x)))


def _gelu_grad(x):
    t = jnp.tanh(_GELU_C * (x + 0.044715 * x * x * x))
    return 0.5 * (1.0 + t) + 0.5 * x * (1.0 - t * t) * _GELU_C * (1.0 + 3.0 * 0.044715 * x * x)


def _softplus(x):
    return jnp.maximum(x, 0.0) + jnp.log1p(jnp.exp(-jnp.abs(x)))


def _rms_scale(x):
    return lax.rsqrt(jnp.mean(x * x, axis=-1, keepdims=True) + EPS)


def _rms_bwd(dy, x, g):
    r = _rms_scale(x)
    xn = x * r
    dxn = dy * g
    dx = r * (dxn - xn * jnp.mean(dxn * xn, axis=-1, keepdims=True))
    return dx, dy * xn


def _colsum(x):
    return jnp.sum(x, axis=0, keepdims=True)


class _Exchange:
    def __init__(self, arrays, modes):
        self.arrays, self.modes, self.n = list(arrays), list(modes), len(arrays)
        self.blks = []
        for x, mode in zip(arrays, modes):
            if mode == 'gather':
                self.blks.append(tuple(x.shape))
            elif mode == 'slots':
                self.blks.append(tuple(x.shape[1:]))
            else:
                self.blks.append((x.shape[0] // N_DEV,) + tuple(x.shape[1:]))
        self.out_shape = [jax.ShapeDtypeStruct((N_DEV,) + blk, x.dtype) for x, blk in zip(arrays, self.blks)]
        self.in_specs = [pl.BlockSpec(memory_space=pl.ANY)] * self.n
        self.out_specs = [pl.BlockSpec(memory_space=pl.ANY) for _ in range(self.n)]
        n_sem = self.n * (N_DEV - 1)
        self.scratch = [pltpu.SemaphoreType.DMA((n_sem,)), pltpu.SemaphoreType.DMA((n_sem,)),
                        pltpu.SemaphoreType.DMA((self.n,))]

    def _copies(self, x_refs, out_refs, send_sems, recv_sems, local_sems, with_recvs):
        mx, my, mc = lax.axis_index("x"), lax.axis_index("y"), lax.axis_index("c")
        me = 4 * mx + 2 * my + mc

        def flip(v, bit):
            return 1 - v if bit else v

        def part(a, dev):
            if self.modes[a] == 'gather':
                return x_refs[a]
            if self.modes[a] == 'slots':
                return x_refs[a].at[dev]
            r = self.blks[a][0]
            return x_refs[a].at[pl.ds(pl.multiple_of(dev * r, 16), r)]

        sends, recvs, owns = [], [], []
        for k in (1, 2, 4, 6, 3, 5, 7):
            px, py, pc = flip(mx, (k >> 2) & 1), flip(my, (k >> 1) & 1), flip(mc, k & 1)
            peer = 4 * px + 2 * py + pc
            for a in range(self.n):
                sem = a * (N_DEV - 1) + k - 1
                sends.append(pltpu.make_async_remote_copy(
                    src_ref=part(a, peer), dst_ref=out_refs[a].at[me], send_sem=send_sems.at[sem],
                    recv_sem=recv_sems.at[sem], device_id=(px, py, pc), device_id_type=pl.DeviceIdType.MESH))
                if with_recvs:
                    recvs.append(pltpu.make_async_remote_copy(
                        src_ref=part(a, peer), dst_ref=out_refs[a].at[peer], send_sem=send_sems.at[sem],
                        recv_sem=recv_sems.at[sem], device_id=(px, py, pc), device_id_type=pl.DeviceIdType.MESH))
        for a in range(self.n):
            owns.append(pltpu.make_async_copy(part(a, me), out_refs[a].at[me], local_sems.at[a]))
        return sends, recvs, owns

    def start(self, *refs):
        sends, _, owns = self._copies(*refs, with_recvs=False)
        for cp in sends + owns:
            cp.start()

    def wait(self, *refs):
        sends, recvs, owns = self._copies(*refs, with_recvs=True)
        for cp in recvs:
            cp.wait_recv()
        for cp in sends:
            cp.wait_send()
        for cp in owns:
            cp.wait()


def _exchange(arrays, modes, *, name):
    ex = _Exchange(arrays, modes)

    def body(*refs):
        x_refs, out_refs, sems = refs[:ex.n], refs[ex.n:2 * ex.n], refs[2 * ex.n:]
        ex.start(x_refs, out_refs, *sems)
        ex.wait(x_refs, out_refs, *sems)

    return pl.pallas_call(
        body, name=name, out_shape=tuple(ex.out_shape), in_specs=ex.in_specs, out_specs=tuple(ex.out_specs),
        scratch_shapes=ex.scratch,
    )(*arrays)


def _gather_two_level(arrays, *, name):
    n = len(arrays)
    per = N_DEV - 1

    def body(*refs):
        x_refs, out_refs = refs[:n], refs[n:2 * n]
        send_sems, recv_sems, local_sems = refs[2 * n:]
        x, y, c = lax.axis_index("x"), lax.axis_index("y"), lax.axis_index("c")
        me, sibling = (x, y, c), (x, y, 1 - c)
        chips = [(1 - x, y), (x, 1 - y), (1 - x, 1 - y)]

        def copy(a, k, block, to, src=None):
            slot = out_refs[a].at[4 * block[0] + 2 * block[1] + block[2]]
            return pltpu.make_async_remote_copy(
                src_ref=slot if src is None else src, dst_ref=slot, send_sem=send_sems.at[a * per + k],
                recv_sem=recv_sems.at[a * per + k], device_id=to, device_id_type=pl.DeviceIdType.MESH)

        mines = [pltpu.make_async_copy(x_refs[a], out_refs[a].at[4 * x + 2 * y + c], local_sems.at[a]) for a in range(n)]
        firsts = []
        for a in range(n):
            firsts.append(copy(a, 0, me, sibling, src=x_refs[a]))
            firsts += [copy(a, 1 + j, me, (*chip, c), src=x_refs[a]) for j, chip in enumerate(chips)]
        for cp in mines + firsts:
            cp.start()
        passed = []
        for j, chip in enumerate(chips):
            for a in range(n):
                copy(a, 1 + j, (*chip, c), me).wait_recv()
                passed.append(copy(a, 4 + j, (*chip, c), sibling))
                passed[-1].start()
        for a in range(n):
            copy(a, 0, sibling, me).wait_recv()
            for j, chip in enumerate(chips):
                copy(a, 4 + j, (*chip, 1 - c), me).wait_recv()
        for cp in firsts + passed:
            cp.wait_send()
        for cp in mines:
            cp.wait()

    return pl.pallas_call(
        body, name=name,
        out_shape=tuple(jax.ShapeDtypeStruct((N_DEV,) + tuple(a.shape), a.dtype) for a in arrays),
        in_specs=[pl.BlockSpec(memory_space=pl.ANY)] * n,
        out_specs=tuple(pl.BlockSpec(memory_space=pl.ANY) for _ in range(n)),
        scratch_shapes=[pltpu.SemaphoreType.DMA((n * per,)), pltpu.SemaphoreType.DMA((n * per,)),
                        pltpu.SemaphoreType.DMA((n,))],
    )(*arrays)


def _hosted(body, n_in, n_out, n_scratch, grid, ex):
    def wrapped(*refs):
        ins, x_refs = refs[:n_in], refs[n_in:n_in + ex.n]
        outs = refs[n_in + ex.n:n_in + ex.n + n_out]
        xo_refs = refs[n_in + ex.n + n_out:n_in + 2 * ex.n + n_out]
        scr = refs[n_in + 2 * ex.n + n_out:n_in + 2 * ex.n + n_out + n_scratch]
        sems = refs[n_in + 2 * ex.n + n_out + n_scratch:]
        ids = [pl.program_id(d) for d in range(len(grid))]
        first = functools.reduce(jnp.logical_and, [i == 0 for i in ids])
        last = functools.reduce(jnp.logical_and, [i == g - 1 for i, g in zip(ids, grid)])

        @pl.when(first)
        def _():
            ex.start(x_refs, xo_refs, *sems)

        body(*ins, *outs, *scr)

        @pl.when(last)
        def _():
            ex.wait(x_refs, xo_refs, *sems)

    return wrapped


def _call(body, *, name, grid, inputs, in_specs, out_shape, out_specs, scratch, semantics, ex=None):
    if ex is None:
        return pl.pallas_call(
            body, name=name, grid=grid, out_shape=tuple(out_shape), in_specs=list(in_specs),
            out_specs=tuple(out_specs), scratch_shapes=list(scratch), compiler_params=_cparams(semantics))(*inputs)
    n_out = len(out_shape)
    res = pl.pallas_call(
        _hosted(body, len(inputs), n_out, len(scratch), grid, ex), name=name, grid=grid,
        out_shape=tuple(out_shape) + tuple(ex.out_shape), in_specs=list(in_specs) + ex.in_specs,
        out_specs=tuple(out_specs) + tuple(ex.out_specs), scratch_shapes=list(scratch) + ex.scratch,
        compiler_params=_cparams(("arbitrary",) * len(grid)))(*inputs, *ex.arrays)
    return res[:n_out], res[n_out:]


def _mm(a, b, *, name, out_dtype=F32, tm=MM_TM, tn=512, tk=None, ex=None):
    m, k = a.shape
    n = b.shape[1]
    tm, tn = min(tm, m), min(tn, n)
    tk = k if tk is None else tk
    nk = k // tk
    assert m % tm == 0 and n % tn == 0 and k % tk == 0

    def body(a_ref, b_ref, o_ref, acc_ref):
        kk = pl.program_id(2)
        part = _dot(a_ref[...], b_ref[...])
        if nk == 1:
            o_ref[...] = part.astype(out_dtype)
        else:
            @pl.when(kk == 0)
            def _():
                acc_ref[...] = part

            @pl.when(kk > 0)
            def _():
                acc_ref[...] += part

            @pl.when(kk == nk - 1)
            def _():
                o_ref[...] = acc_ref[...].astype(out_dtype)

    res = _call(
        body, name=name, grid=(m // tm, n // tn, nk), inputs=(a, b),
        in_specs=[pl.BlockSpec((tm, tk), lambda i, j, kk: (i, kk)), pl.BlockSpec((tk, tn), lambda i, j, kk: (kk, j))],
        out_shape=[jax.ShapeDtypeStruct((m, n), out_dtype)],
        out_specs=[pl.BlockSpec((tm, tn), lambda i, j, kk: (i, j))],
        scratch=[pltpu.VMEM((tm, tn) if nk > 1 else (8, LANES), F32)],
        semantics=("parallel", "parallel", "arbitrary"), ex=ex)
    return res[0] if ex is None else (res[0][0], res[1])


def _mm_tn(a, b, *, name, out_dtype=F32, tm=1024, tn=512, tk=1024, shift=0):
    t, m = a.shape
    n = b.shape[1]
    tm, tn, tk = min(tm, m), min(tn, n), min(tk, t)
    nk = t // tk
    nb = m // tm
    assert m % tm == 0 and n % tn == 0 and t % tk == 0

    def body(a_ref, b_ref, o_ref, acc_ref):
        kk = pl.program_id(2)
        part = _dot_tn(a_ref[...], b_ref[...])

        @pl.when(kk == 0)
        def _():
            acc_ref[...] = part

        @pl.when(kk > 0)
        def _():
            acc_ref[...] += part

        @pl.when(kk == nk - 1)
        def _():
            o_ref[...] = acc_ref[...].astype(out_dtype)

    return pl.pallas_call(
        body, name=name, grid=(nb, n // tn, nk),
        out_shape=jax.ShapeDtypeStruct((m, n), out_dtype),
        in_specs=[pl.BlockSpec((tk, tm), lambda i, j, kk: (kk, i)), pl.BlockSpec((tk, tn), lambda i, j, kk: (kk, j))],
        out_specs=pl.BlockSpec((tm, tn), lambda i, j, kk: ((i + shift) % nb, j)),
        scratch_shapes=[pltpu.VMEM((tm, tn), F32)],
        compiler_params=_cparams(("parallel", "parallel", "arbitrary")),
    )(a, b)


def _mm_tn_gate_up(dgu, yn, *, name, out_dtype, tk=2048):
    t, m = dgu.shape
    n = yn.shape[1]
    tk = min(tk, t)
    nk = t // tk
    nb = m // (2 * FF_TILE)

    def body(a_ref, b_ref, og_ref, ou_ref, acc_ref):
        kk = pl.program_id(1)
        part = _dot_tn(a_ref[...], b_ref[...])

        @pl.when(kk == 0)
        def _():
            acc_ref[...] = part

        @pl.when(kk > 0)
        def _():
            acc_ref[...] += part

        @pl.when(kk == nk - 1)
        def _():
            og_ref[...] = acc_ref[:FF_TILE, :].astype(out_dtype)
            ou_ref[...] = acc_ref[FF_TILE:, :].astype(out_dtype)

    out = jax.ShapeDtypeStruct((m // 2, n), out_dtype)
    o_spec = pl.BlockSpec((FF_TILE, n), lambda i, kk: (i, 0))
    return pl.pallas_call(
        body, name=name, grid=(nb, nk), out_shape=(out, out),
        in_specs=[pl.BlockSpec((tk, 2 * FF_TILE), lambda i, kk: (kk, i)), pl.BlockSpec((tk, n), lambda i, kk: (kk, 0))],
        out_specs=(o_spec, o_spec),
        scratch_shapes=[pltpu.VMEM((2 * FF_TILE, n), F32)],
        compiler_params=_cparams(("parallel", "arbitrary")),
    )(dgu, yn)


def _mm_swiglu(a, w_gu, *, name, tm=MM_TM, ex=None):
    m, k = a.shape
    n = w_gu.shape[1]
    nt = n // (2 * FF_TILE)
    tm = min(tm, m)

    def body(a_ref, b_ref, gu_ref, act_ref):
        gu = _dot(a_ref[...], b_ref[...])
        gu_ref[...] = gu.astype(MXU_DTYPE)
        act_ref[...] = (_silu(gu[:, :FF_TILE]) * gu[:, FF_TILE:]).astype(MXU_DTYPE)

    return _call(
        body, name=name, grid=(m // tm, nt), inputs=(a, w_gu),
        in_specs=[pl.BlockSpec((tm, k), lambda i, j: (i, 0)), pl.BlockSpec((k, 2 * FF_TILE), lambda i, j: (0, j))],
        out_shape=[jax.ShapeDtypeStruct((m, n), MXU_DTYPE), jax.ShapeDtypeStruct((m, n // 2), MXU_DTYPE)],
        out_specs=[pl.BlockSpec((tm, 2 * FF_TILE), lambda i, j: (i, j)), pl.BlockSpec((tm, FF_TILE), lambda i, j: (i, j))],
        scratch=[], semantics=("parallel", "parallel"), ex=ex)


def _mm_dswiglu(dd, w_down_t, gu, *, name, tm=MM_TM):
    m, k = dd.shape
    n = gu.shape[1]
    nt = n // (2 * FF_TILE)
    tm = min(tm, m)

    def body(d_ref, w_ref, gu_ref, o_ref):
        dact = _dot(d_ref[...], w_ref[...])
        gate, up = gu_ref[:, :FF_TILE].astype(F32), gu_ref[:, FF_TILE:].astype(F32)
        o_ref[:, :FF_TILE] = (dact * up * _silu_grad(gate)).astype(MXU_DTYPE)
        o_ref[:, FF_TILE:] = (dact * _silu(gate)).astype(MXU_DTYPE)

    return pl.pallas_call(
        body, name=name, grid=(m // tm, nt),
        out_shape=jax.ShapeDtypeStruct((m, n), MXU_DTYPE),
        in_specs=[pl.BlockSpec((tm, k), lambda i, j: (i, 0)), pl.BlockSpec((k, FF_TILE), lambda i, j: (0, j)),
                  pl.BlockSpec((tm, 2 * FF_TILE), lambda i, j: (i, j))],
        out_specs=pl.BlockSpec((tm, 2 * FF_TILE), lambda i, j: (i, j)),
        compiler_params=_cparams(("parallel", "parallel")),
    )(dd, w_down_t, gu)


def _row_spec(width, tr=ROW_TILE):
    return pl.BlockSpec((tr, width), lambda i: (i, 0))


def _vec_spec(width, rows=1):
    return pl.BlockSpec((rows, width), lambda i: (0, 0))


def _rn_fwd(h, g, *, name, out_dtype):
    rows, d = h.shape

    def body(h_ref, g_ref, o_ref):
        x = h_ref[...]
        o_ref[...] = (x * _rms_scale(x) * g_ref[...]).astype(out_dtype)

    return pl.pallas_call(
        body, name=name, grid=(rows // ROW_TILE,),
        out_shape=jax.ShapeDtypeStruct((rows, d), out_dtype),
        in_specs=[_row_spec(d), _vec_spec(d)], out_specs=_row_spec(d),
        compiler_params=_cparams(("parallel",)),
    )(h, g)


def _resid_rn_fwd(h_in, o, g_post, g_next, *, name, next_dtype):
    rows, d = h_in.shape

    def body(h_ref, o_ref, gp_ref, gn_ref, ho_ref, yn_ref):
        ov = o_ref[...]
        h = h_ref[...] + ov * _rms_scale(ov) * gp_ref[...]
        ho_ref[...] = h
        yn_ref[...] = (h * _rms_scale(h) * gn_ref[...]).astype(next_dtype)

    return pl.pallas_call(
        body, name=name, grid=(rows // ROW_TILE,),
        out_shape=(jax.ShapeDtypeStruct((rows, d), F32), jax.ShapeDtypeStruct((rows, d), next_dtype)),
        in_specs=[_row_spec(d), _row_spec(d), _vec_spec(d), _vec_spec(d)],
        out_specs=(_row_spec(d), _row_spec(d)),
        compiler_params=_cparams(("parallel",)),
    )(h_in, o, g_post, g_next)


def _resid_loss(h_in, o, g_post, target, *, name):
    rows, d = h_in.shape

    def body(h_ref, o_ref, gp_ref, t_ref, dh_ref, loss_ref, do_ref, dg_ref):
        ov = o_ref[...]
        gp = gp_ref[...]
        err = h_ref[...] + ov * _rms_scale(ov) * gp - t_ref[...]
        dh = err * (1.0 / d)
        dh_ref[...] = dh
        do, dg = _rms_bwd(dh, ov, gp)
        do_ref[...] = do.astype(MXU_DTYPE)

        @pl.when(pl.program_id(0) == 0)
        def _():
            loss_ref[...] = jnp.zeros_like(loss_ref)
            dg_ref[...] = jnp.zeros_like(dg_ref)

        loss_ref[...] += 0.5 * jnp.sum(jnp.mean(err * err, axis=-1, keepdims=True), axis=0, keepdims=True)
        dg_ref[...] += _colsum(dg)

    return pl.pallas_call(
        body, name=name, grid=(rows // ROW_TILE,),
        out_shape=(jax.ShapeDtypeStruct((rows, d), F32), jax.ShapeDtypeStruct((1, 1), F32),
                   jax.ShapeDtypeStruct((rows, d), MXU_DTYPE), jax.ShapeDtypeStruct((1, d), F32)),
        in_specs=[_row_spec(d), _row_spec(d), _vec_spec(d), _row_spec(d)],
        out_specs=(_row_spec(d), pl.BlockSpec((1, 1), lambda i: (0, 0)), _row_spec(d), _vec_spec(d)),
        compiler_params=_cparams(("arbitrary",)),
    )(h_in, o, g_post, target)


def _resid_bwd_pre_post(dh, dyn_list, h_in, g_pre, o_prev, g_post_prev, *, name, out_dtype):
    rows, d = dh.shape
    n_dyn = len(dyn_list)

    def body(*refs):
        dh_ref, dyn_refs = refs[0], refs[1:1 + n_dyn]
        h_ref, g_ref, o_ref, gp_ref, out_ref, do_ref, dg_ref, dgp_ref = refs[1 + n_dyn:]
        dyn = dyn_refs[0][...]
        for r in dyn_refs[1:]:
            dyn = dyn + r[...]
        dx, dg = _rms_bwd(dyn, h_ref[...], g_ref[...])
        dh_in = dh_ref[...] + dx
        out_ref[...] = dh_in
        do, dgp = _rms_bwd(dh_in, o_ref[...], gp_ref[...])
        do_ref[...] = do.astype(out_dtype)

        @pl.when(pl.program_id(0) == 0)
        def _():
            dg_ref[...] = jnp.zeros_like(dg_ref)
            dgp_ref[...] = jnp.zeros_like(dgp_ref)

        dg_ref[...] += _colsum(dg)
        dgp_ref[...] += _colsum(dgp)

    return pl.pallas_call(
        body, name=name, grid=(rows // ROW_TILE,),
        out_shape=(jax.ShapeDtypeStruct((rows, d), F32), jax.ShapeDtypeStruct((rows, d), out_dtype),
                   jax.ShapeDtypeStruct((1, d), F32), jax.ShapeDtypeStruct((1, d), F32)),
        in_specs=[_row_spec(d)] + [_row_spec(d)] * n_dyn + [_row_spec(d), _vec_spec(d), _row_spec(d), _vec_spec(d)],
        out_specs=(_row_spec(d), _row_spec(d), _vec_spec(d), _vec_spec(d)),
        compiler_params=_cparams(("arbitrary",)),
    )(dh, *dyn_list, h_in, g_pre, o_prev, g_post_prev)


def _resid_bwd_post(dh, o, g_post, *, name, out_dtype):
    rows, d = dh.shape

    def body(dh_ref, o_ref, g_ref, do_ref, dg_ref):
        do, dg = _rms_bwd(dh_ref[...], o_ref[...], g_ref[...])
        do_ref[...] = do.astype(out_dtype)

        @pl.when(pl.program_id(0) == 0)
        def _():
            dg_ref[...] = jnp.zeros_like(dg_ref)

        dg_ref[...] += _colsum(dg)

    return pl.pallas_call(
        body, name=name, grid=(rows // ROW_TILE,),
        out_shape=(jax.ShapeDtypeStruct((rows, d), out_dtype), jax.ShapeDtypeStruct((1, d), F32)),
        in_specs=[_row_spec(d), _row_spec(d), _vec_spec(d)],
        out_specs=(_row_spec(d), _vec_spec(d)),
        compiler_params=_cparams(("arbitrary",)),
    )(dh, o, g_post)


def _resid_bwd_pre(dh, dyn_list, h_in, g_pre, *, name):
    rows, d = dh.shape
    n_dyn = len(dyn_list)

    def body(*refs):
        dh_ref, dyn_refs, h_ref, g_ref, out_ref, dg_ref = refs[0], refs[1:1 + n_dyn], *refs[1 + n_dyn:]
        dyn = dyn_refs[0][...]
        for r in dyn_refs[1:]:
            dyn = dyn + r[...]
        dx, dg = _rms_bwd(dyn, h_ref[...], g_ref[...])
        out_ref[...] = dh_ref[...] + dx

        @pl.when(pl.program_id(0) == 0)
        def _():
            dg_ref[...] = jnp.zeros_like(dg_ref)

        dg_ref[...] += _colsum(dg)

    return pl.pallas_call(
        body, name=name, grid=(rows // ROW_TILE,),
        out_shape=(jax.ShapeDtypeStruct((rows, d), F32), jax.ShapeDtypeStruct((1, d), F32)),
        in_specs=[_row_spec(d)] + [_row_spec(d)] * n_dyn + [_row_spec(d), _vec_spec(d)],
        out_specs=(_row_spec(d), _vec_spec(d)),
        compiler_params=_cparams(("arbitrary",)),
    )(dh, *dyn_list, h_in, g_pre)


def _layer_norm_stats(x):
    mu = jnp.mean(x, axis=-1, keepdims=True)
    xc = x - mu
    rstd = lax.rsqrt(jnp.mean(xc * xc, axis=-1, keepdims=True) + EPS)
    return xc * rstd, rstd


def _gmlp_fwd(proj, ln_g, ln_b, wm, bcol, *, name):
    rows = proj.shape[0]
    tr = ROW_TILE

    def body(u_ref, v_ref, lg_ref, lb_ref, wm_ref, bc_ref, ya_ref):
        vhat, _ = _layer_norm_stats(_gelu(v_ref[...]))
        vl = (vhat * lg_ref[...] + lb_ref[...]).astype(MXU_DTYPE)
        gu = _gelu(u_ref[...])
        bc = bc_ref[...]
        for c in range(tr // CHUNK):
            rs = slice(c * CHUNK, (c + 1) * CHUNK)
            for h in range(GM_HEADS):
                cs = slice(h * GM_HEAD_DIM, (h + 1) * GM_HEAD_DIM)
                mixed = _dot(wm_ref[h], vl[rs, cs]) + bc[:, h:h + 1]
                ya_ref[rs, cs] = (gu[rs, cs] * mixed).astype(MXU_DTYPE)

    return pl.pallas_call(
        body, name=name, grid=(rows // tr,),
        out_shape=jax.ShapeDtypeStruct((rows, 2 * D_MODEL), MXU_DTYPE),
        in_specs=[pl.BlockSpec((tr, D_MODEL), lambda i: (i, 2)), pl.BlockSpec((tr, D_MODEL), lambda i: (i, 3)),
                  _vec_spec(D_MODEL), _vec_spec(D_MODEL),
                  pl.BlockSpec((GM_HEADS, CHUNK, CHUNK), lambda i: (0, 0, 0)), _vec_spec(LANES, CHUNK)],
        out_specs=_row_spec(D_MODEL, tr),
        compiler_params=_cparams(("parallel",)),
    )(proj, proj, ln_g, ln_b, wm, bcol)


def _gmlp_bwd(proj, dcat, ln_g, ln_b, wm, wm_t, bcol, *, name):
    rows = proj.shape[0]
    tr = ROW_TILE

    def body(u_ref, v_ref, dy_ref, lg_ref, lb_ref, wm_ref, wmt_ref, bc_ref,
             duv_ref, dwm_ref, dbc_ref, dlg_ref, dlb_ref, dvl_scr):
        @pl.when(pl.program_id(0) == 0)
        def _():
            dwm_ref[...] = jnp.zeros_like(dwm_ref)
            dbc_ref[...] = jnp.zeros_like(dbc_ref)
            dlg_ref[...] = jnp.zeros_like(dlg_ref)
            dlb_ref[...] = jnp.zeros_like(dlb_ref)

        u, v = u_ref[...], v_ref[...]
        gv = _gelu(v)
        vhat, rstd = _layer_norm_stats(gv)
        lg = lg_ref[...]
        vl = (vhat * lg + lb_ref[...]).astype(MXU_DTYPE)
        gu = _gelu(u)
        dy = dy_ref[...]
        bc = bc_ref[...]
        row = lax.broadcasted_iota(jnp.int32, (CHUNK, CHUNK), 0)
        lane = lax.broadcasted_iota(jnp.int32, (CHUNK, CHUNK), 1)
        causal = lane <= row
        dbc = jnp.zeros((CHUNK, LANES), F32)
        for c in range(tr // CHUNK):
            rs = slice(c * CHUNK, (c + 1) * CHUNK)
            for h in range(GM_HEADS):
                cs = slice(h * GM_HEAD_DIM, (h + 1) * GM_HEAD_DIM)
                vl_h = vl[rs, cs]
                mixed = _dot(wm_ref[h], vl_h) + bc[:, h:h + 1]
                dy_h = dy[rs, cs]
                duv_ref[rs, cs] = (dy_h * mixed * _gelu_grad(u[rs, cs])).astype(MXU_DTYPE)
                dmixed = dy_h * gu[rs, cs]
                dwm_ref[h] += jnp.where(causal, _dot_nt(dmixed, vl_h), 0.0)
                dbc = dbc + jnp.where(lane == h, jnp.sum(dmixed, axis=1, keepdims=True), 0.0)
                dvl_scr[rs, cs] = _dot(wmt_ref[h], dmixed)
        dbc_ref[...] += dbc
        dvl = dvl_scr[...]
        dlg_ref[...] += _colsum(dvl * vhat)
        dlb_ref[...] += _colsum(dvl)
        dvh = dvl * lg
        dgv = rstd * (dvh - jnp.mean(dvh, axis=-1, keepdims=True) - vhat * jnp.mean(dvh * vhat, axis=-1, keepdims=True))
        duv_ref[:, D_MODEL:] = (dgv * _gelu_grad(v)).astype(MXU_DTYPE)

    return pl.pallas_call(
        body, name=name, grid=(rows // tr,),
        out_shape=(jax.ShapeDtypeStruct((rows, IN_MAIN), MXU_DTYPE),
                   jax.ShapeDtypeStruct((GM_HEADS, CHUNK, CHUNK), F32), jax.ShapeDtypeStruct((CHUNK, LANES), F32),
                   jax.ShapeDtypeStruct((1, D_MODEL), F32), jax.ShapeDtypeStruct((1, D_MODEL), F32)),
        in_specs=[pl.BlockSpec((tr, D_MODEL), lambda i: (i, 2)), pl.BlockSpec((tr, D_MODEL), lambda i: (i, 3)),
                  pl.BlockSpec((tr, D_MODEL), lambda i: (i, 0)), _vec_spec(D_MODEL), _vec_spec(D_MODEL),
                  pl.BlockSpec((GM_HEADS, CHUNK, CHUNK), lambda i: (0, 0, 0)),
                  pl.BlockSpec((GM_HEADS, CHUNK, CHUNK), lambda i: (0, 0, 0)), _vec_spec(LANES, CHUNK)],
        out_specs=(pl.BlockSpec((tr, 2 * D_MODEL), lambda i: (i, 1)),
                   pl.BlockSpec((GM_HEADS, CHUNK, CHUNK), lambda i: (0, 0, 0)), _vec_spec(LANES, CHUNK),
                   _vec_spec(D_MODEL), _vec_spec(D_MODEL)),
        scratch_shapes=[pltpu.VMEM((tr, D_MODEL), F32)],
        compiler_params=_cparams(("arbitrary",)),
    )(proj, proj, dcat, ln_g, ln_b, wm, wm_t, bcol)


def _conv_fwd(proj, conv_w8, conv_b, *, name):
    rows = proj.shape[0]
    tr = ROW_TILE
    hb = tr // CONV_HALO

    def body(x_ref, prev_ref, w_ref, b_ref, pre_ref, buf):
        first = pl.program_id(0) == 0
        buf[pl.ds(0, CONV_HALO), :] = jnp.where(first, 0.0, prev_ref[...])
        buf[pl.ds(CONV_HALO, tr), :] = x_ref[...]
        ext = buf[...]
        acc = jnp.broadcast_to(b_ref[...], (tr, CONV_DIM))
        for k in range(SSM_CONV):
            s = SSM_CONV - 1 - k
            acc = acc + w_ref[k:k + 1, :] * (x_ref[...] if s == 0 else pltpu.roll(ext, s, axis=0)[CONV_HALO:])
        pre_ref[...] = acc

    return pl.pallas_call(
        body, name=name, grid=(rows // tr,),
        out_shape=jax.ShapeDtypeStruct((rows, CONV_DIM), F32),
        in_specs=[pl.BlockSpec((tr, CONV_DIM), lambda i: (i, 0)),
                  pl.BlockSpec((CONV_HALO, CONV_DIM), lambda i: (jnp.maximum(i * hb - 1, 0), 0)),
                  _vec_spec(CONV_DIM, 8), _vec_spec(CONV_DIM)],
        out_specs=_row_spec(CONV_DIM, tr),
        scratch_shapes=[pltpu.VMEM((tr + CONV_HALO, CONV_DIM), F32)],
        compiler_params=_cparams(("parallel",)),
    )(proj, proj, conv_w8, conv_b)


def _conv_bwd(dpre, proj, conv_w8, dproj, *, name):
    rows = proj.shape[0]
    tr = ROW_TILE
    hb = tr // CONV_HALO
    nblk = rows // tr

    def body(d_ref, dnext_ref, x_ref, w_ref, dproj_ref, dx_ref, dw_ref, db_ref, dbuf):
        i = pl.program_id(0)

        @pl.when(i == 0)
        def _():
            dw_ref[...] = jnp.zeros_like(dw_ref)
            db_ref[...] = jnp.zeros_like(db_ref)

        d = d_ref[...]
        x = x_ref[...]
        dbuf[pl.ds(0, tr), :] = d
        dbuf[pl.ds(tr, CONV_HALO), :] = jnp.where(i == nblk - 1, 0.0, dnext_ref[...])
        ext = dbuf[...]
        acc = jnp.zeros((tr, CONV_DIM), F32)
        for k in range(SSM_CONV):
            s = SSM_CONV - 1 - k
            shifted = d if s == 0 else pltpu.roll(ext, tr + CONV_HALO - s, axis=0)[:tr]
            acc = acc + w_ref[k:k + 1, :] * shifted
            dw_ref[k:k + 1, :] += _colsum(shifted * x)
        dx_ref[...] = acc.astype(MXU_DTYPE)
        db_ref[...] += _colsum(d)

    return pl.pallas_call(
        body, name=name, grid=(nblk,),
        out_shape=(jax.ShapeDtypeStruct((rows, IN_MAIN), MXU_DTYPE), jax.ShapeDtypeStruct((8, CONV_DIM), F32),
                   jax.ShapeDtypeStruct((1, CONV_DIM), F32)),
        in_specs=[_row_spec(CONV_DIM, tr),
                  pl.BlockSpec((CONV_HALO, CONV_DIM), lambda i: (jnp.minimum((i + 1) * hb, rows // CONV_HALO - 1), 0)),
                  pl.BlockSpec((tr, CONV_DIM), lambda i: (i, 0)),
                  _vec_spec(CONV_DIM, 8), pl.BlockSpec(memory_space=pl.ANY)],
        out_specs=(_row_spec(CONV_DIM, tr), _vec_spec(CONV_DIM, 8), _vec_spec(CONV_DIM)),
        scratch_shapes=[pltpu.VMEM((tr + CONV_HALO, CONV_DIM), F32)],
        input_output_aliases={4: 0},
        compiler_params=_cparams(("arbitrary",)),
    )(dpre, dpre, proj, conv_w8, dproj)


N_PAIRS = SSM_HEADS // 2


def _chunk_iotas():
    row = lax.broadcasted_iota(jnp.int32, (CHUNK, CHUNK), 0)
    lane = lax.broadcasted_iota(jnp.int32, (CHUNK, CHUNK), 1)
    return row, lane, lane <= row


def _silu_and_grad(x):
    s = _sigmoid(x)
    return x * s, s * (1.0 + x * (1.0 - s))


def _pair_select(lo, mat, ha):
    return jnp.where(lo, mat[:, ha:ha + 1], mat[:, ha + 1:ha + 2])


def _ssd_fwd(pre, dtr, proj, dtb, alog, dsk, gn, *, name):
    rows = pre.shape[0]
    nc = rows // CHUNK

    def body(pre_ref, dtr_ref, z_ref, dtb_ref, alog_ref, dsk_ref, gn_ref, yb_ref, y_ref, st_ref, dt_ref, acum_ref, s_scr):
        @pl.when(pl.program_id(0) == 0)
        def _():
            s_scr[...] = jnp.zeros_like(s_scr)

        row, lane, tril = _chunk_iotas()
        dt = _softplus(dtr_ref[...] + dtb_ref[...])
        acum = _dot_exact(tril.astype(F32), dt * (-jnp.exp(alog_ref[...])))
        dt_ref[...] = dt
        acum_ref[...] = acum
        acum_t = acum.T
        lo = lane < 64
        eacum = jnp.exp(acum)
        a_end = acum[CHUNK - 1:CHUNK, :]
        e_end = jnp.exp(a_end)
        dte_all = jnp.exp(a_end - acum)
        dsk_v = dsk_ref[...]
        for g in range(SSM_GROUPS):
            b_g = _silu(pre_ref[:, 1024 + SSM_STATE * g:1024 + SSM_STATE * (g + 1)]).astype(MXU_DTYPE)
            c_g = _silu(pre_ref[:, 1536 + SSM_STATE * g:1536 + SSM_STATE * (g + 1)]).astype(MXU_DTYPE)
            cb = _dot_nt(c_g, b_g)
            gated = []
            for jj in range(2):
                j = 2 * g + jj
                ha = 2 * j
                cs = slice(LANES * j, LANES * (j + 1))
                xs = _silu(pre_ref[:, cs])
                xdt = xs * _pair_select(lo, dt, ha)
                xdt_m = xdt.astype(MXU_DTYPE)
                y_heads = []
                for h in (ha, ha + 1):
                    dec = jnp.exp(jnp.where(tril, acum[:, h:h + 1] - acum_t[h:h + 1, :], -jnp.inf))
                    y_heads.append(_dot(cb * dec, xdt_m))
                s_prev = s_scr[j]
                st_ref[0, j] = s_prev
                y = jnp.where(lo, y_heads[0], y_heads[1])
                y = y + _dot_nt(c_g, s_prev) * _pair_select(lo, eacum, ha)
                y = y + _pair_select(lo, dsk_v, ha) * xs
                xw = xdt * _pair_select(lo, dte_all, ha)
                e_rows = jnp.where(row < 64, e_end[:, ha:ha + 1], e_end[:, ha + 1:ha + 2])
                s_scr[j] = e_rows * s_prev + _dot(xw.T, b_g)
                y_ref[:, cs] = y
                gated.append(y * _silu(z_ref[:, cs]))
            ms = (jnp.sum(gated[0] * gated[0], axis=1, keepdims=True)
                  + jnp.sum(gated[1] * gated[1], axis=1, keepdims=True)) * (1.0 / 256.0)
            r = lax.rsqrt(ms + EPS)
            for jj in range(2):
                cs = slice(LANES * (2 * g + jj), LANES * (2 * g + jj + 1))
                yb_ref[:, cs] = (gated[jj] * r * gn_ref[:, cs]).astype(MXU_DTYPE)

    return pl.pallas_call(
        body, name=name, grid=(nc,),
        out_shape=(jax.ShapeDtypeStruct((rows, D_MODEL), MXU_DTYPE), jax.ShapeDtypeStruct((rows, D_MODEL), F32),
                   jax.ShapeDtypeStruct((nc, N_PAIRS, LANES, SSM_STATE), F32),
                   jax.ShapeDtypeStruct((rows, LANES), F32), jax.ShapeDtypeStruct((rows, LANES), F32)),
        in_specs=[_row_spec(CONV_DIM, CHUNK), _row_spec(LANES, CHUNK), pl.BlockSpec((CHUNK, D_MODEL), lambda i: (i, 4)),
                  _vec_spec(LANES), _vec_spec(LANES), _vec_spec(LANES), _vec_spec(D_MODEL)],
        out_specs=(_row_spec(D_MODEL, CHUNK), _row_spec(D_MODEL, CHUNK),
                   pl.BlockSpec((1, N_PAIRS, LANES, SSM_STATE), lambda i: (i, 0, 0, 0)),
                   _row_spec(LANES, CHUNK), _row_spec(LANES, CHUNK)),
        scratch_shapes=[pltpu.VMEM((N_PAIRS, LANES, SSM_STATE), F32)],
        compiler_params=_cparams(("arbitrary",)),
    )(pre, dtr, proj, dtb, alog, dsk, gn)


def _ssd_bwd(pre, dtr, dt_saved, acum_saved, proj, y_saved, states, dcat, dtb, alog, dsk, gn, *, name):
    rows = pre.shape[0]
    nc = rows // CHUNK

    def rev(i):
        return nc - 1 - i

    def body(pre_ref, dtr_ref, dt_ref, acum_ref, z_ref, y_ref, st_ref, dyb_ref, dtb_ref, alog_ref, dsk_ref, gn_ref,
             dpre_ref, dz_ref, ddtr_ref, dgn_ref, dvec_ref, g_scr):
        @pl.when(pl.program_id(0) == 0)
        def _():
            g_scr[...] = jnp.zeros_like(g_scr)
            dgn_ref[...] = jnp.zeros_like(dgn_ref)
            dvec_ref[...] = jnp.zeros_like(dvec_ref)

        dtb = dtb_ref[...]
        dtr = dtr_ref[...]
        row, lane, tril = _chunk_iotas()
        dt, acum = dt_ref[...], acum_ref[...]
        a = -jnp.exp(alog_ref[...])
        acum_t = acum.T
        lo = lane < 64
        eacum = jnp.exp(acum)
        a_end = acum[CHUNK - 1:CHUNK, :]
        e_end = jnp.exp(a_end)
        dte_all = jnp.exp(a_end - acum)
        dsk_v = dsk_ref[...]
        zero = jnp.zeros((CHUNK, LANES), F32)
        dacum_c, dacum_r, ddt_c = zero, zero, zero
        d_aend = jnp.zeros((1, LANES), F32)
        d_dsk = jnp.zeros((1, LANES), F32)
        lane1 = lane[0:1, :]

        def put_col(acc, h, colvec):
            return acc + jnp.where(lane == h, colvec, 0.0)

        for g in range(SSM_GROUPS):
            gated, sz, dgh = [], [], []
            for jj in range(2):
                cs = slice(LANES * (2 * g + jj), LANES * (2 * g + jj + 1))
                sz.append(_silu_and_grad(z_ref[:, cs]))
                gated.append(y_ref[:, cs] * sz[jj][0])
                dgh.append(dyb_ref[:, cs] * gn_ref[:, cs])
            ms = (jnp.sum(gated[0] * gated[0], axis=1, keepdims=True)
                  + jnp.sum(gated[1] * gated[1], axis=1, keepdims=True)) * (1.0 / 256.0)
            r = lax.rsqrt(ms + EPS)
            proj_g = (jnp.sum(dgh[0] * gated[0], axis=1, keepdims=True)
                      + jnp.sum(dgh[1] * gated[1], axis=1, keepdims=True)) * (1.0 / 256.0)
            dys = []
            for jj in range(2):
                cs = slice(LANES * (2 * g + jj), LANES * (2 * g + jj + 1))
                dgn_ref[:, cs] += _colsum(dyb_ref[:, cs] * gated[jj] * r)
                dgated = r * dgh[jj] - gated[jj] * (r * r * r * proj_g)
                dys.append(dgated * sz[jj][0])
                dz_ref[:, cs] = (dgated * y_ref[:, cs] * sz[jj][1]).astype(MXU_DTYPE)

            b_f, b_grad = _silu_and_grad(pre_ref[:, 1024 + SSM_STATE * g:1024 + SSM_STATE * (g + 1)])
            c_f, c_grad = _silu_and_grad(pre_ref[:, 1536 + SSM_STATE * g:1536 + SSM_STATE * (g + 1)])
            b_g = b_f.astype(MXU_DTYPE)
            c_g = c_f.astype(MXU_DTYPE)
            cb = _dot_nt(c_g, b_g)
            dcb = zero
            db_g, dc_g = zero, zero
            for jj in range(2):
                j = 2 * g + jj
                ha = 2 * j
                cs = slice(LANES * j, LANES * (j + 1))
                xs, xs_grad = _silu_and_grad(pre_ref[:, cs])
                dtsel = _pair_select(lo, dt, ha)
                xdt = xs * dtsel
                xdt_m = xdt.astype(MXU_DTYPE)
                dyp = dys[jj]
                dyp_m = dyp.astype(MXU_DTYPE)
                s_prev = st_ref[0, j]
                g_next = g_scr[j]
                eac = _pair_select(lo, eacum, ha)
                dte = _pair_select(lo, dte_all, ha)
                yoff = _dot_nt(c_g, s_prev) * eac
                t_off = dyp * yoff
                dye = dyp * eac
                dc_g = dc_g + _dot(dye, s_prev)
                bg = _dot_nt(b_g, g_next)
                dxdt = bg * dte
                xw = xdt * dte
                db_g = db_g + _dot(xw, g_next)
                t_w = xw * bg
                gs = g_next * s_prev
                e_rows = jnp.where(row < 64, e_end[:, ha:ha + 1], e_end[:, ha + 1:ha + 2])
                g_scr[j] = e_rows * g_next + _dot(dye.T, c_g)
                dxdt_heads = []
                for hh, h in enumerate((ha, ha + 1)):
                    half = slice(64 * hh, 64 * (hh + 1))
                    dec = jnp.exp(jnp.where(tril, acum[:, h:h + 1] - acum_t[h:h + 1, :], -jnp.inf))
                    m_h = cb * dec
                    dy_h = jnp.where(lo if hh == 0 else jnp.logical_not(lo), dyp, 0.0)
                    dm = _dot_nt(dy_h, xdt_m)
                    dxdt_heads.append(_dot(m_h.T, dyp_m))
                    e_h = dm * m_h
                    dcb = dcb + dm * dec
                    w_col = jnp.sum(t_w[:, half], axis=1, keepdims=True)
                    col = (jnp.sum(e_h, axis=1, keepdims=True) + jnp.sum(t_off[:, half], axis=1, keepdims=True) - w_col)
                    dacum_c = put_col(dacum_c, h, col)
                    dacum_r = dacum_r + jnp.where(row == h, _colsum(e_h), 0.0)
                    d_end_h = jnp.sum(w_col, keepdims=True) + e_end[:, h:h + 1] * jnp.sum(gs[half, :], keepdims=True)
                    d_aend = d_aend + jnp.where(lane1 == h, d_end_h, 0.0)
                dxdt = dxdt + jnp.where(lo, dxdt_heads[0], dxdt_heads[1])
                dsel = _pair_select(lo, dsk_v, ha)
                dxs = dxdt * dtsel + dsel * dyp
                dpre_ref[:, cs] = dxs * xs_grad
                t_dt = dxdt * xs
                t_dk = dyp * xs
                for hh, h in enumerate((ha, ha + 1)):
                    half = slice(64 * hh, 64 * (hh + 1))
                    ddt_c = put_col(ddt_c, h, jnp.sum(t_dt[:, half], axis=1, keepdims=True))
                    d_dsk = d_dsk + jnp.where(lane1 == h, jnp.sum(t_dk[:, half], keepdims=True), 0.0)
            dc_g = dc_g + _dot(dcb, b_g)
            db_g = db_g + _dot(dcb.T, c_g)
            dpre_ref[:, 1024 + SSM_STATE * g:1024 + SSM_STATE * (g + 1)] = db_g * b_grad
            dpre_ref[:, 1536 + SSM_STATE * g:1536 + SSM_STATE * (g + 1)] = dc_g * c_grad

        dacum = dacum_c - dacum_r.T + jnp.where(row == CHUNK - 1, d_aend, 0.0)
        dda = _dot_exact((lane >= row).astype(F32), dacum)
        ddt = dda * a + ddt_c
        ddtr = ddt * _sigmoid(dtr + dtb)
        ddtr_ref[...] = ddtr.astype(MXU_DTYPE)
        dvec_ref[0:1, :] += _colsum(ddtr)
        dvec_ref[1:2, :] += _colsum(dda * dt)
        dvec_ref[2:3, :] += d_dsk

    return pl.pallas_call(
        body, name=name, grid=(nc,),
        out_shape=(jax.ShapeDtypeStruct((rows, CONV_DIM), F32), jax.ShapeDtypeStruct((rows, D_MODEL), MXU_DTYPE),
                   jax.ShapeDtypeStruct((rows, LANES), MXU_DTYPE), jax.ShapeDtypeStruct((1, D_MODEL), F32),
                   jax.ShapeDtypeStruct((8, LANES), F32)),
        in_specs=[pl.BlockSpec((CHUNK, CONV_DIM), lambda i: (rev(i), 0)), pl.BlockSpec((CHUNK, LANES), lambda i: (rev(i), 0)),
                  pl.BlockSpec((CHUNK, LANES), lambda i: (rev(i), 0)), pl.BlockSpec((CHUNK, LANES), lambda i: (rev(i), 0)),
                  pl.BlockSpec((CHUNK, D_MODEL), lambda i: (rev(i), 4)), pl.BlockSpec((CHUNK, D_MODEL), lambda i: (rev(i), 0)),
                  pl.BlockSpec((1, N_PAIRS, LANES, SSM_STATE), lambda i: (rev(i), 0, 0, 0)),
                  pl.BlockSpec((CHUNK, D_MODEL), lambda i: (rev(i), 1)),
                  _vec_spec(LANES), _vec_spec(LANES), _vec_spec(LANES), _vec_spec(D_MODEL)],
        out_specs=(pl.BlockSpec((CHUNK, CONV_DIM), lambda i: (rev(i), 0)), pl.BlockSpec((CHUNK, D_MODEL), lambda i: (rev(i), 0)),
                   pl.BlockSpec((CHUNK, LANES), lambda i: (rev(i), 0)), _vec_spec(D_MODEL), _vec_spec(LANES, 8)),
        scratch_shapes=[pltpu.VMEM((N_PAIRS, LANES, SSM_STATE), F32)],
        compiler_params=_cparams(("arbitrary",)),
    )(pre, dtr, dt_saved, acum_saved, proj, y_saved, states, dcat, dtb, alog, dsk, gn)


SSD_BWD_CHUNKS = 1
GROUP_DIM = D_MODEL // SSM_GROUPS
HEADS_PER_GROUP = SSM_HEADS // SSM_GROUPS
HEAD_DIM = GROUP_DIM // HEADS_PER_GROUP


def _split(x):
    hi = x.astype(MXU_DTYPE)
    return hi, (x - hi.astype(F32)).astype(MXU_DTYPE)


def _dot_split(x, sel):
    hi, lo = _split(x)
    return jnp.dot(hi, sel, preferred_element_type=F32) + jnp.dot(lo, sel, preferred_element_type=F32)


def _dot_split_rhs(sel, x):
    hi, lo = _split(x)
    return jnp.dot(sel, hi, preferred_element_type=F32) + jnp.dot(sel, lo, preferred_element_type=F32)


def _dot_split_tn(x, sel):
    hi, lo = _split(x)
    dims = (((0,), (0,)), ((), ()))
    return (lax.dot_general(hi, sel, dims, preferred_element_type=F32)
            + lax.dot_general(lo, sel, dims, preferred_element_type=F32))


def _split3(x):
    hi = x.astype(MXU_DTYPE)
    r = x - hi.astype(F32)
    mid = r.astype(MXU_DTYPE)
    return hi, mid, (r - mid.astype(F32)).astype(MXU_DTYPE)


def _head_selectors():
    h = lax.broadcasted_iota(jnp.int32, (LANES, D_MODEL), 0)
    p = lax.broadcasted_iota(jnp.int32, (LANES, D_MODEL), 1)
    sel_t = (h == p // HEAD_DIM).astype(MXU_DTYPE)
    return sel_t, sel_t.T


def _expand_heads(per_head, e_end, selt_ref, sel_ref):
    stacked = jnp.concatenate(per_head, axis=0)
    wide = _dot_split(stacked, selt_ref[...])
    n = per_head[0].shape[0]
    e_cols = jnp.broadcast_to(e_end, (LANES, LANES)).T
    tall = _dot_split_rhs(sel_ref[...], e_cols)
    return [wide[n * i:n * (i + 1)] for i in range(len(per_head))], tall


def _by_quarter(index, pieces):
    out = pieces[3]
    for q in (2, 1, 0):
        out = jnp.where(index == q, pieces[q], out)
    return out


def _ssd_fwd_grouped(pre, dtr, proj, dtb, alog, dsk, gn, cat, *, name):
    rows = pre.shape[0]
    nc = rows // CHUNK

    def body(pre_ref, dtr_ref, z_ref, dtb_ref, alog_ref, dsk_ref, gn_ref, cat_ref, selt_ref, sel_ref, yb_ref, y_ref,
             st_ref, dt_ref, acum_ref, s_scr):
        @pl.when(pl.program_id(0) == 0)
        def _():
            s_scr[...] = jnp.zeros_like(s_scr)

        row, lane, tril = _chunk_iotas()
        dt = _softplus(dtr_ref[...] + dtb_ref[...])
        acum = _dot_exact(tril.astype(F32), dt * (-jnp.exp(alog_ref[...])))
        dt_ref[...] = dt
        acum_ref[...] = acum
        acum_t = acum.T
        a_end = acum[CHUNK - 1:CHUNK, :]
        (dt_x, eacum_x, dte_x, dsk_x), e_rows_all = _expand_heads(
            [dt, jnp.exp(acum), jnp.exp(a_end - acum), jnp.broadcast_to(dsk_ref[...], (CHUNK, LANES))], jnp.exp(a_end),
            selt_ref, sel_ref)
        lane_q = lax.broadcasted_iota(jnp.int32, (CHUNK, GROUP_DIM), 1) // HEAD_DIM

        for g in range(SSM_GROUPS):
            cs = slice(GROUP_DIM * g, GROUP_DIM * (g + 1))
            b_g = _silu(pre_ref[:, 1024 + SSM_STATE * g:1024 + SSM_STATE * (g + 1)]).astype(MXU_DTYPE)
            c_g = _silu(pre_ref[:, 1536 + SSM_STATE * g:1536 + SSM_STATE * (g + 1)]).astype(MXU_DTYPE)
            cb = _dot_nt(c_g, b_g)
            xs = _silu(pre_ref[:, cs])
            xdt = xs * dt_x[:, cs]
            m_stack = jnp.concatenate(
                [(cb * jnp.exp(jnp.where(tril, acum[:, h:h + 1] - acum_t[h:h + 1, :], -jnp.inf))).astype(MXU_DTYPE)
                 for h in range(4 * g, 4 * g + 4)], axis=0)
            y_all = _dot(m_stack, xdt)
            y = _by_quarter(lane_q, [y_all[CHUNK * q:CHUNK * (q + 1)] for q in range(HEADS_PER_GROUP)])
            s_prev = s_scr[g]
            st_ref[0, g] = s_prev
            y = y + _dot_nt(c_g, s_prev) * eacum_x[:, cs] + dsk_x[:, cs] * xs
            xw = xdt * dte_x[:, cs]
            s_scr[g] = e_rows_all[cs, :] * s_prev + _dot(xw.T, b_g)
            y_ref[:, cs] = y
            gated = y * _silu(z_ref[:, cs])
            r = lax.rsqrt(jnp.mean(gated * gated, axis=1, keepdims=True) + EPS)
            yb_ref[:, cs] = (gated * r * gn_ref[:, cs]).astype(MXU_DTYPE)

    return pl.pallas_call(
        body, name=name, grid=(nc,),
        out_shape=(jax.ShapeDtypeStruct((rows, 2 * D_MODEL), MXU_DTYPE), jax.ShapeDtypeStruct((rows, D_MODEL), F32),
                   jax.ShapeDtypeStruct((nc, SSM_GROUPS, GROUP_DIM, SSM_STATE), F32),
                   jax.ShapeDtypeStruct((rows, LANES), F32), jax.ShapeDtypeStruct((rows, LANES), F32)),
        in_specs=[_row_spec(CONV_DIM, CHUNK), _row_spec(LANES, CHUNK), pl.BlockSpec((CHUNK, D_MODEL), lambda i: (i, 4)),
                  _vec_spec(LANES), _vec_spec(LANES), _vec_spec(LANES), _vec_spec(D_MODEL),
                  pl.BlockSpec(memory_space=pl.ANY), _vec_spec(D_MODEL, LANES), _vec_spec(LANES, D_MODEL)],
        out_specs=(pl.BlockSpec((CHUNK, D_MODEL), lambda i: (i, 1)), _row_spec(D_MODEL, CHUNK),
                   pl.BlockSpec((1, SSM_GROUPS, GROUP_DIM, SSM_STATE), lambda i: (i, 0, 0, 0)),
                   _row_spec(LANES, CHUNK), _row_spec(LANES, CHUNK)),
        scratch_shapes=[pltpu.VMEM((SSM_GROUPS, GROUP_DIM, SSM_STATE), F32)],
        input_output_aliases={7: 0},
        compiler_params=_cparams(("arbitrary",)),
    )(pre, dtr, proj, dtb, alog, dsk, gn, cat, *_head_selectors())


def _ssd_bwd_grouped(pre, dtr, dt_saved, acum_saved, proj, y_saved, states, dcat, dtb, alog, dsk, gn, dproj, *, name):
    rows = pre.shape[0]
    cps = SSD_BWD_CHUNKS
    tr = CHUNK * cps
    nsteps = rows // tr

    def rev(i):
        return nsteps - 1 - i

    def body(pre_ref, dtr_ref, dt_ref, acum_ref, z_ref, y_ref, st_ref, dyb_ref, dtb_ref, alog_ref, dsk_ref, gn_ref,
             dproj_ref, selt_ref, sel_ref, dpre_ref, dz_ref, ddtr_ref, dgn_ref, dvec_ref, g_scr):
        @pl.when(pl.program_id(0) == 0)
        def _():
            g_scr[...] = jnp.zeros_like(g_scr)
            dgn_ref[...] = jnp.zeros_like(dgn_ref)
            dvec_ref[...] = jnp.zeros_like(dvec_ref)

        for cc in reversed(range(cps)):
            at = lambda ref: ref.at[pl.ds(cc * CHUNK, CHUNK)]
            chunk(at(pre_ref), at(dtr_ref), at(dt_ref), at(acum_ref), at(z_ref), at(y_ref), st_ref.at[cc], at(dyb_ref),
                  dtb_ref, alog_ref, dsk_ref, gn_ref, selt_ref, sel_ref, at(dpre_ref), at(dz_ref), at(ddtr_ref), dgn_ref,
                  dvec_ref, g_scr)

    def chunk(pre_ref, dtr_ref, dt_ref, acum_ref, z_ref, y_ref, st_ref, dyb_ref, dtb_ref, alog_ref, dsk_ref, gn_ref,
              selt_ref, sel_ref, dpre_ref, dz_ref, ddtr_ref, dgn_ref, dvec_ref, g_scr):
        row, lane, tril = _chunk_iotas()
        triu = lane >= row
        dt, acum = dt_ref[...], acum_ref[...]
        a = -jnp.exp(alog_ref[...])
        acum_t = acum.T
        a_end = acum[CHUNK - 1:CHUNK, :]
        e_end = jnp.exp(a_end)
        (dt_x, eacum_x, dte_x, dsk_x), e_rows_all = _expand_heads(
            [dt, jnp.exp(acum), jnp.exp(a_end - acum), jnp.broadcast_to(dsk_ref[...], (CHUNK, LANES))], e_end,
            selt_ref, sel_ref)
        lane_q = lax.broadcasted_iota(jnp.int32, (CHUNK, GROUP_DIM), 1) // HEAD_DIM
        zero = jnp.zeros((CHUNK, LANES), F32)
        dacum_c, dacum_r, ddt_c = zero, zero, zero
        d_aend = jnp.zeros((1, LANES), F32)
        d_dsk = jnp.zeros((1, LANES), F32)
        iota = lambda shape, dim: lax.broadcasted_iota(jnp.int32, shape, dim)
        q256, lane_256 = iota((GROUP_DIM, LANES), 0) // HEAD_DIM, iota((GROUP_DIM, LANES), 1)
        q512, lane_512 = iota((4 * CHUNK, LANES), 0) // CHUNK, iota((4 * CHUNK, LANES), 1)
        row_512t, q512t = iota((LANES, 4 * CHUNK), 0), iota((LANES, 4 * CHUNK), 1) // CHUNK

        for g in range(SSM_GROUPS):
            cs = slice(GROUP_DIM * g, GROUP_DIM * (g + 1))
            yv = y_ref[:, cs]
            sz, sz_grad = _silu_and_grad(z_ref[:, cs])
            gated = yv * sz
            dyb = dyb_ref[:, cs]
            dgh = dyb * gn_ref[:, cs]
            r = lax.rsqrt(jnp.mean(gated * gated, axis=1, keepdims=True) + EPS)
            dgn_ref[:, cs] += _colsum(dyb * gated * r)
            dgated = r * dgh - gated * (r * r * r * jnp.mean(dgh * gated, axis=1, keepdims=True))
            dy = dgated * sz
            dz_ref[:, cs] = (dgated * yv * sz_grad).astype(MXU_DTYPE)

            b_f, b_grad = _silu_and_grad(pre_ref[:, 1024 + SSM_STATE * g:1024 + SSM_STATE * (g + 1)])
            c_f, c_grad = _silu_and_grad(pre_ref[:, 1536 + SSM_STATE * g:1536 + SSM_STATE * (g + 1)])
            b_g, c_g = b_f.astype(MXU_DTYPE), c_f.astype(MXU_DTYPE)
            xs, xs_grad = _silu_and_grad(pre_ref[:, cs])
            dtq = dt_x[:, cs]
            xdt = xs * dtq
            xdt_m = xdt.astype(MXU_DTYPE)
            dy_m = dy.astype(MXU_DTYPE)
            s_prev = st_ref[g]
            g_next = g_scr[g]
            eacq, dteq = eacum_x[:, cs], dte_x[:, cs]
            t_off = dy * (_dot_nt(c_g, s_prev) * eacq)
            dye = dy * eacq
            dc_g = _dot(dye, s_prev)
            bg = _dot_nt(b_g, g_next)
            xw = xdt * dteq
            db_g = _dot(xw, g_next)
            t_w = xw * bg
            gs = g_next * s_prev
            g_scr[g] = e_rows_all[cs, :] * g_next + _dot(dye.T, c_g)
            cb = _dot_nt(c_g, b_g)
            cb_t = cb.T
            heads = range(4 * g, 4 * g + 4)
            decs = [jnp.exp(jnp.where(tril, acum[:, h:h + 1] - acum_t[h:h + 1, :], -jnp.inf)) for h in heads]
            mt_stack = jnp.concatenate(
                [(cb_t * jnp.exp(jnp.where(triu, acum_t[h:h + 1, :] - acum[:, h:h + 1], -jnp.inf))).astype(MXU_DTYPE)
                 for h in heads], axis=0)
            dy_stack = jnp.concatenate([jnp.where(lane_q == q, dy, 0.0).astype(MXU_DTYPE)
                                        for q in range(HEADS_PER_GROUP)], axis=0)
            dm_all = _dot_nt(dy_stack, xdt_m)
            dx_all = _dot(mt_stack, dy_m)
            dxdt = bg * dteq + _by_quarter(lane_q, [dx_all[CHUNK * q:CHUNK * (q + 1)] for q in range(HEADS_PER_GROUP)])
            t_dt = dxdt * xs
            t_dk = dy * xs
            dec_stack = jnp.concatenate(decs, axis=0)
            dm_dec = dm_all * dec_stack
            e_all = dm_dec * jnp.concatenate([cb] * HEADS_PER_GROUP, axis=0)
            dcb = functools.reduce(jnp.add, [dm_dec[CHUNK * q:CHUNK * (q + 1)] for q in range(HEADS_PER_GROUP)])
            one = jnp.ones((), MXU_DTYPE)
            sel_lanes = jnp.where(q256 + 4 * g == lane_256, one, 0)
            sel_rows = jnp.where(q512 + 4 * g == lane_512, one, 0)
            sel_rows_t = jnp.where(row_512t == q512t + 4 * g, one, 0)
            e_lanes = jnp.concatenate([e_all[CHUNK * q:CHUNK * (q + 1)] for q in range(HEADS_PER_GROUP)], axis=1)
            w_heads = _dot(t_w, sel_lanes)
            dacum_c = dacum_c + _dot(e_lanes, sel_rows) + _dot(t_off, sel_lanes) - w_heads
            dacum_r = dacum_r + _dot(sel_rows_t, e_all)
            ddt_c = ddt_c + _dot(t_dt, sel_lanes)
            d_aend = d_aend + _colsum(w_heads) + e_end * _colsum(_dot_tn(gs, sel_lanes))
            d_dsk = d_dsk + _colsum(_dot(t_dk, sel_lanes))
            dpre_ref[:, cs] = (dxdt * dtq + dsk_x[:, cs] * dy) * xs_grad
            dc_g = dc_g + _dot(dcb, b_g)
            db_g = db_g + _dot(dcb.T, c_g)
            dpre_ref[:, 1024 + SSM_STATE * g:1024 + SSM_STATE * (g + 1)] = db_g * b_grad
            dpre_ref[:, 1536 + SSM_STATE * g:1536 + SSM_STATE * (g + 1)] = dc_g * c_grad

        dacum = dacum_c - dacum_r.T + jnp.where(row == CHUNK - 1, d_aend, 0.0)
        dda = _dot_exact(triu.astype(F32), dacum)
        ddtr = (dda * a + ddt_c) * _sigmoid(dtr_ref[...] + dtb_ref[...])
        ddtr_ref[...] = ddtr.astype(MXU_DTYPE)
        dvec_ref[0:1, :] += _colsum(ddtr)
        dvec_ref[1:2, :] += _colsum(dda * dt)
        dvec_ref[2:3, :] += d_dsk

    return pl.pallas_call(
        body, name=name, grid=(nsteps,),
        out_shape=(jax.ShapeDtypeStruct((rows, CONV_DIM), F32), jax.ShapeDtypeStruct((rows, IN_MAIN), MXU_DTYPE),
                   jax.ShapeDtypeStruct((rows, LANES), MXU_DTYPE), jax.ShapeDtypeStruct((1, D_MODEL), F32),
                   jax.ShapeDtypeStruct((8, LANES), F32)),
        in_specs=[pl.BlockSpec((tr, CONV_DIM), lambda i: (rev(i), 0)), pl.BlockSpec((tr, LANES), lambda i: (rev(i), 0)),
                  pl.BlockSpec((tr, LANES), lambda i: (rev(i), 0)), pl.BlockSpec((tr, LANES), lambda i: (rev(i), 0)),
                  pl.BlockSpec((tr, D_MODEL), lambda i: (rev(i), 4)), pl.BlockSpec((tr, D_MODEL), lambda i: (rev(i), 0)),
                  pl.BlockSpec((cps, SSM_GROUPS, GROUP_DIM, SSM_STATE), lambda i: (rev(i), 0, 0, 0)),
                  pl.BlockSpec((tr, D_MODEL), lambda i: (rev(i), 1)),
                  _vec_spec(LANES), _vec_spec(LANES), _vec_spec(LANES), _vec_spec(D_MODEL),
                  pl.BlockSpec(memory_space=pl.ANY), _vec_spec(D_MODEL, LANES), _vec_spec(LANES, D_MODEL)],
        out_specs=(pl.BlockSpec((tr, CONV_DIM), lambda i: (rev(i), 0)), pl.BlockSpec((tr, D_MODEL), lambda i: (rev(i), 4)),
                   pl.BlockSpec((tr, LANES), lambda i: (rev(i), 0)), _vec_spec(D_MODEL), _vec_spec(LANES, 8)),
        scratch_shapes=[pltpu.VMEM((SSM_GROUPS, GROUP_DIM, SSM_STATE), F32)],
        input_output_aliases={12: 1},
        compiler_params=_cparams(("arbitrary",)),
    )(pre, dtr, dt_saved, acum_saved, proj, y_saved, states, dcat, dtb, alog, dsk, gn, dproj, *_head_selectors())


def _pool_counts(first_row, n_rows, win):
    t = first_row + lax.broadcasted_iota(jnp.int32, (n_rows, POOL_DIM), 0)
    return jnp.minimum(t + 1, win).astype(F32)


def _pool_fwd(yn, pool_w, pool_b, pool_scale, *, name):
    rows = yn.shape[0]
    tr = ROW_TILE
    hb = tr // POOL_HALO

    def body(y_ref, prev_ref, w_ref, b_ref, s_ref, pm_ref, diff_ref, buf):
        i = pl.program_id(0)
        buf[pl.ds(0, POOL_HALO), :] = jnp.where(i == 0, 0.0, prev_ref[...])
        buf[pl.ds(POOL_HALO, tr), :] = y_ref[...]
        for g, win in enumerate(POOL_WINDOWS):
            cs = slice(POOL_DIM * g, POOL_DIM * (g + 1))
            acc = buf[pl.ds(POOL_HALO, tr), cs]
            for s in range(1, win):
                acc = acc + buf[pl.ds(POOL_HALO - s, tr), cs]
            diff = (acc / _pool_counts(i * tr, tr, win) - y_ref[:, cs]).astype(MXU_DTYPE)
            diff_ref[:, cs] = diff
            pm_ref[:, cs] = (_dot(diff, w_ref[g]) + b_ref[:, cs]) * s_ref[:, cs]

    return pl.pallas_call(
        body, name=name, grid=(rows // tr,),
        out_shape=(jax.ShapeDtypeStruct((rows, D_MODEL), F32), jax.ShapeDtypeStruct((rows, D_MODEL), MXU_DTYPE)),
        in_specs=[_row_spec(D_MODEL, tr),
                  pl.BlockSpec((POOL_HALO, D_MODEL), lambda i: (jnp.maximum(i * hb - 1, 0), 0)),
                  pl.BlockSpec((4, POOL_DIM, POOL_DIM), lambda i: (0, 0, 0)), _vec_spec(D_MODEL), _vec_spec(D_MODEL)],
        out_specs=(_row_spec(D_MODEL, tr), _row_spec(D_MODEL, tr)),
        scratch_shapes=[pltpu.VMEM((tr + POOL_HALO, D_MODEL), F32)],
        compiler_params=_cparams(("parallel",)),
    )(yn, yn, pool_w, pool_b, pool_scale)


def _pool_bwd(dpm, diff, pool_w, pool_w_t, pool_b, pool_scale, *, name):
    rows = dpm.shape[0]
    tr = ROW_TILE
    hb = tr // POOL_HALO
    nblk = rows // tr

    def body(d_ref, dnext_ref, diff_ref, w_ref, wt_ref, b_ref, s_ref, dy_ref, dw_ref, db_ref, ds_ref, ebuf):
        i = pl.program_id(0)

        @pl.when(i == 0)
        def _():
            dw_ref[...] = jnp.zeros_like(dw_ref)
            db_ref[...] = jnp.zeros_like(db_ref)
            ds_ref[...] = jnp.zeros_like(ds_ref)

        last = i == nblk - 1
        for g, win in enumerate(POOL_WINDOWS):
            cs = slice(POOL_DIM * g, POOL_DIM * (g + 1))
            d = d_ref[:, cs]
            diff = diff_ref[:, cs]
            out_pre = _dot(diff, w_ref[g]) + b_ref[:, cs]
            ds_ref[:, cs] += _colsum(d * out_pre)
            dout = d * s_ref[:, cs]
            db_ref[:, cs] += _colsum(dout)
            dw_ref[g] += _dot_tn(diff, dout)
            ddiff = _dot(dout, wt_ref[g])
            ddiff_next = _dot(jnp.where(last, 0.0, dnext_ref[:, cs]) * s_ref[:, cs], wt_ref[g])
            ebuf[pl.ds(0, tr), cs] = ddiff / _pool_counts(i * tr, tr, win)
            ebuf[pl.ds(tr, POOL_HALO), cs] = ddiff_next / _pool_counts((i + 1) * tr, POOL_HALO, win)
            acc = -ddiff
            for s in range(win):
                acc = acc + ebuf[pl.ds(s, tr), cs]
            dy_ref[:, cs] = acc

    return pl.pallas_call(
        body, name=name, grid=(nblk,),
        out_shape=(jax.ShapeDtypeStruct((rows, D_MODEL), F32), jax.ShapeDtypeStruct((4, POOL_DIM, POOL_DIM), F32),
                   jax.ShapeDtypeStruct((1, D_MODEL), F32), jax.ShapeDtypeStruct((1, D_MODEL), F32)),
        in_specs=[_row_spec(D_MODEL, tr),
                  pl.BlockSpec((POOL_HALO, D_MODEL), lambda i: (jnp.minimum((i + 1) * hb, rows // POOL_HALO - 1), 0)),
                  _row_spec(D_MODEL, tr),
                  pl.BlockSpec((4, POOL_DIM, POOL_DIM), lambda i: (0, 0, 0)),
                  pl.BlockSpec((4, POOL_DIM, POOL_DIM), lambda i: (0, 0, 0)), _vec_spec(D_MODEL), _vec_spec(D_MODEL)],
        out_specs=(_row_spec(D_MODEL, tr), pl.BlockSpec((4, POOL_DIM, POOL_DIM), lambda i: (0, 0, 0)),
                   _vec_spec(D_MODEL), _vec_spec(D_MODEL)),
        scratch_shapes=[pltpu.VMEM((tr + POOL_HALO, D_MODEL), F32)],
        compiler_params=_cparams(("arbitrary",)),
    )(dpm, dpm, diff, pool_w, pool_w_t, pool_b, pool_scale)


def _row_tile(rows, cap, step):
    best = rows
    for t in range(step, min(rows, cap) + 1, step):
        if rows % t == 0:
            best = t
    return best if best <= cap else rows


def _sum8(recv, *, name):
    _, r, c = recv.shape
    step = 8 if recv.dtype == F32 else 16

    def body(r_ref, g_ref):
        g = r_ref[0].astype(F32)
        for j in range(1, N_DEV):
            g = g + r_ref[j].astype(F32)
        g_ref[...] = g

    if r % step == 0:
        tr = _row_tile(r, 256, step)
        grid, in_spec, out_spec = (r // tr,), pl.BlockSpec((N_DEV, tr, c), lambda i: (0, i, 0)), pl.BlockSpec((tr, c), lambda i: (i, 0))
    else:
        tc = 256
        grid, in_spec, out_spec = (c // tc,), pl.BlockSpec((N_DEV, r, tc), lambda i: (0, 0, i)), pl.BlockSpec((r, tc), lambda i: (0, i))
    return pl.pallas_call(
        body, name=name, grid=grid, out_shape=jax.ShapeDtypeStruct((r, c), F32),
        in_specs=[in_spec], out_specs=out_spec, compiler_params=_cparams(("parallel",)),
    )(recv)


def _adamw(g, w, m, v, *, name):
    rows, cols = w.shape
    tr = _row_tile(rows, max(8, (256 * 1024) // cols // 8 * 8), 8)
    c1 = 1.0 / (1.0 - ADAM_B1 ** ADAM_STEP)
    c2 = 1.0 / (1.0 - ADAM_B2 ** ADAM_STEP)

    def body(g_ref, w_ref, m_ref, v_ref, d_ref, mo_ref, vo_ref):
        g = g_ref[...]
        m_new = ADAM_B1 * m_ref[...] + (1.0 - ADAM_B1) * g
        v_new = ADAM_B2 * v_ref[...] + (1.0 - ADAM_B2) * (g * g)
        mo_ref[...] = m_new
        vo_ref[...] = v_new
        d_ref[...] = -ADAM_LR * ((m_new * c1) / (jnp.sqrt(v_new * c2) + ADAM_EPS) + ADAM_WD * w_ref[...])

    spec = pl.BlockSpec((tr, cols), lambda i: (i, 0))
    return pl.pallas_call(
        body, name=name, grid=(rows // tr,),
        out_shape=tuple(jax.ShapeDtypeStruct((rows, cols), F32) for _ in range(3)),
        in_specs=[spec] * 4, out_specs=(spec, spec, spec),
        compiler_params=_cparams(("parallel",)),
    )(g, w, m, v)


def _pad_rows(flat, mult):
    n = flat.shape[-1]
    pad = (-n) % mult
    if pad:
        flat = jnp.pad(flat, [(0, 0)] * (flat.ndim - 1) + [(0, pad)])
    return flat


def _pack_blocks(blocks, row_mult):
    flat = jnp.concatenate([_pad_rows(b.reshape(-1), LANES) for b in blocks])
    return _pad_rows(flat, LANES * row_mult).reshape(-1, LANES)


def _block_sizes(blocks):
    return [-(-math.prod(b.shape) // LANES) * LANES for b in blocks]


def _unpack_blocks(slab, like, lead=()):
    flat = slab.reshape(lead + (-1,))
    out, off = [], 0
    for b, size in zip(like, _block_sizes(like)):
        n = math.prod(b.shape)
        out.append(flat[..., off:off + n].reshape(lead + tuple(b.shape)))
        off += size
    return out


def _join_shards(gathered, axis):
    return jnp.concatenate([gathered[j] for j in range(N_DEV)], axis=axis)


def _split_shards(full, axis):
    return jnp.stack(jnp.split(full, N_DEV, axis=axis))


def _interleave_ff(w_gate, w_up):
    k = w_gate.shape[0]
    nt = D_FF // FF_TILE
    return jnp.stack([w_gate.reshape(k, nt, FF_TILE), w_up.reshape(k, nt, FF_TILE)], axis=2).reshape(k, 2 * D_FF)


def _row128(vec):
    return jnp.pad(vec.reshape(1, -1), ((0, 0), (0, LANES - vec.shape[-1])))


def kernel(x, norm_g, w_in, gm_ln_g, gm_ln_b, gm_ws, gm_bs, conv_w, conv_b, dt_bias, a_log, d_skip, ssm_norm_g, w_out, pool_w, pool_b, pool_scale, ffn_w_gate, ffn_w_up, ffn_w_down, loss_target, m_norm_g, m_w_in, m_gm_ln_g, m_gm_ln_b, m_gm_ws, m_gm_bs, m_conv_w, m_conv_b, m_dt_bias, m_a_log, m_d_skip, m_ssm_norm_g, m_w_out, m_pool_w, m_pool_b, m_pool_scale, m_ffn_w_gate, m_ffn_w_up, m_ffn_w_down, v_norm_g, v_w_in, v_gm_ln_g, v_gm_ln_b, v_gm_ws, v_gm_bs, v_conv_w, v_conv_b, v_dt_bias, v_a_log, v_d_skip, v_ssm_norm_g, v_w_out, v_pool_w, v_pool_b, v_pool_scale, v_ffn_w_gate, v_ffn_w_up, v_ffn_w_down):
    w_loc = dict(norm_g=norm_g, w_in=w_in, gm_ln_g=gm_ln_g, gm_ln_b=gm_ln_b, gm_ws=gm_ws, gm_bs=gm_bs, conv_w=conv_w,
                 conv_b=conv_b, dt_bias=dt_bias, a_log=a_log, d_skip=d_skip, ssm_norm_g=ssm_norm_g, w_out=w_out,
                 pool_w=pool_w, pool_b=pool_b, pool_scale=pool_scale, ffn_w_gate=ffn_w_gate, ffn_w_up=ffn_w_up,
                 ffn_w_down=ffn_w_down)
    m_loc = dict(zip(WEIGHTS, [m_norm_g, m_w_in, m_gm_ln_g, m_gm_ln_b, m_gm_ws, m_gm_bs, m_conv_w, m_conv_b, m_dt_bias,
                               m_a_log, m_d_skip, m_ssm_norm_g, m_w_out, m_pool_w, m_pool_b, m_pool_scale,
                               m_ffn_w_gate, m_ffn_w_up, m_ffn_w_down]))
    v_loc = dict(zip(WEIGHTS, [v_norm_g, v_w_in, v_gm_ln_g, v_gm_ln_b, v_gm_ws, v_gm_bs, v_conv_w, v_conv_b, v_dt_bias,
                               v_a_log, v_d_skip, v_ssm_norm_g, v_w_out, v_pool_w, v_pool_b, v_pool_scale,
                               v_ffn_w_gate, v_ffn_w_up, v_ffn_w_down]))

    small_blocks = [w_loc[n] for n in GATHER_F32]
    got = _gather_two_level([w_in[0].astype(MXU_DTYPE), _pack_blocks(small_blocks, 8)], name="gather_first")
    full = {n: w_loc[n] for n in WEIGHTS if SHARD_AXIS[n] is None}
    full['w_in'] = got[0].transpose(1, 0, 2).reshape(1, D_MODEL, -1)
    for n, g in zip(GATHER_F32, _unpack_blocks(got[1], small_blocks, (N_DEV,))):
        full[n] = _join_shards(g, SHARD_AXIS[n])
    shards = {n: w_loc[n].astype(MXU_DTYPE) for n in ('w_out', 'ffn_w_gate', 'ffn_w_up', 'ffn_w_down', 'pool_w')}

    loss_part, grad_x, grads, recv = _local_step(x[0], loss_target[0], full, shards)

    small = [n for n in WEIGHTS if n not in BIG_WEIGHTS]
    like = [w_loc[n] for n in small]
    slots = []
    for n in small:
        ax = SHARD_AXIS[n]
        g = grads[n].astype(F32)
        sh = _split_shards(g, ax) if ax is not None else jnp.broadcast_to(g[None], (N_DEV,) + g.shape)
        slots.append(_pad_rows(sh.reshape(N_DEV, -1), LANES))
    send_small = _pad_rows(jnp.concatenate(slots, axis=1), LANES * 8).reshape(N_DEV, -1, LANES)
    recv_small, = _exchange([send_small], ['slots'], name="exchange_last")

    g_small = _sum8(recv_small, name="sum_small")
    g_own = dict(zip(small, _unpack_blocks(g_small, like)))
    g_own['w_in'] = _sum8(recv['w_in'], name="sum_w_in").T[None]
    g_own['w_out'] = _sum8(recv['w_out'], name="sum_w_out")[None]
    g_own['ffn_w_gate'] = jnp.stack([_sum8(recv['ffn_w_gate'][l], name=f"sum_ffn{l}_gate").T for l in range(2)])
    g_own['ffn_w_up'] = jnp.stack([_sum8(recv['ffn_w_up'][l], name=f"sum_ffn{l}_up").T for l in range(2)])
    g_own['ffn_w_down'] = jnp.stack([_sum8(recv['ffn_w_down'][l], name=f"sum_ffn{l}_down") for l in range(2)])

    delta, m_new, v_new = {}, {}, {}
    pk = lambda d: _pack_blocks([d[n] for n in small], 8)
    d_s, m_s, v_s = _adamw(g_small, pk(w_loc), pk(m_loc), pk(v_loc), name="adamw_small")
    for dst, slab in ((delta, d_s), (m_new, m_s), (v_new, v_s)):
        dst.update(zip(small, _unpack_blocks(slab, like)))
    for n in BIG_WEIGHTS:
        shape = w_loc[n].shape
        two_d = lambda t: t.reshape(-1, shape[-1])
        res = _adamw(two_d(g_own[n]), two_d(w_loc[n]), two_d(m_loc[n]), two_d(v_loc[n]), name=f"adamw_{n}")
        delta[n], m_new[n], v_new[n] = (t.reshape(shape) for t in res)

    loss = lax.psum(loss_part[0, 0], ("x", "y", "c"))
    outs = [d[n] for d in (g_own, delta, m_new, v_new) for n in WEIGHTS]
    return (loss, grad_x[None], *outs)


def _local_step(h0, tgt, full, shards):
    gm_ln_g, gm_ln_b, gm_ws, gm_bs = full['gm_ln_g'], full['gm_ln_b'], full['gm_ws'], full['gm_bs']
    conv_b, dt_bias, a_log, d_skip, ssm_norm_g = (full['conv_b'], full['dt_bias'], full['a_log'], full['d_skip'],
                                                  full['ssm_norm_g'])
    w_in_f = full['w_in'][0]
    w_main = jnp.concatenate([w_in_f[:, 3072:5120], w_in_f[:, :3072]], axis=1)
    w_dt = jnp.pad(w_in_f[:, 5120:], ((0, 0), (0, LANES - SSM_HEADS)))
    ng = full['norm_g']

    def ffn_shards(layer):
        return [shards['ffn_w_gate'][layer], shards['ffn_w_up'][layer], shards['ffn_w_down'][layer]]

    def ffn_weights(got_gate, got_up, got_down):
        cols = lambda g: g.transpose(1, 0, 2).reshape(D_MODEL, D_FF)
        return _interleave_ff(cols(got_gate), cols(got_up)), got_down.reshape(D_FF, D_MODEL)

    w_gu, w_dn = [None, None], [None, None]
    causal = jnp.tril(jnp.ones((CHUNK, CHUNK), bool))
    wm = jnp.where(causal[None], gm_ws[0], 0.0).astype(MXU_DTYPE)
    wm_t = jnp.swapaxes(wm, 1, 2)
    bcol = jnp.pad(gm_bs[0].T, ((0, 0), (0, LANES - GM_HEADS)))
    conv_w8 = jnp.pad(full['conv_w'][0], ((0, 8 - SSM_CONV), (0, 0)))
    dtb, alog, dsk = _row128(dt_bias[0]), _row128(a_log[0]), _row128(d_skip[0])
    pool_b_f = full['pool_b'][0].reshape(1, D_MODEL)
    pool_s_f = full['pool_scale']

    def g_(layer, i):
        return ng[layer, i].reshape(1, D_MODEL)

    yn0 = _rn_fwd(h0, g_(0, 0), name="rn_fwd_0", out_dtype=MXU_DTYPE)
    proj, got = _mm(yn0, w_main, name="mm_in_proj", tm=2048,
                    ex=_Exchange([shards['w_out'][0]] + ffn_shards(0), ['gather'] * 4))
    w_out_f = got[0].reshape(-1, D_MODEL)
    w_gu[0], w_dn[0] = ffn_weights(*got[1:])
    dtr = _mm(yn0, w_dt, name="mm_in_proj_dt")
    pre = _conv_fwd(proj, conv_w8, conv_b, name="conv_fwd")
    cat = _gmlp_fwd(proj, gm_ln_g, gm_ln_b, wm, bcol, name="gmlp_fwd")
    cat, y_ssd, states, dt_ssd, acum_ssd = _ssd_fwd_grouped(pre, dtr, proj, dtb, alog, dsk, ssm_norm_g, cat,
                                                            name="ssd_fwd")
    o0 = _mm(cat, w_out_f, name="mm_out_proj", tm=1024, tn=1024)
    h1, yn1 = _resid_rn_fwd(h0, o0, g_(0, 1), g_(0, 2), name="resid_fwd_0a", next_dtype=MXU_DTYPE)
    (gu0, act0), got = _mm_swiglu(yn1, w_gu[0], name="mm_ffn0_gate_up",
                                  ex=_Exchange(ffn_shards(1) + [shards['pool_w'][0]], ['gather'] * 4))
    w_gu[1], w_dn[1] = ffn_weights(*got[:3])
    pool_w_f = got[3].transpose(1, 0, 2, 3).reshape(4, POOL_DIM, POOL_DIM)
    d0 = _mm(act0, w_dn[0], name="mm_ffn0_down", tm=1024, tn=1024)
    h2, yn2 = _resid_rn_fwd(h1, d0, g_(0, 3), g_(1, 0), name="resid_fwd_0b", next_dtype=F32)
    pm, pdiff = _pool_fwd(yn2, pool_w_f, pool_b_f, pool_s_f, name="pool_fwd")
    h3, yn3 = _resid_rn_fwd(h2, pm, g_(1, 1), g_(1, 2), name="resid_fwd_1a", next_dtype=MXU_DTYPE)
    gu1, act1 = _mm_swiglu(yn3, w_gu[1], name="mm_ffn1_gate_up")
    d1 = _mm(act1, w_dn[1], name="mm_ffn1_down", tm=1024, tn=1024)
    grads = {}
    recv = {'ffn_w_gate': [None, None], 'ffn_w_up': [None, None], 'ffn_w_down': [None, None]}
    dng = [[None] * 4 for _ in range(2)]
    dh4, loss_part, dd1, dng[1][3] = _resid_loss(h3, d1, g_(1, 3), tgt, name="resid_loss")

    def ffn_bwd(layer, dd, gu, act, yn):
        dw_dn = _mm_tn(act, dd, name=f"mm_ffn{layer}_dw_down", out_dtype=MXU_DTYPE, tm=1408, tn=1024)
        dgu = _mm_dswiglu(dd, w_dn[layer].T, gu, name=f"mm_ffn{layer}_dact")
        dw_g_t, dw_u_t = _mm_tn_gate_up(dgu, yn, name=f"mm_ffn{layer}_dw_gate_up", out_dtype=MXU_DTYPE)
        dyn, got = _mm(dgu, w_gu[layer].T, name=f"mm_ffn{layer}_dyn", tm=512, tn=1024,
                       ex=_Exchange([dw_g_t, dw_u_t, dw_dn], ['rows'] * 3))
        recv['ffn_w_gate'][layer], recv['ffn_w_up'][layer], recv['ffn_w_down'][layer] = got
        return dyn

    dyn3 = ffn_bwd(1, dd1, gu1, act1, yn3)
    dh3, dpm, dng[1][2], dng[1][1] = _resid_bwd_pre_post(dh4, [dyn3], h3, g_(1, 2), pm, g_(1, 1), name="resid_bwd_1b_1a",
                                                         out_dtype=F32)
    dyn2, d_pool_w, d_pool_b, d_pool_s = _pool_bwd(dpm, pdiff, pool_w_f, jnp.swapaxes(pool_w_f, 1, 2), pool_b_f, pool_s_f,
                                                   name="pool_bwd")
    dh2, dd0, dng[1][0], dng[0][3] = _resid_bwd_pre_post(dh3, [dyn2], h2, g_(1, 0), d0, g_(0, 3), name="resid_bwd_1a_0b",
                                                         out_dtype=MXU_DTYPE)
    dyn1 = ffn_bwd(0, dd0, gu0, act0, yn1)
    dh1, do0, dng[0][2], dng[0][1] = _resid_bwd_pre_post(dh2, [dyn1], h1, g_(0, 2), o0, g_(0, 1), name="resid_bwd_0b_0a",
                                                         out_dtype=MXU_DTYPE)
    d_w_out =_mm_tn(cat, do0, name="mm_out_proj_dw", out_dtype=MXU_DTYPE, tn=1024)
    dcat, got = _mm(do0, w_out_f.T, name="mm_out_proj_dx", tm=2048, tn=1024, ex=_Exchange([d_w_out], ['rows']))
    recv['w_out'] = got[0]
    dproj, d_wm, d_bcol, d_ln_g, d_ln_b = _gmlp_bwd(proj, dcat, gm_ln_g, gm_ln_b, wm, wm_t, bcol, name="gmlp_bwd")
    dpre, dproj, ddtr, d_gn, d_vec = _ssd_bwd_grouped(pre, dtr, dt_ssd, acum_ssd, proj, y_ssd, states, dcat, dtb, alog,
                                                      dsk, ssm_norm_g, dproj, name="ssd_bwd")
    dproj, d_conv_w8, d_conv_b = _conv_bwd(dpre, proj, conv_w8, dproj, name="conv_bwd")
    d_w_main_t = _mm_tn(dproj, yn0, name="mm_in_proj_dw", out_dtype=MXU_DTYPE, tn=1024, shift=3)
    d_w_dt_t = _mm_tn(ddtr, yn0, name="mm_in_proj_dt_dw", out_dtype=MXU_DTYPE, tn=1024)
    d_w_in_t = jnp.concatenate([d_w_main_t, d_w_dt_t[:SSM_HEADS]], axis=0).reshape(N_DEV, -1, D_MODEL)
    dyn0, got = _mm(dproj, w_main.T, name="mm_in_proj_dx", tm=512, tn=1024,
                    ex=_Exchange([d_w_in_t], ['slots']))
    recv['w_in'] = got[0]
    dyn0_dt = _mm(ddtr, w_dt.T, name="mm_in_proj_dt_dx")
    grad_x, dng[0][0] = _resid_bwd_pre(dh1, [dyn0, dyn0_dt], h0, g_(0, 0), name="resid_bwd_pre_0a")

    grads['norm_g'] = jnp.stack([jnp.concatenate(dng[l], axis=0) for l in range(2)])
    grads['gm_ln_g'], grads['gm_ln_b'] = d_ln_g, d_ln_b
    grads['gm_ws'] = d_wm[None]
    grads['gm_bs'] = d_bcol[:, :GM_HEADS].T[None]
    grads['conv_w'] = d_conv_w8[None, :SSM_CONV]
    grads['conv_b'] = d_conv_b
    grads['dt_bias'] = d_vec[0:1, :SSM_HEADS]
    grads['a_log'] = d_vec[1:2, :SSM_HEADS] * (-jnp.exp(a_log))
    grads['d_skip'] = d_vec[2:3, :SSM_HEADS]
    grads['ssm_norm_g'] = d_gn
    grads['pool_w'] = d_pool_w[None]
    grads['pool_b'] = d_pool_b.reshape(1, 4, POOL_DIM)
    grads['pool_scale'] = d_pool_s
    return loss_part, grad_x, grads, recv
```

```python
import functools
import math

import jax
import jax.numpy as jnp
from jax import lax
from jax.experimental import pallas as pl
from jax.experimental.pallas import tpu as pltpu

F32 = jnp.float32
MXU_DTYPE = jnp.bfloat16

N_DEV = 8
D_MODEL = 1024
EPS = 1e-6
GM_HEADS = 4
GM_HEAD_DIM = 256
CHUNK = 128
SSM_HEADS = 16
SSM_GROUPS = 4
SSM_STATE = 128
SSM_CONV = 4
CONV_DIM = 2048
POOL_WINDOWS = (2, 4, 8, 16)
POOL_DIM = 256
D_FF = 2816
FF_TILE = 256
IN_MAIN = 5120
LANES = 128
CONV_HALO = 8
POOL_HALO = 16
ADAM_LR, ADAM_B1, ADAM_B2, ADAM_EPS, ADAM_WD, ADAM_STEP = 0.001, 0.9, 0.999, 1e-08, 0.01, 10

VMEM_LIMIT = 56 * 1024 * 1024
ROW_TILE = 512
MM_TM = 2048

WEIGHTS = ['norm_g', 'w_in', 'gm_ln_g', 'gm_ln_b', 'gm_ws', 'gm_bs', 'conv_w', 'conv_b', 'dt_bias', 'a_log',
           'd_skip', 'ssm_norm_g', 'w_out', 'pool_w', 'pool_b', 'pool_scale', 'ffn_w_gate', 'ffn_w_up', 'ffn_w_down']
SHARD_AXIS = {'norm_g': 2, 'w_in': 2, 'gm_ln_g': None, 'gm_ln_b': None, 'gm_ws': None, 'gm_bs': None, 'conv_w': 2,
              'conv_b': None, 'dt_bias': None, 'a_log': None, 'd_skip': None, 'ssm_norm_g': None, 'w_out': 1,
              'pool_w': 2, 'pool_b': 2, 'pool_scale': 1, 'ffn_w_gate': 2, 'ffn_w_up': 2, 'ffn_w_down': 1}
GATHER_F32 =['norm_g', 'conv_w', 'pool_b', 'pool_scale']
BIG_WEIGHTS = ['w_in', 'w_out', 'ffn_w_gate', 'ffn_w_up', 'ffn_w_down']


def _cparams(sem=None):
    return pltpu.CompilerParams(dimension_semantics=sem, vmem_limit_bytes=VMEM_LIMIT)


def _dot(a, b):
    return jnp.dot(a.astype(MXU_DTYPE), b.astype(MXU_DTYPE), preferred_element_type=F32)


def _dot_nt(a, b):
    return lax.dot_general(a.astype(MXU_DTYPE), b.astype(MXU_DTYPE), (((1,), (1,)), ((), ())),
                           preferred_element_type=F32)


def _dot_tn(a, b):
    return lax.dot_general(a.astype(MXU_DTYPE), b.astype(MXU_DTYPE), (((0,), (0,)), ((), ())),
                           preferred_element_type=F32)


def _dot_exact(a, b):
    return jnp.dot(a, b, precision=lax.Precision.HIGHEST, preferred_element_type=F32)


def _sigmoid(x):
    return 1.0 / (1.0 + jnp.exp(-x))


def _silu(x):
    return x * _sigmoid(x)


def _silu_and_grad(x):
    s = _sigmoid(x)
    return x * s, s * (1.0 + x * (1.0 - s))


_GELU_C = math.sqrt(2.0 / math.pi)


def _gelu(x):
    return _gelu_and_grad(x)[0]


def _gelu_and_grad(x):
    x2 = x * x
    t = jnp.tanh(_GELU_C * x * (1.0 + 0.044715 * x2))
    half = 0.5 * (1.0 + t)
    return x * half, half + 0.5 * x * (1.0 - t * t) * (_GELU_C * (1.0 + 3.0 * 0.044715 * x2))


def _softplus(x):
    return jnp.maximum(x, 0.0) + jnp.log1p(jnp.exp(-jnp.abs(x)))


def _rms_scale(x):
    return lax.rsqrt(jnp.mean(x * x, axis=-1, keepdims=True) + EPS)


def _rms_bwd(dy, x, g):
    r = _rms_scale(x)
    xn = x * r
    dxn = dy * g
    dx = r * (dxn - xn * jnp.mean(dxn * xn, axis=-1, keepdims=True))
    return dx, dy * xn


def _colsum(x):
    return jnp.sum(x, axis=0, keepdims=True)


class _Exchange:
    def __init__(self, arrays, modes):
        self.arrays, self.modes, self.n = list(arrays), list(modes), len(arrays)
        self.blks = []
        for x, mode in zip(arrays, modes):
            if mode == 'gather':
                self.blks.append(tuple(x.shape))
            elif mode == 'slots':
                self.blks.append(tuple(x.shape[1:]))
            else:
                self.blks.append((x.shape[0] // N_DEV,) + tuple(x.shape[1:]))
        self.out_shape = [jax.ShapeDtypeStruct((N_DEV,) + blk, x.dtype) for x, blk in zip(arrays, self.blks)]
        self.in_specs = [pl.BlockSpec(memory_space=pl.ANY)] * self.n
        self.out_specs = [pl.BlockSpec(memory_space=pl.ANY) for _ in range(self.n)]
        n_sem = self.n * (N_DEV - 1)
        self.scratch = [pltpu.SemaphoreType.DMA((n_sem,)), pltpu.SemaphoreType.DMA((n_sem,)),
                        pltpu.SemaphoreType.DMA((self.n,))]

    def _copies(self, x_refs, out_refs, send_sems, recv_sems, local_sems, with_recvs):
        mx, my, mc = lax.axis_index("x"), lax.axis_index("y"), lax.axis_index("c")
        me = 4 * mx + 2 * my + mc

        def flip(v, bit):
            return 1 - v if bit else v

        def part(a, dev):
            if self.modes[a] == 'gather':
                return x_refs[a]
            if self.modes[a] == 'slots':
                return x_refs[a].at[dev]
            r = self.blks[a][0]
            return x_refs[a].at[pl.ds(pl.multiple_of(dev * r, 16), r)]

        sends, recvs, owns = [], [], []
        for k in (1, 2, 4, 6, 3, 5, 7):
            px, py, pc = flip(mx, (k >> 2) & 1), flip(my, (k >> 1) & 1), flip(mc, k & 1)
            peer = 4 * px + 2 * py + pc
            for a in range(self.n):
                sem = a * (N_DEV - 1) + k - 1
                sends.append(pltpu.make_async_remote_copy(
                    src_ref=part(a, peer), dst_ref=out_refs[a].at[me], send_sem=send_sems.at[sem],
                    recv_sem=recv_sems.at[sem], device_id=(px, py, pc), device_id_type=pl.DeviceIdType.MESH))
                if with_recvs:
                    recvs.append(pltpu.make_async_remote_copy(
                        src_ref=part(a, peer), dst_ref=out_refs[a].at[peer], send_sem=send_sems.at[sem],
                        recv_sem=recv_sems.at[sem], device_id=(px, py, pc), device_id_type=pl.DeviceIdType.MESH))
        for a in range(self.n):
            owns.append(pltpu.make_async_copy(part(a, me), out_refs[a].at[me], local_sems.at[a]))
        return sends, recvs, owns

    def start(self, *refs):
        sends, _, owns = self._copies(*refs, with_recvs=False)
        for cp in sends + owns:
            cp.start()

    def wait(self, *refs):
        sends, recvs, owns = self._copies(*refs, with_recvs=True)
        for cp in recvs:
            cp.wait_recv()
        for cp in sends:
            cp.wait_send()
        for cp in owns:
            cp.wait()


def _exchange(arrays, modes, *, name):
    ex = _Exchange(arrays, modes)

    def body(*refs):
        x_refs, out_refs, sems = refs[:ex.n], refs[ex.n:2 * ex.n], refs[2 * ex.n:]
        ex.start(x_refs, out_refs, *sems)
        ex.wait(x_refs, out_refs, *sems)

    return pl.pallas_call(
        body, name=name, out_shape=tuple(ex.out_shape), in_specs=ex.in_specs, out_specs=tuple(ex.out_specs),
        scratch_shapes=ex.scratch,
    )(*arrays)


def _gather_two_level(arrays, *, name):
    n = len(arrays)
    per = N_DEV - 1

    def body(*refs):
        x_refs, out_refs = refs[:n], refs[n:2 * n]
        send_sems, recv_sems, local_sems = refs[2 * n:]
        x, y, c = lax.axis_index("x"), lax.axis_index("y"), lax.axis_index("c")
        me, sibling = (x, y, c), (x, y, 1 - c)
        chips = [(1 - x, y), (x, 1 - y), (1 - x, 1 - y)]

        def copy(a, k, block, to, src=None):
            slot = out_refs[a].at[4 * block[0] + 2 * block[1] + block[2]]
            return pltpu.make_async_remote_copy(
                src_ref=slot if src is None else src, dst_ref=slot, send_sem=send_sems.at[a * per + k],
                recv_sem=recv_sems.at[a * per + k], device_id=to, device_id_type=pl.DeviceIdType.MESH)

        mines = [pltpu.make_async_copy(x_refs[a], out_refs[a].at[4 * x + 2 * y + c], local_sems.at[a]) for a in range(n)]
        firsts = []
        for a in range(n):
            firsts.append(copy(a, 0, me, sibling, src=x_refs[a]))
            firsts += [copy(a, 1 + j, me, (*chip, c), src=x_refs[a]) for j, chip in enumerate(chips)]
        for cp in mines + firsts:
            cp.start()
        passed = []
        for j, chip in enumerate(chips):
            for a in range(n):
                copy(a, 1 + j, (*chip, c), me).wait_recv()
                passed.append(copy(a, 4 + j, (*chip, c), sibling))
                passed[-1].start()
        for a in range(n):
            copy(a, 0, sibling, me).wait_recv()
            for j, chip in enumerate(chips):
                copy(a, 4 + j, (*chip, 1 - c), me).wait_recv()
        for cp in firsts + passed:
            cp.wait_send()
        for cp in mines:
            cp.wait()

    return pl.pallas_call(
        body, name=name,
        out_shape=tuple(jax.ShapeDtypeStruct((N_DEV,) + tuple(a.shape), a.dtype) for a in arrays),
        in_specs=[pl.BlockSpec(memory_space=pl.ANY)] * n,
        out_specs=tuple(pl.BlockSpec(memory_space=pl.ANY) for _ in range(n)),
        scratch_shapes=[pltpu.SemaphoreType.DMA((n * per,)), pltpu.SemaphoreType.DMA((n * per,)),
                        pltpu.SemaphoreType.DMA((n,))],
    )(*arrays)


def _hosted(body, n_in, n_out, n_scratch, grid, ex):
    def wrapped(*refs):
        ins, x_refs = refs[:n_in], refs[n_in:n_in + ex.n]
        outs = refs[n_in + ex.n:n_in + ex.n + n_out]
        xo_refs = refs[n_in + ex.n + n_out:n_in + 2 * ex.n + n_out]
        scr = refs[n_in + 2 * ex.n + n_out:n_in + 2 * ex.n + n_out + n_scratch]
        sems = refs[n_in + 2 * ex.n + n_out + n_scratch:]
        ids = [pl.program_id(d) for d in range(len(grid))]
        first = functools.reduce(jnp.logical_and, [i == 0 for i in ids])
        last = functools.reduce(jnp.logical_and, [i == g - 1 for i, g in zip(ids, grid)])

        @pl.when(first)
        def _():
            ex.start(x_refs, xo_refs, *sems)

        body(*ins, *outs, *scr)

        @pl.when(last)
        def _():
            ex.wait(x_refs, xo_refs, *sems)

    return wrapped


def _call(body, *, name, grid, inputs, in_specs, out_shape, out_specs, scratch, semantics, ex=None):
    if ex is None:
        return pl.pallas_call(
            body, name=name, grid=grid, out_shape=tuple(out_shape), in_specs=list(in_specs),
            out_specs=tuple(out_specs), scratch_shapes=list(scratch), compiler_params=_cparams(semantics))(*inputs)
    n_out = len(out_shape)
    res = pl.pallas_call(
        _hosted(body, len(inputs), n_out, len(scratch), grid, ex), name=name, grid=grid,
        out_shape=tuple(out_shape) + tuple(ex.out_shape), in_specs=list(in_specs) + ex.in_specs,
        out_specs=tuple(out_specs) + tuple(ex.out_specs), scratch_shapes=list(scratch) + ex.scratch,
        compiler_params=_cparams(("arbitrary",) * len(grid)))(*inputs, *ex.arrays)
    return res[:n_out], res[n_out:]


def _mm(a, b, *, name, out_dtype=F32, tm=MM_TM, tn=512, tk=None, ex=None):
    m, k = a.shape
    n = b.shape[1]
    tm, tn = min(tm, m), min(tn, n)
    tk = k if tk is None else tk
    nk = k // tk
    assert m % tm == 0 and n % tn == 0 and k % tk == 0

    def body(a_ref, b_ref, o_ref, acc_ref):
        kk = pl.program_id(2)
        part = _dot(a_ref[...], b_ref[...])
        if nk == 1:
            o_ref[...] = part.astype(out_dtype)
        else:
            @pl.when(kk == 0)
            def _():
                acc_ref[...] = part

            @pl.when(kk > 0)
            def _():
                acc_ref[...] += part

            @pl.when(kk == nk - 1)
            def _():
                o_ref[...] = acc_ref[...].astype(out_dtype)

    res = _call(
        body, name=name, grid=(m // tm, n // tn, nk), inputs=(a, b),
        in_specs=[pl.BlockSpec((tm, tk), lambda i, j, kk: (i, kk)), pl.BlockSpec((tk, tn), lambda i, j, kk: (kk, j))],
        out_shape=[jax.ShapeDtypeStruct((m, n), out_dtype)],
        out_specs=[pl.BlockSpec((tm, tn), lambda i, j, kk: (i, j))],
        scratch=[pltpu.VMEM((tm, tn) if nk > 1 else (8, LANES), F32)],
        semantics=("parallel", "parallel", "arbitrary"), ex=ex)
    return res[0] if ex is None else (res[0][0], res[1])


def _mm_tn(a, b, *, name, out_dtype=F32, tm=1024, tn=512, tk=1024, shift=0):
    t, m = a.shape
    n = b.shape[1]
    tm, tn, tk = min(tm, m), min(tn, n), min(tk, t)
    nk = t // tk
    nb = m // tm
    assert m % tm == 0 and n % tn == 0 and t % tk == 0

    def body(a_ref, b_ref, o_ref, acc_ref):
        kk = pl.program_id(2)
        part = _dot_tn(a_ref[...], b_ref[...])

        @pl.when(kk == 0)
        def _():
            acc_ref[...] = part

        @pl.when(kk > 0)
        def _():
            acc_ref[...] += part

        @pl.when(kk == nk - 1)
        def _():
            o_ref[...] = acc_ref[...].astype(out_dtype)

    return pl.pallas_call(
        body, name=name, grid=(nb, n // tn, nk),
        out_shape=jax.ShapeDtypeStruct((m, n), out_dtype),
        in_specs=[pl.BlockSpec((tk, tm), lambda i, j, kk: (kk, i)), pl.BlockSpec((tk, tn), lambda i, j, kk: (kk, j))],
        out_specs=pl.BlockSpec((tm, tn), lambda i, j, kk: ((i + shift) % nb, j)),
        scratch_shapes=[pltpu.VMEM((tm, tn), F32)],
        compiler_params=_cparams(("parallel", "parallel", "arbitrary")),
    )(a, b)


def _mm_tn_gate_up(dgu, yn, *, name, out_dtype, tk=2048):
    t, m = dgu.shape
    n = yn.shape[1]
    tk = min(tk, t)
    nk = t // tk
    nb = m // (2 * FF_TILE)

    def body(a_ref, b_ref, og_ref, ou_ref, acc_ref):
        kk = pl.program_id(1)
        part = _dot_tn(a_ref[...], b_ref[...])

        @pl.when(kk == 0)
        def _():
            acc_ref[...] = part

        @pl.when(kk > 0)
        def _():
            acc_ref[...] += part

        @pl.when(kk == nk - 1)
        def _():
            og_ref[...] = acc_ref[:FF_TILE, :].astype(out_dtype)
            ou_ref[...] = acc_ref[FF_TILE:, :].astype(out_dtype)

    out = jax.ShapeDtypeStruct((m // 2, n), out_dtype)
    o_spec = pl.BlockSpec((FF_TILE, n), lambda i, kk: (i, 0))
    return pl.pallas_call(
        body, name=name, grid=(nb, nk), out_shape=(out, out),
        in_specs=[pl.BlockSpec((tk, 2 * FF_TILE), lambda i, kk: (kk, i)), pl.BlockSpec((tk, n), lambda i, kk: (kk, 0))],
        out_specs=(o_spec, o_spec),
        scratch_shapes=[pltpu.VMEM((2 * FF_TILE, n), F32)],
        compiler_params=_cparams(("parallel", "arbitrary")),
    )(dgu, yn)


def _mm_swiglu(a, w_gu, *, name, tm=MM_TM, ex=None):
    m, k = a.shape
    n = w_gu.shape[1]
    nt = n // (2 * FF_TILE)
    tm = min(tm, m)

    def body(a_ref, b_ref, gu_ref, act_ref):
        gu = _dot(a_ref[...], b_ref[...])
        gu_ref[...] = gu.astype(MXU_DTYPE)
        act_ref[...] = (_silu(gu[:, :FF_TILE]) * gu[:, FF_TILE:]).astype(MXU_DTYPE)

    return _call(
        body, name=name, grid=(m // tm, nt), inputs=(a, w_gu),
        in_specs=[pl.BlockSpec((tm, k), lambda i, j: (i, 0)), pl.BlockSpec((k, 2 * FF_TILE), lambda i, j: (0, j))],
        out_shape=[jax.ShapeDtypeStruct((m, n), MXU_DTYPE), jax.ShapeDtypeStruct((m, n // 2), MXU_DTYPE)],
        out_specs=[pl.BlockSpec((tm, 2 * FF_TILE), lambda i, j: (i, j)), pl.BlockSpec((tm, FF_TILE), lambda i, j: (i, j))],
        scratch=[], semantics=("parallel", "parallel"), ex=ex)


def _mm_dswiglu(dd, w_down_t, gu, *, name, tm=MM_TM):
    m, k = dd.shape
    n = gu.shape[1]
    nt = n // (2 * FF_TILE)
    tm = min(tm, m)

    def body(d_ref, w_ref, gu_ref, o_ref):
        dact = _dot(d_ref[...], w_ref[...])
        gate, up = gu_ref[:, :FF_TILE].astype(F32), gu_ref[:, FF_TILE:].astype(F32)
        act_gate, act_grad = _silu_and_grad(gate)
        o_ref[:, :FF_TILE] = (dact * up * act_grad).astype(MXU_DTYPE)
        o_ref[:, FF_TILE:] = (dact * act_gate).astype(MXU_DTYPE)

    return pl.pallas_call(
        body, name=name, grid=(m // tm, nt),
        out_shape=jax.ShapeDtypeStruct((m, n), MXU_DTYPE),
        in_specs=[pl.BlockSpec((tm, k), lambda i, j: (i, 0)), pl.BlockSpec((k, FF_TILE), lambda i, j: (0, j)),
                  pl.BlockSpec((tm, 2 * FF_TILE), lambda i, j: (i, j))],
        out_specs=pl.BlockSpec((tm, 2 * FF_TILE), lambda i, j: (i, j)),
        compiler_params=_cparams(("parallel", "parallel")),
    )(dd, w_down_t, gu)


def _row_spec(width, tr=ROW_TILE):
    return pl.BlockSpec((tr, width), lambda i: (i, 0))


def _vec_spec(width, rows=1):
    return pl.BlockSpec((rows, width), lambda i: (0, 0))


def _rn_fwd(h, g, *, name, out_dtype):
    rows, d = h.shape

    def body(h_ref, g_ref, o_ref):
        x = h_ref[...]
        o_ref[...] = (x * _rms_scale(x) * g_ref[...]).astype(out_dtype)

    return pl.pallas_call(
        body, name=name, grid=(rows // ROW_TILE,),
        out_shape=jax.ShapeDtypeStruct((rows, d), out_dtype),
        in_specs=[_row_spec(d), _vec_spec(d)], out_specs=_row_spec(d),
        compiler_params=_cparams(("parallel",)),
    )(h, g)


def _resid_rn_fwd(h_in, o, g_post, g_next, *, name, next_dtype):
    rows, d = h_in.shape

    def body(h_ref, o_ref, gp_ref, gn_ref, ho_ref, yn_ref):
        ov = o_ref[...]
        h = h_ref[...] + ov * _rms_scale(ov) * gp_ref[...]
        ho_ref[...] = h
        yn_ref[...] = (h * _rms_scale(h) * gn_ref[...]).astype(next_dtype)

    return pl.pallas_call(
        body, name=name, grid=(rows // ROW_TILE,),
        out_shape=(jax.ShapeDtypeStruct((rows, d), F32), jax.ShapeDtypeStruct((rows, d), next_dtype)),
        in_specs=[_row_spec(d), _row_spec(d), _vec_spec(d), _vec_spec(d)],
        out_specs=(_row_spec(d), _row_spec(d)),
        compiler_params=_cparams(("parallel",)),
    )(h_in, o, g_post, g_next)


def _resid_loss(h_in, o, g_post, target, *, name):
    rows, d = h_in.shape

    def body(h_ref, o_ref, gp_ref, t_ref, dh_ref, loss_ref, do_ref, dg_ref):
        ov = o_ref[...]
        gp = gp_ref[...]
        err = h_ref[...] + ov * _rms_scale(ov) * gp - t_ref[...]
        dh = err * (1.0 / d)
        dh_ref[...] = dh
        do, dg = _rms_bwd(dh, ov, gp)
        do_ref[...] = do.astype(MXU_DTYPE)

        @pl.when(pl.program_id(0) == 0)
        def _():
            loss_ref[...] = jnp.zeros_like(loss_ref)
            dg_ref[...] = jnp.zeros_like(dg_ref)

        loss_ref[...] += 0.5 * jnp.sum(jnp.mean(err * err, axis=-1, keepdims=True), axis=0, keepdims=True)
        dg_ref[...] += _colsum(dg)

    return pl.pallas_call(
        body, name=name, grid=(rows // ROW_TILE,),
        out_shape=(jax.ShapeDtypeStruct((rows, d), F32), jax.ShapeDtypeStruct((1, 1), F32),
                   jax.ShapeDtypeStruct((rows, d), MXU_DTYPE), jax.ShapeDtypeStruct((1, d), F32)),
        in_specs=[_row_spec(d), _row_spec(d), _vec_spec(d), _row_spec(d)],
        out_specs=(_row_spec(d), pl.BlockSpec((1, 1), lambda i: (0, 0)), _row_spec(d), _vec_spec(d)),
        compiler_params=_cparams(("arbitrary",)),
    )(h_in, o, g_post, target)


def _resid_bwd_pre_post(dh, dyn_list, h_in, g_pre, o_prev, g_post_prev, *, name, out_dtype):
    rows, d = dh.shape
    n_dyn = len(dyn_list)

    def body(*refs):
        dh_ref, dyn_refs = refs[0], refs[1:1 + n_dyn]
        h_ref, g_ref, o_ref, gp_ref, out_ref, do_ref, dg_ref, dgp_ref = refs[1 + n_dyn:]
        dyn = dyn_refs[0][...]
        for r in dyn_refs[1:]:
            dyn = dyn + r[...]
        dx, dg = _rms_bwd(dyn, h_ref[...], g_ref[...])
        dh_in = dh_ref[...] + dx
        out_ref[...] = dh_in
        do, dgp = _rms_bwd(dh_in, o_ref[...], gp_ref[...])
        do_ref[...] = do.astype(out_dtype)

        @pl.when(pl.program_id(0) == 0)
        def _():
            dg_ref[...] = jnp.zeros_like(dg_ref)
            dgp_ref[...] = jnp.zeros_like(dgp_ref)

        dg_ref[...] += _colsum(dg)
        dgp_ref[...] += _colsum(dgp)

    return pl.pallas_call(
        body, name=name, grid=(rows // ROW_TILE,),
        out_shape=(jax.ShapeDtypeStruct((rows, d), F32), jax.ShapeDtypeStruct((rows, d), out_dtype),
                   jax.ShapeDtypeStruct((1, d), F32), jax.ShapeDtypeStruct((1, d), F32)),
        in_specs=[_row_spec(d)] + [_row_spec(d)] * n_dyn + [_row_spec(d), _vec_spec(d), _row_spec(d), _vec_spec(d)],
        out_specs=(_row_spec(d), _row_spec(d), _vec_spec(d), _vec_spec(d)),
        compiler_params=_cparams(("arbitrary",)),
    )(dh, *dyn_list, h_in, g_pre, o_prev, g_post_prev)


def _resid_bwd_pre(dh, dyn_list, h_in, g_pre, *, name):
    rows, d = dh.shape
    n_dyn = len(dyn_list)

    def body(*refs):
        dh_ref, dyn_refs, h_ref, g_ref, out_ref, dg_ref = refs[0], refs[1:1 + n_dyn], *refs[1 + n_dyn:]
        dyn = dyn_refs[0][...]
        for r in dyn_refs[1:]:
            dyn = dyn + r[...]
        dx, dg = _rms_bwd(dyn, h_ref[...], g_ref[...])
        out_ref[...] = dh_ref[...] + dx

        @pl.when(pl.program_id(0) == 0)
        def _():
            dg_ref[...] = jnp.zeros_like(dg_ref)

        dg_ref[...] += _colsum(dg)

    return pl.pallas_call(
        body, name=name, grid=(rows // ROW_TILE,),
        out_shape=(jax.ShapeDtypeStruct((rows, d), F32), jax.ShapeDtypeStruct((1, d), F32)),
        in_specs=[_row_spec(d)] + [_row_spec(d)] * n_dyn + [_row_spec(d), _vec_spec(d)],
        out_specs=(_row_spec(d), _vec_spec(d)),
        compiler_params=_cparams(("arbitrary",)),
    )(dh, *dyn_list, h_in, g_pre)


def _layer_norm_stats(x):
    mu = jnp.mean(x, axis=-1, keepdims=True)
    xc = x - mu
    rstd = lax.rsqrt(jnp.mean(xc * xc, axis=-1, keepdims=True) + EPS)
    return xc * rstd, rstd


def _gmlp_fwd(proj, ln_g, ln_b, wm, bcol, *, name):
    rows = proj.shape[0]
    tr = ROW_TILE

    def body(u_ref, v_ref, lg_ref, lb_ref, wm_ref, bc_ref, ya_ref):
        vhat, _ = _layer_norm_stats(_gelu(v_ref[...]))
        vl = (vhat * lg_ref[...] + lb_ref[...]).astype(MXU_DTYPE)
        gu = _gelu(u_ref[...])
        bc = bc_ref[...]
        for c in range(tr // CHUNK):
            rs = slice(c * CHUNK, (c + 1) * CHUNK)
            for h in range(GM_HEADS):
                cs = slice(h * GM_HEAD_DIM, (h + 1) * GM_HEAD_DIM)
                mixed = _dot(wm_ref[h], vl[rs, cs]) + bc[:, h:h + 1]
                ya_ref[rs, cs] = (gu[rs, cs] * mixed).astype(MXU_DTYPE)

    return pl.pallas_call(
        body, name=name, grid=(rows // tr,),
        out_shape=jax.ShapeDtypeStruct((rows, 2 * D_MODEL), MXU_DTYPE),
        in_specs=[pl.BlockSpec((tr, D_MODEL), lambda i: (i, 2)), pl.BlockSpec((tr, D_MODEL), lambda i: (i, 3)),
                  _vec_spec(D_MODEL), _vec_spec(D_MODEL),
                  pl.BlockSpec((GM_HEADS, CHUNK, CHUNK), lambda i: (0, 0, 0)), _vec_spec(LANES, CHUNK)],
        out_specs=_row_spec(D_MODEL, tr),
        compiler_params=_cparams(("parallel",)),
    )(proj, proj, ln_g, ln_b, wm, bcol)


def _gmlp_bwd(proj, dcat, ln_g, ln_b, wm, wm_t, bcol, *, name):
    rows = proj.shape[0]
    tr = ROW_TILE

    def body(u_ref, v_ref, dy_ref, lg_ref, lb_ref, wm_ref, wmt_ref, bc_ref,
             duv_ref, dwm_ref, dbc_ref, dlg_ref, dlb_ref, dvl_scr):
        @pl.when(pl.program_id(0) == 0)
        def _():
            dwm_ref[...] = jnp.zeros_like(dwm_ref)
            dbc_ref[...] = jnp.zeros_like(dbc_ref)
            dlg_ref[...] = jnp.zeros_like(dlg_ref)
            dlb_ref[...] = jnp.zeros_like(dlb_ref)

        gv, gv_grad = _gelu_and_grad(v_ref[...])
        vhat, rstd = _layer_norm_stats(gv)
        lg = lg_ref[...]
        vl = (vhat * lg + lb_ref[...]).astype(MXU_DTYPE)
        gu, gu_grad = _gelu_and_grad(u_ref[...])
        dy = dy_ref[...]
        bc = bc_ref[...]
        row = lax.broadcasted_iota(jnp.int32, (CHUNK, CHUNK), 0)
        lane = lax.broadcasted_iota(jnp.int32, (CHUNK, CHUNK), 1)
        causal = lane <= row
        dbc = jnp.zeros((CHUNK, LANES), F32)
        for c in range(tr // CHUNK):
            rs = slice(c * CHUNK, (c + 1) * CHUNK)
            for h in range(GM_HEADS):
                cs = slice(h * GM_HEAD_DIM, (h + 1) * GM_HEAD_DIM)
                vl_h = vl[rs, cs]
                mixed = _dot(wm_ref[h], vl_h) + bc[:, h:h + 1]
                dy_h = dy[rs, cs]
                duv_ref[rs, cs] = (dy_h * mixed * gu_grad[rs, cs]).astype(MXU_DTYPE)
                dmixed = dy_h * gu[rs, cs]
                dwm_ref[h] += jnp.where(causal, _dot_nt(dmixed, vl_h), 0.0)
                dbc = dbc + jnp.where(lane == h, jnp.sum(dmixed, axis=1, keepdims=True), 0.0)
                dvl_scr[rs, cs] = _dot(wmt_ref[h], dmixed)
        dbc_ref[...] += dbc
        dvl = dvl_scr[...]
        dlg_ref[...] += _colsum(dvl * vhat)
        dlb_ref[...] += _colsum(dvl)
        dvh = dvl * lg
        dgv = rstd * (dvh - jnp.mean(dvh, axis=-1, keepdims=True) - vhat * jnp.mean(dvh * vhat, axis=-1, keepdims=True))
        duv_ref[:, D_MODEL:] = (dgv * gv_grad).astype(MXU_DTYPE)

    return pl.pallas_call(
        body, name=name, grid=(rows // tr,),
        out_shape=(jax.ShapeDtypeStruct((rows, IN_MAIN), MXU_DTYPE),
                   jax.ShapeDtypeStruct((GM_HEADS, CHUNK, CHUNK), F32), jax.ShapeDtypeStruct((CHUNK, LANES), F32),
                   jax.ShapeDtypeStruct((1, D_MODEL), F32), jax.ShapeDtypeStruct((1, D_MODEL), F32)),
        in_specs=[pl.BlockSpec((tr, D_MODEL), lambda i: (i, 2)), pl.BlockSpec((tr, D_MODEL), lambda i: (i, 3)),
                  pl.BlockSpec((tr, D_MODEL), lambda i: (i, 0)), _vec_spec(D_MODEL), _vec_spec(D_MODEL),
                  pl.BlockSpec((GM_HEADS, CHUNK, CHUNK), lambda i: (0, 0, 0)),
                  pl.BlockSpec((GM_HEADS, CHUNK, CHUNK), lambda i: (0, 0, 0)), _vec_spec(LANES, CHUNK)],
        out_specs=(pl.BlockSpec((tr, 2 * D_MODEL), lambda i: (i, 1)),
                   pl.BlockSpec((GM_HEADS, CHUNK, CHUNK), lambda i: (0, 0, 0)), _vec_spec(LANES, CHUNK),
                   _vec_spec(D_MODEL), _vec_spec(D_MODEL)),
        scratch_shapes=[pltpu.VMEM((tr, D_MODEL), F32)],
        compiler_params=_cparams(("arbitrary",)),
    )(proj, proj, dcat, ln_g, ln_b, wm, wm_t, bcol)


def _conv_fwd(proj, conv_w8, conv_b, *, name):
    rows = proj.shape[0]
    tr = ROW_TILE
    hb = tr // CONV_HALO

    def body(x_ref, prev_ref, w_ref, b_ref, pre_ref, buf):
        first = pl.program_id(0) == 0
        buf[pl.ds(0, CONV_HALO), :] = jnp.where(first, 0.0, prev_ref[...])
        buf[pl.ds(CONV_HALO, tr), :] = x_ref[...]
        ext = buf[...]
        acc = jnp.broadcast_to(b_ref[...], (tr, CONV_DIM))
        for k in range(SSM_CONV):
            s = SSM_CONV - 1 - k
            acc = acc + w_ref[k:k + 1, :] * (x_ref[...] if s == 0 else pltpu.roll(ext, s, axis=0)[CONV_HALO:])
        pre_ref[...] = acc

    return pl.pallas_call(
        body, name=name, grid=(rows // tr,),
        out_shape=jax.ShapeDtypeStruct((rows, CONV_DIM), F32),
        in_specs=[pl.BlockSpec((tr, CONV_DIM), lambda i: (i, 0)),
                  pl.BlockSpec((CONV_HALO, CONV_DIM), lambda i: (jnp.maximum(i * hb - 1, 0), 0)),
                  _vec_spec(CONV_DIM, 8), _vec_spec(CONV_DIM)],
        out_specs=_row_spec(CONV_DIM, tr),
        scratch_shapes=[pltpu.VMEM((tr + CONV_HALO, CONV_DIM), F32)],
        compiler_params=_cparams(("parallel",)),
    )(proj, proj, conv_w8, conv_b)


def _conv_bwd(dpre, proj, conv_w8, dproj, *, name):
    rows = proj.shape[0]
    tr = ROW_TILE
    hb = tr // CONV_HALO
    nblk = rows // tr

    def body(d_ref, dnext_ref, x_ref, w_ref, dproj_ref, dx_ref, dw_ref, db_ref, dbuf):
        i = pl.program_id(0)

        @pl.when(i == 0)
        def _():
            dw_ref[...] = jnp.zeros_like(dw_ref)
            db_ref[...] = jnp.zeros_like(db_ref)

        d = d_ref[...]
        x = x_ref[...]
        dbuf[pl.ds(0, tr), :] = d
        dbuf[pl.ds(tr, CONV_HALO), :] = jnp.where(i == nblk - 1, 0.0, dnext_ref[...])
        ext = dbuf[...]
        acc = jnp.zeros((tr, CONV_DIM), F32)
        for k in range(SSM_CONV):
            s = SSM_CONV - 1 - k
            shifted = d if s == 0 else pltpu.roll(ext, tr + CONV_HALO - s, axis=0)[:tr]
            acc = acc + w_ref[k:k + 1, :] * shifted
            dw_ref[k:k + 1, :] += _colsum(shifted * x)
        dx_ref[...] = acc.astype(MXU_DTYPE)
        db_ref[...] += _colsum(d)

    return pl.pallas_call(
        body, name=name, grid=(nblk,),
        out_shape=(jax.ShapeDtypeStruct((rows, IN_MAIN), MXU_DTYPE), jax.ShapeDtypeStruct((8, CONV_DIM), F32),
                   jax.ShapeDtypeStruct((1, CONV_DIM), F32)),
        in_specs=[_row_spec(CONV_DIM, tr),
                  pl.BlockSpec((CONV_HALO, CONV_DIM), lambda i: (jnp.minimum((i + 1) * hb, rows // CONV_HALO - 1), 0)),
                  pl.BlockSpec((tr, CONV_DIM), lambda i: (i, 0)),
                  _vec_spec(CONV_DIM, 8), pl.BlockSpec(memory_space=pl.ANY)],
        out_specs=(_row_spec(CONV_DIM, tr), _vec_spec(CONV_DIM, 8), _vec_spec(CONV_DIM)),
        scratch_shapes=[pltpu.VMEM((tr + CONV_HALO, CONV_DIM), F32)],
        input_output_aliases={4: 0},
        compiler_params=_cparams(("arbitrary",)),
    )(dpre, dpre, proj, conv_w8, dproj)


def _chunk_iotas():
    row = lax.broadcasted_iota(jnp.int32, (CHUNK, CHUNK), 0)
    lane = lax.broadcasted_iota(jnp.int32, (CHUNK, CHUNK), 1)
    return row, lane, lane <= row


SSD_BWD_CHUNKS = 1
GROUP_DIM = D_MODEL // SSM_GROUPS
HEADS_PER_GROUP = SSM_HEADS // SSM_GROUPS
HEAD_DIM = GROUP_DIM // HEADS_PER_GROUP


def _split(x):
    hi = x.astype(MXU_DTYPE)
    return hi, (x - hi.astype(F32)).astype(MXU_DTYPE)


def _dot_split(x, sel):
    hi, lo = _split(x)
    return jnp.dot(hi, sel, preferred_element_type=F32) + jnp.dot(lo, sel, preferred_element_type=F32)


def _dot_split_rhs(sel, x):
    hi, lo = _split(x)
    return jnp.dot(sel, hi, preferred_element_type=F32) + jnp.dot(sel, lo, preferred_element_type=F32)


def _head_selectors():
    h = lax.broadcasted_iota(jnp.int32, (LANES, D_MODEL), 0)
    p = lax.broadcasted_iota(jnp.int32, (LANES, D_MODEL), 1)
    sel_t = (h == p // HEAD_DIM).astype(MXU_DTYPE)
    return sel_t, sel_t.T


def _expand_heads(per_head, e_end, selt_ref, sel_ref):
    stacked = jnp.concatenate(per_head, axis=0)
    wide = _dot_split(stacked, selt_ref[...])
    n = per_head[0].shape[0]
    e_cols = jnp.broadcast_to(e_end, (LANES, LANES)).T
    tall = _dot_split_rhs(sel_ref[...], e_cols)
    return [wide[n * i:n * (i + 1)] for i in range(len(per_head))], tall


def _by_quarter(index, pieces):
    out = pieces[3]
    for q in (2, 1, 0):
        out = jnp.where(index == q, pieces[q], out)
    return out


def _ssd_fwd_grouped(pre, dtr, proj, dtb, alog, dsk, gn, cat, *, name):
    rows = pre.shape[0]
    nc = rows // CHUNK

    def body(pre_ref, dtr_ref, z_ref, dtb_ref, alog_ref, dsk_ref, gn_ref, cat_ref, selt_ref, sel_ref, yb_ref, y_ref,
             st_ref, dt_ref, acum_ref, s_scr):
        @pl.when(pl.program_id(0) == 0)
        def _():
            s_scr[...] = jnp.zeros_like(s_scr)

        row, lane, tril = _chunk_iotas()
        dt = _softplus(dtr_ref[...] + dtb_ref[...])
        acum = _dot_exact(tril.astype(F32), dt * (-jnp.exp(alog_ref[...])))
        dt_ref[...] = dt
        acum_ref[...] = acum
        acum_t = acum.T
        a_end = acum[CHUNK - 1:CHUNK, :]
        (dt_x, eacum_x, dte_x, dsk_x), e_rows_all = _expand_heads(
            [dt, jnp.exp(acum), jnp.exp(a_end - acum), jnp.broadcast_to(dsk_ref[...], (CHUNK, LANES))], jnp.exp(a_end),
            selt_ref, sel_ref)
        lane_q = lax.broadcasted_iota(jnp.int32, (CHUNK, GROUP_DIM), 1) // HEAD_DIM

        for g in range(SSM_GROUPS):
            cs = slice(GROUP_DIM * g, GROUP_DIM * (g + 1))
            b_g = _silu(pre_ref[:, 1024 + SSM_STATE * g:1024 + SSM_STATE * (g + 1)]).astype(MXU_DTYPE)
            c_g = _silu(pre_ref[:, 1536 + SSM_STATE * g:1536 + SSM_STATE * (g + 1)]).astype(MXU_DTYPE)
            cb = _dot_nt(c_g, b_g)
            xs = _silu(pre_ref[:, cs])
            xdt = xs * dt_x[:, cs]
            m_stack = jnp.concatenate(
                [(cb * jnp.exp(jnp.where(tril, acum[:, h:h + 1] - acum_t[h:h + 1, :], -jnp.inf))).astype(MXU_DTYPE)
                 for h in range(4 * g, 4 * g + 4)], axis=0)
            y_all = _dot(m_stack, xdt)
            y = _by_quarter(lane_q, [y_all[CHUNK * q:CHUNK * (q + 1)] for q in range(HEADS_PER_GROUP)])
            s_prev = s_scr[g]
            st_ref[0, g] = s_prev
            y = y + _dot_nt(c_g, s_prev) * eacum_x[:, cs] + dsk_x[:, cs] * xs
            xw = xdt * dte_x[:, cs]
            s_scr[g] = e_rows_all[cs, :] * s_prev + _dot(xw.T, b_g)
            y_ref[:, cs] = y
            gated = y * _silu(z_ref[:, cs])
            r = lax.rsqrt(jnp.mean(gated * gated, axis=1, keepdims=True) + EPS)
            yb_ref[:, cs] = (gated * r * gn_ref[:, cs]).astype(MXU_DTYPE)

    return pl.pallas_call(
        body, name=name, grid=(nc,),
        out_shape=(jax.ShapeDtypeStruct((rows, 2 * D_MODEL), MXU_DTYPE), jax.ShapeDtypeStruct((rows, D_MODEL), F32),
                   jax.ShapeDtypeStruct((nc, SSM_GROUPS, GROUP_DIM, SSM_STATE), F32),
                   jax.ShapeDtypeStruct((rows, LANES), F32), jax.ShapeDtypeStruct((rows, LANES), F32)),
        in_specs=[_row_spec(CONV_DIM, CHUNK), _row_spec(LANES, CHUNK), pl.BlockSpec((CHUNK, D_MODEL), lambda i: (i, 4)),
                  _vec_spec(LANES), _vec_spec(LANES), _vec_spec(LANES), _vec_spec(D_MODEL),
                  pl.BlockSpec(memory_space=pl.ANY), _vec_spec(D_MODEL, LANES), _vec_spec(LANES, D_MODEL)],
        out_specs=(pl.BlockSpec((CHUNK, D_MODEL), lambda i: (i, 1)), _row_spec(D_MODEL, CHUNK),
                   pl.BlockSpec((1, SSM_GROUPS, GROUP_DIM, SSM_STATE), lambda i: (i, 0, 0, 0)),
                   _row_spec(LANES, CHUNK), _row_spec(LANES, CHUNK)),
        scratch_shapes=[pltpu.VMEM((SSM_GROUPS, GROUP_DIM, SSM_STATE), F32)],
        input_output_aliases={7: 0},
        compiler_params=_cparams(("arbitrary",)),
    )(pre, dtr, proj, dtb, alog, dsk, gn, cat, *_head_selectors())


def _ssd_bwd_grouped(pre, dtr, dt_saved, acum_saved, proj, y_saved, states, dcat, dtb, alog, dsk, gn, dproj, *, name):
    rows = pre.shape[0]
    cps = SSD_BWD_CHUNKS
    tr = CHUNK * cps
    nsteps = rows // tr

    def rev(i):
        return nsteps - 1 - i

    def body(pre_ref, dtr_ref, dt_ref, acum_ref, z_ref, y_ref, st_ref, dyb_ref, dtb_ref, alog_ref, dsk_ref, gn_ref,
             dproj_ref, selt_ref, sel_ref, dpre_ref, dz_ref, ddtr_ref, dgn_ref, dvec_ref, g_scr):
        @pl.when(pl.program_id(0) == 0)
        def _():
            g_scr[...] = jnp.zeros_like(g_scr)
            dgn_ref[...] = jnp.zeros_like(dgn_ref)
            dvec_ref[...] = jnp.zeros_like(dvec_ref)

        for cc in reversed(range(cps)):
            at = lambda ref: ref.at[pl.ds(cc * CHUNK, CHUNK)]
            chunk(at(pre_ref), at(dtr_ref), at(dt_ref), at(acum_ref), at(z_ref), at(y_ref), st_ref.at[cc], at(dyb_ref),
                  dtb_ref, alog_ref, dsk_ref, gn_ref, selt_ref, sel_ref, at(dpre_ref), at(dz_ref), at(ddtr_ref), dgn_ref,
                  dvec_ref, g_scr)

    def chunk(pre_ref, dtr_ref, dt_ref, acum_ref, z_ref, y_ref, st_ref, dyb_ref, dtb_ref, alog_ref, dsk_ref, gn_ref,
              selt_ref, sel_ref, dpre_ref, dz_ref, ddtr_ref, dgn_ref, dvec_ref, g_scr):
        row, lane, tril = _chunk_iotas()
        triu = lane >= row
        dt, acum = dt_ref[...], acum_ref[...]
        a = -jnp.exp(alog_ref[...])
        acum_t = acum.T
        a_end = acum[CHUNK - 1:CHUNK, :]
        e_end = jnp.exp(a_end)
        (dt_x, eacum_x, dte_x, dsk_x), e_rows_all = _expand_heads(
            [dt, jnp.exp(acum), jnp.exp(a_end - acum), jnp.broadcast_to(dsk_ref[...], (CHUNK, LANES))], e_end,
            selt_ref, sel_ref)
        lane_q = lax.broadcasted_iota(jnp.int32, (CHUNK, GROUP_DIM), 1) // HEAD_DIM
        zero = jnp.zeros((CHUNK, LANES), F32)
        dacum_c, dacum_r, ddt_c = zero, zero, zero
        d_aend = jnp.zeros((1, LANES), F32)
        d_dsk = jnp.zeros((1, LANES), F32)
        iota = lambda shape, dim: lax.broadcasted_iota(jnp.int32, shape, dim)
        q256, lane_256 = iota((GROUP_DIM, LANES), 0) // HEAD_DIM, iota((GROUP_DIM, LANES), 1)
        q512, lane_512 = iota((4 * CHUNK, LANES), 0) // CHUNK, iota((4 * CHUNK, LANES), 1)
        row_512t, q512t = iota((LANES, 4 * CHUNK), 0), iota((LANES, 4 * CHUNK), 1) // CHUNK

        for g in range(SSM_GROUPS):
            cs = slice(GROUP_DIM * g, GROUP_DIM * (g + 1))
            yv = y_ref[:, cs]
            sz, sz_grad = _silu_and_grad(z_ref[:, cs])
            gated = yv * sz
            dyb = dyb_ref[:, cs]
            dgh = dyb * gn_ref[:, cs]
            r = lax.rsqrt(jnp.mean(gated * gated, axis=1, keepdims=True) + EPS)
            dgn_ref[:, cs] += _colsum(dyb * gated * r)
            dgated = r * dgh - gated * (r * r * r * jnp.mean(dgh * gated, axis=1, keepdims=True))
            dy = dgated * sz
            dz_ref[:, cs] = (dgated * yv * sz_grad).astype(MXU_DTYPE)

            b_f, b_grad = _silu_and_grad(pre_ref[:, 1024 + SSM_STATE * g:1024 + SSM_STATE * (g + 1)])
            c_f, c_grad = _silu_and_grad(pre_ref[:, 1536 + SSM_STATE * g:1536 + SSM_STATE * (g + 1)])
            b_g, c_g = b_f.astype(MXU_DTYPE), c_f.astype(MXU_DTYPE)
            xs, xs_grad = _silu_and_grad(pre_ref[:, cs])
            dtq = dt_x[:, cs]
            xdt = xs * dtq
            xdt_m = xdt.astype(MXU_DTYPE)
            dy_m = dy.astype(MXU_DTYPE)
            s_prev = st_ref[g]
            g_next = g_scr[g]
            eacq, dteq = eacum_x[:, cs], dte_x[:, cs]
            t_off = dy * (_dot_nt(c_g, s_prev) * eacq)
            dye = dy * eacq
            dc_g = _dot(dye, s_prev)
            bg = _dot_nt(b_g, g_next)
            xw = xdt * dteq
            db_g = _dot(xw, g_next)
            t_w = xw * bg
            gs = g_next * s_prev
            g_scr[g] = e_rows_all[cs, :] * g_next + _dot(dye.T, c_g)
            cb = _dot_nt(c_g, b_g)
            cb_t = cb.T
            heads = range(4 * g, 4 * g + 4)
            decs = [jnp.exp(jnp.where(tril, acum[:, h:h + 1] - acum_t[h:h + 1, :], -jnp.inf)) for h in heads]
            mt_stack = jnp.concatenate(
                [(cb_t * jnp.exp(jnp.where(triu, acum_t[h:h + 1, :] - acum[:, h:h + 1], -jnp.inf))).astype(MXU_DTYPE)
                 for h in heads], axis=0)
            dy_stack = jnp.concatenate([jnp.where(lane_q == q, dy, 0.0).astype(MXU_DTYPE)
                                        for q in range(HEADS_PER_GROUP)], axis=0)
            dm_all = _dot_nt(dy_stack, xdt_m)
            dx_all = _dot(mt_stack, dy_m)
            dxdt = bg * dteq + _by_quarter(lane_q, [dx_all[CHUNK * q:CHUNK * (q + 1)] for q in range(HEADS_PER_GROUP)])
            t_dt = dxdt * xs
            t_dk = dy * xs
            dec_stack = jnp.concatenate(decs, axis=0)
            dm_dec = dm_all * dec_stack
            e_all = dm_dec * jnp.concatenate([cb] * HEADS_PER_GROUP, axis=0)
            dcb = functools.reduce(jnp.add, [dm_dec[CHUNK * q:CHUNK * (q + 1)] for q in range(HEADS_PER_GROUP)])
            one = jnp.ones((), MXU_DTYPE)
            sel_lanes = jnp.where(q256 + 4 * g == lane_256, one, 0)
            sel_rows = jnp.where(q512 + 4 * g == lane_512, one, 0)
            sel_rows_t = jnp.where(row_512t == q512t + 4 * g, one, 0)
            e_lanes = jnp.concatenate([e_all[CHUNK * q:CHUNK * (q + 1)] for q in range(HEADS_PER_GROUP)], axis=1)
            w_heads = _dot(t_w, sel_lanes)
            dacum_c = dacum_c + _dot(e_lanes, sel_rows) + _dot(t_off, sel_lanes) - w_heads
            dacum_r = dacum_r + _dot(sel_rows_t, e_all)
            ddt_c = ddt_c + _dot(t_dt, sel_lanes)
            d_aend = d_aend + _colsum(w_heads) + e_end * _colsum(_dot_tn(gs, sel_lanes))
            d_dsk = d_dsk + _colsum(_dot(t_dk, sel_lanes))
            dpre_ref[:, cs] = (dxdt * dtq + dsk_x[:, cs] * dy) * xs_grad
            dc_g = dc_g + _dot(dcb, b_g)
            db_g = db_g + _dot(dcb.T, c_g)
            dpre_ref[:, 1024 + SSM_STATE * g:1024 + SSM_STATE * (g + 1)] = db_g * b_grad
            dpre_ref[:, 1536 + SSM_STATE * g:1536 + SSM_STATE * (g + 1)] = dc_g * c_grad

        dacum = dacum_c - dacum_r.T + jnp.where(row == CHUNK - 1, d_aend, 0.0)
        dda = _dot_exact(triu.astype(F32), dacum)
        ddtr = (dda * a + ddt_c) * _sigmoid(dtr_ref[...] + dtb_ref[...])
        ddtr_ref[...] = ddtr.astype(MXU_DTYPE)
        dvec_ref[0:1, :] += _colsum(ddtr)
        dvec_ref[1:2, :] += _colsum(dda * dt)
        dvec_ref[2:3, :] += d_dsk

    return pl.pallas_call(
        body, name=name, grid=(nsteps,),
        out_shape=(jax.ShapeDtypeStruct((rows, CONV_DIM), F32), jax.ShapeDtypeStruct((rows, IN_MAIN), MXU_DTYPE),
                   jax.ShapeDtypeStruct((rows, LANES), MXU_DTYPE), jax.ShapeDtypeStruct((1, D_MODEL), F32),
                   jax.ShapeDtypeStruct((8, LANES), F32)),
        in_specs=[pl.BlockSpec((tr, CONV_DIM), lambda i: (rev(i), 0)), pl.BlockSpec((tr, LANES), lambda i: (rev(i), 0)),
                  pl.BlockSpec((tr, LANES), lambda i: (rev(i), 0)), pl.BlockSpec((tr, LANES), lambda i: (rev(i), 0)),
                  pl.BlockSpec((tr, D_MODEL), lambda i: (rev(i), 4)), pl.BlockSpec((tr, D_MODEL), lambda i: (rev(i), 0)),
                  pl.BlockSpec((cps, SSM_GROUPS, GROUP_DIM, SSM_STATE), lambda i: (rev(i), 0, 0, 0)),
                  pl.BlockSpec((tr, D_MODEL), lambda i: (rev(i), 1)),
                  _vec_spec(LANES), _vec_spec(LANES), _vec_spec(LANES), _vec_spec(D_MODEL),
                  pl.BlockSpec(memory_space=pl.ANY), _vec_spec(D_MODEL, LANES), _vec_spec(LANES, D_MODEL)],
        out_specs=(pl.BlockSpec((tr, CONV_DIM), lambda i: (rev(i), 0)), pl.BlockSpec((tr, D_MODEL), lambda i: (rev(i), 4)),
                   pl.BlockSpec((tr, LANES), lambda i: (rev(i), 0)), _vec_spec(D_MODEL), _vec_spec(LANES, 8)),
        scratch_shapes=[pltpu.VMEM((SSM_GROUPS, GROUP_DIM, SSM_STATE), F32)],
        input_output_aliases={12: 1},
        compiler_params=_cparams(("arbitrary",)),
    )(pre, dtr, dt_saved, acum_saved, proj, y_saved, states, dcat, dtb, alog, dsk, gn, dproj, *_head_selectors())


def _pool_counts(first_row, n_rows, win):
    t = first_row + lax.broadcasted_iota(jnp.int32, (n_rows, POOL_DIM), 0)
    return jnp.minimum(t + 1, win).astype(F32)


def _pool_fwd(yn, pool_w, pool_b, pool_scale, *, name):
    rows = yn.shape[0]
    tr = ROW_TILE
    hb = tr // POOL_HALO

    def body(y_ref, prev_ref, w_ref, b_ref, s_ref, pm_ref, diff_ref, buf):
        i = pl.program_id(0)
        buf[pl.ds(0, POOL_HALO), :] = jnp.where(i == 0, 0.0, prev_ref[...])
        buf[pl.ds(POOL_HALO, tr), :] = y_ref[...]
        level = buf[...]
        sums = []
        for g, win in enumerate(POOL_WINDOWS):
            level = level + pltpu.roll(level, win // 2, axis=0)
            sums.append(level[POOL_HALO:, :POOL_DIM])
            if g + 1 < len(POOL_WINDOWS):
                level = level[:, POOL_DIM:]
        for g, win in enumerate(POOL_WINDOWS):
            cs = slice(POOL_DIM * g, POOL_DIM * (g + 1))
            diff = (sums[g] / _pool_counts(i * tr, tr, win) - y_ref[:, cs]).astype(MXU_DTYPE)
            diff_ref[:, cs] = diff
            pm_ref[:, cs] = (_dot(diff, w_ref[g]) + b_ref[:, cs]) * s_ref[:, cs]

    return pl.pallas_call(
        body, name=name, grid=(rows // tr,),
        out_shape=(jax.ShapeDtypeStruct((rows, D_MODEL), F32), jax.ShapeDtypeStruct((rows, D_MODEL), MXU_DTYPE)),
        in_specs=[_row_spec(D_MODEL, tr),
                  pl.BlockSpec((POOL_HALO, D_MODEL), lambda i: (jnp.maximum(i * hb - 1, 0), 0)),
                  pl.BlockSpec((4, POOL_DIM, POOL_DIM), lambda i: (0, 0, 0)), _vec_spec(D_MODEL), _vec_spec(D_MODEL)],
        out_specs=(_row_spec(D_MODEL, tr), _row_spec(D_MODEL, tr)),
        scratch_shapes=[pltpu.VMEM((tr + POOL_HALO, D_MODEL), F32)],
        compiler_params=_cparams(("parallel",)),
    )(yn, yn, pool_w, pool_b, pool_scale)


def _pool_bwd(dpm, diff, pool_w, pool_w_t, pool_b, pool_scale, *, name):
    rows = dpm.shape[0]
    tr = ROW_TILE
    hb = tr // POOL_HALO
    nblk = rows // tr

    def body(d_ref, dnext_ref, diff_ref, w_ref, wt_ref, b_ref, s_ref, dy_ref, dw_ref, db_ref, ds_ref, ebuf):
        i = pl.program_id(0)

        @pl.when(i == 0)
        def _():
            dw_ref[...] = jnp.zeros_like(dw_ref)
            db_ref[...] = jnp.zeros_like(db_ref)
            ds_ref[...] = jnp.zeros_like(ds_ref)

        last = i == nblk - 1
        for g, win in enumerate(POOL_WINDOWS):
            cs = slice(POOL_DIM * g, POOL_DIM * (g + 1))
            d = d_ref[:, cs]
            diff = diff_ref[:, cs]
            out_pre = _dot(diff, w_ref[g]) + b_ref[:, cs]
            ds_ref[:, cs] += _colsum(d * out_pre)
            dout = d * s_ref[:, cs]
            db_ref[:, cs] += _colsum(dout)
            dw_ref[g] += _dot_tn(diff, dout)
            ddiff = _dot(dout, wt_ref[g])
            ddiff_next = _dot(jnp.where(last, 0.0, dnext_ref[:, cs]) * s_ref[:, cs], wt_ref[g])
            ebuf[pl.ds(0, tr), cs] = ddiff / _pool_counts(i * tr, tr, win)
            ebuf[pl.ds(tr, POOL_HALO), cs] = ddiff_next / _pool_counts((i + 1) * tr, POOL_HALO, win)
            dy_ref[:, cs] = -ddiff
        level = ebuf[...]
        n = tr + POOL_HALO
        for g, win in enumerate(POOL_WINDOWS):
            level = level + pltpu.roll(level, n - win // 2, axis=0)
            dy_ref[:, POOL_DIM * g:POOL_DIM * (g + 1)] += level[:tr, :POOL_DIM]
            if g + 1 < len(POOL_WINDOWS):
                level = level[:, POOL_DIM:]

    return pl.pallas_call(
        body, name=name, grid=(nblk,),
        out_shape=(jax.ShapeDtypeStruct((rows, D_MODEL), F32), jax.ShapeDtypeStruct((4, POOL_DIM, POOL_DIM), F32),
                   jax.ShapeDtypeStruct((1, D_MODEL), F32), jax.ShapeDtypeStruct((1, D_MODEL), F32)),
        in_specs=[_row_spec(D_MODEL, tr),
                  pl.BlockSpec((POOL_HALO, D_MODEL), lambda i: (jnp.minimum((i + 1) * hb, rows // POOL_HALO - 1), 0)),
                  _row_spec(D_MODEL, tr),
                  pl.BlockSpec((4, POOL_DIM, POOL_DIM), lambda i: (0, 0, 0)),
                  pl.BlockSpec((4, POOL_DIM, POOL_DIM), lambda i: (0, 0, 0)), _vec_spec(D_MODEL), _vec_spec(D_MODEL)],
        out_specs=(_row_spec(D_MODEL, tr), pl.BlockSpec((4, POOL_DIM, POOL_DIM), lambda i: (0, 0, 0)),
                   _vec_spec(D_MODEL), _vec_spec(D_MODEL)),
        scratch_shapes=[pltpu.VMEM((tr + POOL_HALO, D_MODEL), F32)],
        compiler_params=_cparams(("arbitrary",)),
    )(dpm, dpm, diff, pool_w, pool_w_t, pool_b, pool_scale)


def _row_tile(rows, cap, step):
    best = rows
    for t in range(step, min(rows, cap) + 1, step):
        if rows % t == 0:
            best = t
    return best if best <= cap else rows


def _sum8(recv, *, name):
    _, r, c = recv.shape
    step = 8 if recv.dtype == F32 else 16

    def body(r_ref, g_ref):
        g = r_ref[0].astype(F32)
        for j in range(1, N_DEV):
            g = g + r_ref[j].astype(F32)
        g_ref[...] = g

    if r % step == 0:
        tr = _row_tile(r, 256, step)
        grid, in_spec, out_spec = (r // tr,), pl.BlockSpec((N_DEV, tr, c), lambda i: (0, i, 0)), pl.BlockSpec((tr, c), lambda i: (i, 0))
    else:
        tc = 256
        grid, in_spec, out_spec = (c // tc,), pl.BlockSpec((N_DEV, r, tc), lambda i: (0, 0, i)), pl.BlockSpec((r, tc), lambda i: (0, i))
    return pl.pallas_call(
        body, name=name, grid=grid, out_shape=jax.ShapeDtypeStruct((r, c), F32),
        in_specs=[in_spec], out_specs=out_spec, compiler_params=_cparams(("parallel",)),
    )(recv)


def _adamw(g, w, m, v, *, name):
    rows, cols = w.shape
    tr = _row_tile(rows, max(8, (256 * 1024) // cols // 8 * 8), 8)
    c1 = 1.0 / (1.0 - ADAM_B1 ** ADAM_STEP)
    c2 = 1.0 / (1.0 - ADAM_B2 ** ADAM_STEP)

    def body(g_ref, w_ref, m_ref, v_ref, d_ref, mo_ref, vo_ref):
        g = g_ref[...]
        m_new = ADAM_B1 * m_ref[...] + (1.0 - ADAM_B1) * g
        v_new = ADAM_B2 * v_ref[...] + (1.0 - ADAM_B2) * (g * g)
        mo_ref[...] = m_new
        vo_ref[...] = v_new
        d_ref[...] = -ADAM_LR * ((m_new * c1) / (jnp.sqrt(v_new * c2) + ADAM_EPS) + ADAM_WD * w_ref[...])

    spec = pl.BlockSpec((tr, cols), lambda i: (i, 0))
    return pl.pallas_call(
        body, name=name, grid=(rows // tr,),
        out_shape=tuple(jax.ShapeDtypeStruct((rows, cols), F32) for _ in range(3)),
        in_specs=[spec] * 4, out_specs=(spec, spec, spec),
        compiler_params=_cparams(("parallel",)),
    )(g, w, m, v)


def _pad_rows(flat, mult):
    n = flat.shape[-1]
    pad = (-n) % mult
    if pad:
        flat = jnp.pad(flat, [(0, 0)] * (flat.ndim - 1) + [(0, pad)])
    return flat


def _pack_blocks(blocks, row_mult):
    flat = jnp.concatenate([_pad_rows(b.reshape(-1), LANES) for b in blocks])
    return _pad_rows(flat, LANES * row_mult).reshape(-1, LANES)


def _block_sizes(blocks):
    return [-(-math.prod(b.shape) // LANES) * LANES for b in blocks]


def _unpack_blocks(slab, like, lead=()):
    flat = slab.reshape(lead + (-1,))
    out, off = [], 0
    for b, size in zip(like, _block_sizes(like)):
        n = math.prod(b.shape)
        out.append(flat[..., off:off + n].reshape(lead + tuple(b.shape)))
        off += size
    return out


def _join_shards(gathered, axis):
    return jnp.concatenate([gathered[j] for j in range(N_DEV)], axis=axis)


def _split_shards(full, axis):
    return jnp.stack(jnp.split(full, N_DEV, axis=axis))


def _interleave_ff(w_gate, w_up):
    k = w_gate.shape[0]
    nt = D_FF // FF_TILE
    return jnp.stack([w_gate.reshape(k, nt, FF_TILE), w_up.reshape(k, nt, FF_TILE)], axis=2).reshape(k, 2 * D_FF)


def _row128(vec):
    return jnp.pad(vec.reshape(1, -1), ((0, 0), (0, LANES - vec.shape[-1])))


def kernel(x, norm_g, w_in, gm_ln_g, gm_ln_b, gm_ws, gm_bs, conv_w, conv_b, dt_bias, a_log, d_skip, ssm_norm_g, w_out, pool_w, pool_b, pool_scale, ffn_w_gate, ffn_w_up, ffn_w_down, loss_target, m_norm_g, m_w_in, m_gm_ln_g, m_gm_ln_b, m_gm_ws, m_gm_bs, m_conv_w, m_conv_b, m_dt_bias, m_a_log, m_d_skip, m_ssm_norm_g, m_w_out, m_pool_w, m_pool_b, m_pool_scale, m_ffn_w_gate, m_ffn_w_up, m_ffn_w_down, v_norm_g, v_w_in, v_gm_ln_g, v_gm_ln_b, v_gm_ws, v_gm_bs, v_conv_w, v_conv_b, v_dt_bias, v_a_log, v_d_skip, v_ssm_norm_g, v_w_out, v_pool_w, v_pool_b, v_pool_scale, v_ffn_w_gate, v_ffn_w_up, v_ffn_w_down):
    w_loc = dict(norm_g=norm_g, w_in=w_in, gm_ln_g=gm_ln_g, gm_ln_b=gm_ln_b, gm_ws=gm_ws, gm_bs=gm_bs, conv_w=conv_w,
                 conv_b=conv_b, dt_bias=dt_bias, a_log=a_log, d_skip=d_skip, ssm_norm_g=ssm_norm_g, w_out=w_out,
                 pool_w=pool_w, pool_b=pool_b, pool_scale=pool_scale, ffn_w_gate=ffn_w_gate, ffn_w_up=ffn_w_up,
                 ffn_w_down=ffn_w_down)
    m_loc = dict(zip(WEIGHTS, [m_norm_g, m_w_in, m_gm_ln_g, m_gm_ln_b, m_gm_ws, m_gm_bs, m_conv_w, m_conv_b, m_dt_bias,
                               m_a_log, m_d_skip, m_ssm_norm_g, m_w_out, m_pool_w, m_pool_b, m_pool_scale,
                               m_ffn_w_gate, m_ffn_w_up, m_ffn_w_down]))
    v_loc = dict(zip(WEIGHTS, [v_norm_g, v_w_in, v_gm_ln_g, v_gm_ln_b, v_gm_ws, v_gm_bs, v_conv_w, v_conv_b, v_dt_bias,
                               v_a_log, v_d_skip, v_ssm_norm_g, v_w_out, v_pool_w, v_pool_b, v_pool_scale,
                               v_ffn_w_gate, v_ffn_w_up, v_ffn_w_down]))

    small_blocks = [w_loc[n] for n in GATHER_F32]
    got = _gather_two_level([w_in[0].astype(MXU_DTYPE), _pack_blocks(small_blocks, 8)], name="gather_first")
    full = {n: w_loc[n] for n in WEIGHTS if SHARD_AXIS[n] is None}
    full['w_in'] = got[0].transpose(1, 0, 2).reshape(1, D_MODEL, -1)
    for n, g in zip(GATHER_F32, _unpack_blocks(got[1], small_blocks, (N_DEV,))):
        full[n] = _join_shards(g, SHARD_AXIS[n])
    shards = {n: w_loc[n].astype(MXU_DTYPE) for n in ('w_out', 'ffn_w_gate', 'ffn_w_up', 'ffn_w_down', 'pool_w')}

    loss_part, grad_x, grads, recv = _local_step(x[0], loss_target[0], full, shards)

    small = [n for n in WEIGHTS if n not in BIG_WEIGHTS]
    like = [w_loc[n] for n in small]
    slots = []
    for n in small:
        ax = SHARD_AXIS[n]
        g = grads[n].astype(F32)
        sh = _split_shards(g, ax) if ax is not None else jnp.broadcast_to(g[None], (N_DEV,) + g.shape)
        slots.append(_pad_rows(sh.reshape(N_DEV, -1), LANES))
    send_small = _pad_rows(jnp.concatenate(slots, axis=1), LANES * 8).reshape(N_DEV, -1, LANES)
    recv_small, = _exchange([send_small], ['slots'], name="exchange_last")

    g_small = _sum8(recv_small, name="sum_small")
    g_own = dict(zip(small, _unpack_blocks(g_small, like)))
    g_own['w_in'] = _sum8(recv['w_in'], name="sum_w_in").T[None]
    g_own['w_out'] = _sum8(recv['w_out'], name="sum_w_out")[None]
    g_own['ffn_w_gate'] = jnp.stack([_sum8(recv['ffn_w_gate'][l], name=f"sum_ffn{l}_gate").T for l in range(2)])
    g_own['ffn_w_up'] = jnp.stack([_sum8(recv['ffn_w_up'][l], name=f"sum_ffn{l}_up").T for l in range(2)])
    g_own['ffn_w_down'] = jnp.stack([_sum8(recv['ffn_w_down'][l], name=f"sum_ffn{l}_down") for l in range(2)])

    delta, m_new, v_new = {}, {}, {}
    pk = lambda d: _pack_blocks([d[n] for n in small], 8)
    d_s, m_s, v_s = _adamw(g_small, pk(w_loc), pk(m_loc), pk(v_loc), name="adamw_small")
    for dst, slab in ((delta, d_s), (m_new, m_s), (v_new, v_s)):
        dst.update(zip(small, _unpack_blocks(slab, like)))
    for n in BIG_WEIGHTS:
        shape = w_loc[n].shape
        two_d = lambda t: t.reshape(-1, shape[-1])
        res = _adamw(two_d(g_own[n]), two_d(w_loc[n]), two_d(m_loc[n]), two_d(v_loc[n]), name=f"adamw_{n}")
        delta[n], m_new[n], v_new[n] = (t.reshape(shape) for t in res)

    loss = lax.psum(loss_part[0, 0], ("x", "y", "c"))
    outs = [d[n] for d in (g_own, delta, m_new, v_new) for n in WEIGHTS]
    return (loss, grad_x[None], *outs)


def _local_step(h0, tgt, full, shards):
    gm_ln_g, gm_ln_b, gm_ws, gm_bs = full['gm_ln_g'], full['gm_ln_b'], full['gm_ws'], full['gm_bs']
    conv_b, dt_bias, a_log, d_skip, ssm_norm_g = (full['conv_b'], full['dt_bias'], full['a_log'], full['d_skip'],
                                                  full['ssm_norm_g'])
    w_in_f = full['w_in'][0]
    w_main = jnp.concatenate([w_in_f[:, 3072:5120], w_in_f[:, :3072]], axis=1)
    w_dt = jnp.pad(w_in_f[:, 5120:], ((0, 0), (0, LANES - SSM_HEADS)))
    ng = full['norm_g']

    def ffn_shards(layer):
        return [shards['ffn_w_gate'][layer], shards['ffn_w_up'][layer], shards['ffn_w_down'][layer]]

    def ffn_weights(got_gate, got_up, got_down):
        cols = lambda g: g.transpose(1, 0, 2).reshape(D_MODEL, D_FF)
        return _interleave_ff(cols(got_gate), cols(got_up)), got_down.reshape(D_FF, D_MODEL)

    w_gu, w_dn = [None, None], [None, None]
    causal = jnp.tril(jnp.ones((CHUNK, CHUNK), bool))
    wm = jnp.where(causal[None], gm_ws[0], 0.0).astype(MXU_DTYPE)
    wm_t = jnp.swapaxes(wm, 1, 2)
    bcol = jnp.pad(gm_bs[0].T, ((0, 0), (0, LANES - GM_HEADS)))
    conv_w8 = jnp.pad(full['conv_w'][0], ((0, 8 - SSM_CONV), (0, 0)))
    dtb, alog, dsk = _row128(dt_bias[0]), _row128(a_log[0]), _row128(d_skip[0])
    pool_b_f = full['pool_b'][0].reshape(1, D_MODEL)
    pool_s_f = full['pool_scale']

    def g_(layer, i):
        return ng[layer, i].reshape(1, D_MODEL)

    yn0 = _rn_fwd(h0, g_(0, 0), name="rn_fwd_0", out_dtype=MXU_DTYPE)
    proj, got = _mm(yn0, w_main, name="mm_in_proj", tm=2048,
                    ex=_Exchange([shards['w_out'][0]] + ffn_shards(0), ['gather'] * 4))
    w_out_f = got[0].reshape(-1, D_MODEL)
    w_gu[0], w_dn[0] = ffn_weights(*got[1:])
    dtr = _mm(yn0, w_dt, name="mm_in_proj_dt")
    pre = _conv_fwd(proj, conv_w8, conv_b, name="conv_fwd")
    cat = _gmlp_fwd(proj, gm_ln_g, gm_ln_b, wm, bcol, name="gmlp_fwd")
    cat, y_ssd, states, dt_ssd, acum_ssd = _ssd_fwd_grouped(pre, dtr, proj, dtb, alog, dsk, ssm_norm_g, cat,
                                                            name="ssd_fwd")
    o0 = _mm(cat, w_out_f, name="mm_out_proj", tm=1024, tn=1024)
    h1, yn1 = _resid_rn_fwd(h0, o0, g_(0, 1), g_(0, 2), name="resid_fwd_0a", next_dtype=MXU_DTYPE)
    (gu0, act0), got = _mm_swiglu(yn1, w_gu[0], name="mm_ffn0_gate_up",
                                  ex=_Exchange(ffn_shards(1) + [shards['pool_w'][0]], ['gather'] * 4))
    w_gu[1], w_dn[1] = ffn_weights(*got[:3])
    pool_w_f = got[3].transpose(1, 0, 2, 3).reshape(4, POOL_DIM, POOL_DIM)
    d0 = _mm(act0, w_dn[0], name="mm_ffn0_down", tm=1024, tn=1024)
    h2, yn2 = _resid_rn_fwd(h1, d0, g_(0, 3), g_(1, 0), name="resid_fwd_0b", next_dtype=F32)
    pm, pdiff = _pool_fwd(yn2, pool_w_f, pool_b_f, pool_s_f, name="pool_fwd")
    h3, yn3 = _resid_rn_fwd(h2, pm, g_(1, 1), g_(1, 2), name="resid_fwd_1a", next_dtype=MXU_DTYPE)
    gu1, act1 = _mm_swiglu(yn3, w_gu[1], name="mm_ffn1_gate_up")
    d1 = _mm(act1, w_dn[1], name="mm_ffn1_down", tm=1024, tn=1024)
    grads = {}
    recv = {'ffn_w_gate': [None, None], 'ffn_w_up': [None, None], 'ffn_w_down': [None, None]}
    dng = [[None] * 4 for _ in range(2)]
    dh4, loss_part, dd1, dng[1][3] = _resid_loss(h3, d1, g_(1, 3), tgt, name="resid_loss")

    def ffn_bwd(layer, dd, gu, act, yn):
        dw_dn = _mm_tn(act, dd, name=f"mm_ffn{layer}_dw_down", out_dtype=MXU_DTYPE, tm=1408, tn=1024)
        dgu = _mm_dswiglu(dd, w_dn[layer].T, gu, name=f"mm_ffn{layer}_dact")
        dw_g_t, dw_u_t = _mm_tn_gate_up(dgu, yn, name=f"mm_ffn{layer}_dw_gate_up", out_dtype=MXU_DTYPE)
        dyn, got = _mm(dgu, w_gu[layer].T, name=f"mm_ffn{layer}_dyn", tm=512, tn=1024,
                       ex=_Exchange([dw_g_t, dw_u_t, dw_dn], ['rows'] * 3))
        recv['ffn_w_gate'][layer], recv['ffn_w_up'][layer], recv['ffn_w_down'][layer] = got
        return dyn

    dyn3 = ffn_bwd(1, dd1, gu1, act1, yn3)
    dh3, dpm, dng[1][2], dng[1][1] = _resid_bwd_pre_post(dh4, [dyn3], h3, g_(1, 2), pm, g_(1, 1), name="resid_bwd_1b_1a",
                                                         out_dtype=F32)
    dyn2, d_pool_w, d_pool_b, d_pool_s = _pool_bwd(dpm, pdiff, pool_w_f, jnp.swapaxes(pool_w_f, 1, 2), pool_b_f, pool_s_f,
                                                   name="pool_bwd")
    dh2, dd0, dng[1][0], dng[0][3] = _resid_bwd_pre_post(dh3, [dyn2], h2, g_(1, 0), d0, g_(0, 3), name="resid_bwd_1a_0b",
                                                         out_dtype=MXU_DTYPE)
    dyn1 = ffn_bwd(0, dd0, gu0, act0, yn1)
    dh1, do0, dng[0][2], dng[0][1] = _resid_bwd_pre_post(dh2, [dyn1], h1, g_(0, 2), o0, g_(0, 1), name="resid_bwd_0b_0a",
                                                         out_dtype=MXU_DTYPE)
    d_w_out =_mm_tn(cat, do0, name="mm_out_proj_dw", out_dtype=MXU_DTYPE, tn=1024)
    dcat, got = _mm(do0, w_out_f.T, name="mm_out_proj_dx", tm=2048, tn=1024, ex=_Exchange([d_w_out], ['rows']))
    recv['w_out'] = got[0]
    dproj, d_wm, d_bcol, d_ln_g, d_ln_b = _gmlp_bwd(proj, dcat, gm_ln_g, gm_ln_b, wm, wm_t, bcol, name="gmlp_bwd")
    dpre, dproj, ddtr, d_gn, d_vec = _ssd_bwd_grouped(pre, dtr, dt_ssd, acum_ssd, proj, y_ssd, states, dcat, dtb, alog,
                                                      dsk, ssm_norm_g, dproj, name="ssd_bwd")
    dproj, d_conv_w8, d_conv_b = _conv_bwd(dpre, proj, conv_w8, dproj, name="conv_bwd")
    d_w_main_t = _mm_tn(dproj, yn0, name="mm_in_proj_dw", out_dtype=MXU_DTYPE, tn=1024, shift=3)
    d_w_dt_t = _mm_tn(ddtr, yn0, name="mm_in_proj_dt_dw", out_dtype=MXU_DTYPE, tn=1024)
    d_w_in_t = jnp.concatenate([d_w_main_t, d_w_dt_t[:SSM_HEADS]], axis=0).reshape(N_DEV, -1, D_MODEL)
    dyn0, got = _mm(dproj, w_main.T, name="mm_in_proj_dx", tm=512, tn=1024,
                    ex=_Exchange([d_w_in_t], ['slots']))
    recv['w_in'] = got[0]
    dyn0_dt = _mm(ddtr, w_dt.T, name="mm_in_proj_dt_dx")
    grad_x, dng[0][0] = _resid_bwd_pre(dh1, [dyn0, dyn0_dt], h0, g_(0, 0), name="resid_bwd_pre_0a")

    grads['norm_g'] = jnp.stack([jnp.concatenate(dng[l], axis=0) for l in range(2)])
    grads['gm_ln_g'], grads['gm_ln_b'] = d_ln_g, d_ln_b
    grads['gm_ws'] = d_wm[None]
    grads['gm_bs'] = d_bcol[:, :GM_HEADS].T[None]
    grads['conv_w'] = d_conv_w8[None, :SSM_CONV]
    grads['conv_b'] = d_conv_b
    grads['dt_bias'] = d_vec[0:1, :SSM_HEADS]
    grads['a_log'] = d_vec[1:2, :SSM_HEADS] * (-jnp.exp(a_log))
    grads['d_skip'] = d_vec[2:3, :SSM_HEADS]
    grads['ssm_norm_g'] = d_gn
    grads['pool_w'] = d_pool_w[None]
    grads['pool_b'] = d_pool_b.reshape(1, 4, POOL_DIM)
    grads['pool_scale'] = d_pool_s
    return loss_part, grad_x, grads, recv
```

```python
import functools
import math

import jax
import jax.numpy as jnp
from jax import lax
from jax.experimental import pallas as pl
from jax.experimental.pallas import tpu as pltpu

F32 = jnp.float32
MXU_DTYPE = jnp.bfloat16

N_DEV = 8
D_MODEL = 1024
EPS = 1e-6
GM_HEADS = 4
GM_HEAD_DIM = 256
CHUNK = 128
SSM_HEADS = 16
SSM_GROUPS = 4
SSM_STATE = 128
SSM_CONV = 4
CONV_DIM = 2048
POOL_WINDOWS = (2, 4, 8, 16)
POOL_DIM = 256
D_FF = 2816
FF_TILE = 256
IN_MAIN = 5120
LANES = 128
CONV_HALO = 8
POOL_HALO = 16
ADAM_LR, ADAM_B1, ADAM_B2, ADAM_EPS, ADAM_WD, ADAM_STEP = 0.001, 0.9, 0.999, 1e-08, 0.01, 10

VMEM_LIMIT = 56 * 1024 * 1024
ROW_TILE = 512
MM_TM = 2048

WEIGHTS = ['norm_g', 'w_in', 'gm_ln_g', 'gm_ln_b', 'gm_ws', 'gm_bs', 'conv_w', 'conv_b', 'dt_bias', 'a_log',
           'd_skip', 'ssm_norm_g', 'w_out', 'pool_w', 'pool_b', 'pool_scale', 'ffn_w_gate', 'ffn_w_up', 'ffn_w_down']
SHARD_AXIS = {'norm_g': 2, 'w_in': 2, 'gm_ln_g': None, 'gm_ln_b': None, 'gm_ws': None, 'gm_bs': None, 'conv_w': 2,
              'conv_b': None, 'dt_bias': None, 'a_log': None, 'd_skip': None, 'ssm_norm_g': None, 'w_out': 1,
              'pool_w': 2, 'pool_b': 2, 'pool_scale': 1, 'ffn_w_gate': 2, 'ffn_w_up': 2, 'ffn_w_down': 1}
GATHER_F32 =['norm_g', 'conv_w', 'pool_b', 'pool_scale']
BIG_WEIGHTS = ['w_in', 'w_out', 'ffn_w_gate', 'ffn_w_up', 'ffn_w_down']


def _cparams(sem=None):
    return pltpu.CompilerParams(dimension_semantics=sem, vmem_limit_bytes=VMEM_LIMIT)


def _dot(a, b):
    return jnp.dot(a.astype(MXU_DTYPE), b.astype(MXU_DTYPE), preferred_element_type=F32)


def _dot_nt(a, b):
    return lax.dot_general(a.astype(MXU_DTYPE), b.astype(MXU_DTYPE), (((1,), (1,)), ((), ())),
                           preferred_element_type=F32)


def _dot_tn(a, b):
    return lax.dot_general(a.astype(MXU_DTYPE), b.astype(MXU_DTYPE), (((0,), (0,)), ((), ())),
                           preferred_element_type=F32)


def _dot_exact(a, b):
    return jnp.dot(a, b, precision=lax.Precision.HIGHEST, preferred_element_type=F32)


def _sigmoid(x):
    return 1.0 / (1.0 + jnp.exp(-x))


def _silu(x):
    return x * _sigmoid(x)


def _silu_and_grad(x):
    s = _sigmoid(x)
    return x * s, s * (1.0 + x * (1.0 - s))


_GELU_C = math.sqrt(2.0 / math.pi)


def _gelu(x):
    return _gelu_and_grad(x)[0]


def _gelu_and_grad(x):
    x2 = x * x
    t = jnp.tanh(_GELU_C * x * (1.0 + 0.044715 * x2))
    half = 0.5 * (1.0 + t)
    return x * half, half + 0.5 * x * (1.0 - t * t) * (_GELU_C * (1.0 + 3.0 * 0.044715 * x2))


def _softplus(x):
    return jnp.maximum(x, 0.0) + jnp.log1p(jnp.exp(-jnp.abs(x)))


def _rms_scale(x):
    return lax.rsqrt(jnp.mean(x * x, axis=-1, keepdims=True) + EPS)


def _rms_bwd(dy, x, g):
    r = _rms_scale(x)
    xn = x * r
    dxn = dy * g
    dx = r * (dxn - xn * jnp.mean(dxn * xn, axis=-1, keepdims=True))
    return dx, dy * xn


def _colsum(x):
    return jnp.sum(x, axis=0, keepdims=True)


class _Exchange:
    def __init__(self, arrays, modes):
        self.arrays, self.modes, self.n = list(arrays), list(modes), len(arrays)
        self.blks = []
        for x, mode in zip(arrays, modes):
            if mode == 'gather':
                self.blks.append(tuple(x.shape))
            elif mode == 'slots':
                self.blks.append(tuple(x.shape[1:]))
            else:
                self.blks.append((x.shape[0] // N_DEV,) + tuple(x.shape[1:]))
        self.out_shape = [jax.ShapeDtypeStruct((N_DEV,) + blk, x.dtype) for x, blk in zip(arrays, self.blks)]
        self.in_specs = [pl.BlockSpec(memory_space=pl.ANY)] * self.n
        self.out_specs = [pl.BlockSpec(memory_space=pl.ANY) for _ in range(self.n)]
        n_sem = self.n * (N_DEV - 1)
        self.scratch = [pltpu.SemaphoreType.DMA((n_sem,)), pltpu.SemaphoreType.DMA((n_sem,)),
                        pltpu.SemaphoreType.DMA((self.n,))]

    def _copies(self, x_refs, out_refs, send_sems, recv_sems, local_sems, with_recvs):
        mx, my, mc = lax.axis_index("x"), lax.axis_index("y"), lax.axis_index("c")
        me = 4 * mx + 2 * my + mc

        def flip(v, bit):
            return 1 - v if bit else v

        def part(a, dev):
            if self.modes[a] == 'gather':
                return x_refs[a]
            if self.modes[a] == 'slots':
                return x_refs[a].at[dev]
            r = self.blks[a][0]
            return x_refs[a].at[pl.ds(pl.multiple_of(dev * r, 16), r)]

        sends, recvs, owns = [], [], []
        for k in (1, 2, 4, 6, 3, 5, 7):
            px, py, pc = flip(mx, (k >> 2) & 1), flip(my, (k >> 1) & 1), flip(mc, k & 1)
            peer = 4 * px + 2 * py + pc
            for a in range(self.n):
                sem = a * (N_DEV - 1) + k - 1
                sends.append(pltpu.make_async_remote_copy(
                    src_ref=part(a, peer), dst_ref=out_refs[a].at[me], send_sem=send_sems.at[sem],
                    recv_sem=recv_sems.at[sem], device_id=(px, py, pc), device_id_type=pl.DeviceIdType.MESH))
                if with_recvs:
                    recvs.append(pltpu.make_async_remote_copy(
                        src_ref=part(a, peer), dst_ref=out_refs[a].at[peer], send_sem=send_sems.at[sem],
                        recv_sem=recv_sems.at[sem], device_id=(px, py, pc), device_id_type=pl.DeviceIdType.MESH))
        for a in range(self.n):
            owns.append(pltpu.make_async_copy(part(a, me), out_refs[a].at[me], local_sems.at[a]))
        return sends, recvs, owns

    def start(self, *refs):
        sends, _, owns = self._copies(*refs, with_recvs=False)
        for cp in sends + owns:
            cp.start()

    def wait(self, *refs):
        sends, recvs, owns = self._copies(*refs, with_recvs=True)
        for cp in recvs:
            cp.wait_recv()
        for cp in sends:
            cp.wait_send()
        for cp in owns:
            cp.wait()


def _exchange(arrays, modes, *, name):
    ex = _Exchange(arrays, modes)

    def body(*refs):
        x_refs, out_refs, sems = refs[:ex.n], refs[ex.n:2 * ex.n], refs[2 * ex.n:]
        ex.start(x_refs, out_refs, *sems)
        ex.wait(x_refs, out_refs, *sems)

    return pl.pallas_call(
        body, name=name, out_shape=tuple(ex.out_shape), in_specs=ex.in_specs, out_specs=tuple(ex.out_specs),
        scratch_shapes=ex.scratch,
    )(*arrays)


def _gather_two_level(arrays, *, name):
    n = len(arrays)
    per = N_DEV - 1

    def body(*refs):
        x_refs, out_refs = refs[:n], refs[n:2 * n]
        send_sems, recv_sems, local_sems = refs[2 * n:]
        x, y, c = lax.axis_index("x"), lax.axis_index("y"), lax.axis_index("c")
        me, sibling = (x, y, c), (x, y, 1 - c)
        chips = [(1 - x, y), (x, 1 - y), (1 - x, 1 - y)]

        def copy(a, k, block, to, src=None):
            slot = out_refs[a].at[4 * block[0] + 2 * block[1] + block[2]]
            return pltpu.make_async_remote_copy(
                src_ref=slot if src is None else src, dst_ref=slot, send_sem=send_sems.at[a * per + k],
                recv_sem=recv_sems.at[a * per + k], device_id=to, device_id_type=pl.DeviceIdType.MESH)

        mines = [pltpu.make_async_copy(x_refs[a], out_refs[a].at[4 * x + 2 * y + c], local_sems.at[a]) for a in range(n)]
        firsts = []
        for a in range(n):
            firsts.append(copy(a, 0, me, sibling, src=x_refs[a]))
            firsts += [copy(a, 1 + j, me, (*chip, c), src=x_refs[a]) for j, chip in enumerate(chips)]
        for cp in mines + firsts:
            cp.start()
        passed = []
        for j, chip in enumerate(chips):
            for a in range(n):
                copy(a, 1 + j, (*chip, c), me).wait_recv()
                passed.append(copy(a, 4 + j, (*chip, c), sibling))
                passed[-1].start()
        for a in range(n):
            copy(a, 0, sibling, me).wait_recv()
            for j, chip in enumerate(chips):
                copy(a, 4 + j, (*chip, 1 - c), me).wait_recv()
        for cp in firsts + passed:
            cp.wait_send()
        for cp in mines:
            cp.wait()

    return pl.pallas_call(
        body, name=name,
        out_shape=tuple(jax.ShapeDtypeStruct((N_DEV,) + tuple(a.shape), a.dtype) for a in arrays),
        in_specs=[pl.BlockSpec(memory_space=pl.ANY)] * n,
        out_specs=tuple(pl.BlockSpec(memory_space=pl.ANY) for _ in range(n)),
        scratch_shapes=[pltpu.SemaphoreType.DMA((n * per,)), pltpu.SemaphoreType.DMA((n * per,)),
                        pltpu.SemaphoreType.DMA((n,))],
    )(*arrays)


def _hosted(body, n_in, n_out, n_scratch, grid, ex):
    def wrapped(*refs):
        ins, x_refs = refs[:n_in], refs[n_in:n_in + ex.n]
        outs = refs[n_in + ex.n:n_in + ex.n + n_out]
        xo_refs = refs[n_in + ex.n + n_out:n_in + 2 * ex.n + n_out]
        scr = refs[n_in + 2 * ex.n + n_out:n_in + 2 * ex.n + n_out + n_scratch]
        sems = refs[n_in + 2 * ex.n + n_out + n_scratch:]
        ids = [pl.program_id(d) for d in range(len(grid))]
        first = functools.reduce(jnp.logical_and, [i == 0 for i in ids])
        last = functools.reduce(jnp.logical_and, [i == g - 1 for i, g in zip(ids, grid)])

        @pl.when(first)
        def _():
            ex.start(x_refs, xo_refs, *sems)

        body(*ins, *outs, *scr)

        @pl.when(last)
        def _():
            ex.wait(x_refs, xo_refs, *sems)

    return wrapped


def _call(body, *, name, grid, inputs, in_specs, out_shape, out_specs, scratch, semantics, ex=None):
    if ex is None:
        return pl.pallas_call(
            body, name=name, grid=grid, out_shape=tuple(out_shape), in_specs=list(in_specs),
            out_specs=tuple(out_specs), scratch_shapes=list(scratch), compiler_params=_cparams(semantics))(*inputs)
    n_out = len(out_shape)
    res = pl.pallas_call(
        _hosted(body, len(inputs), n_out, len(scratch), grid, ex), name=name, grid=grid,
        out_shape=tuple(out_shape) + tuple(ex.out_shape), in_specs=list(in_specs) + ex.in_specs,
        out_specs=tuple(out_specs) + tuple(ex.out_specs), scratch_shapes=list(scratch) + ex.scratch,
        compiler_params=_cparams(("arbitrary",) * len(grid)))(*inputs, *ex.arrays)
    return res[:n_out], res[n_out:]


def _mm(a, b, *, name, out_dtype=F32, tm=MM_TM, tn=512, tk=None, ex=None):
    m, k = a.shape
    n = b.shape[1]
    tm, tn = min(tm, m), min(tn, n)
    tk = k if tk is None else tk
    nk = k // tk
    assert m % tm == 0 and n % tn == 0 and k % tk == 0

    def body(a_ref, b_ref, o_ref, acc_ref):
        kk = pl.program_id(2)
        part = _dot(a_ref[...], b_ref[...])
        if nk == 1:
            o_ref[...] = part.astype(out_dtype)
        else:
            @pl.when(kk == 0)
            def _():
                acc_ref[...] = part

            @pl.when(kk > 0)
            def _():
                acc_ref[...] += part

            @pl.when(kk == nk - 1)
            def _():
                o_ref[...] = acc_ref[...].astype(out_dtype)

    res = _call(
        body, name=name, grid=(m // tm, n // tn, nk), inputs=(a, b),
        in_specs=[pl.BlockSpec((tm, tk), lambda i, j, kk: (i, kk)), pl.BlockSpec((tk, tn), lambda i, j, kk: (kk, j))],
        out_shape=[jax.ShapeDtypeStruct((m, n), out_dtype)],
        out_specs=[pl.BlockSpec((tm, tn), lambda i, j, kk: (i, j))],
        scratch=[pltpu.VMEM((tm, tn) if nk > 1 else (8, LANES), F32)],
        semantics=("parallel", "parallel", "arbitrary"), ex=ex)
    return res[0] if ex is None else (res[0][0], res[1])


def _mm_tn(a, b, *, name, out_dtype=F32, tm=1024, tn=512, tk=1024, shift=0):
    t, m = a.shape
    n = b.shape[1]
    tm, tn, tk = min(tm, m), min(tn, n), min(tk, t)
    nk = t // tk
    nb = m // tm
    assert m % tm == 0 and n % tn == 0 and t % tk == 0

    def body(a_ref, b_ref, o_ref, acc_ref):
        kk = pl.program_id(2)
        part = _dot_tn(a_ref[...], b_ref[...])

        @pl.when(kk == 0)
        def _():
            acc_ref[...] = part

        @pl.when(kk > 0)
        def _():
            acc_ref[...] += part

        @pl.when(kk == nk - 1)
        def _():
            o_ref[...] = acc_ref[...].astype(out_dtype)

    return pl.pallas_call(
        body, name=name, grid=(nb, n // tn, nk),
        out_shape=jax.ShapeDtypeStruct((m, n), out_dtype),
        in_specs=[pl.BlockSpec((tk, tm), lambda i, j, kk: (kk, i)), pl.BlockSpec((tk, tn), lambda i, j, kk: (kk, j))],
        out_specs=pl.BlockSpec((tm, tn), lambda i, j, kk: ((i + shift) % nb, j)),
        scratch_shapes=[pltpu.VMEM((tm, tn), F32)],
        compiler_params=_cparams(("parallel", "parallel", "arbitrary")),
    )(a, b)


def _mm_tn_gate_up(dgu, yn, *, name, out_dtype, tk=2048):
    t, m = dgu.shape
    n = yn.shape[1]
    tk = min(tk, t)
    nk = t // tk
    nb = m // (2 * FF_TILE)

    def body(a_ref, b_ref, og_ref, ou_ref, acc_ref):
        kk = pl.program_id(1)
        part = _dot_tn(a_ref[...], b_ref[...])

        @pl.when(kk == 0)
        def _():
            acc_ref[...] = part

        @pl.when(kk > 0)
        def _():
            acc_ref[...] += part

        @pl.when(kk == nk - 1)
        def _():
            og_ref[...] = acc_ref[:FF_TILE, :].astype(out_dtype)
            ou_ref[...] = acc_ref[FF_TILE:, :].astype(out_dtype)

    out = jax.ShapeDtypeStruct((m // 2, n), out_dtype)
    o_spec = pl.BlockSpec((FF_TILE, n), lambda i, kk: (i, 0))
    return pl.pallas_call(
        body, name=name, grid=(nb, nk), out_shape=(out, out),
        in_specs=[pl.BlockSpec((tk, 2 * FF_TILE), lambda i, kk: (kk, i)), pl.BlockSpec((tk, n), lambda i, kk: (kk, 0))],
        out_specs=(o_spec, o_spec),
        scratch_shapes=[pltpu.VMEM((2 * FF_TILE, n), F32)],
        compiler_params=_cparams(("parallel", "arbitrary")),
    )(dgu, yn)


def _mm_swiglu(a, w_gu, *, name, tm=MM_TM, ex=None):
    m, k = a.shape
    n = w_gu.shape[1]
    nt = n // (2 * FF_TILE)
    tm = min(tm, m)

    def body(a_ref, b_ref, gu_ref, act_ref):
        gu = _dot(a_ref[...], b_ref[...])
        gu_ref[...] = gu.astype(MXU_DTYPE)
        act_ref[...] = (_silu(gu[:, :FF_TILE]) * gu[:, FF_TILE:]).astype(MXU_DTYPE)

    return _call(
        body, name=name, grid=(m // tm, nt), inputs=(a, w_gu),
        in_specs=[pl.BlockSpec((tm, k), lambda i, j: (i, 0)), pl.BlockSpec((k, 2 * FF_TILE), lambda i, j: (0, j))],
        out_shape=[jax.ShapeDtypeStruct((m, n), MXU_DTYPE), jax.ShapeDtypeStruct((m, n // 2), MXU_DTYPE)],
        out_specs=[pl.BlockSpec((tm, 2 * FF_TILE), lambda i, j: (i, j)), pl.BlockSpec((tm, FF_TILE), lambda i, j: (i, j))],
        scratch=[], semantics=("parallel", "parallel"), ex=ex)


def _mm_dswiglu(dd, w_down_t, gu, *, name, tm=MM_TM):
    m, k = dd.shape
    n = gu.shape[1]
    nt = n // (2 * FF_TILE)
    tm = min(tm, m)

    def body(d_ref, w_ref, gu_ref, o_ref):
        dact = _dot(d_ref[...], w_ref[...])
        gate, up = gu_ref[:, :FF_TILE].astype(F32), gu_ref[:, FF_TILE:].astype(F32)
        act_gate, act_grad = _silu_and_grad(gate)
        o_ref[:, :FF_TILE] = (dact * up * act_grad).astype(MXU_DTYPE)
        o_ref[:, FF_TILE:] = (dact * act_gate).astype(MXU_DTYPE)

    return pl.pallas_call(
        body, name=name, grid=(m // tm, nt),
        out_shape=jax.ShapeDtypeStruct((m, n), MXU_DTYPE),
        in_specs=[pl.BlockSpec((tm, k), lambda i, j: (i, 0)), pl.BlockSpec((k, FF_TILE), lambda i, j: (0, j)),
                  pl.BlockSpec((tm, 2 * FF_TILE), lambda i, j: (i, j))],
        out_specs=pl.BlockSpec((tm, 2 * FF_TILE), lambda i, j: (i, j)),
        compiler_params=_cparams(("parallel", "parallel")),
    )(dd, w_down_t, gu)


def _row_spec(width, tr=ROW_TILE):
    return pl.BlockSpec((tr, width), lambda i: (i, 0))


def _vec_spec(width, rows=1):
    return pl.BlockSpec((rows, width), lambda i: (0, 0))


def _rn_fwd(h, g, *, name, out_dtype):
    rows, d = h.shape

    def body(h_ref, g_ref, o_ref):
        x = h_ref[...]
        o_ref[...] = (x * _rms_scale(x) * g_ref[...]).astype(out_dtype)

    return pl.pallas_call(
        body, name=name, grid=(rows // ROW_TILE,),
        out_shape=jax.ShapeDtypeStruct((rows, d), out_dtype),
        in_specs=[_row_spec(d), _vec_spec(d)], out_specs=_row_spec(d),
        compiler_params=_cparams(("parallel",)),
    )(h, g)


def _resid_rn_fwd(h_in, o, g_post, g_next, *, name, next_dtype):
    rows, d = h_in.shape

    def body(h_ref, o_ref, gp_ref, gn_ref, ho_ref, yn_ref):
        ov = o_ref[...]
        h = h_ref[...] + ov * _rms_scale(ov) * gp_ref[...]
        ho_ref[...] = h
        yn_ref[...] = (h * _rms_scale(h) * gn_ref[...]).astype(next_dtype)

    return pl.pallas_call(
        body, name=name, grid=(rows // ROW_TILE,),
        out_shape=(jax.ShapeDtypeStruct((rows, d), F32), jax.ShapeDtypeStruct((rows, d), next_dtype)),
        in_specs=[_row_spec(d), _row_spec(d), _vec_spec(d), _vec_spec(d)],
        out_specs=(_row_spec(d), _row_spec(d)),
        compiler_params=_cparams(("parallel",)),
    )(h_in, o, g_post, g_next)


def _resid_loss(h_in, o, g_post, target, *, name):
    rows, d = h_in.shape

    def body(h_ref, o_ref, gp_ref, t_ref, dh_ref, loss_ref, do_ref, dg_ref):
        ov = o_ref[...]
        gp = gp_ref[...]
        err = h_ref[...] + ov * _rms_scale(ov) * gp - t_ref[...]
        dh = err * (1.0 / d)
        dh_ref[...] = dh
        do, dg = _rms_bwd(dh, ov, gp)
        do_ref[...] = do.astype(MXU_DTYPE)

        @pl.when(pl.program_id(0) == 0)
        def _():
            loss_ref[...] = jnp.zeros_like(loss_ref)
            dg_ref[...] = jnp.zeros_like(dg_ref)

        loss_ref[...] += 0.5 * jnp.sum(jnp.mean(err * err, axis=-1, keepdims=True), axis=0, keepdims=True)
        dg_ref[...] += _colsum(dg)

    return pl.pallas_call(
        body, name=name, grid=(rows // ROW_TILE,),
        out_shape=(jax.ShapeDtypeStruct((rows, d), F32), jax.ShapeDtypeStruct((1, 1), F32),
                   jax.ShapeDtypeStruct((rows, d), MXU_DTYPE), jax.ShapeDtypeStruct((1, d), F32)),
        in_specs=[_row_spec(d), _row_spec(d), _vec_spec(d), _row_spec(d)],
        out_specs=(_row_spec(d), pl.BlockSpec((1, 1), lambda i: (0, 0)), _row_spec(d), _vec_spec(d)),
        compiler_params=_cparams(("arbitrary",)),
    )(h_in, o, g_post, target)


def _resid_bwd_pre_post(dh, dyn_list, h_in, g_pre, o_prev, g_post_prev, *, name, out_dtype):
    rows, d = dh.shape
    n_dyn = len(dyn_list)

    def body(*refs):
        dh_ref, dyn_refs = refs[0], refs[1:1 + n_dyn]
        h_ref, g_ref, o_ref, gp_ref, out_ref, do_ref, dg_ref, dgp_ref = refs[1 + n_dyn:]
        dyn = dyn_refs[0][...].astype(F32)
        for r in dyn_refs[1:]:
            dyn = dyn + r[...].astype(F32)
        dx, dg = _rms_bwd(dyn, h_ref[...], g_ref[...])
        dh_in = dh_ref[...] + dx
        out_ref[...] = dh_in
        do, dgp = _rms_bwd(dh_in, o_ref[...], gp_ref[...])
        do_ref[...] = do.astype(out_dtype)

        @pl.when(pl.program_id(0) == 0)
        def _():
            dg_ref[...] = jnp.zeros_like(dg_ref)
            dgp_ref[...] = jnp.zeros_like(dgp_ref)

        dg_ref[...] += _colsum(dg)
        dgp_ref[...] += _colsum(dgp)

    return pl.pallas_call(
        body, name=name, grid=(rows // ROW_TILE,),
        out_shape=(jax.ShapeDtypeStruct((rows, d), F32), jax.ShapeDtypeStruct((rows, d), out_dtype),
                   jax.ShapeDtypeStruct((1, d), F32), jax.ShapeDtypeStruct((1, d), F32)),
        in_specs=[_row_spec(d)] + [_row_spec(d)] * n_dyn + [_row_spec(d), _vec_spec(d), _row_spec(d), _vec_spec(d)],
        out_specs=(_row_spec(d), _row_spec(d), _vec_spec(d), _vec_spec(d)),
        compiler_params=_cparams(("arbitrary",)),
    )(dh, *dyn_list, h_in, g_pre, o_prev, g_post_prev)


def _resid_bwd_pre(dh, dyn_list, h_in, g_pre, *, name):
    rows, d = dh.shape
    n_dyn = len(dyn_list)

    def body(*refs):
        dh_ref, dyn_refs, h_ref, g_ref, out_ref, dg_ref = refs[0], refs[1:1 + n_dyn], *refs[1 + n_dyn:]
        dyn = dyn_refs[0][...].astype(F32)
        for r in dyn_refs[1:]:
            dyn = dyn + r[...].astype(F32)
        dx, dg = _rms_bwd(dyn, h_ref[...], g_ref[...])
        out_ref[...] = dh_ref[...] + dx

        @pl.when(pl.program_id(0) == 0)
        def _():
            dg_ref[...] = jnp.zeros_like(dg_ref)

        dg_ref[...] += _colsum(dg)

    return pl.pallas_call(
        body, name=name, grid=(rows // ROW_TILE,),
        out_shape=(jax.ShapeDtypeStruct((rows, d), F32), jax.ShapeDtypeStruct((1, d), F32)),
        in_specs=[_row_spec(d)] + [_row_spec(d)] * n_dyn + [_row_spec(d), _vec_spec(d)],
        out_specs=(_row_spec(d), _vec_spec(d)),
        compiler_params=_cparams(("arbitrary",)),
    )(dh, *dyn_list, h_in, g_pre)


def _layer_norm_stats(x):
    mu = jnp.mean(x, axis=-1, keepdims=True)
    xc = x - mu
    rstd = lax.rsqrt(jnp.mean(xc * xc, axis=-1, keepdims=True) + EPS)
    return xc * rstd, rstd


def _gmlp_fwd(proj, ln_g, ln_b, wm, bcol, *, name):
    rows = proj.shape[0]
    tr = ROW_TILE

    def body(u_ref, v_ref, lg_ref, lb_ref, wm_ref, bc_ref, ya_ref):
        vhat, _ = _layer_norm_stats(_gelu(v_ref[...]))
        vl = (vhat * lg_ref[...] + lb_ref[...]).astype(MXU_DTYPE)
        gu = _gelu(u_ref[...])
        bc = bc_ref[...]
        for c in range(tr // CHUNK):
            rs = slice(c * CHUNK, (c + 1) * CHUNK)
            for h in range(GM_HEADS):
                cs = slice(h * GM_HEAD_DIM, (h + 1) * GM_HEAD_DIM)
                mixed = _dot(wm_ref[h], vl[rs, cs]) + bc[:, h:h + 1]
                ya_ref[rs, cs] = (gu[rs, cs] * mixed).astype(MXU_DTYPE)

    return pl.pallas_call(
        body, name=name, grid=(rows // tr,),
        out_shape=jax.ShapeDtypeStruct((rows, 2 * D_MODEL), MXU_DTYPE),
        in_specs=[pl.BlockSpec((tr, D_MODEL), lambda i: (i, 2)), pl.BlockSpec((tr, D_MODEL), lambda i: (i, 3)),
                  _vec_spec(D_MODEL), _vec_spec(D_MODEL),
                  pl.BlockSpec((GM_HEADS, CHUNK, CHUNK), lambda i: (0, 0, 0)), _vec_spec(LANES, CHUNK)],
        out_specs=_row_spec(D_MODEL, tr),
        compiler_params=_cparams(("parallel",)),
    )(proj, proj, ln_g, ln_b, wm, bcol)


def _gmlp_bwd(proj, dcat, ln_g, ln_b, wm, wm_t, bcol, *, name):
    rows = proj.shape[0]
    tr = ROW_TILE

    def body(u_ref, v_ref, dy_ref, lg_ref, lb_ref, wm_ref, wmt_ref, bc_ref,
             duv_ref, dwm_ref, dbc_ref, dlg_ref, dlb_ref, dvl_scr):
        @pl.when(pl.program_id(0) == 0)
        def _():
            dwm_ref[...] = jnp.zeros_like(dwm_ref)
            dbc_ref[...] = jnp.zeros_like(dbc_ref)
            dlg_ref[...] = jnp.zeros_like(dlg_ref)
            dlb_ref[...] = jnp.zeros_like(dlb_ref)

        gv, gv_grad = _gelu_and_grad(v_ref[...])
        vhat, rstd = _layer_norm_stats(gv)
        lg = lg_ref[...]
        vl = (vhat * lg + lb_ref[...]).astype(MXU_DTYPE)
        gu, gu_grad = _gelu_and_grad(u_ref[...])
        dy = dy_ref[...].astype(F32)
        bc = bc_ref[...]
        row = lax.broadcasted_iota(jnp.int32, (CHUNK, CHUNK), 0)
        lane = lax.broadcasted_iota(jnp.int32, (CHUNK, CHUNK), 1)
        causal = lane <= row
        dbc = jnp.zeros((CHUNK, LANES), F32)
        for c in range(tr // CHUNK):
            rs = slice(c * CHUNK, (c + 1) * CHUNK)
            for h in range(GM_HEADS):
                cs = slice(h * GM_HEAD_DIM, (h + 1) * GM_HEAD_DIM)
                vl_h = vl[rs, cs]
                mixed = _dot(wm_ref[h], vl_h) + bc[:, h:h + 1]
                dy_h = dy[rs, cs]
                duv_ref[rs, cs] = (dy_h * mixed * gu_grad[rs, cs]).astype(MXU_DTYPE)
                dmixed = dy_h * gu[rs, cs]
                dwm_ref[h] += jnp.where(causal, _dot_nt(dmixed, vl_h), 0.0)
                dbc = dbc + jnp.where(lane == h, jnp.sum(dmixed, axis=1, keepdims=True), 0.0)
                dvl_scr[rs, cs] = _dot(wmt_ref[h], dmixed)
        dbc_ref[...] += dbc
        dvl = dvl_scr[...]
        dlg_ref[...] += _colsum(dvl * vhat)
        dlb_ref[...] += _colsum(dvl)
        dvh = dvl * lg
        dgv = rstd * (dvh - jnp.mean(dvh, axis=-1, keepdims=True) - vhat * jnp.mean(dvh * vhat, axis=-1, keepdims=True))
        duv_ref[:, D_MODEL:] = (dgv * gv_grad).astype(MXU_DTYPE)

    return pl.pallas_call(
        body, name=name, grid=(rows // tr,),
        out_shape=(jax.ShapeDtypeStruct((rows, IN_MAIN), MXU_DTYPE),
                   jax.ShapeDtypeStruct((GM_HEADS, CHUNK, CHUNK), F32), jax.ShapeDtypeStruct((CHUNK, LANES), F32),
                   jax.ShapeDtypeStruct((1, D_MODEL), F32), jax.ShapeDtypeStruct((1, D_MODEL), F32)),
        in_specs=[pl.BlockSpec((tr, D_MODEL), lambda i: (i, 2)), pl.BlockSpec((tr, D_MODEL), lambda i: (i, 3)),
                  pl.BlockSpec((tr, D_MODEL), lambda i: (i, 0)), _vec_spec(D_MODEL), _vec_spec(D_MODEL),
                  pl.BlockSpec((GM_HEADS, CHUNK, CHUNK), lambda i: (0, 0, 0)),
                  pl.BlockSpec((GM_HEADS, CHUNK, CHUNK), lambda i: (0, 0, 0)), _vec_spec(LANES, CHUNK)],
        out_specs=(pl.BlockSpec((tr, 2 * D_MODEL), lambda i: (i, 1)),
                   pl.BlockSpec((GM_HEADS, CHUNK, CHUNK), lambda i: (0, 0, 0)), _vec_spec(LANES, CHUNK),
                   _vec_spec(D_MODEL), _vec_spec(D_MODEL)),
        scratch_shapes=[pltpu.VMEM((tr, D_MODEL), F32)],
        compiler_params=_cparams(("arbitrary",)),
    )(proj, proj, dcat, ln_g, ln_b, wm, wm_t, bcol)


def _conv_fwd(proj, conv_w8, conv_b, *, name):
    rows = proj.shape[0]
    tr = ROW_TILE
    hb = tr // CONV_HALO

    def body(x_ref, prev_ref, w_ref, b_ref, pre_ref, buf):
        first = pl.program_id(0) == 0
        buf[pl.ds(0, CONV_HALO), :] = jnp.where(first, 0.0, prev_ref[...])
        buf[pl.ds(CONV_HALO, tr), :] = x_ref[...]
        ext = buf[...]
        acc = jnp.broadcast_to(b_ref[...], (tr, CONV_DIM))
        for k in range(SSM_CONV):
            s = SSM_CONV - 1 - k
            acc = acc + w_ref[k:k + 1, :] * (x_ref[...] if s == 0 else pltpu.roll(ext, s, axis=0)[CONV_HALO:])
        pre_ref[...] = acc

    return pl.pallas_call(
        body, name=name, grid=(rows // tr,),
        out_shape=jax.ShapeDtypeStruct((rows, CONV_DIM), F32),
        in_specs=[pl.BlockSpec((tr, CONV_DIM), lambda i: (i, 0)),
                  pl.BlockSpec((CONV_HALO, CONV_DIM), lambda i: (jnp.maximum(i * hb - 1, 0), 0)),
                  _vec_spec(CONV_DIM, 8), _vec_spec(CONV_DIM)],
        out_specs=_row_spec(CONV_DIM, tr),
        scratch_shapes=[pltpu.VMEM((tr + CONV_HALO, CONV_DIM), F32)],
        compiler_params=_cparams(("parallel",)),
    )(proj, proj, conv_w8, conv_b)


def _conv_bwd(dpre, proj, conv_w8, dproj, *, name):
    rows = proj.shape[0]
    tr = ROW_TILE
    halo = 16
    hb = tr // halo
    nblk = rows // tr

    def body(d_ref, dnext_ref, x_ref, w_ref, dproj_ref, dx_ref, dw_ref, db_ref, dbuf):
        i = pl.program_id(0)

        @pl.when(i == 0)
        def _():
            dw_ref[...] = jnp.zeros_like(dw_ref)
            db_ref[...] = jnp.zeros_like(db_ref)

        d = d_ref[...].astype(F32)
        x = x_ref[...]
        dbuf[pl.ds(0, tr), :] = d
        dbuf[pl.ds(tr, halo), :] = jnp.where(i == nblk - 1, 0.0, dnext_ref[...].astype(F32))
        ext = dbuf[...]
        acc = jnp.zeros((tr, CONV_DIM), F32)
        for k in range(SSM_CONV):
            s = SSM_CONV - 1 - k
            shifted = d if s == 0 else pltpu.roll(ext, tr + halo - s, axis=0)[:tr]
            acc = acc + w_ref[k:k + 1, :] * shifted
            dw_ref[k:k + 1, :] += _colsum(shifted * x)
        dx_ref[...] = acc.astype(MXU_DTYPE)
        db_ref[...] += _colsum(d)

    return pl.pallas_call(
        body, name=name, grid=(nblk,),
        out_shape=(jax.ShapeDtypeStruct((rows, IN_MAIN), MXU_DTYPE), jax.ShapeDtypeStruct((8, CONV_DIM), F32),
                   jax.ShapeDtypeStruct((1, CONV_DIM), F32)),
        in_specs=[_row_spec(CONV_DIM, tr),
                  pl.BlockSpec((halo, CONV_DIM), lambda i: (jnp.minimum((i + 1) * hb, rows // halo - 1), 0)),
                  pl.BlockSpec((tr, CONV_DIM), lambda i: (i, 0)),
                  _vec_spec(CONV_DIM, 8), pl.BlockSpec(memory_space=pl.ANY)],
        out_specs=(_row_spec(CONV_DIM, tr), _vec_spec(CONV_DIM, 8), _vec_spec(CONV_DIM)),
        scratch_shapes=[pltpu.VMEM((tr + halo, CONV_DIM), F32)],
        input_output_aliases={4: 0},
        compiler_params=_cparams(("arbitrary",)),
    )(dpre, dpre, proj, conv_w8, dproj)


def _chunk_iotas():
    row = lax.broadcasted_iota(jnp.int32, (CHUNK, CHUNK), 0)
    lane = lax.broadcasted_iota(jnp.int32, (CHUNK, CHUNK), 1)
    return row, lane, lane <= row


SSD_BWD_CHUNKS = 1
GROUP_DIM = D_MODEL // SSM_GROUPS
HEADS_PER_GROUP = SSM_HEADS // SSM_GROUPS
HEAD_DIM = GROUP_DIM // HEADS_PER_GROUP


def _split(x):
    hi = x.astype(MXU_DTYPE)
    return hi, (x - hi.astype(F32)).astype(MXU_DTYPE)


def _dot_split(x, sel):
    hi, lo = _split(x)
    return jnp.dot(hi, sel, preferred_element_type=F32) + jnp.dot(lo, sel, preferred_element_type=F32)


def _dot_split_rhs(sel, x):
    hi, lo = _split(x)
    return jnp.dot(sel, hi, preferred_element_type=F32) + jnp.dot(sel, lo, preferred_element_type=F32)


def _head_selectors():
    h = lax.broadcasted_iota(jnp.int32, (LANES, D_MODEL), 0)
    p = lax.broadcasted_iota(jnp.int32, (LANES, D_MODEL), 1)
    sel_t = (h == p // HEAD_DIM).astype(MXU_DTYPE)
    return sel_t, sel_t.T


def _expand_heads(per_head, e_end, selt_ref, sel_ref):
    stacked = jnp.concatenate(per_head, axis=0)
    wide = _dot_split(stacked, selt_ref[...])
    n = per_head[0].shape[0]
    e_cols = jnp.broadcast_to(e_end, (LANES, LANES)).T
    tall = _dot_split_rhs(sel_ref[...], e_cols)
    return [wide[n * i:n * (i + 1)] for i in range(len(per_head))], tall


def _by_quarter(index, pieces):
    out = pieces[3]
    for q in (2, 1, 0):
        out = jnp.where(index == q, pieces[q], out)
    return out


def _ssd_fwd_grouped(pre, dtr, proj, dtb, alog, dsk, gn, cat, *, name):
    rows = pre.shape[0]
    nc = rows // CHUNK

    def body(pre_ref, dtr_ref, z_ref, dtb_ref, alog_ref, dsk_ref, gn_ref, cat_ref, selt_ref, sel_ref, yb_ref, y_ref,
             st_ref, dt_ref, acum_ref, s_scr):
        @pl.when(pl.program_id(0) == 0)
        def _():
            s_scr[...] = jnp.zeros_like(s_scr)

        row, lane, tril = _chunk_iotas()
        dt = _softplus(dtr_ref[...] + dtb_ref[...])
        acum = _dot_exact(tril.astype(F32), dt * (-jnp.exp(alog_ref[...])))
        dt_ref[...] = dt
        acum_ref[...] = acum
        acum_t = acum.T
        a_end = acum[CHUNK - 1:CHUNK, :]
        (dt_x, eacum_x, dte_x, dsk_x), e_rows_all = _expand_heads(
            [dt, jnp.exp(acum), jnp.exp(a_end - acum), jnp.broadcast_to(dsk_ref[...], (CHUNK, LANES))], jnp.exp(a_end),
            selt_ref, sel_ref)
        lane_q = lax.broadcasted_iota(jnp.int32, (CHUNK, GROUP_DIM), 1) // HEAD_DIM

        for g in range(SSM_GROUPS):
            cs = slice(GROUP_DIM * g, GROUP_DIM * (g + 1))
            b_g = _silu(pre_ref[:, 1024 + SSM_STATE * g:1024 + SSM_STATE * (g + 1)]).astype(MXU_DTYPE)
            c_g = _silu(pre_ref[:, 1536 + SSM_STATE * g:1536 + SSM_STATE * (g + 1)]).astype(MXU_DTYPE)
            cb = _dot_nt(c_g, b_g)
            xs = _silu(pre_ref[:, cs])
            xdt = xs * dt_x[:, cs]
            m_stack = jnp.concatenate(
                [(cb * jnp.exp(jnp.where(tril, acum[:, h:h + 1] - acum_t[h:h + 1, :], -jnp.inf))).astype(MXU_DTYPE)
                 for h in range(4 * g, 4 * g + 4)], axis=0)
            y_all = _dot(m_stack, xdt)
            y = _by_quarter(lane_q, [y_all[CHUNK * q:CHUNK * (q + 1)] for q in range(HEADS_PER_GROUP)])
            s_prev = s_scr[g]
            st_ref[0, g] = s_prev
            y = y + _dot_nt(c_g, s_prev) * eacum_x[:, cs] + dsk_x[:, cs] * xs
            xw = xdt * dte_x[:, cs]
            s_scr[g] = e_rows_all[cs, :] * s_prev + _dot(xw.T, b_g)
            y_ref[:, cs] = y
            gated = y * _silu(z_ref[:, cs])
            r = lax.rsqrt(jnp.mean(gated * gated, axis=1, keepdims=True) + EPS)
            yb_ref[:, cs] = (gated * r * gn_ref[:, cs]).astype(MXU_DTYPE)

    return pl.pallas_call(
        body, name=name, grid=(nc,),
        out_shape=(jax.ShapeDtypeStruct((rows, 2 * D_MODEL), MXU_DTYPE), jax.ShapeDtypeStruct((rows, D_MODEL), F32),
                   jax.ShapeDtypeStruct((nc, SSM_GROUPS, GROUP_DIM, SSM_STATE), F32),
                   jax.ShapeDtypeStruct((rows, LANES), F32), jax.ShapeDtypeStruct((rows, LANES), F32)),
        in_specs=[_row_spec(CONV_DIM, CHUNK), _row_spec(LANES, CHUNK), pl.BlockSpec((CHUNK, D_MODEL), lambda i: (i, 4)),
                  _vec_spec(LANES), _vec_spec(LANES), _vec_spec(LANES), _vec_spec(D_MODEL),
                  pl.BlockSpec(memory_space=pl.ANY), _vec_spec(D_MODEL, LANES), _vec_spec(LANES, D_MODEL)],
        out_specs=(pl.BlockSpec((CHUNK, D_MODEL), lambda i: (i, 1)), _row_spec(D_MODEL, CHUNK),
                   pl.BlockSpec((1, SSM_GROUPS, GROUP_DIM, SSM_STATE), lambda i: (i, 0, 0, 0)),
                   _row_spec(LANES, CHUNK), _row_spec(LANES, CHUNK)),
        scratch_shapes=[pltpu.VMEM((SSM_GROUPS, GROUP_DIM, SSM_STATE), F32)],
        input_output_aliases={7: 0},
        compiler_params=_cparams(("arbitrary",)),
    )(pre, dtr, proj, dtb, alog, dsk, gn, cat, *_head_selectors())


def _ssd_bwd_grouped(pre, dtr, dt_saved, acum_saved, proj, y_saved, states, dcat, dtb, alog, dsk, gn, dproj, *, name):
    rows = pre.shape[0]
    cps = SSD_BWD_CHUNKS
    tr = CHUNK * cps
    nsteps = rows // tr

    def rev(i):
        return nsteps - 1 - i

    def body(pre_ref, dtr_ref, dt_ref, acum_ref, z_ref, y_ref, st_ref, dyb_ref, dtb_ref, alog_ref, dsk_ref, gn_ref,
             dproj_ref, selt_ref, sel_ref, dpre_ref, dz_ref, ddtr_ref, dgn_ref, dvec_ref, g_scr):
        @pl.when(pl.program_id(0) == 0)
        def _():
            g_scr[...] = jnp.zeros_like(g_scr)
            dgn_ref[...] = jnp.zeros_like(dgn_ref)
            dvec_ref[...] = jnp.zeros_like(dvec_ref)

        for cc in reversed(range(cps)):
            at = lambda ref: ref.at[pl.ds(cc * CHUNK, CHUNK)]
            chunk(at(pre_ref), at(dtr_ref), at(dt_ref), at(acum_ref), at(z_ref), at(y_ref), st_ref.at[cc], at(dyb_ref),
                  dtb_ref, alog_ref, dsk_ref, gn_ref, selt_ref, sel_ref, at(dpre_ref), at(dz_ref), at(ddtr_ref), dgn_ref,
                  dvec_ref, g_scr)

    def chunk(pre_ref, dtr_ref, dt_ref, acum_ref, z_ref, y_ref, st_ref, dyb_ref, dtb_ref, alog_ref, dsk_ref, gn_ref,
              selt_ref, sel_ref, dpre_ref, dz_ref, ddtr_ref, dgn_ref, dvec_ref, g_scr):
        row, lane, tril = _chunk_iotas()
        triu = lane >= row
        dt, acum = dt_ref[...], acum_ref[...]
        a = -jnp.exp(alog_ref[...])
        acum_t = acum.T
        a_end = acum[CHUNK - 1:CHUNK, :]
        e_end = jnp.exp(a_end)
        (dt_x, eacum_x, dte_x, dsk_x), e_rows_all = _expand_heads(
            [dt, jnp.exp(acum), jnp.exp(a_end - acum), jnp.broadcast_to(dsk_ref[...], (CHUNK, LANES))], e_end,
            selt_ref, sel_ref)
        lane_q = lax.broadcasted_iota(jnp.int32, (CHUNK, GROUP_DIM), 1) // HEAD_DIM
        zero = jnp.zeros((CHUNK, LANES), F32)
        dacum_c, dacum_r, ddt_c = zero, zero, zero
        d_aend = jnp.zeros((1, LANES), F32)
        d_dsk = jnp.zeros((1, LANES), F32)
        iota = lambda shape, dim: lax.broadcasted_iota(jnp.int32, shape, dim)
        q256, lane_256 = iota((GROUP_DIM, LANES), 0) // HEAD_DIM, iota((GROUP_DIM, LANES), 1)
        q512, lane_512 = iota((4 * CHUNK, LANES), 0) // CHUNK, iota((4 * CHUNK, LANES), 1)
        row_512t, q512t = iota((LANES, 4 * CHUNK), 0), iota((LANES, 4 * CHUNK), 1) // CHUNK

        for g in range(SSM_GROUPS):
            cs = slice(GROUP_DIM * g, GROUP_DIM * (g + 1))
            yv = y_ref[:, cs]
            sz, sz_grad = _silu_and_grad(z_ref[:, cs])
            gated = yv * sz
            dyb = dyb_ref[:, cs].astype(F32)
            dgh = dyb * gn_ref[:, cs]
            r = lax.rsqrt(jnp.mean(gated * gated, axis=1, keepdims=True) + EPS)
            dgn_ref[:, cs] += _colsum(dyb * gated * r)
            dgated = r * dgh - gated * (r * r * r * jnp.mean(dgh * gated, axis=1, keepdims=True))
            dy = dgated * sz
            dz_ref[:, cs] = (dgated * yv * sz_grad).astype(MXU_DTYPE)

            b_f, b_grad = _silu_and_grad(pre_ref[:, 1024 + SSM_STATE * g:1024 + SSM_STATE * (g + 1)])
            c_f, c_grad = _silu_and_grad(pre_ref[:, 1536 + SSM_STATE * g:1536 + SSM_STATE * (g + 1)])
            b_g, c_g = b_f.astype(MXU_DTYPE), c_f.astype(MXU_DTYPE)
            xs, xs_grad = _silu_and_grad(pre_ref[:, cs])
            dtq = dt_x[:, cs]
            xdt = xs * dtq
            xdt_m = xdt.astype(MXU_DTYPE)
            dy_m = dy.astype(MXU_DTYPE)
            s_prev = st_ref[g]
            g_next = g_scr[g]
            eacq, dteq = eacum_x[:, cs], dte_x[:, cs]
            t_off = dy * (_dot_nt(c_g, s_prev) * eacq)
            dye = dy * eacq
            dc_g = _dot(dye, s_prev)
            bg = _dot_nt(b_g, g_next)
            xw = xdt * dteq
            db_g = _dot(xw, g_next)
            t_w = xw * bg
            gs = g_next * s_prev
            g_scr[g] = e_rows_all[cs, :] * g_next + _dot(dye.T, c_g)
            cb = _dot_nt(c_g, b_g)
            cb_t = cb.T
            heads = range(4 * g, 4 * g + 4)
            decs = [jnp.exp(jnp.where(tril, acum[:, h:h + 1] - acum_t[h:h + 1, :], -jnp.inf)) for h in heads]
            mt_stack = jnp.concatenate(
                [(cb_t * jnp.exp(jnp.where(triu, acum_t[h:h + 1, :] - acum[:, h:h + 1], -jnp.inf))).astype(MXU_DTYPE)
                 for h in heads], axis=0)
            dy_stack = jnp.concatenate([jnp.where(lane_q == q, dy, 0.0).astype(MXU_DTYPE)
                                        for q in range(HEADS_PER_GROUP)], axis=0)
            dm_all = _dot_nt(dy_stack, xdt_m)
            dx_all = _dot(mt_stack, dy_m)
            dxdt = bg * dteq + _by_quarter(lane_q, [dx_all[CHUNK * q:CHUNK * (q + 1)] for q in range(HEADS_PER_GROUP)])
            t_dt = dxdt * xs
            t_dk = dy * xs
            dec_stack = jnp.concatenate(decs, axis=0)
            dm_dec = dm_all * dec_stack
            e_all = dm_dec * jnp.concatenate([cb] * HEADS_PER_GROUP, axis=0)
            dcb = functools.reduce(jnp.add, [dm_dec[CHUNK * q:CHUNK * (q + 1)] for q in range(HEADS_PER_GROUP)])
            one = jnp.ones((), MXU_DTYPE)
            sel_lanes = jnp.where(q256 + 4 * g == lane_256, one, 0)
            sel_rows = jnp.where(q512 + 4 * g == lane_512, one, 0)
            sel_rows_t = jnp.where(row_512t == q512t + 4 * g, one, 0)
            e_lanes = jnp.concatenate([e_all[CHUNK * q:CHUNK * (q + 1)] for q in range(HEADS_PER_GROUP)], axis=1)
            w_heads = _dot(t_w, sel_lanes)
            dacum_c = dacum_c + _dot(e_lanes, sel_rows) + _dot(t_off, sel_lanes) - w_heads
            dacum_r = dacum_r + _dot(sel_rows_t, e_all)
            ddt_c = ddt_c + _dot(t_dt, sel_lanes)
            d_aend = d_aend + _colsum(w_heads) + e_end * _colsum(_dot_tn(gs, sel_lanes))
            d_dsk = d_dsk + _colsum(_dot(t_dk, sel_lanes))
            dpre_ref[:, cs] = ((dxdt * dtq + dsk_x[:, cs] * dy) * xs_grad).astype(MXU_DTYPE)
            dc_g = dc_g + _dot(dcb, b_g)
            db_g = db_g + _dot(dcb.T, c_g)
            dpre_ref[:, 1024 + SSM_STATE * g:1024 + SSM_STATE * (g + 1)] = (db_g * b_grad).astype(MXU_DTYPE)
            dpre_ref[:, 1536 + SSM_STATE * g:1536 + SSM_STATE * (g + 1)] = (dc_g * c_grad).astype(MXU_DTYPE)

        dacum = dacum_c - dacum_r.T + jnp.where(row == CHUNK - 1, d_aend, 0.0)
        dda = _dot_exact(triu.astype(F32), dacum)
        ddtr = (dda * a + ddt_c) * _sigmoid(dtr_ref[...] + dtb_ref[...])
        ddtr_ref[...] = ddtr.astype(MXU_DTYPE)
        dvec_ref[0:1, :] += _colsum(ddtr)
        dvec_ref[1:2, :] += _colsum(dda * dt)
        dvec_ref[2:3, :] += d_dsk

    return pl.pallas_call(
        body, name=name, grid=(nsteps,),
        out_shape=(jax.ShapeDtypeStruct((rows, CONV_DIM), MXU_DTYPE), jax.ShapeDtypeStruct((rows, IN_MAIN), MXU_DTYPE),
                   jax.ShapeDtypeStruct((rows, LANES), MXU_DTYPE), jax.ShapeDtypeStruct((1, D_MODEL), F32),
                   jax.ShapeDtypeStruct((8, LANES), F32)),
        in_specs=[pl.BlockSpec((tr, CONV_DIM), lambda i: (rev(i), 0)), pl.BlockSpec((tr, LANES), lambda i: (rev(i), 0)),
                  pl.BlockSpec((tr, LANES), lambda i: (rev(i), 0)), pl.BlockSpec((tr, LANES), lambda i: (rev(i), 0)),
                  pl.BlockSpec((tr, D_MODEL), lambda i: (rev(i), 4)), pl.BlockSpec((tr, D_MODEL), lambda i: (rev(i), 0)),
                  pl.BlockSpec((cps, SSM_GROUPS, GROUP_DIM, SSM_STATE), lambda i: (rev(i), 0, 0, 0)),
                  pl.BlockSpec((tr, D_MODEL), lambda i: (rev(i), 1)),
                  _vec_spec(LANES), _vec_spec(LANES), _vec_spec(LANES), _vec_spec(D_MODEL),
                  pl.BlockSpec(memory_space=pl.ANY), _vec_spec(D_MODEL, LANES), _vec_spec(LANES, D_MODEL)],
        out_specs=(pl.BlockSpec((tr, CONV_DIM), lambda i: (rev(i), 0)), pl.BlockSpec((tr, D_MODEL), lambda i: (rev(i), 4)),
                   pl.BlockSpec((tr, LANES), lambda i: (rev(i), 0)), _vec_spec(D_MODEL), _vec_spec(LANES, 8)),
        scratch_shapes=[pltpu.VMEM((SSM_GROUPS, GROUP_DIM, SSM_STATE), F32)],
        input_output_aliases={12: 1},
        compiler_params=_cparams(("arbitrary",)),
    )(pre, dtr, dt_saved, acum_saved, proj, y_saved, states, dcat, dtb, alog, dsk, gn, dproj, *_head_selectors())


def _pool_counts(first_row, n_rows, win):
    t = first_row + lax.broadcasted_iota(jnp.int32, (n_rows, POOL_DIM), 0)
    return jnp.minimum(t + 1, win).astype(F32)


def _pool_fwd(yn, pool_w, pool_b, pool_scale, *, name):
    rows = yn.shape[0]
    tr = ROW_TILE
    hb = tr // POOL_HALO

    def body(y_ref, prev_ref, w_ref, b_ref, s_ref, pm_ref, diff_ref, buf):
        i = pl.program_id(0)
        buf[pl.ds(0, POOL_HALO), :] = jnp.where(i == 0, 0.0, prev_ref[...])
        buf[pl.ds(POOL_HALO, tr), :] = y_ref[...]
        level = buf[...]
        sums = []
        for g, win in enumerate(POOL_WINDOWS):
            level = level + pltpu.roll(level, win // 2, axis=0)
            sums.append(level[POOL_HALO:, :POOL_DIM])
            if g + 1 < len(POOL_WINDOWS):
                level = level[:, POOL_DIM:]
        for g, win in enumerate(POOL_WINDOWS):
            cs = slice(POOL_DIM * g, POOL_DIM * (g + 1))
            diff = (sums[g] / _pool_counts(i * tr, tr, win) - y_ref[:, cs]).astype(MXU_DTYPE)
            diff_ref[:, cs] = diff
            pm_ref[:, cs] = (_dot(diff, w_ref[g]) + b_ref[:, cs]) * s_ref[:, cs]

    return pl.pallas_call(
        body, name=name, grid=(rows // tr,),
        out_shape=(jax.ShapeDtypeStruct((rows, D_MODEL), F32), jax.ShapeDtypeStruct((rows, D_MODEL), MXU_DTYPE)),
        in_specs=[_row_spec(D_MODEL, tr),
                  pl.BlockSpec((POOL_HALO, D_MODEL), lambda i: (jnp.maximum(i * hb - 1, 0), 0)),
                  pl.BlockSpec((4, POOL_DIM, POOL_DIM), lambda i: (0, 0, 0)), _vec_spec(D_MODEL), _vec_spec(D_MODEL)],
        out_specs=(_row_spec(D_MODEL, tr), _row_spec(D_MODEL, tr)),
        scratch_shapes=[pltpu.VMEM((tr + POOL_HALO, D_MODEL), F32)],
        compiler_params=_cparams(("parallel",)),
    )(yn, yn, pool_w, pool_b, pool_scale)


def _pool_bwd(dpm, diff, pool_w, pool_w_t, pool_b, pool_scale, *, name):
    rows = dpm.shape[0]
    tr = ROW_TILE
    hb = tr // POOL_HALO
    nblk = rows // tr

    def body(d_ref, dnext_ref, diff_ref, w_ref, wt_ref, b_ref, s_ref, dy_ref, dw_ref, db_ref, ds_ref, ebuf):
        i = pl.program_id(0)

        @pl.when(i == 0)
        def _():
            dw_ref[...] = jnp.zeros_like(dw_ref)
            db_ref[...] = jnp.zeros_like(db_ref)
            ds_ref[...] = jnp.zeros_like(ds_ref)

        last = i == nblk - 1
        for g, win in enumerate(POOL_WINDOWS):
            cs = slice(POOL_DIM * g, POOL_DIM * (g + 1))
            d = d_ref[:, cs]
            diff = diff_ref[:, cs]
            out_pre = _dot(diff, w_ref[g]) + b_ref[:, cs]
            ds_ref[:, cs] += _colsum(d * out_pre)
            dout = d * s_ref[:, cs]
            db_ref[:, cs] += _colsum(dout)
            dw_ref[g] += _dot_tn(diff, dout)
            ddiff = _dot(dout, wt_ref[g])
            ddiff_next = _dot(jnp.where(last, 0.0, dnext_ref[:, cs]) * s_ref[:, cs], wt_ref[g])
            ebuf[pl.ds(0, tr), cs] = ddiff / _pool_counts(i * tr, tr, win)
            ebuf[pl.ds(tr, POOL_HALO), cs] = ddiff_next / _pool_counts((i + 1) * tr, POOL_HALO, win)
            dy_ref[:, cs] = -ddiff
        level = ebuf[...]
        n = tr + POOL_HALO
        for g, win in enumerate(POOL_WINDOWS):
            level = level + pltpu.roll(level, n - win // 2, axis=0)
            dy_ref[:, POOL_DIM * g:POOL_DIM * (g + 1)] += level[:tr, :POOL_DIM]
            if g + 1 < len(POOL_WINDOWS):
                level = level[:, POOL_DIM:]

    return pl.pallas_call(
        body, name=name, grid=(nblk,),
        out_shape=(jax.ShapeDtypeStruct((rows, D_MODEL), F32), jax.ShapeDtypeStruct((4, POOL_DIM, POOL_DIM), F32),
                   jax.ShapeDtypeStruct((1, D_MODEL), F32), jax.ShapeDtypeStruct((1, D_MODEL), F32)),
        in_specs=[_row_spec(D_MODEL, tr),
                  pl.BlockSpec((POOL_HALO, D_MODEL), lambda i: (jnp.minimum((i + 1) * hb, rows // POOL_HALO - 1), 0)),
                  _row_spec(D_MODEL, tr),
                  pl.BlockSpec((4, POOL_DIM, POOL_DIM), lambda i: (0, 0, 0)),
                  pl.BlockSpec((4, POOL_DIM, POOL_DIM), lambda i: (0, 0, 0)), _vec_spec(D_MODEL), _vec_spec(D_MODEL)],
        out_specs=(_row_spec(D_MODEL, tr), pl.BlockSpec((4, POOL_DIM, POOL_DIM), lambda i: (0, 0, 0)),
                   _vec_spec(D_MODEL), _vec_spec(D_MODEL)),
        scratch_shapes=[pltpu.VMEM((tr + POOL_HALO, D_MODEL), F32)],
        compiler_params=_cparams(("arbitrary",)),
    )(dpm, dpm, diff, pool_w, pool_w_t, pool_b, pool_scale)


def _row_tile(rows, cap, step):
    best = rows
    for t in range(step, min(rows, cap) + 1, step):
        if rows % t == 0:
            best = t
    return best if best <= cap else rows


def _sum8(recv, *, name):
    _, r, c = recv.shape
    step = 8 if recv.dtype == F32 else 16

    def body(r_ref, g_ref):
        g = r_ref[0].astype(F32)
        for j in range(1, N_DEV):
            g = g + r_ref[j].astype(F32)
        g_ref[...] = g

    if r % step == 0:
        tr = _row_tile(r, 256, step)
        grid, in_spec, out_spec = (r // tr,), pl.BlockSpec((N_DEV, tr, c), lambda i: (0, i, 0)), pl.BlockSpec((tr, c), lambda i: (i, 0))
    else:
        tc = 256
        grid, in_spec, out_spec = (c // tc,), pl.BlockSpec((N_DEV, r, tc), lambda i: (0, 0, i)), pl.BlockSpec((r, tc), lambda i: (0, i))
    return pl.pallas_call(
        body, name=name, grid=grid, out_shape=jax.ShapeDtypeStruct((r, c), F32),
        in_specs=[in_spec], out_specs=out_spec, compiler_params=_cparams(("parallel",)),
    )(recv)


def _adamw(g, w, m, v, *, name):
    rows, cols = w.shape
    tr = _row_tile(rows, max(8, (256 * 1024) // cols // 8 * 8), 8)
    c1 = 1.0 / (1.0 - ADAM_B1 ** ADAM_STEP)
    c2 = 1.0 / (1.0 - ADAM_B2 ** ADAM_STEP)

    def body(g_ref, w_ref, m_ref, v_ref, d_ref, mo_ref, vo_ref):
        g = g_ref[...]
        m_new = ADAM_B1 * m_ref[...] + (1.0 - ADAM_B1) * g
        v_new = ADAM_B2 * v_ref[...] + (1.0 - ADAM_B2) * (g * g)
        mo_ref[...] = m_new
        vo_ref[...] = v_new
        d_ref[...] = -ADAM_LR * ((m_new * c1) / (jnp.sqrt(v_new * c2) + ADAM_EPS) + ADAM_WD * w_ref[...])

    spec = pl.BlockSpec((tr, cols), lambda i: (i, 0))
    return pl.pallas_call(
        body, name=name, grid=(rows // tr,),
        out_shape=tuple(jax.ShapeDtypeStruct((rows, cols), F32) for _ in range(3)),
        in_specs=[spec] * 4, out_specs=(spec, spec, spec),
        compiler_params=_cparams(("parallel",)),
    )(g, w, m, v)


def _pad_rows(flat, mult):
    n = flat.shape[-1]
    pad = (-n) % mult
    if pad:
        flat = jnp.pad(flat, [(0, 0)] * (flat.ndim - 1) + [(0, pad)])
    return flat


def _pack_blocks(blocks, row_mult):
    flat = jnp.concatenate([_pad_rows(b.reshape(-1), LANES) for b in blocks])
    return _pad_rows(flat, LANES * row_mult).reshape(-1, LANES)


def _block_sizes(blocks):
    return [-(-math.prod(b.shape) // LANES) * LANES for b in blocks]


def _unpack_blocks(slab, like, lead=()):
    flat = slab.reshape(lead + (-1,))
    out, off = [], 0
    for b, size in zip(like, _block_sizes(like)):
        n = math.prod(b.shape)
        out.append(flat[..., off:off + n].reshape(lead + tuple(b.shape)))
        off += size
    return out


def _join_shards(gathered, axis):
    return jnp.concatenate([gathered[j] for j in range(N_DEV)], axis=axis)


def _split_shards(full, axis):
    return jnp.stack(jnp.split(full, N_DEV, axis=axis))


def _interleave_ff(w_gate, w_up):
    k = w_gate.shape[0]
    nt = D_FF // FF_TILE
    return jnp.stack([w_gate.reshape(k, nt, FF_TILE), w_up.reshape(k, nt, FF_TILE)], axis=2).reshape(k, 2 * D_FF)


def _row128(vec):
    return jnp.pad(vec.reshape(1, -1), ((0, 0), (0, LANES - vec.shape[-1])))


def kernel(x, norm_g, w_in, gm_ln_g, gm_ln_b, gm_ws, gm_bs, conv_w, conv_b, dt_bias, a_log, d_skip, ssm_norm_g, w_out, pool_w, pool_b, pool_scale, ffn_w_gate, ffn_w_up, ffn_w_down, loss_target, m_norm_g, m_w_in, m_gm_ln_g, m_gm_ln_b, m_gm_ws, m_gm_bs, m_conv_w, m_conv_b, m_dt_bias, m_a_log, m_d_skip, m_ssm_norm_g, m_w_out, m_pool_w, m_pool_b, m_pool_scale, m_ffn_w_gate, m_ffn_w_up, m_ffn_w_down, v_norm_g, v_w_in, v_gm_ln_g, v_gm_ln_b, v_gm_ws, v_gm_bs, v_conv_w, v_conv_b, v_dt_bias, v_a_log, v_d_skip, v_ssm_norm_g, v_w_out, v_pool_w, v_pool_b, v_pool_scale, v_ffn_w_gate, v_ffn_w_up, v_ffn_w_down):
    w_loc = dict(norm_g=norm_g, w_in=w_in, gm_ln_g=gm_ln_g, gm_ln_b=gm_ln_b, gm_ws=gm_ws, gm_bs=gm_bs, conv_w=conv_w,
                 conv_b=conv_b, dt_bias=dt_bias, a_log=a_log, d_skip=d_skip, ssm_norm_g=ssm_norm_g, w_out=w_out,
                 pool_w=pool_w, pool_b=pool_b, pool_scale=pool_scale, ffn_w_gate=ffn_w_gate, ffn_w_up=ffn_w_up,
                 ffn_w_down=ffn_w_down)
    m_loc = dict(zip(WEIGHTS, [m_norm_g, m_w_in, m_gm_ln_g, m_gm_ln_b, m_gm_ws, m_gm_bs, m_conv_w, m_conv_b, m_dt_bias,
                               m_a_log, m_d_skip, m_ssm_norm_g, m_w_out, m_pool_w, m_pool_b, m_pool_scale,
                               m_ffn_w_gate, m_ffn_w_up, m_ffn_w_down]))
    v_loc = dict(zip(WEIGHTS, [v_norm_g, v_w_in, v_gm_ln_g, v_gm_ln_b, v_gm_ws, v_gm_bs, v_conv_w, v_conv_b, v_dt_bias,
                               v_a_log, v_d_skip, v_ssm_norm_g, v_w_out, v_pool_w, v_pool_b, v_pool_scale,
                               v_ffn_w_gate, v_ffn_w_up, v_ffn_w_down]))

    small_blocks = [w_loc[n] for n in GATHER_F32]
    got = _gather_two_level([w_in[0].astype(MXU_DTYPE), _pack_blocks(small_blocks, 8)], name="gather_first")
    full = {n: w_loc[n] for n in WEIGHTS if SHARD_AXIS[n] is None}
    full['w_in'] = got[0].transpose(1, 0, 2).reshape(1, D_MODEL, -1)
    for n, g in zip(GATHER_F32, _unpack_blocks(got[1], small_blocks, (N_DEV,))):
        full[n] = _join_shards(g, SHARD_AXIS[n])
    shards = {n: w_loc[n].astype(MXU_DTYPE) for n in ('w_out', 'ffn_w_gate', 'ffn_w_up', 'ffn_w_down', 'pool_w')}

    loss_part, grad_x, grads, recv = _local_step(x[0], loss_target[0], full, shards)

    small = [n for n in WEIGHTS if n not in BIG_WEIGHTS]
    like = [w_loc[n] for n in small]
    slots = []
    for n in small:
        ax = SHARD_AXIS[n]
        g = grads[n].astype(F32)
        sh = _split_shards(g, ax) if ax is not None else jnp.broadcast_to(g[None], (N_DEV,) + g.shape)
        slots.append(_pad_rows(sh.reshape(N_DEV, -1), LANES))
    send_small = _pad_rows(jnp.concatenate(slots, axis=1), LANES * 8).reshape(N_DEV, -1, LANES)
    recv_small, = _exchange([send_small], ['slots'], name="exchange_last")

    g_small = _sum8(recv_small, name="sum_small")
    g_own = dict(zip(small, _unpack_blocks(g_small, like)))
    g_own['w_in'] = _sum8(recv['w_in'], name="sum_w_in").T[None]
    g_own['w_out'] = _sum8(recv['w_out'], name="sum_w_out")[None]
    g_own['ffn_w_gate'] = jnp.stack([_sum8(recv['ffn_w_gate'][l], name=f"sum_ffn{l}_gate").T for l in range(2)])
    g_own['ffn_w_up'] = jnp.stack([_sum8(recv['ffn_w_up'][l], name=f"sum_ffn{l}_up").T for l in range(2)])
    g_own['ffn_w_down'] = jnp.stack([_sum8(recv['ffn_w_down'][l], name=f"sum_ffn{l}_down") for l in range(2)])

    delta, m_new, v_new = {}, {}, {}
    pk = lambda d: _pack_blocks([d[n] for n in small], 8)
    d_s, m_s, v_s = _adamw(g_small, pk(w_loc), pk(m_loc), pk(v_loc), name="adamw_small")
    for dst, slab in ((delta, d_s), (m_new, m_s), (v_new, v_s)):
        dst.update(zip(small, _unpack_blocks(slab, like)))
    for n in BIG_WEIGHTS:
        shape = w_loc[n].shape
        two_d = lambda t: t.reshape(-1, shape[-1])
        res = _adamw(two_d(g_own[n]), two_d(w_loc[n]), two_d(m_loc[n]), two_d(v_loc[n]), name=f"adamw_{n}")
        delta[n], m_new[n], v_new[n] = (t.reshape(shape) for t in res)

    loss = lax.psum(loss_part[0, 0], ("x", "y", "c"))
    outs = [d[n] for d in (g_own, delta, m_new, v_new) for n in WEIGHTS]
    return (loss, grad_x[None], *outs)


def _local_step(h0, tgt, full, shards):
    gm_ln_g, gm_ln_b, gm_ws, gm_bs = full['gm_ln_g'], full['gm_ln_b'], full['gm_ws'], full['gm_bs']
    conv_b, dt_bias, a_log, d_skip, ssm_norm_g = (full['conv_b'], full['dt_bias'], full['a_log'], full['d_skip'],
                                                  full['ssm_norm_g'])
    w_in_f = full['w_in'][0]
    w_main = jnp.concatenate([w_in_f[:, 3072:5120], w_in_f[:, :3072]], axis=1)
    w_dt = jnp.pad(w_in_f[:, 5120:], ((0, 0), (0, LANES - SSM_HEADS)))
    ng = full['norm_g']

    def ffn_shards(layer):
        return [shards['ffn_w_gate'][layer], shards['ffn_w_up'][layer], shards['ffn_w_down'][layer]]

    def ffn_weights(got_gate, got_up, got_down):
        cols = lambda g: g.transpose(1, 0, 2).reshape(D_MODEL, D_FF)
        return _interleave_ff(cols(got_gate), cols(got_up)), got_down.reshape(D_FF, D_MODEL)

    w_gu, w_dn = [None, None], [None, None]
    causal = jnp.tril(jnp.ones((CHUNK, CHUNK), bool))
    wm = jnp.where(causal[None], gm_ws[0], 0.0).astype(MXU_DTYPE)
    wm_t = jnp.swapaxes(wm, 1, 2)
    bcol = jnp.pad(gm_bs[0].T, ((0, 0), (0, LANES - GM_HEADS)))
    conv_w8 = jnp.pad(full['conv_w'][0], ((0, 8 - SSM_CONV), (0, 0)))
    dtb, alog, dsk = _row128(dt_bias[0]), _row128(a_log[0]), _row128(d_skip[0])
    pool_b_f = full['pool_b'][0].reshape(1, D_MODEL)
    pool_s_f = full['pool_scale']

    def g_(layer, i):
        return ng[layer, i].reshape(1, D_MODEL)

    yn0 = _rn_fwd(h0, g_(0, 0), name="rn_fwd_0", out_dtype=MXU_DTYPE)
    proj, got = _mm(yn0, w_main, name="mm_in_proj", tm=2048,
                    ex=_Exchange([shards['w_out'][0]] + ffn_shards(0), ['gather'] * 4))
    w_out_f = got[0].reshape(-1, D_MODEL)
    w_gu[0], w_dn[0] = ffn_weights(*got[1:])
    dtr = _mm(yn0, w_dt, name="mm_in_proj_dt")
    pre = _conv_fwd(proj, conv_w8, conv_b, name="conv_fwd")
    cat = _gmlp_fwd(proj, gm_ln_g, gm_ln_b, wm, bcol, name="gmlp_fwd")
    cat, y_ssd, states, dt_ssd, acum_ssd = _ssd_fwd_grouped(pre, dtr, proj, dtb, alog, dsk, ssm_norm_g, cat,
                                                            name="ssd_fwd")
    o0 = _mm(cat, w_out_f, name="mm_out_proj", tm=1024, tn=1024)
    h1, yn1 = _resid_rn_fwd(h0, o0, g_(0, 1), g_(0, 2), name="resid_fwd_0a", next_dtype=MXU_DTYPE)
    (gu0, act0), got = _mm_swiglu(yn1, w_gu[0], name="mm_ffn0_gate_up",
                                  ex=_Exchange(ffn_shards(1) + [shards['pool_w'][0]], ['gather'] * 4))
    w_gu[1], w_dn[1] = ffn_weights(*got[:3])
    pool_w_f = got[3].transpose(1, 0, 2, 3).reshape(4, POOL_DIM, POOL_DIM)
    d0 = _mm(act0, w_dn[0], name="mm_ffn0_down", tm=1024, tn=1024)
    h2, yn2 = _resid_rn_fwd(h1, d0, g_(0, 3), g_(1, 0), name="resid_fwd_0b", next_dtype=F32)
    pm, pdiff = _pool_fwd(yn2, pool_w_f, pool_b_f, pool_s_f, name="pool_fwd")
    h3, yn3 = _resid_rn_fwd(h2, pm, g_(1, 1), g_(1, 2), name="resid_fwd_1a", next_dtype=MXU_DTYPE)
    gu1, act1 = _mm_swiglu(yn3, w_gu[1], name="mm_ffn1_gate_up")
    d1 = _mm(act1, w_dn[1], name="mm_ffn1_down", tm=1024, tn=1024)
    grads = {}
    recv = {'ffn_w_gate': [None, None], 'ffn_w_up': [None, None], 'ffn_w_down': [None, None]}
    dng = [[None] * 4 for _ in range(2)]
    dh4, loss_part, dd1, dng[1][3] = _resid_loss(h3, d1, g_(1, 3), tgt, name="resid_loss")

    def ffn_bwd(layer, dd, gu, act, yn):
        dw_dn = _mm_tn(act, dd, name=f"mm_ffn{layer}_dw_down", out_dtype=MXU_DTYPE, tm=1408, tn=1024)
        dgu = _mm_dswiglu(dd, w_dn[layer].T, gu, name=f"mm_ffn{layer}_dact")
        dw_g_t, dw_u_t = _mm_tn_gate_up(dgu, yn, name=f"mm_ffn{layer}_dw_gate_up", out_dtype=MXU_DTYPE)
        dyn, got = _mm(dgu, w_gu[layer].T, name=f"mm_ffn{layer}_dyn", out_dtype=MXU_DTYPE, tm=512, tn=1024,
                       ex=_Exchange([dw_g_t, dw_u_t, dw_dn], ['rows'] * 3))
        recv['ffn_w_gate'][layer], recv['ffn_w_up'][layer], recv['ffn_w_down'][layer] = got
        return dyn

    dyn3 = ffn_bwd(1, dd1, gu1, act1, yn3)
    dh3, dpm, dng[1][2], dng[1][1] = _resid_bwd_pre_post(dh4, [dyn3], h3, g_(1, 2), pm, g_(1, 1), name="resid_bwd_1b_1a",
                                                         out_dtype=F32)
    dyn2, d_pool_w, d_pool_b, d_pool_s = _pool_bwd(dpm, pdiff, pool_w_f, jnp.swapaxes(pool_w_f, 1, 2), pool_b_f, pool_s_f,
                                                   name="pool_bwd")
    dh2, dd0, dng[1][0], dng[0][3] = _resid_bwd_pre_post(dh3, [dyn2], h2, g_(1, 0), d0, g_(0, 3), name="resid_bwd_1a_0b",
                                                         out_dtype=MXU_DTYPE)
    dyn1 = ffn_bwd(0, dd0, gu0, act0, yn1)
    dh1, do0, dng[0][2], dng[0][1] = _resid_bwd_pre_post(dh2, [dyn1], h1, g_(0, 2), o0, g_(0, 1), name="resid_bwd_0b_0a",
                                                         out_dtype=MXU_DTYPE)
    d_w_out =_mm_tn(cat, do0, name="mm_out_proj_dw", out_dtype=MXU_DTYPE, tn=1024)
    dcat, got = _mm(do0, w_out_f.T, name="mm_out_proj_dx", out_dtype=MXU_DTYPE, tm=2048, tn=1024, ex=_Exchange([d_w_out], ['rows']))
    recv['w_out'] = got[0]
    dproj, d_wm, d_bcol, d_ln_g, d_ln_b = _gmlp_bwd(proj, dcat, gm_ln_g, gm_ln_b, wm, wm_t, bcol, name="gmlp_bwd")
    dpre, dproj, ddtr, d_gn, d_vec = _ssd_bwd_grouped(pre, dtr, dt_ssd, acum_ssd, proj, y_ssd, states, dcat, dtb, alog,
                                                      dsk, ssm_norm_g, dproj, name="ssd_bwd")
    dproj, d_conv_w8, d_conv_b = _conv_bwd(dpre, proj, conv_w8, dproj, name="conv_bwd")
    d_w_main_t = _mm_tn(dproj, yn0, name="mm_in_proj_dw", out_dtype=MXU_DTYPE, tn=1024, shift=3)
    d_w_dt_t = _mm_tn(ddtr, yn0, name="mm_in_proj_dt_dw", out_dtype=MXU_DTYPE, tn=1024)
    d_w_in_t = jnp.concatenate([d_w_main_t, d_w_dt_t[:SSM_HEADS]], axis=0).reshape(N_DEV, -1, D_MODEL)
    dyn0, got = _mm(dproj, w_main.T, name="mm_in_proj_dx", out_dtype=MXU_DTYPE, tm=512, tn=1024,
                    ex=_Exchange([d_w_in_t], ['slots']))
    recv['w_in'] = got[0]
    dyn0_dt = _mm(ddtr, w_dt.T, name="mm_in_proj_dt_dx", out_dtype=MXU_DTYPE)
    grad_x, dng[0][0] = _resid_bwd_pre(dh1, [dyn0, dyn0_dt], h0, g_(0, 0), name="resid_bwd_pre_0a")

    grads['norm_g'] = jnp.stack([jnp.concatenate(dng[l], axis=0) for l in range(2)])
    grads['gm_ln_g'], grads['gm_ln_b'] = d_ln_g, d_ln_b
    grads['gm_ws'] = d_wm[None]
    grads['gm_bs'] = d_bcol[:, :GM_HEADS].T[None]
    grads['conv_w'] = d_conv_w8[None, :SSM_CONV]
    grads['conv_b'] = d_conv_b
    grads['dt_bias'] = d_vec[0:1, :SSM_HEADS]
    grads['a_log'] = d_vec[1:2, :SSM_HEADS] * (-jnp.exp(a_log))
    grads['d_skip'] = d_vec[2:3, :SSM_HEADS]
    grads['ssm_norm_g'] = d_gn
    grads['pool_w'] = d_pool_w[None]
    grads['pool_b'] = d_pool_b.reshape(1, 4, POOL_DIM)
    grads['pool_scale'] = d_pool_s
    return loss_part, grad_x, grads, recv
```

```python
import functools
import math

import jax
import jax.numpy as jnp
from jax import lax
from jax.experimental import pallas as pl
from jax.experimental.pallas import tpu as pltpu

F32 = jnp.float32
MXU_DTYPE = jnp.bfloat16

N_DEV = 8
D_MODEL = 1024
EPS = 1e-6
GM_HEADS = 4
GM_HEAD_DIM = 256
CHUNK = 128
SSM_HEADS = 16
SSM_GROUPS = 4
SSM_STATE = 128
SSM_CONV = 4
CONV_DIM = 2048
POOL_WINDOWS = (2, 4, 8, 16)
POOL_DIM = 256
D_FF = 2816
FF_TILE = 256
IN_MAIN = 5120
LANES = 128
CONV_HALO = 8
POOL_HALO = 16
ADAM_LR, ADAM_B1, ADAM_B2, ADAM_EPS, ADAM_WD, ADAM_STEP = 0.001, 0.9, 0.999, 1e-08, 0.01, 10

VMEM_LIMIT = 56 * 1024 * 1024
ROW_TILE = 512
MM_TM = 2048

WEIGHTS = ['norm_g', 'w_in', 'gm_ln_g', 'gm_ln_b', 'gm_ws', 'gm_bs', 'conv_w', 'conv_b', 'dt_bias', 'a_log',
           'd_skip', 'ssm_norm_g', 'w_out', 'pool_w', 'pool_b', 'pool_scale', 'ffn_w_gate', 'ffn_w_up', 'ffn_w_down']
SHARD_AXIS = {'norm_g': 2, 'w_in': 2, 'gm_ln_g': None, 'gm_ln_b': None, 'gm_ws': None, 'gm_bs': None, 'conv_w': 2,
              'conv_b': None, 'dt_bias': None, 'a_log': None, 'd_skip': None, 'ssm_norm_g': None, 'w_out': 1,
              'pool_w': 2, 'pool_b': 2, 'pool_scale': 1, 'ffn_w_gate': 2, 'ffn_w_up': 2, 'ffn_w_down': 1}
GATHER_F32 =['norm_g', 'conv_w', 'pool_b', 'pool_scale']
BIG_WEIGHTS = ['w_in', 'w_out', 'ffn_w_gate', 'ffn_w_up', 'ffn_w_down']


def _cparams(sem=None):
    return pltpu.CompilerParams(dimension_semantics=sem, vmem_limit_bytes=VMEM_LIMIT)


def _dot(a, b):
    return jnp.dot(a.astype(MXU_DTYPE), b.astype(MXU_DTYPE), preferred_element_type=F32)


def _dot_nt(a, b):
    return lax.dot_general(a.astype(MXU_DTYPE), b.astype(MXU_DTYPE), (((1,), (1,)), ((), ())),
                           preferred_element_type=F32)


def _dot_tn(a, b):
    return lax.dot_general(a.astype(MXU_DTYPE), b.astype(MXU_DTYPE), (((0,), (0,)), ((), ())),
                           preferred_element_type=F32)


def _dot_exact(a, b):
    return jnp.dot(a, b, precision=lax.Precision.HIGHEST, preferred_element_type=F32)


def _sigmoid(x):
    return 0.5 * jnp.tanh(0.5 * x) + 0.5


def _sigmoid_small(x):
    return 1.0 / (1.0 + jnp.exp(-x))


def _silu(x):
    return x * _sigmoid(x)


def _silu_and_grad(x):
    s = _sigmoid(x)
    return x * s, s * (1.0 + x * (1.0 - s))


_GELU_C = math.sqrt(2.0 / math.pi)


def _gelu(x):
    return _gelu_and_grad(x)[0]


def _gelu_and_grad(x):
    x2 = x * x
    t = jnp.tanh(_GELU_C * x * (1.0 + 0.044715 * x2))
    half = 0.5 * (1.0 + t)
    return x * half, half + 0.5 * x * (1.0 - t * t) * (_GELU_C * (1.0 + 3.0 * 0.044715 * x2))


def _softplus(x):
    return jnp.maximum(x, 0.0) + jnp.log1p(jnp.exp(-jnp.abs(x)))


def _rms_scale(x):
    return lax.rsqrt(jnp.mean(x * x, axis=-1, keepdims=True) + EPS)


def _rms_bwd(dy, x, g):
    r = _rms_scale(x)
    xn = x * r
    dxn = dy * g
    dx = r * (dxn - xn * jnp.mean(dxn * xn, axis=-1, keepdims=True))
    return dx, dy * xn


def _colsum(x):
    return jnp.sum(x, axis=0, keepdims=True)


class _Exchange:
    def __init__(self, arrays, modes):
        self.arrays, self.modes, self.n = list(arrays), list(modes), len(arrays)
        self.blks = []
        for x, mode in zip(arrays, modes):
            if mode == 'gather':
                self.blks.append(tuple(x.shape))
            elif mode == 'slots':
                self.blks.append(tuple(x.shape[1:]))
            else:
                self.blks.append((x.shape[0] // N_DEV,) + tuple(x.shape[1:]))
        self.out_shape = [jax.ShapeDtypeStruct((N_DEV,) + blk, x.dtype) for x, blk in zip(arrays, self.blks)]
        self.in_specs = [pl.BlockSpec(memory_space=pl.ANY)] * self.n
        self.out_specs = [pl.BlockSpec(memory_space=pl.ANY) for _ in range(self.n)]
        n_sem = self.n * (N_DEV - 1)
        self.scratch = [pltpu.SemaphoreType.DMA((n_sem,)), pltpu.SemaphoreType.DMA((n_sem,)),
                        pltpu.SemaphoreType.DMA((self.n,))]

    def _copies(self, x_refs, out_refs, send_sems, recv_sems, local_sems, with_recvs):
        mx, my, mc = lax.axis_index("x"), lax.axis_index("y"), lax.axis_index("c")
        me = 4 * mx + 2 * my + mc

        def flip(v, bit):
            return 1 - v if bit else v

        def part(a, dev):
            if self.modes[a] == 'gather':
                return x_refs[a]
            if self.modes[a] == 'slots':
                return x_refs[a].at[dev]
            r = self.blks[a][0]
            return x_refs[a].at[pl.ds(pl.multiple_of(dev * r, 16), r)]

        sends, recvs, owns = [], [], []
        for k in (1, 2, 4, 6, 3, 5, 7):
            px, py, pc = flip(mx, (k >> 2) & 1), flip(my, (k >> 1) & 1), flip(mc, k & 1)
            peer = 4 * px + 2 * py + pc
            for a in range(self.n):
                sem = a * (N_DEV - 1) + k - 1
                sends.append(pltpu.make_async_remote_copy(
                    src_ref=part(a, peer), dst_ref=out_refs[a].at[me], send_sem=send_sems.at[sem],
                    recv_sem=recv_sems.at[sem], device_id=(px, py, pc), device_id_type=pl.DeviceIdType.MESH))
                if with_recvs:
                    recvs.append(pltpu.make_async_remote_copy(
                        src_ref=part(a, peer), dst_ref=out_refs[a].at[peer], send_sem=send_sems.at[sem],
                        recv_sem=recv_sems.at[sem], device_id=(px, py, pc), device_id_type=pl.DeviceIdType.MESH))
        for a in range(self.n):
            owns.append(pltpu.make_async_copy(part(a, me), out_refs[a].at[me], local_sems.at[a]))
        return sends, recvs, owns

    def start(self, *refs):
        sends, _, owns = self._copies(*refs, with_recvs=False)
        for cp in sends + owns:
            cp.start()

    def wait(self, *refs):
        sends, recvs, owns = self._copies(*refs, with_recvs=True)
        for cp in recvs:
            cp.wait_recv()
        for cp in sends:
            cp.wait_send()
        for cp in owns:
            cp.wait()


def _exchange(arrays, modes, *, name):
    ex = _Exchange(arrays, modes)

    def body(*refs):
        x_refs, out_refs, sems = refs[:ex.n], refs[ex.n:2 * ex.n], refs[2 * ex.n:]
        ex.start(x_refs, out_refs, *sems)
        ex.wait(x_refs, out_refs, *sems)

    return pl.pallas_call(
        body, name=name, out_shape=tuple(ex.out_shape), in_specs=ex.in_specs, out_specs=tuple(ex.out_specs),
        scratch_shapes=ex.scratch,
    )(*arrays)


def _gather_two_level(arrays, *, name):
    n = len(arrays)
    per = N_DEV - 1

    def body(*refs):
        x_refs, out_refs = refs[:n], refs[n:2 * n]
        send_sems, recv_sems, local_sems = refs[2 * n:]
        x, y, c = lax.axis_index("x"), lax.axis_index("y"), lax.axis_index("c")
        me, sibling = (x, y, c), (x, y, 1 - c)
        chips = [(1 - x, y), (x, 1 - y), (1 - x, 1 - y)]

        def copy(a, k, block, to, src=None):
            slot = out_refs[a].at[4 * block[0] + 2 * block[1] + block[2]]
            return pltpu.make_async_remote_copy(
                src_ref=slot if src is None else src, dst_ref=slot, send_sem=send_sems.at[a * per + k],
                recv_sem=recv_sems.at[a * per + k], device_id=to, device_id_type=pl.DeviceIdType.MESH)

        mines = [pltpu.make_async_copy(x_refs[a], out_refs[a].at[4 * x + 2 * y + c], local_sems.at[a]) for a in range(n)]
        firsts = []
        for a in range(n):
            firsts.append(copy(a, 0, me, sibling, src=x_refs[a]))
            firsts += [copy(a, 1 + j, me, (*chip, c), src=x_refs[a]) for j, chip in enumerate(chips)]
        for cp in mines + firsts:
            cp.start()
        passed = []
        for j, chip in enumerate(chips):
            for a in range(n):
                copy(a, 1 + j, (*chip, c), me).wait_recv()
                passed.append(copy(a, 4 + j, (*chip, c), sibling))
                passed[-1].start()
        for a in range(n):
            copy(a, 0, sibling, me).wait_recv()
            for j, chip in enumerate(chips):
                copy(a, 4 + j, (*chip, 1 - c), me).wait_recv()
        for cp in firsts + passed:
            cp.wait_send()
        for cp in mines:
            cp.wait()

    return pl.pallas_call(
        body, name=name,
        out_shape=tuple(jax.ShapeDtypeStruct((N_DEV,) + tuple(a.shape), a.dtype) for a in arrays),
        in_specs=[pl.BlockSpec(memory_space=pl.ANY)] * n,
        out_specs=tuple(pl.BlockSpec(memory_space=pl.ANY) for _ in range(n)),
        scratch_shapes=[pltpu.SemaphoreType.DMA((n * per,)), pltpu.SemaphoreType.DMA((n * per,)),
                        pltpu.SemaphoreType.DMA((n,))],
    )(*arrays)


def _hosted(body, n_in, n_out, n_scratch, grid, ex):
    def wrapped(*refs):
        ins, x_refs = refs[:n_in], refs[n_in:n_in + ex.n]
        outs = refs[n_in + ex.n:n_in + ex.n + n_out]
        xo_refs = refs[n_in + ex.n + n_out:n_in + 2 * ex.n + n_out]
        scr = refs[n_in + 2 * ex.n + n_out:n_in + 2 * ex.n + n_out + n_scratch]
        sems = refs[n_in + 2 * ex.n + n_out + n_scratch:]
        ids = [pl.program_id(d) for d in range(len(grid))]
        first = functools.reduce(jnp.logical_and, [i == 0 for i in ids])
        last = functools.reduce(jnp.logical_and, [i == g - 1 for i, g in zip(ids, grid)])

        @pl.when(first)
        def _():
            ex.start(x_refs, xo_refs, *sems)

        body(*ins, *outs, *scr)

        @pl.when(last)
        def _():
            ex.wait(x_refs, xo_refs, *sems)

    return wrapped


def _call(body, *, name, grid, inputs, in_specs, out_shape, out_specs, scratch, semantics, ex=None):
    if ex is None:
        return pl.pallas_call(
            body, name=name, grid=grid, out_shape=tuple(out_shape), in_specs=list(in_specs),
            out_specs=tuple(out_specs), scratch_shapes=list(scratch), compiler_params=_cparams(semantics))(*inputs)
    n_out = len(out_shape)
    res = pl.pallas_call(
        _hosted(body, len(inputs), n_out, len(scratch), grid, ex), name=name, grid=grid,
        out_shape=tuple(out_shape) + tuple(ex.out_shape), in_specs=list(in_specs) + ex.in_specs,
        out_specs=tuple(out_specs) + tuple(ex.out_specs), scratch_shapes=list(scratch) + ex.scratch,
        compiler_params=_cparams(("arbitrary",) * len(grid)))(*inputs, *ex.arrays)
    return res[:n_out], res[n_out:]


def _mm(a, b, *, name, out_dtype=F32, tm=MM_TM, tn=512, tk=None, ex=None):
    m, k = a.shape
    n = b.shape[1]
    tm, tn = min(tm, m), min(tn, n)
    tk = k if tk is None else tk
    nk = k // tk
    assert m % tm == 0 and n % tn == 0 and k % tk == 0

    def body(a_ref, b_ref, o_ref, acc_ref):
        kk = pl.program_id(2)
        part = _dot(a_ref[...], b_ref[...])
        if nk == 1:
            o_ref[...] = part.astype(out_dtype)
        else:
            @pl.when(kk == 0)
            def _():
                acc_ref[...] = part

            @pl.when(kk > 0)
            def _():
                acc_ref[...] += part

            @pl.when(kk == nk - 1)
            def _():
                o_ref[...] = acc_ref[...].astype(out_dtype)

    res = _call(
        body, name=name, grid=(m // tm, n // tn, nk), inputs=(a, b),
        in_specs=[pl.BlockSpec((tm, tk), lambda i, j, kk: (i, kk)), pl.BlockSpec((tk, tn), lambda i, j, kk: (kk, j))],
        out_shape=[jax.ShapeDtypeStruct((m, n), out_dtype)],
        out_specs=[pl.BlockSpec((tm, tn), lambda i, j, kk: (i, j))],
        scratch=[pltpu.VMEM((tm, tn) if nk > 1 else (8, LANES), F32)],
        semantics=("parallel", "parallel", "arbitrary"), ex=ex)
    return res[0] if ex is None else (res[0][0], res[1])


def _mm_tn(a, b, *, name, out_dtype=F32, tm=1024, tn=512, tk=1024, shift=0):
    t, m = a.shape
    n = b.shape[1]
    tm, tn, tk = min(tm, m), min(tn, n), min(tk, t)
    nk = t // tk
    nb = m // tm
    assert m % tm == 0 and n % tn == 0 and t % tk == 0

    def body(a_ref, b_ref, o_ref, acc_ref):
        kk = pl.program_id(2)
        part = _dot_tn(a_ref[...], b_ref[...])

        @pl.when(kk == 0)
        def _():
            acc_ref[...] = part

        @pl.when(kk > 0)
        def _():
            acc_ref[...] += part

        @pl.when(kk == nk - 1)
        def _():
            o_ref[...] = acc_ref[...].astype(out_dtype)

    return pl.pallas_call(
        body, name=name, grid=(nb, n // tn, nk),
        out_shape=jax.ShapeDtypeStruct((m, n), out_dtype),
        in_specs=[pl.BlockSpec((tk, tm), lambda i, j, kk: (kk, i)), pl.BlockSpec((tk, tn), lambda i, j, kk: (kk, j))],
        out_specs=pl.BlockSpec((tm, tn), lambda i, j, kk: ((i + shift) % nb, j)),
        scratch_shapes=[pltpu.VMEM((tm, tn), F32)],
        compiler_params=_cparams(("parallel", "parallel", "arbitrary")),
    )(a, b)


def _mm_tn_gate_up(dgu, yn, *, name, out_dtype, tk=2048):
    t, m = dgu.shape
    n = yn.shape[1]
    tk = min(tk, t)
    nk = t // tk
    nb = m // (2 * FF_TILE)

    def body(a_ref, b_ref, og_ref, ou_ref, acc_ref):
        kk = pl.program_id(1)
        part = _dot_tn(a_ref[...], b_ref[...])

        @pl.when(kk == 0)
        def _():
            acc_ref[...] = part

        @pl.when(kk > 0)
        def _():
            acc_ref[...] += part

        @pl.when(kk == nk - 1)
        def _():
            og_ref[...] = acc_ref[:FF_TILE, :].astype(out_dtype)
            ou_ref[...] = acc_ref[FF_TILE:, :].astype(out_dtype)

    out = jax.ShapeDtypeStruct((m // 2, n), out_dtype)
    o_spec = pl.BlockSpec((FF_TILE, n), lambda i, kk: (i, 0))
    return pl.pallas_call(
        body, name=name, grid=(nb, nk), out_shape=(out, out),
        in_specs=[pl.BlockSpec((tk, 2 * FF_TILE), lambda i, kk: (kk, i)), pl.BlockSpec((tk, n), lambda i, kk: (kk, 0))],
        out_specs=(o_spec, o_spec),
        scratch_shapes=[pltpu.VMEM((2 * FF_TILE, n), F32)],
        compiler_params=_cparams(("parallel", "arbitrary")),
    )(dgu, yn)


def _mm_swiglu(a, w_gu, *, name, tm=MM_TM, ex=None):
    m, k = a.shape
    n = w_gu.shape[1]
    nt = n // (2 * FF_TILE)
    tm = min(tm, m)

    def body(a_ref, b_ref, gu_ref, act_ref):
        gu = _dot(a_ref[...], b_ref[...])
        gu_ref[...] = gu.astype(MXU_DTYPE)
        act_ref[...] = (_silu(gu[:, :FF_TILE]) * gu[:, FF_TILE:]).astype(MXU_DTYPE)

    return _call(
        body, name=name, grid=(m // tm, nt), inputs=(a, w_gu),
        in_specs=[pl.BlockSpec((tm, k), lambda i, j: (i, 0)), pl.BlockSpec((k, 2 * FF_TILE), lambda i, j: (0, j))],
        out_shape=[jax.ShapeDtypeStruct((m, n), MXU_DTYPE), jax.ShapeDtypeStruct((m, n // 2), MXU_DTYPE)],
        out_specs=[pl.BlockSpec((tm, 2 * FF_TILE), lambda i, j: (i, j)), pl.BlockSpec((tm, FF_TILE), lambda i, j: (i, j))],
        scratch=[], semantics=("parallel", "parallel"), ex=ex)


def _mm_dswiglu(dd, w_down_t, gu, *, name, tm=MM_TM):
    m, k = dd.shape
    n = gu.shape[1]
    nt = n // (2 * FF_TILE)
    tm = min(tm, m)

    def body(d_ref, w_ref, gu_ref, o_ref):
        dact = _dot(d_ref[...], w_ref[...])
        gate, up = gu_ref[:, :FF_TILE].astype(F32), gu_ref[:, FF_TILE:].astype(F32)
        act_gate, act_grad = _silu_and_grad(gate)
        o_ref[:, :FF_TILE] = (dact * up * act_grad).astype(MXU_DTYPE)
        o_ref[:, FF_TILE:] = (dact * act_gate).astype(MXU_DTYPE)

    return pl.pallas_call(
        body, name=name, grid=(m // tm, nt),
        out_shape=jax.ShapeDtypeStruct((m, n), MXU_DTYPE),
        in_specs=[pl.BlockSpec((tm, k), lambda i, j: (i, 0)), pl.BlockSpec((k, FF_TILE), lambda i, j: (0, j)),
                  pl.BlockSpec((tm, 2 * FF_TILE), lambda i, j: (i, j))],
        out_specs=pl.BlockSpec((tm, 2 * FF_TILE), lambda i, j: (i, j)),
        compiler_params=_cparams(("parallel", "parallel")),
    )(dd, w_down_t, gu)


def _row_spec(width, tr=ROW_TILE):
    return pl.BlockSpec((tr, width), lambda i: (i, 0))


def _vec_spec(width, rows=1):
    return pl.BlockSpec((rows, width), lambda i: (0, 0))


def _rn_fwd(h, g, *, name, out_dtype):
    rows, d = h.shape

    def body(h_ref, g_ref, o_ref):
        x = h_ref[...]
        o_ref[...] = (x * _rms_scale(x) * g_ref[...]).astype(out_dtype)

    return pl.pallas_call(
        body, name=name, grid=(rows // ROW_TILE,),
        out_shape=jax.ShapeDtypeStruct((rows, d), out_dtype),
        in_specs=[_row_spec(d), _vec_spec(d)], out_specs=_row_spec(d),
        compiler_params=_cparams(("parallel",)),
    )(h, g)


def _resid_rn_fwd(h_in, o, g_post, g_next, *, name, next_dtype):
    rows, d = h_in.shape

    def body(h_ref, o_ref, gp_ref, gn_ref, ho_ref, yn_ref):
        ov = o_ref[...]
        h = h_ref[...] + ov * _rms_scale(ov) * gp_ref[...]
        ho_ref[...] = h
        yn_ref[...] = (h * _rms_scale(h) * gn_ref[...]).astype(next_dtype)

    return pl.pallas_call(
        body, name=name, grid=(rows // ROW_TILE,),
        out_shape=(jax.ShapeDtypeStruct((rows, d), F32), jax.ShapeDtypeStruct((rows, d), next_dtype)),
        in_specs=[_row_spec(d), _row_spec(d), _vec_spec(d), _vec_spec(d)],
        out_specs=(_row_spec(d), _row_spec(d)),
        compiler_params=_cparams(("parallel",)),
    )(h_in, o, g_post, g_next)


def _resid_loss(h_in, o, g_post, target, *, name):
    rows, d = h_in.shape

    def body(h_ref, o_ref, gp_ref, t_ref, dh_ref, loss_ref, do_ref, dg_ref):
        ov = o_ref[...]
        gp = gp_ref[...]
        err = h_ref[...] + ov * _rms_scale(ov) * gp - t_ref[...]
        dh = err * (1.0 / d)
        dh_ref[...] = dh
        do, dg = _rms_bwd(dh, ov, gp)
        do_ref[...] = do.astype(MXU_DTYPE)

        @pl.when(pl.program_id(0) == 0)
        def _():
            loss_ref[...] = jnp.zeros_like(loss_ref)
            dg_ref[...] = jnp.zeros_like(dg_ref)

        loss_ref[...] += 0.5 * jnp.sum(jnp.mean(err * err, axis=-1, keepdims=True), axis=0, keepdims=True)
        dg_ref[...] += _colsum(dg)

    return pl.pallas_call(
        body, name=name, grid=(rows // ROW_TILE,),
        out_shape=(jax.ShapeDtypeStruct((rows, d), F32), jax.ShapeDtypeStruct((1, 1), F32),
                   jax.ShapeDtypeStruct((rows, d), MXU_DTYPE), jax.ShapeDtypeStruct((1, d), F32)),
        in_specs=[_row_spec(d), _row_spec(d), _vec_spec(d), _row_spec(d)],
        out_specs=(_row_spec(d), pl.BlockSpec((1, 1), lambda i: (0, 0)), _row_spec(d), _vec_spec(d)),
        compiler_params=_cparams(("arbitrary",)),
    )(h_in, o, g_post, target)


def _resid_bwd_pre_post(dh, dyn_list, h_in, g_pre, o_prev, g_post_prev, *, name, out_dtype):
    rows, d = dh.shape
    n_dyn = len(dyn_list)

    def body(*refs):
        dh_ref, dyn_refs = refs[0], refs[1:1 + n_dyn]
        h_ref, g_ref, o_ref, gp_ref, out_ref, do_ref, dg_ref, dgp_ref = refs[1 + n_dyn:]
        dyn = dyn_refs[0][...].astype(F32)
        for r in dyn_refs[1:]:
            dyn = dyn + r[...].astype(F32)
        dx, dg = _rms_bwd(dyn, h_ref[...], g_ref[...])
        dh_in = dh_ref[...] + dx
        out_ref[...] = dh_in
        do, dgp = _rms_bwd(dh_in, o_ref[...], gp_ref[...])
        do_ref[...] = do.astype(out_dtype)

        @pl.when(pl.program_id(0) == 0)
        def _():
            dg_ref[...] = jnp.zeros_like(dg_ref)
            dgp_ref[...] = jnp.zeros_like(dgp_ref)

        dg_ref[...] += _colsum(dg)
        dgp_ref[...] += _colsum(dgp)

    return pl.pallas_call(
        body, name=name, grid=(rows // ROW_TILE,),
        out_shape=(jax.ShapeDtypeStruct((rows, d), F32), jax.ShapeDtypeStruct((rows, d), out_dtype),
                   jax.ShapeDtypeStruct((1, d), F32), jax.ShapeDtypeStruct((1, d), F32)),
        in_specs=[_row_spec(d)] + [_row_spec(d)] * n_dyn + [_row_spec(d), _vec_spec(d), _row_spec(d), _vec_spec(d)],
        out_specs=(_row_spec(d), _row_spec(d), _vec_spec(d), _vec_spec(d)),
        compiler_params=_cparams(("arbitrary",)),
    )(dh, *dyn_list, h_in, g_pre, o_prev, g_post_prev)


def _resid_bwd_pre(dh, dyn_list, h_in, g_pre, *, name):
    rows, d = dh.shape
    n_dyn = len(dyn_list)

    def body(*refs):
        dh_ref, dyn_refs, h_ref, g_ref, out_ref, dg_ref = refs[0], refs[1:1 + n_dyn], *refs[1 + n_dyn:]
        dyn = dyn_refs[0][...].astype(F32)
        for r in dyn_refs[1:]:
            dyn = dyn + r[...].astype(F32)
        dx, dg = _rms_bwd(dyn, h_ref[...], g_ref[...])
        out_ref[...] = dh_ref[...] + dx

        @pl.when(pl.program_id(0) == 0)
        def _():
            dg_ref[...] = jnp.zeros_like(dg_ref)

        dg_ref[...] += _colsum(dg)

    return pl.pallas_call(
        body, name=name, grid=(rows // ROW_TILE,),
        out_shape=(jax.ShapeDtypeStruct((rows, d), F32), jax.ShapeDtypeStruct((1, d), F32)),
        in_specs=[_row_spec(d)] + [_row_spec(d)] * n_dyn + [_row_spec(d), _vec_spec(d)],
        out_specs=(_row_spec(d), _vec_spec(d)),
        compiler_params=_cparams(("arbitrary",)),
    )(dh, *dyn_list, h_in, g_pre)


def _layer_norm_stats(x):
    mu = jnp.mean(x, axis=-1, keepdims=True)
    xc = x - mu
    rstd = lax.rsqrt(jnp.mean(xc * xc, axis=-1, keepdims=True) + EPS)
    return xc * rstd, rstd


def _gmlp_fwd(proj, ln_g, ln_b, wm, bcol, *, name):
    rows = proj.shape[0]
    tr = ROW_TILE

    def body(u_ref, v_ref, lg_ref, lb_ref, wm_ref, bc_ref, ya_ref):
        vhat, _ = _layer_norm_stats(_gelu(v_ref[...]))
        vl = (vhat * lg_ref[...] + lb_ref[...]).astype(MXU_DTYPE)
        gu = _gelu(u_ref[...])
        bc = bc_ref[...]
        for c in range(tr // CHUNK):
            rs = slice(c * CHUNK, (c + 1) * CHUNK)
            for h in range(GM_HEADS):
                cs = slice(h * GM_HEAD_DIM, (h + 1) * GM_HEAD_DIM)
                mixed = _dot(wm_ref[h], vl[rs, cs]) + bc[:, h:h + 1]
                ya_ref[rs, cs] = (gu[rs, cs] * mixed).astype(MXU_DTYPE)

    return pl.pallas_call(
        body, name=name, grid=(rows // tr,),
        out_shape=jax.ShapeDtypeStruct((rows, 2 * D_MODEL), MXU_DTYPE),
        in_specs=[pl.BlockSpec((tr, D_MODEL), lambda i: (i, 2)), pl.BlockSpec((tr, D_MODEL), lambda i: (i, 3)),
                  _vec_spec(D_MODEL), _vec_spec(D_MODEL),
                  pl.BlockSpec((GM_HEADS, CHUNK, CHUNK), lambda i: (0, 0, 0)), _vec_spec(LANES, CHUNK)],
        out_specs=_row_spec(D_MODEL, tr),
        compiler_params=_cparams(("parallel",)),
    )(proj, proj, ln_g, ln_b, wm, bcol)


def _gmlp_bwd(proj, dcat, ln_g, ln_b, wm, wm_t, bcol, *, name):
    rows = proj.shape[0]
    tr = ROW_TILE

    def body(u_ref, v_ref, dy_ref, lg_ref, lb_ref, wm_ref, wmt_ref, bc_ref,
             duv_ref, dwm_ref, dbc_ref, dlg_ref, dlb_ref, dvl_scr):
        @pl.when(pl.program_id(0) == 0)
        def _():
            dwm_ref[...] = jnp.zeros_like(dwm_ref)
            dbc_ref[...] = jnp.zeros_like(dbc_ref)
            dlg_ref[...] = jnp.zeros_like(dlg_ref)
            dlb_ref[...] = jnp.zeros_like(dlb_ref)

        gv, gv_grad = _gelu_and_grad(v_ref[...])
        vhat, rstd = _layer_norm_stats(gv)
        lg = lg_ref[...]
        vl = (vhat * lg + lb_ref[...]).astype(MXU_DTYPE)
        gu, gu_grad = _gelu_and_grad(u_ref[...])
        dy = dy_ref[...].astype(F32)
        bc = bc_ref[...]
        row = lax.broadcasted_iota(jnp.int32, (CHUNK, CHUNK), 0)
        lane = lax.broadcasted_iota(jnp.int32, (CHUNK, CHUNK), 1)
        causal = lane <= row
        dbc = jnp.zeros((CHUNK, LANES), F32)
        for c in range(tr // CHUNK):
            rs = slice(c * CHUNK, (c + 1) * CHUNK)
            for h in range(GM_HEADS):
                cs = slice(h * GM_HEAD_DIM, (h + 1) * GM_HEAD_DIM)
                vl_h = vl[rs, cs]
                mixed = _dot(wm_ref[h], vl_h) + bc[:, h:h + 1]
                dy_h = dy[rs, cs]
                duv_ref[rs, cs] = (dy_h * mixed * gu_grad[rs, cs]).astype(MXU_DTYPE)
                dmixed = dy_h * gu[rs, cs]
                dwm_ref[h] += jnp.where(causal, _dot_nt(dmixed, vl_h), 0.0)
                dbc = dbc + jnp.where(lane == h, jnp.sum(dmixed, axis=1, keepdims=True), 0.0)
                dvl_scr[rs, cs] = _dot(wmt_ref[h], dmixed)
        dbc_ref[...] += dbc
        dvl = dvl_scr[...]
        dlg_ref[...] += _colsum(dvl * vhat)
        dlb_ref[...] += _colsum(dvl)
        dvh = dvl * lg
        dgv = rstd * (dvh - jnp.mean(dvh, axis=-1, keepdims=True) - vhat * jnp.mean(dvh * vhat, axis=-1, keepdims=True))
        duv_ref[:, D_MODEL:] = (dgv * gv_grad).astype(MXU_DTYPE)

    return pl.pallas_call(
        body, name=name, grid=(rows // tr,),
        out_shape=(jax.ShapeDtypeStruct((rows, IN_MAIN), MXU_DTYPE),
                   jax.ShapeDtypeStruct((GM_HEADS, CHUNK, CHUNK), F32), jax.ShapeDtypeStruct((CHUNK, LANES), F32),
                   jax.ShapeDtypeStruct((1, D_MODEL), F32), jax.ShapeDtypeStruct((1, D_MODEL), F32)),
        in_specs=[pl.BlockSpec((tr, D_MODEL), lambda i: (i, 2)), pl.BlockSpec((tr, D_MODEL), lambda i: (i, 3)),
                  pl.BlockSpec((tr, D_MODEL), lambda i: (i, 0)), _vec_spec(D_MODEL), _vec_spec(D_MODEL),
                  pl.BlockSpec((GM_HEADS, CHUNK, CHUNK), lambda i: (0, 0, 0)),
                  pl.BlockSpec((GM_HEADS, CHUNK, CHUNK), lambda i: (0, 0, 0)), _vec_spec(LANES, CHUNK)],
        out_specs=(pl.BlockSpec((tr, 2 * D_MODEL), lambda i: (i, 1)),
                   pl.BlockSpec((GM_HEADS, CHUNK, CHUNK), lambda i: (0, 0, 0)), _vec_spec(LANES, CHUNK),
                   _vec_spec(D_MODEL), _vec_spec(D_MODEL)),
        scratch_shapes=[pltpu.VMEM((tr, D_MODEL), F32)],
        compiler_params=_cparams(("arbitrary",)),
    )(proj, proj, dcat, ln_g, ln_b, wm, wm_t, bcol)


def _conv_fwd(proj, conv_w8, conv_b, *, name):
    rows = proj.shape[0]
    tr = ROW_TILE
    hb = tr // CONV_HALO

    def body(x_ref, prev_ref, w_ref, b_ref, pre_ref, buf):
        first = pl.program_id(0) == 0
        buf[pl.ds(0, CONV_HALO), :] = jnp.where(first, 0.0, prev_ref[...])
        buf[pl.ds(CONV_HALO, tr), :] = x_ref[...]
        ext = buf[...]
        acc = jnp.broadcast_to(b_ref[...], (tr, CONV_DIM))
        for k in range(SSM_CONV):
            s = SSM_CONV - 1 - k
            acc = acc + w_ref[k:k + 1, :] * (x_ref[...] if s == 0 else pltpu.roll(ext, s, axis=0)[CONV_HALO:])
        pre_ref[...] = acc

    return pl.pallas_call(
        body, name=name, grid=(rows // tr,),
        out_shape=jax.ShapeDtypeStruct((rows, CONV_DIM), F32),
        in_specs=[pl.BlockSpec((tr, CONV_DIM), lambda i: (i, 0)),
                  pl.BlockSpec((CONV_HALO, CONV_DIM), lambda i: (jnp.maximum(i * hb - 1, 0), 0)),
                  _vec_spec(CONV_DIM, 8), _vec_spec(CONV_DIM)],
        out_specs=_row_spec(CONV_DIM, tr),
        scratch_shapes=[pltpu.VMEM((tr + CONV_HALO, CONV_DIM), F32)],
        compiler_params=_cparams(("parallel",)),
    )(proj, proj, conv_w8, conv_b)


def _conv_bwd(dpre, proj, conv_w8, dproj, *, name):
    rows = proj.shape[0]
    tr = ROW_TILE
    halo = 16
    hb = tr // halo
    nblk = rows // tr

    def body(d_ref, dnext_ref, x_ref, w_ref, dproj_ref, dx_ref, dw_ref, db_ref, dbuf):
        i = pl.program_id(0)

        @pl.when(i == 0)
        def _():
            dw_ref[...] = jnp.zeros_like(dw_ref)
            db_ref[...] = jnp.zeros_like(db_ref)

        d = d_ref[...].astype(F32)
        x = x_ref[...]
        dbuf[pl.ds(0, tr), :] = d
        dbuf[pl.ds(tr, halo), :] = jnp.where(i == nblk - 1, 0.0, dnext_ref[...].astype(F32))
        ext = dbuf[...]
        acc = jnp.zeros((tr, CONV_DIM), F32)
        for k in range(SSM_CONV):
            s = SSM_CONV - 1 - k
            shifted = d if s == 0 else pltpu.roll(ext, tr + halo - s, axis=0)[:tr]
            acc = acc + w_ref[k:k + 1, :] * shifted
            dw_ref[k:k + 1, :] += _colsum(shifted * x)
        dx_ref[...] = acc.astype(MXU_DTYPE)
        db_ref[...] += _colsum(d)

    return pl.pallas_call(
        body, name=name, grid=(nblk,),
        out_shape=(jax.ShapeDtypeStruct((rows, IN_MAIN), MXU_DTYPE), jax.ShapeDtypeStruct((8, CONV_DIM), F32),
                   jax.ShapeDtypeStruct((1, CONV_DIM), F32)),
        in_specs=[_row_spec(CONV_DIM, tr),
                  pl.BlockSpec((halo, CONV_DIM), lambda i: (jnp.minimum((i + 1) * hb, rows // halo - 1), 0)),
                  pl.BlockSpec((tr, CONV_DIM), lambda i: (i, 0)),
                  _vec_spec(CONV_DIM, 8), pl.BlockSpec(memory_space=pl.ANY)],
        out_specs=(_row_spec(CONV_DIM, tr), _vec_spec(CONV_DIM, 8), _vec_spec(CONV_DIM)),
        scratch_shapes=[pltpu.VMEM((tr + halo, CONV_DIM), F32)],
        input_output_aliases={4: 0},
        compiler_params=_cparams(("arbitrary",)),
    )(dpre, dpre, proj, conv_w8, dproj)


def _chunk_iotas():
    row = lax.broadcasted_iota(jnp.int32, (CHUNK, CHUNK), 0)
    lane = lax.broadcasted_iota(jnp.int32, (CHUNK, CHUNK), 1)
    return row, lane, lane <= row


SSD_BWD_CHUNKS = 2
GROUP_DIM = D_MODEL // SSM_GROUPS
HEADS_PER_GROUP = SSM_HEADS // SSM_GROUPS
HEAD_DIM = GROUP_DIM // HEADS_PER_GROUP


def _split(x):
    hi = x.astype(MXU_DTYPE)
    return hi, (x - hi.astype(F32)).astype(MXU_DTYPE)


def _dot_split(x, sel):
    hi, lo = _split(x)
    return jnp.dot(hi, sel, preferred_element_type=F32) + jnp.dot(lo, sel, preferred_element_type=F32)


def _dot_split_rhs(sel, x):
    hi, lo = _split(x)
    return jnp.dot(sel, hi, preferred_element_type=F32) + jnp.dot(sel, lo, preferred_element_type=F32)


def _head_selectors():
    h = lax.broadcasted_iota(jnp.int32, (LANES, D_MODEL), 0)
    p = lax.broadcasted_iota(jnp.int32, (LANES, D_MODEL), 1)
    sel_t = (h == p // HEAD_DIM).astype(MXU_DTYPE)
    return sel_t, sel_t.T


def _expand_heads(per_head, e_end, selt_ref, sel_ref):
    stacked = jnp.concatenate(per_head, axis=0)
    wide = _dot_split(stacked, selt_ref[...])
    n = per_head[0].shape[0]
    e_cols = jnp.broadcast_to(e_end, (LANES, LANES)).T
    tall = _dot_split_rhs(sel_ref[...], e_cols)
    return [wide[n * i:n * (i + 1)] for i in range(len(per_head))], tall


def _by_quarter(index, pieces):
    out = pieces[3]
    for q in (2, 1, 0):
        out = jnp.where(index == q, pieces[q], out)
    return out


def _ssd_fwd_grouped(pre, dtr, proj, dtb, alog, dsk, gn, cat, *, name):
    rows = pre.shape[0]
    nc = rows // CHUNK

    def body(pre_ref, dtr_ref, z_ref, dtb_ref, alog_ref, dsk_ref, gn_ref, cat_ref, selt_ref, sel_ref, yb_ref, y_ref,
             st_ref, dt_ref, acum_ref, s_scr):
        @pl.when(pl.program_id(0) == 0)
        def _():
            s_scr[...] = jnp.zeros_like(s_scr)

        row, lane, tril = _chunk_iotas()
        dt = _softplus(dtr_ref[...] + dtb_ref[...])
        acum = _dot_exact(tril.astype(F32), dt * (-jnp.exp(alog_ref[...])))
        dt_ref[...] = dt
        acum_ref[...] = acum
        acum_t = acum.T
        a_end = acum[CHUNK - 1:CHUNK, :]
        (dt_x, eacum_x, dte_x, dsk_x), e_rows_all = _expand_heads(
            [dt, jnp.exp(acum), jnp.exp(a_end - acum), jnp.broadcast_to(dsk_ref[...], (CHUNK, LANES))], jnp.exp(a_end),
            selt_ref, sel_ref)
        lane_q = lax.broadcasted_iota(jnp.int32, (CHUNK, GROUP_DIM), 1) // HEAD_DIM

        for g in range(SSM_GROUPS):
            cs = slice(GROUP_DIM * g, GROUP_DIM * (g + 1))
            b_g = _silu(pre_ref[:, 1024 + SSM_STATE * g:1024 + SSM_STATE * (g + 1)]).astype(MXU_DTYPE)
            c_g = _silu(pre_ref[:, 1536 + SSM_STATE * g:1536 + SSM_STATE * (g + 1)]).astype(MXU_DTYPE)
            cb = _dot_nt(c_g, b_g)
            xs = _silu(pre_ref[:, cs])
            xdt = xs * dt_x[:, cs]
            m_stack = jnp.concatenate(
                [(cb * jnp.exp(jnp.where(tril, acum[:, h:h + 1] - acum_t[h:h + 1, :], -jnp.inf))).astype(MXU_DTYPE)
                 for h in range(4 * g, 4 * g + 4)], axis=0)
            y_all = _dot(m_stack, xdt)
            y = _by_quarter(lane_q, [y_all[CHUNK * q:CHUNK * (q + 1)] for q in range(HEADS_PER_GROUP)])
            s_prev = s_scr[g]
            st_ref[0, g] = s_prev
            y = y + _dot_nt(c_g, s_prev) * eacum_x[:, cs] + dsk_x[:, cs] * xs
            xw = xdt * dte_x[:, cs]
            s_scr[g] = e_rows_all[cs, :] * s_prev + _dot(xw.T, b_g)
            y_ref[:, cs] = y
            gated = y * _silu(z_ref[:, cs])
            r = lax.rsqrt(jnp.mean(gated * gated, axis=1, keepdims=True) + EPS)
            yb_ref[:, cs] = (gated * r * gn_ref[:, cs]).astype(MXU_DTYPE)

    return pl.pallas_call(
        body, name=name, grid=(nc,),
        out_shape=(jax.ShapeDtypeStruct((rows, 2 * D_MODEL), MXU_DTYPE), jax.ShapeDtypeStruct((rows, D_MODEL), F32),
                   jax.ShapeDtypeStruct((nc, SSM_GROUPS, GROUP_DIM, SSM_STATE), F32),
                   jax.ShapeDtypeStruct((rows, LANES), F32), jax.ShapeDtypeStruct((rows, LANES), F32)),
        in_specs=[_row_spec(CONV_DIM, CHUNK), _row_spec(LANES, CHUNK), pl.BlockSpec((CHUNK, D_MODEL), lambda i: (i, 4)),
                  _vec_spec(LANES), _vec_spec(LANES), _vec_spec(LANES), _vec_spec(D_MODEL),
                  pl.BlockSpec(memory_space=pl.ANY), _vec_spec(D_MODEL, LANES), _vec_spec(LANES, D_MODEL)],
        out_specs=(pl.BlockSpec((CHUNK, D_MODEL), lambda i: (i, 1)), _row_spec(D_MODEL, CHUNK),
                   pl.BlockSpec((1, SSM_GROUPS, GROUP_DIM, SSM_STATE), lambda i: (i, 0, 0, 0)),
                   _row_spec(LANES, CHUNK), _row_spec(LANES, CHUNK)),
        scratch_shapes=[pltpu.VMEM((SSM_GROUPS, GROUP_DIM, SSM_STATE), F32)],
        input_output_aliases={7: 0},
        compiler_params=_cparams(("arbitrary",)),
    )(pre, dtr, proj, dtb, alog, dsk, gn, cat, *_head_selectors())


def _ssd_bwd_grouped(pre, dtr, dt_saved, acum_saved, proj, y_saved, states, dcat, dtb, alog, dsk, gn, dproj, *, name):
    rows = pre.shape[0]
    cps = SSD_BWD_CHUNKS
    tr = CHUNK * cps
    nsteps = rows // tr

    def rev(i):
        return nsteps - 1 - i

    def body(pre_ref, dtr_ref, dt_ref, acum_ref, z_ref, y_ref, st_ref, dyb_ref, dtb_ref, alog_ref, dsk_ref, gn_ref,
             dproj_ref, selt_ref, sel_ref, dpre_ref, dz_ref, ddtr_ref, dgn_ref, dvec_ref, g_scr):
        @pl.when(pl.program_id(0) == 0)
        def _():
            g_scr[...] = jnp.zeros_like(g_scr)
            dgn_ref[...] = jnp.zeros_like(dgn_ref)
            dvec_ref[...] = jnp.zeros_like(dvec_ref)

        for cc in reversed(range(cps)):
            at = lambda ref: ref.at[pl.ds(cc * CHUNK, CHUNK)]
            chunk(at(pre_ref), at(dtr_ref), at(dt_ref), at(acum_ref), at(z_ref), at(y_ref), st_ref.at[cc], at(dyb_ref),
                  dtb_ref, alog_ref, dsk_ref, gn_ref, selt_ref, sel_ref, at(dpre_ref), at(dz_ref), at(ddtr_ref), dgn_ref,
                  dvec_ref, g_scr)

    def chunk(pre_ref, dtr_ref, dt_ref, acum_ref, z_ref, y_ref, st_ref, dyb_ref, dtb_ref, alog_ref, dsk_ref, gn_ref,
              selt_ref, sel_ref, dpre_ref, dz_ref, ddtr_ref, dgn_ref, dvec_ref, g_scr):
        row, lane, tril = _chunk_iotas()
        triu = lane >= row
        dt, acum = dt_ref[...], acum_ref[...]
        a = -jnp.exp(alog_ref[...])
        acum_t = acum.T
        a_end = acum[CHUNK - 1:CHUNK, :]
        e_end = jnp.exp(a_end)
        (dt_x, eacum_x, dte_x, dsk_x), e_rows_all = _expand_heads(
            [dt, jnp.exp(acum), jnp.exp(a_end - acum), jnp.broadcast_to(dsk_ref[...], (CHUNK, LANES))], e_end,
            selt_ref, sel_ref)
        lane_q = lax.broadcasted_iota(jnp.int32, (CHUNK, GROUP_DIM), 1) // HEAD_DIM
        zero = jnp.zeros((CHUNK, LANES), F32)
        dacum_c, dacum_r, ddt_c = zero, zero, zero
        d_aend = jnp.zeros((1, LANES), F32)
        d_dsk = jnp.zeros((1, LANES), F32)
        iota = lambda shape, dim: lax.broadcasted_iota(jnp.int32, shape, dim)
        q256, lane_256 = iota((GROUP_DIM, LANES), 0) // HEAD_DIM, iota((GROUP_DIM, LANES), 1)
        q512, lane_512 = iota((4 * CHUNK, LANES), 0) // CHUNK, iota((4 * CHUNK, LANES), 1)
        row_512t, q512t = iota((LANES, 4 * CHUNK), 0), iota((LANES, 4 * CHUNK), 1) // CHUNK

        for g in range(SSM_GROUPS):
            cs = slice(GROUP_DIM * g, GROUP_DIM * (g + 1))
            yv = y_ref[:, cs]
            sz, sz_grad = _silu_and_grad(z_ref[:, cs])
            gated = yv * sz
            dyb = dyb_ref[:, cs].astype(F32)
            dgh = dyb * gn_ref[:, cs]
            r = lax.rsqrt(jnp.mean(gated * gated, axis=1, keepdims=True) + EPS)
            dgn_ref[:, cs] += _colsum(dyb * gated * r)
            dgated = r * dgh - gated * (r * r * r * jnp.mean(dgh * gated, axis=1, keepdims=True))
            dy = dgated * sz
            dz_ref[:, cs] = (dgated * yv * sz_grad).astype(MXU_DTYPE)

            b_f, b_grad = _silu_and_grad(pre_ref[:, 1024 + SSM_STATE * g:1024 + SSM_STATE * (g + 1)])
            c_f, c_grad = _silu_and_grad(pre_ref[:, 1536 + SSM_STATE * g:1536 + SSM_STATE * (g + 1)])
            b_g, c_g = b_f.astype(MXU_DTYPE), c_f.astype(MXU_DTYPE)
            xs, xs_grad = _silu_and_grad(pre_ref[:, cs])
            dtq = dt_x[:, cs]
            xdt = xs * dtq
            xdt_m = xdt.astype(MXU_DTYPE)
            dy_m = dy.astype(MXU_DTYPE)
            s_prev = st_ref[g]
            g_next = g_scr[g]
            eacq, dteq = eacum_x[:, cs], dte_x[:, cs]
            t_off = dy * (_dot_nt(c_g, s_prev) * eacq)
            dye = dy * eacq
            dc_g = _dot(dye, s_prev)
            bg = _dot_nt(b_g, g_next)
            xw = xdt * dteq
            db_g = _dot(xw, g_next)
            t_w = xw * bg
            gs = g_next * s_prev
            g_scr[g] = e_rows_all[cs, :] * g_next + _dot(dye.T, c_g)
            cb = _dot_nt(c_g, b_g)
            cb_t = cb.T
            heads = range(4 * g, 4 * g + 4)
            decs = [jnp.exp(jnp.where(tril, acum[:, h:h + 1] - acum_t[h:h + 1, :], -jnp.inf)) for h in heads]
            mt_stack = jnp.concatenate(
                [(cb_t * jnp.exp(jnp.where(triu, acum_t[h:h + 1, :] - acum[:, h:h + 1], -jnp.inf))).astype(MXU_DTYPE)
                 for h in heads], axis=0)
            dy_stack = jnp.concatenate([jnp.where(lane_q == q, dy, 0.0).astype(MXU_DTYPE)
                                        for q in range(HEADS_PER_GROUP)], axis=0)
            dm_all = _dot_nt(dy_stack, xdt_m)
            dx_all = _dot(mt_stack, dy_m)
            dxdt = bg * dteq + _by_quarter(lane_q, [dx_all[CHUNK * q:CHUNK * (q + 1)] for q in range(HEADS_PER_GROUP)])
            t_dt = dxdt * xs
            t_dk = dy * xs
            dec_stack = jnp.concatenate(decs, axis=0)
            dm_dec = dm_all * dec_stack
            e_all = dm_dec * jnp.concatenate([cb] * HEADS_PER_GROUP, axis=0)
            dcb = functools.reduce(jnp.add, [dm_dec[CHUNK * q:CHUNK * (q + 1)] for q in range(HEADS_PER_GROUP)])
            one = jnp.ones((), MXU_DTYPE)
            sel_lanes = jnp.where(q256 + 4 * g == lane_256, one, 0)
            sel_rows = jnp.where(q512 + 4 * g == lane_512, one, 0)
            sel_rows_t = jnp.where(row_512t == q512t + 4 * g, one, 0)
            e_lanes = jnp.concatenate([e_all[CHUNK * q:CHUNK * (q + 1)] for q in range(HEADS_PER_GROUP)], axis=1)
            w_heads = _dot(t_w, sel_lanes)
            dacum_c = dacum_c + _dot(e_lanes, sel_rows) + _dot(t_off, sel_lanes) - w_heads
            dacum_r = dacum_r + _dot(sel_rows_t, e_all)
            ddt_c = ddt_c + _dot(t_dt, sel_lanes)
            d_aend = d_aend + _colsum(w_heads) + e_end * _colsum(_dot_tn(gs, sel_lanes))
            d_dsk = d_dsk + _colsum(_dot(t_dk, sel_lanes))
            dpre_ref[:, cs] = ((dxdt * dtq + dsk_x[:, cs] * dy) * xs_grad).astype(MXU_DTYPE)
            dc_g = dc_g + _dot(dcb, b_g)
            db_g = db_g + _dot(dcb.T, c_g)
            dpre_ref[:, 1024 + SSM_STATE * g:1024 + SSM_STATE * (g + 1)] = (db_g * b_grad).astype(MXU_DTYPE)
            dpre_ref[:, 1536 + SSM_STATE * g:1536 + SSM_STATE * (g + 1)] = (dc_g * c_grad).astype(MXU_DTYPE)

        dacum = dacum_c - dacum_r.T + jnp.where(row == CHUNK - 1, d_aend, 0.0)
        dda = _dot_exact(triu.astype(F32), dacum)
        ddtr = (dda * a + ddt_c) * _sigmoid_small(dtr_ref[...] + dtb_ref[...])
        ddtr_ref[...] = ddtr.astype(MXU_DTYPE)
        dvec_ref[0:1, :] += _colsum(ddtr)
        dvec_ref[1:2, :] += _colsum(dda * dt)
        dvec_ref[2:3, :] += d_dsk

    return pl.pallas_call(
        body, name=name, grid=(nsteps,),
        out_shape=(jax.ShapeDtypeStruct((rows, CONV_DIM), MXU_DTYPE), jax.ShapeDtypeStruct((rows, IN_MAIN), MXU_DTYPE),
                   jax.ShapeDtypeStruct((rows, LANES), MXU_DTYPE), jax.ShapeDtypeStruct((1, D_MODEL), F32),
                   jax.ShapeDtypeStruct((8, LANES), F32)),
        in_specs=[pl.BlockSpec((tr, CONV_DIM), lambda i: (rev(i), 0)), pl.BlockSpec((tr, LANES), lambda i: (rev(i), 0)),
                  pl.BlockSpec((tr, LANES), lambda i: (rev(i), 0)), pl.BlockSpec((tr, LANES), lambda i: (rev(i), 0)),
                  pl.BlockSpec((tr, D_MODEL), lambda i: (rev(i), 4)), pl.BlockSpec((tr, D_MODEL), lambda i: (rev(i), 0)),
                  pl.BlockSpec((cps, SSM_GROUPS, GROUP_DIM, SSM_STATE), lambda i: (rev(i), 0, 0, 0)),
                  pl.BlockSpec((tr, D_MODEL), lambda i: (rev(i), 1)),
                  _vec_spec(LANES), _vec_spec(LANES), _vec_spec(LANES), _vec_spec(D_MODEL),
                  pl.BlockSpec(memory_space=pl.ANY), _vec_spec(D_MODEL, LANES), _vec_spec(LANES, D_MODEL)],
        out_specs=(pl.BlockSpec((tr, CONV_DIM), lambda i: (rev(i), 0)), pl.BlockSpec((tr, D_MODEL), lambda i: (rev(i), 4)),
                   pl.BlockSpec((tr, LANES), lambda i: (rev(i), 0)), _vec_spec(D_MODEL), _vec_spec(LANES, 8)),
        scratch_shapes=[pltpu.VMEM((SSM_GROUPS, GROUP_DIM, SSM_STATE), F32)],
        input_output_aliases={12: 1},
        compiler_params=_cparams(("arbitrary",)),
    )(pre, dtr, dt_saved, acum_saved, proj, y_saved, states, dcat, dtb, alog, dsk, gn, dproj, *_head_selectors())


def _pool_counts(first_row, n_rows, win):
    t = first_row + lax.broadcasted_iota(jnp.int32, (n_rows, POOL_DIM), 0)
    return jnp.minimum(t + 1, win).astype(F32)


def _pool_fwd(yn, pool_w, pool_b, pool_scale, *, name):
    rows = yn.shape[0]
    tr = ROW_TILE
    hb = tr // POOL_HALO

    def body(y_ref, prev_ref, w_ref, b_ref, s_ref, pm_ref, diff_ref, buf):
        i = pl.program_id(0)
        buf[pl.ds(0, POOL_HALO), :] = jnp.where(i == 0, 0.0, prev_ref[...])
        buf[pl.ds(POOL_HALO, tr), :] = y_ref[...]
        level = buf[...]
        sums = []
        for g, win in enumerate(POOL_WINDOWS):
            level = level + pltpu.roll(level, win // 2, axis=0)
            sums.append(level[POOL_HALO:, :POOL_DIM])
            if g + 1 < len(POOL_WINDOWS):
                level = level[:, POOL_DIM:]
        for g, win in enumerate(POOL_WINDOWS):
            cs = slice(POOL_DIM * g, POOL_DIM * (g + 1))
            diff = (sums[g] / _pool_counts(i * tr, tr, win) - y_ref[:, cs]).astype(MXU_DTYPE)
            diff_ref[:, cs] = diff
            pm_ref[:, cs] = (_dot(diff, w_ref[g]) + b_ref[:, cs]) * s_ref[:, cs]

    return pl.pallas_call(
        body, name=name, grid=(rows // tr,),
        out_shape=(jax.ShapeDtypeStruct((rows, D_MODEL), F32), jax.ShapeDtypeStruct((rows, D_MODEL), MXU_DTYPE)),
        in_specs=[_row_spec(D_MODEL, tr),
                  pl.BlockSpec((POOL_HALO, D_MODEL), lambda i: (jnp.maximum(i * hb - 1, 0), 0)),
                  pl.BlockSpec((4, POOL_DIM, POOL_DIM), lambda i: (0, 0, 0)), _vec_spec(D_MODEL), _vec_spec(D_MODEL)],
        out_specs=(_row_spec(D_MODEL, tr), _row_spec(D_MODEL, tr)),
        scratch_shapes=[pltpu.VMEM((tr + POOL_HALO, D_MODEL), F32)],
        compiler_params=_cparams(("parallel",)),
    )(yn, yn, pool_w, pool_b, pool_scale)


def _pool_bwd(dpm, diff, pool_w, pool_w_t, pool_b, pool_scale, *, name):
    rows = dpm.shape[0]
    tr = ROW_TILE
    hb = tr // POOL_HALO
    nblk = rows // tr

    def body(d_ref, dnext_ref, diff_ref, w_ref, wt_ref, b_ref, s_ref, dy_ref, dw_ref, db_ref, ds_ref, ebuf):
        i = pl.program_id(0)

        @pl.when(i == 0)
        def _():
            dw_ref[...] = jnp.zeros_like(dw_ref)
            db_ref[...] = jnp.zeros_like(db_ref)
            ds_ref[...] = jnp.zeros_like(ds_ref)

        last = i == nblk - 1
        for g, win in enumerate(POOL_WINDOWS):
            cs = slice(POOL_DIM * g, POOL_DIM * (g + 1))
            d = d_ref[:, cs]
            diff = diff_ref[:, cs]
            out_pre = _dot(diff, w_ref[g]) + b_ref[:, cs]
            ds_ref[:, cs] += _colsum(d * out_pre)
            dout = d * s_ref[:, cs]
            db_ref[:, cs] += _colsum(dout)
            dw_ref[g] += _dot_tn(diff, dout)
            ddiff = _dot(dout, wt_ref[g])
            ddiff_next = _dot(jnp.where(last, 0.0, dnext_ref[:, cs]) * s_ref[:, cs], wt_ref[g])
            ebuf[pl.ds(0, tr), cs] = ddiff / _pool_counts(i * tr, tr, win)
            ebuf[pl.ds(tr, POOL_HALO), cs] = ddiff_next / _pool_counts((i + 1) * tr, POOL_HALO, win)
            dy_ref[:, cs] = -ddiff
        level = ebuf[...]
        n = tr + POOL_HALO
        for g, win in enumerate(POOL_WINDOWS):
            level = level + pltpu.roll(level, n - win // 2, axis=0)
            dy_ref[:, POOL_DIM * g:POOL_DIM * (g + 1)] += level[:tr, :POOL_DIM]
            if g + 1 < len(POOL_WINDOWS):
                level = level[:, POOL_DIM:]

    return pl.pallas_call(
        body, name=name, grid=(nblk,),
        out_shape=(jax.ShapeDtypeStruct((rows, D_MODEL), F32), jax.ShapeDtypeStruct((4, POOL_DIM, POOL_DIM), F32),
                   jax.ShapeDtypeStruct((1, D_MODEL), F32), jax.ShapeDtypeStruct((1, D_MODEL), F32)),
        in_specs=[_row_spec(D_MODEL, tr),
                  pl.BlockSpec((POOL_HALO, D_MODEL), lambda i: (jnp.minimum((i + 1) * hb, rows // POOL_HALO - 1), 0)),
                  _row_spec(D_MODEL, tr),
                  pl.BlockSpec((4, POOL_DIM, POOL_DIM), lambda i: (0, 0, 0)),
                  pl.BlockSpec((4, POOL_DIM, POOL_DIM), lambda i: (0, 0, 0)), _vec_spec(D_MODEL), _vec_spec(D_MODEL)],
        out_specs=(_row_spec(D_MODEL, tr), pl.BlockSpec((4, POOL_DIM, POOL_DIM), lambda i: (0, 0, 0)),
                   _vec_spec(D_MODEL), _vec_spec(D_MODEL)),
        scratch_shapes=[pltpu.VMEM((tr + POOL_HALO, D_MODEL), F32)],
        compiler_params=_cparams(("arbitrary",)),
    )(dpm, dpm, diff, pool_w, pool_w_t, pool_b, pool_scale)


def _row_tile(rows, cap, step):
    best = rows
    for t in range(step, min(rows, cap) + 1, step):
        if rows % t == 0:
            best = t
    return best if best <= cap else rows


def _sum8(recv, *, name):
    _, r, c = recv.shape
    step = 8 if recv.dtype == F32 else 16

    def body(r_ref, g_ref):
        g = r_ref[0].astype(F32)
        for j in range(1, N_DEV):
            g = g + r_ref[j].astype(F32)
        g_ref[...] = g

    if r % step == 0:
        tr = _row_tile(r, 256, step)
        grid, in_spec, out_spec = (r // tr,), pl.BlockSpec((N_DEV, tr, c), lambda i: (0, i, 0)), pl.BlockSpec((tr, c), lambda i: (i, 0))
    else:
        tc = 256
        grid, in_spec, out_spec = (c // tc,), pl.BlockSpec((N_DEV, r, tc), lambda i: (0, 0, i)), pl.BlockSpec((r, tc), lambda i: (0, i))
    return pl.pallas_call(
        body, name=name, grid=grid, out_shape=jax.ShapeDtypeStruct((r, c), F32),
        in_specs=[in_spec], out_specs=out_spec, compiler_params=_cparams(("parallel",)),
    )(recv)


def _adamw(g, w, m, v, *, name):
    rows, cols = w.shape
    tr = _row_tile(rows, max(8, (256 * 1024) // cols // 8 * 8), 8)
    c1 = 1.0 / (1.0 - ADAM_B1 ** ADAM_STEP)
    c2 = 1.0 / (1.0 - ADAM_B2 ** ADAM_STEP)

    def body(g_ref, w_ref, m_ref, v_ref, d_ref, mo_ref, vo_ref):
        g = g_ref[...]
        m_new = ADAM_B1 * m_ref[...] + (1.0 - ADAM_B1) * g
        v_new = ADAM_B2 * v_ref[...] + (1.0 - ADAM_B2) * (g * g)
        mo_ref[...] = m_new
        vo_ref[...] = v_new
        d_ref[...] = -ADAM_LR * ((m_new * c1) / (jnp.sqrt(v_new * c2) + ADAM_EPS) + ADAM_WD * w_ref[...])

    spec = pl.BlockSpec((tr, cols), lambda i: (i, 0))
    return pl.pallas_call(
        body, name=name, grid=(rows // tr,),
        out_shape=tuple(jax.ShapeDtypeStruct((rows, cols), F32) for _ in range(3)),
        in_specs=[spec] * 4, out_specs=(spec, spec, spec),
        compiler_params=_cparams(("parallel",)),
    )(g, w, m, v)


def _pad_rows(flat, mult):
    n = flat.shape[-1]
    pad = (-n) % mult
    if pad:
        flat = jnp.pad(flat, [(0, 0)] * (flat.ndim - 1) + [(0, pad)])
    return flat


def _pack_blocks(blocks, row_mult):
    flat = jnp.concatenate([_pad_rows(b.reshape(-1), LANES) for b in blocks])
    return _pad_rows(flat, LANES * row_mult).reshape(-1, LANES)


def _block_sizes(blocks):
    return [-(-math.prod(b.shape) // LANES) * LANES for b in blocks]


def _unpack_blocks(slab, like, lead=()):
    flat = slab.reshape(lead + (-1,))
    out, off = [], 0
    for b, size in zip(like, _block_sizes(like)):
        n = math.prod(b.shape)
        out.append(flat[..., off:off + n].reshape(lead + tuple(b.shape)))
        off += size
    return out


def _join_shards(gathered, axis):
    return jnp.concatenate([gathered[j] for j in range(N_DEV)], axis=axis)


def _split_shards(full, axis):
    return jnp.stack(jnp.split(full, N_DEV, axis=axis))


def _interleave_ff(w_gate, w_up):
    k = w_gate.shape[0]
    nt = D_FF // FF_TILE
    return jnp.stack([w_gate.reshape(k, nt, FF_TILE), w_up.reshape(k, nt, FF_TILE)], axis=2).reshape(k, 2 * D_FF)


def _row128(vec):
    return jnp.pad(vec.reshape(1, -1), ((0, 0), (0, LANES - vec.shape[-1])))


def kernel(x, norm_g, w_in, gm_ln_g, gm_ln_b, gm_ws, gm_bs, conv_w, conv_b, dt_bias, a_log, d_skip, ssm_norm_g, w_out, pool_w, pool_b, pool_scale, ffn_w_gate, ffn_w_up, ffn_w_down, loss_target, m_norm_g, m_w_in, m_gm_ln_g, m_gm_ln_b, m_gm_ws, m_gm_bs, m_conv_w, m_conv_b, m_dt_bias, m_a_log, m_d_skip, m_ssm_norm_g, m_w_out, m_pool_w, m_pool_b, m_pool_scale, m_ffn_w_gate, m_ffn_w_up, m_ffn_w_down, v_norm_g, v_w_in, v_gm_ln_g, v_gm_ln_b, v_gm_ws, v_gm_bs, v_conv_w, v_conv_b, v_dt_bias, v_a_log, v_d_skip, v_ssm_norm_g, v_w_out, v_pool_w, v_pool_b, v_pool_scale, v_ffn_w_gate, v_ffn_w_up, v_ffn_w_down):
    w_loc = dict(norm_g=norm_g, w_in=w_in, gm_ln_g=gm_ln_g, gm_ln_b=gm_ln_b, gm_ws=gm_ws, gm_bs=gm_bs, conv_w=conv_w,
                 conv_b=conv_b, dt_bias=dt_bias, a_log=a_log, d_skip=d_skip, ssm_norm_g=ssm_norm_g, w_out=w_out,
                 pool_w=pool_w, pool_b=pool_b, pool_scale=pool_scale, ffn_w_gate=ffn_w_gate, ffn_w_up=ffn_w_up,
                 ffn_w_down=ffn_w_down)
    m_loc = dict(zip(WEIGHTS, [m_norm_g, m_w_in, m_gm_ln_g, m_gm_ln_b, m_gm_ws, m_gm_bs, m_conv_w, m_conv_b, m_dt_bias,
                               m_a_log, m_d_skip, m_ssm_norm_g, m_w_out, m_pool_w, m_pool_b, m_pool_scale,
                               m_ffn_w_gate, m_ffn_w_up, m_ffn_w_down]))
    v_loc = dict(zip(WEIGHTS, [v_norm_g, v_w_in, v_gm_ln_g, v_gm_ln_b, v_gm_ws, v_gm_bs, v_conv_w, v_conv_b, v_dt_bias,
                               v_a_log, v_d_skip, v_ssm_norm_g, v_w_out, v_pool_w, v_pool_b, v_pool_scale,
                               v_ffn_w_gate, v_ffn_w_up, v_ffn_w_down]))

    small_blocks = [w_loc[n] for n in GATHER_F32]
    got = _gather_two_level([w_in[0].astype(MXU_DTYPE), _pack_blocks(small_blocks, 8)], name="gather_first")
    full = {n: w_loc[n] for n in WEIGHTS if SHARD_AXIS[n] is None}
    full['w_in'] = got[0].transpose(1, 0, 2).reshape(1, D_MODEL, -1)
    for n, g in zip(GATHER_F32, _unpack_blocks(got[1], small_blocks, (N_DEV,))):
        full[n] = _join_shards(g, SHARD_AXIS[n])
    shards = {n: w_loc[n].astype(MXU_DTYPE) for n in ('w_out', 'ffn_w_gate', 'ffn_w_up', 'ffn_w_down', 'pool_w')}

    loss_part, grad_x, grads, recv = _local_step(x[0], loss_target[0], full, shards)

    small = [n for n in WEIGHTS if n not in BIG_WEIGHTS]
    like = [w_loc[n] for n in small]
    slots = []
    for n in small:
        ax = SHARD_AXIS[n]
        g = grads[n].astype(F32)
        sh = _split_shards(g, ax) if ax is not None else jnp.broadcast_to(g[None], (N_DEV,) + g.shape)
        slots.append(_pad_rows(sh.reshape(N_DEV, -1), LANES))
    send_small = _pad_rows(jnp.concatenate(slots, axis=1), LANES * 8).reshape(N_DEV, -1, LANES)
    recv_small, = _exchange([send_small], ['slots'], name="exchange_last")

    g_small = _sum8(recv_small, name="sum_small")
    g_own = dict(zip(small, _unpack_blocks(g_small, like)))
    g_own['w_in'] = _sum8(recv['w_in'], name="sum_w_in").T[None]
    g_own['w_out'] = _sum8(recv['w_out'], name="sum_w_out")[None]
    g_own['ffn_w_gate'] = jnp.stack([_sum8(recv['ffn_w_gate'][l], name=f"sum_ffn{l}_gate").T for l in range(2)])
    g_own['ffn_w_up'] = jnp.stack([_sum8(recv['ffn_w_up'][l], name=f"sum_ffn{l}_up").T for l in range(2)])
    g_own['ffn_w_down'] = jnp.stack([_sum8(recv['ffn_w_down'][l], name=f"sum_ffn{l}_down") for l in range(2)])

    delta, m_new, v_new = {}, {}, {}
    pk = lambda d: _pack_blocks([d[n] for n in small], 8)
    d_s, m_s, v_s = _adamw(g_small, pk(w_loc), pk(m_loc), pk(v_loc), name="adamw_small")
    for dst, slab in ((delta, d_s), (m_new, m_s), (v_new, v_s)):
        dst.update(zip(small, _unpack_blocks(slab, like)))
    for n in BIG_WEIGHTS:
        shape = w_loc[n].shape
        two_d = lambda t: t.reshape(-1, shape[-1])
        res = _adamw(two_d(g_own[n]), two_d(w_loc[n]), two_d(m_loc[n]), two_d(v_loc[n]), name=f"adamw_{n}")
        delta[n], m_new[n], v_new[n] = (t.reshape(shape) for t in res)

    loss = lax.psum(loss_part[0, 0], ("x", "y", "c"))
    outs = [d[n] for d in (g_own, delta, m_new, v_new) for n in WEIGHTS]
    return (loss, grad_x[None], *outs)


def _local_step(h0, tgt, full, shards):
    gm_ln_g, gm_ln_b, gm_ws, gm_bs = full['gm_ln_g'], full['gm_ln_b'], full['gm_ws'], full['gm_bs']
    conv_b, dt_bias, a_log, d_skip, ssm_norm_g = (full['conv_b'], full['dt_bias'], full['a_log'], full['d_skip'],
                                                  full['ssm_norm_g'])
    w_in_f = full['w_in'][0]
    w_main = jnp.concatenate([w_in_f[:, 3072:5120], w_in_f[:, :3072]], axis=1)
    w_dt = jnp.pad(w_in_f[:, 5120:], ((0, 0), (0, LANES - SSM_HEADS)))
    ng = full['norm_g']

    def ffn_shards(layer):
        return [shards['ffn_w_gate'][layer], shards['ffn_w_up'][layer], shards['ffn_w_down'][layer]]

    def ffn_weights(got_gate, got_up, got_down):
        cols = lambda g: g.transpose(1, 0, 2).reshape(D_MODEL, D_FF)
        return _interleave_ff(cols(got_gate), cols(got_up)), got_down.reshape(D_FF, D_MODEL)

    w_gu, w_dn = [None, None], [None, None]
    causal = jnp.tril(jnp.ones((CHUNK, CHUNK), bool))
    wm = jnp.where(causal[None], gm_ws[0], 0.0).astype(MXU_DTYPE)
    wm_t = jnp.swapaxes(wm, 1, 2)
    bcol = jnp.pad(gm_bs[0].T, ((0, 0), (0, LANES - GM_HEADS)))
    conv_w8 = jnp.pad(full['conv_w'][0], ((0, 8 - SSM_CONV), (0, 0)))
    dtb, alog, dsk = _row128(dt_bias[0]), _row128(a_log[0]), _row128(d_skip[0])
    pool_b_f = full['pool_b'][0].reshape(1, D_MODEL)
    pool_s_f = full['pool_scale']

    def g_(layer, i):
        return ng[layer, i].reshape(1, D_MODEL)

    yn0 = _rn_fwd(h0, g_(0, 0), name="rn_fwd_0", out_dtype=MXU_DTYPE)
    proj, got = _mm(yn0, w_main, name="mm_in_proj", tm=2048,
                    ex=_Exchange([shards['w_out'][0]] + ffn_shards(0), ['gather'] * 4))
    w_out_f = got[0].reshape(-1, D_MODEL)
    w_gu[0], w_dn[0] = ffn_weights(*got[1:])
    dtr = _mm(yn0, w_dt, name="mm_in_proj_dt")
    pre = _conv_fwd(proj, conv_w8, conv_b, name="conv_fwd")
    cat = _gmlp_fwd(proj, gm_ln_g, gm_ln_b, wm, bcol, name="gmlp_fwd")
    cat, y_ssd, states, dt_ssd, acum_ssd = _ssd_fwd_grouped(pre, dtr, proj, dtb, alog, dsk, ssm_norm_g, cat,
                                                            name="ssd_fwd")
    o0 = _mm(cat, w_out_f, name="mm_out_proj", tm=1024, tn=1024)
    h1, yn1 = _resid_rn_fwd(h0, o0, g_(0, 1), g_(0, 2), name="resid_fwd_0a", next_dtype=MXU_DTYPE)
    (gu0, act0), got = _mm_swiglu(yn1, w_gu[0], name="mm_ffn0_gate_up",
                                  ex=_Exchange(ffn_shards(1) + [shards['pool_w'][0]], ['gather'] * 4))
    w_gu[1], w_dn[1] = ffn_weights(*got[:3])
    pool_w_f = got[3].transpose(1, 0, 2, 3).reshape(4, POOL_DIM, POOL_DIM)
    d0 = _mm(act0, w_dn[0], name="mm_ffn0_down", tm=1024, tn=1024)
    h2, yn2 = _resid_rn_fwd(h1, d0, g_(0, 3), g_(1, 0), name="resid_fwd_0b", next_dtype=F32)
    pm, pdiff = _pool_fwd(yn2, pool_w_f, pool_b_f, pool_s_f, name="pool_fwd")
    h3, yn3 = _resid_rn_fwd(h2, pm, g_(1, 1), g_(1, 2), name="resid_fwd_1a", next_dtype=MXU_DTYPE)
    gu1, act1 = _mm_swiglu(yn3, w_gu[1], name="mm_ffn1_gate_up")
    d1 = _mm(act1, w_dn[1], name="mm_ffn1_down", tm=1024, tn=1024)
    grads = {}
    recv = {'ffn_w_gate': [None, None], 'ffn_w_up': [None, None], 'ffn_w_down': [None, None]}
    dng = [[None] * 4 for _ in range(2)]
    dh4, loss_part, dd1, dng[1][3] = _resid_loss(h3, d1, g_(1, 3), tgt, name="resid_loss")

    def ffn_bwd(layer, dd, gu, act, yn):
        dw_dn = _mm_tn(act, dd, name=f"mm_ffn{layer}_dw_down", out_dtype=MXU_DTYPE, tm=1408, tn=1024)
        dgu = _mm_dswiglu(dd, w_dn[layer].T, gu, name=f"mm_ffn{layer}_dact")
        dw_g_t, dw_u_t = _mm_tn_gate_up(dgu, yn, name=f"mm_ffn{layer}_dw_gate_up", out_dtype=MXU_DTYPE)
        dyn, got = _mm(dgu, w_gu[layer].T, name=f"mm_ffn{layer}_dyn", out_dtype=MXU_DTYPE, tm=512, tn=1024,
                       ex=_Exchange([dw_g_t, dw_u_t, dw_dn], ['rows'] * 3))
        recv['ffn_w_gate'][layer], recv['ffn_w_up'][layer], recv['ffn_w_down'][layer] = got
        return dyn

    dyn3 = ffn_bwd(1, dd1, gu1, act1, yn3)
    dh3, dpm, dng[1][2], dng[1][1] = _resid_bwd_pre_post(dh4, [dyn3], h3, g_(1, 2), pm, g_(1, 1), name="resid_bwd_1b_1a",
                                                         out_dtype=F32)
    dyn2, d_pool_w, d_pool_b, d_pool_s = _pool_bwd(dpm, pdiff, pool_w_f, jnp.swapaxes(pool_w_f, 1, 2), pool_b_f, pool_s_f,
                                                   name="pool_bwd")
    dh2, dd0, dng[1][0], dng[0][3] = _resid_bwd_pre_post(dh3, [dyn2], h2, g_(1, 0), d0, g_(0, 3), name="resid_bwd_1a_0b",
                                                         out_dtype=MXU_DTYPE)
    dyn1 = ffn_bwd(0, dd0, gu0, act0, yn1)
    dh1, do0, dng[0][2], dng[0][1] = _resid_bwd_pre_post(dh2, [dyn1], h1, g_(0, 2), o0, g_(0, 1), name="resid_bwd_0b_0a",
                                                         out_dtype=MXU_DTYPE)
    d_w_out =_mm_tn(cat, do0, name="mm_out_proj_dw", out_dtype=MXU_DTYPE, tn=1024)
    dcat, got = _mm(do0, w_out_f.T, name="mm_out_proj_dx", out_dtype=MXU_DTYPE, tm=2048, tn=1024, ex=_Exchange([d_w_out], ['rows']))
    recv['w_out'] = got[0]
    dproj, d_wm, d_bcol, d_ln_g, d_ln_b = _gmlp_bwd(proj, dcat, gm_ln_g, gm_ln_b, wm, wm_t, bcol, name="gmlp_bwd")
    dpre, dproj, ddtr, d_gn, d_vec = _ssd_bwd_grouped(pre, dtr, dt_ssd, acum_ssd, proj, y_ssd, states, dcat, dtb, alog,
                                                      dsk, ssm_norm_g, dproj, name="ssd_bwd")
    dproj, d_conv_w8, d_conv_b = _conv_bwd(dpre, proj, conv_w8, dproj, name="conv_bwd")
    d_w_main_t = _mm_tn(dproj, yn0, name="mm_in_proj_dw", out_dtype=MXU_DTYPE, tn=1024, shift=3)
    d_w_dt_t = _mm_tn(ddtr, yn0, name="mm_in_proj_dt_dw", out_dtype=MXU_DTYPE, tn=1024)
    d_w_in_t = jnp.concatenate([d_w_main_t, d_w_dt_t[:SSM_HEADS]], axis=0).reshape(N_DEV, -1, D_MODEL)
    dyn0, got = _mm(dproj, w_main.T, name="mm_in_proj_dx", out_dtype=MXU_DTYPE, tm=512, tn=1024,
                    ex=_Exchange([d_w_in_t], ['slots']))
    recv['w_in'] = got[0]
    dyn0_dt = _mm(ddtr, w_dt.T, name="mm_in_proj_dt_dx", out_dtype=MXU_DTYPE)
    grad_x, dng[0][0] = _resid_bwd_pre(dh1, [dyn0, dyn0_dt], h0, g_(0, 0), name="resid_bwd_pre_0a")

    grads['norm_g'] = jnp.stack([jnp.concatenate(dng[l], axis=0) for l in range(2)])
    grads['gm_ln_g'], grads['gm_ln_b'] = d_ln_g, d_ln_b
    grads['gm_ws'] = d_wm[None]
    grads['gm_bs'] = d_bcol[:, :GM_HEADS].T[None]
    grads['conv_w'] = d_conv_w8[None, :SSM_CONV]
    grads['conv_b'] = d_conv_b
    grads['dt_bias'] = d_vec[0:1, :SSM_HEADS]
    grads['a_log'] = d_vec[1:2, :SSM_HEADS] * (-jnp.exp(a_log))
    grads['d_skip'] = d_vec[2:3, :SSM_HEADS]
    grads['ssm_norm_g'] = d_gn
    grads['pool_w'] = d_pool_w[None]
    grads['pool_b'] = d_pool_b.reshape(1, 4, POOL_DIM)
    grads['pool_scale'] = d_pool_s
    return loss_part, grad_x, grads, recv
```

```python
import functools
import math

import jax
import jax.numpy as jnp
from jax import lax
from jax.experimental import pallas as pl
from jax.experimental.pallas import tpu as pltpu

F32 = jnp.float32
MXU_DTYPE = jnp.bfloat16

N_DEV = 8
D_MODEL = 1024
EPS = 1e-6
GM_HEADS = 4
GM_HEAD_DIM = 256
CHUNK = 128
SSM_HEADS = 16
SSM_GROUPS = 4
SSM_STATE = 128
SSM_CONV = 4
CONV_DIM = 2048
POOL_WINDOWS = (2, 4, 8, 16)
POOL_DIM = 256
D_FF = 2816
FF_TILE = 256
IN_MAIN = 5120
LANES = 128
CONV_HALO = 8
POOL_HALO = 16
ADAM_LR, ADAM_B1, ADAM_B2, ADAM_EPS, ADAM_WD, ADAM_STEP = 0.001, 0.9, 0.999, 1e-08, 0.01, 10

VMEM_LIMIT = 56 * 1024 * 1024
ROW_TILE = 512
MM_TM = 2048

WEIGHTS = ['norm_g', 'w_in', 'gm_ln_g', 'gm_ln_b', 'gm_ws', 'gm_bs', 'conv_w', 'conv_b', 'dt_bias', 'a_log',
           'd_skip', 'ssm_norm_g', 'w_out', 'pool_w', 'pool_b', 'pool_scale', 'ffn_w_gate', 'ffn_w_up', 'ffn_w_down']
SHARD_AXIS = {'norm_g': 2, 'w_in': 2, 'gm_ln_g': None, 'gm_ln_b': None, 'gm_ws': None, 'gm_bs': None, 'conv_w': 2,
              'conv_b': None, 'dt_bias': None, 'a_log': None, 'd_skip': None, 'ssm_norm_g': None, 'w_out': 1,
              'pool_w': 2, 'pool_b': 2, 'pool_scale': 1, 'ffn_w_gate': 2, 'ffn_w_up': 2, 'ffn_w_down': 1}
GATHER_F32 =['norm_g', 'conv_w', 'pool_b', 'pool_scale']
BIG_WEIGHTS = ['w_in', 'w_out', 'ffn_w_gate', 'ffn_w_up', 'ffn_w_down']


def _cparams(sem=None):
    return pltpu.CompilerParams(dimension_semantics=sem, vmem_limit_bytes=VMEM_LIMIT)


def _dot(a, b):
    return jnp.dot(a.astype(MXU_DTYPE), b.astype(MXU_DTYPE), preferred_element_type=F32)


def _dot_nt(a, b):
    return lax.dot_general(a.astype(MXU_DTYPE), b.astype(MXU_DTYPE), (((1,), (1,)), ((), ())),
                           preferred_element_type=F32)


def _dot_tn(a, b):
    return lax.dot_general(a.astype(MXU_DTYPE), b.astype(MXU_DTYPE), (((0,), (0,)), ((), ())),
                           preferred_element_type=F32)


def _dot_exact(a, b):
    return jnp.dot(a, b, precision=lax.Precision.HIGHEST, preferred_element_type=F32)


def _sigmoid(x):
    return 0.5 * jnp.tanh(0.5 * x) + 0.5


def _sigmoid_small(x):
    return 1.0 / (1.0 + jnp.exp(-x))


def _silu(x):
    return x * _sigmoid(x)


def _silu_and_grad(x):
    s = _sigmoid(x)
    return x * s, s * (1.0 + x * (1.0 - s))


_GELU_C = math.sqrt(2.0 / math.pi)


def _gelu(x):
    return _gelu_and_grad(x)[0]


def _gelu_and_grad(x):
    x2 = x * x
    t = jnp.tanh(_GELU_C * x * (1.0 + 0.044715 * x2))
    half = 0.5 * (1.0 + t)
    return x * half, half + 0.5 * x * (1.0 - t * t) * (_GELU_C * (1.0 + 3.0 * 0.044715 * x2))


def _softplus(x):
    return jnp.maximum(x, 0.0) + jnp.log1p(jnp.exp(-jnp.abs(x)))


def _rms_scale(x):
    return lax.rsqrt(jnp.mean(x * x, axis=-1, keepdims=True) + EPS)


def _rms_bwd(dy, x, g):
    r = _rms_scale(x)
    xn = x * r
    dxn = dy * g
    dx = r * (dxn - xn * jnp.mean(dxn * xn, axis=-1, keepdims=True))
    return dx, dy * xn


def _colsum(x):
    return jnp.sum(x, axis=0, keepdims=True)


class _Exchange:
    def __init__(self, arrays, modes):
        self.arrays, self.modes, self.n = list(arrays), list(modes), len(arrays)
        self.blks = []
        for x, mode in zip(arrays, modes):
            if mode == 'gather':
                self.blks.append(tuple(x.shape))
            elif mode == 'slots':
                self.blks.append(tuple(x.shape[1:]))
            else:
                self.blks.append((x.shape[0] // N_DEV,) + tuple(x.shape[1:]))
        self.out_shape = [jax.ShapeDtypeStruct((N_DEV,) + blk, x.dtype) for x, blk in zip(arrays, self.blks)]
        self.in_specs = [pl.BlockSpec(memory_space=pl.ANY)] * self.n
        self.out_specs = [pl.BlockSpec(memory_space=pl.ANY) for _ in range(self.n)]
        n_sem = self.n * (N_DEV - 1)
        self.scratch = [pltpu.SemaphoreType.DMA((n_sem,)), pltpu.SemaphoreType.DMA((n_sem,)),
                        pltpu.SemaphoreType.DMA((self.n,))]

    def _copies(self, x_refs, out_refs, send_sems, recv_sems, local_sems, with_recvs):
        mx, my, mc = lax.axis_index("x"), lax.axis_index("y"), lax.axis_index("c")
        me = 4 * mx + 2 * my + mc

        def flip(v, bit):
            return 1 - v if bit else v

        def part(a, dev):
            if self.modes[a] == 'gather':
                return x_refs[a]
            if self.modes[a] == 'slots':
                return x_refs[a].at[dev]
            r = self.blks[a][0]
            return x_refs[a].at[pl.ds(pl.multiple_of(dev * r, 16), r)]

        sends, recvs, owns = [], [], []
        for k in (1, 2, 4, 6, 3, 5, 7):
            px, py, pc = flip(mx, (k >> 2) & 1), flip(my, (k >> 1) & 1), flip(mc, k & 1)
            peer = 4 * px + 2 * py + pc
            for a in range(self.n):
                sem = a * (N_DEV - 1) + k - 1
                sends.append(pltpu.make_async_remote_copy(
                    src_ref=part(a, peer), dst_ref=out_refs[a].at[me], send_sem=send_sems.at[sem],
                    recv_sem=recv_sems.at[sem], device_id=(px, py, pc), device_id_type=pl.DeviceIdType.MESH))
                if with_recvs:
                    recvs.append(pltpu.make_async_remote_copy(
                        src_ref=part(a, peer), dst_ref=out_refs[a].at[peer], send_sem=send_sems.at[sem],
                        recv_sem=recv_sems.at[sem], device_id=(px, py, pc), device_id_type=pl.DeviceIdType.MESH))
        for a in range(self.n):
            owns.append(pltpu.make_async_copy(part(a, me), out_refs[a].at[me], local_sems.at[a]))
        return sends, recvs, owns

    def start(self, *refs):
        sends, _, owns = self._copies(*refs, with_recvs=False)
        for cp in sends + owns:
            cp.start()

    def wait(self, *refs):
        sends, recvs, owns = self._copies(*refs, with_recvs=True)
        for cp in recvs:
            cp.wait_recv()
        for cp in sends:
            cp.wait_send()
        for cp in owns:
            cp.wait()


def _exchange(arrays, modes, *, name):
    ex = _Exchange(arrays, modes)

    def body(*refs):
        x_refs, out_refs, sems = refs[:ex.n], refs[ex.n:2 * ex.n], refs[2 * ex.n:]
        ex.start(x_refs, out_refs, *sems)
        ex.wait(x_refs, out_refs, *sems)

    return pl.pallas_call(
        body, name=name, out_shape=tuple(ex.out_shape), in_specs=ex.in_specs, out_specs=tuple(ex.out_specs),
        scratch_shapes=ex.scratch,
    )(*arrays)


def _gather_two_level(arrays, *, name):
    n = len(arrays)
    per = N_DEV - 1

    def body(*refs):
        x_refs, out_refs = refs[:n], refs[n:2 * n]
        send_sems, recv_sems, local_sems = refs[2 * n:]
        x, y, c = lax.axis_index("x"), lax.axis_index("y"), lax.axis_index("c")
        me, sibling = (x, y, c), (x, y, 1 - c)
        chips = [(1 - x, y), (x, 1 - y), (1 - x, 1 - y)]

        def copy(a, k, block, to, src=None):
            slot = out_refs[a].at[4 * block[0] + 2 * block[1] + block[2]]
            return pltpu.make_async_remote_copy(
                src_ref=slot if src is None else src, dst_ref=slot, send_sem=send_sems.at[a * per + k],
                recv_sem=recv_sems.at[a * per + k], device_id=to, device_id_type=pl.DeviceIdType.MESH)

        mines = [pltpu.make_async_copy(x_refs[a], out_refs[a].at[4 * x + 2 * y + c], local_sems.at[a]) for a in range(n)]
        firsts = []
        for a in range(n):
            firsts.append(copy(a, 0, me, sibling, src=x_refs[a]))
            firsts += [copy(a, 1 + j, me, (*chip, c), src=x_refs[a]) for j, chip in enumerate(chips)]
        for cp in mines + firsts:
            cp.start()
        passed = []
        for j, chip in enumerate(chips):
            for a in range(n):
                copy(a, 1 + j, (*chip, c), me).wait_recv()
                passed.append(copy(a, 4 + j, (*chip, c), sibling))
                passed[-1].start()
        for a in range(n):
            copy(a, 0, sibling, me).wait_recv()
            for j, chip in enumerate(chips):
                copy(a, 4 + j, (*chip, 1 - c), me).wait_recv()
        for cp in firsts + passed:
            cp.wait_send()
        for cp in mines:
            cp.wait()

    return pl.pallas_call(
        body, name=name,
        out_shape=tuple(jax.ShapeDtypeStruct((N_DEV,) + tuple(a.shape), a.dtype) for a in arrays),
        in_specs=[pl.BlockSpec(memory_space=pl.ANY)] * n,
        out_specs=tuple(pl.BlockSpec(memory_space=pl.ANY) for _ in range(n)),
        scratch_shapes=[pltpu.SemaphoreType.DMA((n * per,)), pltpu.SemaphoreType.DMA((n * per,)),
                        pltpu.SemaphoreType.DMA((n,))],
    )(*arrays)


def _hosted(body, n_in, n_out, n_scratch, grid, ex):
    def wrapped(*refs):
        ins, x_refs = refs[:n_in], refs[n_in:n_in + ex.n]
        outs = refs[n_in + ex.n:n_in + ex.n + n_out]
        xo_refs = refs[n_in + ex.n + n_out:n_in + 2 * ex.n + n_out]
        scr = refs[n_in + 2 * ex.n + n_out:n_in + 2 * ex.n + n_out + n_scratch]
        sems = refs[n_in + 2 * ex.n + n_out + n_scratch:]
        ids = [pl.program_id(d) for d in range(len(grid))]
        first = functools.reduce(jnp.logical_and, [i == 0 for i in ids])
        last = functools.reduce(jnp.logical_and, [i == g - 1 for i, g in zip(ids, grid)])

        @pl.when(first)
        def _():
            ex.start(x_refs, xo_refs, *sems)

        body(*ins, *outs, *scr)

        @pl.when(last)
        def _():
            ex.wait(x_refs, xo_refs, *sems)

    return wrapped


def _call(body, *, name, grid, inputs, in_specs, out_shape, out_specs, scratch, semantics, ex=None):
    if ex is None:
        return pl.pallas_call(
            body, name=name, grid=grid, out_shape=tuple(out_shape), in_specs=list(in_specs),
            out_specs=tuple(out_specs), scratch_shapes=list(scratch), compiler_params=_cparams(semantics))(*inputs)
    n_out = len(out_shape)
    res = pl.pallas_call(
        _hosted(body, len(inputs), n_out, len(scratch), grid, ex), name=name, grid=grid,
        out_shape=tuple(out_shape) + tuple(ex.out_shape), in_specs=list(in_specs) + ex.in_specs,
        out_specs=tuple(out_specs) + tuple(ex.out_specs), scratch_shapes=list(scratch) + ex.scratch,
        compiler_params=_cparams(("arbitrary",) * len(grid)))(*inputs, *ex.arrays)
    return res[:n_out], res[n_out:]


def _mm(a, b, *, name, out_dtype=F32, tm=MM_TM, tn=512, tk=None, ex=None):
    m, k = a.shape
    n = b.shape[1]
    tm, tn = min(tm, m), min(tn, n)
    tk = k if tk is None else tk
    nk = k // tk
    assert m % tm == 0 and n % tn == 0 and k % tk == 0

    def body(a_ref, b_ref, o_ref, acc_ref):
        kk = pl.program_id(2)
        part = _dot(a_ref[...], b_ref[...])
        if nk == 1:
            o_ref[...] = part.astype(out_dtype)
        else:
            @pl.when(kk == 0)
            def _():
                acc_ref[...] = part

            @pl.when(kk > 0)
            def _():
                acc_ref[...] += part

            @pl.when(kk == nk - 1)
            def _():
                o_ref[...] = acc_ref[...].astype(out_dtype)

    res = _call(
        body, name=name, grid=(m // tm, n // tn, nk), inputs=(a, b),
        in_specs=[pl.BlockSpec((tm, tk), lambda i, j, kk: (i, kk)), pl.BlockSpec((tk, tn), lambda i, j, kk: (kk, j))],
        out_shape=[jax.ShapeDtypeStruct((m, n), out_dtype)],
        out_specs=[pl.BlockSpec((tm, tn), lambda i, j, kk: (i, j))],
        scratch=[pltpu.VMEM((tm, tn) if nk > 1 else (8, LANES), F32)],
        semantics=("parallel", "parallel", "arbitrary"), ex=ex)
    return res[0] if ex is None else (res[0][0], res[1])


def _mm_tn(a, b, *, name, out_dtype=F32, tm=1024, tn=512, tk=1024, shift=0):
    t, m = a.shape
    n = b.shape[1]
    tm, tn, tk = min(tm, m), min(tn, n), min(tk, t)
    nk = t // tk
    nb = m // tm
    assert m % tm == 0 and n % tn == 0 and t % tk == 0

    def body(a_ref, b_ref, o_ref, acc_ref):
        kk = pl.program_id(2)
        part = _dot_tn(a_ref[...], b_ref[...])

        @pl.when(kk == 0)
        def _():
            acc_ref[...] = part

        @pl.when(kk > 0)
        def _():
            acc_ref[...] += part

        @pl.when(kk == nk - 1)
        def _():
            o_ref[...] = acc_ref[...].astype(out_dtype)

    return pl.pallas_call(
        body, name=name, grid=(nb, n // tn, nk),
        out_shape=jax.ShapeDtypeStruct((m, n), out_dtype),
        in_specs=[pl.BlockSpec((tk, tm), lambda i, j, kk: (kk, i)), pl.BlockSpec((tk, tn), lambda i, j, kk: (kk, j))],
        out_specs=pl.BlockSpec((tm, tn), lambda i, j, kk: ((i + shift) % nb, j)),
        scratch_shapes=[pltpu.VMEM((tm, tn), F32)],
        compiler_params=_cparams(("parallel", "parallel", "arbitrary")),
    )(a, b)


def _mm_tn_gate_up(dgu, yn, *, name, out_dtype, tk=2048):
    t, m = dgu.shape
    n = yn.shape[1]
    tk = min(tk, t)
    nk = t // tk
    nb = m // (2 * FF_TILE)

    def body(a_ref, b_ref, og_ref, ou_ref, acc_ref):
        kk = pl.program_id(1)
        part = _dot_tn(a_ref[...], b_ref[...])

        @pl.when(kk == 0)
        def _():
            acc_ref[...] = part

        @pl.when(kk > 0)
        def _():
            acc_ref[...] += part

        @pl.when(kk == nk - 1)
        def _():
            og_ref[...] = acc_ref[:FF_TILE, :].astype(out_dtype)
            ou_ref[...] = acc_ref[FF_TILE:, :].astype(out_dtype)

    out = jax.ShapeDtypeStruct((m // 2, n), out_dtype)
    o_spec = pl.BlockSpec((FF_TILE, n), lambda i, kk: (i, 0))
    return pl.pallas_call(
        body, name=name, grid=(nb, nk), out_shape=(out, out),
        in_specs=[pl.BlockSpec((tk, 2 * FF_TILE), lambda i, kk: (kk, i)), pl.BlockSpec((tk, n), lambda i, kk: (kk, 0))],
        out_specs=(o_spec, o_spec),
        scratch_shapes=[pltpu.VMEM((2 * FF_TILE, n), F32)],
        compiler_params=_cparams(("parallel", "arbitrary")),
    )(dgu, yn)


def _mm_swiglu(a, w_gu, *, name, tm=MM_TM, ex=None):
    m, k = a.shape
    n = w_gu.shape[1]
    nt = n // (2 * FF_TILE)
    tm = min(tm, m)

    def body(a_ref, b_ref, gu_ref, act_ref):
        gu = _dot(a_ref[...], b_ref[...])
        gu_ref[...] = gu.astype(MXU_DTYPE)
        act_ref[...] = (_silu(gu[:, :FF_TILE]) * gu[:, FF_TILE:]).astype(MXU_DTYPE)

    return _call(
        body, name=name, grid=(m // tm, nt), inputs=(a, w_gu),
        in_specs=[pl.BlockSpec((tm, k), lambda i, j: (i, 0)), pl.BlockSpec((k, 2 * FF_TILE), lambda i, j: (0, j))],
        out_shape=[jax.ShapeDtypeStruct((m, n), MXU_DTYPE), jax.ShapeDtypeStruct((m, n // 2), MXU_DTYPE)],
        out_specs=[pl.BlockSpec((tm, 2 * FF_TILE), lambda i, j: (i, j)), pl.BlockSpec((tm, FF_TILE), lambda i, j: (i, j))],
        scratch=[], semantics=("parallel", "parallel"), ex=ex)


def _mm_dswiglu(dd, w_down_t, gu, *, name, tm=MM_TM):
    m, k = dd.shape
    n = gu.shape[1]
    nt = n // (2 * FF_TILE)
    tm = min(tm, m)

    def body(d_ref, w_ref, gu_ref, o_ref):
        dact = _dot(d_ref[...], w_ref[...])
        gate, up = gu_ref[:, :FF_TILE].astype(F32), gu_ref[:, FF_TILE:].astype(F32)
        act_gate, act_grad = _silu_and_grad(gate)
        o_ref[:, :FF_TILE] = (dact * up * act_grad).astype(MXU_DTYPE)
        o_ref[:, FF_TILE:] = (dact * act_gate).astype(MXU_DTYPE)

    return pl.pallas_call(
        body, name=name, grid=(m // tm, nt),
        out_shape=jax.ShapeDtypeStruct((m, n), MXU_DTYPE),
        in_specs=[pl.BlockSpec((tm, k), lambda i, j: (i, 0)), pl.BlockSpec((k, FF_TILE), lambda i, j: (0, j)),
                  pl.BlockSpec((tm, 2 * FF_TILE), lambda i, j: (i, j))],
        out_specs=pl.BlockSpec((tm, 2 * FF_TILE), lambda i, j: (i, j)),
        compiler_params=_cparams(("parallel", "parallel")),
    )(dd, w_down_t, gu)


def _row_spec(width, tr=ROW_TILE):
    return pl.BlockSpec((tr, width), lambda i: (i, 0))


def _vec_spec(width, rows=1):
    return pl.BlockSpec((rows, width), lambda i: (0, 0))


def _rn_fwd(h, g, *, name, out_dtype):
    rows, d = h.shape

    def body(h_ref, g_ref, o_ref):
        x = h_ref[...]
        o_ref[...] = (x * _rms_scale(x) * g_ref[...]).astype(out_dtype)

    return pl.pallas_call(
        body, name=name, grid=(rows // ROW_TILE,),
        out_shape=jax.ShapeDtypeStruct((rows, d), out_dtype),
        in_specs=[_row_spec(d), _vec_spec(d)], out_specs=_row_spec(d),
        compiler_params=_cparams(("parallel",)),
    )(h, g)


def _resid_rn_fwd(h_in, o, g_post, g_next, *, name, next_dtype):
    rows, d = h_in.shape

    def body(h_ref, o_ref, gp_ref, gn_ref, ho_ref, yn_ref):
        ov = o_ref[...]
        h = h_ref[...] + ov * _rms_scale(ov) * gp_ref[...]
        ho_ref[...] = h
        yn_ref[...] = (h * _rms_scale(h) * gn_ref[...]).astype(next_dtype)

    return pl.pallas_call(
        body, name=name, grid=(rows // ROW_TILE,),
        out_shape=(jax.ShapeDtypeStruct((rows, d), F32), jax.ShapeDtypeStruct((rows, d), next_dtype)),
        in_specs=[_row_spec(d), _row_spec(d), _vec_spec(d), _vec_spec(d)],
        out_specs=(_row_spec(d), _row_spec(d)),
        compiler_params=_cparams(("parallel",)),
    )(h_in, o, g_post, g_next)


def _resid_loss(h_in, o, g_post, target, *, name):
    rows, d = h_in.shape

    def body(h_ref, o_ref, gp_ref, t_ref, dh_ref, loss_ref, do_ref, dg_ref):
        ov = o_ref[...]
        gp = gp_ref[...]
        err = h_ref[...] + ov * _rms_scale(ov) * gp - t_ref[...]
        dh = err * (1.0 / d)
        dh_ref[...] = dh
        do, dg = _rms_bwd(dh, ov, gp)
        do_ref[...] = do.astype(MXU_DTYPE)

        @pl.when(pl.program_id(0) == 0)
        def _():
            loss_ref[...] = jnp.zeros_like(loss_ref)
            dg_ref[...] = jnp.zeros_like(dg_ref)

        loss_ref[...] += 0.5 * jnp.sum(jnp.mean(err * err, axis=-1, keepdims=True), axis=0, keepdims=True)
        dg_ref[...] += _colsum(dg)

    return pl.pallas_call(
        body, name=name, grid=(rows // ROW_TILE,),
        out_shape=(jax.ShapeDtypeStruct((rows, d), F32), jax.ShapeDtypeStruct((1, 1), F32),
                   jax.ShapeDtypeStruct((rows, d), MXU_DTYPE), jax.ShapeDtypeStruct((1, d), F32)),
        in_specs=[_row_spec(d), _row_spec(d), _vec_spec(d), _row_spec(d)],
        out_specs=(_row_spec(d), pl.BlockSpec((1, 1), lambda i: (0, 0)), _row_spec(d), _vec_spec(d)),
        compiler_params=_cparams(("arbitrary",)),
    )(h_in, o, g_post, target)


def _resid_bwd_pre_post(dh, dyn_list, h_in, g_pre, o_prev, g_post_prev, *, name, out_dtype):
    rows, d = dh.shape
    n_dyn = len(dyn_list)

    def body(*refs):
        dh_ref, dyn_refs = refs[0], refs[1:1 + n_dyn]
        h_ref, g_ref, o_ref, gp_ref, out_ref, do_ref, dg_ref, dgp_ref = refs[1 + n_dyn:]
        dyn = dyn_refs[0][...].astype(F32)
        for r in dyn_refs[1:]:
            dyn = dyn + r[...].astype(F32)
        dx, dg = _rms_bwd(dyn, h_ref[...], g_ref[...])
        dh_in = dh_ref[...] + dx
        out_ref[...] = dh_in
        do, dgp = _rms_bwd(dh_in, o_ref[...], gp_ref[...])
        do_ref[...] = do.astype(out_dtype)

        @pl.when(pl.program_id(0) == 0)
        def _():
            dg_ref[...] = jnp.zeros_like(dg_ref)
            dgp_ref[...] = jnp.zeros_like(dgp_ref)

        dg_ref[...] += _colsum(dg)
        dgp_ref[...] += _colsum(dgp)

    return pl.pallas_call(
        body, name=name, grid=(rows // ROW_TILE,),
        out_shape=(jax.ShapeDtypeStruct((rows, d), F32), jax.ShapeDtypeStruct((rows, d), out_dtype),
                   jax.ShapeDtypeStruct((1, d), F32), jax.ShapeDtypeStruct((1, d), F32)),
        in_specs=[_row_spec(d)] + [_row_spec(d)] * n_dyn + [_row_spec(d), _vec_spec(d), _row_spec(d), _vec_spec(d)],
        out_specs=(_row_spec(d), _row_spec(d), _vec_spec(d), _vec_spec(d)),
        compiler_params=_cparams(("arbitrary",)),
    )(dh, *dyn_list, h_in, g_pre, o_prev, g_post_prev)


def _resid_bwd_pre(dh, dyn_list, h_in, g_pre, *, name):
    rows, d = dh.shape
    n_dyn = len(dyn_list)

    def body(*refs):
        dh_ref, dyn_refs, h_ref, g_ref, out_ref, dg_ref = refs[0], refs[1:1 + n_dyn], *refs[1 + n_dyn:]
        dyn = dyn_refs[0][...].astype(F32)
        for r in dyn_refs[1:]:
            dyn = dyn + r[...].astype(F32)
        dx, dg = _rms_bwd(dyn, h_ref[...], g_ref[...])
        out_ref[...] = dh_ref[...] + dx

        @pl.when(pl.program_id(0) == 0)
        def _():
            dg_ref[...] = jnp.zeros_like(dg_ref)

        dg_ref[...] += _colsum(dg)

    return pl.pallas_call(
        body, name=name, grid=(rows // ROW_TILE,),
        out_shape=(jax.ShapeDtypeStruct((rows, d), F32), jax.ShapeDtypeStruct((1, d), F32)),
        in_specs=[_row_spec(d)] + [_row_spec(d)] * n_dyn + [_row_spec(d), _vec_spec(d)],
        out_specs=(_row_spec(d), _vec_spec(d)),
        compiler_params=_cparams(("arbitrary",)),
    )(dh, *dyn_list, h_in, g_pre)


def _layer_norm_stats(x):
    mu = jnp.mean(x, axis=-1, keepdims=True)
    xc = x - mu
    rstd = lax.rsqrt(jnp.mean(xc * xc, axis=-1, keepdims=True) + EPS)
    return xc * rstd, rstd


def _gmlp_fwd(proj, ln_g, ln_b, wm, bcol, *, name):
    rows = proj.shape[0]
    tr = ROW_TILE

    def body(u_ref, v_ref, lg_ref, lb_ref, wm_ref, bc_ref, ya_ref):
        vhat, _ = _layer_norm_stats(_gelu(v_ref[...]))
        vl = (vhat * lg_ref[...] + lb_ref[...]).astype(MXU_DTYPE)
        gu = _gelu(u_ref[...])
        bc = bc_ref[...]
        for c in range(tr // CHUNK):
            rs = slice(c * CHUNK, (c + 1) * CHUNK)
            for h in range(GM_HEADS):
                cs = slice(h * GM_HEAD_DIM, (h + 1) * GM_HEAD_DIM)
                mixed = _dot(wm_ref[h], vl[rs, cs]) + bc[:, h:h + 1]
                ya_ref[rs, cs] = (gu[rs, cs] * mixed).astype(MXU_DTYPE)

    return pl.pallas_call(
        body, name=name, grid=(rows // tr,),
        out_shape=jax.ShapeDtypeStruct((rows, 2 * D_MODEL), MXU_DTYPE),
        in_specs=[pl.BlockSpec((tr, D_MODEL), lambda i: (i, 2)), pl.BlockSpec((tr, D_MODEL), lambda i: (i, 3)),
                  _vec_spec(D_MODEL), _vec_spec(D_MODEL),
                  pl.BlockSpec((GM_HEADS, CHUNK, CHUNK), lambda i: (0, 0, 0)), _vec_spec(LANES, CHUNK)],
        out_specs=_row_spec(D_MODEL, tr),
        compiler_params=_cparams(("parallel",)),
    )(proj, proj, ln_g, ln_b, wm, bcol)


def _gmlp_bwd(proj, dcat, ln_g, ln_b, wm, wm_t, bcol, *, name):
    rows = proj.shape[0]
    tr = ROW_TILE

    def body(u_ref, v_ref, dy_ref, lg_ref, lb_ref, wm_ref, wmt_ref, bc_ref,
             duv_ref, dwm_ref, dbc_ref, dlg_ref, dlb_ref, dvl_scr):
        @pl.when(pl.program_id(0) == 0)
        def _():
            dwm_ref[...] = jnp.zeros_like(dwm_ref)
            dbc_ref[...] = jnp.zeros_like(dbc_ref)
            dlg_ref[...] = jnp.zeros_like(dlg_ref)
            dlb_ref[...] = jnp.zeros_like(dlb_ref)

        gv, gv_grad = _gelu_and_grad(v_ref[...])
        vhat, rstd = _layer_norm_stats(gv)
        lg = lg_ref[...]
        vl = (vhat * lg + lb_ref[...]).astype(MXU_DTYPE)
        gu, gu_grad = _gelu_and_grad(u_ref[...])
        dy = dy_ref[...].astype(F32)
        bc = bc_ref[...]
        row = lax.broadcasted_iota(jnp.int32, (CHUNK, CHUNK), 0)
        lane = lax.broadcasted_iota(jnp.int32, (CHUNK, CHUNK), 1)
        causal = lane <= row
        dbc = jnp.zeros((CHUNK, LANES), F32)
        for c in range(tr // CHUNK):
            rs = slice(c * CHUNK, (c + 1) * CHUNK)
            for h in range(GM_HEADS):
                cs = slice(h * GM_HEAD_DIM, (h + 1) * GM_HEAD_DIM)
                vl_h = vl[rs, cs]
                mixed = _dot(wm_ref[h], vl_h) + bc[:, h:h + 1]
                dy_h = dy[rs, cs]
                duv_ref[rs, cs] = (dy_h * mixed * gu_grad[rs, cs]).astype(MXU_DTYPE)
                dmixed = dy_h * gu[rs, cs]
                dwm_ref[h] += jnp.where(causal, _dot_nt(dmixed, vl_h), 0.0)
                dbc = dbc + jnp.where(lane == h, jnp.sum(dmixed, axis=1, keepdims=True), 0.0)
                dvl_scr[rs, cs] = _dot(wmt_ref[h], dmixed)
        dbc_ref[...] += dbc
        dvl = dvl_scr[...]
        dlg_ref[...] += _colsum(dvl * vhat)
        dlb_ref[...] += _colsum(dvl)
        dvh = dvl * lg
        dgv = rstd * (dvh - jnp.mean(dvh, axis=-1, keepdims=True) - vhat * jnp.mean(dvh * vhat, axis=-1, keepdims=True))
        duv_ref[:, D_MODEL:] = (dgv * gv_grad).astype(MXU_DTYPE)

    return pl.pallas_call(
        body, name=name, grid=(rows // tr,),
        out_shape=(jax.ShapeDtypeStruct((rows, IN_MAIN), MXU_DTYPE),
                   jax.ShapeDtypeStruct((GM_HEADS, CHUNK, CHUNK), F32), jax.ShapeDtypeStruct((CHUNK, LANES), F32),
                   jax.ShapeDtypeStruct((1, D_MODEL), F32), jax.ShapeDtypeStruct((1, D_MODEL), F32)),
        in_specs=[pl.BlockSpec((tr, D_MODEL), lambda i: (i, 2)), pl.BlockSpec((tr, D_MODEL), lambda i: (i, 3)),
                  pl.BlockSpec((tr, D_MODEL), lambda i: (i, 0)), _vec_spec(D_MODEL), _vec_spec(D_MODEL),
                  pl.BlockSpec((GM_HEADS, CHUNK, CHUNK), lambda i: (0, 0, 0)),
                  pl.BlockSpec((GM_HEADS, CHUNK, CHUNK), lambda i: (0, 0, 0)), _vec_spec(LANES, CHUNK)],
        out_specs=(pl.BlockSpec((tr, 2 * D_MODEL), lambda i: (i, 1)),
                   pl.BlockSpec((GM_HEADS, CHUNK, CHUNK), lambda i: (0, 0, 0)), _vec_spec(LANES, CHUNK),
                   _vec_spec(D_MODEL), _vec_spec(D_MODEL)),
        scratch_shapes=[pltpu.VMEM((tr, D_MODEL), F32)],
        compiler_params=_cparams(("arbitrary",)),
    )(proj, proj, dcat, ln_g, ln_b, wm, wm_t, bcol)


def _conv_fwd(proj, conv_w8, conv_b, *, name):
    rows = proj.shape[0]
    tr = ROW_TILE
    hb = tr // CONV_HALO

    def body(x_ref, prev_ref, w_ref, b_ref, pre_ref, buf):
        first = pl.program_id(0) == 0
        buf[pl.ds(0, CONV_HALO), :] = jnp.where(first, 0.0, prev_ref[...])
        buf[pl.ds(CONV_HALO, tr), :] = x_ref[...]
        ext = buf[...]
        acc = jnp.broadcast_to(b_ref[...], (tr, CONV_DIM))
        for k in range(SSM_CONV):
            s = SSM_CONV - 1 - k
            acc = acc + w_ref[k:k + 1, :] * (x_ref[...] if s == 0 else pltpu.roll(ext, s, axis=0)[CONV_HALO:])
        pre_ref[...] = acc

    return pl.pallas_call(
        body, name=name, grid=(rows // tr,),
        out_shape=jax.ShapeDtypeStruct((rows, CONV_DIM), F32),
        in_specs=[pl.BlockSpec((tr, CONV_DIM), lambda i: (i, 0)),
                  pl.BlockSpec((CONV_HALO, CONV_DIM), lambda i: (jnp.maximum(i * hb - 1, 0), 0)),
                  _vec_spec(CONV_DIM, 8), _vec_spec(CONV_DIM)],
        out_specs=_row_spec(CONV_DIM, tr),
        scratch_shapes=[pltpu.VMEM((tr + CONV_HALO, CONV_DIM), F32)],
        compiler_params=_cparams(("parallel",)),
    )(proj, proj, conv_w8, conv_b)


def _conv_bwd(dpre, proj, conv_w8, dproj, *, name):
    rows = proj.shape[0]
    tr = ROW_TILE
    halo = 16
    hb = tr // halo
    nblk = rows // tr

    def body(d_ref, dnext_ref, x_ref, w_ref, dproj_ref, dx_ref, dw_ref, db_ref, dbuf):
        i = pl.program_id(0)

        @pl.when(i == 0)
        def _():
            dw_ref[...] = jnp.zeros_like(dw_ref)
            db_ref[...] = jnp.zeros_like(db_ref)

        d = d_ref[...].astype(F32)
        x = x_ref[...]
        dbuf[pl.ds(0, tr), :] = d
        dbuf[pl.ds(tr, halo), :] = jnp.where(i == nblk - 1, 0.0, dnext_ref[...].astype(F32))
        ext = dbuf[...]
        acc = jnp.zeros((tr, CONV_DIM), F32)
        for k in range(SSM_CONV):
            s = SSM_CONV - 1 - k
            shifted = d if s == 0 else pltpu.roll(ext, tr + halo - s, axis=0)[:tr]
            acc = acc + w_ref[k:k + 1, :] * shifted
            dw_ref[k:k + 1, :] += _colsum(shifted * x)
        dx_ref[...] = acc.astype(MXU_DTYPE)
        db_ref[...] += _colsum(d)

    return pl.pallas_call(
        body, name=name, grid=(nblk,),
        out_shape=(jax.ShapeDtypeStruct((rows, IN_MAIN), MXU_DTYPE), jax.ShapeDtypeStruct((8, CONV_DIM), F32),
                   jax.ShapeDtypeStruct((1, CONV_DIM), F32)),
        in_specs=[_row_spec(CONV_DIM, tr),
                  pl.BlockSpec((halo, CONV_DIM), lambda i: (jnp.minimum((i + 1) * hb, rows // halo - 1), 0)),
                  pl.BlockSpec((tr, CONV_DIM), lambda i: (i, 0)),
                  _vec_spec(CONV_DIM, 8), pl.BlockSpec(memory_space=pl.ANY)],
        out_specs=(_row_spec(CONV_DIM, tr), _vec_spec(CONV_DIM, 8), _vec_spec(CONV_DIM)),
        scratch_shapes=[pltpu.VMEM((tr + halo, CONV_DIM), F32)],
        input_output_aliases={4: 0},
        compiler_params=_cparams(("arbitrary",)),
    )(dpre, dpre, proj, conv_w8, dproj)


def _chunk_iotas():
    row = lax.broadcasted_iota(jnp.int32, (CHUNK, CHUNK), 0)
    lane = lax.broadcasted_iota(jnp.int32, (CHUNK, CHUNK), 1)
    return row, lane, lane <= row


SSD_BWD_CHUNKS = 4
GROUP_DIM = D_MODEL // SSM_GROUPS
HEADS_PER_GROUP = SSM_HEADS // SSM_GROUPS
HEAD_DIM = GROUP_DIM // HEADS_PER_GROUP


def _split(x):
    hi = x.astype(MXU_DTYPE)
    return hi, (x - hi.astype(F32)).astype(MXU_DTYPE)


def _dot_split(x, sel):
    hi, lo = _split(x)
    return jnp.dot(hi, sel, preferred_element_type=F32) + jnp.dot(lo, sel, preferred_element_type=F32)


def _dot_split_rhs(sel, x):
    hi, lo = _split(x)
    return jnp.dot(sel, hi, preferred_element_type=F32) + jnp.dot(sel, lo, preferred_element_type=F32)


def _head_selectors():
    h = lax.broadcasted_iota(jnp.int32, (LANES, D_MODEL), 0)
    p = lax.broadcasted_iota(jnp.int32, (LANES, D_MODEL), 1)
    sel_t = (h == p // HEAD_DIM).astype(MXU_DTYPE)
    return sel_t, sel_t.T


def _expand_heads(per_head, e_end, selt_ref, sel_ref):
    stacked = jnp.concatenate(per_head, axis=0)
    wide = _dot_split(stacked, selt_ref[...])
    n = per_head[0].shape[0]
    e_cols = jnp.broadcast_to(e_end, (LANES, LANES)).T
    tall = _dot_split_rhs(sel_ref[...], e_cols)
    return [wide[n * i:n * (i + 1)] for i in range(len(per_head))], tall


def _by_quarter(index, pieces):
    out = pieces[3]
    for q in (2, 1, 0):
        out = jnp.where(index == q, pieces[q], out)
    return out


def _ssd_fwd_grouped(pre, dtr, proj, dtb, alog, dsk, gn, cat, *, name):
    rows = pre.shape[0]
    nc = rows // CHUNK

    def body(pre_ref, dtr_ref, z_ref, dtb_ref, alog_ref, dsk_ref, gn_ref, cat_ref, selt_ref, sel_ref, yb_ref, y_ref,
             st_ref, dt_ref, acum_ref, s_scr):
        @pl.when(pl.program_id(0) == 0)
        def _():
            s_scr[...] = jnp.zeros_like(s_scr)

        row, lane, tril = _chunk_iotas()
        dt = _softplus(dtr_ref[...] + dtb_ref[...])
        acum = _dot_exact(tril.astype(F32), dt * (-jnp.exp(alog_ref[...])))
        dt_ref[...] = dt
        acum_ref[...] = acum
        acum_t = acum.T
        a_end = acum[CHUNK - 1:CHUNK, :]
        (dt_x, eacum_x, dte_x, dsk_x), e_rows_all = _expand_heads(
            [dt, jnp.exp(acum), jnp.exp(a_end - acum), jnp.broadcast_to(dsk_ref[...], (CHUNK, LANES))], jnp.exp(a_end),
            selt_ref, sel_ref)
        lane_q = lax.broadcasted_iota(jnp.int32, (CHUNK, GROUP_DIM), 1) // HEAD_DIM

        for g in range(SSM_GROUPS):
            cs = slice(GROUP_DIM * g, GROUP_DIM * (g + 1))
            b_g = _silu(pre_ref[:, 1024 + SSM_STATE * g:1024 + SSM_STATE * (g + 1)]).astype(MXU_DTYPE)
            c_g = _silu(pre_ref[:, 1536 + SSM_STATE * g:1536 + SSM_STATE * (g + 1)]).astype(MXU_DTYPE)
            cb = _dot_nt(c_g, b_g)
            xs = _silu(pre_ref[:, cs])
            xdt = xs * dt_x[:, cs]
            m_stack = jnp.concatenate(
                [(cb * jnp.exp(jnp.where(tril, acum[:, h:h + 1] - acum_t[h:h + 1, :], -jnp.inf))).astype(MXU_DTYPE)
                 for h in range(4 * g, 4 * g + 4)], axis=0)
            y_all = _dot(m_stack, xdt)
            y = _by_quarter(lane_q, [y_all[CHUNK * q:CHUNK * (q + 1)] for q in range(HEADS_PER_GROUP)])
            s_prev = s_scr[g]
            st_ref[0, g] = s_prev
            y = y + _dot_nt(c_g, s_prev) * eacum_x[:, cs] + dsk_x[:, cs] * xs
            xw = xdt * dte_x[:, cs]
            s_scr[g] = e_rows_all[cs, :] * s_prev + _dot(xw.T, b_g)
            y_ref[:, cs] = y
            gated = y * _silu(z_ref[:, cs])
            r = lax.rsqrt(jnp.mean(gated * gated, axis=1, keepdims=True) + EPS)
            yb_ref[:, cs] = (gated * r * gn_ref[:, cs]).astype(MXU_DTYPE)

    return pl.pallas_call(
        body, name=name, grid=(nc,),
        out_shape=(jax.ShapeDtypeStruct((rows, 2 * D_MODEL), MXU_DTYPE), jax.ShapeDtypeStruct((rows, D_MODEL), F32),
                   jax.ShapeDtypeStruct((nc, SSM_GROUPS, GROUP_DIM, SSM_STATE), F32),
                   jax.ShapeDtypeStruct((rows, LANES), F32), jax.ShapeDtypeStruct((rows, LANES), F32)),
        in_specs=[_row_spec(CONV_DIM, CHUNK), _row_spec(LANES, CHUNK), pl.BlockSpec((CHUNK, D_MODEL), lambda i: (i, 4)),
                  _vec_spec(LANES), _vec_spec(LANES), _vec_spec(LANES), _vec_spec(D_MODEL),
                  pl.BlockSpec(memory_space=pl.ANY), _vec_spec(D_MODEL, LANES), _vec_spec(LANES, D_MODEL)],
        out_specs=(pl.BlockSpec((CHUNK, D_MODEL), lambda i: (i, 1)), _row_spec(D_MODEL, CHUNK),
                   pl.BlockSpec((1, SSM_GROUPS, GROUP_DIM, SSM_STATE), lambda i: (i, 0, 0, 0)),
                   _row_spec(LANES, CHUNK), _row_spec(LANES, CHUNK)),
        scratch_shapes=[pltpu.VMEM((SSM_GROUPS, GROUP_DIM, SSM_STATE), F32)],
        input_output_aliases={7: 0},
        compiler_params=_cparams(("arbitrary",)),
    )(pre, dtr, proj, dtb, alog, dsk, gn, cat, *_head_selectors())


def _ssd_bwd_grouped(pre, dtr, dt_saved, acum_saved, proj, y_saved, states, dcat, dtb, alog, dsk, gn, dproj, *, name):
    rows = pre.shape[0]
    cps = SSD_BWD_CHUNKS
    tr = CHUNK * cps
    nsteps = rows // tr

    def rev(i):
        return nsteps - 1 - i

    def body(pre_ref, dtr_ref, dt_ref, acum_ref, z_ref, y_ref, st_ref, dyb_ref, dtb_ref, alog_ref, dsk_ref, gn_ref,
             dproj_ref, selt_ref, sel_ref, dpre_ref, dz_ref, ddtr_ref, dgn_ref, dvec_ref, g_scr):
        @pl.when(pl.program_id(0) == 0)
        def _():
            g_scr[...] = jnp.zeros_like(g_scr)
            dgn_ref[...] = jnp.zeros_like(dgn_ref)
            dvec_ref[...] = jnp.zeros_like(dvec_ref)

        for cc in reversed(range(cps)):
            at = lambda ref: ref.at[pl.ds(cc * CHUNK, CHUNK)]
            chunk(at(pre_ref), at(dtr_ref), at(dt_ref), at(acum_ref), at(z_ref), at(y_ref), st_ref.at[cc], at(dyb_ref),
                  dtb_ref, alog_ref, dsk_ref, gn_ref, selt_ref, sel_ref, at(dpre_ref), at(dz_ref), at(ddtr_ref), dgn_ref,
                  dvec_ref, g_scr)

    def chunk(pre_ref, dtr_ref, dt_ref, acum_ref, z_ref, y_ref, st_ref, dyb_ref, dtb_ref, alog_ref, dsk_ref, gn_ref,
              selt_ref, sel_ref, dpre_ref, dz_ref, ddtr_ref, dgn_ref, dvec_ref, g_scr):
        row, lane, tril = _chunk_iotas()
        triu = lane >= row
        dt, acum = dt_ref[...], acum_ref[...]
        a = -jnp.exp(alog_ref[...])
        acum_t = acum.T
        a_end = acum[CHUNK - 1:CHUNK, :]
        e_end = jnp.exp(a_end)
        (dt_x, eacum_x, dte_x, dsk_x), e_rows_all = _expand_heads(
            [dt, jnp.exp(acum), jnp.exp(a_end - acum), jnp.broadcast_to(dsk_ref[...], (CHUNK, LANES))], e_end,
            selt_ref, sel_ref)
        lane_q = lax.broadcasted_iota(jnp.int32, (CHUNK, GROUP_DIM), 1) // HEAD_DIM
        zero = jnp.zeros((CHUNK, LANES), F32)
        dacum_c, dacum_r, ddt_c = zero, zero, zero
        d_aend = jnp.zeros((1, LANES), F32)
        d_dsk = jnp.zeros((1, LANES), F32)
        iota = lambda shape, dim: lax.broadcasted_iota(jnp.int32, shape, dim)
        q256, lane_256 = iota((GROUP_DIM, LANES), 0) // HEAD_DIM, iota((GROUP_DIM, LANES), 1)
        q512, lane_512 = iota((4 * CHUNK, LANES), 0) // CHUNK, iota((4 * CHUNK, LANES), 1)
        row_512t, q512t = iota((LANES, 4 * CHUNK), 0), iota((LANES, 4 * CHUNK), 1) // CHUNK

        for g in range(SSM_GROUPS):
            cs = slice(GROUP_DIM * g, GROUP_DIM * (g + 1))
            yv = y_ref[:, cs]
            sz, sz_grad = _silu_and_grad(z_ref[:, cs])
            gated = yv * sz
            dyb = dyb_ref[:, cs].astype(F32)
            dgh = dyb * gn_ref[:, cs]
            r = lax.rsqrt(jnp.mean(gated * gated, axis=1, keepdims=True) + EPS)
            dgn_ref[:, cs] += _colsum(dyb * gated * r)
            dgated = r * dgh - gated * (r * r * r * jnp.mean(dgh * gated, axis=1, keepdims=True))
            dy = dgated * sz
            dz_ref[:, cs] = (dgated * yv * sz_grad).astype(MXU_DTYPE)

            b_f, b_grad = _silu_and_grad(pre_ref[:, 1024 + SSM_STATE * g:1024 + SSM_STATE * (g + 1)])
            c_f, c_grad = _silu_and_grad(pre_ref[:, 1536 + SSM_STATE * g:1536 + SSM_STATE * (g + 1)])
            b_g, c_g = b_f.astype(MXU_DTYPE), c_f.astype(MXU_DTYPE)
            xs, xs_grad = _silu_and_grad(pre_ref[:, cs])
            dtq = dt_x[:, cs]
            xdt = xs * dtq
            xdt_m = xdt.astype(MXU_DTYPE)
            dy_m = dy.astype(MXU_DTYPE)
            s_prev = st_ref[g]
            g_next = g_scr[g]
            eacq, dteq = eacum_x[:, cs], dte_x[:, cs]
            t_off = dy * (_dot_nt(c_g, s_prev) * eacq)
            dye = dy * eacq
            dc_g = _dot(dye, s_prev)
            bg = _dot_nt(b_g, g_next)
            xw = xdt * dteq
            db_g = _dot(xw, g_next)
            t_w = xw * bg
            gs = g_next * s_prev
            g_scr[g] = e_rows_all[cs, :] * g_next + _dot(dye.T, c_g)
            cb = _dot_nt(c_g, b_g)
            cb_t = cb.T
            heads = range(4 * g, 4 * g + 4)
            decs = [jnp.exp(jnp.where(tril, acum[:, h:h + 1] - acum_t[h:h + 1, :], -jnp.inf)) for h in heads]
            mt_stack = jnp.concatenate(
                [(cb_t * jnp.exp(jnp.where(triu, acum_t[h:h + 1, :] - acum[:, h:h + 1], -jnp.inf))).astype(MXU_DTYPE)
                 for h in heads], axis=0)
            dy_stack = jnp.concatenate([jnp.where(lane_q == q, dy, 0.0).astype(MXU_DTYPE)
                                        for q in range(HEADS_PER_GROUP)], axis=0)
            dm_all = _dot_nt(dy_stack, xdt_m)
            dx_all = _dot(mt_stack, dy_m)
            dxdt = bg * dteq + _by_quarter(lane_q, [dx_all[CHUNK * q:CHUNK * (q + 1)] for q in range(HEADS_PER_GROUP)])
            t_dt = dxdt * xs
            t_dk = dy * xs
            dec_stack = jnp.concatenate(decs, axis=0)
            dm_dec = dm_all * dec_stack
            e_all = dm_dec * jnp.concatenate([cb] * HEADS_PER_GROUP, axis=0)
            dcb = functools.reduce(jnp.add, [dm_dec[CHUNK * q:CHUNK * (q + 1)] for q in range(HEADS_PER_GROUP)])
            one = jnp.ones((), MXU_DTYPE)
            sel_lanes = jnp.where(q256 + 4 * g == lane_256, one, 0)
            sel_rows = jnp.where(q512 + 4 * g == lane_512, one, 0)
            sel_rows_t = jnp.where(row_512t == q512t + 4 * g, one, 0)
            e_lanes = jnp.concatenate([e_all[CHUNK * q:CHUNK * (q + 1)] for q in range(HEADS_PER_GROUP)], axis=1)
            w_heads = _dot(t_w, sel_lanes)
            dacum_c = dacum_c + _dot(e_lanes, sel_rows) + _dot(t_off, sel_lanes) - w_heads
            dacum_r = dacum_r + _dot(sel_rows_t, e_all)
            ddt_c = ddt_c + _dot(t_dt, sel_lanes)
            d_aend = d_aend + _colsum(w_heads) + e_end * _colsum(_dot_tn(gs, sel_lanes))
            d_dsk = d_dsk + _colsum(_dot(t_dk, sel_lanes))
            dpre_ref[:, cs] = ((dxdt * dtq + dsk_x[:, cs] * dy) * xs_grad).astype(MXU_DTYPE)
            dc_g = dc_g + _dot(dcb, b_g)
            db_g = db_g + _dot(dcb.T, c_g)
            dpre_ref[:, 1024 + SSM_STATE * g:1024 + SSM_STATE * (g + 1)] = (db_g * b_grad).astype(MXU_DTYPE)
            dpre_ref[:, 1536 + SSM_STATE * g:1536 + SSM_STATE * (g + 1)] = (dc_g * c_grad).astype(MXU_DTYPE)

        dacum = dacum_c - dacum_r.T + jnp.where(row == CHUNK - 1, d_aend, 0.0)
        dda = _dot_exact(triu.astype(F32), dacum)
        ddtr = (dda * a + ddt_c) * _sigmoid_small(dtr_ref[...] + dtb_ref[...])
        ddtr_ref[...] = ddtr.astype(MXU_DTYPE)
        dvec_ref[0:1, :] += _colsum(ddtr)
        dvec_ref[1:2, :] += _colsum(dda * dt)
        dvec_ref[2:3, :] += d_dsk

    return pl.pallas_call(
        body, name=name, grid=(nsteps,),
        out_shape=(jax.ShapeDtypeStruct((rows, CONV_DIM), MXU_DTYPE), jax.ShapeDtypeStruct((rows, IN_MAIN), MXU_DTYPE),
                   jax.ShapeDtypeStruct((rows, LANES), MXU_DTYPE), jax.ShapeDtypeStruct((1, D_MODEL), F32),
                   jax.ShapeDtypeStruct((8, LANES), F32)),
        in_specs=[pl.BlockSpec((tr, CONV_DIM), lambda i: (rev(i), 0)), pl.BlockSpec((tr, LANES), lambda i: (rev(i), 0)),
                  pl.BlockSpec((tr, LANES), lambda i: (rev(i), 0)), pl.BlockSpec((tr, LANES), lambda i: (rev(i), 0)),
                  pl.BlockSpec((tr, D_MODEL), lambda i: (rev(i), 4)), pl.BlockSpec((tr, D_MODEL), lambda i: (rev(i), 0)),
                  pl.BlockSpec((cps, SSM_GROUPS, GROUP_DIM, SSM_STATE), lambda i: (rev(i), 0, 0, 0)),
                  pl.BlockSpec((tr, D_MODEL), lambda i: (rev(i), 1)),
                  _vec_spec(LANES), _vec_spec(LANES), _vec_spec(LANES), _vec_spec(D_MODEL),
                  pl.BlockSpec(memory_space=pl.ANY), _vec_spec(D_MODEL, LANES), _vec_spec(LANES, D_MODEL)],
        out_specs=(pl.BlockSpec((tr, CONV_DIM), lambda i: (rev(i), 0)), pl.BlockSpec((tr, D_MODEL), lambda i: (rev(i), 4)),
                   pl.BlockSpec((tr, LANES), lambda i: (rev(i), 0)), _vec_spec(D_MODEL), _vec_spec(LANES, 8)),
        scratch_shapes=[pltpu.VMEM((SSM_GROUPS, GROUP_DIM, SSM_STATE), F32)],
        input_output_aliases={12: 1},
        compiler_params=_cparams(("arbitrary",)),
    )(pre, dtr, dt_saved, acum_saved, proj, y_saved, states, dcat, dtb, alog, dsk, gn, dproj, *_head_selectors())


def _pool_counts(first_row, n_rows, win):
    t = first_row + lax.broadcasted_iota(jnp.int32, (n_rows, POOL_DIM), 0)
    return jnp.minimum(t + 1, win).astype(F32)


def _pool_fwd(yn, pool_w, pool_b, pool_scale, *, name):
    rows = yn.shape[0]
    tr = ROW_TILE
    hb = tr // POOL_HALO

    def body(y_ref, prev_ref, w_ref, b_ref, s_ref, pm_ref, diff_ref, buf):
        i = pl.program_id(0)
        buf[pl.ds(0, POOL_HALO), :] = jnp.where(i == 0, 0.0, prev_ref[...])
        buf[pl.ds(POOL_HALO, tr), :] = y_ref[...]
        level = buf[...]
        sums = []
        for g, win in enumerate(POOL_WINDOWS):
            level = level + pltpu.roll(level, win // 2, axis=0)
            sums.append(level[POOL_HALO:, :POOL_DIM])
            if g + 1 < len(POOL_WINDOWS):
                level = level[:, POOL_DIM:]
        for g, win in enumerate(POOL_WINDOWS):
            cs = slice(POOL_DIM * g, POOL_DIM * (g + 1))
            diff = (sums[g] / _pool_counts(i * tr, tr, win) - y_ref[:, cs]).astype(MXU_DTYPE)
            diff_ref[:, cs] = diff
            pm_ref[:, cs] = (_dot(diff, w_ref[g]) + b_ref[:, cs]) * s_ref[:, cs]

    return pl.pallas_call(
        body, name=name, grid=(rows // tr,),
        out_shape=(jax.ShapeDtypeStruct((rows, D_MODEL), F32), jax.ShapeDtypeStruct((rows, D_MODEL), MXU_DTYPE)),
        in_specs=[_row_spec(D_MODEL, tr),
                  pl.BlockSpec((POOL_HALO, D_MODEL), lambda i: (jnp.maximum(i * hb - 1, 0), 0)),
                  pl.BlockSpec((4, POOL_DIM, POOL_DIM), lambda i: (0, 0, 0)), _vec_spec(D_MODEL), _vec_spec(D_MODEL)],
        out_specs=(_row_spec(D_MODEL, tr), _row_spec(D_MODEL, tr)),
        scratch_shapes=[pltpu.VMEM((tr + POOL_HALO, D_MODEL), F32)],
        compiler_params=_cparams(("parallel",)),
    )(yn, yn, pool_w, pool_b, pool_scale)


def _pool_bwd(dpm, diff, pool_w, pool_w_t, pool_b, pool_scale, *, name):
    rows = dpm.shape[0]
    tr = ROW_TILE
    hb = tr // POOL_HALO
    nblk = rows // tr

    def body(d_ref, dnext_ref, diff_ref, w_ref, wt_ref, b_ref, s_ref, dy_ref, dw_ref, db_ref, ds_ref, ebuf):
        i = pl.program_id(0)

        @pl.when(i == 0)
        def _():
            dw_ref[...] = jnp.zeros_like(dw_ref)
            db_ref[...] = jnp.zeros_like(db_ref)
            ds_ref[...] = jnp.zeros_like(ds_ref)

        last = i == nblk - 1
        for g, win in enumerate(POOL_WINDOWS):
            cs = slice(POOL_DIM * g, POOL_DIM * (g + 1))
            d = d_ref[:, cs]
            diff = diff_ref[:, cs]
            out_pre = _dot(diff, w_ref[g]) + b_ref[:, cs]
            ds_ref[:, cs] += _colsum(d * out_pre)
            dout = d * s_ref[:, cs]
            db_ref[:, cs] += _colsum(dout)
            dw_ref[g] += _dot_tn(diff, dout)
            ddiff = _dot(dout, wt_ref[g])
            ddiff_next = _dot(jnp.where(last, 0.0, dnext_ref[:, cs]) * s_ref[:, cs], wt_ref[g])
            ebuf[pl.ds(0, tr), cs] = ddiff / _pool_counts(i * tr, tr, win)
            ebuf[pl.ds(tr, POOL_HALO), cs] = ddiff_next / _pool_counts((i + 1) * tr, POOL_HALO, win)
            dy_ref[:, cs] = -ddiff
        level = ebuf[...]
        n = tr + POOL_HALO
        for g, win in enumerate(POOL_WINDOWS):
            level = level + pltpu.roll(level, n - win // 2, axis=0)
            dy_ref[:, POOL_DIM * g:POOL_DIM * (g + 1)] += level[:tr, :POOL_DIM]
            if g + 1 < len(POOL_WINDOWS):
                level = level[:, POOL_DIM:]

    return pl.pallas_call(
        body, name=name, grid=(nblk,),
        out_shape=(jax.ShapeDtypeStruct((rows, D_MODEL), F32), jax.ShapeDtypeStruct((4, POOL_DIM, POOL_DIM), F32),
                   jax.ShapeDtypeStruct((1, D_MODEL), F32), jax.ShapeDtypeStruct((1, D_MODEL), F32)),
        in_specs=[_row_spec(D_MODEL, tr),
                  pl.BlockSpec((POOL_HALO, D_MODEL), lambda i: (jnp.minimum((i + 1) * hb, rows // POOL_HALO - 1), 0)),
                  _row_spec(D_MODEL, tr),
                  pl.BlockSpec((4, POOL_DIM, POOL_DIM), lambda i: (0, 0, 0)),
                  pl.BlockSpec((4, POOL_DIM, POOL_DIM), lambda i: (0, 0, 0)), _vec_spec(D_MODEL), _vec_spec(D_MODEL)],
        out_specs=(_row_spec(D_MODEL, tr), pl.BlockSpec((4, POOL_DIM, POOL_DIM), lambda i: (0, 0, 0)),
                   _vec_spec(D_MODEL), _vec_spec(D_MODEL)),
        scratch_shapes=[pltpu.VMEM((tr + POOL_HALO, D_MODEL), F32)],
        compiler_params=_cparams(("arbitrary",)),
    )(dpm, dpm, diff, pool_w, pool_w_t, pool_b, pool_scale)


def _row_tile(rows, cap, step):
    best = rows
    for t in range(step, min(rows, cap) + 1, step):
        if rows % t == 0:
            best = t
    return best if best <= cap else rows


def _sum8(recv, *, name):
    _, r, c = recv.shape
    step = 8 if recv.dtype == F32 else 16

    def body(r_ref, g_ref):
        g = r_ref[0].astype(F32)
        for j in range(1, N_DEV):
            g = g + r_ref[j].astype(F32)
        g_ref[...] = g

    if r % step == 0:
        tr = _row_tile(r, 256, step)
        grid, in_spec, out_spec = (r // tr,), pl.BlockSpec((N_DEV, tr, c), lambda i: (0, i, 0)), pl.BlockSpec((tr, c), lambda i: (i, 0))
    else:
        tc = 256
        grid, in_spec, out_spec = (c // tc,), pl.BlockSpec((N_DEV, r, tc), lambda i: (0, 0, i)), pl.BlockSpec((r, tc), lambda i: (0, i))
    return pl.pallas_call(
        body, name=name, grid=grid, out_shape=jax.ShapeDtypeStruct((r, c), F32),
        in_specs=[in_spec], out_specs=out_spec, compiler_params=_cparams(("parallel",)),
    )(recv)


def _adamw(g, w, m, v, *, name):
    rows, cols = w.shape
    tr = _row_tile(rows, max(8, (256 * 1024) // cols // 8 * 8), 8)
    c1 = 1.0 / (1.0 - ADAM_B1 ** ADAM_STEP)
    c2 = 1.0 / (1.0 - ADAM_B2 ** ADAM_STEP)

    def body(g_ref, w_ref, m_ref, v_ref, d_ref, mo_ref, vo_ref):
        g = g_ref[...]
        m_new = ADAM_B1 * m_ref[...] + (1.0 - ADAM_B1) * g
        v_new = ADAM_B2 * v_ref[...] + (1.0 - ADAM_B2) * (g * g)
        mo_ref[...] = m_new
        vo_ref[...] = v_new
        d_ref[...] = -ADAM_LR * ((m_new * c1) / (jnp.sqrt(v_new * c2) + ADAM_EPS) + ADAM_WD * w_ref[...])

    spec = pl.BlockSpec((tr, cols), lambda i: (i, 0))
    return pl.pallas_call(
        body, name=name, grid=(rows // tr,),
        out_shape=tuple(jax.ShapeDtypeStruct((rows, cols), F32) for _ in range(3)),
        in_specs=[spec] * 4, out_specs=(spec, spec, spec),
        compiler_params=_cparams(("parallel",)),
    )(g, w, m, v)


def _pad_rows(flat, mult):
    n = flat.shape[-1]
    pad = (-n) % mult
    if pad:
        flat = jnp.pad(flat, [(0, 0)] * (flat.ndim - 1) + [(0, pad)])
    return flat


def _pack_blocks(blocks, row_mult):
    flat = jnp.concatenate([_pad_rows(b.reshape(-1), LANES) for b in blocks])
    return _pad_rows(flat, LANES * row_mult).reshape(-1, LANES)


def _block_sizes(blocks):
    return [-(-math.prod(b.shape) // LANES) * LANES for b in blocks]


def _unpack_blocks(slab, like, lead=()):
    flat = slab.reshape(lead + (-1,))
    out, off = [], 0
    for b, size in zip(like, _block_sizes(like)):
        n = math.prod(b.shape)
        out.append(flat[..., off:off + n].reshape(lead + tuple(b.shape)))
        off += size
    return out


def _join_shards(gathered, axis):
    return jnp.concatenate([gathered[j] for j in range(N_DEV)], axis=axis)


def _split_shards(full, axis):
    return jnp.stack(jnp.split(full, N_DEV, axis=axis))


def _interleave_ff(w_gate, w_up):
    k = w_gate.shape[0]
    nt = D_FF // FF_TILE
    return jnp.stack([w_gate.reshape(k, nt, FF_TILE), w_up.reshape(k, nt, FF_TILE)], axis=2).reshape(k, 2 * D_FF)


def _row128(vec):
    return jnp.pad(vec.reshape(1, -1), ((0, 0), (0, LANES - vec.shape[-1])))


def kernel(x, norm_g, w_in, gm_ln_g, gm_ln_b, gm_ws, gm_bs, conv_w, conv_b, dt_bias, a_log, d_skip, ssm_norm_g, w_out, pool_w, pool_b, pool_scale, ffn_w_gate, ffn_w_up, ffn_w_down, loss_target, m_norm_g, m_w_in, m_gm_ln_g, m_gm_ln_b, m_gm_ws, m_gm_bs, m_conv_w, m_conv_b, m_dt_bias, m_a_log, m_d_skip, m_ssm_norm_g, m_w_out, m_pool_w, m_pool_b, m_pool_scale, m_ffn_w_gate, m_ffn_w_up, m_ffn_w_down, v_norm_g, v_w_in, v_gm_ln_g, v_gm_ln_b, v_gm_ws, v_gm_bs, v_conv_w, v_conv_b, v_dt_bias, v_a_log, v_d_skip, v_ssm_norm_g, v_w_out, v_pool_w, v_pool_b, v_pool_scale, v_ffn_w_gate, v_ffn_w_up, v_ffn_w_down):
    w_loc = dict(norm_g=norm_g, w_in=w_in, gm_ln_g=gm_ln_g, gm_ln_b=gm_ln_b, gm_ws=gm_ws, gm_bs=gm_bs, conv_w=conv_w,
                 conv_b=conv_b, dt_bias=dt_bias, a_log=a_log, d_skip=d_skip, ssm_norm_g=ssm_norm_g, w_out=w_out,
                 pool_w=pool_w, pool_b=pool_b, pool_scale=pool_scale, ffn_w_gate=ffn_w_gate, ffn_w_up=ffn_w_up,
                 ffn_w_down=ffn_w_down)
    m_loc = dict(zip(WEIGHTS, [m_norm_g, m_w_in, m_gm_ln_g, m_gm_ln_b, m_gm_ws, m_gm_bs, m_conv_w, m_conv_b, m_dt_bias,
                               m_a_log, m_d_skip, m_ssm_norm_g, m_w_out, m_pool_w, m_pool_b, m_pool_scale,
                               m_ffn_w_gate, m_ffn_w_up, m_ffn_w_down]))
    v_loc = dict(zip(WEIGHTS, [v_norm_g, v_w_in, v_gm_ln_g, v_gm_ln_b, v_gm_ws, v_gm_bs, v_conv_w, v_conv_b, v_dt_bias,
                               v_a_log, v_d_skip, v_ssm_norm_g, v_w_out, v_pool_w, v_pool_b, v_pool_scale,
                               v_ffn_w_gate, v_ffn_w_up, v_ffn_w_down]))

    small_blocks = [w_loc[n] for n in GATHER_F32]
    got = _gather_two_level([w_in[0].astype(MXU_DTYPE), _pack_blocks(small_blocks, 8)], name="gather_first")
    full = {n: w_loc[n] for n in WEIGHTS if SHARD_AXIS[n] is None}
    full['w_in'] = got[0].transpose(1, 0, 2).reshape(1, D_MODEL, -1)
    for n, g in zip(GATHER_F32, _unpack_blocks(got[1], small_blocks, (N_DEV,))):
        full[n] = _join_shards(g, SHARD_AXIS[n])
    shards = {n: w_loc[n].astype(MXU_DTYPE) for n in ('w_out', 'ffn_w_gate', 'ffn_w_up', 'ffn_w_down', 'pool_w')}

    loss_part, grad_x, grads, recv = _local_step(x[0], loss_target[0], full, shards)

    small = [n for n in WEIGHTS if n not in BIG_WEIGHTS]
    like = [w_loc[n] for n in small]
    slots = []
    for n in small:
        ax = SHARD_AXIS[n]
        g = grads[n].astype(F32)
        sh = _split_shards(g, ax) if ax is not None else jnp.broadcast_to(g[None], (N_DEV,) + g.shape)
        slots.append(_pad_rows(sh.reshape(N_DEV, -1), LANES))
    send_small = _pad_rows(jnp.concatenate(slots, axis=1), LANES * 8).reshape(N_DEV, -1, LANES)
    recv_small, = _exchange([send_small], ['slots'], name="exchange_last")

    g_small = _sum8(recv_small, name="sum_small")
    g_own = dict(zip(small, _unpack_blocks(g_small, like)))
    g_own['w_in'] = _sum8(recv['w_in'], name="sum_w_in").T[None]
    g_own['w_out'] = _sum8(recv['w_out'], name="sum_w_out")[None]
    g_own['ffn_w_gate'] = jnp.stack([_sum8(recv['ffn_w_gate'][l], name=f"sum_ffn{l}_gate").T for l in range(2)])
    g_own['ffn_w_up'] = jnp.stack([_sum8(recv['ffn_w_up'][l], name=f"sum_ffn{l}_up").T for l in range(2)])
    g_own['ffn_w_down'] = jnp.stack([_sum8(recv['ffn_w_down'][l], name=f"sum_ffn{l}_down") for l in range(2)])

    delta, m_new, v_new = {}, {}, {}
    pk = lambda d: _pack_blocks([d[n] for n in small], 8)
    d_s, m_s, v_s = _adamw(g_small, pk(w_loc), pk(m_loc), pk(v_loc), name="adamw_small")
    for dst, slab in ((delta, d_s), (m_new, m_s), (v_new, v_s)):
        dst.update(zip(small, _unpack_blocks(slab, like)))
    for n in BIG_WEIGHTS:
        shape = w_loc[n].shape
        two_d = lambda t: t.reshape(-1, shape[-1])
        res = _adamw(two_d(g_own[n]), two_d(w_loc[n]), two_d(m_loc[n]), two_d(v_loc[n]), name=f"adamw_{n}")
        delta[n], m_new[n], v_new[n] = (t.reshape(shape) for t in res)

    loss = lax.psum(loss_part[0, 0], ("x", "y", "c"))
    outs = [d[n] for d in (g_own, delta, m_new, v_new) for n in WEIGHTS]
    return (loss, grad_x[None], *outs)


def _local_step(h0, tgt, full, shards):
    gm_ln_g, gm_ln_b, gm_ws, gm_bs = full['gm_ln_g'], full['gm_ln_b'], full['gm_ws'], full['gm_bs']
    conv_b, dt_bias, a_log, d_skip, ssm_norm_g = (full['conv_b'], full['dt_bias'], full['a_log'], full['d_skip'],
                                                  full['ssm_norm_g'])
    w_in_f = full['w_in'][0]
    w_main = jnp.concatenate([w_in_f[:, 3072:5120], w_in_f[:, :3072]], axis=1)
    w_dt = jnp.pad(w_in_f[:, 5120:], ((0, 0), (0, LANES - SSM_HEADS)))
    ng = full['norm_g']

    def ffn_shards(layer):
        return [shards['ffn_w_gate'][layer], shards['ffn_w_up'][layer], shards['ffn_w_down'][layer]]

    def ffn_weights(got_gate, got_up, got_down):
        cols = lambda g: g.transpose(1, 0, 2).reshape(D_MODEL, D_FF)
        return _interleave_ff(cols(got_gate), cols(got_up)), got_down.reshape(D_FF, D_MODEL)

    w_gu, w_dn = [None, None], [None, None]
    causal = jnp.tril(jnp.ones((CHUNK, CHUNK), bool))
    wm = jnp.where(causal[None], gm_ws[0], 0.0).astype(MXU_DTYPE)
    wm_t = jnp.swapaxes(wm, 1, 2)
    bcol = jnp.pad(gm_bs[0].T, ((0, 0), (0, LANES - GM_HEADS)))
    conv_w8 = jnp.pad(full['conv_w'][0], ((0, 8 - SSM_CONV), (0, 0)))
    dtb, alog, dsk = _row128(dt_bias[0]), _row128(a_log[0]), _row128(d_skip[0])
    pool_b_f = full['pool_b'][0].reshape(1, D_MODEL)
    pool_s_f = full['pool_scale']

    def g_(layer, i):
        return ng[layer, i].reshape(1, D_MODEL)

    yn0 = _rn_fwd(h0, g_(0, 0), name="rn_fwd_0", out_dtype=MXU_DTYPE)
    proj, got = _mm(yn0, w_main, name="mm_in_proj", tm=2048,
                    ex=_Exchange([shards['w_out'][0]] + ffn_shards(0), ['gather'] * 4))
    w_out_f = got[0].reshape(-1, D_MODEL)
    w_gu[0], w_dn[0] = ffn_weights(*got[1:])
    dtr = _mm(yn0, w_dt, name="mm_in_proj_dt")
    pre = _conv_fwd(proj, conv_w8, conv_b, name="conv_fwd")
    cat = _gmlp_fwd(proj, gm_ln_g, gm_ln_b, wm, bcol, name="gmlp_fwd")
    cat, y_ssd, states, dt_ssd, acum_ssd = _ssd_fwd_grouped(pre, dtr, proj, dtb, alog, dsk, ssm_norm_g, cat,
                                                            name="ssd_fwd")
    o0 = _mm(cat, w_out_f, name="mm_out_proj", tm=1024, tn=1024)
    h1, yn1 = _resid_rn_fwd(h0, o0, g_(0, 1), g_(0, 2), name="resid_fwd_0a", next_dtype=MXU_DTYPE)
    (gu0, act0), got = _mm_swiglu(yn1, w_gu[0], name="mm_ffn0_gate_up",
                                  ex=_Exchange(ffn_shards(1) + [shards['pool_w'][0]], ['gather'] * 4))
    w_gu[1], w_dn[1] = ffn_weights(*got[:3])
    pool_w_f = got[3].transpose(1, 0, 2, 3).reshape(4, POOL_DIM, POOL_DIM)
    d0 = _mm(act0, w_dn[0], name="mm_ffn0_down", tm=1024, tn=1024)
    h2, yn2 = _resid_rn_fwd(h1, d0, g_(0, 3), g_(1, 0), name="resid_fwd_0b", next_dtype=F32)
    pm, pdiff = _pool_fwd(yn2, pool_w_f, pool_b_f, pool_s_f, name="pool_fwd")
    h3, yn3 = _resid_rn_fwd(h2, pm, g_(1, 1), g_(1, 2), name="resid_fwd_1a", next_dtype=MXU_DTYPE)
    gu1, act1 = _mm_swiglu(yn3, w_gu[1], name="mm_ffn1_gate_up")
    d1 = _mm(act1, w_dn[1], name="mm_ffn1_down", tm=1024, tn=1024)
    grads = {}
    recv = {'ffn_w_gate': [None, None], 'ffn_w_up': [None, None], 'ffn_w_down': [None, None]}
    dng = [[None] * 4 for _ in range(2)]
    dh4, loss_part, dd1, dng[1][3] = _resid_loss(h3, d1, g_(1, 3), tgt, name="resid_loss")

    def ffn_bwd(layer, dd, gu, act, yn):
        dw_dn = _mm_tn(act, dd, name=f"mm_ffn{layer}_dw_down", out_dtype=MXU_DTYPE, tm=1408, tn=1024)
        dgu = _mm_dswiglu(dd, w_dn[layer].T, gu, name=f"mm_ffn{layer}_dact")
        dw_g_t, dw_u_t = _mm_tn_gate_up(dgu, yn, name=f"mm_ffn{layer}_dw_gate_up", out_dtype=MXU_DTYPE)
        dyn, got = _mm(dgu, w_gu[layer].T, name=f"mm_ffn{layer}_dyn", out_dtype=MXU_DTYPE, tm=512, tn=1024,
                       ex=_Exchange([dw_g_t, dw_u_t, dw_dn], ['rows'] * 3))
        recv['ffn_w_gate'][layer], recv['ffn_w_up'][layer], recv['ffn_w_down'][layer] = got
        return dyn

    dyn3 = ffn_bwd(1, dd1, gu1, act1, yn3)
    dh3, dpm, dng[1][2], dng[1][1] = _resid_bwd_pre_post(dh4, [dyn3], h3, g_(1, 2), pm, g_(1, 1), name="resid_bwd_1b_1a",
                                                         out_dtype=F32)
    dyn2, d_pool_w, d_pool_b, d_pool_s = _pool_bwd(dpm, pdiff, pool_w_f, jnp.swapaxes(pool_w_f, 1, 2), pool_b_f, pool_s_f,
                                                   name="pool_bwd")
    dh2, dd0, dng[1][0], dng[0][3] = _resid_bwd_pre_post(dh3, [dyn2], h2, g_(1, 0), d0, g_(0, 3), name="resid_bwd_1a_0b",
                                                         out_dtype=MXU_DTYPE)
    dyn1 = ffn_bwd(0, dd0, gu0, act0, yn1)
    dh1, do0, dng[0][2], dng[0][1] = _resid_bwd_pre_post(dh2, [dyn1], h1, g_(0, 2), o0, g_(0, 1), name="resid_bwd_0b_0a",
                                                         out_dtype=MXU_DTYPE)
    d_w_out =_mm_tn(cat, do0, name="mm_out_proj_dw", out_dtype=MXU_DTYPE, tn=1024)
    dcat, got = _mm(do0, w_out_f.T, name="mm_out_proj_dx", out_dtype=MXU_DTYPE, tm=2048, tn=1024, ex=_Exchange([d_w_out], ['rows']))
    recv['w_out'] = got[0]
    dproj, d_wm, d_bcol, d_ln_g, d_ln_b = _gmlp_bwd(proj, dcat, gm_ln_g, gm_ln_b, wm, wm_t, bcol, name="gmlp_bwd")
    dpre, dproj, ddtr, d_gn, d_vec = _ssd_bwd_grouped(pre, dtr, dt_ssd, acum_ssd, proj, y_ssd, states, dcat, dtb, alog,
                                                      dsk, ssm_norm_g, dproj, name="ssd_bwd")
    dproj, d_conv_w8, d_conv_b = _conv_bwd(dpre, proj, conv_w8, dproj, name="conv_bwd")
    d_w_main_t = _mm_tn(dproj, yn0, name="mm_in_proj_dw", out_dtype=MXU_DTYPE, tn=1024, shift=3)
    d_w_dt_t = _mm_tn(ddtr, yn0, name="mm_in_proj_dt_dw", out_dtype=MXU_DTYPE, tn=1024)
    d_w_in_t = jnp.concatenate([d_w_main_t, d_w_dt_t[:SSM_HEADS]], axis=0).reshape(N_DEV, -1, D_MODEL)
    dyn0, got = _mm(dproj, w_main.T, name="mm_in_proj_dx", out_dtype=MXU_DTYPE, tm=512, tn=1024,
                    ex=_Exchange([d_w_in_t], ['slots']))
    recv['w_in'] = got[0]
    dyn0_dt = _mm(ddtr, w_dt.T, name="mm_in_proj_dt_dx", out_dtype=MXU_DTYPE)
    grad_x, dng[0][0] = _resid_bwd_pre(dh1, [dyn0, dyn0_dt], h0, g_(0, 0), name="resid_bwd_pre_0a")

    grads['norm_g'] = jnp.stack([jnp.concatenate(dng[l], axis=0) for l in range(2)])
    grads['gm_ln_g'], grads['gm_ln_b'] = d_ln_g, d_ln_b
    grads['gm_ws'] = d_wm[None]
    grads['gm_bs'] = d_bcol[:, :GM_HEADS].T[None]
    grads['conv_w'] = d_conv_w8[None, :SSM_CONV]
    grads['conv_b'] = d_conv_b
    grads['dt_bias'] = d_vec[0:1, :SSM_HEADS]
    grads['a_log'] = d_vec[1:2, :SSM_HEADS] * (-jnp.exp(a_log))
    grads['d_skip'] = d_vec[2:3, :SSM_HEADS]
    grads['ssm_norm_g'] = d_gn
    grads['pool_w'] = d_pool_w[None]
    grads['pool_b'] = d_pool_b.reshape(1, 4, POOL_DIM)
    grads['pool_scale'] = d_pool_s
    return loss_part, grad_x, grads, recv
```

```python
import functools
import math

import jax
import jax.numpy as jnp
from jax import lax
from jax.experimental import pallas as pl
from jax.experimental.pallas import tpu as pltpu

F32 = jnp.float32
MXU_DTYPE = jnp.bfloat16

N_DEV = 8
D_MODEL = 1024
EPS = 1e-6
GM_HEADS = 4
GM_HEAD_DIM = 256
CHUNK = 128
SSM_HEADS = 16
SSM_GROUPS = 4
SSM_STATE = 128
SSM_CONV = 4
CONV_DIM = 2048
POOL_WINDOWS = (2, 4, 8, 16)
POOL_DIM = 256
D_FF = 2816
FF_TILE = 256
IN_MAIN = 5120
LANES = 128
CONV_HALO = 8
POOL_HALO = 16
ADAM_LR, ADAM_B1, ADAM_B2, ADAM_EPS, ADAM_WD, ADAM_STEP = 0.001, 0.9, 0.999, 1e-08, 0.01, 10

VMEM_LIMIT = 56 * 1024 * 1024
ROW_TILE = 512
MM_TM = 2048

WEIGHTS = ['norm_g', 'w_in', 'gm_ln_g', 'gm_ln_b', 'gm_ws', 'gm_bs', 'conv_w', 'conv_b', 'dt_bias', 'a_log',
           'd_skip', 'ssm_norm_g', 'w_out', 'pool_w', 'pool_b', 'pool_scale', 'ffn_w_gate', 'ffn_w_up', 'ffn_w_down']
SHARD_AXIS = {'norm_g': 2, 'w_in': 2, 'gm_ln_g': None, 'gm_ln_b': None, 'gm_ws': None, 'gm_bs': None, 'conv_w': 2,
              'conv_b': None, 'dt_bias': None, 'a_log': None, 'd_skip': None, 'ssm_norm_g': None, 'w_out': 1,
              'pool_w': 2, 'pool_b': 2, 'pool_scale': 1, 'ffn_w_gate': 2, 'ffn_w_up': 2, 'ffn_w_down': 1}
GATHER_F32 =['norm_g', 'conv_w', 'pool_b', 'pool_scale']
BIG_WEIGHTS = ['w_in', 'w_out', 'ffn_w_gate', 'ffn_w_up', 'ffn_w_down']


def _cparams(sem=None):
    return pltpu.CompilerParams(dimension_semantics=sem, vmem_limit_bytes=VMEM_LIMIT)


def _dot(a, b):
    return jnp.dot(a.astype(MXU_DTYPE), b.astype(MXU_DTYPE), preferred_element_type=F32)


def _dot_nt(a, b):
    return lax.dot_general(a.astype(MXU_DTYPE), b.astype(MXU_DTYPE), (((1,), (1,)), ((), ())),
                           preferred_element_type=F32)


def _dot_tn(a, b):
    return lax.dot_general(a.astype(MXU_DTYPE), b.astype(MXU_DTYPE), (((0,), (0,)), ((), ())),
                           preferred_element_type=F32)


def _dot_exact(a, b):
    return jnp.dot(a, b, precision=lax.Precision.HIGHEST, preferred_element_type=F32)


def _sigmoid(x):
    return 0.5 * jnp.tanh(0.5 * x) + 0.5


def _sigmoid_small(x):
    return 1.0 / (1.0 + jnp.exp(-x))


def _silu(x):
    return x * _sigmoid(x)


def _silu_and_grad(x):
    s = _sigmoid(x)
    return x * s, s * (1.0 + x * (1.0 - s))


_GELU_C = math.sqrt(2.0 / math.pi)


def _gelu(x):
    return _gelu_and_grad(x)[0]


def _gelu_and_grad(x):
    x2 = x * x
    t = jnp.tanh(_GELU_C * x * (1.0 + 0.044715 * x2))
    half = 0.5 * (1.0 + t)
    return x * half, half + 0.5 * x * (1.0 - t * t) * (_GELU_C * (1.0 + 3.0 * 0.044715 * x2))


def _softplus(x):
    return jnp.maximum(x, 0.0) + jnp.log1p(jnp.exp(-jnp.abs(x)))


def _rms_scale(x):
    return lax.rsqrt(jnp.mean(x * x, axis=-1, keepdims=True) + EPS)


def _rms_bwd(dy, x, g):
    r = _rms_scale(x)
    xn = x * r
    dxn = dy * g
    dx = r * (dxn - xn * jnp.mean(dxn * xn, axis=-1, keepdims=True))
    return dx, dy * xn


def _colsum(x):
    return jnp.sum(x, axis=0, keepdims=True)


class _Exchange:
    def __init__(self, arrays, modes):
        self.arrays, self.modes, self.n = list(arrays), list(modes), len(arrays)
        self.blks = []
        for x, mode in zip(arrays, modes):
            if mode == 'gather':
                self.blks.append(tuple(x.shape))
            elif mode == 'slots':
                self.blks.append(tuple(x.shape[1:]))
            else:
                self.blks.append((x.shape[0] // N_DEV,) + tuple(x.shape[1:]))
        self.out_shape = [jax.ShapeDtypeStruct((N_DEV,) + blk, x.dtype) for x, blk in zip(arrays, self.blks)]
        self.in_specs = [pl.BlockSpec(memory_space=pl.ANY)] * self.n
        self.out_specs = [pl.BlockSpec(memory_space=pl.ANY) for _ in range(self.n)]
        n_sem = self.n * (N_DEV - 1)
        self.scratch = [pltpu.SemaphoreType.DMA((n_sem,)), pltpu.SemaphoreType.DMA((n_sem,)),
                        pltpu.SemaphoreType.DMA((self.n,))]

    def _copies(self, x_refs, out_refs, send_sems, recv_sems, local_sems, with_recvs):
        mx, my, mc = lax.axis_index("x"), lax.axis_index("y"), lax.axis_index("c")
        me = 4 * mx + 2 * my + mc

        def flip(v, bit):
            return 1 - v if bit else v

        def part(a, dev):
            if self.modes[a] == 'gather':
                return x_refs[a]
            if self.modes[a] == 'slots':
                return x_refs[a].at[dev]
            r = self.blks[a][0]
            return x_refs[a].at[pl.ds(pl.multiple_of(dev * r, 16), r)]

        sends, recvs, owns = [], [], []
        for k in (1, 2, 4, 6, 3, 5, 7):
            px, py, pc = flip(mx, (k >> 2) & 1), flip(my, (k >> 1) & 1), flip(mc, k & 1)
            peer = 4 * px + 2 * py + pc
            for a in range(self.n):
                sem = a * (N_DEV - 1) + k - 1
                sends.append(pltpu.make_async_remote_copy(
                    src_ref=part(a, peer), dst_ref=out_refs[a].at[me], send_sem=send_sems.at[sem],
                    recv_sem=recv_sems.at[sem], device_id=(px, py, pc), device_id_type=pl.DeviceIdType.MESH))
                if with_recvs:
                    recvs.append(pltpu.make_async_remote_copy(
                        src_ref=part(a, peer), dst_ref=out_refs[a].at[peer], send_sem=send_sems.at[sem],
                        recv_sem=recv_sems.at[sem], device_id=(px, py, pc), device_id_type=pl.DeviceIdType.MESH))
        for a in range(self.n):
            owns.append(pltpu.make_async_copy(part(a, me), out_refs[a].at[me], local_sems.at[a]))
        return sends, recvs, owns

    def start(self, *refs):
        sends, _, owns = self._copies(*refs, with_recvs=False)
        for cp in sends + owns:
            cp.start()

    def wait(self, *refs):
        sends, recvs, owns = self._copies(*refs, with_recvs=True)
        for cp in recvs:
            cp.wait_recv()
        for cp in sends:
            cp.wait_send()
        for cp in owns:
            cp.wait()


def _exchange(arrays, modes, *, name):
    ex = _Exchange(arrays, modes)

    def body(*refs):
        x_refs, out_refs, sems = refs[:ex.n], refs[ex.n:2 * ex.n], refs[2 * ex.n:]
        ex.start(x_refs, out_refs, *sems)
        ex.wait(x_refs, out_refs, *sems)

    return pl.pallas_call(
        body, name=name, out_shape=tuple(ex.out_shape), in_specs=ex.in_specs, out_specs=tuple(ex.out_specs),
        scratch_shapes=ex.scratch,
    )(*arrays)


def _gather_two_level(arrays, *, name):
    n = len(arrays)
    per = N_DEV - 1

    def body(*refs):
        x_refs, out_refs = refs[:n], refs[n:2 * n]
        send_sems, recv_sems, local_sems = refs[2 * n:]
        x, y, c = lax.axis_index("x"), lax.axis_index("y"), lax.axis_index("c")
        me, sibling = (x, y, c), (x, y, 1 - c)
        chips = [(1 - x, y), (x, 1 - y), (1 - x, 1 - y)]

        def copy(a, k, block, to, src=None):
            slot = out_refs[a].at[4 * block[0] + 2 * block[1] + block[2]]
            return pltpu.make_async_remote_copy(
                src_ref=slot if src is None else src, dst_ref=slot, send_sem=send_sems.at[a * per + k],
                recv_sem=recv_sems.at[a * per + k], device_id=to, device_id_type=pl.DeviceIdType.MESH)

        mines = [pltpu.make_async_copy(x_refs[a], out_refs[a].at[4 * x + 2 * y + c], local_sems.at[a]) for a in range(n)]
        firsts = []
        for a in range(n):
            firsts.append(copy(a, 0, me, sibling, src=x_refs[a]))
            firsts += [copy(a, 1 + j, me, (*chip, c), src=x_refs[a]) for j, chip in enumerate(chips)]
        for cp in mines + firsts:
            cp.start()
        passed = []
        for j, chip in enumerate(chips):
            for a in range(n):
                copy(a, 1 + j, (*chip, c), me).wait_recv()
                passed.append(copy(a, 4 + j, (*chip, c), sibling))
                passed[-1].start()
        for a in range(n):
            copy(a, 0, sibling, me).wait_recv()
            for j, chip in enumerate(chips):
                copy(a, 4 + j, (*chip, 1 - c), me).wait_recv()
        for cp in firsts + passed:
            cp.wait_send()
        for cp in mines:
            cp.wait()

    return pl.pallas_call(
        body, name=name,
        out_shape=tuple(jax.ShapeDtypeStruct((N_DEV,) + tuple(a.shape), a.dtype) for a in arrays),
        in_specs=[pl.BlockSpec(memory_space=pl.ANY)] * n,
        out_specs=tuple(pl.BlockSpec(memory_space=pl.ANY) for _ in range(n)),
        scratch_shapes=[pltpu.SemaphoreType.DMA((n * per,)), pltpu.SemaphoreType.DMA((n * per,)),
                        pltpu.SemaphoreType.DMA((n,))],
    )(*arrays)


def _hosted(body, n_in, n_out, n_scratch, grid, ex):
    def wrapped(*refs):
        ins, x_refs = refs[:n_in], refs[n_in:n_in + ex.n]
        outs = refs[n_in + ex.n:n_in + ex.n + n_out]
        xo_refs = refs[n_in + ex.n + n_out:n_in + 2 * ex.n + n_out]
        scr = refs[n_in + 2 * ex.n + n_out:n_in + 2 * ex.n + n_out + n_scratch]
        sems = refs[n_in + 2 * ex.n + n_out + n_scratch:]
        ids = [pl.program_id(d) for d in range(len(grid))]
        first = functools.reduce(jnp.logical_and, [i == 0 for i in ids])
        last = functools.reduce(jnp.logical_and, [i == g - 1 for i, g in zip(ids, grid)])

        @pl.when(first)
        def _():
            ex.start(x_refs, xo_refs, *sems)

        body(*ins, *outs, *scr)

        @pl.when(last)
        def _():
            ex.wait(x_refs, xo_refs, *sems)

    return wrapped


def _call(body, *, name, grid, inputs, in_specs, out_shape, out_specs, scratch, semantics, ex=None):
    if ex is None:
        return pl.pallas_call(
            body, name=name, grid=grid, out_shape=tuple(out_shape), in_specs=list(in_specs),
            out_specs=tuple(out_specs), scratch_shapes=list(scratch), compiler_params=_cparams(semantics))(*inputs)
    n_out = len(out_shape)
    res = pl.pallas_call(
        _hosted(body, len(inputs), n_out, len(scratch), grid, ex), name=name, grid=grid,
        out_shape=tuple(out_shape) + tuple(ex.out_shape), in_specs=list(in_specs) + ex.in_specs,
        out_specs=tuple(out_specs) + tuple(ex.out_specs), scratch_shapes=list(scratch) + ex.scratch,
        compiler_params=_cparams(("arbitrary",) * len(grid)))(*inputs, *ex.arrays)
    return res[:n_out], res[n_out:]


def _mm(a, b, *, name, out_dtype=F32, tm=MM_TM, tn=512, tk=None, ex=None):
    m, k = a.shape
    n = b.shape[1]
    tm, tn = min(tm, m), min(tn, n)
    tk = k if tk is None else tk
    nk = k // tk
    assert m % tm == 0 and n % tn == 0 and k % tk == 0

    def body(a_ref, b_ref, o_ref, acc_ref):
        kk = pl.program_id(2)
        part = _dot(a_ref[...], b_ref[...])
        if nk == 1:
            o_ref[...] = part.astype(out_dtype)
        else:
            @pl.when(kk == 0)
            def _():
                acc_ref[...] = part

            @pl.when(kk > 0)
            def _():
                acc_ref[...] += part

            @pl.when(kk == nk - 1)
            def _():
                o_ref[...] = acc_ref[...].astype(out_dtype)

    res = _call(
        body, name=name, grid=(m // tm, n // tn, nk), inputs=(a, b),
        in_specs=[pl.BlockSpec((tm, tk), lambda i, j, kk: (i, kk)), pl.BlockSpec((tk, tn), lambda i, j, kk: (kk, j))],
        out_shape=[jax.ShapeDtypeStruct((m, n), out_dtype)],
        out_specs=[pl.BlockSpec((tm, tn), lambda i, j, kk: (i, j))],
        scratch=[pltpu.VMEM((tm, tn) if nk > 1 else (8, LANES), F32)],
        semantics=("parallel", "parallel", "arbitrary"), ex=ex)
    return res[0] if ex is None else (res[0][0], res[1])


def _mm_resid(a, b, h_in, g_post, g_next, *, name, next_dtype, tm=512):
    m, k = a.shape
    n = b.shape[1]
    tm = min(tm, m)

    def body(a_ref, b_ref, h_ref, gp_ref, gn_ref, o_ref, ho_ref, yn_ref):
        o = _dot(a_ref[...], b_ref[...])
        o_ref[...] = o
        h = h_ref[...] + o * _rms_scale(o) * gp_ref[...]
        ho_ref[...] = h
        yn_ref[...] = (h * _rms_scale(h) * gn_ref[...]).astype(next_dtype)

    row = pl.BlockSpec((tm, n), lambda i: (i, 0))
    return pl.pallas_call(
        body, name=name, grid=(m // tm,),
        out_shape=(jax.ShapeDtypeStruct((m, n), F32), jax.ShapeDtypeStruct((m, n), F32),
                   jax.ShapeDtypeStruct((m, n), next_dtype)),
        in_specs=[pl.BlockSpec((tm, k), lambda i: (i, 0)), pl.BlockSpec((k, n), lambda i: (0, 0)), row,
                  _vec_spec(n), _vec_spec(n)],
        out_specs=(row, row, row),
        compiler_params=_cparams(("parallel",)),
    )(a, b, h_in, g_post, g_next)


def _mm_tn(a, b, *, name, out_dtype=F32, tm=1024, tn=512, tk=1024, shift=0):
    t, m = a.shape
    n = b.shape[1]
    tm, tn, tk = min(tm, m), min(tn, n), min(tk, t)
    nk = t // tk
    nb = m // tm
    assert m % tm == 0 and n % tn == 0 and t % tk == 0

    def body(a_ref, b_ref, o_ref, acc_ref):
        kk = pl.program_id(2)
        part = _dot_tn(a_ref[...], b_ref[...])

        @pl.when(kk == 0)
        def _():
            acc_ref[...] = part

        @pl.when(kk > 0)
        def _():
            acc_ref[...] += part

        @pl.when(kk == nk - 1)
        def _():
            o_ref[...] = acc_ref[...].astype(out_dtype)

    return pl.pallas_call(
        body, name=name, grid=(nb, n // tn, nk),
        out_shape=jax.ShapeDtypeStruct((m, n), out_dtype),
        in_specs=[pl.BlockSpec((tk, tm), lambda i, j, kk: (kk, i)), pl.BlockSpec((tk, tn), lambda i, j, kk: (kk, j))],
        out_specs=pl.BlockSpec((tm, tn), lambda i, j, kk: ((i + shift) % nb, j)),
        scratch_shapes=[pltpu.VMEM((tm, tn), F32)],
        compiler_params=_cparams(("parallel", "parallel", "arbitrary")),
    )(a, b)


def _mm_tn_gate_up(dgu, yn, *, name, out_dtype, tk=2048):
    t, m = dgu.shape
    n = yn.shape[1]
    tk = min(tk, t)
    nk = t // tk
    nb = m // (2 * FF_TILE)

    def body(a_ref, b_ref, og_ref, ou_ref, acc_ref):
        kk = pl.program_id(1)
        part = _dot_tn(a_ref[...], b_ref[...])

        @pl.when(kk == 0)
        def _():
            acc_ref[...] = part

        @pl.when(kk > 0)
        def _():
            acc_ref[...] += part

        @pl.when(kk == nk - 1)
        def _():
            og_ref[...] = acc_ref[:FF_TILE, :].astype(out_dtype)
            ou_ref[...] = acc_ref[FF_TILE:, :].astype(out_dtype)

    out = jax.ShapeDtypeStruct((m // 2, n), out_dtype)
    o_spec = pl.BlockSpec((FF_TILE, n), lambda i, kk: (i, 0))
    return pl.pallas_call(
        body, name=name, grid=(nb, nk), out_shape=(out, out),
        in_specs=[pl.BlockSpec((tk, 2 * FF_TILE), lambda i, kk: (kk, i)), pl.BlockSpec((tk, n), lambda i, kk: (kk, 0))],
        out_specs=(o_spec, o_spec),
        scratch_shapes=[pltpu.VMEM((2 * FF_TILE, n), F32)],
        compiler_params=_cparams(("parallel", "arbitrary")),
    )(dgu, yn)


def _mm_swiglu(a, w_gu, *, name, tm=MM_TM, ex=None):
    m, k = a.shape
    n = w_gu.shape[1]
    nt = n // (2 * FF_TILE)
    tm = min(tm, m)

    def body(a_ref, b_ref, gu_ref, act_ref):
        gu = _dot(a_ref[...], b_ref[...])
        gu_ref[...] = gu.astype(MXU_DTYPE)
        act_ref[...] = (_silu(gu[:, :FF_TILE]) * gu[:, FF_TILE:]).astype(MXU_DTYPE)

    return _call(
        body, name=name, grid=(m // tm, nt), inputs=(a, w_gu),
        in_specs=[pl.BlockSpec((tm, k), lambda i, j: (i, 0)), pl.BlockSpec((k, 2 * FF_TILE), lambda i, j: (0, j))],
        out_shape=[jax.ShapeDtypeStruct((m, n), MXU_DTYPE), jax.ShapeDtypeStruct((m, n // 2), MXU_DTYPE)],
        out_specs=[pl.BlockSpec((tm, 2 * FF_TILE), lambda i, j: (i, j)), pl.BlockSpec((tm, FF_TILE), lambda i, j: (i, j))],
        scratch=[], semantics=("parallel", "parallel"), ex=ex)


def _mm_dswiglu(dd, w_down_t, gu, *, name, tm=MM_TM):
    m, k = dd.shape
    n = gu.shape[1]
    nt = n // (2 * FF_TILE)
    tm = min(tm, m)

    def body(d_ref, w_ref, gu_ref, o_ref):
        dact = _dot(d_ref[...], w_ref[...])
        gate, up = gu_ref[:, :FF_TILE].astype(F32), gu_ref[:, FF_TILE:].astype(F32)
        act_gate, act_grad = _silu_and_grad(gate)
        o_ref[:, :FF_TILE] = (dact * up * act_grad).astype(MXU_DTYPE)
        o_ref[:, FF_TILE:] = (dact * act_gate).astype(MXU_DTYPE)

    return pl.pallas_call(
        body, name=name, grid=(m // tm, nt),
        out_shape=jax.ShapeDtypeStruct((m, n), MXU_DTYPE),
        in_specs=[pl.BlockSpec((tm, k), lambda i, j: (i, 0)), pl.BlockSpec((k, FF_TILE), lambda i, j: (0, j)),
                  pl.BlockSpec((tm, 2 * FF_TILE), lambda i, j: (i, j))],
        out_specs=pl.BlockSpec((tm, 2 * FF_TILE), lambda i, j: (i, j)),
        compiler_params=_cparams(("parallel", "parallel")),
    )(dd, w_down_t, gu)


def _row_spec(width, tr=ROW_TILE):
    return pl.BlockSpec((tr, width), lambda i: (i, 0))


def _vec_spec(width, rows=1):
    return pl.BlockSpec((rows, width), lambda i: (0, 0))


def _rn_fwd(h, g, *, name, out_dtype):
    rows, d = h.shape

    def body(h_ref, g_ref, o_ref):
        x = h_ref[...]
        o_ref[...] = (x * _rms_scale(x) * g_ref[...]).astype(out_dtype)

    return pl.pallas_call(
        body, name=name, grid=(rows // ROW_TILE,),
        out_shape=jax.ShapeDtypeStruct((rows, d), out_dtype),
        in_specs=[_row_spec(d), _vec_spec(d)], out_specs=_row_spec(d),
        compiler_params=_cparams(("parallel",)),
    )(h, g)


def _resid_rn_fwd(h_in, o, g_post, g_next, *, name, next_dtype):
    rows, d = h_in.shape

    def body(h_ref, o_ref, gp_ref, gn_ref, ho_ref, yn_ref):
        ov = o_ref[...]
        h = h_ref[...] + ov * _rms_scale(ov) * gp_ref[...]
        ho_ref[...] = h
        yn_ref[...] = (h * _rms_scale(h) * gn_ref[...]).astype(next_dtype)

    return pl.pallas_call(
        body, name=name, grid=(rows // ROW_TILE,),
        out_shape=(jax.ShapeDtypeStruct((rows, d), F32), jax.ShapeDtypeStruct((rows, d), next_dtype)),
        in_specs=[_row_spec(d), _row_spec(d), _vec_spec(d), _vec_spec(d)],
        out_specs=(_row_spec(d), _row_spec(d)),
        compiler_params=_cparams(("parallel",)),
    )(h_in, o, g_post, g_next)


def _resid_loss(h_in, o, g_post, target, *, name):
    rows, d = h_in.shape

    def body(h_ref, o_ref, gp_ref, t_ref, dh_ref, loss_ref, do_ref, dg_ref):
        ov = o_ref[...]
        gp = gp_ref[...]
        err = h_ref[...] + ov * _rms_scale(ov) * gp - t_ref[...]
        dh = err * (1.0 / d)
        dh_ref[...] = dh
        do, dg = _rms_bwd(dh, ov, gp)
        do_ref[...] = do.astype(MXU_DTYPE)

        @pl.when(pl.program_id(0) == 0)
        def _():
            loss_ref[...] = jnp.zeros_like(loss_ref)
            dg_ref[...] = jnp.zeros_like(dg_ref)

        loss_ref[...] += 0.5 * jnp.sum(jnp.mean(err * err, axis=-1, keepdims=True), axis=0, keepdims=True)
        dg_ref[...] += _colsum(dg)

    return pl.pallas_call(
        body, name=name, grid=(rows // ROW_TILE,),
        out_shape=(jax.ShapeDtypeStruct((rows, d), F32), jax.ShapeDtypeStruct((1, 1), F32),
                   jax.ShapeDtypeStruct((rows, d), MXU_DTYPE), jax.ShapeDtypeStruct((1, d), F32)),
        in_specs=[_row_spec(d), _row_spec(d), _vec_spec(d), _row_spec(d)],
        out_specs=(_row_spec(d), pl.BlockSpec((1, 1), lambda i: (0, 0)), _row_spec(d), _vec_spec(d)),
        compiler_params=_cparams(("arbitrary",)),
    )(h_in, o, g_post, target)


def _resid_bwd_pre_post(dh, dyn_list, h_in, g_pre, o_prev, g_post_prev, *, name, out_dtype):
    rows, d = dh.shape
    n_dyn = len(dyn_list)

    def body(*refs):
        dh_ref, dyn_refs = refs[0], refs[1:1 + n_dyn]
        h_ref, g_ref, o_ref, gp_ref, out_ref, do_ref, dg_ref, dgp_ref = refs[1 + n_dyn:]
        dyn = dyn_refs[0][...].astype(F32)
        for r in dyn_refs[1:]:
            dyn = dyn + r[...].astype(F32)
        dx, dg = _rms_bwd(dyn, h_ref[...], g_ref[...])
        dh_in = dh_ref[...] + dx
        out_ref[...] = dh_in
        do, dgp = _rms_bwd(dh_in, o_ref[...], gp_ref[...])
        do_ref[...] = do.astype(out_dtype)

        @pl.when(pl.program_id(0) == 0)
        def _():
            dg_ref[...] = jnp.zeros_like(dg_ref)
            dgp_ref[...] = jnp.zeros_like(dgp_ref)

        dg_ref[...] += _colsum(dg)
        dgp_ref[...] += _colsum(dgp)

    return pl.pallas_call(
        body, name=name, grid=(rows // ROW_TILE,),
        out_shape=(jax.ShapeDtypeStruct((rows, d), F32), jax.ShapeDtypeStruct((rows, d), out_dtype),
                   jax.ShapeDtypeStruct((1, d), F32), jax.ShapeDtypeStruct((1, d), F32)),
        in_specs=[_row_spec(d)] + [_row_spec(d)] * n_dyn + [_row_spec(d), _vec_spec(d), _row_spec(d), _vec_spec(d)],
        out_specs=(_row_spec(d), _row_spec(d), _vec_spec(d), _vec_spec(d)),
        compiler_params=_cparams(("arbitrary",)),
    )(dh, *dyn_list, h_in, g_pre, o_prev, g_post_prev)


def _resid_bwd_pre(dh, dyn_list, h_in, g_pre, *, name):
    rows, d = dh.shape
    n_dyn = len(dyn_list)

    def body(*refs):
        dh_ref, dyn_refs, h_ref, g_ref, out_ref, dg_ref = refs[0], refs[1:1 + n_dyn], *refs[1 + n_dyn:]
        dyn = dyn_refs[0][...].astype(F32)
        for r in dyn_refs[1:]:
            dyn = dyn + r[...].astype(F32)
        dx, dg = _rms_bwd(dyn, h_ref[...], g_ref[...])
        out_ref[...] = dh_ref[...] + dx

        @pl.when(pl.program_id(0) == 0)
        def _():
            dg_ref[...] = jnp.zeros_like(dg_ref)

        dg_ref[...] += _colsum(dg)

    return pl.pallas_call(
        body, name=name, grid=(rows // ROW_TILE,),
        out_shape=(jax.ShapeDtypeStruct((rows, d), F32), jax.ShapeDtypeStruct((1, d), F32)),
        in_specs=[_row_spec(d)] + [_row_spec(d)] * n_dyn + [_row_spec(d), _vec_spec(d)],
        out_specs=(_row_spec(d), _vec_spec(d)),
        compiler_params=_cparams(("arbitrary",)),
    )(dh, *dyn_list, h_in, g_pre)


def _layer_norm_stats(x):
    mu = jnp.mean(x, axis=-1, keepdims=True)
    xc = x - mu
    rstd = lax.rsqrt(jnp.mean(xc * xc, axis=-1, keepdims=True) + EPS)
    return xc * rstd, rstd


def _gmlp_fwd(proj, ln_g, ln_b, wm, bcol, *, name):
    rows = proj.shape[0]
    tr = ROW_TILE

    def body(u_ref, v_ref, lg_ref, lb_ref, wm_ref, bc_ref, ya_ref):
        vhat, _ = _layer_norm_stats(_gelu(v_ref[...]))
        vl = (vhat * lg_ref[...] + lb_ref[...]).astype(MXU_DTYPE)
        gu = _gelu(u_ref[...])
        bc = bc_ref[...]
        for c in range(tr // CHUNK):
            rs = slice(c * CHUNK, (c + 1) * CHUNK)
            for h in range(GM_HEADS):
                cs = slice(h * GM_HEAD_DIM, (h + 1) * GM_HEAD_DIM)
                mixed = _dot(wm_ref[h], vl[rs, cs]) + bc[:, h:h + 1]
                ya_ref[rs, cs] = (gu[rs, cs] * mixed).astype(MXU_DTYPE)

    return pl.pallas_call(
        body, name=name, grid=(rows // tr,),
        out_shape=jax.ShapeDtypeStruct((rows, 2 * D_MODEL), MXU_DTYPE),
        in_specs=[pl.BlockSpec((tr, D_MODEL), lambda i: (i, 2)), pl.BlockSpec((tr, D_MODEL), lambda i: (i, 3)),
                  _vec_spec(D_MODEL), _vec_spec(D_MODEL),
                  pl.BlockSpec((GM_HEADS, CHUNK, CHUNK), lambda i: (0, 0, 0)), _vec_spec(LANES, CHUNK)],
        out_specs=_row_spec(D_MODEL, tr),
        compiler_params=_cparams(("parallel",)),
    )(proj, proj, ln_g, ln_b, wm, bcol)


def _gmlp_bwd(proj, dcat, ln_g, ln_b, wm, wm_t, bcol, *, name):
    rows = proj.shape[0]
    tr = ROW_TILE

    def body(u_ref, v_ref, dy_ref, lg_ref, lb_ref, wm_ref, wmt_ref, bc_ref,
             duv_ref, dwm_ref, dbc_ref, dlg_ref, dlb_ref, dvl_scr):
        @pl.when(pl.program_id(0) == 0)
        def _():
            dwm_ref[...] = jnp.zeros_like(dwm_ref)
            dbc_ref[...] = jnp.zeros_like(dbc_ref)
            dlg_ref[...] = jnp.zeros_like(dlg_ref)
            dlb_ref[...] = jnp.zeros_like(dlb_ref)

        gv, gv_grad = _gelu_and_grad(v_ref[...])
        vhat, rstd = _layer_norm_stats(gv)
        lg = lg_ref[...]
        vl = (vhat * lg + lb_ref[...]).astype(MXU_DTYPE)
        gu, gu_grad = _gelu_and_grad(u_ref[...])
        dy = dy_ref[...].astype(F32)
        bc = bc_ref[...]
        row = lax.broadcasted_iota(jnp.int32, (CHUNK, CHUNK), 0)
        lane = lax.broadcasted_iota(jnp.int32, (CHUNK, CHUNK), 1)
        causal = lane <= row
        dbc = jnp.zeros((CHUNK, LANES), F32)
        for c in range(tr // CHUNK):
            rs = slice(c * CHUNK, (c + 1) * CHUNK)
            for h in range(GM_HEADS):
                cs = slice(h * GM_HEAD_DIM, (h + 1) * GM_HEAD_DIM)
                vl_h = vl[rs, cs]
                mixed = _dot(wm_ref[h], vl_h) + bc[:, h:h + 1]
                dy_h = dy[rs, cs]
                duv_ref[rs, cs] = (dy_h * mixed * gu_grad[rs, cs]).astype(MXU_DTYPE)
                dmixed = dy_h * gu[rs, cs]
                dwm_ref[h] += jnp.where(causal, _dot_nt(dmixed, vl_h), 0.0)
                dbc = dbc + jnp.where(lane == h, jnp.sum(dmixed, axis=1, keepdims=True), 0.0)
                dvl_scr[rs, cs] = _dot(wmt_ref[h], dmixed)
        dbc_ref[...] += dbc
        dvl = dvl_scr[...]
        dlg_ref[...] += _colsum(dvl * vhat)
        dlb_ref[...] += _colsum(dvl)
        dvh = dvl * lg
        dgv = rstd * (dvh - jnp.mean(dvh, axis=-1, keepdims=True) - vhat * jnp.mean(dvh * vhat, axis=-1, keepdims=True))
        duv_ref[:, D_MODEL:] = (dgv * gv_grad).astype(MXU_DTYPE)

    return pl.pallas_call(
        body, name=name, grid=(rows // tr,),
        out_shape=(jax.ShapeDtypeStruct((rows, IN_MAIN), MXU_DTYPE),
                   jax.ShapeDtypeStruct((GM_HEADS, CHUNK, CHUNK), F32), jax.ShapeDtypeStruct((CHUNK, LANES), F32),
                   jax.ShapeDtypeStruct((1, D_MODEL), F32), jax.ShapeDtypeStruct((1, D_MODEL), F32)),
        in_specs=[pl.BlockSpec((tr, D_MODEL), lambda i: (i, 2)), pl.BlockSpec((tr, D_MODEL), lambda i: (i, 3)),
                  pl.BlockSpec((tr, D_MODEL), lambda i: (i, 0)), _vec_spec(D_MODEL), _vec_spec(D_MODEL),
                  pl.BlockSpec((GM_HEADS, CHUNK, CHUNK), lambda i: (0, 0, 0)),
                  pl.BlockSpec((GM_HEADS, CHUNK, CHUNK), lambda i: (0, 0, 0)), _vec_spec(LANES, CHUNK)],
        out_specs=(pl.BlockSpec((tr, 2 * D_MODEL), lambda i: (i, 1)),
                   pl.BlockSpec((GM_HEADS, CHUNK, CHUNK), lambda i: (0, 0, 0)), _vec_spec(LANES, CHUNK),
                   _vec_spec(D_MODEL), _vec_spec(D_MODEL)),
        scratch_shapes=[pltpu.VMEM((tr, D_MODEL), F32)],
        compiler_params=_cparams(("arbitrary",)),
    )(proj, proj, dcat, ln_g, ln_b, wm, wm_t, bcol)


def _conv_fwd(proj, conv_w8, conv_b, *, name):
    rows = proj.shape[0]
    tr = ROW_TILE
    hb = tr // CONV_HALO

    def body(x_ref, prev_ref, w_ref, b_ref, pre_ref, buf):
        first = pl.program_id(0) == 0
        buf[pl.ds(0, CONV_HALO), :] = jnp.where(first, 0.0, prev_ref[...])
        buf[pl.ds(CONV_HALO, tr), :] = x_ref[...]
        ext = buf[...]
        acc = jnp.broadcast_to(b_ref[...], (tr, CONV_DIM))
        for k in range(SSM_CONV):
            s = SSM_CONV - 1 - k
            acc = acc + w_ref[k:k + 1, :] * (x_ref[...] if s == 0 else pltpu.roll(ext, s, axis=0)[CONV_HALO:])
        pre_ref[...] = acc

    return pl.pallas_call(
        body, name=name, grid=(rows // tr,),
        out_shape=jax.ShapeDtypeStruct((rows, CONV_DIM), F32),
        in_specs=[pl.BlockSpec((tr, CONV_DIM), lambda i: (i, 0)),
                  pl.BlockSpec((CONV_HALO, CONV_DIM), lambda i: (jnp.maximum(i * hb - 1, 0), 0)),
                  _vec_spec(CONV_DIM, 8), _vec_spec(CONV_DIM)],
        out_specs=_row_spec(CONV_DIM, tr),
        scratch_shapes=[pltpu.VMEM((tr + CONV_HALO, CONV_DIM), F32)],
        compiler_params=_cparams(("parallel",)),
    )(proj, proj, conv_w8, conv_b)


def _conv_bwd(dpre, proj, conv_w8, dproj, *, name):
    rows = proj.shape[0]
    tr = ROW_TILE
    halo = 16
    hb = tr // halo
    nblk = rows // tr

    def body(d_ref, dnext_ref, x_ref, w_ref, dproj_ref, dx_ref, dw_ref, db_ref, dbuf):
        i = pl.program_id(0)

        @pl.when(i == 0)
        def _():
            dw_ref[...] = jnp.zeros_like(dw_ref)
            db_ref[...] = jnp.zeros_like(db_ref)

        d = d_ref[...].astype(F32)
        x = x_ref[...]
        dbuf[pl.ds(0, tr), :] = d
        dbuf[pl.ds(tr, halo), :] = jnp.where(i == nblk - 1, 0.0, dnext_ref[...].astype(F32))
        ext = dbuf[...]
        acc = jnp.zeros((tr, CONV_DIM), F32)
        for k in range(SSM_CONV):
            s = SSM_CONV - 1 - k
            shifted = d if s == 0 else pltpu.roll(ext, tr + halo - s, axis=0)[:tr]
            acc = acc + w_ref[k:k + 1, :] * shifted
            dw_ref[k:k + 1, :] += _colsum(shifted * x)
        dx_ref[...] = acc.astype(MXU_DTYPE)
        db_ref[...] += _colsum(d)

    return pl.pallas_call(
        body, name=name, grid=(nblk,),
        out_shape=(jax.ShapeDtypeStruct((rows, IN_MAIN), MXU_DTYPE), jax.ShapeDtypeStruct((8, CONV_DIM), F32),
                   jax.ShapeDtypeStruct((1, CONV_DIM), F32)),
        in_specs=[_row_spec(CONV_DIM, tr),
                  pl.BlockSpec((halo, CONV_DIM), lambda i: (jnp.minimum((i + 1) * hb, rows // halo - 1), 0)),
                  pl.BlockSpec((tr, CONV_DIM), lambda i: (i, 0)),
                  _vec_spec(CONV_DIM, 8), pl.BlockSpec(memory_space=pl.ANY)],
        out_specs=(_row_spec(CONV_DIM, tr), _vec_spec(CONV_DIM, 8), _vec_spec(CONV_DIM)),
        scratch_shapes=[pltpu.VMEM((tr + halo, CONV_DIM), F32)],
        input_output_aliases={4: 0},
        compiler_params=_cparams(("arbitrary",)),
    )(dpre, dpre, proj, conv_w8, dproj)


def _chunk_iotas():
    row = lax.broadcasted_iota(jnp.int32, (CHUNK, CHUNK), 0)
    lane = lax.broadcasted_iota(jnp.int32, (CHUNK, CHUNK), 1)
    return row, lane, lane <= row


SSD_BWD_CHUNKS = 4
GROUP_DIM = D_MODEL // SSM_GROUPS
HEADS_PER_GROUP = SSM_HEADS // SSM_GROUPS
HEAD_DIM = GROUP_DIM // HEADS_PER_GROUP


def _split(x):
    hi = x.astype(MXU_DTYPE)
    return hi, (x - hi.astype(F32)).astype(MXU_DTYPE)


def _dot_split(x, sel):
    hi, lo = _split(x)
    return jnp.dot(hi, sel, preferred_element_type=F32) + jnp.dot(lo, sel, preferred_element_type=F32)


def _dot_split_rhs(sel, x):
    hi, lo = _split(x)
    return jnp.dot(sel, hi, preferred_element_type=F32) + jnp.dot(sel, lo, preferred_element_type=F32)


def _head_selectors():
    h = lax.broadcasted_iota(jnp.int32, (LANES, D_MODEL), 0)
    p = lax.broadcasted_iota(jnp.int32, (LANES, D_MODEL), 1)
    sel_t = (h == p // HEAD_DIM).astype(MXU_DTYPE)
    return sel_t, sel_t.T


def _expand_heads(per_head, e_end, selt_ref, sel_ref):
    stacked = jnp.concatenate(per_head, axis=0)
    wide = _dot_split(stacked, selt_ref[...])
    n = per_head[0].shape[0]
    e_cols = jnp.broadcast_to(e_end, (LANES, LANES)).T
    tall = _dot_split_rhs(sel_ref[...], e_cols)
    return [wide[n * i:n * (i + 1)] for i in range(len(per_head))], tall


def _by_quarter(index, pieces):
    out = pieces[3]
    for q in (2, 1, 0):
        out = jnp.where(index == q, pieces[q], out)
    return out


def _ssd_fwd_grouped(pre, dtr, proj, dtb, alog, dsk, gn, cat, *, name):
    rows = pre.shape[0]
    nc = rows // CHUNK

    def body(pre_ref, dtr_ref, z_ref, dtb_ref, alog_ref, dsk_ref, gn_ref, cat_ref, selt_ref, sel_ref, yb_ref, y_ref,
             st_ref, dt_ref, acum_ref, s_scr):
        @pl.when(pl.program_id(0) == 0)
        def _():
            s_scr[...] = jnp.zeros_like(s_scr)

        row, lane, tril = _chunk_iotas()
        dt = _softplus(dtr_ref[...] + dtb_ref[...])
        acum = _dot_exact(tril.astype(F32), dt * (-jnp.exp(alog_ref[...])))
        dt_ref[...] = dt
        acum_ref[...] = acum
        acum_t = acum.T
        a_end = acum[CHUNK - 1:CHUNK, :]
        (dt_x, eacum_x, dte_x, dsk_x), e_rows_all = _expand_heads(
            [dt, jnp.exp(acum), jnp.exp(a_end - acum), jnp.broadcast_to(dsk_ref[...], (CHUNK, LANES))], jnp.exp(a_end),
            selt_ref, sel_ref)
        lane_q = lax.broadcasted_iota(jnp.int32, (CHUNK, GROUP_DIM), 1) // HEAD_DIM

        for g in range(SSM_GROUPS):
            cs = slice(GROUP_DIM * g, GROUP_DIM * (g + 1))
            b_g = _silu(pre_ref[:, 1024 + SSM_STATE * g:1024 + SSM_STATE * (g + 1)]).astype(MXU_DTYPE)
            c_g = _silu(pre_ref[:, 1536 + SSM_STATE * g:1536 + SSM_STATE * (g + 1)]).astype(MXU_DTYPE)
            cb = _dot_nt(c_g, b_g)
            xs = _silu(pre_ref[:, cs])
            xdt = xs * dt_x[:, cs]
            m_stack = jnp.concatenate(
                [(cb * jnp.exp(jnp.where(tril, acum[:, h:h + 1] - acum_t[h:h + 1, :], -jnp.inf))).astype(MXU_DTYPE)
                 for h in range(4 * g, 4 * g + 4)], axis=0)
            y_all = _dot(m_stack, xdt)
            y = _by_quarter(lane_q, [y_all[CHUNK * q:CHUNK * (q + 1)] for q in range(HEADS_PER_GROUP)])
            s_prev = s_scr[g]
            st_ref[0, g] = s_prev
            y = y + _dot_nt(c_g, s_prev) * eacum_x[:, cs] + dsk_x[:, cs] * xs
            xw = xdt * dte_x[:, cs]
            s_scr[g] = e_rows_all[cs, :] * s_prev + _dot(xw.T, b_g)
            y_ref[:, cs] = y
            gated = y * _silu(z_ref[:, cs])
            r = lax.rsqrt(jnp.mean(gated * gated, axis=1, keepdims=True) + EPS)
            yb_ref[:, cs] = (gated * r * gn_ref[:, cs]).astype(MXU_DTYPE)

    return pl.pallas_call(
        body, name=name, grid=(nc,),
        out_shape=(jax.ShapeDtypeStruct((rows, 2 * D_MODEL), MXU_DTYPE), jax.ShapeDtypeStruct((rows, D_MODEL), F32),
                   jax.ShapeDtypeStruct((nc, SSM_GROUPS, GROUP_DIM, SSM_STATE), F32),
                   jax.ShapeDtypeStruct((rows, LANES), F32), jax.ShapeDtypeStruct((rows, LANES), F32)),
        in_specs=[_row_spec(CONV_DIM, CHUNK), _row_spec(LANES, CHUNK), pl.BlockSpec((CHUNK, D_MODEL), lambda i: (i, 4)),
                  _vec_spec(LANES), _vec_spec(LANES), _vec_spec(LANES), _vec_spec(D_MODEL),
                  pl.BlockSpec(memory_space=pl.ANY), _vec_spec(D_MODEL, LANES), _vec_spec(LANES, D_MODEL)],
        out_specs=(pl.BlockSpec((CHUNK, D_MODEL), lambda i: (i, 1)), _row_spec(D_MODEL, CHUNK),
                   pl.BlockSpec((1, SSM_GROUPS, GROUP_DIM, SSM_STATE), lambda i: (i, 0, 0, 0)),
                   _row_spec(LANES, CHUNK), _row_spec(LANES, CHUNK)),
        scratch_shapes=[pltpu.VMEM((SSM_GROUPS, GROUP_DIM, SSM_STATE), F32)],
        input_output_aliases={7: 0},
        compiler_params=_cparams(("arbitrary",)),
    )(pre, dtr, proj, dtb, alog, dsk, gn, cat, *_head_selectors())


def _ssd_bwd_grouped(pre, dtr, dt_saved, acum_saved, proj, y_saved, states, dcat, dtb, alog, dsk, gn, dproj, *, name):
    rows = pre.shape[0]
    cps = SSD_BWD_CHUNKS
    tr = CHUNK * cps
    nsteps = rows // tr

    def rev(i):
        return nsteps - 1 - i

    def body(pre_ref, dtr_ref, dt_ref, acum_ref, z_ref, y_ref, st_ref, dyb_ref, dtb_ref, alog_ref, dsk_ref, gn_ref,
             dproj_ref, selt_ref, sel_ref, dpre_ref, dz_ref, ddtr_ref, dgn_ref, dvec_ref, g_scr):
        @pl.when(pl.program_id(0) == 0)
        def _():
            g_scr[...] = jnp.zeros_like(g_scr)
            dgn_ref[...] = jnp.zeros_like(dgn_ref)
            dvec_ref[...] = jnp.zeros_like(dvec_ref)

        for cc in reversed(range(cps)):
            at = lambda ref: ref.at[pl.ds(cc * CHUNK, CHUNK)]
            chunk(at(pre_ref), at(dtr_ref), at(dt_ref), at(acum_ref), at(z_ref), at(y_ref), st_ref.at[cc], at(dyb_ref),
                  dtb_ref, alog_ref, dsk_ref, gn_ref, selt_ref, sel_ref, at(dpre_ref), at(dz_ref), at(ddtr_ref), dgn_ref,
                  dvec_ref, g_scr)

    def chunk(pre_ref, dtr_ref, dt_ref, acum_ref, z_ref, y_ref, st_ref, dyb_ref, dtb_ref, alog_ref, dsk_ref, gn_ref,
              selt_ref, sel_ref, dpre_ref, dz_ref, ddtr_ref, dgn_ref, dvec_ref, g_scr):
        row, lane, tril = _chunk_iotas()
        triu = lane >= row
        dt, acum = dt_ref[...], acum_ref[...]
        a = -jnp.exp(alog_ref[...])
        acum_t = acum.T
        a_end = acum[CHUNK - 1:CHUNK, :]
        e_end = jnp.exp(a_end)
        (dt_x, eacum_x, dte_x, dsk_x), e_rows_all = _expand_heads(
            [dt, jnp.exp(acum), jnp.exp(a_end - acum), jnp.broadcast_to(dsk_ref[...], (CHUNK, LANES))], e_end,
            selt_ref, sel_ref)
        lane_q = lax.broadcasted_iota(jnp.int32, (CHUNK, GROUP_DIM), 1) // HEAD_DIM
        zero = jnp.zeros((CHUNK, LANES), F32)
        dacum_c, dacum_r, ddt_c = zero, zero, zero
        d_aend = jnp.zeros((1, LANES), F32)
        d_dsk = jnp.zeros((1, LANES), F32)
        iota = lambda shape, dim: lax.broadcasted_iota(jnp.int32, shape, dim)
        q256, lane_256 = iota((GROUP_DIM, LANES), 0) // HEAD_DIM, iota((GROUP_DIM, LANES), 1)
        q512, lane_512 = iota((4 * CHUNK, LANES), 0) // CHUNK, iota((4 * CHUNK, LANES), 1)
        row_512t, q512t = iota((LANES, 4 * CHUNK), 0), iota((LANES, 4 * CHUNK), 1) // CHUNK

        for g in range(SSM_GROUPS):
            cs = slice(GROUP_DIM * g, GROUP_DIM * (g + 1))
            yv = y_ref[:, cs]
            sz, sz_grad = _silu_and_grad(z_ref[:, cs])
            gated = yv * sz
            dyb = dyb_ref[:, cs].astype(F32)
            dgh = dyb * gn_ref[:, cs]
            r = lax.rsqrt(jnp.mean(gated * gated, axis=1, keepdims=True) + EPS)
            dgn_ref[:, cs] += _colsum(dyb * gated * r)
            dgated = r * dgh - gated * (r * r * r * jnp.mean(dgh * gated, axis=1, keepdims=True))
            dy = dgated * sz
            dz_ref[:, cs] = (dgated * yv * sz_grad).astype(MXU_DTYPE)

            b_f, b_grad = _silu_and_grad(pre_ref[:, 1024 + SSM_STATE * g:1024 + SSM_STATE * (g + 1)])
            c_f, c_grad = _silu_and_grad(pre_ref[:, 1536 + SSM_STATE * g:1536 + SSM_STATE * (g + 1)])
            b_g, c_g = b_f.astype(MXU_DTYPE), c_f.astype(MXU_DTYPE)
            xs, xs_grad = _silu_and_grad(pre_ref[:, cs])
            dtq = dt_x[:, cs]
            xdt = xs * dtq
            xdt_m = xdt.astype(MXU_DTYPE)
            dy_m = dy.astype(MXU_DTYPE)
            s_prev = st_ref[g]
            g_next = g_scr[g]
            eacq, dteq = eacum_x[:, cs], dte_x[:, cs]
            t_off = dy * (_dot_nt(c_g, s_prev) * eacq)
            dye = dy * eacq
            dc_g = _dot(dye, s_prev)
            bg = _dot_nt(b_g, g_next)
            xw = xdt * dteq
            db_g = _dot(xw, g_next)
            t_w = xw * bg
            gs = g_next * s_prev
            g_scr[g] = e_rows_all[cs, :] * g_next + _dot(dye.T, c_g)
            cb = _dot_nt(c_g, b_g)
            cb_t = cb.T
            heads = range(4 * g, 4 * g + 4)
            decs = [jnp.exp(jnp.where(tril, acum[:, h:h + 1] - acum_t[h:h + 1, :], -jnp.inf)) for h in heads]
            mt_stack = jnp.concatenate(
                [(cb_t * jnp.exp(jnp.where(triu, acum_t[h:h + 1, :] - acum[:, h:h + 1], -jnp.inf))).astype(MXU_DTYPE)
                 for h in heads], axis=0)
            dy_stack = jnp.concatenate([jnp.where(lane_q == q, dy, 0.0).astype(MXU_DTYPE)
                                        for q in range(HEADS_PER_GROUP)], axis=0)
            dm_all = _dot_nt(dy_stack, xdt_m)
            dx_all = _dot(mt_stack, dy_m)
            dxdt = bg * dteq + _by_quarter(lane_q, [dx_all[CHUNK * q:CHUNK * (q + 1)] for q in range(HEADS_PER_GROUP)])
            t_dt = dxdt * xs
            t_dk = dy * xs
            dec_stack = jnp.concatenate(decs, axis=0)
            dm_dec = dm_all * dec_stack
            e_all = dm_dec * jnp.concatenate([cb] * HEADS_PER_GROUP, axis=0)
            dcb = functools.reduce(jnp.add, [dm_dec[CHUNK * q:CHUNK * (q + 1)] for q in range(HEADS_PER_GROUP)])
            one = jnp.ones((), MXU_DTYPE)
            sel_lanes = jnp.where(q256 + 4 * g == lane_256, one, 0)
            sel_rows = jnp.where(q512 + 4 * g == lane_512, one, 0)
            sel_rows_t = jnp.where(row_512t == q512t + 4 * g, one, 0)
            e_lanes = jnp.concatenate([e_all[CHUNK * q:CHUNK * (q + 1)] for q in range(HEADS_PER_GROUP)], axis=1)
            w_heads = _dot(t_w, sel_lanes)
            dacum_c = dacum_c + _dot(e_lanes, sel_rows) + _dot(t_off, sel_lanes) - w_heads
            dacum_r = dacum_r + _dot(sel_rows_t, e_all)
            ddt_c = ddt_c + _dot(t_dt, sel_lanes)
            d_aend = d_aend + _colsum(w_heads) + e_end * _colsum(_dot_tn(gs, sel_lanes))
            d_dsk = d_dsk + _colsum(_dot(t_dk, sel_lanes))
            dpre_ref[:, cs] = ((dxdt * dtq + dsk_x[:, cs] * dy) * xs_grad).astype(MXU_DTYPE)
            dc_g = dc_g + _dot(dcb, b_g)
            db_g = db_g + _dot(dcb.T, c_g)
            dpre_ref[:, 1024 + SSM_STATE * g:1024 + SSM_STATE * (g + 1)] = (db_g * b_grad).astype(MXU_DTYPE)
            dpre_ref[:, 1536 + SSM_STATE * g:1536 + SSM_STATE * (g + 1)] = (dc_g * c_grad).astype(MXU_DTYPE)

        dacum = dacum_c - dacum_r.T + jnp.where(row == CHUNK - 1, d_aend, 0.0)
        dda = _dot_exact(triu.astype(F32), dacum)
        ddtr = (dda * a + ddt_c) * _sigmoid_small(dtr_ref[...] + dtb_ref[...])
        ddtr_ref[...] = ddtr.astype(MXU_DTYPE)
        dvec_ref[0:1, :] += _colsum(ddtr)
        dvec_ref[1:2, :] += _colsum(dda * dt)
        dvec_ref[2:3, :] += d_dsk

    return pl.pallas_call(
        body, name=name, grid=(nsteps,),
        out_shape=(jax.ShapeDtypeStruct((rows, CONV_DIM), MXU_DTYPE), jax.ShapeDtypeStruct((rows, IN_MAIN), MXU_DTYPE),
                   jax.ShapeDtypeStruct((rows, LANES), MXU_DTYPE), jax.ShapeDtypeStruct((1, D_MODEL), F32),
                   jax.ShapeDtypeStruct((8, LANES), F32)),
        in_specs=[pl.BlockSpec((tr, CONV_DIM), lambda i: (rev(i), 0)), pl.BlockSpec((tr, LANES), lambda i: (rev(i), 0)),
                  pl.BlockSpec((tr, LANES), lambda i: (rev(i), 0)), pl.BlockSpec((tr, LANES), lambda i: (rev(i), 0)),
                  pl.BlockSpec((tr, D_MODEL), lambda i: (rev(i), 4)), pl.BlockSpec((tr, D_MODEL), lambda i: (rev(i), 0)),
                  pl.BlockSpec((cps, SSM_GROUPS, GROUP_DIM, SSM_STATE), lambda i: (rev(i), 0, 0, 0)),
                  pl.BlockSpec((tr, D_MODEL), lambda i: (rev(i), 1)),
                  _vec_spec(LANES), _vec_spec(LANES), _vec_spec(LANES), _vec_spec(D_MODEL),
                  pl.BlockSpec(memory_space=pl.ANY), _vec_spec(D_MODEL, LANES), _vec_spec(LANES, D_MODEL)],
        out_specs=(pl.BlockSpec((tr, CONV_DIM), lambda i: (rev(i), 0)), pl.BlockSpec((tr, D_MODEL), lambda i: (rev(i), 4)),
                   pl.BlockSpec((tr, LANES), lambda i: (rev(i), 0)), _vec_spec(D_MODEL), _vec_spec(LANES, 8)),
        scratch_shapes=[pltpu.VMEM((SSM_GROUPS, GROUP_DIM, SSM_STATE), F32)],
        input_output_aliases={12: 1},
        compiler_params=_cparams(("arbitrary",)),
    )(pre, dtr, dt_saved, acum_saved, proj, y_saved, states, dcat, dtb, alog, dsk, gn, dproj, *_head_selectors())


def _pool_counts(first_row, n_rows, win):
    t = first_row + lax.broadcasted_iota(jnp.int32, (n_rows, POOL_DIM), 0)
    return jnp.minimum(t + 1, win).astype(F32)


def _pool_fwd(yn, pool_w, pool_b, pool_scale, *, name):
    rows = yn.shape[0]
    tr = ROW_TILE
    hb = tr // POOL_HALO

    def body(y_ref, prev_ref, w_ref, b_ref, s_ref, pm_ref, diff_ref, buf):
        i = pl.program_id(0)
        buf[pl.ds(0, POOL_HALO), :] = jnp.where(i == 0, 0.0, prev_ref[...])
        buf[pl.ds(POOL_HALO, tr), :] = y_ref[...]
        level = buf[...]
        sums = []
        for g, win in enumerate(POOL_WINDOWS):
            level = level + pltpu.roll(level, win // 2, axis=0)
            sums.append(level[POOL_HALO:, :POOL_DIM])
            if g + 1 < len(POOL_WINDOWS):
                level = level[:, POOL_DIM:]
        for g, win in enumerate(POOL_WINDOWS):
            cs = slice(POOL_DIM * g, POOL_DIM * (g + 1))
            diff = (sums[g] / _pool_counts(i * tr, tr, win) - y_ref[:, cs]).astype(MXU_DTYPE)
            diff_ref[:, cs] = diff
            pm_ref[:, cs] = (_dot(diff, w_ref[g]) + b_ref[:, cs]) * s_ref[:, cs]

    return pl.pallas_call(
        body, name=name, grid=(rows // tr,),
        out_shape=(jax.ShapeDtypeStruct((rows, D_MODEL), F32), jax.ShapeDtypeStruct((rows, D_MODEL), MXU_DTYPE)),
        in_specs=[_row_spec(D_MODEL, tr),
                  pl.BlockSpec((POOL_HALO, D_MODEL), lambda i: (jnp.maximum(i * hb - 1, 0), 0)),
                  pl.BlockSpec((4, POOL_DIM, POOL_DIM), lambda i: (0, 0, 0)), _vec_spec(D_MODEL), _vec_spec(D_MODEL)],
        out_specs=(_row_spec(D_MODEL, tr), _row_spec(D_MODEL, tr)),
        scratch_shapes=[pltpu.VMEM((tr + POOL_HALO, D_MODEL), F32)],
        compiler_params=_cparams(("parallel",)),
    )(yn, yn, pool_w, pool_b, pool_scale)


def _pool_bwd(dpm, diff, pool_w, pool_w_t, pool_b, pool_scale, *, name):
    rows = dpm.shape[0]
    tr = ROW_TILE
    hb = tr // POOL_HALO
    nblk = rows // tr

    def body(d_ref, dnext_ref, diff_ref, w_ref, wt_ref, b_ref, s_ref, dy_ref, dw_ref, db_ref, ds_ref, ebuf):
        i = pl.program_id(0)

        @pl.when(i == 0)
        def _():
            dw_ref[...] = jnp.zeros_like(dw_ref)
            db_ref[...] = jnp.zeros_like(db_ref)
            ds_ref[...] = jnp.zeros_like(ds_ref)

        last = i == nblk - 1
        for g, win in enumerate(POOL_WINDOWS):
            cs = slice(POOL_DIM * g, POOL_DIM * (g + 1))
            d = d_ref[:, cs]
            diff = diff_ref[:, cs]
            out_pre = _dot(diff, w_ref[g]) + b_ref[:, cs]
            ds_ref[:, cs] += _colsum(d * out_pre)
            dout = d * s_ref[:, cs]
            db_ref[:, cs] += _colsum(dout)
            dw_ref[g] += _dot_tn(diff, dout)
            ddiff = _dot(dout, wt_ref[g])
            ddiff_next = _dot(jnp.where(last, 0.0, dnext_ref[:, cs]) * s_ref[:, cs], wt_ref[g])
            ebuf[pl.ds(0, tr), cs] = ddiff / _pool_counts(i * tr, tr, win)
            ebuf[pl.ds(tr, POOL_HALO), cs] = ddiff_next / _pool_counts((i + 1) * tr, POOL_HALO, win)
            dy_ref[:, cs] = -ddiff
        level = ebuf[...]
        n = tr + POOL_HALO
        for g, win in enumerate(POOL_WINDOWS):
            level = level + pltpu.roll(level, n - win // 2, axis=0)
            dy_ref[:, POOL_DIM * g:POOL_DIM * (g + 1)] += level[:tr, :POOL_DIM]
            if g + 1 < len(POOL_WINDOWS):
                level = level[:, POOL_DIM:]

    return pl.pallas_call(
        body, name=name, grid=(nblk,),
        out_shape=(jax.ShapeDtypeStruct((rows, D_MODEL), F32), jax.ShapeDtypeStruct((4, POOL_DIM, POOL_DIM), F32),
                   jax.ShapeDtypeStruct((1, D_MODEL), F32), jax.ShapeDtypeStruct((1, D_MODEL), F32)),
        in_specs=[_row_spec(D_MODEL, tr),
                  pl.BlockSpec((POOL_HALO, D_MODEL), lambda i: (jnp.minimum((i + 1) * hb, rows // POOL_HALO - 1), 0)),
                  _row_spec(D_MODEL, tr),
                  pl.BlockSpec((4, POOL_DIM, POOL_DIM), lambda i: (0, 0, 0)),
                  pl.BlockSpec((4, POOL_DIM, POOL_DIM), lambda i: (0, 0, 0)), _vec_spec(D_MODEL), _vec_spec(D_MODEL)],
        out_specs=(_row_spec(D_MODEL, tr), pl.BlockSpec((4, POOL_DIM, POOL_DIM), lambda i: (0, 0, 0)),
                   _vec_spec(D_MODEL), _vec_spec(D_MODEL)),
        scratch_shapes=[pltpu.VMEM((tr + POOL_HALO, D_MODEL), F32)],
        compiler_params=_cparams(("arbitrary",)),
    )(dpm, dpm, diff, pool_w, pool_w_t, pool_b, pool_scale)


def _row_tile(rows, cap, step):
    best = rows
    for t in range(step, min(rows, cap) + 1, step):
        if rows % t == 0:
            best = t
    return best if best <= cap else rows


def _sum8(recv, *, name):
    _, r, c = recv.shape
    step = 8 if recv.dtype == F32 else 16

    def body(r_ref, g_ref):
        g = r_ref[0].astype(F32)
        for j in range(1, N_DEV):
            g = g + r_ref[j].astype(F32)
        g_ref[...] = g

    if r % step == 0:
        tr = _row_tile(r, 256, step)
        grid, in_spec, out_spec = (r // tr,), pl.BlockSpec((N_DEV, tr, c), lambda i: (0, i, 0)), pl.BlockSpec((tr, c), lambda i: (i, 0))
    else:
        tc = 256
        grid, in_spec, out_spec = (c // tc,), pl.BlockSpec((N_DEV, r, tc), lambda i: (0, 0, i)), pl.BlockSpec((r, tc), lambda i: (0, i))
    return pl.pallas_call(
        body, name=name, grid=grid, out_shape=jax.ShapeDtypeStruct((r, c), F32),
        in_specs=[in_spec], out_specs=out_spec, compiler_params=_cparams(("parallel",)),
    )(recv)


def _adamw(g, w, m, v, *, name):
    rows, cols = w.shape
    tr = _row_tile(rows, max(8, (256 * 1024) // cols // 8 * 8), 8)
    c1 = 1.0 / (1.0 - ADAM_B1 ** ADAM_STEP)
    c2 = 1.0 / (1.0 - ADAM_B2 ** ADAM_STEP)

    def body(g_ref, w_ref, m_ref, v_ref, d_ref, mo_ref, vo_ref):
        g = g_ref[...]
        m_new = ADAM_B1 * m_ref[...] + (1.0 - ADAM_B1) * g
        v_new = ADAM_B2 * v_ref[...] + (1.0 - ADAM_B2) * (g * g)
        mo_ref[...] = m_new
        vo_ref[...] = v_new
        d_ref[...] = -ADAM_LR * ((m_new * c1) / (jnp.sqrt(v_new * c2) + ADAM_EPS) + ADAM_WD * w_ref[...])

    spec = pl.BlockSpec((tr, cols), lambda i: (i, 0))
    return pl.pallas_call(
        body, name=name, grid=(rows // tr,),
        out_shape=tuple(jax.ShapeDtypeStruct((rows, cols), F32) for _ in range(3)),
        in_specs=[spec] * 4, out_specs=(spec, spec, spec),
        compiler_params=_cparams(("parallel",)),
    )(g, w, m, v)


def _pad_rows(flat, mult):
    n = flat.shape[-1]
    pad = (-n) % mult
    if pad:
        flat = jnp.pad(flat, [(0, 0)] * (flat.ndim - 1) + [(0, pad)])
    return flat


def _pack_blocks(blocks, row_mult):
    flat = jnp.concatenate([_pad_rows(b.reshape(-1), LANES) for b in blocks])
    return _pad_rows(flat, LANES * row_mult).reshape(-1, LANES)


def _block_sizes(blocks):
    return [-(-math.prod(b.shape) // LANES) * LANES for b in blocks]


def _unpack_blocks(slab, like, lead=()):
    flat = slab.reshape(lead + (-1,))
    out, off = [], 0
    for b, size in zip(like, _block_sizes(like)):
        n = math.prod(b.shape)
        out.append(flat[..., off:off + n].reshape(lead + tuple(b.shape)))
        off += size
    return out


def _join_shards(gathered, axis):
    return jnp.concatenate([gathered[j] for j in range(N_DEV)], axis=axis)


def _split_shards(full, axis):
    return jnp.stack(jnp.split(full, N_DEV, axis=axis))


def _interleave_ff(w_gate, w_up):
    k = w_gate.shape[0]
    nt = D_FF // FF_TILE
    return jnp.stack([w_gate.reshape(k, nt, FF_TILE), w_up.reshape(k, nt, FF_TILE)], axis=2).reshape(k, 2 * D_FF)


def _row128(vec):
    return jnp.pad(vec.reshape(1, -1), ((0, 0), (0, LANES - vec.shape[-1])))


def kernel(x, norm_g, w_in, gm_ln_g, gm_ln_b, gm_ws, gm_bs, conv_w, conv_b, dt_bias, a_log, d_skip, ssm_norm_g, w_out, pool_w, pool_b, pool_scale, ffn_w_gate, ffn_w_up, ffn_w_down, loss_target, m_norm_g, m_w_in, m_gm_ln_g, m_gm_ln_b, m_gm_ws, m_gm_bs, m_conv_w, m_conv_b, m_dt_bias, m_a_log, m_d_skip, m_ssm_norm_g, m_w_out, m_pool_w, m_pool_b, m_pool_scale, m_ffn_w_gate, m_ffn_w_up, m_ffn_w_down, v_norm_g, v_w_in, v_gm_ln_g, v_gm_ln_b, v_gm_ws, v_gm_bs, v_conv_w, v_conv_b, v_dt_bias, v_a_log, v_d_skip, v_ssm_norm_g, v_w_out, v_pool_w, v_pool_b, v_pool_scale, v_ffn_w_gate, v_ffn_w_up, v_ffn_w_down):
    w_loc = dict(norm_g=norm_g, w_in=w_in, gm_ln_g=gm_ln_g, gm_ln_b=gm_ln_b, gm_ws=gm_ws, gm_bs=gm_bs, conv_w=conv_w,
                 conv_b=conv_b, dt_bias=dt_bias, a_log=a_log, d_skip=d_skip, ssm_norm_g=ssm_norm_g, w_out=w_out,
                 pool_w=pool_w, pool_b=pool_b, pool_scale=pool_scale, ffn_w_gate=ffn_w_gate, ffn_w_up=ffn_w_up,
                 ffn_w_down=ffn_w_down)
    m_loc = dict(zip(WEIGHTS, [m_norm_g, m_w_in, m_gm_ln_g, m_gm_ln_b, m_gm_ws, m_gm_bs, m_conv_w, m_conv_b, m_dt_bias,
                               m_a_log, m_d_skip, m_ssm_norm_g, m_w_out, m_pool_w, m_pool_b, m_pool_scale,
                               m_ffn_w_gate, m_ffn_w_up, m_ffn_w_down]))
    v_loc = dict(zip(WEIGHTS, [v_norm_g, v_w_in, v_gm_ln_g, v_gm_ln_b, v_gm_ws, v_gm_bs, v_conv_w, v_conv_b, v_dt_bias,
                               v_a_log, v_d_skip, v_ssm_norm_g, v_w_out, v_pool_w, v_pool_b, v_pool_scale,
                               v_ffn_w_gate, v_ffn_w_up, v_ffn_w_down]))

    small_blocks = [w_loc[n] for n in GATHER_F32]
    got = _gather_two_level([w_in[0].astype(MXU_DTYPE), _pack_blocks(small_blocks, 8)], name="gather_first")
    full = {n: w_loc[n] for n in WEIGHTS if SHARD_AXIS[n] is None}
    full['w_in'] = got[0].transpose(1, 0, 2).reshape(1, D_MODEL, -1)
    for n, g in zip(GATHER_F32, _unpack_blocks(got[1], small_blocks, (N_DEV,))):
        full[n] = _join_shards(g, SHARD_AXIS[n])
    shards = {n: w_loc[n].astype(MXU_DTYPE) for n in ('w_out', 'ffn_w_gate', 'ffn_w_up', 'ffn_w_down', 'pool_w')}

    loss_part, grad_x, grads, recv = _local_step(x[0], loss_target[0], full, shards)

    small = [n for n in WEIGHTS if n not in BIG_WEIGHTS]
    like = [w_loc[n] for n in small]
    slots = []
    for n in small:
        ax = SHARD_AXIS[n]
        g = grads[n].astype(F32)
        sh = _split_shards(g, ax) if ax is not None else jnp.broadcast_to(g[None], (N_DEV,) + g.shape)
        slots.append(_pad_rows(sh.reshape(N_DEV, -1), LANES))
    send_small = _pad_rows(jnp.concatenate(slots, axis=1), LANES * 8).reshape(N_DEV, -1, LANES)
    recv_small, = _exchange([send_small], ['slots'], name="exchange_last")

    g_small = _sum8(recv_small, name="sum_small")
    g_own = dict(zip(small, _unpack_blocks(g_small, like)))
    g_own['w_in'] = _sum8(recv['w_in'], name="sum_w_in").T[None]
    g_own['w_out'] = _sum8(recv['w_out'], name="sum_w_out")[None]
    g_own['ffn_w_gate'] = jnp.stack([_sum8(recv['ffn_w_gate'][l], name=f"sum_ffn{l}_gate").T for l in range(2)])
    g_own['ffn_w_up'] = jnp.stack([_sum8(recv['ffn_w_up'][l], name=f"sum_ffn{l}_up").T for l in range(2)])
    g_own['ffn_w_down'] = jnp.stack([_sum8(recv['ffn_w_down'][l], name=f"sum_ffn{l}_down") for l in range(2)])

    delta, m_new, v_new = {}, {}, {}
    pk = lambda d: _pack_blocks([d[n] for n in small], 8)
    d_s, m_s, v_s = _adamw(g_small, pk(w_loc), pk(m_loc), pk(v_loc), name="adamw_small")
    for dst, slab in ((delta, d_s), (m_new, m_s), (v_new, v_s)):
        dst.update(zip(small, _unpack_blocks(slab, like)))
    for n in BIG_WEIGHTS:
        shape = w_loc[n].shape
        two_d = lambda t: t.reshape(-1, shape[-1])
        res = _adamw(two_d(g_own[n]), two_d(w_loc[n]), two_d(m_loc[n]), two_d(v_loc[n]), name=f"adamw_{n}")
        delta[n], m_new[n], v_new[n] = (t.reshape(shape) for t in res)

    loss = lax.psum(loss_part[0, 0], ("x", "y", "c"))
    outs = [d[n] for d in (g_own, delta, m_new, v_new) for n in WEIGHTS]
    return (loss, grad_x[None], *outs)


def _local_step(h0, tgt, full, shards):
    gm_ln_g, gm_ln_b, gm_ws, gm_bs = full['gm_ln_g'], full['gm_ln_b'], full['gm_ws'], full['gm_bs']
    conv_b, dt_bias, a_log, d_skip, ssm_norm_g = (full['conv_b'], full['dt_bias'], full['a_log'], full['d_skip'],
                                                  full['ssm_norm_g'])
    w_in_f = full['w_in'][0]
    w_main = jnp.concatenate([w_in_f[:, 3072:5120], w_in_f[:, :3072]], axis=1)
    w_dt = jnp.pad(w_in_f[:, 5120:], ((0, 0), (0, LANES - SSM_HEADS)))
    ng = full['norm_g']

    def ffn_shards(layer):
        return [shards['ffn_w_gate'][layer], shards['ffn_w_up'][layer], shards['ffn_w_down'][layer]]

    def ffn_weights(got_gate, got_up, got_down):
        cols = lambda g: g.transpose(1, 0, 2).reshape(D_MODEL, D_FF)
        return _interleave_ff(cols(got_gate), cols(got_up)), got_down.reshape(D_FF, D_MODEL)

    w_gu, w_dn = [None, None], [None, None]
    causal = jnp.tril(jnp.ones((CHUNK, CHUNK), bool))
    wm = jnp.where(causal[None], gm_ws[0], 0.0).astype(MXU_DTYPE)
    wm_t = jnp.swapaxes(wm, 1, 2)
    bcol = jnp.pad(gm_bs[0].T, ((0, 0), (0, LANES - GM_HEADS)))
    conv_w8 = jnp.pad(full['conv_w'][0], ((0, 8 - SSM_CONV), (0, 0)))
    dtb, alog, dsk = _row128(dt_bias[0]), _row128(a_log[0]), _row128(d_skip[0])
    pool_b_f = full['pool_b'][0].reshape(1, D_MODEL)
    pool_s_f = full['pool_scale']

    def g_(layer, i):
        return ng[layer, i].reshape(1, D_MODEL)

    yn0 = _rn_fwd(h0, g_(0, 0), name="rn_fwd_0", out_dtype=MXU_DTYPE)
    proj, got = _mm(yn0, w_main, name="mm_in_proj", tm=2048,
                    ex=_Exchange([shards['w_out'][0]] + ffn_shards(0), ['gather'] * 4))
    w_out_f = got[0].reshape(-1, D_MODEL)
    w_gu[0], w_dn[0] = ffn_weights(*got[1:])
    dtr = _mm(yn0, w_dt, name="mm_in_proj_dt")
    pre = _conv_fwd(proj, conv_w8, conv_b, name="conv_fwd")
    cat = _gmlp_fwd(proj, gm_ln_g, gm_ln_b, wm, bcol, name="gmlp_fwd")
    cat, y_ssd, states, dt_ssd, acum_ssd = _ssd_fwd_grouped(pre, dtr, proj, dtb, alog, dsk, ssm_norm_g, cat,
                                                            name="ssd_fwd")
    o0, h1, yn1 = _mm_resid(cat, w_out_f, h0, g_(0, 1), g_(0, 2), name="mm_out_proj_resid", next_dtype=MXU_DTYPE)
    (gu0, act0), got = _mm_swiglu(yn1, w_gu[0], name="mm_ffn0_gate_up",
                                  ex=_Exchange(ffn_shards(1) + [shards['pool_w'][0]], ['gather'] * 4))
    w_gu[1], w_dn[1] = ffn_weights(*got[:3])
    pool_w_f = got[3].transpose(1, 0, 2, 3).reshape(4, POOL_DIM, POOL_DIM)
    d0, h2, yn2 = _mm_resid(act0, w_dn[0], h1, g_(0, 3), g_(1, 0), name="mm_ffn0_down_resid", next_dtype=F32)
    pm, pdiff = _pool_fwd(yn2, pool_w_f, pool_b_f, pool_s_f, name="pool_fwd")
    h3, yn3 = _resid_rn_fwd(h2, pm, g_(1, 1), g_(1, 2), name="resid_fwd_1a", next_dtype=MXU_DTYPE)
    gu1, act1 = _mm_swiglu(yn3, w_gu[1], name="mm_ffn1_gate_up")
    d1 = _mm(act1, w_dn[1], name="mm_ffn1_down", tm=1024, tn=1024)
    grads = {}
    recv = {'ffn_w_gate': [None, None], 'ffn_w_up': [None, None], 'ffn_w_down': [None, None]}
    dng = [[None] * 4 for _ in range(2)]
    dh4, loss_part, dd1, dng[1][3] = _resid_loss(h3, d1, g_(1, 3), tgt, name="resid_loss")

    def ffn_bwd(layer, dd, gu, act, yn):
        dw_dn = _mm_tn(act, dd, name=f"mm_ffn{layer}_dw_down", out_dtype=MXU_DTYPE, tm=1408, tn=1024)
        dgu = _mm_dswiglu(dd, w_dn[layer].T, gu, name=f"mm_ffn{layer}_dact")
        dw_g_t, dw_u_t = _mm_tn_gate_up(dgu, yn, name=f"mm_ffn{layer}_dw_gate_up", out_dtype=MXU_DTYPE)
        dyn, got = _mm(dgu, w_gu[layer].T, name=f"mm_ffn{layer}_dyn", out_dtype=MXU_DTYPE, tm=512, tn=1024,
                       ex=_Exchange([dw_g_t, dw_u_t, dw_dn], ['rows'] * 3))
        recv['ffn_w_gate'][layer], recv['ffn_w_up'][layer], recv['ffn_w_down'][layer] = got
        return dyn

    dyn3 = ffn_bwd(1, dd1, gu1, act1, yn3)
    dh3, dpm, dng[1][2], dng[1][1] = _resid_bwd_pre_post(dh4, [dyn3], h3, g_(1, 2), pm, g_(1, 1), name="resid_bwd_1b_1a",
                                                         out_dtype=F32)
    dyn2, d_pool_w, d_pool_b, d_pool_s = _pool_bwd(dpm, pdiff, pool_w_f, jnp.swapaxes(pool_w_f, 1, 2), pool_b_f, pool_s_f,
                                                   name="pool_bwd")
    dh2, dd0, dng[1][0], dng[0][3] = _resid_bwd_pre_post(dh3, [dyn2], h2, g_(1, 0), d0, g_(0, 3), name="resid_bwd_1a_0b",
                                                         out_dtype=MXU_DTYPE)
    dyn1 = ffn_bwd(0, dd0, gu0, act0, yn1)
    dh1, do0, dng[0][2], dng[0][1] = _resid_bwd_pre_post(dh2, [dyn1], h1, g_(0, 2), o0, g_(0, 1), name="resid_bwd_0b_0a",
                                                         out_dtype=MXU_DTYPE)
    d_w_out =_mm_tn(cat, do0, name="mm_out_proj_dw", out_dtype=MXU_DTYPE, tn=1024)
    dcat, got = _mm(do0, w_out_f.T, name="mm_out_proj_dx", out_dtype=MXU_DTYPE, tm=2048, tn=1024, ex=_Exchange([d_w_out], ['rows']))
    recv['w_out'] = got[0]
    dproj, d_wm, d_bcol, d_ln_g, d_ln_b = _gmlp_bwd(proj, dcat, gm_ln_g, gm_ln_b, wm, wm_t, bcol, name="gmlp_bwd")
    dpre, dproj, ddtr, d_gn, d_vec = _ssd_bwd_grouped(pre, dtr, dt_ssd, acum_ssd, proj, y_ssd, states, dcat, dtb, alog,
                                                      dsk, ssm_norm_g, dproj, name="ssd_bwd")
    dproj, d_conv_w8, d_conv_b = _conv_bwd(dpre, proj, conv_w8, dproj, name="conv_bwd")
    d_w_main_t = _mm_tn(dproj, yn0, name="mm_in_proj_dw", out_dtype=MXU_DTYPE, tn=1024, shift=3)
    d_w_dt_t = _mm_tn(ddtr, yn0, name="mm_in_proj_dt_dw", out_dtype=MXU_DTYPE, tn=1024)
    d_w_in_t = jnp.concatenate([d_w_main_t, d_w_dt_t[:SSM_HEADS]], axis=0).reshape(N_DEV, -1, D_MODEL)
    dyn0, got = _mm(dproj, w_main.T, name="mm_in_proj_dx", out_dtype=MXU_DTYPE, tm=512, tn=1024,
                    ex=_Exchange([d_w_in_t], ['slots']))
    recv['w_in'] = got[0]
    dyn0_dt = _mm(ddtr, w_dt.T, name="mm_in_proj_dt_dx", out_dtype=MXU_DTYPE)
    grad_x, dng[0][0] = _resid_bwd_pre(dh1, [dyn0, dyn0_dt], h0, g_(0, 0), name="resid_bwd_pre_0a")

    grads['norm_g'] = jnp.stack([jnp.concatenate(dng[l], axis=0) for l in range(2)])
    grads['gm_ln_g'], grads['gm_ln_b'] = d_ln_g, d_ln_b
    grads['gm_ws'] = d_wm[None]
    grads['gm_bs'] = d_bcol[:, :GM_HEADS].T[None]
    grads['conv_w'] = d_conv_w8[None, :SSM_CONV]
    grads['conv_b'] = d_conv_b
    grads['dt_bias'] = d_vec[0:1, :SSM_HEADS]
    grads['a_log'] = d_vec[1:2, :SSM_HEADS] * (-jnp.exp(a_log))
    grads['d_skip'] = d_vec[2:3, :SSM_HEADS]
    grads['ssm_norm_g'] = d_gn
    grads['pool_w'] = d_pool_w[None]
    grads['pool_b'] = d_pool_b.reshape(1, 4, POOL_DIM)
    grads['pool_scale'] = d_pool_s
    return loss_part, grad_x, grads, recv
```
